```python
import jax, jax.numpy as jnp
from jax import lax
import numpy as np

D_MODEL = 1024
BATCH = 4
SEQ = 4096
DEPTH = 1
DEC_BATCH = 8
DEC_SEQ = 16
PAST_LEN = 1024

CHUNK = 64
D_MIX = D_MODEL
D_MLSTM = D_MIX // 2
N_MLSTM_HEADS = 4
MLSTM_HEAD_DIM = D_MLSTM // N_MLSTM_HEADS
D_CONV = D_MIX - D_MLSTM
CONV_WIDTH = 3
N_MEM = 256
N_XHEADS = 4
XHEAD_DIM = D_MODEL // N_XHEADS
N_EXPERTS = 32
TOP_K = 4
D_FF = D_MODEL
SWIGLU_LIMIT = 7.0
SWIGLU_ALPHA = 1.702
MOE_BLOCK = 128
EPS = 1e-5
SPLITS = [int(s) for s in np.cumsum([D_MLSTM, D_MLSTM, D_MLSTM, D_MLSTM,
                                     N_MLSTM_HEADS, N_MLSTM_HEADS, D_CONV, D_CONV])]
IN_COLS = 4 * D_MLSTM + 2 * N_MLSTM_HEADS + 3 * D_CONV

kernel_name = 'hybrid_mlstm_shortconv_memxattn_moe_stream_step'


def rmsnorm(x, g):
    xf = x.astype(jnp.float32)
    y = xf * lax.rsqrt(jnp.mean(xf * xf, axis=-1, keepdims=True) + EPS)
    return (y * g.astype(jnp.float32)).astype(x.dtype)


def mlstm_chunk(state, inp):
    c_prev, n_prev, m_prev = state
    q, k, v, logi, logf = inp
    L = q.shape[2]
    b = jnp.cumsum(logf, axis=-1)
    causal = jnp.tril(jnp.ones((L, L), dtype=bool))
    dmat = b[..., :, None] - b[..., None, :] + logi[..., None, :]
    dmat = jnp.where(causal, dmat, -jnp.inf)
    inter = b + m_prev[..., None]
    m_t = jnp.maximum(inter, jnp.max(dmat, axis=-1))
    w_intra = jnp.exp(dmat - m_t[..., None])
    w_inter = jnp.exp(inter - m_t)
    s = jnp.einsum('bhtd,bhjd->bhtj', q, k) * w_intra
    num = (w_inter[..., None] * jnp.einsum('bhtd,bhde->bhte', q, c_prev)
           + jnp.einsum('bhtj,bhje->bhte', s, v))
    den = w_inter * jnp.einsum('bhtd,bhd->bht', q, n_prev) + jnp.sum(s, axis=-1)
    h = num / jnp.maximum(jnp.abs(den), jnp.exp(-m_t))[..., None]
    m_new = m_t[..., -1]
    decay = jnp.exp(b[..., -1] + m_prev - m_new)
    w_k = jnp.exp(b[..., -1:] - b + logi - m_new[..., None])
    c_new = decay[..., None, None] * c_prev + jnp.einsum('bhj,bhjd,bhje->bhde', w_k, k, v)
    n_new = decay[..., None] * n_prev + jnp.einsum('bhj,bhjd->bhd', w_k, k)
    return (c_new, n_new, m_new), h


def mlstm_scan(q, k, v, logi, logf, state):
    bsz, nh, L, dh = q.shape
    blk = min(CHUNK, L)
    nc = L // blk

    def to_blocks(a):
        return jnp.moveaxis(a.reshape(a.shape[:2] + (nc, blk) + a.shape[3:]), 2, 0)

    xs = (to_blocks(q), to_blocks(k), to_blocks(v), to_blocks(logi), to_blocks(logf))
    state, hs = lax.scan(mlstm_chunk, state, xs)
    h = jnp.moveaxis(hs, 0, 2).reshape(bsz, nh, L, dh)
    return state, h


def token_mixer(xn, conv_buf, c0, n0, m0, w_in, b_gate, mlstm_norm_g, conv_w, w_mix_out):
    bsz, L, _ = xn.shape
    proj = xn @ w_in
    q, k, v, o, gi, gf, cb, cc, ch = jnp.split(proj, SPLITS, axis=-1)

    def heads(a):
        return a.reshape(bsz, L, N_MLSTM_HEADS, MLSTM_HEAD_DIM).transpose(0, 2, 1, 3).astype(jnp.float32)

    qh, kh, vh = heads(q), heads(k) * (MLSTM_HEAD_DIM ** -0.5), heads(v)
    gates = jnp.concatenate([gi, gf], axis=-1).astype(jnp.float32) + b_gate.astype(jnp.float32)
    logi = gates[..., :N_MLSTM_HEADS].transpose(0, 2, 1)
    logf = jax.nn.log_sigmoid(gates[..., N_MLSTM_HEADS:]).transpose(0, 2, 1)
    state0 = (c0.astype(jnp.float32), n0.astype(jnp.float32), m0.astype(jnp.float32))
    (c1, n1, m1), h = mlstm_scan(qh, kh, vh, logi, logf, state0)
    h = h * lax.rsqrt(jnp.mean(h * h, axis=-1, keepdims=True) + EPS)
    h = h.transpose(0, 2, 1, 3).reshape(bsz, L, D_MLSTM) * mlstm_norm_g.astype(jnp.float32)
    y_m = h.astype(xn.dtype) * jax.nn.sigmoid(o)
    u = cc * ch
    upad = jnp.concatenate([conv_buf.astype(u.dtype), u], axis=1)
    yc = sum(conv_w[j] * upad[:, j:j + L] for j in range(CONV_WIDTH))
    y_c = cb * yc
    new_buf = upad[:, upad.shape[1] - (CONV_WIDTH - 1):]
    out = jnp.concatenate([y_m, y_c], axis=-1) @ w_mix_out
    return out, new_buf, c1, n1, m1


def mem_kv(mem, norm_mem_g, w_xk, w_xv):
    bsz, nm, _ = mem.shape
    mn = rmsnorm(mem, norm_mem_g)
    mk = (mn @ w_xk).reshape(bsz, nm, N_XHEADS, XHEAD_DIM)
    mv = (mn @ w_xv).reshape(bsz, nm, N_XHEADS, XHEAD_DIM)
    return mk, mv


def cross_attn(xn, mk, mv, w_xq, w_xo):
    bsz, L, _ = xn.shape
    q = (xn @ w_xq).reshape(bsz, L, N_XHEADS, XHEAD_DIM)
    s = jnp.einsum('bqhd,bkhd->bhqk', q, mk.astype(q.dtype)).astype(jnp.float32) * (XHEAD_DIM ** -0.5)
    p = jax.nn.softmax(s, axis=-1).astype(xn.dtype)
    o = jnp.einsum('bhqk,bkhd->bqhd', p, mv.astype(xn.dtype)).reshape(bsz, L, N_XHEADS * XHEAD_DIM)
    return o @ w_xo


def moe(xn, w_router, b_router, w_up, b_up, w_down, b_down):
    bsz, L, D = xn.shape
    T = bsz * L
    xt = xn.reshape(T, D)
    logits = (xt @ w_router + b_router).astype(jnp.float32)
    top_v, top_e = lax.top_k(logits, TOP_K)
    gates = jax.nn.softmax(top_v, axis=-1)
    A = T * TOP_K
    flat_e = top_e.reshape(A)
    flat_t = jnp.repeat(jnp.arange(T, dtype=jnp.int32), TOP_K)
    flat_g = gates.reshape(A)
    order = jnp.argsort(flat_e)
    e_sorted = flat_e[order]
    counts = jnp.bincount(flat_e, length=N_EXPERTS)
    padded = ((counts + MOE_BLOCK - 1) // MOE_BLOCK) * MOE_BLOCK
    start = jnp.cumsum(counts) - counts
    pend = jnp.cumsum(padded)
    pstart = pend - padded
    dest = pstart[e_sorted] + jnp.arange(A) - start[e_sorted]
    nb = (A + N_EXPERTS * (MOE_BLOCK - 1)) // MOE_BLOCK + 1
    P = nb * MOE_BLOCK
    tok_buf = jnp.zeros((P,), jnp.int32).at[dest].set(flat_t[order])
    gate_buf = jnp.zeros((P,), jnp.float32).at[dest].set(flat_g[order])
    block_e = jnp.minimum(jnp.searchsorted(pend, jnp.arange(nb) * MOE_BLOCK, side='right'), N_EXPERTS - 1)
    xb = xt[tok_buf].reshape(nb, MOE_BLOCK, D)

    def run_block(args):
        xblk, e = args
        hcat = xblk @ w_up[e] + b_up[e]
        glu = jnp.minimum(hcat[:, :D_FF], SWIGLU_LIMIT)
        lin = jnp.clip(hcat[:, D_FF:], -SWIGLU_LIMIT, SWIGLU_LIMIT)
        act = glu * jax.nn.sigmoid(SWIGLU_ALPHA * glu) * (lin + 1)
        return act @ w_down[e] + b_down[e]

    yb = lax.map(run_block, (xb, block_e)).reshape(P, D)
    yb = yb * gate_buf[:, None].astype(yb.dtype)
    out = jax.ops.segment_sum(yb, tok_buf, num_segments=T)
    return out.reshape(bsz, L, D)


def layer(h, mk, mv, conv_buf, c0, n0, m0, norm_mix_g, w_in, b_gate, mlstm_norm_g, conv_w,
          w_mix_out, norm_x_g, w_xq, w_xo, norm_ffn_g, w_router, b_router, w_up, b_up, w_down, b_down):
    mix, new_buf, c1, n1, m1 = token_mixer(rmsnorm(h, norm_mix_g), conv_buf, c0, n0, m0,
                                           w_in, b_gate, mlstm_norm_g, conv_w, w_mix_out)
    h = h + mix
    h = h + cross_attn(rmsnorm(h, norm_x_g), mk, mv, w_xq, w_xo)
    h = h + moe(rmsnorm(h, norm_ffn_g), w_router, b_router, w_up, b_up, w_down, b_down)
    return h, new_buf, c1, n1, m1


def setup_inputs(seed: int = 0) -> dict:
    key = jax.random.key(seed)
    ks = jax.random.split(key, 32)

    def nrm(k, shape, scale):
        return jax.random.normal(k, shape, jnp.float32) * scale

    f_bias = jnp.linspace(3.0, 6.0, N_MLSTM_HEADS, dtype=jnp.float32)[None, :] + nrm(ks[25], (DEPTH, N_MLSTM_HEADS), 0.1)
    b_gate = jnp.concatenate([nrm(ks[24], (DEPTH, N_MLSTM_HEADS), 0.1), f_bias], axis=-1)
    return {
        'x_prompt': nrm(ks[0], (BATCH, SEQ, D_MODEL), 1.0),
        'x_sample': nrm(ks[1], (DEC_BATCH, DEC_SEQ, D_MODEL), 1.0),
        'state_mlstm_c': nrm(ks[2], (DEPTH, DEC_BATCH, N_MLSTM_HEADS, MLSTM_HEAD_DIM, MLSTM_HEAD_DIM), 0.1),
        'state_mlstm_n': nrm(ks[3], (DEPTH, DEC_BATCH, N_MLSTM_HEADS, MLSTM_HEAD_DIM), 0.1),
        'state_mlstm_m': nrm(ks[4], (DEPTH, DEC_BATCH, N_MLSTM_HEADS), 1.0),
        'state_conv': nrm(ks[5], (DEPTH, DEC_BATCH, CONV_WIDTH - 1, D_CONV), 1.0),
        'cache_mem_k': nrm(ks[6], (DEPTH, DEC_BATCH, N_MEM, N_XHEADS, XHEAD_DIM), 1.0),
        'cache_mem_v': nrm(ks[7], (DEPTH, DEC_BATCH, N_MEM, N_XHEADS, XHEAD_DIM), 1.0),
        'mem_prompt': nrm(ks[8], (BATCH, N_MEM, D_MODEL), 1.0),
        'norm_mix_g': 1.0 + nrm(ks[9], (DEPTH, D_MODEL), 0.02),
        'w_in': nrm(ks[10], (DEPTH, D_MODEL, IN_COLS), D_MODEL ** -0.5),
        'b_gate': b_gate,
        'mlstm_norm_g': 1.0 + nrm(ks[11], (DEPTH, D_MLSTM), 0.02),
        'conv_w': nrm(ks[12], (DEPTH, CONV_WIDTH, D_CONV), CONV_WIDTH ** -0.5),
        'w_mix_out': nrm(ks[13], (DEPTH, D_MIX, D_MODEL), D_MIX ** -0.5),
        'norm_x_g': 1.0 + nrm(ks[14], (DEPTH, D_MODEL), 0.02),
        'norm_mem_g': 1.0 + nrm(ks[15], (DEPTH, D_MODEL), 0.02),
        'w_xq': nrm(ks[16], (DEPTH, D_MODEL, N_XHEADS * XHEAD_DIM), D_MODEL ** -0.5),
        'w_xk': nrm(ks[17], (DEPTH, D_MODEL, N_XHEADS * XHEAD_DIM), D_MODEL ** -0.5),
        'w_xv': nrm(ks[18], (DEPTH, D_MODEL, N_XHEADS * XHEAD_DIM), D_MODEL ** -0.5),
        'w_xo': nrm(ks[19], (DEPTH, N_XHEADS * XHEAD_DIM, D_MODEL), D_MODEL ** -0.5),
        'norm_ffn_g': 1.0 + nrm(ks[20], (DEPTH, D_MODEL), 0.02),
        'w_router': nrm(ks[21], (DEPTH, D_MODEL, N_EXPERTS), D_MODEL ** -0.5),
        'b_router': nrm(ks[22], (DEPTH, N_EXPERTS), 0.01),
        'w_up': nrm(ks[23], (DEPTH, N_EXPERTS, D_MODEL, 2 * D_FF), D_MODEL ** -0.5),
        'b_up': nrm(ks[26], (DEPTH, N_EXPERTS, 2 * D_FF), 0.01),
        'w_down': nrm(ks[27], (DEPTH, N_EXPERTS, D_FF, D_MODEL), D_FF ** -0.5),
        'b_down': nrm(ks[28], (DEPTH, N_EXPERTS, D_MODEL), 0.01),
        'norm_final_g': 1.0 + nrm(ks[29], (D_MODEL,), 0.02),
    }


def reference(x_prompt, x_sample, state_mlstm_c, state_mlstm_n, state_mlstm_m, state_conv,
              cache_mem_k, cache_mem_v, mem_prompt, norm_mix_g, w_in, b_gate, mlstm_norm_g, conv_w,
              w_mix_out, norm_x_g, norm_mem_g, w_xq, w_xk, w_xv, w_xo, norm_ffn_g, w_router, b_router,
              w_up, b_up, w_down, b_down, norm_final_g):
    hp, hs = x_prompt, x_sample
    bp = x_prompt.shape[0]
    cp_l, np_l, mp_l, bufp_l, mkp_l, mvp_l = [], [], [], [], [], []
    cs_l, ns_l, ms_l, bufs_l = [], [], [], []
    for l in range(DEPTH):
        lp = (norm_mix_g[l], w_in[l], b_gate[l], mlstm_norm_g[l], conv_w[l], w_mix_out[l],
              norm_x_g[l], w_xq[l], w_xo[l], norm_ffn_g[l], w_router[l], b_router[l],
              w_up[l], b_up[l], w_down[l], b_down[l])
        mk_p, mv_p = mem_kv(mem_prompt, norm_mem_g[l], w_xk[l], w_xv[l])
        buf0 = jnp.zeros((bp, CONV_WIDTH - 1, D_CONV), hp.dtype)
        c0 = jnp.zeros((bp, N_MLSTM_HEADS, MLSTM_HEAD_DIM, MLSTM_HEAD_DIM), jnp.float32)
        n0 = jnp.zeros((bp, N_MLSTM_HEADS, MLSTM_HEAD_DIM), jnp.float32)
        m0 = jnp.zeros((bp, N_MLSTM_HEADS), jnp.float32)
        hp, bufp, cp, npv, mp = layer(hp, mk_p, mv_p, buf0, c0, n0, m0, *lp)
        hs, bufs, cs, ns, ms = layer(hs, cache_mem_k[l], cache_mem_v[l], state_conv[l],
                                     state_mlstm_c[l], state_mlstm_n[l], state_mlstm_m[l], *lp)
        cp_l.append(cp); np_l.append(npv); mp_l.append(mp); bufp_l.append(bufp)
        mkp_l.append(mk_p); mvp_l.append(mv_p)
        cs_l.append(cs); ns_l.append(ns); ms_l.append(ms); bufs_l.append(bufs)
    y_prompt = rmsnorm(hp, norm_final_g)
    y_sample = rmsnorm(hs, norm_final_g)
    return (y_prompt, y_sample,
            jnp.stack(cp_l), jnp.stack(np_l), jnp.stack(mp_l), jnp.stack(bufp_l),
            jnp.stack(mkp_l), jnp.stack(mvp_l),
            jnp.stack(cs_l), jnp.stack(ns_l), jnp.stack(ms_l), jnp.stack(bufs_l))
```

```python
import functools

import jax
import jax.numpy as jnp
import numpy as np
from jax import lax
from jax.experimental import pallas as pl
from jax.experimental.pallas import tpu as pltpu

F32 = jnp.float32
BF16 = jnp.bfloat16
I32 = jnp.int32

D_MODEL = 1024
N_HEADS = 4
HEAD_DIM = 128
D_MLSTM = N_HEADS * HEAD_DIM
D_CONV = D_MODEL - D_MLSTM
CONV_WIDTH = 3
CHUNK = 64
N_MEM = 256
N_XHEADS = 4
XHEAD_DIM = D_MODEL // N_XHEADS
N_EXPERTS = 32
TOP_K = 4
D_FF = D_MODEL
SWIGLU_LIMIT = 7.0
SWIGLU_ALPHA = 1.702
EPS = 1e-5
PACKED = D_MODEL // 2

ROW_BLOCK = 256
VMEM_LIMIT = 56 * 1024 * 1024


def _cparams(sem):
    return pltpu.CompilerParams(dimension_semantics=sem, vmem_limit_bytes=VMEM_LIMIT)


def _rms(x, g):
    return x * lax.rsqrt(jnp.mean(x * x, axis=-1, keepdims=True) + EPS) * g


def _log_sigmoid(x):
    return -(jnp.maximum(-x, 0.0) + jnp.log1p(jnp.exp(-jnp.abs(x))))


def _sigmoid(x):
    return 1.0 / (1.0 + jnp.exp(-x))


def _dot(a, b):
    return jnp.dot(a, b, preferred_element_type=F32)


def _dot_nt(a, b):
    return lax.dot_general(a, b, (((1,), (1,)), ((), ())), preferred_element_type=F32)


def _dot_tn(a, b):
    return lax.dot_general(a, b, (((0,), (0,)), ((), ())), preferred_element_type=F32)


def _memkv_kernel(mem_ref, g_ref, wk_ref, wv_ref, mk_ref, mv_ref, mkb_ref, mvb_ref):
    mn = _rms(mem_ref[...], g_ref[...]).astype(BF16)
    mk = _dot(mn, wk_ref[...])
    mv = _dot(mn, wv_ref[...])
    mk_ref[...] = mk
    mv_ref[...] = mv
    mkb_ref[...] = mk.astype(BF16)
    mvb_ref[...] = mv.astype(BF16)


def _memkv(mem2d, g, wk, wv):
    rows = mem2d.shape[0]
    tm = N_MEM
    row_spec = pl.BlockSpec((tm, D_MODEL), lambda i: (i, 0))
    full = lambda shape: pl.BlockSpec(shape, lambda i: (0,) * len(shape))
    return pl.pallas_call(
        _memkv_kernel,
        grid=(rows // tm,),
        in_specs=[row_spec, full((1, D_MODEL)), full((D_MODEL, D_MODEL)), full((D_MODEL, D_MODEL))],
        out_specs=[row_spec, row_spec, row_spec, row_spec],
        out_shape=[jax.ShapeDtypeStruct((rows, D_MODEL), F32)] * 2
        + [jax.ShapeDtypeStruct((rows, D_MODEL), BF16)] * 2,
        compiler_params=_cparams(("arbitrary",)),
        name="memkv",
    )(mem2d, g, wk, wv)


def _inproj_kernel(x_ref, g_ref, wq_ref, wg_ref, wgt_ref, wc_ref, bg_ref, bgt_ref, cw_ref, cbuf_ref,
                   q_ref, k_ref, v_ref, og_ref, gcol_ref, grow_ref, yc_ref, nbuf_ref,
                   carry_ref, *, tm, chunk):
    j = pl.program_id(1)

    @pl.when(j == 0)
    def _():
        carry_ref[0:2, :] = cbuf_ref[...]

    xb = _rms(x_ref[...], g_ref[...]).astype(BF16)

    p = _dot(xb, wq_ref[...])
    q_ref[...] = p[:, 0:D_MLSTM].astype(BF16)
    k_ref[...] = (p[:, D_MLSTM:2 * D_MLSTM] * (HEAD_DIM ** -0.5)).astype(BF16)
    v_ref[...] = p[:, 2 * D_MLSTM:3 * D_MLSTM].astype(BF16)
    og_ref[...] = _sigmoid(p[:, 3 * D_MLSTM:4 * D_MLSTM])

    gc = _dot(xb, wg_ref[...])[:, 0:2 * N_HEADS] + bg_ref[...]
    col = lax.broadcasted_iota(I32, gc.shape, 1)
    gcol_ref[...] = jnp.where(col < N_HEADS, gc, _log_sigmoid(gc))
    gr = _dot_nt(wgt_ref[...], xb) + bgt_ref[...]
    row = lax.broadcasted_iota(I32, gr.shape, 0)
    gr = jnp.where(row < N_HEADS, gr, _log_sigmoid(gr))
    for c in range(tm // chunk):
        grow_ref[c] = gr[:, c * chunk:(c + 1) * chunk]

    pc = _dot(xb, wc_ref[...])
    cb = pc[:, 0:D_CONV]
    u = pc[:, D_CONV:2 * D_CONV] * pc[:, 2 * D_CONV:3 * D_CONV]
    c0 = carry_ref[0:1, :]
    c1 = carry_ref[1:2, :]
    rid = lax.broadcasted_iota(I32, u.shape, 0)
    um1 = jnp.where(rid == 0, c1, pltpu.roll(u, 1, 0))
    um2 = jnp.where(rid == 0, c0, jnp.where(rid == 1, c1, pltpu.roll(u, 2, 0)))
    yc = cw_ref[0:1, :] * um2 + cw_ref[1:2, :] * um1 + cw_ref[2:3, :] * u
    yc_ref[...] = (cb * yc).astype(BF16)
    tail = u[tm - 2:tm, :]
    carry_ref[0:2, :] = tail
    nbuf_ref[...] = tail


def _inproj(x, g, wq, wg, wgt, wc, bg, bgt, cw, cbuf, *, tm, chunk):
    bsz, seq, _ = x.shape
    grid = (bsz, seq // tm)
    tok = lambda c: pl.BlockSpec((None, tm, c), lambda b, j: (b, j, 0))
    full = lambda shape: pl.BlockSpec(shape, lambda b, j: (0,) * len(shape))
    nck = tm // chunk
    return pl.pallas_call(
        functools.partial(_inproj_kernel, tm=tm, chunk=chunk),
        grid=grid,
        in_specs=[tok(D_MODEL), full((1, D_MODEL)), full((D_MODEL, 4 * D_MLSTM)),
                  full((D_MODEL, 128)), full((8, D_MODEL)), full((D_MODEL, 3 * D_CONV)),
                  full((1, 8)), full((8, 1)), full((CONV_WIDTH, D_CONV)),
                  pl.BlockSpec((None, 2, D_CONV), lambda b, j: (b, 0, 0))],
        out_specs=[tok(D_MLSTM), tok(D_MLSTM), tok(D_MLSTM), tok(D_MLSTM), tok(8),
                   pl.BlockSpec((None, nck, 8, chunk), lambda b, j: (b, j, 0, 0)),
                   tok(D_CONV),
                   pl.BlockSpec((None, 2, D_CONV), lambda b, j: (b, 0, 0))],
        out_shape=[jax.ShapeDtypeStruct((bsz, seq, D_MLSTM), BF16)] * 3
        + [jax.ShapeDtypeStruct((bsz, seq, D_MLSTM), F32),
           jax.ShapeDtypeStruct((bsz, seq, 8), F32),
           jax.ShapeDtypeStruct((bsz, seq // chunk, 8, chunk), F32),
           jax.ShapeDtypeStruct((bsz, seq, D_CONV), BF16),
           jax.ShapeDtypeStruct((bsz, 2, D_CONV), F32)],
        scratch_shapes=[pltpu.VMEM((8, D_CONV), F32)],
        compiler_params=_cparams(("arbitrary", "arbitrary")),
        name="inproj",
    )(x, g, wq, wg, wgt, wc, bg, bgt, cw, cbuf)


def _mlstm_kernel(q_ref, k_ref, v_ref, og_ref, gc_ref, gr_ref, c0_ref, n0_ref, m0_ref, ng_ref,
                  ym_ref, c1_ref, n1_ref, m1_ref, c_s, n_s, m_s, *, chunk, nchunks):
    j = pl.program_id(1)

    @pl.when(j == 0)
    def _():
        c_s[...] = c0_ref[...]
        n_s[...] = n0_ref[...]
        m_s[...] = m0_ref[...]

    ti = lax.broadcasted_iota(I32, (chunk, chunk), 0)
    ji = lax.broadcasted_iota(I32, (chunk, chunk), 1)
    causal = ji <= ti

    def body(ci, carry):
        r0 = pl.multiple_of(ci * chunk, chunk)
        rows = pl.ds(r0, chunk)
        gc = gc_ref[rows, :]
        gr = gr_ref[ci]
        for h in range(N_HEADS):
            cols = slice(h * HEAD_DIM, (h + 1) * HEAD_DIM)
            qh = q_ref[rows, cols]
            kh = k_ref[rows, cols]
            vh = v_ref[rows, cols]
            c_prev = c_s[h]
            n_prev = n_s[h:h + 1, :]
            m_prev = m_s[h:h + 1, :]
            li_c = gc[:, h:h + 1]
            lf_c = gc[:, N_HEADS + h:N_HEADS + h + 1]
            li_r = gr[h:h + 1, :]
            lf_r = gr[N_HEADS + h:N_HEADS + h + 1, :]

            b_c = jnp.sum(jnp.where(causal, lf_r, 0.0), axis=1, keepdims=True)
            b_r = jnp.sum(jnp.where(ti <= ji, lf_c, 0.0), axis=0, keepdims=True)
            dmat = jnp.where(causal, b_c - b_r + li_r, -jnp.inf)
            inter = b_c + m_prev
            m_t = jnp.maximum(inter, jnp.max(dmat, axis=1, keepdims=True))
            w_intra = jnp.exp(dmat - m_t)
            w_inter = jnp.exp(inter - m_t)
            s = _dot_nt(qh, kh) * w_intra
            num = w_inter * _dot(qh, c_prev.astype(BF16)) + _dot(s.astype(BF16), vh)
            den = (w_inter * jnp.sum(qh.astype(F32) * n_prev, axis=1, keepdims=True)
                   + jnp.sum(s, axis=1, keepdims=True))
            hh = num / jnp.maximum(jnp.abs(den), jnp.exp(-m_t))

            m_new = m_t[chunk - 1:chunk, :]
            b_last = b_c[chunk - 1:chunk, :]
            decay = jnp.exp(b_last + m_prev - m_new)
            wk_c = jnp.exp(b_last - b_c + li_c - m_new)
            kw = kh.astype(F32) * wk_c
            c_s[h] = decay * c_prev + _dot_tn(kw.astype(BF16), vh)
            n_s[h:h + 1, :] = decay * n_prev + jnp.sum(kw, axis=0, keepdims=True)
            m_s[h:h + 1, :] = m_new

            hn = hh * lax.rsqrt(jnp.mean(hh * hh, axis=1, keepdims=True) + EPS) * ng_ref[:, cols]
            ym_ref[rows, cols] = (hn * og_ref[rows, cols]).astype(BF16)
        return carry

    lax.fori_loop(0, nchunks, body, 0)

    @pl.when(j == pl.num_programs(1) - 1)
    def _():
        c1_ref[...] = c_s[...]
        n1_ref[...] = n_s[...]
        m1_ref[...] = m_s[...]


def _mlstm(q, k, v, og, gcol, grow, c0, n0, m0, ng, *, ct, chunk):
    bsz, seq, _ = q.shape
    nchunks = ct // chunk
    grid = (bsz, seq // ct)
    tok = lambda c: pl.BlockSpec((None, ct, c), lambda b, j: (b, j, 0))
    st_c = pl.BlockSpec((None, N_HEADS, HEAD_DIM, HEAD_DIM), lambda b, j: (b, 0, 0, 0))
    st_n = pl.BlockSpec((None, N_HEADS, HEAD_DIM), lambda b, j: (b, 0, 0))
    st_m = pl.BlockSpec((None, N_HEADS, 1), lambda b, j: (b, 0, 0))
    return pl.pallas_call(
        functools.partial(_mlstm_kernel, chunk=chunk, nchunks=nchunks),
        grid=grid,
        in_specs=[tok(D_MLSTM), tok(D_MLSTM), tok(D_MLSTM), tok(D_MLSTM), tok(8),
                  pl.BlockSpec((None, nchunks, 8, chunk), lambda b, j: (b, j, 0, 0)),
                  st_c, st_n, st_m,
                  pl.BlockSpec((1, D_MLSTM), lambda b, j: (0, 0))],
        out_specs=[tok(D_MLSTM), st_c, st_n, st_m],
        out_shape=[jax.ShapeDtypeStruct((bsz, seq, D_MLSTM), BF16),
                   jax.ShapeDtypeStruct((bsz, N_HEADS, HEAD_DIM, HEAD_DIM), F32),
                   jax.ShapeDtypeStruct((bsz, N_HEADS, HEAD_DIM), F32),
                   jax.ShapeDtypeStruct((bsz, N_HEADS, 1), F32)],
        scratch_shapes=[pltpu.VMEM((N_HEADS, HEAD_DIM, HEAD_DIM), F32),
                        pltpu.VMEM((N_HEADS, HEAD_DIM), F32),
                        pltpu.VMEM((N_HEADS, 1), F32)],
        compiler_params=_cparams(("arbitrary", "arbitrary")),
        name="mlstm",
    )(q, k, v, og, gcol, grow, c0, n0, m0, ng)


def _post_kernel(ym_ref, yc_ref, x_ref, wmo_ref, gx_ref, wxq_ref, mk_ref, mv_ref, wxo_ref,
                 gf_ref, wrt_ref, br_ref,
                 h2_ref, xp_ref, eid_ref, gate_ref, rank_ref, cnt_ref, cnt_s, *, tm):
    first = jnp.logical_and(pl.program_id(0) == 0, pl.program_id(1) == 0)

    @pl.when(first)
    def _():
        cnt_s[...] = jnp.zeros_like(cnt_s)

    mix = _dot(ym_ref[...], wmo_ref[0:D_MLSTM, :]) + _dot(yc_ref[...], wmo_ref[D_MLSTM:D_MODEL, :])
    h1 = x_ref[...] + mix

    xq = _dot(_rms(h1, gx_ref[...]).astype(BF16), wxq_ref[...]).astype(BF16)
    att = jnp.zeros((tm, D_MODEL), F32)
    for hd in range(N_XHEADS):
        cols = slice(hd * XHEAD_DIM, (hd + 1) * XHEAD_DIM)
        s = _dot_nt(xq[:, cols], mk_ref[:, cols]) * (XHEAD_DIM ** -0.5)
        e = jnp.exp(s - jnp.max(s, axis=-1, keepdims=True))
        p = (e / jnp.sum(e, axis=-1, keepdims=True)).astype(BF16)
        o = _dot(p, mv_ref[:, cols]).astype(BF16)
        att = att + _dot(o, wxo_ref[cols, :])
    h2 = h1 + att
    h2_ref[...] = h2

    xn2 = _rms(h2, gf_ref[...])
    lo = lax.bitcast_convert_type(xn2[:, 0:PACKED].astype(BF16).astype(F32), I32)
    hi = lax.bitcast_convert_type(xn2[:, PACKED:D_MODEL].astype(BF16).astype(F32), I32)
    xp_ref[...] = lax.shift_right_logical(lo, jnp.full_like(lo, 16)) | (hi & jnp.int32(-65536))

    logits = _dot_nt(wrt_ref[...], xn2.astype(BF16)) + br_ref[...]
    eidx = lax.broadcasted_iota(I32, logits.shape, 0).astype(F32)
    work = logits
    vals, ids, hots = [], [], []
    for _ in range(TOP_K):
        mx = jnp.max(work, axis=0, keepdims=True)
        idx = jnp.min(jnp.where(work == mx, eidx, float(N_EXPERTS)), axis=0, keepdims=True)
        sel = eidx == idx
        vals.append(mx)
        ids.append(idx)
        hots.append(sel)
        work = jnp.where(sel, -jnp.inf, work)
    exps = [jnp.exp(v - vals[0]) for v in vals]
    denom = exps[0] + exps[1] + exps[2] + exps[3]

    picked = jnp.zeros(logits.shape, F32)
    for sel in hots:
        picked = picked + sel.astype(F32)
    tj = lax.broadcasted_iota(I32, (tm, tm), 0)
    tt = lax.broadcasted_iota(I32, (tm, tm), 1)
    before = (tj < tt).astype(BF16)
    prior = _dot(picked.astype(BF16), before) + cnt_s[...]
    for kk in range(TOP_K):
        eid_ref[kk:kk + 1, :] = ids[kk].astype(I32)
        gate_ref[kk:kk + 1, :] = exps[kk] / denom
        rank_ref[kk:kk + 1, :] = jnp.sum(jnp.where(hots[kk], prior, 0.0), axis=0,
                                         keepdims=True).astype(I32)
    total = cnt_s[...] + jnp.sum(picked, axis=1, keepdims=True)
    cnt_s[...] = total
    cnt_ref[...] = total.astype(I32)


def _post(ym, yc, x, wmo, gx, wxq, mkb, mvb, wxo, gf, wrt, br, *, tm):
    bsz, seq, _ = x.shape
    grid = (bsz, seq // tm)
    tok = lambda c: pl.BlockSpec((None, tm, c), lambda b, j: (b, j, 0))
    full = lambda shape: pl.BlockSpec(shape, lambda b, j: (0,) * len(shape))
    mem = pl.BlockSpec((None, N_MEM, D_MODEL), lambda b, j: (b, 0, 0))
    sel = pl.BlockSpec((None, TOP_K, tm), lambda b, j: (b, 0, j))
    return pl.pallas_call(
        functools.partial(_post_kernel, tm=tm),
        grid=grid,
        in_specs=[tok(D_MLSTM), tok(D_CONV), tok(D_MODEL), full((D_MODEL, D_MODEL)),
                  full((1, D_MODEL)), full((D_MODEL, D_MODEL)), mem, mem,
                  full((D_MODEL, D_MODEL)), full((1, D_MODEL)), full((N_EXPERTS, D_MODEL)),
                  full((N_EXPERTS, 1))],
        out_specs=[tok(D_MODEL), tok(PACKED), sel, sel, sel, full((N_EXPERTS, 1))],
        out_shape=[jax.ShapeDtypeStruct((bsz, seq, D_MODEL), F32),
                   jax.ShapeDtypeStruct((bsz, seq, PACKED), I32),
                   jax.ShapeDtypeStruct((bsz, TOP_K, seq), I32),
                   jax.ShapeDtypeStruct((bsz, TOP_K, seq), F32),
                   jax.ShapeDtypeStruct((bsz, TOP_K, seq), I32),
                   jax.ShapeDtypeStruct((N_EXPERTS, 1), I32)],
        scratch_shapes=[pltpu.VMEM((N_EXPERTS, 1), F32)],
        compiler_params=_cparams(("arbitrary", "arbitrary")),
        name="post",
    )(ym, yc, x, wmo, gx, wxq, mkb, mvb, wxo, gf, wrt, br)


def _dispatch_kernel(posp_ref, poss_ref, fill_ref, xp_ref, xs_ref, xb_ref, zero_s, sem,
                     *, tmd, n_ptiles, n_sample, n_blocks):
    i = pl.program_id(0)

    def row_copy(src, src_row, dst_row):
        return pltpu.make_async_copy(src.at[pl.ds(src_row, 1), :], xb_ref.at[pl.ds(dst_row, 1), :], sem)

    def scatter(pos_ref, src, base, ntok):
        def issue(t, c):
            for kk in range(TOP_K):
                row_copy(src, base + t, pos_ref[kk * ntok + t]).start()
            return c
        lax.fori_loop(0, ntok, issue, 0)

        def drain(t, c):
            for kk in range(TOP_K):
                row_copy(src, 0, 0).wait()
            return c
        lax.fori_loop(0, ntok, drain, 0)

    @pl.when(i < n_ptiles)
    def _():
        scatter(posp_ref, xp_ref, i * tmd, tmd)

    @pl.when(i == n_ptiles)
    def _():
        scatter(poss_ref, xs_ref, 0, n_sample)
        zero_s[...] = jnp.zeros_like(zero_s)
        for e in range(N_EXPERTS):
            start = fill_ref[e]
            cnt = fill_ref[N_EXPERTS + e]

            def zissue(r, c):
                row_copy(zero_s, 0, start + r).start()
                return c
            lax.fori_loop(0, cnt, zissue, 0)

            def zdrain(r, c):
                row_copy(zero_s, 0, 0).wait()
                return c
            lax.fori_loop(0, cnt, zdrain, 0)
        first_free = fill_ref[2 * N_EXPERTS]

        def blk_copy(b):
            return pltpu.make_async_copy(
                zero_s, xb_ref.at[pl.ds(pl.multiple_of(b * ROW_BLOCK, ROW_BLOCK), ROW_BLOCK), :], sem)

        def bissue(b, c):
            blk_copy(b).start()
            return c
        lax.fori_loop(first_free, n_blocks, bissue, 0)

        def bdrain(b, c):
            blk_copy(0).wait()
            return c
        lax.fori_loop(first_free, n_blocks, bdrain, 0)


def _dispatch(pos_p, pos_s, fill, xp, xs, *, tmd, n_blocks):
    n_ptok = xp.shape[0]
    n_sample = xs.shape[0]
    n_ptiles = n_ptok // tmd
    smem = lambda shape, imap: pl.BlockSpec(shape, imap, memory_space=pltpu.SMEM)
    return pl.pallas_call(
        functools.partial(_dispatch_kernel, tmd=tmd, n_ptiles=n_ptiles, n_sample=n_sample,
                          n_blocks=n_blocks),
        grid=(n_ptiles + 1,),
        in_specs=[smem((TOP_K * tmd,), lambda i: (jnp.minimum(i, n_ptiles - 1),)),
                  smem((TOP_K * n_sample,), lambda i: (0,)),
                  smem((2 * N_EXPERTS + 1,), lambda i: (0,)),
                  pl.BlockSpec(memory_space=pl.ANY),
                  pl.BlockSpec(memory_space=pl.ANY)],
        out_specs=pl.BlockSpec(memory_space=pl.ANY),
        out_shape=jax.ShapeDtypeStruct((n_blocks * ROW_BLOCK, PACKED), I32),
        scratch_shapes=[pltpu.VMEM((ROW_BLOCK, PACKED), I32), pltpu.SemaphoreType.DMA(())],
        compiler_params=_cparams(("arbitrary",)),
        name="dispatch",
    )(pos_p, pos_s, fill, xp, xs)


def _expert_kernel(be_ref, nu_ref, xb_ref, wu_ref, bu_ref, wd_ref, bd_ref, yb_ref, wu_s, wd_s):
    i = pl.program_id(0)
    used = i < nu_ref[0]

    @pl.when(used)
    def _():
        prev = be_ref[jnp.maximum(i - 1, 0)]
        fresh = jnp.logical_or(i == 0, be_ref[i] != prev)

        @pl.when(fresh)
        def _():
            step = 128

            def cast(r, c):
                rows = pl.ds(pl.multiple_of(r * step, step), step)
                wu_s[rows, :] = wu_ref[rows, :].astype(BF16)
                wd_s[rows, :] = wd_ref[rows, :].astype(BF16)
                return c
            lax.fori_loop(0, D_MODEL // step, cast, 0)

        w = xb_ref[...]
        x_lo = lax.bitcast_convert_type(lax.shift_left(w, jnp.full_like(w, 16)), F32).astype(BF16)
        x_hi = lax.bitcast_convert_type(w & jnp.int32(-65536), F32).astype(BF16)
        hcat = (_dot(x_lo, wu_s[0:PACKED, :]) + _dot(x_hi, wu_s[PACKED:D_MODEL, :])
                + bu_ref[...])
        glu = jnp.minimum(hcat[:, 0:D_FF], SWIGLU_LIMIT)
        lin = jnp.clip(hcat[:, D_FF:2 * D_FF], -SWIGLU_LIMIT, SWIGLU_LIMIT)
        act = glu * _sigmoid(SWIGLU_ALPHA * glu) * (lin + 1.0)
        yb_ref[...] = _dot(act.astype(BF16), wd_s[...]) + bd_ref[...]

    @pl.when(jnp.logical_not(used))
    def _():
        yb_ref[...] = jnp.zeros_like(yb_ref)


def _experts(block_e, n_used, xb, w_up, b_up, w_down, b_down, *, n_blocks):
    def x_map(i, be, nu):
        return (jnp.minimum(i, nu[0] - 1), 0)

    def w_map(i, be, nu):
        return (be[i], 0, 0)

    grid_spec = pltpu.PrefetchScalarGridSpec(
        num_scalar_prefetch=2,
        grid=(n_blocks,),
        in_specs=[pl.BlockSpec((ROW_BLOCK, PACKED), x_map),
                  pl.BlockSpec((None, D_MODEL, 2 * D_FF), w_map),
                  pl.BlockSpec((None, 1, 2 * D_FF), w_map),
                  pl.BlockSpec((None, D_FF, D_MODEL), w_map),
                  pl.BlockSpec((None, 1, D_MODEL), w_map)],
        out_specs=pl.BlockSpec((ROW_BLOCK, D_MODEL), lambda i, be, nu: (i, 0)),
        scratch_shapes=[pltpu.VMEM((D_MODEL, 2 * D_FF), BF16), pltpu.VMEM((D_FF, D_MODEL), BF16)],
    )
    return pl.pallas_call(
        _expert_kernel,
        grid_spec=grid_spec,
        out_shape=jax.ShapeDtypeStruct((n_blocks * ROW_BLOCK, D_MODEL), F32),
        compiler_params=_cparams(("arbitrary",)),
        name="experts",
    )(block_e, n_used, xb, w_up, b_up, w_down, b_down)


def _combine_kernel(pos_ref, gate_ref, h2_ref, gfin_ref, yb_ref, y_ref, buf, sem, *, tmc):
    def row_copy(src_row, kk, t):
        return pltpu.make_async_copy(yb_ref.at[pl.ds(src_row, 1), :], buf.at[kk, pl.ds(t, 1), :], sem)

    def issue(t, c):
        for kk in range(TOP_K):
            row_copy(pos_ref[kk * tmc + t], kk, t).start()
        return c
    lax.fori_loop(0, tmc, issue, 0)

    def drain(t, c):
        for kk in range(TOP_K):
            row_copy(0, 0, 0).wait()
        return c
    lax.fori_loop(0, tmc, drain, 0)

    acc = gate_ref[:, 0:1] * buf[0]
    for kk in range(1, TOP_K):
        acc = acc + gate_ref[:, kk:kk + 1] * buf[kk]
    y_ref[...] = _rms(h2_ref[...] + acc, gfin_ref[...])


def _combine(pos, gate_col, h2, gfin, yb, *, tmc):
    ntok = h2.shape[0]
    return pl.pallas_call(
        functools.partial(_combine_kernel, tmc=tmc),
        grid=(ntok // tmc,),
        in_specs=[pl.BlockSpec((TOP_K * tmc,), lambda i: (i,), memory_space=pltpu.SMEM),
                  pl.BlockSpec((tmc, TOP_K), lambda i: (i, 0)),
                  pl.BlockSpec((tmc, D_MODEL), lambda i: (i, 0)),
                  pl.BlockSpec((1, D_MODEL), lambda i: (0, 0)),
                  pl.BlockSpec(memory_space=pl.ANY)],
        out_specs=pl.BlockSpec((tmc, D_MODEL), lambda i: (i, 0)),
        out_shape=jax.ShapeDtypeStruct((ntok, D_MODEL), F32),
        scratch_shapes=[pltpu.VMEM((TOP_K, tmc, D_MODEL), F32), pltpu.SemaphoreType.DMA(())],
        compiler_params=_cparams(("arbitrary",)),
        name="combine",
    )(pos, gate_col, h2, gfin, yb)


def _tile_major(a, tile):
    bsz, kk, seq = a.shape
    return a.reshape(bsz, kk, seq // tile, tile).transpose(0, 2, 1, 3).reshape(-1)


def _path(x, cbuf, c0, n0, m0, mkb, mvb, wts, *, tm_in, ct, chunk, tm_post):
    q, k, v, og, gcol, grow, yc, nbuf = _inproj(
        x, wts["g_mix"], wts["wq"], wts["wg"], wts["wgt"], wts["wc"], wts["bg"], wts["bgt"],
        wts["cw"], cbuf, tm=tm_in, chunk=chunk)
    ym, c1, n1, m1 = _mlstm(q, k, v, og, gcol, grow, c0, n0, m0, wts["ng"], ct=ct, chunk=chunk)
    h2, xp, eid, gate, rank, cnt = _post(
        ym, yc, x, wts["wmo"], wts["g_x"], wts["wxq"], mkb, mvb, wts["wxo"], wts["g_ffn"],
        wts["wrt"], wts["br"], tm=tm_post)
    return dict(h2=h2, xp=xp, eid=eid, gate=gate, rank=rank, cnt=cnt[:, 0],
                c1=c1, n1=n1, m1=m1[..., 0], nbuf=nbuf)


def kernel(x_prompt, x_sample, state_mlstm_c, state_mlstm_n, state_mlstm_m, state_conv, cache_mem_k, cache_mem_v, mem_prompt, norm_mix_g, w_in, b_gate, mlstm_norm_g, conv_w, w_mix_out, norm_x_g, norm_mem_g, w_xq, w_xk, w_xv, w_xo, norm_ffn_g, w_router, b_router, w_up, b_up, w_down, b_down, norm_final_g):
    bp, lp, _ = x_prompt.shape
    bs, ls, _ = x_sample.shape
    l = 0
    row = lambda a: a.reshape(1, -1)

    wi = w_in[l]
    gate_cols = wi[:, 4 * D_MLSTM:4 * D_MLSTM + 2 * N_HEADS]
    wts = dict(
        g_mix=row(norm_mix_g[l]),
        wq=wi[:, 0:4 * D_MLSTM].astype(BF16),
        wg=jnp.pad(gate_cols, ((0, 0), (0, 128 - 2 * N_HEADS))).astype(BF16),
        wgt=gate_cols.T.astype(BF16),
        wc=wi[:, 4 * D_MLSTM + 2 * N_HEADS:].astype(BF16),
        bg=row(b_gate[l]), bgt=b_gate[l].reshape(-1, 1),
        cw=conv_w[l], ng=row(mlstm_norm_g[l]),
        wmo=w_mix_out[l].astype(BF16), g_x=row(norm_x_g[l]), wxq=w_xq[l].astype(BF16),
        wxo=w_xo[l].astype(BF16), g_ffn=row(norm_ffn_g[l]),
        wrt=w_router[l].T.astype(BF16), br=b_router[l].reshape(-1, 1),
    )

    mk, mv, mkb, mvb = _memkv(mem_prompt.reshape(bp * N_MEM, D_MODEL), row(norm_mem_g[l]),
                              w_xk[l].astype(BF16), w_xv[l].astype(BF16))
    zeros = lambda *s: jnp.zeros(s, F32)
    pr = _path(x_prompt, zeros(bp, CONV_WIDTH - 1, D_CONV), zeros(bp, N_HEADS, HEAD_DIM, HEAD_DIM),
               zeros(bp, N_HEADS, HEAD_DIM), zeros(bp, N_HEADS, 1),
               mkb.reshape(bp, N_MEM, D_MODEL), mvb.reshape(bp, N_MEM, D_MODEL), wts,
               tm_in=256, ct=512, chunk=CHUNK, tm_post=256)
    sa = _path(x_sample, state_conv[l], state_mlstm_c[l], state_mlstm_n[l],
               state_mlstm_m[l][..., None],
               cache_mem_k[l].reshape(bs, N_MEM, D_MODEL).astype(BF16),
               cache_mem_v[l].reshape(bs, N_MEM, D_MODEL).astype(BF16), wts,
               tm_in=ls, ct=ls, chunk=min(CHUNK, ls), tm_post=ls)

    n_ptok, n_stok = bp * lp, bs * ls
    n_assign = TOP_K * (n_ptok + n_stok)
    n_blocks = -(-n_assign // ROW_BLOCK) + N_EXPERTS
    tot = pr["cnt"] + sa["cnt"]
    blocks_e = (tot + ROW_BLOCK - 1) // ROW_BLOCK
    padded = blocks_e * ROW_BLOCK
    pend = jnp.cumsum(padded)
    pstart = pend - padded
    bend = jnp.cumsum(blocks_e)
    n_used = bend[-1]
    blk = jnp.arange(n_blocks, dtype=I32)
    block_e = jnp.searchsorted(bend, jnp.minimum(blk, n_used - 1), side="right").astype(I32)
    pos_p = pstart[pr["eid"]] + pr["rank"]
    pos_s = (pstart + pr["cnt"])[sa["eid"]] + sa["rank"]
    fill = jnp.concatenate([pstart + tot, padded - tot, n_used[None]]).astype(I32)

    tmd, tmc = 512, 128
    pos_s_flat = pos_s.transpose(1, 0, 2).reshape(-1).astype(I32)
    xb = _dispatch(_tile_major(pos_p, tmd).astype(I32), pos_s_flat, fill,
                   pr["xp"].reshape(n_ptok, PACKED), sa["xp"].reshape(n_stok, PACKED),
                   tmd=tmd, n_blocks=n_blocks)
    yb = _experts(block_e, n_used[None].astype(I32), xb, w_up[l], b_up[l][:, None, :], w_down[l],
                  b_down[l][:, None, :], n_blocks=n_blocks)

    gfin = row(norm_final_g)
    y_p = _combine(_tile_major(pos_p, tmc).astype(I32),
                   pr["gate"].transpose(0, 2, 1).reshape(n_ptok, TOP_K),
                   pr["h2"].reshape(n_ptok, D_MODEL), gfin, yb, tmc=tmc)
    y_s = _combine(pos_s_flat, sa["gate"].transpose(0, 2, 1).reshape(n_stok, TOP_K),
                   sa["h2"].reshape(n_stok, D_MODEL), gfin, yb, tmc=n_stok)

    lead = lambda a: a[None]
    return (y_p.reshape(bp, lp, D_MODEL), y_s.reshape(bs, ls, D_MODEL),
            lead(pr["c1"]), lead(pr["n1"]), lead(pr["m1"]), lead(pr["nbuf"]),
            lead(mk.reshape(bp, N_MEM, N_XHEADS, XHEAD_DIM)),
            lead(mv.reshape(bp, N_MEM, N_XHEADS, XHEAD_DIM)),
            lead(sa["c1"]), lead(sa["n1"]), lead(sa["m1"]), lead(sa["nbuf"]))
```

```python
import functools

import jax
import jax.numpy as jnp
import numpy as np
from jax import lax
from jax.experimental import pallas as pl
from jax.experimental.pallas import tpu as pltpu

F32 = jnp.float32
BF16 = jnp.bfloat16
I32 = jnp.int32

D_MODEL = 1024
N_HEADS = 4
HEAD_DIM = 128
D_MLSTM = N_HEADS * HEAD_DIM
D_CONV = D_MODEL - D_MLSTM
CONV_WIDTH = 3
CHUNK = 64
N_MEM = 256
N_XHEADS = 4
XHEAD_DIM = D_MODEL // N_XHEADS
N_EXPERTS = 32
TOP_K = 4
D_FF = D_MODEL
SWIGLU_LIMIT = 7.0
SWIGLU_ALPHA = 1.702
EPS = 1e-5

SUBLANES = 8
TABLE_WIDTH = 128
ROW_BLOCK = 256
VMEM_LIMIT = 56 * 1024 * 1024


def _cparams(sem):
    return pltpu.CompilerParams(dimension_semantics=sem, vmem_limit_bytes=VMEM_LIMIT)


def _rms(x, g):
    return x * lax.rsqrt(jnp.mean(x * x, axis=-1, keepdims=True) + EPS) * g


def _log_sigmoid(x):
    return -(jnp.maximum(-x, 0.0) + jnp.log1p(jnp.exp(-jnp.abs(x))))


def _sigmoid(x):
    return 1.0 / (1.0 + jnp.exp(-x))


def _dot(a, b):
    return jnp.dot(a, b, preferred_element_type=F32)


def _dot_nt(a, b):
    return lax.dot_general(a, b, (((1,), (1,)), ((), ())), preferred_element_type=F32)


def _dot_tn(a, b):
    return lax.dot_general(a, b, (((0,), (0,)), ((), ())), preferred_element_type=F32)


def _memkv_kernel(mem_ref, g_ref, wk_ref, wv_ref, mk_ref, mv_ref, mkb_ref, mvb_ref):
    mn = _rms(mem_ref[...], g_ref[...]).astype(BF16)
    mk = _dot(mn, wk_ref[...])
    mv = _dot(mn, wv_ref[...])
    mk_ref[...] = mk
    mv_ref[...] = mv
    mkb_ref[...] = mk.astype(BF16)
    mvb_ref[...] = mv.astype(BF16)


def _memkv(mem2d, g, wk, wv):
    rows = mem2d.shape[0]
    tm = N_MEM
    row_spec = pl.BlockSpec((tm, D_MODEL), lambda i: (i, 0))
    full = lambda shape: pl.BlockSpec(shape, lambda i: (0,) * len(shape))
    return pl.pallas_call(
        _memkv_kernel,
        grid=(rows // tm,),
        in_specs=[row_spec, full((1, D_MODEL)), full((D_MODEL, D_MODEL)), full((D_MODEL, D_MODEL))],
        out_specs=[row_spec, row_spec, row_spec, row_spec],
        out_shape=[jax.ShapeDtypeStruct((rows, D_MODEL), F32)] * 2
        + [jax.ShapeDtypeStruct((rows, D_MODEL), BF16)] * 2,
        compiler_params=_cparams(("arbitrary",)),
        name="memkv",
    )(mem2d, g, wk, wv)


def _inproj_kernel(x_ref, g_ref, wq_ref, wg_ref, wgt_ref, wc_ref, bg_ref, bgt_ref, cw_ref, cbuf_ref,
                   q_ref, k_ref, v_ref, og_ref, gcol_ref, grow_ref, yc_ref, nbuf_ref,
                   carry_ref, *, tm, chunk):
    j = pl.program_id(1)

    @pl.when(j == 0)
    def _():
        carry_ref[0:2, :] = cbuf_ref[...]

    xb = _rms(x_ref[...], g_ref[...]).astype(BF16)

    p = _dot(xb, wq_ref[...])
    q_ref[...] = p[:, 0:D_MLSTM].astype(BF16)
    k_ref[...] = (p[:, D_MLSTM:2 * D_MLSTM] * (HEAD_DIM ** -0.5)).astype(BF16)
    v_ref[...] = p[:, 2 * D_MLSTM:3 * D_MLSTM].astype(BF16)
    og_ref[...] = _sigmoid(p[:, 3 * D_MLSTM:4 * D_MLSTM])

    gc = _dot(xb, wg_ref[...])[:, 0:2 * N_HEADS] + bg_ref[...]
    col = lax.broadcasted_iota(I32, gc.shape, 1)
    gcol_ref[...] = jnp.where(col < N_HEADS, gc, _log_sigmoid(gc))
    gr = _dot_nt(wgt_ref[...], xb) + bgt_ref[...]
    row = lax.broadcasted_iota(I32, gr.shape, 0)
    gr = jnp.where(row < N_HEADS, gr, _log_sigmoid(gr))
    for c in range(tm // chunk):
        grow_ref[c] = gr[:, c * chunk:(c + 1) * chunk]

    pc = _dot(xb, wc_ref[...])
    cb = pc[:, 0:D_CONV]
    u = pc[:, D_CONV:2 * D_CONV] * pc[:, 2 * D_CONV:3 * D_CONV]
    c0 = carry_ref[0:1, :]
    c1 = carry_ref[1:2, :]
    rid = lax.broadcasted_iota(I32, u.shape, 0)
    um1 = jnp.where(rid == 0, c1, pltpu.roll(u, 1, 0))
    um2 = jnp.where(rid == 0, c0, jnp.where(rid == 1, c1, pltpu.roll(u, 2, 0)))
    yc = cw_ref[0:1, :] * um2 + cw_ref[1:2, :] * um1 + cw_ref[2:3, :] * u
    yc_ref[...] = (cb * yc).astype(BF16)
    tail = u[tm - 2:tm, :]
    carry_ref[0:2, :] = tail
    nbuf_ref[...] = tail


def _inproj(x, g, wq, wg, wgt, wc, bg, bgt, cw, cbuf, *, tm, chunk):
    bsz, seq, _ = x.shape
    grid = (bsz, seq // tm)
    tok = lambda c: pl.BlockSpec((None, tm, c), lambda b, j: (b, j, 0))
    full = lambda shape: pl.BlockSpec(shape, lambda b, j: (0,) * len(shape))
    nck = tm // chunk
    return pl.pallas_call(
        functools.partial(_inproj_kernel, tm=tm, chunk=chunk),
        grid=grid,
        in_specs=[tok(D_MODEL), full((1, D_MODEL)), full((D_MODEL, 4 * D_MLSTM)),
                  full((D_MODEL, 128)), full((8, D_MODEL)), full((D_MODEL, 3 * D_CONV)),
                  full((1, 8)), full((8, 1)), full((CONV_WIDTH, D_CONV)),
                  pl.BlockSpec((None, 2, D_CONV), lambda b, j: (b, 0, 0))],
        out_specs=[tok(D_MLSTM), tok(D_MLSTM), tok(D_MLSTM), tok(D_MLSTM), tok(8),
                   pl.BlockSpec((None, nck, 8, chunk), lambda b, j: (b, j, 0, 0)),
                   tok(D_CONV),
                   pl.BlockSpec((None, 2, D_CONV), lambda b, j: (b, 0, 0))],
        out_shape=[jax.ShapeDtypeStruct((bsz, seq, D_MLSTM), BF16)] * 3
        + [jax.ShapeDtypeStruct((bsz, seq, D_MLSTM), F32),
           jax.ShapeDtypeStruct((bsz, seq, 8), F32),
           jax.ShapeDtypeStruct((bsz, seq // chunk, 8, chunk), F32),
           jax.ShapeDtypeStruct((bsz, seq, D_CONV), BF16),
           jax.ShapeDtypeStruct((bsz, 2, D_CONV), F32)],
        scratch_shapes=[pltpu.VMEM((8, D_CONV), F32)],
        compiler_params=_cparams(("arbitrary", "arbitrary")),
        name="inproj",
    )(x, g, wq, wg, wgt, wc, bg, bgt, cw, cbuf)


def _mlstm_kernel(q_ref, k_ref, v_ref, og_ref, gc_ref, gr_ref, c0_ref, n0_ref, m0_ref, ng_ref,
                  ym_ref, c1_ref, n1_ref, m1_ref, c_s, n_s, m_s, *, chunk, nchunks, bsz):
    j = pl.program_id(0)

    @pl.when(j == 0)
    def _():
        c_s[...] = c0_ref[...]
        n_s[...] = n0_ref[...]
        m_s[...] = m0_ref[...]

    ti = lax.broadcasted_iota(I32, (chunk, chunk), 0)
    ji = lax.broadcasted_iota(I32, (chunk, chunk), 1)
    causal = ji <= ti

    def body(ci, carry):
        r0 = pl.multiple_of(ci * chunk, chunk)
        rows = pl.ds(r0, chunk)
        chains = [(b, h) for b in range(bsz) for h in range(N_HEADS)]
        cols = lambda h: slice(h * HEAD_DIM, (h + 1) * HEAD_DIM)
        each = lambda f: [f(n, b, h) for n, (b, h) in enumerate(chains)]
        q = lambda b, h: q_ref[b, rows, cols(h)]
        k = lambda b, h: k_ref[b, rows, cols(h)]
        v = lambda b, h: v_ref[b, rows, cols(h)]
        gcs = [gc_ref[b, rows, :] for b in range(bsz)]
        grs = [gr_ref[b, ci] for b in range(bsz)]
        li_c = each(lambda n, b, h: gcs[b][:, h:h + 1])
        lf_c = each(lambda n, b, h: gcs[b][:, N_HEADS + h:N_HEADS + h + 1])
        li_r = each(lambda n, b, h: grs[b][h:h + 1, :])
        lf_r = each(lambda n, b, h: grs[b][N_HEADS + h:N_HEADS + h + 1, :])
        m_prev = each(lambda n, b, h: m_s[b, h:h + 1, :])

        b_c = each(lambda n, b, h: jnp.sum(jnp.where(causal, lf_r[n], 0.0), axis=1, keepdims=True))
        b_r = each(lambda n, b, h: jnp.sum(jnp.where(ti <= ji, lf_c[n], 0.0), axis=0, keepdims=True))
        dmat = each(lambda n, b, h: jnp.where(causal, b_c[n] - b_r[n] + li_r[n], -jnp.inf))
        dmax = each(lambda n, b, h: jnp.max(dmat[n], axis=1, keepdims=True))
        inter = each(lambda n, b, h: b_c[n] + m_prev[n])
        m_t = each(lambda n, b, h: jnp.maximum(inter[n], dmax[n]))
        w_inter = each(lambda n, b, h: jnp.exp(inter[n] - m_t[n]))
        s = each(lambda n, b, h: _dot_nt(q(b, h), k(b, h)) * jnp.exp(dmat[n] - m_t[n]))
        qc = each(lambda n, b, h: _dot(q(b, h), c_s[b, h].astype(BF16)))
        sv = each(lambda n, b, h: _dot(s[n].astype(BF16), v(b, h)))
        qn = each(lambda n, b, h: jnp.sum(q(b, h).astype(F32) * n_s[b, h:h + 1, :], axis=1,
                                          keepdims=True))
        den = each(lambda n, b, h: w_inter[n] * qn[n] + jnp.sum(s[n], axis=1, keepdims=True))
        hh = each(lambda n, b, h: (w_inter[n] * qc[n] + sv[n])
                  / jnp.maximum(jnp.abs(den[n]), jnp.exp(-m_t[n])))

        m_new = each(lambda n, b, h: m_t[n][chunk - 1:chunk, :])
        b_last = each(lambda n, b, h: b_c[n][chunk - 1:chunk, :])
        decay = each(lambda n, b, h: jnp.exp(b_last[n] + m_prev[n] - m_new[n]))
        kw = each(lambda n, b, h: k(b, h).astype(F32)
                  * jnp.exp(b_last[n] - b_c[n] + li_c[n] - m_new[n]))
        kv = each(lambda n, b, h: _dot_tn(kw[n].astype(BF16), v(b, h)))
        for n, (b, h) in enumerate(chains):
            c_s[b, h] = decay[n] * c_s[b, h] + kv[n]
            n_s[b, h:h + 1, :] = decay[n] * n_s[b, h:h + 1, :] + jnp.sum(kw[n], axis=0, keepdims=True)
            m_s[b, h:h + 1, :] = m_new[n]

        hn = each(lambda n, b, h: hh[n] * lax.rsqrt(jnp.mean(hh[n] * hh[n], axis=1, keepdims=True) + EPS)
                  * ng_ref[:, cols(h)])
        for n, (b, h) in enumerate(chains):
            ym_ref[b, rows, cols(h)] = (hn[n] * og_ref[b, rows, cols(h)]).astype(BF16)
        return carry

    lax.fori_loop(0, nchunks, body, 0)

    @pl.when(j == pl.num_programs(0) - 1)
    def _():
        c1_ref[...] = c_s[...]
        n1_ref[...] = n_s[...]
        m1_ref[...] = m_s[...]


def _mlstm(q, k, v, og, gcol, grow, c0, n0, m0, ng, *, ct, chunk):
    bsz, seq, _ = q.shape
    nchunks = ct // chunk
    grid = (seq // ct,)
    tok = lambda c: pl.BlockSpec((bsz, ct, c), lambda j: (0, j, 0))
    st_c = pl.BlockSpec((bsz, N_HEADS, HEAD_DIM, HEAD_DIM), lambda j: (0, 0, 0, 0))
    st_n = pl.BlockSpec((bsz, N_HEADS, HEAD_DIM), lambda j: (0, 0, 0))
    st_m = pl.BlockSpec((bsz, N_HEADS, 1), lambda j: (0, 0, 0))
    return pl.pallas_call(
        functools.partial(_mlstm_kernel, chunk=chunk, nchunks=nchunks, bsz=bsz),
        grid=grid,
        in_specs=[tok(D_MLSTM), tok(D_MLSTM), tok(D_MLSTM), tok(D_MLSTM), tok(8),
                  pl.BlockSpec((bsz, nchunks, 8, chunk), lambda j: (0, j, 0, 0)),
                  st_c, st_n, st_m,
                  pl.BlockSpec((1, D_MLSTM), lambda j: (0, 0))],
        out_specs=[tok(D_MLSTM), st_c, st_n, st_m],
        out_shape=[jax.ShapeDtypeStruct((bsz, seq, D_MLSTM), BF16),
                   jax.ShapeDtypeStruct((bsz, N_HEADS, HEAD_DIM, HEAD_DIM), F32),
                   jax.ShapeDtypeStruct((bsz, N_HEADS, HEAD_DIM), F32),
                   jax.ShapeDtypeStruct((bsz, N_HEADS, 1), F32)],
        scratch_shapes=[pltpu.VMEM((bsz, N_HEADS, HEAD_DIM, HEAD_DIM), F32),
                        pltpu.VMEM((bsz, N_HEADS, HEAD_DIM), F32),
                        pltpu.VMEM((bsz, N_HEADS, 1), F32)],
        compiler_params=_cparams(("arbitrary",)),
        name="mlstm",
    )(q, k, v, og, gcol, grow, c0, n0, m0, ng)


def _post_kernel(ym_ref, yc_ref, x_ref, wmo_ref, gx_ref, wxq_ref, mk_ref, mv_ref, wxo_ref,
                 gf_ref, wrt_ref, br_ref,
                 h2_ref, xn_ref, eid_ref, gate_ref, rank_ref, cnt_ref, cnt_s, *, tm, group):
    step = pl.program_id(0) * pl.num_programs(1) + pl.program_id(1)

    @pl.when(step % group == 0)
    def _():
        cnt_s[...] = jnp.zeros_like(cnt_s)

    mix = _dot(ym_ref[...], wmo_ref[0:D_MLSTM, :]) + _dot(yc_ref[...], wmo_ref[D_MLSTM:D_MODEL, :])
    h1 = x_ref[...] + mix

    xq = _dot(_rms(h1, gx_ref[...]).astype(BF16), wxq_ref[...]).astype(BF16)
    att = jnp.zeros((tm, D_MODEL), F32)
    for hd in range(N_XHEADS):
        cols = slice(hd * XHEAD_DIM, (hd + 1) * XHEAD_DIM)
        s = _dot_nt(xq[:, cols], mk_ref[:, cols]) * (XHEAD_DIM ** -0.5)
        e = jnp.exp(s - jnp.max(s, axis=-1, keepdims=True))
        p = (e / jnp.sum(e, axis=-1, keepdims=True)).astype(BF16)
        o = _dot(p, mv_ref[:, cols]).astype(BF16)
        att = att + _dot(o, wxo_ref[cols, :])
    h2 = h1 + att
    h2_ref[...] = h2

    xn2 = _rms(h2, gf_ref[...]).astype(BF16)
    xn_ref[...] = xn2

    logits = _dot_nt(wrt_ref[...], xn2) + br_ref[...]
    eidx = lax.broadcasted_iota(I32, logits.shape, 0).astype(F32)
    work = logits
    vals, ids, hots = [], [], []
    for _ in range(TOP_K):
        mx = jnp.max(work, axis=0, keepdims=True)
        idx = jnp.min(jnp.where(work == mx, eidx, float(N_EXPERTS)), axis=0, keepdims=True)
        sel = eidx == idx
        vals.append(mx)
        ids.append(idx)
        hots.append(sel)
        work = jnp.where(sel, -jnp.inf, work)
    exps = [jnp.exp(v - vals[0]) for v in vals]
    denom = exps[0] + exps[1] + exps[2] + exps[3]

    picked = jnp.zeros(logits.shape, F32)
    for sel in hots:
        picked = picked + sel.astype(F32)
    tj = lax.broadcasted_iota(I32, (tm, tm), 0)
    tt = lax.broadcasted_iota(I32, (tm, tm), 1)
    before = (tj < tt).astype(BF16)
    prior = _dot(picked.astype(BF16), before) + cnt_s[...]
    for kk in range(TOP_K):
        eid_ref[kk:kk + 1, :] = ids[kk].astype(I32)
        gate_ref[kk:kk + 1, :] = exps[kk] / denom
        rank_ref[kk:kk + 1, :] = jnp.sum(jnp.where(hots[kk], prior, 0.0), axis=0,
                                         keepdims=True).astype(I32)
    total = cnt_s[...] + jnp.sum(picked, axis=1, keepdims=True)
    cnt_s[...] = total
    cnt_ref[...] = total.astype(I32)


def _post(ym, yc, x, wmo, gx, wxq, mkb, mvb, wxo, gf, wrt, br, *, tm, group):
    bsz, seq, _ = x.shape
    nj = seq // tm
    grid = (bsz, nj)
    n_tiles = bsz * nj // group
    tok = lambda c: pl.BlockSpec((None, tm, c), lambda b, j: (b, j, 0))
    full = lambda shape: pl.BlockSpec(shape, lambda b, j: (0,) * len(shape))
    mem = pl.BlockSpec((None, N_MEM, D_MODEL), lambda b, j: (b, 0, 0))
    sel = pl.BlockSpec((None, TOP_K, tm), lambda b, j: (b, 0, j))
    return pl.pallas_call(
        functools.partial(_post_kernel, tm=tm, group=group),
        grid=grid,
        in_specs=[tok(D_MLSTM), tok(D_CONV), tok(D_MODEL), full((D_MODEL, D_MODEL)),
                  full((1, D_MODEL)), full((D_MODEL, D_MODEL)), mem, mem,
                  full((D_MODEL, D_MODEL)), full((1, D_MODEL)), full((N_EXPERTS, D_MODEL)),
                  full((N_EXPERTS, 1))],
        out_specs=[tok(D_MODEL), tok(D_MODEL), sel, sel, sel,
                   pl.BlockSpec((None, N_EXPERTS, 1), lambda b, j: ((b * nj + j) // group, 0, 0))],
        out_shape=[jax.ShapeDtypeStruct((bsz, seq, D_MODEL), F32),
                   jax.ShapeDtypeStruct((bsz, seq, D_MODEL), BF16),
                   jax.ShapeDtypeStruct((bsz, TOP_K, seq), I32),
                   jax.ShapeDtypeStruct((bsz, TOP_K, seq), F32),
                   jax.ShapeDtypeStruct((bsz, TOP_K, seq), I32),
                   jax.ShapeDtypeStruct((n_tiles, N_EXPERTS, 1), I32)],
        scratch_shapes=[pltpu.VMEM((N_EXPERTS, 1), F32)],
        compiler_params=_cparams(("arbitrary", "arbitrary")),
        name="post",
    )(ym, yc, x, wmo, gx, wxq, mkb, mvb, wxo, gf, wrt, br)


def _sorted_rows(n_tokens):
    return -(-(TOP_K * n_tokens + N_EXPERTS * (SUBLANES - 1)) // ROW_BLOCK) * ROW_BLOCK


def _dispatch_kernel(tab_ref, fill_ref, eid_ref, rank_ref, ls_ref, x_ref, eids_ref, ranks_ref, xs_ref,
                     xb_ref, srt, zero_s, sem, *, n_ptiles, n_blocks):
    i = pl.program_id(0)

    def granule(src, src_row, dst_row):
        return pltpu.make_async_copy(
            src.at[pl.ds(pl.multiple_of(src_row, SUBLANES), SUBLANES), :],
            xb_ref.at[pl.ds(pl.multiple_of(dst_row, SUBLANES), SUBLANES), :], sem)

    def drain(count):
        def body(g, c):
            granule(zero_s, 0, 0).wait()
            return c
        lax.fori_loop(0, count, body, 0)

    def sort_and_move(eid, rank, x):
        ntok = x.shape[0]
        nrows = _sorted_rows(ntok)
        e_iota = lax.broadcasted_iota(I32, (N_EXPERTS, ntok), 0)
        s_iota = lax.broadcasted_iota(I32, (nrows, ntok), 0)
        seg_start = ls_ref[...]
        hit = None
        for kk in range(TOP_K):
            start = jnp.sum(jnp.where(e_iota == eid[kk:kk + 1, :], seg_start, 0.0),
                            axis=0, keepdims=True).astype(I32)
            match = s_iota == start + rank[kk:kk + 1, :]
            hit = match if hit is None else jnp.logical_or(hit, match)
        perm = jnp.where(hit, 1.0, 0.0).astype(BF16)
        srt[0:nrows, :] = _dot(perm, x)
        for e in range(N_EXPERTS):
            dst = tab_ref[e]
            src = tab_ref[2 * N_EXPERTS + e]

            def issue(g, c):
                granule(srt, src + g * SUBLANES, dst + g * SUBLANES).start()
                return c
            lax.fori_loop(0, tab_ref[N_EXPERTS + e], issue, 0)
        drain(tab_ref[3 * N_EXPERTS])

    @pl.when(i < n_ptiles)
    def _():
        sort_and_move(eid_ref[...], rank_ref[...], x_ref[...])

    @pl.when(i == n_ptiles)
    def _():
        sort_and_move(eids_ref[...], ranks_ref[...], xs_ref[...])
        zero_s[...] = jnp.zeros_like(zero_s)
        for e in range(N_EXPERTS):
            dst = fill_ref[e]

            def zissue(g, c):
                granule(zero_s, 0, dst + g * SUBLANES).start()
                return c
            lax.fori_loop(0, fill_ref[N_EXPERTS + e], zissue, 0)
        drain(fill_ref[2 * N_EXPERTS])
        first_free = fill_ref[2 * N_EXPERTS + 1]

        def blk_copy(b):
            return pltpu.make_async_copy(
                zero_s, xb_ref.at[pl.ds(pl.multiple_of(b * ROW_BLOCK, ROW_BLOCK), ROW_BLOCK), :], sem)

        def bissue(b, c):
            blk_copy(b).start()
            return c
        lax.fori_loop(first_free, n_blocks, bissue, 0)

        def bdrain(b, c):
            blk_copy(0).wait()
            return c
        lax.fori_loop(first_free, n_blocks, bdrain, 0)


def _dispatch(tab, fill, eid_p, rank_p, seg_start, xn_p, eid_s, rank_s, xn_s, *, tmd, n_blocks):
    n_ptiles = eid_p.shape[0]
    n_sample = xn_s.shape[0]
    last = n_ptiles - 1
    smem = lambda shape, imap: pl.BlockSpec(shape, imap, memory_space=pltpu.SMEM)
    return pl.pallas_call(
        functools.partial(_dispatch_kernel, n_ptiles=n_ptiles, n_blocks=n_blocks),
        grid=(n_ptiles + 1,),
        in_specs=[smem((TABLE_WIDTH,), lambda i: (i,)),
                  smem((TABLE_WIDTH,), lambda i: (0,)),
                  pl.BlockSpec((None, TOP_K, tmd), lambda i: (jnp.minimum(i, last), 0, 0)),
                  pl.BlockSpec((None, TOP_K, tmd), lambda i: (jnp.minimum(i, last), 0, 0)),
                  pl.BlockSpec((None, N_EXPERTS, 1), lambda i: (i, 0, 0)),
                  pl.BlockSpec((tmd, D_MODEL), lambda i: (jnp.minimum(i, last), 0)),
                  pl.BlockSpec((TOP_K, n_sample), lambda i: (0, 0)),
                  pl.BlockSpec((TOP_K, n_sample), lambda i: (0, 0)),
                  pl.BlockSpec((n_sample, D_MODEL), lambda i: (0, 0))],
        out_specs=pl.BlockSpec(memory_space=pl.ANY),
        out_shape=jax.ShapeDtypeStruct((n_blocks * ROW_BLOCK, D_MODEL), F32),
        scratch_shapes=[pltpu.VMEM((_sorted_rows(tmd), D_MODEL), F32),
                        pltpu.VMEM((ROW_BLOCK, D_MODEL), F32), pltpu.SemaphoreType.DMA(())],
        compiler_params=_cparams(("arbitrary",)),
        name="dispatch",
    )(tab, fill, eid_p, rank_p, seg_start, xn_p, eid_s, rank_s, xn_s)


def _expert_kernel(be_ref, nu_ref, xb_ref, wu_ref, bu_ref, wd_ref, bd_ref, yb_ref, wu_s, wd_s):
    i = pl.program_id(0)
    used = i < nu_ref[0]

    @pl.when(used)
    def _():
        prev = be_ref[jnp.maximum(i - 1, 0)]
        fresh = jnp.logical_or(i == 0, be_ref[i] != prev)

        @pl.when(fresh)
        def _():
            step = 128

            def cast(r, c):
                rows = pl.ds(pl.multiple_of(r * step, step), step)
                wu_s[rows, :] = wu_ref[rows, :].astype(BF16)
                wd_s[rows, :] = wd_ref[rows, :].astype(BF16)
                return c
            lax.fori_loop(0, D_MODEL // step, cast, 0)

        hcat = _dot(xb_ref[...].astype(BF16), wu_s[...]) + bu_ref[...]
        glu = jnp.minimum(hcat[:, 0:D_FF], SWIGLU_LIMIT)
        lin = jnp.clip(hcat[:, D_FF:2 * D_FF], -SWIGLU_LIMIT, SWIGLU_LIMIT)
        act = glu * _sigmoid(SWIGLU_ALPHA * glu) * (lin + 1.0)
        yb_ref[...] = _dot(act.astype(BF16), wd_s[...]) + bd_ref[...]

    @pl.when(jnp.logical_not(used))
    def _():
        yb_ref[...] = jnp.zeros_like(yb_ref)


def _experts(block_e, n_used, xb, w_up, b_up, w_down, b_down, *, n_blocks):
    def x_map(i, be, nu):
        return (jnp.minimum(i, nu[0] - 1), 0)

    def w_map(i, be, nu):
        return (be[i], 0, 0)

    grid_spec = pltpu.PrefetchScalarGridSpec(
        num_scalar_prefetch=2,
        grid=(n_blocks,),
        in_specs=[pl.BlockSpec((ROW_BLOCK, D_MODEL), x_map),
                  pl.BlockSpec((None, D_MODEL, 2 * D_FF), w_map),
                  pl.BlockSpec((None, 1, 2 * D_FF), w_map),
                  pl.BlockSpec((None, D_FF, D_MODEL), w_map),
                  pl.BlockSpec((None, 1, D_MODEL), w_map)],
        out_specs=pl.BlockSpec((ROW_BLOCK, D_MODEL), lambda i, be, nu: (i, 0)),
        scratch_shapes=[pltpu.VMEM((D_MODEL, 2 * D_FF), BF16), pltpu.VMEM((D_FF, D_MODEL), BF16)],
    )
    return pl.pallas_call(
        _expert_kernel,
        grid_spec=grid_spec,
        out_shape=jax.ShapeDtypeStruct((n_blocks * ROW_BLOCK, D_MODEL), F32),
        compiler_params=_cparams(("arbitrary",)),
        name="experts",
    )(block_e, n_used, xb, w_up, b_up, w_down, b_down)


def _split_bf16(a):
    hi = a.astype(BF16)
    return hi, (a - hi.astype(F32)).astype(BF16)


def _combine_kernel(tab_ref, nxt_ref, slot_ref, gate_ref, h2_ref, gfin_ref, yb_ref, y_ref, buf, sems,
                    *, tm):
    i = pl.program_id(0)
    n = pl.num_programs(0)
    cur = i % 2
    nrows = buf.shape[1]

    def granule(src_row, sl, dst_row):
        return pltpu.make_async_copy(
            yb_ref.at[pl.ds(pl.multiple_of(src_row, SUBLANES), SUBLANES), :],
            buf.at[sl, pl.ds(pl.multiple_of(dst_row, SUBLANES), SUBLANES), :], sems.at[sl])

    def gather(t_ref, sl):
        for e in range(N_EXPERTS):
            src = t_ref[e]
            dst = t_ref[2 * N_EXPERTS + e]

            def issue(g, c):
                granule(src + g * SUBLANES, sl, dst + g * SUBLANES).start()
                return c
            lax.fori_loop(0, t_ref[N_EXPERTS + e], issue, 0)

    @pl.when(i == 0)
    def _():
        buf[...] = jnp.zeros_like(buf)
        gather(tab_ref, 0)

    @pl.when(i + 1 < n)
    def _():
        gather(nxt_ref, 1 - cur)

    def drain(g, c):
        granule(0, cur, 0).wait()
        return c
    lax.fori_loop(0, tab_ref[3 * N_EXPERTS], drain, 0)

    s_iota = lax.broadcasted_iota(I32, (tm, nrows), 1)
    wgt = jnp.zeros((tm, nrows), F32)
    for kk in range(TOP_K):
        wgt = wgt + jnp.where(s_iota == slot_ref[:, kk:kk + 1], gate_ref[:, kk:kk + 1], 0.0)
    w_hi, w_lo = _split_bf16(wgt)
    y_hi, y_lo = _split_bf16(buf[cur])
    acc = _dot(w_hi, y_hi) + (_dot(w_hi, y_lo) + _dot(w_lo, y_hi))
    y_ref[...] = _rms(h2_ref[...] + acc, gfin_ref[...])


def _combine(tab, slot_col, gate_col, h2, gfin, yb, *, tm):
    ntok = h2.shape[0]
    n = ntok // tm
    smem = lambda imap: pl.BlockSpec((TABLE_WIDTH,), imap, memory_space=pltpu.SMEM)
    return pl.pallas_call(
        functools.partial(_combine_kernel, tm=tm),
        grid=(n,),
        in_specs=[smem(lambda i: (i,)), smem(lambda i: (jnp.minimum(i + 1, n - 1),)),
                  pl.BlockSpec((tm, TOP_K), lambda i: (i, 0)),
                  pl.BlockSpec((tm, TOP_K), lambda i: (i, 0)),
                  pl.BlockSpec((tm, D_MODEL), lambda i: (i, 0)),
                  pl.BlockSpec((1, D_MODEL), lambda i: (0, 0)),
                  pl.BlockSpec(memory_space=pl.ANY)],
        out_specs=pl.BlockSpec((tm, D_MODEL), lambda i: (i, 0)),
        out_shape=jax.ShapeDtypeStruct((ntok, D_MODEL), F32),
        scratch_shapes=[pltpu.VMEM((2, _sorted_rows(tm), D_MODEL), F32),
                        pltpu.SemaphoreType.DMA((2,))],
        compiler_params=_cparams(("arbitrary",)),
        name="combine",
    )(tab, tab, slot_col, gate_col, h2, gfin, yb)


def _tiles(a, tile):
    bsz, kk, seq = a.shape
    return a.reshape(bsz, kk, seq // tile, tile).transpose(0, 2, 1, 3).reshape(-1, kk, tile)


def _path(x, cbuf, c0, n0, m0, mkb, mvb, wts, *, tm_in, ct, chunk, tm_post, group):
    q, k, v, og, gcol, grow, yc, nbuf = _inproj(
        x, wts["g_mix"], wts["wq"], wts["wg"], wts["wgt"], wts["wc"], wts["bg"], wts["bgt"],
        wts["cw"], cbuf, tm=tm_in, chunk=chunk)
    ym, c1, n1, m1 = _mlstm(q, k, v, og, gcol, grow, c0, n0, m0, wts["ng"], ct=ct, chunk=chunk)
    h2, xn, eid, gate, rank, cnt = _post(
        ym, yc, x, wts["wmo"], wts["g_x"], wts["wxq"], mkb, mvb, wts["wxo"], wts["g_ffn"],
        wts["wrt"], wts["br"], tm=tm_post, group=group)
    return dict(h2=h2, xn=xn, eid=eid, gate=gate, rank=rank, cnt=cnt[:, :, 0],
                c1=c1, n1=n1, m1=m1[..., 0], nbuf=nbuf)


def kernel(x_prompt, x_sample, state_mlstm_c, state_mlstm_n, state_mlstm_m, state_conv, cache_mem_k, cache_mem_v, mem_prompt, norm_mix_g, w_in, b_gate, mlstm_norm_g, conv_w, w_mix_out, norm_x_g, norm_mem_g, w_xq, w_xk, w_xv, w_xo, norm_ffn_g, w_router, b_router, w_up, b_up, w_down, b_down, norm_final_g):
    bp, lp, _ = x_prompt.shape
    bs, ls, _ = x_sample.shape
    l = 0
    row = lambda a: a.reshape(1, -1)

    wi = w_in[l]
    gate_cols = wi[:, 4 * D_MLSTM:4 * D_MLSTM + 2 * N_HEADS]
    wts = dict(
        g_mix=row(norm_mix_g[l]),
        wq=wi[:, 0:4 * D_MLSTM].astype(BF16),
        wg=jnp.pad(gate_cols, ((0, 0), (0, 128 - 2 * N_HEADS))).astype(BF16),
        wgt=gate_cols.T.astype(BF16),
        wc=wi[:, 4 * D_MLSTM + 2 * N_HEADS:].astype(BF16),
        bg=row(b_gate[l]), bgt=b_gate[l].reshape(-1, 1),
        cw=conv_w[l], ng=row(mlstm_norm_g[l]),
        wmo=w_mix_out[l].astype(BF16), g_x=row(norm_x_g[l]), wxq=w_xq[l].astype(BF16),
        wxo=w_xo[l].astype(BF16), g_ffn=row(norm_ffn_g[l]),
        wrt=w_router[l].T.astype(BF16), br=b_router[l].reshape(-1, 1),
    )

    mk, mv, mkb, mvb = _memkv(mem_prompt.reshape(bp * N_MEM, D_MODEL), row(norm_mem_g[l]),
                              w_xk[l].astype(BF16), w_xv[l].astype(BF16))
    zeros = lambda *s: jnp.zeros(s, F32)
    tm_post, tmd = 256, 256
    pr = _path(x_prompt, zeros(bp, CONV_WIDTH - 1, D_CONV), zeros(bp, N_HEADS, HEAD_DIM, HEAD_DIM),
               zeros(bp, N_HEADS, HEAD_DIM), zeros(bp, N_HEADS, 1),
               mkb.reshape(bp, N_MEM, D_MODEL), mvb.reshape(bp, N_MEM, D_MODEL), wts,
               tm_in=256, ct=512, chunk=CHUNK, tm_post=tm_post, group=tmd // tm_post)
    sa = _path(x_sample, state_conv[l], state_mlstm_c[l], state_mlstm_n[l],
               state_mlstm_m[l][..., None],
               cache_mem_k[l].reshape(bs, N_MEM, D_MODEL).astype(BF16),
               cache_mem_v[l].reshape(bs, N_MEM, D_MODEL).astype(BF16), wts,
               tm_in=ls, ct=ls, chunk=min(CHUNK, ls), tm_post=ls, group=bs)

    n_ptok, n_stok = bp * lp, bs * ls
    cnt = jnp.concatenate([pr["cnt"], sa["cnt"]], axis=0)
    n_tiles = cnt.shape[0]
    n_blocks = -(-(TOP_K * (n_ptok + n_stok) + n_tiles * N_EXPERTS * (SUBLANES - 1)) // ROW_BLOCK) \
        + N_EXPERTS
    seg = (cnt + SUBLANES - 1) // SUBLANES * SUBLANES
    seg_src = jnp.cumsum(seg, axis=1) - seg
    tot = jnp.sum(seg, axis=0)
    blocks_e = (tot + ROW_BLOCK - 1) // ROW_BLOCK
    padded = blocks_e * ROW_BLOCK
    pstart = jnp.cumsum(padded) - padded
    bend = jnp.cumsum(blocks_e)
    n_used = bend[-1]
    blk = jnp.arange(n_blocks, dtype=I32)
    block_e = jnp.sum(bend[None, :] <= jnp.minimum(blk, n_used - 1)[:, None], axis=1).astype(I32)
    seg_dst = pstart[None, :] + jnp.cumsum(seg, axis=0) - seg

    gran = seg // SUBLANES
    tab = jnp.concatenate([seg_dst, gran, seg_src, jnp.sum(gran, axis=1, keepdims=True)], axis=1)
    tab = jnp.pad(tab, ((0, 0), (0, TABLE_WIDTH - tab.shape[1]))).astype(I32).reshape(-1)
    fgran = (padded - tot) // SUBLANES
    fill = jnp.concatenate([pstart + tot, fgran, jnp.sum(fgran)[None], n_used[None]])
    fill = jnp.pad(fill, (0, TABLE_WIDTH - fill.shape[0])).astype(I32)

    def lookup(table, eid):
        hit = eid[..., None] == jnp.arange(N_EXPERTS, dtype=I32)
        return jnp.sum(jnp.where(hit, table[:, None, None, :], 0), axis=-1)

    eid_pt, rank_pt = _tiles(pr["eid"], tmd), _tiles(pr["rank"], tmd)
    eid_st = sa["eid"].transpose(1, 0, 2).reshape(1, TOP_K, n_stok)
    rank_st = sa["rank"].transpose(1, 0, 2).reshape(1, TOP_K, n_stok)
    slot_p = (lookup(seg_src[:-1], eid_pt) + rank_pt).transpose(0, 2, 1).reshape(n_ptok, TOP_K)
    slot_s = (lookup(seg_src[-1:], eid_st) + rank_st).transpose(0, 2, 1).reshape(n_stok, TOP_K)

    xb = _dispatch(tab, fill, eid_pt, rank_pt, seg_src.astype(F32)[..., None],
                   pr["xn"].reshape(n_ptok, D_MODEL), eid_st[0], rank_st[0],
                   sa["xn"].reshape(n_stok, D_MODEL), tmd=tmd, n_blocks=n_blocks)
    yb = _experts(block_e, n_used[None].astype(I32), xb, w_up[l], b_up[l][:, None, :], w_down[l],
                  b_down[l][:, None, :], n_blocks=n_blocks)

    gfin = row(norm_final_g)
    split = (n_tiles - 1) * TABLE_WIDTH
    y_p = _combine(tab[:split], slot_p.astype(I32), pr["gate"].transpose(0, 2, 1).reshape(n_ptok, TOP_K),
                   pr["h2"].reshape(n_ptok, D_MODEL), gfin, yb, tm=tmd)
    y_s = _combine(tab[split:], slot_s.astype(I32), sa["gate"].transpose(0, 2, 1).reshape(n_stok, TOP_K),
                   sa["h2"].reshape(n_stok, D_MODEL), gfin, yb, tm=n_stok)

    lead = lambda a: a[None]
    return (y_p.reshape(bp, lp, D_MODEL), y_s.reshape(bs, ls, D_MODEL),
            lead(pr["c1"]), lead(pr["n1"]), lead(pr["m1"]), lead(pr["nbuf"]),
            lead(mk.reshape(bp, N_MEM, N_XHEADS, XHEAD_DIM)),
            lead(mv.reshape(bp, N_MEM, N_XHEADS, XHEAD_DIM)),
            lead(sa["c1"]), lead(sa["n1"]), lead(sa["m1"]), lead(sa["nbuf"]))
```

```python
import functools

import jax
import jax.numpy as jnp
import numpy as np
from jax import lax
from jax.experimental import pallas as pl
from jax.experimental.pallas import tpu as pltpu

F32 = jnp.float32
BF16 = jnp.bfloat16
I32 = jnp.int32

D_MODEL = 1024
N_HEADS = 4
HEAD_DIM = 128
D_MLSTM = N_HEADS * HEAD_DIM
D_CONV = D_MODEL - D_MLSTM
CONV_WIDTH = 3
CHUNK = 64
N_MEM = 256
N_XHEADS = 4
XHEAD_DIM = D_MODEL // N_XHEADS
N_EXPERTS = 32
TOP_K = 4
D_FF = D_MODEL
SWIGLU_LIMIT = 7.0
SWIGLU_ALPHA = 1.702
EPS = 1e-5

SUBLANES = 8
TABLE_WIDTH = 128
ROW_BLOCK = 256
VMEM_LIMIT = 56 * 1024 * 1024


def _cparams(sem):
    return pltpu.CompilerParams(dimension_semantics=sem, vmem_limit_bytes=VMEM_LIMIT)


def _rms(x, g):
    return x * lax.rsqrt(jnp.mean(x * x, axis=-1, keepdims=True) + EPS) * g


def _log_sigmoid(x):
    return -(jnp.maximum(-x, 0.0) + jnp.log1p(jnp.exp(-jnp.abs(x))))


def _sigmoid(x):
    return 1.0 / (1.0 + jnp.exp(-x))


def _dot(a, b):
    return jnp.dot(a, b, preferred_element_type=F32)


def _dot_nt(a, b):
    return lax.dot_general(a, b, (((1,), (1,)), ((), ())), preferred_element_type=F32)


def _dot_tn(a, b):
    return lax.dot_general(a, b, (((0,), (0,)), ((), ())), preferred_element_type=F32)


def _memkv_kernel(mem_ref, g_ref, wk_ref, wv_ref, mk_ref, mv_ref, mkb_ref, mvb_ref):
    mn = _rms(mem_ref[...], g_ref[...]).astype(BF16)
    mk = _dot(mn, wk_ref[...])
    mv = _dot(mn, wv_ref[...])
    mk_ref[...] = mk
    mv_ref[...] = mv
    mkb_ref[...] = mk.astype(BF16)
    mvb_ref[...] = mv.astype(BF16)


def _memkv(mem2d, g, wk, wv):
    rows = mem2d.shape[0]
    tm = N_MEM
    row_spec = pl.BlockSpec((tm, D_MODEL), lambda i: (i, 0))
    full = lambda shape: pl.BlockSpec(shape, lambda i: (0,) * len(shape))
    return pl.pallas_call(
        _memkv_kernel,
        grid=(rows // tm,),
        in_specs=[row_spec, full((1, D_MODEL)), full((D_MODEL, D_MODEL)), full((D_MODEL, D_MODEL))],
        out_specs=[row_spec, row_spec, row_spec, row_spec],
        out_shape=[jax.ShapeDtypeStruct((rows, D_MODEL), F32)] * 2
        + [jax.ShapeDtypeStruct((rows, D_MODEL), BF16)] * 2,
        compiler_params=_cparams(("arbitrary",)),
        name="memkv",
    )(mem2d, g, wk, wv)


def _inproj_kernel(x_ref, g_ref, wq_ref, wg_ref, wgt_ref, wc_ref, bg_ref, bgt_ref, cw_ref, cbuf_ref,
                   q_ref, k_ref, v_ref, og_ref, gcol_ref, grow_ref, yc_ref, nbuf_ref,
                   carry_ref, *, tm, chunk):
    j = pl.program_id(1)

    @pl.when(j == 0)
    def _():
        carry_ref[0:2, :] = cbuf_ref[...]

    xb = _rms(x_ref[...], g_ref[...]).astype(BF16)

    p = _dot(xb, wq_ref[...])
    q_ref[...] = p[:, 0:D_MLSTM].astype(BF16)
    k_ref[...] = (p[:, D_MLSTM:2 * D_MLSTM] * (HEAD_DIM ** -0.5)).astype(BF16)
    v_ref[...] = p[:, 2 * D_MLSTM:3 * D_MLSTM].astype(BF16)
    og_ref[...] = _sigmoid(p[:, 3 * D_MLSTM:4 * D_MLSTM])

    gc = _dot(xb, wg_ref[...])[:, 0:2 * N_HEADS] + bg_ref[...]
    col = lax.broadcasted_iota(I32, gc.shape, 1)
    gcol_ref[...] = jnp.where(col < N_HEADS, gc, _log_sigmoid(gc))
    gr = _dot_nt(wgt_ref[...], xb) + bgt_ref[...]
    row = lax.broadcasted_iota(I32, gr.shape, 0)
    gr = jnp.where(row < N_HEADS, gr, _log_sigmoid(gr))
    for c in range(tm // chunk):
        grow_ref[c] = gr[:, c * chunk:(c + 1) * chunk]

    pc = _dot(xb, wc_ref[...])
    cb = pc[:, 0:D_CONV]
    u = pc[:, D_CONV:2 * D_CONV] * pc[:, 2 * D_CONV:3 * D_CONV]
    c0 = carry_ref[0:1, :]
    c1 = carry_ref[1:2, :]
    rid = lax.broadcasted_iota(I32, u.shape, 0)
    um1 = jnp.where(rid == 0, c1, pltpu.roll(u, 1, 0))
    um2 = jnp.where(rid == 0, c0, jnp.where(rid == 1, c1, pltpu.roll(u, 2, 0)))
    yc = cw_ref[0:1, :] * um2 + cw_ref[1:2, :] * um1 + cw_ref[2:3, :] * u
    yc_ref[...] = (cb * yc).astype(BF16)
    tail = u[tm - 2:tm, :]
    carry_ref[0:2, :] = tail
    nbuf_ref[...] = tail


def _inproj(x, g, wq, wg, wgt, wc, bg, bgt, cw, cbuf, *, tm, chunk):
    bsz, seq, _ = x.shape
    grid = (bsz, seq // tm)
    tok = lambda c: pl.BlockSpec((None, tm, c), lambda b, j: (b, j, 0))
    full = lambda shape: pl.BlockSpec(shape, lambda b, j: (0,) * len(shape))
    nck = tm // chunk
    return pl.pallas_call(
        functools.partial(_inproj_kernel, tm=tm, chunk=chunk),
        grid=grid,
        in_specs=[tok(D_MODEL), full((1, D_MODEL)), full((D_MODEL, 4 * D_MLSTM)),
                  full((D_MODEL, 128)), full((8, D_MODEL)), full((D_MODEL, 3 * D_CONV)),
                  full((1, 8)), full((8, 1)), full((CONV_WIDTH, D_CONV)),
                  pl.BlockSpec((None, 2, D_CONV), lambda b, j: (b, 0, 0))],
        out_specs=[tok(D_MLSTM), tok(D_MLSTM), tok(D_MLSTM), tok(D_MLSTM), tok(8),
                   pl.BlockSpec((None, nck, 8, chunk), lambda b, j: (b, j, 0, 0)),
                   tok(D_CONV),
                   pl.BlockSpec((None, 2, D_CONV), lambda b, j: (b, 0, 0))],
        out_shape=[jax.ShapeDtypeStruct((bsz, seq, D_MLSTM), BF16)] * 3
        + [jax.ShapeDtypeStruct((bsz, seq, D_MLSTM), F32),
           jax.ShapeDtypeStruct((bsz, seq, 8), F32),
           jax.ShapeDtypeStruct((bsz, seq // chunk, 8, chunk), F32),
           jax.ShapeDtypeStruct((bsz, seq, D_CONV), BF16),
           jax.ShapeDtypeStruct((bsz, 2, D_CONV), F32)],
        scratch_shapes=[pltpu.VMEM((8, D_CONV), F32)],
        compiler_params=_cparams(("arbitrary", "arbitrary")),
        name="inproj",
    )(x, g, wq, wg, wgt, wc, bg, bgt, cw, cbuf)


def _mlstm_kernel(q_ref, k_ref, v_ref, og_ref, gc_ref, gr_ref, c0_ref, n0_ref, m0_ref, ng_ref,
                  ym_ref, c1_ref, n1_ref, m1_ref, c_s, n_s, m_s, *, chunk, nchunks, bsz):
    j = pl.program_id(0)

    @pl.when(j == 0)
    def _():
        c_s[...] = c0_ref[...]
        n_s[...] = n0_ref[...]
        m_s[...] = m0_ref[...]

    ti = lax.broadcasted_iota(I32, (chunk, chunk), 0)
    ji = lax.broadcasted_iota(I32, (chunk, chunk), 1)
    causal = ji <= ti

    def body(ci, carry):
        r0 = pl.multiple_of(ci * chunk, chunk)
        rows = pl.ds(r0, chunk)
        chains = [(b, h) for b in range(bsz) for h in range(N_HEADS)]
        cols = lambda h: slice(h * HEAD_DIM, (h + 1) * HEAD_DIM)
        each = lambda f: [f(n, b, h) for n, (b, h) in enumerate(chains)]
        q = lambda b, h: q_ref[b, rows, cols(h)]
        k = lambda b, h: k_ref[b, rows, cols(h)]
        v = lambda b, h: v_ref[b, rows, cols(h)]
        gcs = [gc_ref[b, rows, :] for b in range(bsz)]
        grs = [gr_ref[b, ci] for b in range(bsz)]
        li_c = each(lambda n, b, h: gcs[b][:, h:h + 1])
        lf_c = each(lambda n, b, h: gcs[b][:, N_HEADS + h:N_HEADS + h + 1])
        li_r = each(lambda n, b, h: grs[b][h:h + 1, :])
        lf_r = each(lambda n, b, h: grs[b][N_HEADS + h:N_HEADS + h + 1, :])
        m_prev = each(lambda n, b, h: m_s[b, h:h + 1, :])

        b_c = each(lambda n, b, h: jnp.sum(jnp.where(causal, lf_r[n], 0.0), axis=1, keepdims=True))
        b_r = each(lambda n, b, h: jnp.sum(jnp.where(ti <= ji, lf_c[n], 0.0), axis=0, keepdims=True))
        dmat = each(lambda n, b, h: jnp.where(causal, b_c[n] - b_r[n] + li_r[n], -jnp.inf))
        dmax = each(lambda n, b, h: jnp.max(dmat[n], axis=1, keepdims=True))
        inter = each(lambda n, b, h: b_c[n] + m_prev[n])
        m_t = each(lambda n, b, h: jnp.maximum(inter[n], dmax[n]))
        w_inter = each(lambda n, b, h: jnp.exp(inter[n] - m_t[n]))
        s = each(lambda n, b, h: _dot_nt(q(b, h), k(b, h)) * jnp.exp(dmat[n] - m_t[n]))
        qc = each(lambda n, b, h: _dot(q(b, h), c_s[b, h].astype(BF16)))
        sv = each(lambda n, b, h: _dot(s[n].astype(BF16), v(b, h)))
        qn = each(lambda n, b, h: jnp.sum(q(b, h).astype(F32) * n_s[b, h:h + 1, :], axis=1,
                                          keepdims=True))
        den = each(lambda n, b, h: w_inter[n] * qn[n] + jnp.sum(s[n], axis=1, keepdims=True))
        hh = each(lambda n, b, h: (w_inter[n] * qc[n] + sv[n])
                  / jnp.maximum(jnp.abs(den[n]), jnp.exp(-m_t[n])))

        m_new = each(lambda n, b, h: m_t[n][chunk - 1:chunk, :])
        b_last = each(lambda n, b, h: b_c[n][chunk - 1:chunk, :])
        decay = each(lambda n, b, h: jnp.exp(b_last[n] + m_prev[n] - m_new[n]))
        kw = each(lambda n, b, h: k(b, h).astype(F32)
                  * jnp.exp(b_last[n] - b_c[n] + li_c[n] - m_new[n]))
        kv = each(lambda n, b, h: _dot_tn(kw[n].astype(BF16), v(b, h)))
        for n, (b, h) in enumerate(chains):
            c_s[b, h] = decay[n] * c_s[b, h] + kv[n]
            n_s[b, h:h + 1, :] = decay[n] * n_s[b, h:h + 1, :] + jnp.sum(kw[n], axis=0, keepdims=True)
            m_s[b, h:h + 1, :] = m_new[n]

        hn = each(lambda n, b, h: hh[n] * lax.rsqrt(jnp.mean(hh[n] * hh[n], axis=1, keepdims=True) + EPS)
                  * ng_ref[:, cols(h)])
        for n, (b, h) in enumerate(chains):
            ym_ref[b, rows, cols(h)] = (hn[n] * og_ref[b, rows, cols(h)]).astype(BF16)
        return carry

    lax.fori_loop(0, nchunks, body, 0)

    @pl.when(j == pl.num_programs(0) - 1)
    def _():
        c1_ref[...] = c_s[...]
        n1_ref[...] = n_s[...]
        m1_ref[...] = m_s[...]


def _mlstm(q, k, v, og, gcol, grow, c0, n0, m0, ng, *, ct, chunk):
    bsz, seq, _ = q.shape
    nchunks = ct // chunk
    grid = (seq // ct,)
    tok = lambda c: pl.BlockSpec((bsz, ct, c), lambda j: (0, j, 0))
    st_c = pl.BlockSpec((bsz, N_HEADS, HEAD_DIM, HEAD_DIM), lambda j: (0, 0, 0, 0))
    st_n = pl.BlockSpec((bsz, N_HEADS, HEAD_DIM), lambda j: (0, 0, 0))
    st_m = pl.BlockSpec((bsz, N_HEADS, 1), lambda j: (0, 0, 0))
    return pl.pallas_call(
        functools.partial(_mlstm_kernel, chunk=chunk, nchunks=nchunks, bsz=bsz),
        grid=grid,
        in_specs=[tok(D_MLSTM), tok(D_MLSTM), tok(D_MLSTM), tok(D_MLSTM), tok(8),
                  pl.BlockSpec((bsz, nchunks, 8, chunk), lambda j: (0, j, 0, 0)),
                  st_c, st_n, st_m,
                  pl.BlockSpec((1, D_MLSTM), lambda j: (0, 0))],
        out_specs=[tok(D_MLSTM), st_c, st_n, st_m],
        out_shape=[jax.ShapeDtypeStruct((bsz, seq, D_MLSTM), BF16),
                   jax.ShapeDtypeStruct((bsz, N_HEADS, HEAD_DIM, HEAD_DIM), F32),
                   jax.ShapeDtypeStruct((bsz, N_HEADS, HEAD_DIM), F32),
                   jax.ShapeDtypeStruct((bsz, N_HEADS, 1), F32)],
        scratch_shapes=[pltpu.VMEM((bsz, N_HEADS, HEAD_DIM, HEAD_DIM), F32),
                        pltpu.VMEM((bsz, N_HEADS, HEAD_DIM), F32),
                        pltpu.VMEM((bsz, N_HEADS, 1), F32)],
        compiler_params=_cparams(("arbitrary",)),
        name="mlstm",
    )(q, k, v, og, gcol, grow, c0, n0, m0, ng)


def _post_kernel(ym_ref, yc_ref, x_ref, wmo_ref, gx_ref, wxq_ref, mk_ref, mv_ref, wxo_ref,
                 gf_ref, wrt_ref, br_ref,
                 h2_ref, xn_ref, eid_ref, gate_ref, rank_ref, cnt_ref, cnt_s, *, tm, group):
    step = pl.program_id(0) * pl.num_programs(1) + pl.program_id(1)

    @pl.when(step % group == 0)
    def _():
        cnt_s[...] = jnp.zeros_like(cnt_s)

    mix = _dot(ym_ref[...], wmo_ref[0:D_MLSTM, :]) + _dot(yc_ref[...], wmo_ref[D_MLSTM:D_MODEL, :])
    h1 = x_ref[...] + mix

    xq = _dot(_rms(h1, gx_ref[...]).astype(BF16), wxq_ref[...]).astype(BF16)
    att = jnp.zeros((tm, D_MODEL), F32)
    for hd in range(N_XHEADS):
        cols = slice(hd * XHEAD_DIM, (hd + 1) * XHEAD_DIM)
        s = _dot_nt(xq[:, cols], mk_ref[:, cols]) * (XHEAD_DIM ** -0.5)
        e = jnp.exp(s - jnp.max(s, axis=-1, keepdims=True))
        p = (e / jnp.sum(e, axis=-1, keepdims=True)).astype(BF16)
        o = _dot(p, mv_ref[:, cols]).astype(BF16)
        att = att + _dot(o, wxo_ref[cols, :])
    h2 = h1 + att
    h2_ref[...] = h2

    xn2 = _rms(h2, gf_ref[...]).astype(BF16)
    xn_ref[...] = xn2

    logits = _dot_nt(wrt_ref[...], xn2) + br_ref[...]
    eidx = lax.broadcasted_iota(I32, logits.shape, 0).astype(F32)
    work = logits
    vals, ids, hots = [], [], []
    for _ in range(TOP_K):
        mx = jnp.max(work, axis=0, keepdims=True)
        idx = jnp.min(jnp.where(work == mx, eidx, float(N_EXPERTS)), axis=0, keepdims=True)
        sel = eidx == idx
        vals.append(mx)
        ids.append(idx)
        hots.append(sel)
        work = jnp.where(sel, -jnp.inf, work)
    exps = [jnp.exp(v - vals[0]) for v in vals]
    denom = exps[0] + exps[1] + exps[2] + exps[3]

    picked = jnp.zeros(logits.shape, F32)
    for sel in hots:
        picked = picked + sel.astype(F32)
    tj = lax.broadcasted_iota(I32, (tm, tm), 0)
    tt = lax.broadcasted_iota(I32, (tm, tm), 1)
    before = (tj < tt).astype(BF16)
    prior = _dot(picked.astype(BF16), before) + cnt_s[...]
    for kk in range(TOP_K):
        eid_ref[kk:kk + 1, :] = ids[kk].astype(I32)
        gate_ref[kk:kk + 1, :] = exps[kk] / denom
        rank_ref[kk:kk + 1, :] = jnp.sum(jnp.where(hots[kk], prior, 0.0), axis=0,
                                         keepdims=True).astype(I32)
    total = cnt_s[...] + jnp.sum(picked, axis=1, keepdims=True)
    cnt_s[...] = total
    cnt_ref[...] = total.astype(I32)


def _post(ym, yc, x, wmo, gx, wxq, mkb, mvb, wxo, gf, wrt, br, *, tm, group):
    bsz, seq, _ = x.shape
    nj = seq // tm
    grid = (bsz, nj)
    n_tiles = bsz * nj // group
    tok = lambda c: pl.BlockSpec((None, tm, c), lambda b, j: (b, j, 0))
    full = lambda shape: pl.BlockSpec(shape, lambda b, j: (0,) * len(shape))
    mem = pl.BlockSpec((None, N_MEM, D_MODEL), lambda b, j: (b, 0, 0))
    sel = pl.BlockSpec((None, TOP_K, tm), lambda b, j: (b, 0, j))
    return pl.pallas_call(
        functools.partial(_post_kernel, tm=tm, group=group),
        grid=grid,
        in_specs=[tok(D_MLSTM), tok(D_CONV), tok(D_MODEL), full((D_MODEL, D_MODEL)),
                  full((1, D_MODEL)), full((D_MODEL, D_MODEL)), mem, mem,
                  full((D_MODEL, D_MODEL)), full((1, D_MODEL)), full((N_EXPERTS, D_MODEL)),
                  full((N_EXPERTS, 1))],
        out_specs=[tok(D_MODEL), tok(D_MODEL), sel, sel, sel,
                   pl.BlockSpec((None, N_EXPERTS, 1), lambda b, j: ((b * nj + j) // group, 0, 0))],
        out_shape=[jax.ShapeDtypeStruct((bsz, seq, D_MODEL), F32),
                   jax.ShapeDtypeStruct((bsz, seq, D_MODEL), BF16),
                   jax.ShapeDtypeStruct((bsz, TOP_K, seq), I32),
                   jax.ShapeDtypeStruct((bsz, TOP_K, seq), F32),
                   jax.ShapeDtypeStruct((bsz, TOP_K, seq), I32),
                   jax.ShapeDtypeStruct((n_tiles, N_EXPERTS, 1), I32)],
        scratch_shapes=[pltpu.VMEM((N_EXPERTS, 1), F32)],
        compiler_params=_cparams(("arbitrary", "arbitrary")),
        name="post",
    )(ym, yc, x, wmo, gx, wxq, mkb, mvb, wxo, gf, wrt, br)


def _sorted_rows(n_tokens):
    return -(-(TOP_K * n_tokens + N_EXPERTS * (SUBLANES - 1)) // ROW_BLOCK) * ROW_BLOCK


def _dispatch_kernel(tab_ref, fill_ref, eid_ref, rank_ref, ls_ref, x_ref, eids_ref, ranks_ref, xs_ref,
                     xb_ref, srt, zero_s, sem, *, n_ptiles, n_blocks):
    i = pl.program_id(0)

    def granule(src, src_row, dst_row):
        return pltpu.make_async_copy(
            src.at[pl.ds(pl.multiple_of(src_row, SUBLANES), SUBLANES), :],
            xb_ref.at[pl.ds(pl.multiple_of(dst_row, SUBLANES), SUBLANES), :], sem)

    def drain(count):
        def body(g, c):
            granule(zero_s, 0, 0).wait()
            return c
        lax.fori_loop(0, count, body, 0)

    def sort_and_move(eid, rank, x):
        ntok = x.shape[0]
        nrows = _sorted_rows(ntok)
        e_iota = lax.broadcasted_iota(I32, (N_EXPERTS, ntok), 0)
        s_iota = lax.broadcasted_iota(I32, (nrows, ntok), 0)
        seg_start = ls_ref[...]
        hit = None
        for kk in range(TOP_K):
            start = jnp.sum(jnp.where(e_iota == eid[kk:kk + 1, :], seg_start, 0.0),
                            axis=0, keepdims=True).astype(I32)
            match = s_iota == start + rank[kk:kk + 1, :]
            hit = match if hit is None else jnp.logical_or(hit, match)
        perm = jnp.where(hit, 1.0, 0.0).astype(BF16)
        srt[0:nrows, :] = _dot(perm, x)
        for e in range(N_EXPERTS):
            dst = tab_ref[e]
            src = tab_ref[2 * N_EXPERTS + e]

            def issue(g, c):
                granule(srt, src + g * SUBLANES, dst + g * SUBLANES).start()
                return c
            lax.fori_loop(0, tab_ref[N_EXPERTS + e], issue, 0)
        drain(tab_ref[3 * N_EXPERTS])

    @pl.when(i < n_ptiles)
    def _():
        sort_and_move(eid_ref[...], rank_ref[...], x_ref[...])

    @pl.when(i == n_ptiles)
    def _():
        sort_and_move(eids_ref[...], ranks_ref[...], xs_ref[...])
        zero_s[...] = jnp.zeros_like(zero_s)
        for e in range(N_EXPERTS):
            dst = fill_ref[e]

            def zissue(g, c):
                granule(zero_s, 0, dst + g * SUBLANES).start()
                return c
            lax.fori_loop(0, fill_ref[N_EXPERTS + e], zissue, 0)
        drain(fill_ref[2 * N_EXPERTS])
        first_free = fill_ref[2 * N_EXPERTS + 1]

        def blk_copy(b):
            return pltpu.make_async_copy(
                zero_s, xb_ref.at[pl.ds(pl.multiple_of(b * ROW_BLOCK, ROW_BLOCK), ROW_BLOCK), :], sem)

        def bissue(b, c):
            blk_copy(b).start()
            return c
        lax.fori_loop(first_free, n_blocks, bissue, 0)

        def bdrain(b, c):
            blk_copy(0).wait()
            return c
        lax.fori_loop(first_free, n_blocks, bdrain, 0)


def _dispatch(tab, fill, eid_p, rank_p, seg_start, xn_p, eid_s, rank_s, xn_s, *, tmd, n_blocks):
    n_ptiles = eid_p.shape[0]
    n_sample = xn_s.shape[0]
    last = n_ptiles - 1
    smem = lambda shape, imap: pl.BlockSpec(shape, imap, memory_space=pltpu.SMEM)
    return pl.pallas_call(
        functools.partial(_dispatch_kernel, n_ptiles=n_ptiles, n_blocks=n_blocks),
        grid=(n_ptiles + 1,),
        in_specs=[smem((TABLE_WIDTH,), lambda i: (i,)),
                  smem((TABLE_WIDTH,), lambda i: (0,)),
                  pl.BlockSpec((None, TOP_K, tmd), lambda i: (jnp.minimum(i, last), 0, 0)),
                  pl.BlockSpec((None, TOP_K, tmd), lambda i: (jnp.minimum(i, last), 0, 0)),
                  pl.BlockSpec((None, N_EXPERTS, 1), lambda i: (i, 0, 0)),
                  pl.BlockSpec((tmd, D_MODEL), lambda i: (jnp.minimum(i, last), 0)),
                  pl.BlockSpec((TOP_K, n_sample), lambda i: (0, 0)),
                  pl.BlockSpec((TOP_K, n_sample), lambda i: (0, 0)),
                  pl.BlockSpec((n_sample, D_MODEL), lambda i: (0, 0))],
        out_specs=pl.BlockSpec(memory_space=pl.ANY),
        out_shape=jax.ShapeDtypeStruct((n_blocks * ROW_BLOCK, D_MODEL), F32),
        scratch_shapes=[pltpu.VMEM((_sorted_rows(tmd), D_MODEL), F32),
                        pltpu.VMEM((ROW_BLOCK, D_MODEL), F32), pltpu.SemaphoreType.DMA(())],
        compiler_params=_cparams(("arbitrary",)),
        name="dispatch",
    )(tab, fill, eid_p, rank_p, seg_start, xn_p, eid_s, rank_s, xn_s)


def _expert_kernel(tab_ref, xb_ref, wu_ref, bu_ref, wd_ref, bd_ref, yb_ref,
                   wu_s, wd_s, xbuf, ybuf, xsem, ysem, *, n_blocks):
    e = pl.program_id(0)
    first = tab_ref[e]
    nblk = tab_ref[N_EXPERTS + e]

    def rows_of(j):
        return pl.ds(pl.multiple_of((first + j) * ROW_BLOCK, ROW_BLOCK), ROW_BLOCK)

    def x_copy(j, sl):
        return pltpu.make_async_copy(xb_ref.at[rows_of(j), :], xbuf.at[sl], xsem.at[sl])

    def y_copy(j, sl):
        return pltpu.make_async_copy(ybuf.at[sl], yb_ref.at[rows_of(j), :], ysem.at[sl])

    @pl.when(nblk > 0)
    def _():
        x_copy(0, 0).start()

    step = 128

    def cast(r, c):
        rows = pl.ds(pl.multiple_of(r * step, step), step)
        wu_s[rows, :] = wu_ref[rows, :].astype(BF16)
        wd_s[rows, :] = wd_ref[rows, :].astype(BF16)
        return c
    lax.fori_loop(0, D_MODEL // step, cast, 0)

    def block(j, c):
        sl = j % 2
        x_copy(j, sl).wait()

        @pl.when(j + 1 < nblk)
        def _():
            x_copy(j + 1, 1 - sl).start()

        @pl.when(j >= 2)
        def _():
            y_copy(j - 2, sl).wait()

        hcat = _dot(xbuf[sl].astype(BF16), wu_s[...]) + bu_ref[...]
        glu = jnp.minimum(hcat[:, 0:D_FF], SWIGLU_LIMIT)
        lin = jnp.clip(hcat[:, D_FF:2 * D_FF], -SWIGLU_LIMIT, SWIGLU_LIMIT)
        act = glu * _sigmoid(SWIGLU_ALPHA * glu) * (lin + 1.0)
        ybuf[sl] = _dot(act.astype(BF16), wd_s[...]) + bd_ref[...]
        y_copy(j, sl).start()
        return c
    lax.fori_loop(0, nblk, block, 0)

    @pl.when(nblk >= 2)
    def _():
        y_copy(nblk - 2, nblk % 2).wait()

    @pl.when(nblk >= 1)
    def _():
        y_copy(nblk - 1, (nblk - 1) % 2).wait()

    @pl.when(e == N_EXPERTS - 1)
    def _():
        ybuf[0] = jnp.zeros((ROW_BLOCK, D_MODEL), F32)
        free = tab_ref[2 * N_EXPERTS]

        def tail(b):
            return pltpu.make_async_copy(
                ybuf.at[0], yb_ref.at[pl.ds(pl.multiple_of(b * ROW_BLOCK, ROW_BLOCK), ROW_BLOCK), :],
                ysem.at[0])

        def zissue(b, c):
            tail(b).start()
            return c
        lax.fori_loop(free, n_blocks, zissue, 0)

        def zdrain(b, c):
            tail(0).wait()
            return c
        lax.fori_loop(free, n_blocks, zdrain, 0)


def _experts(tab, xb, w_up, b_up, w_down, b_down, *, n_blocks):
    w_map = lambda e, tab: (e, 0, 0)
    grid_spec = pltpu.PrefetchScalarGridSpec(
        num_scalar_prefetch=1,
        grid=(N_EXPERTS,),
        in_specs=[pl.BlockSpec(memory_space=pl.ANY),
                  pl.BlockSpec((None, D_MODEL, 2 * D_FF), w_map),
                  pl.BlockSpec((None, 1, 2 * D_FF), w_map),
                  pl.BlockSpec((None, D_FF, D_MODEL), w_map),
                  pl.BlockSpec((None, 1, D_MODEL), w_map)],
        out_specs=pl.BlockSpec(memory_space=pl.ANY),
        scratch_shapes=[pltpu.VMEM((D_MODEL, 2 * D_FF), BF16), pltpu.VMEM((D_FF, D_MODEL), BF16),
                        pltpu.VMEM((2, ROW_BLOCK, D_MODEL), F32),
                        pltpu.VMEM((2, ROW_BLOCK, D_MODEL), F32),
                        pltpu.SemaphoreType.DMA((2,)), pltpu.SemaphoreType.DMA((2,))],
    )
    return pl.pallas_call(
        functools.partial(_expert_kernel, n_blocks=n_blocks),
        grid_spec=grid_spec,
        out_shape=jax.ShapeDtypeStruct((n_blocks * ROW_BLOCK, D_MODEL), F32),
        compiler_params=_cparams(("arbitrary",)),
        name="experts",
    )(tab, xb, w_up, b_up, w_down, b_down)


def _split_bf16(a):
    hi = a.astype(BF16)
    return hi, (a - hi.astype(F32)).astype(BF16)


def _combine_kernel(tab_ref, nxt_ref, slot_ref, gate_ref, h2_ref, gfin_ref, yb_ref, y_ref, buf, sems,
                    *, tm):
    i = pl.program_id(0)
    n = pl.num_programs(0)
    cur = i % 2
    nrows = buf.shape[1]

    def granule(src_row, sl, dst_row):
        return pltpu.make_async_copy(
            yb_ref.at[pl.ds(pl.multiple_of(src_row, SUBLANES), SUBLANES), :],
            buf.at[sl, pl.ds(pl.multiple_of(dst_row, SUBLANES), SUBLANES), :], sems.at[sl])

    def gather(t_ref, sl):
        for e in range(N_EXPERTS):
            src = t_ref[e]
            dst = t_ref[2 * N_EXPERTS + e]

            def issue(g, c):
                granule(src + g * SUBLANES, sl, dst + g * SUBLANES).start()
                return c
            lax.fori_loop(0, t_ref[N_EXPERTS + e], issue, 0)

    @pl.when(i == 0)
    def _():
        buf[...] = jnp.zeros_like(buf)
        gather(tab_ref, 0)

    @pl.when(i + 1 < n)
    def _():
        gather(nxt_ref, 1 - cur)

    def drain(g, c):
        granule(0, cur, 0).wait()
        return c
    lax.fori_loop(0, tab_ref[3 * N_EXPERTS], drain, 0)

    s_iota = lax.broadcasted_iota(I32, (tm, nrows), 1)
    wgt = jnp.zeros((tm, nrows), F32)
    for kk in range(TOP_K):
        wgt = wgt + jnp.where(s_iota == slot_ref[:, kk:kk + 1], gate_ref[:, kk:kk + 1], 0.0)
    w_hi, w_lo = _split_bf16(wgt)
    y_hi, y_lo = _split_bf16(buf[cur])
    acc = _dot(w_hi, y_hi) + (_dot(w_hi, y_lo) + _dot(w_lo, y_hi))
    y_ref[...] = _rms(h2_ref[...] + acc, gfin_ref[...])


def _combine(tab, slot_col, gate_col, h2, gfin, yb, *, tm):
    ntok = h2.shape[0]
    n = ntok // tm
    smem = lambda imap: pl.BlockSpec((TABLE_WIDTH,), imap, memory_space=pltpu.SMEM)
    return pl.pallas_call(
        functools.partial(_combine_kernel, tm=tm),
        grid=(n,),
        in_specs=[smem(lambda i: (i,)), smem(lambda i: (jnp.minimum(i + 1, n - 1),)),
                  pl.BlockSpec((tm, TOP_K), lambda i: (i, 0)),
                  pl.BlockSpec((tm, TOP_K), lambda i: (i, 0)),
                  pl.BlockSpec((tm, D_MODEL), lambda i: (i, 0)),
                  pl.BlockSpec((1, D_MODEL), lambda i: (0, 0)),
                  pl.BlockSpec(memory_space=pl.ANY)],
        out_specs=pl.BlockSpec((tm, D_MODEL), lambda i: (i, 0)),
        out_shape=jax.ShapeDtypeStruct((ntok, D_MODEL), F32),
        scratch_shapes=[pltpu.VMEM((2, _sorted_rows(tm), D_MODEL), F32),
                        pltpu.SemaphoreType.DMA((2,))],
        compiler_params=_cparams(("arbitrary",)),
        name="combine",
    )(tab, tab, slot_col, gate_col, h2, gfin, yb)


def _tiles(a, tile):
    bsz, kk, seq = a.shape
    return a.reshape(bsz, kk, seq // tile, tile).transpose(0, 2, 1, 3).reshape(-1, kk, tile)


def _path(x, cbuf, c0, n0, m0, mkb, mvb, wts, *, tm_in, ct, chunk, tm_post, group):
    q, k, v, og, gcol, grow, yc, nbuf = _inproj(
        x, wts["g_mix"], wts["wq"], wts["wg"], wts["wgt"], wts["wc"], wts["bg"], wts["bgt"],
        wts["cw"], cbuf, tm=tm_in, chunk=chunk)
    ym, c1, n1, m1 = _mlstm(q, k, v, og, gcol, grow, c0, n0, m0, wts["ng"], ct=ct, chunk=chunk)
    h2, xn, eid, gate, rank, cnt = _post(
        ym, yc, x, wts["wmo"], wts["g_x"], wts["wxq"], mkb, mvb, wts["wxo"], wts["g_ffn"],
        wts["wrt"], wts["br"], tm=tm_post, group=group)
    return dict(h2=h2, xn=xn, eid=eid, gate=gate, rank=rank, cnt=cnt[:, :, 0],
                c1=c1, n1=n1, m1=m1[..., 0], nbuf=nbuf)


def kernel(x_prompt, x_sample, state_mlstm_c, state_mlstm_n, state_mlstm_m, state_conv, cache_mem_k, cache_mem_v, mem_prompt, norm_mix_g, w_in, b_gate, mlstm_norm_g, conv_w, w_mix_out, norm_x_g, norm_mem_g, w_xq, w_xk, w_xv, w_xo, norm_ffn_g, w_router, b_router, w_up, b_up, w_down, b_down, norm_final_g):
    bp, lp, _ = x_prompt.shape
    bs, ls, _ = x_sample.shape
    l = 0
    row = lambda a: a.reshape(1, -1)

    wi = w_in[l]
    gate_cols = wi[:, 4 * D_MLSTM:4 * D_MLSTM + 2 * N_HEADS]
    wts = dict(
        g_mix=row(norm_mix_g[l]),
        wq=wi[:, 0:4 * D_MLSTM].astype(BF16),
        wg=jnp.pad(gate_cols, ((0, 0), (0, 128 - 2 * N_HEADS))).astype(BF16),
        wgt=gate_cols.T.astype(BF16),
        wc=wi[:, 4 * D_MLSTM + 2 * N_HEADS:].astype(BF16),
        bg=row(b_gate[l]), bgt=b_gate[l].reshape(-1, 1),
        cw=conv_w[l], ng=row(mlstm_norm_g[l]),
        wmo=w_mix_out[l].astype(BF16), g_x=row(norm_x_g[l]), wxq=w_xq[l].astype(BF16),
        wxo=w_xo[l].astype(BF16), g_ffn=row(norm_ffn_g[l]),
        wrt=w_router[l].T.astype(BF16), br=b_router[l].reshape(-1, 1),
    )

    mk, mv, mkb, mvb = _memkv(mem_prompt.reshape(bp * N_MEM, D_MODEL), row(norm_mem_g[l]),
                              w_xk[l].astype(BF16), w_xv[l].astype(BF16))
    zeros = lambda *s: jnp.zeros(s, F32)
    tm_post, tmd = 256, 256
    pr = _path(x_prompt, zeros(bp, CONV_WIDTH - 1, D_CONV), zeros(bp, N_HEADS, HEAD_DIM, HEAD_DIM),
               zeros(bp, N_HEADS, HEAD_DIM), zeros(bp, N_HEADS, 1),
               mkb.reshape(bp, N_MEM, D_MODEL), mvb.reshape(bp, N_MEM, D_MODEL), wts,
               tm_in=512, ct=512, chunk=CHUNK, tm_post=tm_post, group=tmd // tm_post)
    sa = _path(x_sample, state_conv[l], state_mlstm_c[l], state_mlstm_n[l],
               state_mlstm_m[l][..., None],
               cache_mem_k[l].reshape(bs, N_MEM, D_MODEL).astype(BF16),
               cache_mem_v[l].reshape(bs, N_MEM, D_MODEL).astype(BF16), wts,
               tm_in=ls, ct=ls, chunk=min(CHUNK, ls), tm_post=ls, group=bs)

    n_ptok, n_stok = bp * lp, bs * ls
    cnt = jnp.concatenate([pr["cnt"], sa["cnt"]], axis=0)
    n_tiles = cnt.shape[0]
    n_blocks = -(-(TOP_K * (n_ptok + n_stok) + n_tiles * N_EXPERTS * (SUBLANES - 1)) // ROW_BLOCK) \
        + N_EXPERTS
    seg = (cnt + SUBLANES - 1) // SUBLANES * SUBLANES
    seg_src = jnp.cumsum(seg, axis=1) - seg
    tot = jnp.sum(seg, axis=0)
    blocks_e = (tot + ROW_BLOCK - 1) // ROW_BLOCK
    padded = blocks_e * ROW_BLOCK
    pstart = jnp.cumsum(padded) - padded
    bend = jnp.cumsum(blocks_e)
    n_used = bend[-1]
    exp_tab = jnp.concatenate([bend - blocks_e, blocks_e, n_used[None]]).astype(I32)
    seg_dst = pstart[None, :] + jnp.cumsum(seg, axis=0) - seg

    gran = seg // SUBLANES
    tab = jnp.concatenate([seg_dst, gran, seg_src, jnp.sum(gran, axis=1, keepdims=True)], axis=1)
    tab = jnp.pad(tab, ((0, 0), (0, TABLE_WIDTH - tab.shape[1]))).astype(I32).reshape(-1)
    fgran = (padded - tot) // SUBLANES
    fill = jnp.concatenate([pstart + tot, fgran, jnp.sum(fgran)[None], n_used[None]])
    fill = jnp.pad(fill, (0, TABLE_WIDTH - fill.shape[0])).astype(I32)

    def lookup(table, eid):
        hit = eid[..., None] == jnp.arange(N_EXPERTS, dtype=I32)
        return jnp.sum(jnp.where(hit, table[:, None, None, :], 0), axis=-1)

    eid_pt, rank_pt = _tiles(pr["eid"], tmd), _tiles(pr["rank"], tmd)
    eid_st = sa["eid"].transpose(1, 0, 2).reshape(1, TOP_K, n_stok)
    rank_st = sa["rank"].transpose(1, 0, 2).reshape(1, TOP_K, n_stok)
    slot_p = (lookup(seg_src[:-1], eid_pt) + rank_pt).transpose(0, 2, 1).reshape(n_ptok, TOP_K)
    slot_s = (lookup(seg_src[-1:], eid_st) + rank_st).transpose(0, 2, 1).reshape(n_stok, TOP_K)

    xb = _dispatch(tab, fill, eid_pt, rank_pt, seg_src.astype(F32)[..., None],
                   pr["xn"].reshape(n_ptok, D_MODEL), eid_st[0], rank_st[0],
                   sa["xn"].reshape(n_stok, D_MODEL), tmd=tmd, n_blocks=n_blocks)
    yb = _experts(exp_tab, xb, w_up[l], b_up[l][:, None, :], w_down[l], b_down[l][:, None, :],
                  n_blocks=n_blocks)

    gfin = row(norm_final_g)
    split = (n_tiles - 1) * TABLE_WIDTH
    y_p = _combine(tab[:split], slot_p.astype(I32), pr["gate"].transpose(0, 2, 1).reshape(n_ptok, TOP_K),
                   pr["h2"].reshape(n_ptok, D_MODEL), gfin, yb, tm=tmd)
    y_s = _combine(tab[split:], slot_s.astype(I32), sa["gate"].transpose(0, 2, 1).reshape(n_stok, TOP_K),
                   sa["h2"].reshape(n_stok, D_MODEL), gfin, yb, tm=n_stok)

    lead = lambda a: a[None]
    return (y_p.reshape(bp, lp, D_MODEL), y_s.reshape(bs, ls, D_MODEL),
            lead(pr["c1"]), lead(pr["n1"]), lead(pr["m1"]), lead(pr["nbuf"]),
            lead(mk.reshape(bp, N_MEM, N_XHEADS, XHEAD_DIM)),
            lead(mv.reshape(bp, N_MEM, N_XHEADS, XHEAD_DIM)),
            lead(sa["c1"]), lead(sa["n1"]), lead(sa["m1"]), lead(sa["nbuf"]))
```

```python
import functools

import jax
import jax.numpy as jnp
import numpy as np
from jax import lax
from jax.experimental import pallas as pl
from jax.experimental.pallas import tpu as pltpu

F32 = jnp.float32
BF16 = jnp.bfloat16
I32 = jnp.int32

D_MODEL = 1024
N_HEADS = 4
HEAD_DIM = 128
D_MLSTM = N_HEADS * HEAD_DIM
D_CONV = D_MODEL - D_MLSTM
CONV_WIDTH = 3
CHUNK = 64
N_MEM = 256
N_XHEADS = 4
XHEAD_DIM = D_MODEL // N_XHEADS
N_EXPERTS = 32
TOP_K = 4
D_FF = D_MODEL
SWIGLU_LIMIT = 7.0
SWIGLU_ALPHA = 1.702
EPS = 1e-5

SUBLANES = 8
TABLE_WIDTH = 128
ROW_BLOCK = 256
VMEM_LIMIT = 56 * 1024 * 1024


def _cparams(sem):
    return pltpu.CompilerParams(dimension_semantics=sem, vmem_limit_bytes=VMEM_LIMIT)


def _rms(x, g):
    return x * lax.rsqrt(jnp.mean(x * x, axis=-1, keepdims=True) + EPS) * g


def _log_sigmoid(x):
    return -(jnp.maximum(-x, 0.0) + jnp.log1p(jnp.exp(-jnp.abs(x))))


def _sigmoid(x):
    return 1.0 / (1.0 + jnp.exp(-x))


def _dot(a, b):
    return jnp.dot(a, b, preferred_element_type=F32)


def _dot_nt(a, b):
    return lax.dot_general(a, b, (((1,), (1,)), ((), ())), preferred_element_type=F32)


def _dot_tn(a, b):
    return lax.dot_general(a, b, (((0,), (0,)), ((), ())), preferred_element_type=F32)


def _memkv_kernel(mem_ref, g_ref, wk_ref, wv_ref, mk_ref, mv_ref, mkb_ref, mvb_ref):
    mn = _rms(mem_ref[...], g_ref[...]).astype(BF16)
    mk = _dot(mn, wk_ref[...])
    mv = _dot(mn, wv_ref[...])
    mk_ref[...] = mk
    mv_ref[...] = mv
    mkb_ref[...] = mk.astype(BF16)
    mvb_ref[...] = mv.astype(BF16)


def _memkv(mem2d, g, wk, wv):
    rows = mem2d.shape[0]
    tm = N_MEM
    row_spec = pl.BlockSpec((tm, D_MODEL), lambda i: (i, 0))
    full = lambda shape: pl.BlockSpec(shape, lambda i: (0,) * len(shape))
    return pl.pallas_call(
        _memkv_kernel,
        grid=(rows // tm,),
        in_specs=[row_spec, full((1, D_MODEL)), full((D_MODEL, D_MODEL)), full((D_MODEL, D_MODEL))],
        out_specs=[row_spec, row_spec, row_spec, row_spec],
        out_shape=[jax.ShapeDtypeStruct((rows, D_MODEL), F32)] * 2
        + [jax.ShapeDtypeStruct((rows, D_MODEL), BF16)] * 2,
        compiler_params=_cparams(("arbitrary",)),
        name="memkv",
    )(mem2d, g, wk, wv)


def _inproj_kernel(x_ref, g_ref, wq_ref, wg_ref, wgt_ref, wc_ref, bg_ref, bgt_ref, cw_ref, cbuf_ref,
                   q_ref, k_ref, v_ref, og_ref, gcol_ref, grow_ref, yc_ref, nbuf_ref,
                   carry_ref, *, tm, chunk):
    j = pl.program_id(1)

    @pl.when(j == 0)
    def _():
        carry_ref[0:2, :] = cbuf_ref[...]

    xb = _rms(x_ref[...], g_ref[...]).astype(BF16)

    p = _dot(xb, wq_ref[...])
    q_ref[...] = p[:, 0:D_MLSTM].astype(BF16)
    k_ref[...] = (p[:, D_MLSTM:2 * D_MLSTM] * (HEAD_DIM ** -0.5)).astype(BF16)
    v_ref[...] = p[:, 2 * D_MLSTM:3 * D_MLSTM].astype(BF16)
    og_ref[...] = _sigmoid(p[:, 3 * D_MLSTM:4 * D_MLSTM])

    gc = _dot(xb, wg_ref[...])[:, 0:2 * N_HEADS] + bg_ref[...]
    col = lax.broadcasted_iota(I32, gc.shape, 1)
    gcol_ref[...] = jnp.where(col < N_HEADS, gc, _log_sigmoid(gc))
    gr = _dot_nt(wgt_ref[...], xb) + bgt_ref[...]
    row = lax.broadcasted_iota(I32, gr.shape, 0)
    gr = jnp.where(row < N_HEADS, gr, _log_sigmoid(gr))
    for c in range(tm // chunk):
        grow_ref[c] = gr[:, c * chunk:(c + 1) * chunk]

    pc = _dot(xb, wc_ref[...])
    cb = pc[:, 0:D_CONV]
    u = pc[:, D_CONV:2 * D_CONV] * pc[:, 2 * D_CONV:3 * D_CONV]
    c0 = carry_ref[0:1, :]
    c1 = carry_ref[1:2, :]
    rid = lax.broadcasted_iota(I32, u.shape, 0)
    um1 = jnp.where(rid == 0, c1, pltpu.roll(u, 1, 0))
    um2 = jnp.where(rid == 0, c0, jnp.where(rid == 1, c1, pltpu.roll(u, 2, 0)))
    yc = cw_ref[0:1, :] * um2 + cw_ref[1:2, :] * um1 + cw_ref[2:3, :] * u
    yc_ref[...] = (cb * yc).astype(BF16)
    tail = u[tm - 2:tm, :]
    carry_ref[0:2, :] = tail
    nbuf_ref[...] = tail


def _inproj(x, g, wq, wg, wgt, wc, bg, bgt, cw, cbuf, *, tm, chunk):
    bsz, seq, _ = x.shape
    grid = (bsz, seq // tm)
    tok = lambda c: pl.BlockSpec((None, tm, c), lambda b, j: (b, j, 0))
    full = lambda shape: pl.BlockSpec(shape, lambda b, j: (0,) * len(shape))
    nck = tm // chunk
    return pl.pallas_call(
        functools.partial(_inproj_kernel, tm=tm, chunk=chunk),
        grid=grid,
        in_specs=[tok(D_MODEL), full((1, D_MODEL)), full((D_MODEL, 4 * D_MLSTM)),
                  full((D_MODEL, 128)), full((8, D_MODEL)), full((D_MODEL, 3 * D_CONV)),
                  full((1, 8)), full((8, 1)), full((CONV_WIDTH, D_CONV)),
                  pl.BlockSpec((None, 2, D_CONV), lambda b, j: (b, 0, 0))],
        out_specs=[tok(D_MLSTM), tok(D_MLSTM), tok(D_MLSTM), tok(D_MLSTM), tok(8),
                   pl.BlockSpec((None, nck, 8, chunk), lambda b, j: (b, j, 0, 0)),
                   tok(D_CONV),
                   pl.BlockSpec((None, 2, D_CONV), lambda b, j: (b, 0, 0))],
        out_shape=[jax.ShapeDtypeStruct((bsz, seq, D_MLSTM), BF16)] * 3
        + [jax.ShapeDtypeStruct((bsz, seq, D_MLSTM), F32),
           jax.ShapeDtypeStruct((bsz, seq, 8), F32),
           jax.ShapeDtypeStruct((bsz, seq // chunk, 8, chunk), F32),
           jax.ShapeDtypeStruct((bsz, seq, D_CONV), BF16),
           jax.ShapeDtypeStruct((bsz, 2, D_CONV), F32)],
        scratch_shapes=[pltpu.VMEM((8, D_CONV), F32)],
        compiler_params=_cparams(("arbitrary", "arbitrary")),
        name="inproj",
    )(x, g, wq, wg, wgt, wc, bg, bgt, cw, cbuf)


def _mlstm_kernel(q_ref, k_ref, v_ref, og_ref, gc_ref, gr_ref, c0_ref, n0_ref, m0_ref, ng_ref,
                  ym_ref, c1_ref, n1_ref, m1_ref, c_s, n_s, m_s, *, chunk, nchunks, bsz):
    j = pl.program_id(0)

    @pl.when(j == 0)
    def _():
        c_s[...] = c0_ref[...]
        n_s[...] = n0_ref[...]
        m_s[...] = m0_ref[...]

    ti = lax.broadcasted_iota(I32, (chunk, chunk), 0)
    ji = lax.broadcasted_iota(I32, (chunk, chunk), 1)
    causal = ji <= ti

    def body(ci, carry):
        r0 = pl.multiple_of(ci * chunk, chunk)
        rows = pl.ds(r0, chunk)
        chains = [(b, h) for b in range(bsz) for h in range(N_HEADS)]
        cols = lambda h: slice(h * HEAD_DIM, (h + 1) * HEAD_DIM)
        each = lambda f: [f(n, b, h) for n, (b, h) in enumerate(chains)]
        q = lambda b, h: q_ref[b, rows, cols(h)]
        k = lambda b, h: k_ref[b, rows, cols(h)]
        v = lambda b, h: v_ref[b, rows, cols(h)]
        gcs = [gc_ref[b, rows, :] for b in range(bsz)]
        grs = [gr_ref[b, ci] for b in range(bsz)]
        li_c = each(lambda n, b, h: gcs[b][:, h:h + 1])
        lf_c = each(lambda n, b, h: gcs[b][:, N_HEADS + h:N_HEADS + h + 1])
        li_r = each(lambda n, b, h: grs[b][h:h + 1, :])
        lf_r = each(lambda n, b, h: grs[b][N_HEADS + h:N_HEADS + h + 1, :])
        m_prev = each(lambda n, b, h: m_s[b, h:h + 1, :])

        b_c = each(lambda n, b, h: jnp.sum(jnp.where(causal, lf_r[n], 0.0), axis=1, keepdims=True))
        b_r = each(lambda n, b, h: jnp.sum(jnp.where(ti <= ji, lf_c[n], 0.0), axis=0, keepdims=True))
        dmat = each(lambda n, b, h: jnp.where(causal, b_c[n] - b_r[n] + li_r[n], -jnp.inf))
        dmax = each(lambda n, b, h: jnp.max(dmat[n], axis=1, keepdims=True))
        inter = each(lambda n, b, h: b_c[n] + m_prev[n])
        m_t = each(lambda n, b, h: jnp.maximum(inter[n], dmax[n]))
        w_inter = each(lambda n, b, h: jnp.exp(inter[n] - m_t[n]))
        s = each(lambda n, b, h: _dot_nt(q(b, h), k(b, h)) * jnp.exp(dmat[n] - m_t[n]))
        qc = each(lambda n, b, h: _dot(q(b, h), c_s[b, h].astype(BF16)))
        sv = each(lambda n, b, h: _dot(s[n].astype(BF16), v(b, h)))
        qn = each(lambda n, b, h: jnp.sum(q(b, h).astype(F32) * n_s[b, h:h + 1, :], axis=1,
                                          keepdims=True))
        den = each(lambda n, b, h: w_inter[n] * qn[n] + jnp.sum(s[n], axis=1, keepdims=True))
        hh = each(lambda n, b, h: (w_inter[n] * qc[n] + sv[n])
                  / jnp.maximum(jnp.abs(den[n]), jnp.exp(-m_t[n])))

        m_new = each(lambda n, b, h: m_t[n][chunk - 1:chunk, :])
        b_last = each(lambda n, b, h: b_c[n][chunk - 1:chunk, :])
        decay = each(lambda n, b, h: jnp.exp(b_last[n] + m_prev[n] - m_new[n]))
        kw = each(lambda n, b, h: k(b, h).astype(F32)
                  * jnp.exp(b_last[n] - b_c[n] + li_c[n] - m_new[n]))
        kv = each(lambda n, b, h: _dot_tn(kw[n].astype(BF16), v(b, h)))
        for n, (b, h) in enumerate(chains):
            c_s[b, h] = decay[n] * c_s[b, h] + kv[n]
            n_s[b, h:h + 1, :] = decay[n] * n_s[b, h:h + 1, :] + jnp.sum(kw[n], axis=0, keepdims=True)
            m_s[b, h:h + 1, :] = m_new[n]

        hn = each(lambda n, b, h: hh[n] * lax.rsqrt(jnp.mean(hh[n] * hh[n], axis=1, keepdims=True) + EPS)
                  * ng_ref[:, cols(h)])
        for n, (b, h) in enumerate(chains):
            ym_ref[b, rows, cols(h)] = (hn[n] * og_ref[b, rows, cols(h)]).astype(BF16)
        return carry

    lax.fori_loop(0, nchunks, body, 0)

    @pl.when(j == pl.num_programs(0) - 1)
    def _():
        c1_ref[...] = c_s[...]
        n1_ref[...] = n_s[...]
        m1_ref[...] = m_s[...]


def _mlstm(q, k, v, og, gcol, grow, c0, n0, m0, ng, *, ct, chunk):
    bsz, seq, _ = q.shape
    nchunks = ct // chunk
    grid = (seq // ct,)
    tok = lambda c: pl.BlockSpec((bsz, ct, c), lambda j: (0, j, 0))
    st_c = pl.BlockSpec((bsz, N_HEADS, HEAD_DIM, HEAD_DIM), lambda j: (0, 0, 0, 0))
    st_n = pl.BlockSpec((bsz, N_HEADS, HEAD_DIM), lambda j: (0, 0, 0))
    st_m = pl.BlockSpec((bsz, N_HEADS, 1), lambda j: (0, 0, 0))
    return pl.pallas_call(
        functools.partial(_mlstm_kernel, chunk=chunk, nchunks=nchunks, bsz=bsz),
        grid=grid,
        in_specs=[tok(D_MLSTM), tok(D_MLSTM), tok(D_MLSTM), tok(D_MLSTM), tok(8),
                  pl.BlockSpec((bsz, nchunks, 8, chunk), lambda j: (0, j, 0, 0)),
                  st_c, st_n, st_m,
                  pl.BlockSpec((1, D_MLSTM), lambda j: (0, 0))],
        out_specs=[tok(D_MLSTM), st_c, st_n, st_m],
        out_shape=[jax.ShapeDtypeStruct((bsz, seq, D_MLSTM), BF16),
                   jax.ShapeDtypeStruct((bsz, N_HEADS, HEAD_DIM, HEAD_DIM), F32),
                   jax.ShapeDtypeStruct((bsz, N_HEADS, HEAD_DIM), F32),
                   jax.ShapeDtypeStruct((bsz, N_HEADS, 1), F32)],
        scratch_shapes=[pltpu.VMEM((bsz, N_HEADS, HEAD_DIM, HEAD_DIM), F32),
                        pltpu.VMEM((bsz, N_HEADS, HEAD_DIM), F32),
                        pltpu.VMEM((bsz, N_HEADS, 1), F32)],
        compiler_params=_cparams(("arbitrary",)),
        name="mlstm",
    )(q, k, v, og, gcol, grow, c0, n0, m0, ng)


def _post_kernel(ym_ref, yc_ref, x_ref, wmo_ref, gx_ref, wxq_ref, mk_ref, mv_ref, wxo_ref,
                 gf_ref, wrt_ref, br_ref,
                 h2_ref, xn_ref, eid_ref, gate_ref, rank_ref, cnt_ref, cnt_s, *, tm, sub, group):
    step = pl.program_id(0) * pl.num_programs(1) + pl.program_id(1)

    @pl.when(step % group == 0)
    def _():
        cnt_s[...] = jnp.zeros_like(cnt_s)

    mix = _dot(ym_ref[...], wmo_ref[0:D_MLSTM, :]) + _dot(yc_ref[...], wmo_ref[D_MLSTM:D_MODEL, :])
    h1 = x_ref[...] + mix

    xq = _dot(_rms(h1, gx_ref[...]).astype(BF16), wxq_ref[...]).astype(BF16)
    att = jnp.zeros((tm, D_MODEL), F32)
    for hd in range(N_XHEADS):
        cols = slice(hd * XHEAD_DIM, (hd + 1) * XHEAD_DIM)
        s = _dot_nt(xq[:, cols], mk_ref[:, cols]) * (XHEAD_DIM ** -0.5)
        e = jnp.exp(s - jnp.max(s, axis=-1, keepdims=True))
        p = (e / jnp.sum(e, axis=-1, keepdims=True)).astype(BF16)
        o = _dot(p, mv_ref[:, cols]).astype(BF16)
        att = att + _dot(o, wxo_ref[cols, :])
    h2 = h1 + att
    h2_ref[...] = h2

    xn2 = _rms(h2, gf_ref[...]).astype(BF16)
    xn_ref[...] = xn2

    logits = _dot_nt(wrt_ref[...], xn2) + br_ref[...]
    eidx = lax.broadcasted_iota(I32, logits.shape, 0).astype(F32)
    work = logits
    vals, ids, hots = [], [], []
    for _ in range(TOP_K):
        mx = jnp.max(work, axis=0, keepdims=True)
        idx = jnp.min(jnp.where(work == mx, eidx, float(N_EXPERTS)), axis=0, keepdims=True)
        sel = eidx == idx
        vals.append(mx)
        ids.append(idx)
        hots.append(sel)
        work = jnp.where(sel, -jnp.inf, work)
    exps = [jnp.exp(v - vals[0]) for v in vals]
    denom = exps[0] + exps[1] + exps[2] + exps[3]

    picked = jnp.zeros(logits.shape, F32)
    for sel in hots:
        picked = picked + sel.astype(F32)
    shift = jnp.full((tm, tm), sub.bit_length() - 1, I32)
    tj = lax.broadcasted_iota(I32, (tm, tm), 0)
    tt = lax.broadcasted_iota(I32, (tm, tm), 1)
    same = lax.shift_right_logical(tj, shift) == lax.shift_right_logical(tt, shift)
    before = jnp.where(jnp.logical_and(tj < tt, same), 1.0, 0.0).astype(BF16)
    prior = _dot(picked.astype(BF16), before) + cnt_s[...]
    for kk in range(TOP_K):
        eid_ref[kk:kk + 1, :] = ids[kk].astype(I32)
        gate_ref[kk:kk + 1, :] = exps[kk] / denom
        rank_ref[kk:kk + 1, :] = jnp.sum(jnp.where(hots[kk], prior, 0.0), axis=0,
                                         keepdims=True).astype(I32)
    for s in range(tm // sub):
        total = cnt_s[...] + jnp.sum(picked[:, s * sub:(s + 1) * sub], axis=1, keepdims=True)
        cnt_ref[s] = total.astype(I32)
    cnt_s[...] = total


def _post(ym, yc, x, wmo, gx, wxq, mkb, mvb, wxo, gf, wrt, br, *, tm, sub, group):
    bsz, seq, _ = x.shape
    nj = seq // tm
    grid = (bsz, nj)
    nsub = tm // sub
    n_tiles = bsz * nj * nsub // group
    tok = lambda c: pl.BlockSpec((None, tm, c), lambda b, j: (b, j, 0))
    full = lambda shape: pl.BlockSpec(shape, lambda b, j: (0,) * len(shape))
    mem = pl.BlockSpec((None, N_MEM, D_MODEL), lambda b, j: (b, 0, 0))
    sel = pl.BlockSpec((None, TOP_K, tm), lambda b, j: (b, 0, j))
    return pl.pallas_call(
        functools.partial(_post_kernel, tm=tm, sub=sub, group=group),
        grid=grid,
        in_specs=[tok(D_MLSTM), tok(D_CONV), tok(D_MODEL), full((D_MODEL, D_MODEL)),
                  full((1, D_MODEL)), full((D_MODEL, D_MODEL)), mem, mem,
                  full((D_MODEL, D_MODEL)), full((1, D_MODEL)), full((N_EXPERTS, D_MODEL)),
                  full((N_EXPERTS, 1))],
        out_specs=[tok(D_MODEL), tok(D_MODEL), sel, sel, sel,
                   pl.BlockSpec((nsub, N_EXPERTS, 1), lambda b, j: ((b * nj + j) // group, 0, 0))],
        out_shape=[jax.ShapeDtypeStruct((bsz, seq, D_MODEL), F32),
                   jax.ShapeDtypeStruct((bsz, seq, D_MODEL), BF16),
                   jax.ShapeDtypeStruct((bsz, TOP_K, seq), I32),
                   jax.ShapeDtypeStruct((bsz, TOP_K, seq), F32),
                   jax.ShapeDtypeStruct((bsz, TOP_K, seq), I32),
                   jax.ShapeDtypeStruct((n_tiles, N_EXPERTS, 1), I32)],
        scratch_shapes=[pltpu.VMEM((N_EXPERTS, 1), F32)],
        compiler_params=_cparams(("arbitrary", "arbitrary")),
        name="post",
    )(ym, yc, x, wmo, gx, wxq, mkb, mvb, wxo, gf, wrt, br)


def _sorted_rows(n_tokens):
    return -(-(TOP_K * n_tokens + N_EXPERTS * (SUBLANES - 1)) // ROW_BLOCK) * ROW_BLOCK


def _dispatch_kernel(tab_ref, fill_ref, eid_ref, rank_ref, ls_ref, x_ref, eids_ref, ranks_ref, xs_ref,
                     xb_ref, srt, zero_s, sem, *, n_ptiles, n_blocks):
    i = pl.program_id(0)

    def granule(src, src_row, dst_row):
        return pltpu.make_async_copy(
            src.at[pl.ds(pl.multiple_of(src_row, SUBLANES), SUBLANES), :],
            xb_ref.at[pl.ds(pl.multiple_of(dst_row, SUBLANES), SUBLANES), :], sem)

    def drain(count):
        def body(g, c):
            granule(zero_s, 0, 0).wait()
            return c
        lax.fori_loop(0, count, body, 0)

    def sort_and_move(eid, rank, x):
        ntok = x.shape[0]
        nrows = _sorted_rows(ntok)
        e_iota = lax.broadcasted_iota(I32, (N_EXPERTS, ntok), 0)
        s_iota = lax.broadcasted_iota(I32, (nrows, ntok), 0)
        seg_start = ls_ref[...]
        hit = None
        for kk in range(TOP_K):
            start = jnp.sum(jnp.where(e_iota == eid[kk:kk + 1, :], seg_start, 0.0),
                            axis=0, keepdims=True).astype(I32)
            match = s_iota == start + rank[kk:kk + 1, :]
            hit = match if hit is None else jnp.logical_or(hit, match)
        perm = jnp.where(hit, 1.0, 0.0).astype(BF16)
        srt[0:nrows, :] = _dot(perm, x)
        for e in range(N_EXPERTS):
            dst = tab_ref[e]
            src = tab_ref[2 * N_EXPERTS + e]

            def issue(g, c):
                granule(srt, src + g * SUBLANES, dst + g * SUBLANES).start()
                return c
            lax.fori_loop(0, tab_ref[N_EXPERTS + e], issue, 0)
        drain(tab_ref[3 * N_EXPERTS])

    @pl.when(i < n_ptiles)
    def _():
        sort_and_move(eid_ref[...], rank_ref[...], x_ref[...])

    @pl.when(i == n_ptiles)
    def _():
        sort_and_move(eids_ref[...], ranks_ref[...], xs_ref[...])
        zero_s[...] = jnp.zeros_like(zero_s)
        for e in range(N_EXPERTS):
            dst = fill_ref[e]

            def zissue(g, c):
                granule(zero_s, 0, dst + g * SUBLANES).start()
                return c
            lax.fori_loop(0, fill_ref[N_EXPERTS + e], zissue, 0)
        drain(fill_ref[2 * N_EXPERTS])
        first_free = fill_ref[2 * N_EXPERTS + 1]

        def blk_copy(b):
            return pltpu.make_async_copy(
                zero_s, xb_ref.at[pl.ds(pl.multiple_of(b * ROW_BLOCK, ROW_BLOCK), ROW_BLOCK), :], sem)

        def bissue(b, c):
            blk_copy(b).start()
            return c
        lax.fori_loop(first_free, n_blocks, bissue, 0)

        def bdrain(b, c):
            blk_copy(0).wait()
            return c
        lax.fori_loop(first_free, n_blocks, bdrain, 0)


def _dispatch(tab, fill, eid_p, rank_p, seg_start, xn_p, eid_s, rank_s, xn_s, *, tmd, n_blocks):
    n_ptiles = eid_p.shape[0]
    n_sample = xn_s.shape[0]
    last = n_ptiles - 1
    smem = lambda shape, imap: pl.BlockSpec(shape, imap, memory_space=pltpu.SMEM)
    return pl.pallas_call(
        functools.partial(_dispatch_kernel, n_ptiles=n_ptiles, n_blocks=n_blocks),
        grid=(n_ptiles + 1,),
        in_specs=[smem((TABLE_WIDTH,), lambda i: (i,)),
                  smem((TABLE_WIDTH,), lambda i: (0,)),
                  pl.BlockSpec((None, TOP_K, tmd), lambda i: (jnp.minimum(i, last), 0, 0)),
                  pl.BlockSpec((None, TOP_K, tmd), lambda i: (jnp.minimum(i, last), 0, 0)),
                  pl.BlockSpec((None, N_EXPERTS, 1), lambda i: (i, 0, 0)),
                  pl.BlockSpec((tmd, D_MODEL), lambda i: (jnp.minimum(i, last), 0)),
                  pl.BlockSpec((TOP_K, n_sample), lambda i: (0, 0)),
                  pl.BlockSpec((TOP_K, n_sample), lambda i: (0, 0)),
                  pl.BlockSpec((n_sample, D_MODEL), lambda i: (0, 0))],
        out_specs=pl.BlockSpec(memory_space=pl.ANY),
        out_shape=jax.ShapeDtypeStruct((n_blocks * ROW_BLOCK, D_MODEL), F32),
        scratch_shapes=[pltpu.VMEM((_sorted_rows(tmd), D_MODEL), F32),
                        pltpu.VMEM((ROW_BLOCK, D_MODEL), F32), pltpu.SemaphoreType.DMA(())],
        compiler_params=_cparams(("arbitrary",)),
        name="dispatch",
    )(tab, fill, eid_p, rank_p, seg_start, xn_p, eid_s, rank_s, xn_s)


def _expert_kernel(tab_ref, xb_ref, wu_ref, bu_ref, wd_ref, bd_ref, yb_ref,
                   wu_s, wd_s, xbuf, ybuf, xsem, ysem, *, n_blocks):
    e = pl.program_id(0)
    base = tab_ref[e] * ROW_BLOCK
    nblk = tab_ref[N_EXPERTS + e]
    npair = nblk // 2
    odd = nblk % 2 == 1
    pair_rows = 2 * ROW_BLOCK
    tail_row = base + npair * pair_rows

    def hbm_rows(row0, nrows):
        return pl.ds(pl.multiple_of(row0, ROW_BLOCK), nrows)

    def x_copy(row0, nrows, sl):
        return pltpu.make_async_copy(xb_ref.at[hbm_rows(row0, nrows), :],
                                     xbuf.at[sl, pl.ds(0, nrows), :], xsem.at[sl])

    def y_copy(row0, nrows, sl):
        return pltpu.make_async_copy(ybuf.at[sl, pl.ds(0, nrows), :],
                                     yb_ref.at[hbm_rows(row0, nrows), :], ysem.at[sl])

    def ffn(nrows, sl):
        hcat = _dot(xbuf[sl, 0:nrows, :].astype(BF16), wu_s[...]) + bu_ref[...]
        glu = jnp.minimum(hcat[:, 0:D_FF], SWIGLU_LIMIT)
        lin = jnp.clip(hcat[:, D_FF:2 * D_FF], -SWIGLU_LIMIT, SWIGLU_LIMIT)
        act = glu * _sigmoid(SWIGLU_ALPHA * glu) * (lin + 1.0)
        ybuf[sl, 0:nrows, :] = _dot(act.astype(BF16), wd_s[...]) + bd_ref[...]

    @pl.when(npair > 0)
    def _():
        x_copy(base, pair_rows, 0).start()

    @pl.when(jnp.logical_and(npair == 0, odd))
    def _():
        x_copy(tail_row, ROW_BLOCK, 0).start()

    step = 128

    def cast(r, c):
        rows = pl.ds(pl.multiple_of(r * step, step), step)
        wu_s[rows, :] = wu_ref[rows, :].astype(BF16)
        wd_s[rows, :] = wd_ref[rows, :].astype(BF16)
        return c
    lax.fori_loop(0, D_MODEL // step, cast, 0)

    def pair(j, c):
        sl = j % 2
        x_copy(base, pair_rows, sl).wait()

        @pl.when(j + 1 < npair)
        def _():
            x_copy(base + (j + 1) * pair_rows, pair_rows, 1 - sl).start()

        @pl.when(jnp.logical_and(j + 1 == npair, odd))
        def _():
            x_copy(tail_row, ROW_BLOCK, 1 - sl).start()

        @pl.when(j >= 2)
        def _():
            y_copy(base, pair_rows, sl).wait()

        ffn(pair_rows, sl)
        y_copy(base + j * pair_rows, pair_rows, sl).start()
        return c
    lax.fori_loop(0, npair, pair, 0)

    last = npair % 2

    @pl.when(odd)
    def _():
        x_copy(tail_row, ROW_BLOCK, last).wait()

        @pl.when(npair >= 2)
        def _():
            y_copy(base, pair_rows, last).wait()

        ffn(ROW_BLOCK, last)
        y_copy(tail_row, ROW_BLOCK, last).start()

    @pl.when(npair >= 1)
    def _():
        y_copy(base, pair_rows, 1 - last).wait()

    @pl.when(jnp.logical_and(npair >= 2, jnp.logical_not(odd)))
    def _():
        y_copy(base, pair_rows, last).wait()

    @pl.when(odd)
    def _():
        y_copy(tail_row, ROW_BLOCK, last).wait()

    @pl.when(e == N_EXPERTS - 1)
    def _():
        ybuf[0, 0:ROW_BLOCK, :] = jnp.zeros((ROW_BLOCK, D_MODEL), F32)
        free = tab_ref[2 * N_EXPERTS]

        def tail(b):
            return y_copy(b * ROW_BLOCK, ROW_BLOCK, 0)

        def zissue(b, c):
            tail(b).start()
            return c
        lax.fori_loop(free, n_blocks, zissue, 0)

        def zdrain(b, c):
            tail(0).wait()
            return c
        lax.fori_loop(free, n_blocks, zdrain, 0)


def _experts(tab, xb, w_up, b_up, w_down, b_down, *, n_blocks):
    w_map = lambda e, tab: (e, 0, 0)
    grid_spec = pltpu.PrefetchScalarGridSpec(
        num_scalar_prefetch=1,
        grid=(N_EXPERTS,),
        in_specs=[pl.BlockSpec(memory_space=pl.ANY),
                  pl.BlockSpec((None, D_MODEL, 2 * D_FF), w_map),
                  pl.BlockSpec((None, 1, 2 * D_FF), w_map),
                  pl.BlockSpec((None, D_FF, D_MODEL), w_map),
                  pl.BlockSpec((None, 1, D_MODEL), w_map)],
        out_specs=pl.BlockSpec(memory_space=pl.ANY),
        scratch_shapes=[pltpu.VMEM((D_MODEL, 2 * D_FF), BF16), pltpu.VMEM((D_FF, D_MODEL), BF16),
                        pltpu.VMEM((2, 2 * ROW_BLOCK, D_MODEL), F32),
                        pltpu.VMEM((2, 2 * ROW_BLOCK, D_MODEL), F32),
                        pltpu.SemaphoreType.DMA((2,)), pltpu.SemaphoreType.DMA((2,))],
    )
    return pl.pallas_call(
        functools.partial(_expert_kernel, n_blocks=n_blocks),
        grid_spec=grid_spec,
        out_shape=jax.ShapeDtypeStruct((n_blocks * ROW_BLOCK, D_MODEL), F32),
        compiler_params=_cparams(("arbitrary",)),
        name="experts",
    )(tab, xb, w_up, b_up, w_down, b_down)


def _split_bf16(a):
    hi = a.astype(BF16)
    return hi, (a - hi.astype(F32)).astype(BF16)


def _combine_kernel(tab_ref, nxt_ref, slot_ref, gate_ref, h2_ref, gfin_ref, yb_ref, y_ref, buf, sems,
                    *, tm):
    i = pl.program_id(0)
    n = pl.num_programs(0)
    cur = i % 2
    nrows = buf.shape[1]

    def granule(src_row, sl, dst_row):
        return pltpu.make_async_copy(
            yb_ref.at[pl.ds(pl.multiple_of(src_row, SUBLANES), SUBLANES), :],
            buf.at[sl, pl.ds(pl.multiple_of(dst_row, SUBLANES), SUBLANES), :], sems.at[sl])

    def gather(t_ref, sl):
        for e in range(N_EXPERTS):
            src = t_ref[e]
            dst = t_ref[2 * N_EXPERTS + e]

            def issue(g, c):
                granule(src + g * SUBLANES, sl, dst + g * SUBLANES).start()
                return c
            lax.fori_loop(0, t_ref[N_EXPERTS + e], issue, 0)

    @pl.when(i == 0)
    def _():
        buf[...] = jnp.zeros_like(buf)
        gather(tab_ref, 0)

    @pl.when(i + 1 < n)
    def _():
        gather(nxt_ref, 1 - cur)

    def drain(g, c):
        granule(0, cur, 0).wait()
        return c
    lax.fori_loop(0, tab_ref[3 * N_EXPERTS], drain, 0)

    s_iota = lax.broadcasted_iota(I32, (tm, nrows), 1)
    wgt = jnp.zeros((tm, nrows), F32)
    for kk in range(TOP_K):
        wgt = wgt + jnp.where(s_iota == slot_ref[:, kk:kk + 1], gate_ref[:, kk:kk + 1], 0.0)
    w_hi, w_lo = _split_bf16(wgt)
    y_hi, y_lo = _split_bf16(buf[cur])
    acc = _dot(w_hi, y_hi) + (_dot(w_hi, y_lo) + _dot(w_lo, y_hi))
    y_ref[...] = _rms(h2_ref[...] + acc, gfin_ref[...])


def _combine(tab, slot_col, gate_col, h2, gfin, yb, *, tm):
    ntok = h2.shape[0]
    n = ntok // tm
    smem = lambda imap: pl.BlockSpec((TABLE_WIDTH,), imap, memory_space=pltpu.SMEM)
    return pl.pallas_call(
        functools.partial(_combine_kernel, tm=tm),
        grid=(n,),
        in_specs=[smem(lambda i: (i,)), smem(lambda i: (jnp.minimum(i + 1, n - 1),)),
                  pl.BlockSpec((tm, TOP_K), lambda i: (i, 0)),
                  pl.BlockSpec((tm, TOP_K), lambda i: (i, 0)),
                  pl.BlockSpec((tm, D_MODEL), lambda i: (i, 0)),
                  pl.BlockSpec((1, D_MODEL), lambda i: (0, 0)),
                  pl.BlockSpec(memory_space=pl.ANY)],
        out_specs=pl.BlockSpec((tm, D_MODEL), lambda i: (i, 0)),
        out_shape=jax.ShapeDtypeStruct((ntok, D_MODEL), F32),
        scratch_shapes=[pltpu.VMEM((2, _sorted_rows(tm), D_MODEL), F32),
                        pltpu.SemaphoreType.DMA((2,))],
        compiler_params=_cparams(("arbitrary",)),
        name="combine",
    )(tab, tab, slot_col, gate_col, h2, gfin, yb)


def _tiles(a, tile):
    bsz, kk, seq = a.shape
    return a.reshape(bsz, kk, seq // tile, tile).transpose(0, 2, 1, 3).reshape(-1, kk, tile)


def _path(x, cbuf, c0, n0, m0, mkb, mvb, wts, *, tm_in, ct, chunk, tm_post, sub, group):
    q, k, v, og, gcol, grow, yc, nbuf = _inproj(
        x, wts["g_mix"], wts["wq"], wts["wg"], wts["wgt"], wts["wc"], wts["bg"], wts["bgt"],
        wts["cw"], cbuf, tm=tm_in, chunk=chunk)
    ym, c1, n1, m1 = _mlstm(q, k, v, og, gcol, grow, c0, n0, m0, wts["ng"], ct=ct, chunk=chunk)
    h2, xn, eid, gate, rank, cnt = _post(
        ym, yc, x, wts["wmo"], wts["g_x"], wts["wxq"], mkb, mvb, wts["wxo"], wts["g_ffn"],
        wts["wrt"], wts["br"], tm=tm_post, sub=sub, group=group)
    return dict(h2=h2, xn=xn, eid=eid, gate=gate, rank=rank, cnt=cnt[:, :, 0],
                c1=c1, n1=n1, m1=m1[..., 0], nbuf=nbuf)


def kernel(x_prompt, x_sample, state_mlstm_c, state_mlstm_n, state_mlstm_m, state_conv, cache_mem_k, cache_mem_v, mem_prompt, norm_mix_g, w_in, b_gate, mlstm_norm_g, conv_w, w_mix_out, norm_x_g, norm_mem_g, w_xq, w_xk, w_xv, w_xo, norm_ffn_g, w_router, b_router, w_up, b_up, w_down, b_down, norm_final_g):
    bp, lp, _ = x_prompt.shape
    bs, ls, _ = x_sample.shape
    l = 0
    row = lambda a: a.reshape(1, -1)

    wi = w_in[l]
    gate_cols = wi[:, 4 * D_MLSTM:4 * D_MLSTM + 2 * N_HEADS]
    wts = dict(
        g_mix=row(norm_mix_g[l]),
        wq=wi[:, 0:4 * D_MLSTM].astype(BF16),
        wg=jnp.pad(gate_cols, ((0, 0), (0, 128 - 2 * N_HEADS))).astype(BF16),
        wgt=gate_cols.T.astype(BF16),
        wc=wi[:, 4 * D_MLSTM + 2 * N_HEADS:].astype(BF16),
        bg=row(b_gate[l]), bgt=b_gate[l].reshape(-1, 1),
        cw=conv_w[l], ng=row(mlstm_norm_g[l]),
        wmo=w_mix_out[l].astype(BF16), g_x=row(norm_x_g[l]), wxq=w_xq[l].astype(BF16),
        wxo=w_xo[l].astype(BF16), g_ffn=row(norm_ffn_g[l]),
        wrt=w_router[l].T.astype(BF16), br=b_router[l].reshape(-1, 1),
    )

    mk, mv, mkb, mvb = _memkv(mem_prompt.reshape(bp * N_MEM, D_MODEL), row(norm_mem_g[l]),
                              w_xk[l].astype(BF16), w_xv[l].astype(BF16))
    zeros = lambda *s: jnp.zeros(s, F32)
    tm_post, tmd = 512, 256
    pr = _path(x_prompt, zeros(bp, CONV_WIDTH - 1, D_CONV), zeros(bp, N_HEADS, HEAD_DIM, HEAD_DIM),
               zeros(bp, N_HEADS, HEAD_DIM), zeros(bp, N_HEADS, 1),
               mkb.reshape(bp, N_MEM, D_MODEL), mvb.reshape(bp, N_MEM, D_MODEL), wts,
               tm_in=512, ct=512, chunk=CHUNK, tm_post=tm_post, sub=tmd, group=1)
    sa = _path(x_sample, state_conv[l], state_mlstm_c[l], state_mlstm_n[l],
               state_mlstm_m[l][..., None],
               cache_mem_k[l].reshape(bs, N_MEM, D_MODEL).astype(BF16),
               cache_mem_v[l].reshape(bs, N_MEM, D_MODEL).astype(BF16), wts,
               tm_in=ls, ct=ls, chunk=min(CHUNK, ls), tm_post=ls, sub=ls, group=bs)

    n_ptok, n_stok = bp * lp, bs * ls
    cnt = jnp.concatenate([pr["cnt"], sa["cnt"]], axis=0)
    n_tiles = cnt.shape[0]
    n_blocks = -(-(TOP_K * (n_ptok + n_stok) + n_tiles * N_EXPERTS * (SUBLANES - 1)) // ROW_BLOCK) \
        + N_EXPERTS
    seg = (cnt + SUBLANES - 1) // SUBLANES * SUBLANES
    seg_src = jnp.cumsum(seg, axis=1) - seg
    tot = jnp.sum(seg, axis=0)
    blocks_e = (tot + ROW_BLOCK - 1) // ROW_BLOCK
    padded = blocks_e * ROW_BLOCK
    pstart = jnp.cumsum(padded) - padded
    bend = jnp.cumsum(blocks_e)
    n_used = bend[-1]
    exp_tab = jnp.concatenate([bend - blocks_e, blocks_e, n_used[None]]).astype(I32)
    seg_dst = pstart[None, :] + jnp.cumsum(seg, axis=0) - seg

    gran = seg // SUBLANES
    tab = jnp.concatenate([seg_dst, gran, seg_src, jnp.sum(gran, axis=1, keepdims=True)], axis=1)
    tab = jnp.pad(tab, ((0, 0), (0, TABLE_WIDTH - tab.shape[1]))).astype(I32).reshape(-1)
    fgran = (padded - tot) // SUBLANES
    fill = jnp.concatenate([pstart + tot, fgran, jnp.sum(fgran)[None], n_used[None]])
    fill = jnp.pad(fill, (0, TABLE_WIDTH - fill.shape[0])).astype(I32)

    def lookup(table, eid):
        hit = eid[..., None] == jnp.arange(N_EXPERTS, dtype=I32)
        return jnp.sum(jnp.where(hit, table[:, None, None, :], 0), axis=-1)

    eid_pt, rank_pt = _tiles(pr["eid"], tmd), _tiles(pr["rank"], tmd)
    eid_st = sa["eid"].transpose(1, 0, 2).reshape(1, TOP_K, n_stok)
    rank_st = sa["rank"].transpose(1, 0, 2).reshape(1, TOP_K, n_stok)
    slot_p = (lookup(seg_src[:-1], eid_pt) + rank_pt).transpose(0, 2, 1).reshape(n_ptok, TOP_K)
    slot_s = (lookup(seg_src[-1:], eid_st) + rank_st).transpose(0, 2, 1).reshape(n_stok, TOP_K)

    xb = _dispatch(tab, fill, eid_pt, rank_pt, seg_src.astype(F32)[..., None],
                   pr["xn"].reshape(n_ptok, D_MODEL), eid_st[0], rank_st[0],
                   sa["xn"].reshape(n_stok, D_MODEL), tmd=tmd, n_blocks=n_blocks)
    yb = _experts(exp_tab, xb, w_up[l], b_up[l][:, None, :], w_down[l], b_down[l][:, None, :],
                  n_blocks=n_blocks)

    gfin = row(norm_final_g)
    split = (n_tiles - 1) * TABLE_WIDTH
    y_p = _combine(tab[:split], slot_p.astype(I32), pr["gate"].transpose(0, 2, 1).reshape(n_ptok, TOP_K),
                   pr["h2"].reshape(n_ptok, D_MODEL), gfin, yb, tm=tmd)
    y_s = _combine(tab[split:], slot_s.astype(I32), sa["gate"].transpose(0, 2, 1).reshape(n_stok, TOP_K),
                   sa["h2"].reshape(n_stok, D_MODEL), gfin, yb, tm=n_stok)

    lead = lambda a: a[None]
    return (y_p.reshape(bp, lp, D_MODEL), y_s.reshape(bs, ls, D_MODEL),
            lead(pr["c1"]), lead(pr["n1"]), lead(pr["m1"]), lead(pr["nbuf"]),
            lead(mk.reshape(bp, N_MEM, N_XHEADS, XHEAD_DIM)),
            lead(mv.reshape(bp, N_MEM, N_XHEADS, XHEAD_DIM)),
            lead(sa["c1"]), lead(sa["n1"]), lead(sa["m1"]), lead(sa["nbuf"]))
```

```python
import functools

import jax
import jax.numpy as jnp
import numpy as np
from jax import lax
from jax.experimental import pallas as pl
from jax.experimental.pallas import tpu as pltpu

F32 = jnp.float32
BF16 = jnp.bfloat16
I32 = jnp.int32

D_MODEL = 1024
N_HEADS = 4
HEAD_DIM = 128
D_MLSTM = N_HEADS * HEAD_DIM
D_CONV = D_MODEL - D_MLSTM
CONV_WIDTH = 3
CHUNK = 64
N_MEM = 256
N_XHEADS = 4
XHEAD_DIM = D_MODEL // N_XHEADS
N_EXPERTS = 32
TOP_K = 4
D_FF = D_MODEL
SWIGLU_LIMIT = 7.0
SWIGLU_ALPHA = 1.702
EPS = 1e-5

SUBLANES = 8
TABLE_WIDTH = 128
ROW_BLOCK = 256
ROW_DMA_PRIORITY = 1
VMEM_LIMIT = 56 * 1024 * 1024


def _cparams(sem):
    return pltpu.CompilerParams(dimension_semantics=sem, vmem_limit_bytes=VMEM_LIMIT)


def _rms(x, g):
    return x * lax.rsqrt(jnp.mean(x * x, axis=-1, keepdims=True) + EPS) * g


def _log_sigmoid(x):
    return -(jnp.maximum(-x, 0.0) + jnp.log1p(jnp.exp(-jnp.abs(x))))


def _sigmoid(x):
    return 1.0 / (1.0 + jnp.exp(-x))


def _dot(a, b):
    return jnp.dot(a, b, preferred_element_type=F32)


def _dot_nt(a, b):
    return lax.dot_general(a, b, (((1,), (1,)), ((), ())), preferred_element_type=F32)


def _dot_tn(a, b):
    return lax.dot_general(a, b, (((0,), (0,)), ((), ())), preferred_element_type=F32)


def _memkv_kernel(mem_ref, g_ref, wk_ref, wv_ref, mk_ref, mv_ref, mkb_ref, mvb_ref):
    mn = _rms(mem_ref[...], g_ref[...]).astype(BF16)
    mk = _dot(mn, wk_ref[...])
    mv = _dot(mn, wv_ref[...])
    mk_ref[...] = mk
    mv_ref[...] = mv
    mkb_ref[...] = mk.astype(BF16)
    mvb_ref[...] = mv.astype(BF16)


def _memkv(mem2d, g, wk, wv):
    rows = mem2d.shape[0]
    tm = N_MEM
    row_spec = pl.BlockSpec((tm, D_MODEL), lambda i: (i, 0))
    full = lambda shape: pl.BlockSpec(shape, lambda i: (0,) * len(shape))
    return pl.pallas_call(
        _memkv_kernel,
        grid=(rows // tm,),
        in_specs=[row_spec, full((1, D_MODEL)), full((D_MODEL, D_MODEL)), full((D_MODEL, D_MODEL))],
        out_specs=[row_spec, row_spec, row_spec, row_spec],
        out_shape=[jax.ShapeDtypeStruct((rows, D_MODEL), F32)] * 2
        + [jax.ShapeDtypeStruct((rows, D_MODEL), BF16)] * 2,
        compiler_params=_cparams(("arbitrary",)),
        name="memkv",
    )(mem2d, g, wk, wv)


def _inproj_kernel(x_ref, g_ref, wq_ref, wg_ref, wgt_ref, wc_ref, bg_ref, bgt_ref, cw_ref, cbuf_ref,
                   q_ref, k_ref, v_ref, og_ref, gcol_ref, grow_ref, yc_ref, nbuf_ref,
                   carry_ref, *, tm, chunk):
    j = pl.program_id(1)

    @pl.when(j == 0)
    def _():
        carry_ref[0:2, :] = cbuf_ref[...]

    xb = _rms(x_ref[...], g_ref[...]).astype(BF16)

    p = _dot(xb, wq_ref[...])
    q_ref[...] = p[:, 0:D_MLSTM].astype(BF16)
    k_ref[...] = (p[:, D_MLSTM:2 * D_MLSTM] * (HEAD_DIM ** -0.5)).astype(BF16)
    v_ref[...] = p[:, 2 * D_MLSTM:3 * D_MLSTM].astype(BF16)
    og_ref[...] = _sigmoid(p[:, 3 * D_MLSTM:4 * D_MLSTM])

    gc = _dot(xb, wg_ref[...])[:, 0:2 * N_HEADS] + bg_ref[...]
    col = lax.broadcasted_iota(I32, gc.shape, 1)
    gcol_ref[...] = jnp.where(col < N_HEADS, gc, _log_sigmoid(gc))
    gr = _dot_nt(wgt_ref[...], xb) + bgt_ref[...]
    row = lax.broadcasted_iota(I32, gr.shape, 0)
    gr = jnp.where(row < N_HEADS, gr, _log_sigmoid(gr))
    for c in range(tm // chunk):
        grow_ref[c] = gr[:, c * chunk:(c + 1) * chunk]

    pc = _dot(xb, wc_ref[...])
    cb = pc[:, 0:D_CONV]
    u = pc[:, D_CONV:2 * D_CONV] * pc[:, 2 * D_CONV:3 * D_CONV]
    c0 = carry_ref[0:1, :]
    c1 = carry_ref[1:2, :]
    rid = lax.broadcasted_iota(I32, u.shape, 0)
    um1 = jnp.where(rid == 0, c1, pltpu.roll(u, 1, 0))
    um2 = jnp.where(rid == 0, c0, jnp.where(rid == 1, c1, pltpu.roll(u, 2, 0)))
    yc = cw_ref[0:1, :] * um2 + cw_ref[1:2, :] * um1 + cw_ref[2:3, :] * u
    yc_ref[...] = (cb * yc).astype(BF16)
    tail = u[tm - 2:tm, :]
    carry_ref[0:2, :] = tail
    nbuf_ref[...] = tail


def _inproj(x, g, wq, wg, wgt, wc, bg, bgt, cw, cbuf, *, tm, chunk):
    bsz, seq, _ = x.shape
    grid = (bsz, seq // tm)
    tok = lambda c: pl.BlockSpec((None, tm, c), lambda b, j: (b, j, 0))
    full = lambda shape: pl.BlockSpec(shape, lambda b, j: (0,) * len(shape))
    nck = tm // chunk
    return pl.pallas_call(
        functools.partial(_inproj_kernel, tm=tm, chunk=chunk),
        grid=grid,
        in_specs=[tok(D_MODEL), full((1, D_MODEL)), full((D_MODEL, 4 * D_MLSTM)),
                  full((D_MODEL, 128)), full((8, D_MODEL)), full((D_MODEL, 3 * D_CONV)),
                  full((1, 8)), full((8, 1)), full((CONV_WIDTH, D_CONV)),
                  pl.BlockSpec((None, 2, D_CONV), lambda b, j: (b, 0, 0))],
        out_specs=[tok(D_MLSTM), tok(D_MLSTM), tok(D_MLSTM), tok(D_MLSTM), tok(8),
                   pl.BlockSpec((None, nck, 8, chunk), lambda b, j: (b, j, 0, 0)),
                   tok(D_CONV),
                   pl.BlockSpec((None, 2, D_CONV), lambda b, j: (b, 0, 0))],
        out_shape=[jax.ShapeDtypeStruct((bsz, seq, D_MLSTM), BF16)] * 3
        + [jax.ShapeDtypeStruct((bsz, seq, D_MLSTM), F32),
           jax.ShapeDtypeStruct((bsz, seq, 8), F32),
           jax.ShapeDtypeStruct((bsz, seq // chunk, 8, chunk), F32),
           jax.ShapeDtypeStruct((bsz, seq, D_CONV), BF16),
           jax.ShapeDtypeStruct((bsz, 2, D_CONV), F32)],
        scratch_shapes=[pltpu.VMEM((8, D_CONV), F32)],
        compiler_params=_cparams(("arbitrary", "arbitrary")),
        name="inproj",
    )(x, g, wq, wg, wgt, wc, bg, bgt, cw, cbuf)


def _mlstm_kernel(q_ref, k_ref, v_ref, og_ref, gc_ref, gr_ref, c0_ref, n0_ref, m0_ref, ng_ref,
                  ym_ref, c1_ref, n1_ref, m1_ref, c_s, n_s, m_s, *, chunk, nchunks, bsz):
    j = pl.program_id(0)

    @pl.when(j == 0)
    def _():
        c_s[...] = c0_ref[...]
        n_s[...] = n0_ref[...]
        m_s[...] = m0_ref[...]

    ti = lax.broadcasted_iota(I32, (chunk, chunk), 0)
    ji = lax.broadcasted_iota(I32, (chunk, chunk), 1)
    causal = ji <= ti

    def body(ci, carry):
        r0 = pl.multiple_of(ci * chunk, chunk)
        rows = pl.ds(r0, chunk)
        chains = [(b, h) for b in range(bsz) for h in range(N_HEADS)]
        cols = lambda h: slice(h * HEAD_DIM, (h + 1) * HEAD_DIM)
        each = lambda f: [f(n, b, h) for n, (b, h) in enumerate(chains)]
        q = lambda b, h: q_ref[b, rows, cols(h)]
        k = lambda b, h: k_ref[b, rows, cols(h)]
        v = lambda b, h: v_ref[b, rows, cols(h)]
        gcs = [gc_ref[b, rows, :] for b in range(bsz)]
        grs = [gr_ref[b, ci] for b in range(bsz)]
        li_c = each(lambda n, b, h: gcs[b][:, h:h + 1])
        lf_c = each(lambda n, b, h: gcs[b][:, N_HEADS + h:N_HEADS + h + 1])
        li_r = each(lambda n, b, h: grs[b][h:h + 1, :])
        lf_r = each(lambda n, b, h: grs[b][N_HEADS + h:N_HEADS + h + 1, :])
        m_prev = each(lambda n, b, h: m_s[b, h:h + 1, :])

        b_c = each(lambda n, b, h: jnp.sum(jnp.where(causal, lf_r[n], 0.0), axis=1, keepdims=True))
        b_r = each(lambda n, b, h: jnp.sum(jnp.where(ti <= ji, lf_c[n], 0.0), axis=0, keepdims=True))
        dmat = each(lambda n, b, h: jnp.where(causal, b_c[n] - b_r[n] + li_r[n], -jnp.inf))
        dmax = each(lambda n, b, h: jnp.max(dmat[n], axis=1, keepdims=True))
        inter = each(lambda n, b, h: b_c[n] + m_prev[n])
        m_t = each(lambda n, b, h: jnp.maximum(inter[n], dmax[n]))
        w_inter = each(lambda n, b, h: jnp.exp(inter[n] - m_t[n]))
        s = each(lambda n, b, h: _dot_nt(q(b, h), k(b, h)) * jnp.exp(dmat[n] - m_t[n]))
        qc = each(lambda n, b, h: _dot(q(b, h), c_s[b, h].astype(BF16)))
        sv = each(lambda n, b, h: _dot(s[n].astype(BF16), v(b, h)))
        qn = each(lambda n, b, h: jnp.sum(q(b, h).astype(F32) * n_s[b, h:h + 1, :], axis=1,
                                          keepdims=True))
        den = each(lambda n, b, h: w_inter[n] * qn[n] + jnp.sum(s[n], axis=1, keepdims=True))
        hh = each(lambda n, b, h: (w_inter[n] * qc[n] + sv[n])
                  / jnp.maximum(jnp.abs(den[n]), jnp.exp(-m_t[n])))

        m_new = each(lambda n, b, h: m_t[n][chunk - 1:chunk, :])
        b_last = each(lambda n, b, h: b_c[n][chunk - 1:chunk, :])
        decay = each(lambda n, b, h: jnp.exp(b_last[n] + m_prev[n] - m_new[n]))
        kw = each(lambda n, b, h: k(b, h).astype(F32)
                  * jnp.exp(b_last[n] - b_c[n] + li_c[n] - m_new[n]))
        kv = each(lambda n, b, h: _dot_tn(kw[n].astype(BF16), v(b, h)))
        for n, (b, h) in enumerate(chains):
            c_s[b, h] = decay[n] * c_s[b, h] + kv[n]
            n_s[b, h:h + 1, :] = decay[n] * n_s[b, h:h + 1, :] + jnp.sum(kw[n], axis=0, keepdims=True)
            m_s[b, h:h + 1, :] = m_new[n]

        hn = each(lambda n, b, h: hh[n] * lax.rsqrt(jnp.mean(hh[n] * hh[n], axis=1, keepdims=True) + EPS)
                  * ng_ref[:, cols(h)])
        for n, (b, h) in enumerate(chains):
            ym_ref[b, rows, cols(h)] = (hn[n] * og_ref[b, rows, cols(h)]).astype(BF16)
        return carry

    lax.fori_loop(0, nchunks, body, 0)

    @pl.when(j == pl.num_programs(0) - 1)
    def _():
        c1_ref[...] = c_s[...]
        n1_ref[...] = n_s[...]
        m1_ref[...] = m_s[...]


def _mlstm(q, k, v, og, gcol, grow, c0, n0, m0, ng, *, ct, chunk):
    bsz, seq, _ = q.shape
    nchunks = ct // chunk
    grid = (seq // ct,)
    tok = lambda c: pl.BlockSpec((bsz, ct, c), lambda j: (0, j, 0))
    st_c = pl.BlockSpec((bsz, N_HEADS, HEAD_DIM, HEAD_DIM), lambda j: (0, 0, 0, 0))
    st_n = pl.BlockSpec((bsz, N_HEADS, HEAD_DIM), lambda j: (0, 0, 0))
    st_m = pl.BlockSpec((bsz, N_HEADS, 1), lambda j: (0, 0, 0))
    return pl.pallas_call(
        functools.partial(_mlstm_kernel, chunk=chunk, nchunks=nchunks, bsz=bsz),
        grid=grid,
        in_specs=[tok(D_MLSTM), tok(D_MLSTM), tok(D_MLSTM), tok(D_MLSTM), tok(8),
                  pl.BlockSpec((bsz, nchunks, 8, chunk), lambda j: (0, j, 0, 0)),
                  st_c, st_n, st_m,
                  pl.BlockSpec((1, D_MLSTM), lambda j: (0, 0))],
        out_specs=[tok(D_MLSTM), st_c, st_n, st_m],
        out_shape=[jax.ShapeDtypeStruct((bsz, seq, D_MLSTM), BF16),
                   jax.ShapeDtypeStruct((bsz, N_HEADS, HEAD_DIM, HEAD_DIM), F32),
                   jax.ShapeDtypeStruct((bsz, N_HEADS, HEAD_DIM), F32),
                   jax.ShapeDtypeStruct((bsz, N_HEADS, 1), F32)],
        scratch_shapes=[pltpu.VMEM((bsz, N_HEADS, HEAD_DIM, HEAD_DIM), F32),
                        pltpu.VMEM((bsz, N_HEADS, HEAD_DIM), F32),
                        pltpu.VMEM((bsz, N_HEADS, 1), F32)],
        compiler_params=_cparams(("arbitrary",)),
        name="mlstm",
    )(q, k, v, og, gcol, grow, c0, n0, m0, ng)


def _post_kernel(ym_ref, yc_ref, x_ref, wmo_ref, gx_ref, wxq_ref, mk_ref, mv_ref, wxo_ref,
                 gf_ref, wrt_ref, br_ref,
                 h2_ref, xn_ref, eid_ref, gate_ref, rank_ref, cnt_ref, cnt_s, *, tm, sub, group):
    step = pl.program_id(0) * pl.num_programs(1) + pl.program_id(1)

    @pl.when(step % group == 0)
    def _():
        cnt_s[...] = jnp.zeros_like(cnt_s)

    mix = _dot(ym_ref[...], wmo_ref[0:D_MLSTM, :]) + _dot(yc_ref[...], wmo_ref[D_MLSTM:D_MODEL, :])
    h1 = x_ref[...] + mix

    xq = _dot(_rms(h1, gx_ref[...]).astype(BF16), wxq_ref[...]).astype(BF16)
    att = jnp.zeros((tm, D_MODEL), F32)
    for hd in range(N_XHEADS):
        cols = slice(hd * XHEAD_DIM, (hd + 1) * XHEAD_DIM)
        s = _dot_nt(xq[:, cols], mk_ref[:, cols]) * (XHEAD_DIM ** -0.5)
        e = jnp.exp(s - jnp.max(s, axis=-1, keepdims=True))
        p = (e / jnp.sum(e, axis=-1, keepdims=True)).astype(BF16)
        o = _dot(p, mv_ref[:, cols]).astype(BF16)
        att = att + _dot(o, wxo_ref[cols, :])
    h2 = h1 + att
    h2_ref[...] = h2

    xn2 = _rms(h2, gf_ref[...]).astype(BF16)
    xn_ref[...] = xn2

    logits = _dot_nt(wrt_ref[...], xn2) + br_ref[...]
    eidx = lax.broadcasted_iota(I32, logits.shape, 0).astype(F32)
    work = logits
    vals, ids, hots = [], [], []
    for _ in range(TOP_K):
        mx = jnp.max(work, axis=0, keepdims=True)
        idx = jnp.min(jnp.where(work == mx, eidx, float(N_EXPERTS)), axis=0, keepdims=True)
        sel = eidx == idx
        vals.append(mx)
        ids.append(idx)
        hots.append(sel)
        work = jnp.where(sel, -jnp.inf, work)
    exps = [jnp.exp(v - vals[0]) for v in vals]
    denom = exps[0] + exps[1] + exps[2] + exps[3]

    picked = jnp.zeros(logits.shape, F32)
    for sel in hots:
        picked = picked + sel.astype(F32)
    shift = jnp.full((tm, tm), sub.bit_length() - 1, I32)
    tj = lax.broadcasted_iota(I32, (tm, tm), 0)
    tt = lax.broadcasted_iota(I32, (tm, tm), 1)
    same = lax.shift_right_logical(tj, shift) == lax.shift_right_logical(tt, shift)
    before = jnp.where(jnp.logical_and(tj < tt, same), 1.0, 0.0).astype(BF16)
    prior = _dot(picked.astype(BF16), before) + cnt_s[...]
    for kk in range(TOP_K):
        eid_ref[kk:kk + 1, :] = ids[kk].astype(I32)
        gate_ref[kk:kk + 1, :] = exps[kk] / denom
        rank_ref[kk:kk + 1, :] = jnp.sum(jnp.where(hots[kk], prior, 0.0), axis=0,
                                         keepdims=True).astype(I32)
    for s in range(tm // sub):
        total = cnt_s[...] + jnp.sum(picked[:, s * sub:(s + 1) * sub], axis=1, keepdims=True)
        cnt_ref[s] = total.astype(I32)
    cnt_s[...] = total


def _post(ym, yc, x, wmo, gx, wxq, mkb, mvb, wxo, gf, wrt, br, *, tm, sub, group):
    bsz, seq, _ = x.shape
    nj = seq // tm
    grid = (bsz, nj)
    nsub = tm // sub
    n_tiles = bsz * nj * nsub // group
    tok = lambda c: pl.BlockSpec((None, tm, c), lambda b, j: (b, j, 0))
    full = lambda shape: pl.BlockSpec(shape, lambda b, j: (0,) * len(shape))
    mem = pl.BlockSpec((None, N_MEM, D_MODEL), lambda b, j: (b, 0, 0))
    sel = pl.BlockSpec((None, TOP_K, tm), lambda b, j: (b, 0, j))
    return pl.pallas_call(
        functools.partial(_post_kernel, tm=tm, sub=sub, group=group),
        grid=grid,
        in_specs=[tok(D_MLSTM), tok(D_CONV), tok(D_MODEL), full((D_MODEL, D_MODEL)),
                  full((1, D_MODEL)), full((D_MODEL, D_MODEL)), mem, mem,
                  full((D_MODEL, D_MODEL)), full((1, D_MODEL)), full((N_EXPERTS, D_MODEL)),
                  full((N_EXPERTS, 1))],
        out_specs=[tok(D_MODEL), tok(D_MODEL), sel, sel, sel,
                   pl.BlockSpec((nsub, N_EXPERTS, 1), lambda b, j: ((b * nj + j) // group, 0, 0))],
        out_shape=[jax.ShapeDtypeStruct((bsz, seq, D_MODEL), F32),
                   jax.ShapeDtypeStruct((bsz, seq, D_MODEL), BF16),
                   jax.ShapeDtypeStruct((bsz, TOP_K, seq), I32),
                   jax.ShapeDtypeStruct((bsz, TOP_K, seq), F32),
                   jax.ShapeDtypeStruct((bsz, TOP_K, seq), I32),
                   jax.ShapeDtypeStruct((n_tiles, N_EXPERTS, 1), I32)],
        scratch_shapes=[pltpu.VMEM((N_EXPERTS, 1), F32)],
        compiler_params=_cparams(("arbitrary", "arbitrary")),
        name="post",
    )(ym, yc, x, wmo, gx, wxq, mkb, mvb, wxo, gf, wrt, br)


def _sorted_rows(n_tokens):
    return -(-(TOP_K * n_tokens + N_EXPERTS * (SUBLANES - 1)) // ROW_BLOCK) * ROW_BLOCK


def _dispatch_kernel(tab_ref, fill_ref, eid_ref, rank_ref, ls_ref, x_ref, eids_ref, ranks_ref, xs_ref,
                     xb_ref, srt, zero_s, sem, *, n_ptiles, n_blocks):
    i = pl.program_id(0)

    def granule(src, src_row, dst_row):
        return pltpu.make_async_copy(
            src.at[pl.ds(pl.multiple_of(src_row, SUBLANES), SUBLANES), :],
            xb_ref.at[pl.ds(pl.multiple_of(dst_row, SUBLANES), SUBLANES), :], sem)

    def drain(count):
        def body(g, c):
            granule(zero_s, 0, 0).wait()
            return c
        lax.fori_loop(0, count, body, 0)

    def sort_and_move(eid, rank, x):
        ntok = x.shape[0]
        nrows = _sorted_rows(ntok)
        e_iota = lax.broadcasted_iota(I32, (N_EXPERTS, ntok), 0)
        s_iota = lax.broadcasted_iota(I32, (nrows, ntok), 0)
        seg_start = ls_ref[...]
        hit = None
        for kk in range(TOP_K):
            start = jnp.sum(jnp.where(e_iota == eid[kk:kk + 1, :], seg_start, 0.0),
                            axis=0, keepdims=True).astype(I32)
            match = s_iota == start + rank[kk:kk + 1, :]
            hit = match if hit is None else jnp.logical_or(hit, match)
        perm = jnp.where(hit, 1.0, 0.0).astype(BF16)
        srt[0:nrows, :] = _dot(perm, x)
        for e in range(N_EXPERTS):
            dst = tab_ref[e]
            src = tab_ref[2 * N_EXPERTS + e]

            def issue(g, c):
                granule(srt, src + g * SUBLANES, dst + g * SUBLANES).start()
                return c
            lax.fori_loop(0, tab_ref[N_EXPERTS + e], issue, 0)
        drain(tab_ref[3 * N_EXPERTS])

    @pl.when(i < n_ptiles)
    def _():
        sort_and_move(eid_ref[...], rank_ref[...], x_ref[...])

    @pl.when(i == n_ptiles)
    def _():
        sort_and_move(eids_ref[...], ranks_ref[...], xs_ref[...])
        zero_s[...] = jnp.zeros_like(zero_s)
        for e in range(N_EXPERTS):
            dst = fill_ref[e]

            def zissue(g, c):
                granule(zero_s, 0, dst + g * SUBLANES).start()
                return c
            lax.fori_loop(0, fill_ref[N_EXPERTS + e], zissue, 0)
        drain(fill_ref[2 * N_EXPERTS])
        first_free = fill_ref[2 * N_EXPERTS + 1]

        def blk_copy(b):
            return pltpu.make_async_copy(
                zero_s, xb_ref.at[pl.ds(pl.multiple_of(b * ROW_BLOCK, ROW_BLOCK), ROW_BLOCK), :], sem)

        def bissue(b, c):
            blk_copy(b).start()
            return c
        lax.fori_loop(first_free, n_blocks, bissue, 0)

        def bdrain(b, c):
            blk_copy(0).wait()
            return c
        lax.fori_loop(first_free, n_blocks, bdrain, 0)


def _dispatch(tab, fill, eid_p, rank_p, seg_start, xn_p, eid_s, rank_s, xn_s, *, tmd, n_blocks):
    n_ptiles = eid_p.shape[0]
    n_sample = xn_s.shape[0]
    last = n_ptiles - 1
    smem = lambda shape, imap: pl.BlockSpec(shape, imap, memory_space=pltpu.SMEM)
    return pl.pallas_call(
        functools.partial(_dispatch_kernel, n_ptiles=n_ptiles, n_blocks=n_blocks),
        grid=(n_ptiles + 1,),
        in_specs=[smem((TABLE_WIDTH,), lambda i: (i,)),
                  smem((TABLE_WIDTH,), lambda i: (0,)),
                  pl.BlockSpec((None, TOP_K, tmd), lambda i: (jnp.minimum(i, last), 0, 0)),
                  pl.BlockSpec((None, TOP_K, tmd), lambda i: (jnp.minimum(i, last), 0, 0)),
                  pl.BlockSpec((None, N_EXPERTS, 1), lambda i: (i, 0, 0)),
                  pl.BlockSpec((tmd, D_MODEL), lambda i: (jnp.minimum(i, last), 0)),
                  pl.BlockSpec((TOP_K, n_sample), lambda i: (0, 0)),
                  pl.BlockSpec((TOP_K, n_sample), lambda i: (0, 0)),
                  pl.BlockSpec((n_sample, D_MODEL), lambda i: (0, 0))],
        out_specs=pl.BlockSpec(memory_space=pl.ANY),
        out_shape=jax.ShapeDtypeStruct((n_blocks * ROW_BLOCK, D_MODEL), F32),
        scratch_shapes=[pltpu.VMEM((_sorted_rows(tmd), D_MODEL), F32),
                        pltpu.VMEM((ROW_BLOCK, D_MODEL), F32), pltpu.SemaphoreType.DMA(())],
        compiler_params=_cparams(("arbitrary",)),
        name="dispatch",
    )(tab, fill, eid_p, rank_p, seg_start, xn_p, eid_s, rank_s, xn_s)


def _expert_kernel(tab_ref, xb_ref, wu_ref, bu_ref, wd_ref, bd_ref, yb_ref,
                   wu_s, wd_s, xbuf, ybuf, xsem, ysem, *, n_blocks):
    e = pl.program_id(0)
    base = tab_ref[e] * ROW_BLOCK
    nblk = tab_ref[N_EXPERTS + e]
    npair = nblk // 2
    odd = nblk % 2 == 1
    pair_rows = 2 * ROW_BLOCK
    tail_row = base + npair * pair_rows

    def hbm_rows(row0, nrows):
        return pl.ds(pl.multiple_of(row0, ROW_BLOCK), nrows)

    def x_copy(row0, nrows, sl):
        return pltpu.make_async_copy(xb_ref.at[hbm_rows(row0, nrows), :],
                                     xbuf.at[sl, pl.ds(0, nrows), :], xsem.at[sl])

    def y_copy(row0, nrows, sl):
        return pltpu.make_async_copy(ybuf.at[sl, pl.ds(0, nrows), :],
                                     yb_ref.at[hbm_rows(row0, nrows), :], ysem.at[sl])

    def ffn(nrows, sl):
        hcat = _dot(xbuf[sl, 0:nrows, :].astype(BF16), wu_s[...]) + bu_ref[...]
        glu = jnp.minimum(hcat[:, 0:D_FF], SWIGLU_LIMIT)
        lin = jnp.clip(hcat[:, D_FF:2 * D_FF], -SWIGLU_LIMIT, SWIGLU_LIMIT)
        act = glu * _sigmoid(SWIGLU_ALPHA * glu) * (lin + 1.0)
        ybuf[sl, 0:nrows, :] = _dot(act.astype(BF16), wd_s[...]) + bd_ref[...]

    @pl.when(npair > 0)
    def _():
        x_copy(base, pair_rows, 0).start(priority=ROW_DMA_PRIORITY)

    @pl.when(jnp.logical_and(npair == 0, odd))
    def _():
        x_copy(tail_row, ROW_BLOCK, 0).start(priority=ROW_DMA_PRIORITY)

    step = 128

    def cast(r, c):
        rows = pl.ds(pl.multiple_of(r * step, step), step)
        wu_s[rows, :] = wu_ref[rows, :].astype(BF16)
        wd_s[rows, :] = wd_ref[rows, :].astype(BF16)
        return c
    lax.fori_loop(0, D_MODEL // step, cast, 0)

    def pair(j, c):
        sl = j % 2
        x_copy(base, pair_rows, sl).wait()

        @pl.when(j + 1 < npair)
        def _():
            x_copy(base + (j + 1) * pair_rows, pair_rows, 1 - sl).start(priority=ROW_DMA_PRIORITY)

        @pl.when(jnp.logical_and(j + 1 == npair, odd))
        def _():
            x_copy(tail_row, ROW_BLOCK, 1 - sl).start(priority=ROW_DMA_PRIORITY)

        @pl.when(j >= 2)
        def _():
            y_copy(base, pair_rows, sl).wait()

        ffn(pair_rows, sl)
        y_copy(base + j * pair_rows, pair_rows, sl).start(priority=ROW_DMA_PRIORITY)
        return c
    lax.fori_loop(0, npair, pair, 0)

    last = npair % 2

    @pl.when(odd)
    def _():
        x_copy(tail_row, ROW_BLOCK, last).wait()

        @pl.when(npair >= 2)
        def _():
            y_copy(base, pair_rows, last).wait()

        ffn(ROW_BLOCK, last)
        y_copy(tail_row, ROW_BLOCK, last).start(priority=ROW_DMA_PRIORITY)

    @pl.when(npair >= 1)
    def _():
        y_copy(base, pair_rows, 1 - last).wait()

    @pl.when(jnp.logical_and(npair >= 2, jnp.logical_not(odd)))
    def _():
        y_copy(base, pair_rows, last).wait()

    @pl.when(odd)
    def _():
        y_copy(tail_row, ROW_BLOCK, last).wait()

    @pl.when(e == N_EXPERTS - 1)
    def _():
        ybuf[0, 0:ROW_BLOCK, :] = jnp.zeros((ROW_BLOCK, D_MODEL), F32)
        free = tab_ref[2 * N_EXPERTS]

        def tail(b):
            return y_copy(b * ROW_BLOCK, ROW_BLOCK, 0)

        def zissue(b, c):
            tail(b).start()
            return c
        lax.fori_loop(free, n_blocks, zissue, 0)

        def zdrain(b, c):
            tail(0).wait()
            return c
        lax.fori_loop(free, n_blocks, zdrain, 0)


def _experts(tab, xb, w_up, b_up, w_down, b_down, *, n_blocks):
    w_map = lambda e, tab: (e, 0, 0)
    grid_spec = pltpu.PrefetchScalarGridSpec(
        num_scalar_prefetch=1,
        grid=(N_EXPERTS,),
        in_specs=[pl.BlockSpec(memory_space=pl.ANY),
                  pl.BlockSpec((None, D_MODEL, 2 * D_FF), w_map),
                  pl.BlockSpec((None, 1, 2 * D_FF), w_map),
                  pl.BlockSpec((None, D_FF, D_MODEL), w_map),
                  pl.BlockSpec((None, 1, D_MODEL), w_map)],
        out_specs=pl.BlockSpec(memory_space=pl.ANY),
        scratch_shapes=[pltpu.VMEM((D_MODEL, 2 * D_FF), BF16), pltpu.VMEM((D_FF, D_MODEL), BF16),
                        pltpu.VMEM((2, 2 * ROW_BLOCK, D_MODEL), F32),
                        pltpu.VMEM((2, 2 * ROW_BLOCK, D_MODEL), F32),
                        pltpu.SemaphoreType.DMA((2,)), pltpu.SemaphoreType.DMA((2,))],
    )
    return pl.pallas_call(
        functools.partial(_expert_kernel, n_blocks=n_blocks),
        grid_spec=grid_spec,
        out_shape=jax.ShapeDtypeStruct((n_blocks * ROW_BLOCK, D_MODEL), F32),
        compiler_params=_cparams(("arbitrary",)),
        name="experts",
    )(tab, xb, w_up, b_up, w_down, b_down)


def _split_bf16(a):
    hi = a.astype(BF16)
    return hi, (a - hi.astype(F32)).astype(BF16)


def _combine_kernel(tab_ref, nxt_ref, slot_ref, gate_ref, h2_ref, gfin_ref, yb_ref, y_ref, buf, sems,
                    *, tm):
    i = pl.program_id(0)
    n = pl.num_programs(0)
    cur = i % 2
    nrows = buf.shape[1]

    def granule(src_row, sl, dst_row):
        return pltpu.make_async_copy(
            yb_ref.at[pl.ds(pl.multiple_of(src_row, SUBLANES), SUBLANES), :],
            buf.at[sl, pl.ds(pl.multiple_of(dst_row, SUBLANES), SUBLANES), :], sems.at[sl])

    def gather(t_ref, sl):
        for e in range(N_EXPERTS):
            src = t_ref[e]
            dst = t_ref[2 * N_EXPERTS + e]

            def issue(g, c):
                granule(src + g * SUBLANES, sl, dst + g * SUBLANES).start()
                return c
            lax.fori_loop(0, t_ref[N_EXPERTS + e], issue, 0)

    @pl.when(i == 0)
    def _():
        buf[...] = jnp.zeros_like(buf)
        gather(tab_ref, 0)

    @pl.when(i + 1 < n)
    def _():
        gather(nxt_ref, 1 - cur)

    def drain(g, c):
        granule(0, cur, 0).wait()
        return c
    lax.fori_loop(0, tab_ref[3 * N_EXPERTS], drain, 0)

    s_iota = lax.broadcasted_iota(I32, (tm, nrows), 1)
    wgt = jnp.zeros((tm, nrows), F32)
    for kk in range(TOP_K):
        wgt = wgt + jnp.where(s_iota == slot_ref[:, kk:kk + 1], gate_ref[:, kk:kk + 1], 0.0)
    w_hi, w_lo = _split_bf16(wgt)
    y_hi, y_lo = _split_bf16(buf[cur])
    acc = _dot(w_hi, y_hi) + (_dot(w_hi, y_lo) + _dot(w_lo, y_hi))
    y_ref[...] = _rms(h2_ref[...] + acc, gfin_ref[...])


def _combine(tab, slot_col, gate_col, h2, gfin, yb, *, tm):
    ntok = h2.shape[0]
    n = ntok // tm
    smem = lambda imap: pl.BlockSpec((TABLE_WIDTH,), imap, memory_space=pltpu.SMEM)
    return pl.pallas_call(
        functools.partial(_combine_kernel, tm=tm),
        grid=(n,),
        in_specs=[smem(lambda i: (i,)), smem(lambda i: (jnp.minimum(i + 1, n - 1),)),
                  pl.BlockSpec((tm, TOP_K), lambda i: (i, 0)),
                  pl.BlockSpec((tm, TOP_K), lambda i: (i, 0)),
                  pl.BlockSpec((tm, D_MODEL), lambda i: (i, 0)),
                  pl.BlockSpec((1, D_MODEL), lambda i: (0, 0)),
                  pl.BlockSpec(memory_space=pl.ANY)],
        out_specs=pl.BlockSpec((tm, D_MODEL), lambda i: (i, 0)),
        out_shape=jax.ShapeDtypeStruct((ntok, D_MODEL), F32),
        scratch_shapes=[pltpu.VMEM((2, _sorted_rows(tm), D_MODEL), F32),
                        pltpu.SemaphoreType.DMA((2,))],
        compiler_params=_cparams(("arbitrary",)),
        name="combine",
    )(tab, tab, slot_col, gate_col, h2, gfin, yb)


def _tiles(a, tile):
    bsz, kk, seq = a.shape
    return a.reshape(bsz, kk, seq // tile, tile).transpose(0, 2, 1, 3).reshape(-1, kk, tile)


def _path(x, cbuf, c0, n0, m0, mkb, mvb, wts, *, tm_in, ct, chunk, tm_post, sub, group):
    q, k, v, og, gcol, grow, yc, nbuf = _inproj(
        x, wts["g_mix"], wts["wq"], wts["wg"], wts["wgt"], wts["wc"], wts["bg"], wts["bgt"],
        wts["cw"], cbuf, tm=tm_in, chunk=chunk)
    ym, c1, n1, m1 = _mlstm(q, k, v, og, gcol, grow, c0, n0, m0, wts["ng"], ct=ct, chunk=chunk)
    h2, xn, eid, gate, rank, cnt = _post(
        ym, yc, x, wts["wmo"], wts["g_x"], wts["wxq"], mkb, mvb, wts["wxo"], wts["g_ffn"],
        wts["wrt"], wts["br"], tm=tm_post, sub=sub, group=group)
    return dict(h2=h2, xn=xn, eid=eid, gate=gate, rank=rank, cnt=cnt[:, :, 0],
                c1=c1, n1=n1, m1=m1[..., 0], nbuf=nbuf)


def kernel(x_prompt, x_sample, state_mlstm_c, state_mlstm_n, state_mlstm_m, state_conv, cache_mem_k, cache_mem_v, mem_prompt, norm_mix_g, w_in, b_gate, mlstm_norm_g, conv_w, w_mix_out, norm_x_g, norm_mem_g, w_xq, w_xk, w_xv, w_xo, norm_ffn_g, w_router, b_router, w_up, b_up, w_down, b_down, norm_final_g):
    bp, lp, _ = x_prompt.shape
    bs, ls, _ = x_sample.shape
    l = 0
    row = lambda a: a.reshape(1, -1)

    wi = w_in[l]
    gate_cols = wi[:, 4 * D_MLSTM:4 * D_MLSTM + 2 * N_HEADS]
    wts = dict(
        g_mix=row(norm_mix_g[l]),
        wq=wi[:, 0:4 * D_MLSTM].astype(BF16),
        wg=jnp.pad(gate_cols, ((0, 0), (0, 128 - 2 * N_HEADS))).astype(BF16),
        wgt=gate_cols.T.astype(BF16),
        wc=wi[:, 4 * D_MLSTM + 2 * N_HEADS:].astype(BF16),
        bg=row(b_gate[l]), bgt=b_gate[l].reshape(-1, 1),
        cw=conv_w[l], ng=row(mlstm_norm_g[l]),
        wmo=w_mix_out[l].astype(BF16), g_x=row(norm_x_g[l]), wxq=w_xq[l].astype(BF16),
        wxo=w_xo[l].astype(BF16), g_ffn=row(norm_ffn_g[l]),
        wrt=w_router[l].T.astype(BF16), br=b_router[l].reshape(-1, 1),
    )

    mk, mv, mkb, mvb = _memkv(mem_prompt.reshape(bp * N_MEM, D_MODEL), row(norm_mem_g[l]),
                              w_xk[l].astype(BF16), w_xv[l].astype(BF16))
    zeros = lambda *s: jnp.zeros(s, F32)
    tm_post, tmd = 512, 256
    pr = _path(x_prompt, zeros(bp, CONV_WIDTH - 1, D_CONV), zeros(bp, N_HEADS, HEAD_DIM, HEAD_DIM),
               zeros(bp, N_HEADS, HEAD_DIM), zeros(bp, N_HEADS, 1),
               mkb.reshape(bp, N_MEM, D_MODEL), mvb.reshape(bp, N_MEM, D_MODEL), wts,
               tm_in=512, ct=512, chunk=CHUNK, tm_post=tm_post, sub=tmd, group=1)
    sa = _path(x_sample, state_conv[l], state_mlstm_c[l], state_mlstm_n[l],
               state_mlstm_m[l][..., None],
               cache_mem_k[l].reshape(bs, N_MEM, D_MODEL).astype(BF16),
               cache_mem_v[l].reshape(bs, N_MEM, D_MODEL).astype(BF16), wts,
               tm_in=ls, ct=ls, chunk=min(CHUNK, ls), tm_post=ls, sub=ls, group=bs)

    n_ptok, n_stok = bp * lp, bs * ls
    cnt = jnp.concatenate([pr["cnt"], sa["cnt"]], axis=0)
    n_tiles = cnt.shape[0]
    n_blocks = -(-(TOP_K * (n_ptok + n_stok) + n_tiles * N_EXPERTS * (SUBLANES - 1)) // ROW_BLOCK) \
        + N_EXPERTS
    seg = (cnt + SUBLANES - 1) // SUBLANES * SUBLANES
    seg_src = jnp.cumsum(seg, axis=1) - seg
    tot = jnp.sum(seg, axis=0)
    blocks_e = (tot + ROW_BLOCK - 1) // ROW_BLOCK
    padded = blocks_e * ROW_BLOCK
    pstart = jnp.cumsum(padded) - padded
    bend = jnp.cumsum(blocks_e)
    n_used = bend[-1]
    exp_tab = jnp.concatenate([bend - blocks_e, blocks_e, n_used[None]]).astype(I32)
    seg_dst = pstart[None, :] + jnp.cumsum(seg, axis=0) - seg

    gran = seg // SUBLANES
    tab = jnp.concatenate([seg_dst, gran, seg_src, jnp.sum(gran, axis=1, keepdims=True)], axis=1)
    tab = jnp.pad(tab, ((0, 0), (0, TABLE_WIDTH - tab.shape[1]))).astype(I32).reshape(-1)
    fgran = (padded - tot) // SUBLANES
    fill = jnp.concatenate([pstart + tot, fgran, jnp.sum(fgran)[None], n_used[None]])
    fill = jnp.pad(fill, (0, TABLE_WIDTH - fill.shape[0])).astype(I32)

    def lookup(table, eid):
        hit = eid[..., None] == jnp.arange(N_EXPERTS, dtype=I32)
        return jnp.sum(jnp.where(hit, table[:, None, None, :], 0), axis=-1)

    eid_pt, rank_pt = _tiles(pr["eid"], tmd), _tiles(pr["rank"], tmd)
    eid_st = sa["eid"].transpose(1, 0, 2).reshape(1, TOP_K, n_stok)
    rank_st = sa["rank"].transpose(1, 0, 2).reshape(1, TOP_K, n_stok)
    slot_p = (lookup(seg_src[:-1], eid_pt) + rank_pt).transpose(0, 2, 1).reshape(n_ptok, TOP_K)
    slot_s = (lookup(seg_src[-1:], eid_st) + rank_st).transpose(0, 2, 1).reshape(n_stok, TOP_K)

    xb = _dispatch(tab, fill, eid_pt, rank_pt, seg_src.astype(F32)[..., None],
                   pr["xn"].reshape(n_ptok, D_MODEL), eid_st[0], rank_st[0],
                   sa["xn"].reshape(n_stok, D_MODEL), tmd=tmd, n_blocks=n_blocks)
    yb = _experts(exp_tab, xb, w_up[l], b_up[l][:, None, :], w_down[l], b_down[l][:, None, :],
                  n_blocks=n_blocks)

    gfin = row(norm_final_g)
    split = (n_tiles - 1) * TABLE_WIDTH
    y_p = _combine(tab[:split], slot_p.astype(I32), pr["gate"].transpose(0, 2, 1).reshape(n_ptok, TOP_K),
                   pr["h2"].reshape(n_ptok, D_MODEL), gfin, yb, tm=tmd)
    y_s = _combine(tab[split:], slot_s.astype(I32), sa["gate"].transpose(0, 2, 1).reshape(n_stok, TOP_K),
                   sa["h2"].reshape(n_stok, D_MODEL), gfin, yb, tm=n_stok)

    lead = lambda a: a[None]
    return (y_p.reshape(bp, lp, D_MODEL), y_s.reshape(bs, ls, D_MODEL),
            lead(pr["c1"]), lead(pr["n1"]), lead(pr["m1"]), lead(pr["nbuf"]),
            lead(mk.reshape(bp, N_MEM, N_XHEADS, XHEAD_DIM)),
            lead(mv.reshape(bp, N_MEM, N_XHEADS, XHEAD_DIM)),
            lead(sa["c1"]), lead(sa["n1"]), lead(sa["m1"]), lead(sa["nbuf"]))
```

```python
import functools

import jax
import jax.numpy as jnp
import numpy as np
from jax import lax
from jax.experimental import pallas as pl
from jax.experimental.pallas import tpu as pltpu

F32 = jnp.float32
BF16 = jnp.bfloat16
I32 = jnp.int32

D_MODEL = 1024
N_HEADS = 4
HEAD_DIM = 128
D_MLSTM = N_HEADS * HEAD_DIM
D_CONV = D_MODEL - D_MLSTM
CONV_WIDTH = 3
CHUNK = 64
N_MEM = 256
N_XHEADS = 4
XHEAD_DIM = D_MODEL // N_XHEADS
N_EXPERTS = 32
TOP_K = 4
D_FF = D_MODEL
SWIGLU_LIMIT = 7.0
SWIGLU_ALPHA = 1.702
EPS = 1e-5

SUBLANES = 8
TABLE_WIDTH = 128
ROW_BLOCK = 256
ROW_LOOKAHEAD = 2
ROW_RING = ROW_LOOKAHEAD + 1
VMEM_LIMIT = 56 * 1024 * 1024


def _cparams(sem):
    return pltpu.CompilerParams(dimension_semantics=sem, vmem_limit_bytes=VMEM_LIMIT)


def _rms(x, g):
    return x * lax.rsqrt(jnp.mean(x * x, axis=-1, keepdims=True) + EPS) * g


def _log_sigmoid(x):
    return -(jnp.maximum(-x, 0.0) + jnp.log1p(jnp.exp(-jnp.abs(x))))


def _sigmoid(x):
    return 1.0 / (1.0 + jnp.exp(-x))


def _dot(a, b):
    return jnp.dot(a, b, preferred_element_type=F32)


def _dot_nt(a, b):
    return lax.dot_general(a, b, (((1,), (1,)), ((), ())), preferred_element_type=F32)


def _dot_tn(a, b):
    return lax.dot_general(a, b, (((0,), (0,)), ((), ())), preferred_element_type=F32)


def _memkv_kernel(mem_ref, g_ref, wk_ref, wv_ref, mk_ref, mv_ref, mkb_ref, mvb_ref):
    mn = _rms(mem_ref[...], g_ref[...]).astype(BF16)
    mk = _dot(mn, wk_ref[...])
    mv = _dot(mn, wv_ref[...])
    mk_ref[...] = mk
    mv_ref[...] = mv
    mkb_ref[...] = mk.astype(BF16)
    mvb_ref[...] = mv.astype(BF16)


def _memkv(mem2d, g, wk, wv):
    rows = mem2d.shape[0]
    tm = N_MEM
    row_spec = pl.BlockSpec((tm, D_MODEL), lambda i: (i, 0))
    full = lambda shape: pl.BlockSpec(shape, lambda i: (0,) * len(shape))
    return pl.pallas_call(
        _memkv_kernel,
        grid=(rows // tm,),
        in_specs=[row_spec, full((1, D_MODEL)), full((D_MODEL, D_MODEL)), full((D_MODEL, D_MODEL))],
        out_specs=[row_spec, row_spec, row_spec, row_spec],
        out_shape=[jax.ShapeDtypeStruct((rows, D_MODEL), F32)] * 2
        + [jax.ShapeDtypeStruct((rows, D_MODEL), BF16)] * 2,
        compiler_params=_cparams(("arbitrary",)),
        name="memkv",
    )(mem2d, g, wk, wv)


def _inproj_kernel(x_ref, g_ref, wq_ref, wg_ref, wgt_ref, wc_ref, bg_ref, bgt_ref, cw_ref, cbuf_ref,
                   q_ref, k_ref, v_ref, og_ref, gcol_ref, grow_ref, yc_ref, nbuf_ref,
                   carry_ref, *, tm, chunk):
    j = pl.program_id(1)

    @pl.when(j == 0)
    def _():
        carry_ref[0:2, :] = cbuf_ref[...]

    xb = _rms(x_ref[...], g_ref[...]).astype(BF16)

    p = _dot(xb, wq_ref[...])
    q_ref[...] = p[:, 0:D_MLSTM].astype(BF16)
    k_ref[...] = (p[:, D_MLSTM:2 * D_MLSTM] * (HEAD_DIM ** -0.5)).astype(BF16)
    v_ref[...] = p[:, 2 * D_MLSTM:3 * D_MLSTM].astype(BF16)
    og_ref[...] = _sigmoid(p[:, 3 * D_MLSTM:4 * D_MLSTM])

    gc = _dot(xb, wg_ref[...])[:, 0:2 * N_HEADS] + bg_ref[...]
    col = lax.broadcasted_iota(I32, gc.shape, 1)
    gcol_ref[...] = jnp.where(col < N_HEADS, gc, _log_sigmoid(gc))
    gr = _dot_nt(wgt_ref[...], xb) + bgt_ref[...]
    row = lax.broadcasted_iota(I32, gr.shape, 0)
    gr = jnp.where(row < N_HEADS, gr, _log_sigmoid(gr))
    for c in range(tm // chunk):
        grow_ref[c] = gr[:, c * chunk:(c + 1) * chunk]

    pc = _dot(xb, wc_ref[...])
    cb = pc[:, 0:D_CONV]
    u = pc[:, D_CONV:2 * D_CONV] * pc[:, 2 * D_CONV:3 * D_CONV]
    c0 = carry_ref[0:1, :]
    c1 = carry_ref[1:2, :]
    rid = lax.broadcasted_iota(I32, u.shape, 0)
    um1 = jnp.where(rid == 0, c1, pltpu.roll(u, 1, 0))
    um2 = jnp.where(rid == 0, c0, jnp.where(rid == 1, c1, pltpu.roll(u, 2, 0)))
    yc = cw_ref[0:1, :] * um2 + cw_ref[1:2, :] * um1 + cw_ref[2:3, :] * u
    yc_ref[...] = (cb * yc).astype(BF16)
    tail = u[tm - 2:tm, :]
    carry_ref[0:2, :] = tail
    nbuf_ref[...] = tail


def _inproj(x, g, wq, wg, wgt, wc, bg, bgt, cw, cbuf, *, tm, chunk):
    bsz, seq, _ = x.shape
    grid = (bsz, seq // tm)
    tok = lambda c: pl.BlockSpec((None, tm, c), lambda b, j: (b, j, 0))
    full = lambda shape: pl.BlockSpec(shape, lambda b, j: (0,) * len(shape))
    nck = tm // chunk
    return pl.pallas_call(
        functools.partial(_inproj_kernel, tm=tm, chunk=chunk),
        grid=grid,
        in_specs=[tok(D_MODEL), full((1, D_MODEL)), full((D_MODEL, 4 * D_MLSTM)),
                  full((D_MODEL, 128)), full((8, D_MODEL)), full((D_MODEL, 3 * D_CONV)),
                  full((1, 8)), full((8, 1)), full((CONV_WIDTH, D_CONV)),
                  pl.BlockSpec((None, 2, D_CONV), lambda b, j: (b, 0, 0))],
        out_specs=[tok(D_MLSTM), tok(D_MLSTM), tok(D_MLSTM), tok(D_MLSTM), tok(8),
                   pl.BlockSpec((None, nck, 8, chunk), lambda b, j: (b, j, 0, 0)),
                   tok(D_CONV),
                   pl.BlockSpec((None, 2, D_CONV), lambda b, j: (b, 0, 0))],
        out_shape=[jax.ShapeDtypeStruct((bsz, seq, D_MLSTM), BF16)] * 3
        + [jax.ShapeDtypeStruct((bsz, seq, D_MLSTM), F32),
           jax.ShapeDtypeStruct((bsz, seq, 8), F32),
           jax.ShapeDtypeStruct((bsz, seq // chunk, 8, chunk), F32),
           jax.ShapeDtypeStruct((bsz, seq, D_CONV), BF16),
           jax.ShapeDtypeStruct((bsz, 2, D_CONV), F32)],
        scratch_shapes=[pltpu.VMEM((8, D_CONV), F32)],
        compiler_params=_cparams(("arbitrary", "arbitrary")),
        name="inproj",
    )(x, g, wq, wg, wgt, wc, bg, bgt, cw, cbuf)


def _mlstm_kernel(q_ref, k_ref, v_ref, og_ref, gc_ref, gr_ref, c0_ref, n0_ref, m0_ref, ng_ref,
                  ym_ref, c1_ref, n1_ref, m1_ref, c_s, n_s, m_s, *, chunk, nchunks, bsz):
    j = pl.program_id(0)

    @pl.when(j == 0)
    def _():
        c_s[...] = c0_ref[...]
        n_s[...] = n0_ref[...]
        m_s[...] = m0_ref[...]

    ti = lax.broadcasted_iota(I32, (chunk, chunk), 0)
    ji = lax.broadcasted_iota(I32, (chunk, chunk), 1)
    causal = ji <= ti

    def body(ci, carry):
        r0 = pl.multiple_of(ci * chunk, chunk)
        rows = pl.ds(r0, chunk)
        chains = [(b, h) for b in range(bsz) for h in range(N_HEADS)]
        cols = lambda h: slice(h * HEAD_DIM, (h + 1) * HEAD_DIM)
        each = lambda f: [f(n, b, h) for n, (b, h) in enumerate(chains)]
        q = lambda b, h: q_ref[b, rows, cols(h)]
        k = lambda b, h: k_ref[b, rows, cols(h)]
        v = lambda b, h: v_ref[b, rows, cols(h)]
        gcs = [gc_ref[b, rows, :] for b in range(bsz)]
        grs = [gr_ref[b, ci] for b in range(bsz)]
        li_c = each(lambda n, b, h: gcs[b][:, h:h + 1])
        lf_c = each(lambda n, b, h: gcs[b][:, N_HEADS + h:N_HEADS + h + 1])
        li_r = each(lambda n, b, h: grs[b][h:h + 1, :])
        lf_r = each(lambda n, b, h: grs[b][N_HEADS + h:N_HEADS + h + 1, :])
        m_prev = each(lambda n, b, h: m_s[b, h:h + 1, :])

        b_c = each(lambda n, b, h: jnp.sum(jnp.where(causal, lf_r[n], 0.0), axis=1, keepdims=True))
        b_r = each(lambda n, b, h: jnp.sum(jnp.where(ti <= ji, lf_c[n], 0.0), axis=0, keepdims=True))
        dmat = each(lambda n, b, h: jnp.where(causal, b_c[n] - b_r[n] + li_r[n], -jnp.inf))
        dmax = each(lambda n, b, h: jnp.max(dmat[n], axis=1, keepdims=True))
        inter = each(lambda n, b, h: b_c[n] + m_prev[n])
        m_t = each(lambda n, b, h: jnp.maximum(inter[n], dmax[n]))
        w_inter = each(lambda n, b, h: jnp.exp(inter[n] - m_t[n]))
        s = each(lambda n, b, h: _dot_nt(q(b, h), k(b, h)) * jnp.exp(dmat[n] - m_t[n]))
        qc = each(lambda n, b, h: _dot(q(b, h), c_s[b, h].astype(BF16)))
        sv = each(lambda n, b, h: _dot(s[n].astype(BF16), v(b, h)))
        qn = each(lambda n, b, h: jnp.sum(q(b, h).astype(F32) * n_s[b, h:h + 1, :], axis=1,
                                          keepdims=True))
        den = each(lambda n, b, h: w_inter[n] * qn[n] + jnp.sum(s[n], axis=1, keepdims=True))
        hh = each(lambda n, b, h: (w_inter[n] * qc[n] + sv[n])
                  / jnp.maximum(jnp.abs(den[n]), jnp.exp(-m_t[n])))

        m_new = each(lambda n, b, h: m_t[n][chunk - 1:chunk, :])
        b_last = each(lambda n, b, h: b_c[n][chunk - 1:chunk, :])
        decay = each(lambda n, b, h: jnp.exp(b_last[n] + m_prev[n] - m_new[n]))
        kw = each(lambda n, b, h: k(b, h).astype(F32)
                  * jnp.exp(b_last[n] - b_c[n] + li_c[n] - m_new[n]))
        kv = each(lambda n, b, h: _dot_tn(kw[n].astype(BF16), v(b, h)))
        for n, (b, h) in enumerate(chains):
            c_s[b, h] = decay[n] * c_s[b, h] + kv[n]
            n_s[b, h:h + 1, :] = decay[n] * n_s[b, h:h + 1, :] + jnp.sum(kw[n], axis=0, keepdims=True)
            m_s[b, h:h + 1, :] = m_new[n]

        hn = each(lambda n, b, h: hh[n] * lax.rsqrt(jnp.mean(hh[n] * hh[n], axis=1, keepdims=True) + EPS)
                  * ng_ref[:, cols(h)])
        for n, (b, h) in enumerate(chains):
            ym_ref[b, rows, cols(h)] = (hn[n] * og_ref[b, rows, cols(h)]).astype(BF16)
        return carry

    lax.fori_loop(0, nchunks, body, 0)

    @pl.when(j == pl.num_programs(0) - 1)
    def _():
        c1_ref[...] = c_s[...]
        n1_ref[...] = n_s[...]
        m1_ref[...] = m_s[...]


def _mlstm(q, k, v, og, gcol, grow, c0, n0, m0, ng, *, ct, chunk):
    bsz, seq, _ = q.shape
    nchunks = ct // chunk
    grid = (seq // ct,)
    tok = lambda c: pl.BlockSpec((bsz, ct, c), lambda j: (0, j, 0))
    st_c = pl.BlockSpec((bsz, N_HEADS, HEAD_DIM, HEAD_DIM), lambda j: (0, 0, 0, 0))
    st_n = pl.BlockSpec((bsz, N_HEADS, HEAD_DIM), lambda j: (0, 0, 0))
    st_m = pl.BlockSpec((bsz, N_HEADS, 1), lambda j: (0, 0, 0))
    return pl.pallas_call(
        functools.partial(_mlstm_kernel, chunk=chunk, nchunks=nchunks, bsz=bsz),
        grid=grid,
        in_specs=[tok(D_MLSTM), tok(D_MLSTM), tok(D_MLSTM), tok(D_MLSTM), tok(8),
                  pl.BlockSpec((bsz, nchunks, 8, chunk), lambda j: (0, j, 0, 0)),
                  st_c, st_n, st_m,
                  pl.BlockSpec((1, D_MLSTM), lambda j: (0, 0))],
        out_specs=[tok(D_MLSTM), st_c, st_n, st_m],
        out_shape=[jax.ShapeDtypeStruct((bsz, seq, D_MLSTM), BF16),
                   jax.ShapeDtypeStruct((bsz, N_HEADS, HEAD_DIM, HEAD_DIM), F32),
                   jax.ShapeDtypeStruct((bsz, N_HEADS, HEAD_DIM), F32),
                   jax.ShapeDtypeStruct((bsz, N_HEADS, 1), F32)],
        scratch_shapes=[pltpu.VMEM((bsz, N_HEADS, HEAD_DIM, HEAD_DIM), F32),
                        pltpu.VMEM((bsz, N_HEADS, HEAD_DIM), F32),
                        pltpu.VMEM((bsz, N_HEADS, 1), F32)],
        compiler_params=_cparams(("arbitrary",)),
        name="mlstm",
    )(q, k, v, og, gcol, grow, c0, n0, m0, ng)


def _post_kernel(ym_ref, yc_ref, x_ref, wmo_ref, gx_ref, wxq_ref, mk_ref, mv_ref, wxo_ref,
                 gf_ref, wrt_ref, br_ref,
                 h2_ref, xn_ref, eid_ref, gate_ref, rank_ref, cnt_ref, cnt_s, *, tm, sub, group):
    step = pl.program_id(0) * pl.num_programs(1) + pl.program_id(1)

    @pl.when(step % group == 0)
    def _():
        cnt_s[...] = jnp.zeros_like(cnt_s)

    mix = _dot(ym_ref[...], wmo_ref[0:D_MLSTM, :]) + _dot(yc_ref[...], wmo_ref[D_MLSTM:D_MODEL, :])
    h1 = x_ref[...] + mix

    xq = _dot(_rms(h1, gx_ref[...]).astype(BF16), wxq_ref[...]).astype(BF16)
    att = jnp.zeros((tm, D_MODEL), F32)
    for hd in range(N_XHEADS):
        cols = slice(hd * XHEAD_DIM, (hd + 1) * XHEAD_DIM)
        s = _dot_nt(xq[:, cols], mk_ref[:, cols]) * (XHEAD_DIM ** -0.5)
        e = jnp.exp(s - jnp.max(s, axis=-1, keepdims=True))
        p = (e / jnp.sum(e, axis=-1, keepdims=True)).astype(BF16)
        o = _dot(p, mv_ref[:, cols]).astype(BF16)
        att = att + _dot(o, wxo_ref[cols, :])
    h2 = h1 + att
    h2_ref[...] = h2

    xn2 = _rms(h2, gf_ref[...]).astype(BF16)
    xn_ref[...] = xn2

    logits = _dot_nt(wrt_ref[...], xn2) + br_ref[...]
    eidx = lax.broadcasted_iota(I32, logits.shape, 0).astype(F32)
    work = logits
    vals, ids, hots = [], [], []
    for _ in range(TOP_K):
        mx = jnp.max(work, axis=0, keepdims=True)
        idx = jnp.min(jnp.where(work == mx, eidx, float(N_EXPERTS)), axis=0, keepdims=True)
        sel = eidx == idx
        vals.append(mx)
        ids.append(idx)
        hots.append(sel)
        work = jnp.where(sel, -jnp.inf, work)
    exps = [jnp.exp(v - vals[0]) for v in vals]
    denom = exps[0] + exps[1] + exps[2] + exps[3]

    picked = jnp.zeros(logits.shape, F32)
    for sel in hots:
        picked = picked + sel.astype(F32)
    shift = jnp.full((tm, tm), sub.bit_length() - 1, I32)
    tj = lax.broadcasted_iota(I32, (tm, tm), 0)
    tt = lax.broadcasted_iota(I32, (tm, tm), 1)
    same = lax.shift_right_logical(tj, shift) == lax.shift_right_logical(tt, shift)
    before = jnp.where(jnp.logical_and(tj < tt, same), 1.0, 0.0).astype(BF16)
    prior = _dot(picked.astype(BF16), before) + cnt_s[...]
    for kk in range(TOP_K):
        eid_ref[kk:kk + 1, :] = ids[kk].astype(I32)
        gate_ref[kk:kk + 1, :] = exps[kk] / denom
        rank_ref[kk:kk + 1, :] = jnp.sum(jnp.where(hots[kk], prior, 0.0), axis=0,
                                         keepdims=True).astype(I32)
    for s in range(tm // sub):
        total = cnt_s[...] + jnp.sum(picked[:, s * sub:(s + 1) * sub], axis=1, keepdims=True)
        cnt_ref[s] = total.astype(I32)
    cnt_s[...] = total


def _post(ym, yc, x, wmo, gx, wxq, mkb, mvb, wxo, gf, wrt, br, *, tm, sub, group):
    bsz, seq, _ = x.shape
    nj = seq // tm
    grid = (bsz, nj)
    nsub = tm // sub
    n_tiles = bsz * nj * nsub // group
    tok = lambda c: pl.BlockSpec((None, tm, c), lambda b, j: (b, j, 0))
    full = lambda shape: pl.BlockSpec(shape, lambda b, j: (0,) * len(shape))
    mem = pl.BlockSpec((None, N_MEM, D_MODEL), lambda b, j: (b, 0, 0))
    sel = pl.BlockSpec((None, TOP_K, tm), lambda b, j: (b, 0, j))
    return pl.pallas_call(
        functools.partial(_post_kernel, tm=tm, sub=sub, group=group),
        grid=grid,
        in_specs=[tok(D_MLSTM), tok(D_CONV), tok(D_MODEL), full((D_MODEL, D_MODEL)),
                  full((1, D_MODEL)), full((D_MODEL, D_MODEL)), mem, mem,
                  full((D_MODEL, D_MODEL)), full((1, D_MODEL)), full((N_EXPERTS, D_MODEL)),
                  full((N_EXPERTS, 1))],
        out_specs=[tok(D_MODEL), tok(D_MODEL), sel, sel, sel,
                   pl.BlockSpec((nsub, N_EXPERTS, 1), lambda b, j: ((b * nj + j) // group, 0, 0))],
        out_shape=[jax.ShapeDtypeStruct((bsz, seq, D_MODEL), F32),
                   jax.ShapeDtypeStruct((bsz, seq, D_MODEL), BF16),
                   jax.ShapeDtypeStruct((bsz, TOP_K, seq), I32),
                   jax.ShapeDtypeStruct((bsz, TOP_K, seq), F32),
                   jax.ShapeDtypeStruct((bsz, TOP_K, seq), I32),
                   jax.ShapeDtypeStruct((n_tiles, N_EXPERTS, 1), I32)],
        scratch_shapes=[pltpu.VMEM((N_EXPERTS, 1), F32)],
        compiler_params=_cparams(("arbitrary", "arbitrary")),
        name="post",
    )(ym, yc, x, wmo, gx, wxq, mkb, mvb, wxo, gf, wrt, br)


def _sorted_rows(n_tokens):
    return -(-(TOP_K * n_tokens + N_EXPERTS * (SUBLANES - 1)) // ROW_BLOCK) * ROW_BLOCK


def _dispatch_kernel(tab_ref, fill_ref, eid_ref, rank_ref, ls_ref, x_ref, eids_ref, ranks_ref, xs_ref,
                     xb_ref, srt, zero_s, sem, *, n_ptiles, n_blocks):
    i = pl.program_id(0)

    def granule(src, src_row, dst_row):
        return pltpu.make_async_copy(
            src.at[pl.ds(pl.multiple_of(src_row, SUBLANES), SUBLANES), :],
            xb_ref.at[pl.ds(pl.multiple_of(dst_row, SUBLANES), SUBLANES), :], sem)

    def drain(count):
        def body(g, c):
            granule(zero_s, 0, 0).wait()
            return c
        lax.fori_loop(0, count, body, 0)

    def sort_and_move(eid, rank, x):
        ntok = x.shape[0]
        nrows = _sorted_rows(ntok)
        e_iota = lax.broadcasted_iota(I32, (N_EXPERTS, ntok), 0)
        s_iota = lax.broadcasted_iota(I32, (nrows, ntok), 0)
        seg_start = ls_ref[...]
        hit = None
        for kk in range(TOP_K):
            start = jnp.sum(jnp.where(e_iota == eid[kk:kk + 1, :], seg_start, 0.0),
                            axis=0, keepdims=True).astype(I32)
            match = s_iota == start + rank[kk:kk + 1, :]
            hit = match if hit is None else jnp.logical_or(hit, match)
        perm = jnp.where(hit, 1.0, 0.0).astype(BF16)
        srt[0:nrows, :] = _dot(perm, x)
        for e in range(N_EXPERTS):
            dst = tab_ref[e]
            src = tab_ref[2 * N_EXPERTS + e]

            def issue(g, c):
                granule(srt, src + g * SUBLANES, dst + g * SUBLANES).start()
                return c
            lax.fori_loop(0, tab_ref[N_EXPERTS + e], issue, 0)
        drain(tab_ref[3 * N_EXPERTS])

    @pl.when(i < n_ptiles)
    def _():
        sort_and_move(eid_ref[...], rank_ref[...], x_ref[...])

    @pl.when(i == n_ptiles)
    def _():
        sort_and_move(eids_ref[...], ranks_ref[...], xs_ref[...])
        zero_s[...] = jnp.zeros_like(zero_s)
        for e in range(N_EXPERTS):
            dst = fill_ref[e]

            def zissue(g, c):
                granule(zero_s, 0, dst + g * SUBLANES).start()
                return c
            lax.fori_loop(0, fill_ref[N_EXPERTS + e], zissue, 0)
        drain(fill_ref[2 * N_EXPERTS])
        first_free = fill_ref[2 * N_EXPERTS + 1]

        def blk_copy(b):
            return pltpu.make_async_copy(
                zero_s, xb_ref.at[pl.ds(pl.multiple_of(b * ROW_BLOCK, ROW_BLOCK), ROW_BLOCK), :], sem)

        def bissue(b, c):
            blk_copy(b).start()
            return c
        lax.fori_loop(first_free, n_blocks, bissue, 0)

        def bdrain(b, c):
            blk_copy(0).wait()
            return c
        lax.fori_loop(first_free, n_blocks, bdrain, 0)


def _dispatch(tab, fill, eid_p, rank_p, seg_start, xn_p, eid_s, rank_s, xn_s, *, tmd, n_blocks):
    n_ptiles = eid_p.shape[0]
    n_sample = xn_s.shape[0]
    last = n_ptiles - 1
    smem = lambda shape, imap: pl.BlockSpec(shape, imap, memory_space=pltpu.SMEM)
    return pl.pallas_call(
        functools.partial(_dispatch_kernel, n_ptiles=n_ptiles, n_blocks=n_blocks),
        grid=(n_ptiles + 1,),
        in_specs=[smem((TABLE_WIDTH,), lambda i: (i,)),
                  smem((TABLE_WIDTH,), lambda i: (0,)),
                  pl.BlockSpec((None, TOP_K, tmd), lambda i: (jnp.minimum(i, last), 0, 0)),
                  pl.BlockSpec((None, TOP_K, tmd), lambda i: (jnp.minimum(i, last), 0, 0)),
                  pl.BlockSpec((None, N_EXPERTS, 1), lambda i: (i, 0, 0)),
                  pl.BlockSpec((tmd, D_MODEL), lambda i: (jnp.minimum(i, last), 0)),
                  pl.BlockSpec((TOP_K, n_sample), lambda i: (0, 0)),
                  pl.BlockSpec((TOP_K, n_sample), lambda i: (0, 0)),
                  pl.BlockSpec((n_sample, D_MODEL), lambda i: (0, 0))],
        out_specs=pl.BlockSpec(memory_space=pl.ANY),
        out_shape=jax.ShapeDtypeStruct((n_blocks * ROW_BLOCK, D_MODEL), F32),
        scratch_shapes=[pltpu.VMEM((_sorted_rows(tmd), D_MODEL), F32),
                        pltpu.VMEM((ROW_BLOCK, D_MODEL), F32), pltpu.SemaphoreType.DMA(())],
        compiler_params=_cparams(("arbitrary",)),
        name="dispatch",
    )(tab, fill, eid_p, rank_p, seg_start, xn_p, eid_s, rank_s, xn_s)


def _expert_kernel(tab_ref, xb_ref, wu_hbm, bu_ref, wd_hbm, bd_ref, yb_ref,
                   wu_f, wd_f, wu_s, wd_s, xbuf, ybuf, wsem, xsem, ysem, *, n_blocks):
    e = pl.program_id(0)
    first = tab_ref[e]
    nblk = tab_ref[N_EXPERTS + e]
    n_used = tab_ref[2 * N_EXPERTS]

    def hbm_rows(b):
        return pl.ds(pl.multiple_of(b * ROW_BLOCK, ROW_BLOCK), ROW_BLOCK)

    def x_copy(b, sl):
        return pltpu.make_async_copy(xb_ref.at[hbm_rows(b), :], xbuf.at[sl], xsem.at[sl])

    def y_copy(b, sl):
        return pltpu.make_async_copy(ybuf.at[sl], yb_ref.at[hbm_rows(b), :], ysem.at[sl])

    def w_copies(ex, sl):
        return (pltpu.make_async_copy(wu_hbm.at[ex], wu_f.at[sl], wsem.at[0, sl]),
                pltpu.make_async_copy(wd_hbm.at[ex], wd_f.at[sl], wsem.at[1, sl]))

    wslot = e % 2

    @pl.when(e == 0)
    def _():
        for c in w_copies(0, 0):
            c.start()
        for b in range(ROW_LOOKAHEAD):
            @pl.when(b < n_used)
            def _():
                x_copy(b, b % ROW_RING).start()

    @pl.when(e + 1 < N_EXPERTS)
    def _():
        for c in w_copies(e + 1, 1 - wslot):
            c.start()

    for c in w_copies(e, wslot):
        c.wait()

    step = 128

    def cast(r, c):
        rows = pl.ds(pl.multiple_of(r * step, step), step)
        wu_s[rows, :] = wu_f[wslot, rows, :].astype(BF16)
        wd_s[rows, :] = wd_f[wslot, rows, :].astype(BF16)
        return c
    lax.fori_loop(0, D_MODEL // step, cast, 0)

    def block(b, c):
        sl = b % ROW_RING
        x_copy(b, sl).wait()

        @pl.when(b + ROW_LOOKAHEAD < n_used)
        def _():
            x_copy(b + ROW_LOOKAHEAD, (b + ROW_LOOKAHEAD) % ROW_RING).start()

        @pl.when(b >= ROW_RING)
        def _():
            y_copy(b, sl).wait()

        hcat = _dot(xbuf[sl].astype(BF16), wu_s[...]) + bu_ref[...]
        glu = jnp.minimum(hcat[:, 0:D_FF], SWIGLU_LIMIT)
        lin = jnp.clip(hcat[:, D_FF:2 * D_FF], -SWIGLU_LIMIT, SWIGLU_LIMIT)
        act = glu * _sigmoid(SWIGLU_ALPHA * glu) * (lin + 1.0)
        ybuf[sl] = _dot(act.astype(BF16), wd_s[...]) + bd_ref[...]
        y_copy(b, sl).start()
        return c
    lax.fori_loop(first, first + nblk, block, 0)

    @pl.when(e == N_EXPERTS - 1)
    def _():
        for k in range(ROW_RING):
            @pl.when(k < n_used)
            def _():
                y_copy(0, (n_used - 1 - k) % ROW_RING).wait()

        ybuf[0] = jnp.zeros((ROW_BLOCK, D_MODEL), F32)

        def zissue(b, c):
            y_copy(b, 0).start()
            return c
        lax.fori_loop(n_used, n_blocks, zissue, 0)

        def zdrain(b, c):
            y_copy(0, 0).wait()
            return c
        lax.fori_loop(n_used, n_blocks, zdrain, 0)


def _experts(tab, xb, w_up, b_up, w_down, b_down, *, n_blocks):
    w_map = lambda e, tab: (e, 0, 0)
    grid_spec = pltpu.PrefetchScalarGridSpec(
        num_scalar_prefetch=1,
        grid=(N_EXPERTS,),
        in_specs=[pl.BlockSpec(memory_space=pl.ANY),
                  pl.BlockSpec(memory_space=pl.ANY),
                  pl.BlockSpec((None, 1, 2 * D_FF), w_map),
                  pl.BlockSpec(memory_space=pl.ANY),
                  pl.BlockSpec((None, 1, D_MODEL), w_map)],
        out_specs=pl.BlockSpec(memory_space=pl.ANY),
        scratch_shapes=[pltpu.VMEM((2, D_MODEL, 2 * D_FF), F32), pltpu.VMEM((2, D_FF, D_MODEL), F32),
                        pltpu.VMEM((D_MODEL, 2 * D_FF), BF16), pltpu.VMEM((D_FF, D_MODEL), BF16),
                        pltpu.VMEM((ROW_RING, ROW_BLOCK, D_MODEL), F32),
                        pltpu.VMEM((ROW_RING, ROW_BLOCK, D_MODEL), F32),
                        pltpu.SemaphoreType.DMA((2, 2)),
                        pltpu.SemaphoreType.DMA((ROW_RING,)), pltpu.SemaphoreType.DMA((ROW_RING,))],
    )
    return pl.pallas_call(
        functools.partial(_expert_kernel, n_blocks=n_blocks),
        grid_spec=grid_spec,
        out_shape=jax.ShapeDtypeStruct((n_blocks * ROW_BLOCK, D_MODEL), F32),
        compiler_params=_cparams(("arbitrary",)),
        name="experts",
    )(tab, xb, w_up, b_up, w_down, b_down)


def _split_bf16(a):
    hi = a.astype(BF16)
    return hi, (a - hi.astype(F32)).astype(BF16)


def _combine_kernel(tab_ref, nxt_ref, slot_ref, gate_ref, h2_ref, gfin_ref, yb_ref, y_ref, buf, sems,
                    *, tm):
    i = pl.program_id(0)
    n = pl.num_programs(0)
    cur = i % 2
    nrows = buf.shape[1]

    def granule(src_row, sl, dst_row):
        return pltpu.make_async_copy(
            yb_ref.at[pl.ds(pl.multiple_of(src_row, SUBLANES), SUBLANES), :],
            buf.at[sl, pl.ds(pl.multiple_of(dst_row, SUBLANES), SUBLANES), :], sems.at[sl])

    def gather(t_ref, sl):
        for e in range(N_EXPERTS):
            src = t_ref[e]
            dst = t_ref[2 * N_EXPERTS + e]

            def issue(g, c):
                granule(src + g * SUBLANES, sl, dst + g * SUBLANES).start()
                return c
            lax.fori_loop(0, t_ref[N_EXPERTS + e], issue, 0)

    @pl.when(i == 0)
    def _():
        buf[...] = jnp.zeros_like(buf)
        gather(tab_ref, 0)

    @pl.when(i + 1 < n)
    def _():
        gather(nxt_ref, 1 - cur)

    def drain(g, c):
        granule(0, cur, 0).wait()
        return c
    lax.fori_loop(0, tab_ref[3 * N_EXPERTS], drain, 0)

    s_iota = lax.broadcasted_iota(I32, (tm, nrows), 1)
    wgt = jnp.zeros((tm, nrows), F32)
    for kk in range(TOP_K):
        wgt = wgt + jnp.where(s_iota == slot_ref[:, kk:kk + 1], gate_ref[:, kk:kk + 1], 0.0)
    w_hi, w_lo = _split_bf16(wgt)
    y_hi, y_lo = _split_bf16(buf[cur])
    acc = _dot(w_hi, y_hi) + (_dot(w_hi, y_lo) + _dot(w_lo, y_hi))
    y_ref[...] = _rms(h2_ref[...] + acc, gfin_ref[...])


def _combine(tab, slot_col, gate_col, h2, gfin, yb, *, tm):
    ntok = h2.shape[0]
    n = ntok // tm
    smem = lambda imap: pl.BlockSpec((TABLE_WIDTH,), imap, memory_space=pltpu.SMEM)
    return pl.pallas_call(
        functools.partial(_combine_kernel, tm=tm),
        grid=(n,),
        in_specs=[smem(lambda i: (i,)), smem(lambda i: (jnp.minimum(i + 1, n - 1),)),
                  pl.BlockSpec((tm, TOP_K), lambda i: (i, 0)),
                  pl.BlockSpec((tm, TOP_K), lambda i: (i, 0)),
                  pl.BlockSpec((tm, D_MODEL), lambda i: (i, 0)),
                  pl.BlockSpec((1, D_MODEL), lambda i: (0, 0)),
                  pl.BlockSpec(memory_space=pl.ANY)],
        out_specs=pl.BlockSpec((tm, D_MODEL), lambda i: (i, 0)),
        out_shape=jax.ShapeDtypeStruct((ntok, D_MODEL), F32),
        scratch_shapes=[pltpu.VMEM((2, _sorted_rows(tm), D_MODEL), F32),
                        pltpu.SemaphoreType.DMA((2,))],
        compiler_params=_cparams(("arbitrary",)),
        name="combine",
    )(tab, tab, slot_col, gate_col, h2, gfin, yb)


def _tiles(a, tile):
    bsz, kk, seq = a.shape
    return a.reshape(bsz, kk, seq // tile, tile).transpose(0, 2, 1, 3).reshape(-1, kk, tile)


def _path(x, cbuf, c0, n0, m0, mkb, mvb, wts, *, tm_in, ct, chunk, tm_post, sub, group):
    q, k, v, og, gcol, grow, yc, nbuf = _inproj(
        x, wts["g_mix"], wts["wq"], wts["wg"], wts["wgt"], wts["wc"], wts["bg"], wts["bgt"],
        wts["cw"], cbuf, tm=tm_in, chunk=chunk)
    ym, c1, n1, m1 = _mlstm(q, k, v, og, gcol, grow, c0, n0, m0, wts["ng"], ct=ct, chunk=chunk)
    h2, xn, eid, gate, rank, cnt = _post(
        ym, yc, x, wts["wmo"], wts["g_x"], wts["wxq"], mkb, mvb, wts["wxo"], wts["g_ffn"],
        wts["wrt"], wts["br"], tm=tm_post, sub=sub, group=group)
    return dict(h2=h2, xn=xn, eid=eid, gate=gate, rank=rank, cnt=cnt[:, :, 0],
                c1=c1, n1=n1, m1=m1[..., 0], nbuf=nbuf)


def kernel(x_prompt, x_sample, state_mlstm_c, state_mlstm_n, state_mlstm_m, state_conv, cache_mem_k, cache_mem_v, mem_prompt, norm_mix_g, w_in, b_gate, mlstm_norm_g, conv_w, w_mix_out, norm_x_g, norm_mem_g, w_xq, w_xk, w_xv, w_xo, norm_ffn_g, w_router, b_router, w_up, b_up, w_down, b_down, norm_final_g):
    bp, lp, _ = x_prompt.shape
    bs, ls, _ = x_sample.shape
    l = 0
    row = lambda a: a.reshape(1, -1)

    wi = w_in[l]
    gate_cols = wi[:, 4 * D_MLSTM:4 * D_MLSTM + 2 * N_HEADS]
    wts = dict(
        g_mix=row(norm_mix_g[l]),
        wq=wi[:, 0:4 * D_MLSTM].astype(BF16),
        wg=jnp.pad(gate_cols, ((0, 0), (0, 128 - 2 * N_HEADS))).astype(BF16),
        wgt=gate_cols.T.astype(BF16),
        wc=wi[:, 4 * D_MLSTM + 2 * N_HEADS:].astype(BF16),
        bg=row(b_gate[l]), bgt=b_gate[l].reshape(-1, 1),
        cw=conv_w[l], ng=row(mlstm_norm_g[l]),
        wmo=w_mix_out[l].astype(BF16), g_x=row(norm_x_g[l]), wxq=w_xq[l].astype(BF16),
        wxo=w_xo[l].astype(BF16), g_ffn=row(norm_ffn_g[l]),
        wrt=w_router[l].T.astype(BF16), br=b_router[l].reshape(-1, 1),
    )

    mk, mv, mkb, mvb = _memkv(mem_prompt.reshape(bp * N_MEM, D_MODEL), row(norm_mem_g[l]),
                              w_xk[l].astype(BF16), w_xv[l].astype(BF16))
    zeros = lambda *s: jnp.zeros(s, F32)
    tm_post, tmd = 512, 256
    pr = _path(x_prompt, zeros(bp, CONV_WIDTH - 1, D_CONV), zeros(bp, N_HEADS, HEAD_DIM, HEAD_DIM),
               zeros(bp, N_HEADS, HEAD_DIM), zeros(bp, N_HEADS, 1),
               mkb.reshape(bp, N_MEM, D_MODEL), mvb.reshape(bp, N_MEM, D_MODEL), wts,
               tm_in=512, ct=512, chunk=CHUNK, tm_post=tm_post, sub=tmd, group=1)
    sa = _path(x_sample, state_conv[l], state_mlstm_c[l], state_mlstm_n[l],
               state_mlstm_m[l][..., None],
               cache_mem_k[l].reshape(bs, N_MEM, D_MODEL).astype(BF16),
               cache_mem_v[l].reshape(bs, N_MEM, D_MODEL).astype(BF16), wts,
               tm_in=ls, ct=ls, chunk=min(CHUNK, ls), tm_post=ls, sub=ls, group=bs)

    n_ptok, n_stok = bp * lp, bs * ls
    cnt = jnp.concatenate([pr["cnt"], sa["cnt"]], axis=0)
    n_tiles = cnt.shape[0]
    n_blocks = -(-(TOP_K * (n_ptok + n_stok) + n_tiles * N_EXPERTS * (SUBLANES - 1)) // ROW_BLOCK) \
        + N_EXPERTS
    seg = (cnt + SUBLANES - 1) // SUBLANES * SUBLANES
    seg_src = jnp.cumsum(seg, axis=1) - seg
    tot = jnp.sum(seg, axis=0)
    blocks_e = (tot + ROW_BLOCK - 1) // ROW_BLOCK
    padded = blocks_e * ROW_BLOCK
    pstart = jnp.cumsum(padded) - padded
    bend = jnp.cumsum(blocks_e)
    n_used = bend[-1]
    exp_tab = jnp.concatenate([bend - blocks_e, blocks_e, n_used[None]]).astype(I32)
    seg_dst = pstart[None, :] + jnp.cumsum(seg, axis=0) - seg

    gran = seg // SUBLANES
    tab = jnp.concatenate([seg_dst, gran, seg_src, jnp.sum(gran, axis=1, keepdims=True)], axis=1)
    tab = jnp.pad(tab, ((0, 0), (0, TABLE_WIDTH - tab.shape[1]))).astype(I32).reshape(-1)
    fgran = (padded - tot) // SUBLANES
    fill = jnp.concatenate([pstart + tot, fgran, jnp.sum(fgran)[None], n_used[None]])
    fill = jnp.pad(fill, (0, TABLE_WIDTH - fill.shape[0])).astype(I32)

    def lookup(table, eid):
        hit = eid[..., None] == jnp.arange(N_EXPERTS, dtype=I32)
        return jnp.sum(jnp.where(hit, table[:, None, None, :], 0), axis=-1)

    eid_pt, rank_pt = _tiles(pr["eid"], tmd), _tiles(pr["rank"], tmd)
    eid_st = sa["eid"].transpose(1, 0, 2).reshape(1, TOP_K, n_stok)
    rank_st = sa["rank"].transpose(1, 0, 2).reshape(1, TOP_K, n_stok)
    slot_p = (lookup(seg_src[:-1], eid_pt) + rank_pt).transpose(0, 2, 1).reshape(n_ptok, TOP_K)
    slot_s = (lookup(seg_src[-1:], eid_st) + rank_st).transpose(0, 2, 1).reshape(n_stok, TOP_K)

    xb = _dispatch(tab, fill, eid_pt, rank_pt, seg_src.astype(F32)[..., None],
                   pr["xn"].reshape(n_ptok, D_MODEL), eid_st[0], rank_st[0],
                   sa["xn"].reshape(n_stok, D_MODEL), tmd=tmd, n_blocks=n_blocks)
    yb = _experts(exp_tab, xb, w_up[l], b_up[l][:, None, :], w_down[l], b_down[l][:, None, :],
                  n_blocks=n_blocks)

    gfin = row(norm_final_g)
    split = (n_tiles - 1) * TABLE_WIDTH
    y_p = _combine(tab[:split], slot_p.astype(I32), pr["gate"].transpose(0, 2, 1).reshape(n_ptok, TOP_K),
                   pr["h2"].reshape(n_ptok, D_MODEL), gfin, yb, tm=tmd)
    y_s = _combine(tab[split:], slot_s.astype(I32), sa["gate"].transpose(0, 2, 1).reshape(n_stok, TOP_K),
                   sa["h2"].reshape(n_stok, D_MODEL), gfin, yb, tm=n_stok)

    lead = lambda a: a[None]
    return (y_p.reshape(bp, lp, D_MODEL), y_s.reshape(bs, ls, D_MODEL),
            lead(pr["c1"]), lead(pr["n1"]), lead(pr["m1"]), lead(pr["nbuf"]),
            lead(mk.reshape(bp, N_MEM, N_XHEADS, XHEAD_DIM)),
            lead(mv.reshape(bp, N_MEM, N_XHEADS, XHEAD_DIM)),
            lead(sa["c1"]), lead(sa["n1"]), lead(sa["m1"]), lead(sa["nbuf"]))
```

```python
import functools

import jax
import jax.numpy as jnp
import numpy as np
from jax import lax
from jax.experimental import pallas as pl
from jax.experimental.pallas import tpu as pltpu

F32 = jnp.float32
BF16 = jnp.bfloat16
I32 = jnp.int32

D_MODEL = 1024
N_HEADS = 4
HEAD_DIM = 128
D_MLSTM = N_HEADS * HEAD_DIM
D_CONV = D_MODEL - D_MLSTM
CONV_WIDTH = 3
CHUNK = 64
N_MEM = 256
N_XHEADS = 4
XHEAD_DIM = D_MODEL // N_XHEADS
N_EXPERTS = 32
TOP_K = 4
D_FF = D_MODEL
SWIGLU_LIMIT = 7.0
SWIGLU_ALPHA = 1.702
EPS = 1e-5

SUBLANES = 8
TABLE_WIDTH = 128
ROW_BLOCK = 256
ROW_LOOKAHEAD = 2
ROW_RING = ROW_LOOKAHEAD + 1
VMEM_LIMIT = 56 * 1024 * 1024


def _cparams(sem):
    return pltpu.CompilerParams(dimension_semantics=sem, vmem_limit_bytes=VMEM_LIMIT)


def _rms(x, g):
    return x * lax.rsqrt(jnp.mean(x * x, axis=-1, keepdims=True) + EPS) * g


def _log_sigmoid(x):
    return -(jnp.maximum(-x, 0.0) + jnp.log1p(jnp.exp(-jnp.abs(x))))


def _sigmoid(x):
    return 1.0 / (1.0 + jnp.exp(-x))


def _dot(a, b):
    return jnp.dot(a, b, preferred_element_type=F32)


def _dot_nt(a, b):
    return lax.dot_general(a, b, (((1,), (1,)), ((), ())), preferred_element_type=F32)


def _dot_tn(a, b):
    return lax.dot_general(a, b, (((0,), (0,)), ((), ())), preferred_element_type=F32)


def _memkv_kernel(mem_ref, g_ref, wk_ref, wv_ref, mk_ref, mv_ref, mkb_ref, mvb_ref):
    mn = _rms(mem_ref[...], g_ref[...]).astype(BF16)
    mk = _dot(mn, wk_ref[...])
    mv = _dot(mn, wv_ref[...])
    mk_ref[...] = mk
    mv_ref[...] = mv
    mkb_ref[...] = mk.astype(BF16)
    mvb_ref[...] = mv.astype(BF16)


def _memkv(mem2d, g, wk, wv):
    rows = mem2d.shape[0]
    tm = N_MEM
    row_spec = pl.BlockSpec((tm, D_MODEL), lambda i: (i, 0))
    full = lambda shape: pl.BlockSpec(shape, lambda i: (0,) * len(shape))
    return pl.pallas_call(
        _memkv_kernel,
        grid=(rows // tm,),
        in_specs=[row_spec, full((1, D_MODEL)), full((D_MODEL, D_MODEL)), full((D_MODEL, D_MODEL))],
        out_specs=[row_spec, row_spec, row_spec, row_spec],
        out_shape=[jax.ShapeDtypeStruct((rows, D_MODEL), F32)] * 2
        + [jax.ShapeDtypeStruct((rows, D_MODEL), BF16)] * 2,
        compiler_params=_cparams(("arbitrary",)),
        name="memkv",
    )(mem2d, g, wk, wv)


def _inproj_kernel(x_ref, g_ref, wq_ref, wg_ref, wgt_ref, wc_ref, bg_ref, bgt_ref, cw_ref, cbuf_ref,
                   q_ref, k_ref, v_ref, og_ref, gcol_ref, grow_ref, yc_ref, nbuf_ref,
                   carry_ref, *, tm, chunk):
    j = pl.program_id(1)

    @pl.when(j == 0)
    def _():
        carry_ref[0:2, :] = cbuf_ref[...]

    xb = _rms(x_ref[...], g_ref[...]).astype(BF16)

    p = _dot(xb, wq_ref[...])
    q_ref[...] = p[:, 0:D_MLSTM].astype(BF16)
    k_ref[...] = (p[:, D_MLSTM:2 * D_MLSTM] * (HEAD_DIM ** -0.5)).astype(BF16)
    v_ref[...] = p[:, 2 * D_MLSTM:3 * D_MLSTM].astype(BF16)
    og_ref[...] = _sigmoid(p[:, 3 * D_MLSTM:4 * D_MLSTM])

    gc = _dot(xb, wg_ref[...])[:, 0:2 * N_HEADS] + bg_ref[...]
    col = lax.broadcasted_iota(I32, gc.shape, 1)
    gcol_ref[...] = jnp.where(col < N_HEADS, gc, _log_sigmoid(gc))
    gr = _dot_nt(wgt_ref[...], xb) + bgt_ref[...]
    row = lax.broadcasted_iota(I32, gr.shape, 0)
    gr = jnp.where(row < N_HEADS, gr, _log_sigmoid(gr))
    for c in range(tm // chunk):
        grow_ref[c] = gr[:, c * chunk:(c + 1) * chunk]

    pc = _dot(xb, wc_ref[...])
    cb = pc[:, 0:D_CONV]
    u = pc[:, D_CONV:2 * D_CONV] * pc[:, 2 * D_CONV:3 * D_CONV]
    c0 = carry_ref[0:1, :]
    c1 = carry_ref[1:2, :]
    rid = lax.broadcasted_iota(I32, u.shape, 0)
    um1 = jnp.where(rid == 0, c1, pltpu.roll(u, 1, 0))
    um2 = jnp.where(rid == 0, c0, jnp.where(rid == 1, c1, pltpu.roll(u, 2, 0)))
    yc = cw_ref[0:1, :] * um2 + cw_ref[1:2, :] * um1 + cw_ref[2:3, :] * u
    yc_ref[...] = (cb * yc).astype(BF16)
    tail = u[tm - 2:tm, :]
    carry_ref[0:2, :] = tail
    nbuf_ref[...] = tail


def _inproj(x, g, wq, wg, wgt, wc, bg, bgt, cw, cbuf, *, tm, chunk):
    bsz, seq, _ = x.shape
    grid = (bsz, seq // tm)
    tok = lambda c: pl.BlockSpec((None, tm, c), lambda b, j: (b, j, 0))
    full = lambda shape: pl.BlockSpec(shape, lambda b, j: (0,) * len(shape))
    nck = tm // chunk
    return pl.pallas_call(
        functools.partial(_inproj_kernel, tm=tm, chunk=chunk),
        grid=grid,
        in_specs=[tok(D_MODEL), full((1, D_MODEL)), full((D_MODEL, 4 * D_MLSTM)),
                  full((D_MODEL, 128)), full((8, D_MODEL)), full((D_MODEL, 3 * D_CONV)),
                  full((1, 8)), full((8, 1)), full((CONV_WIDTH, D_CONV)),
                  pl.BlockSpec((None, 2, D_CONV), lambda b, j: (b, 0, 0))],
        out_specs=[tok(D_MLSTM), tok(D_MLSTM), tok(D_MLSTM), tok(D_MLSTM), tok(8),
                   pl.BlockSpec((None, nck, 8, chunk), lambda b, j: (b, j, 0, 0)),
                   tok(D_CONV),
                   pl.BlockSpec((None, 2, D_CONV), lambda b, j: (b, 0, 0))],
        out_shape=[jax.ShapeDtypeStruct((bsz, seq, D_MLSTM), BF16)] * 3
        + [jax.ShapeDtypeStruct((bsz, seq, D_MLSTM), F32),
           jax.ShapeDtypeStruct((bsz, seq, 8), F32),
           jax.ShapeDtypeStruct((bsz, seq // chunk, 8, chunk), F32),
           jax.ShapeDtypeStruct((bsz, seq, D_CONV), BF16),
           jax.ShapeDtypeStruct((bsz, 2, D_CONV), F32)],
        scratch_shapes=[pltpu.VMEM((8, D_CONV), F32)],
        compiler_params=_cparams(("arbitrary", "arbitrary")),
        name="inproj",
    )(x, g, wq, wg, wgt, wc, bg, bgt, cw, cbuf)


def _mlstm_kernel(q_ref, k_ref, v_ref, og_ref, gc_ref, gr_ref, c0_ref, n0_ref, m0_ref, ng_ref,
                  ym_ref, c1_ref, n1_ref, m1_ref, c_s, n_s, m_s, *, chunk, nchunks, bsz):
    j = pl.program_id(0)

    @pl.when(j == 0)
    def _():
        c_s[...] = c0_ref[...]
        n_s[...] = n0_ref[...]
        m_s[...] = m0_ref[...]

    ti = lax.broadcasted_iota(I32, (chunk, chunk), 0)
    ji = lax.broadcasted_iota(I32, (chunk, chunk), 1)
    causal = ji <= ti

    def body(ci, carry):
        r0 = pl.multiple_of(ci * chunk, chunk)
        rows = pl.ds(r0, chunk)
        chains = [(b, h) for b in range(bsz) for h in range(N_HEADS)]
        cols = lambda h: slice(h * HEAD_DIM, (h + 1) * HEAD_DIM)
        each = lambda f: [f(n, b, h) for n, (b, h) in enumerate(chains)]
        q = lambda b, h: q_ref[b, rows, cols(h)]
        k = lambda b, h: k_ref[b, rows, cols(h)]
        v = lambda b, h: v_ref[b, rows, cols(h)]
        gcs = [gc_ref[b, rows, :] for b in range(bsz)]
        grs = [gr_ref[b, ci] for b in range(bsz)]
        li_c = each(lambda n, b, h: gcs[b][:, h:h + 1])
        lf_c = each(lambda n, b, h: gcs[b][:, N_HEADS + h:N_HEADS + h + 1])
        li_r = each(lambda n, b, h: grs[b][h:h + 1, :])
        lf_r = each(lambda n, b, h: grs[b][N_HEADS + h:N_HEADS + h + 1, :])
        m_prev = each(lambda n, b, h: m_s[b, h:h + 1, :])

        b_c = each(lambda n, b, h: jnp.sum(jnp.where(causal, lf_r[n], 0.0), axis=1, keepdims=True))
        b_r = each(lambda n, b, h: jnp.sum(jnp.where(ti <= ji, lf_c[n], 0.0), axis=0, keepdims=True))
        dmat = each(lambda n, b, h: jnp.where(causal, b_c[n] - b_r[n] + li_r[n], -jnp.inf))
        dmax = each(lambda n, b, h: jnp.max(dmat[n], axis=1, keepdims=True))
        inter = each(lambda n, b, h: b_c[n] + m_prev[n])
        m_t = each(lambda n, b, h: jnp.maximum(inter[n], dmax[n]))
        w_inter = each(lambda n, b, h: jnp.exp(inter[n] - m_t[n]))
        s = each(lambda n, b, h: _dot_nt(q(b, h), k(b, h)) * jnp.exp(dmat[n] - m_t[n]))
        qc = each(lambda n, b, h: _dot(q(b, h), c_s[b, h].astype(BF16)))
        sv = each(lambda n, b, h: _dot(s[n].astype(BF16), v(b, h)))
        qn = each(lambda n, b, h: jnp.sum(q(b, h).astype(F32) * n_s[b, h:h + 1, :], axis=1,
                                          keepdims=True))
        den = each(lambda n, b, h: w_inter[n] * qn[n] + jnp.sum(s[n], axis=1, keepdims=True))
        hh = each(lambda n, b, h: (w_inter[n] * qc[n] + sv[n])
                  / jnp.maximum(jnp.abs(den[n]), jnp.exp(-m_t[n])))

        m_new = each(lambda n, b, h: m_t[n][chunk - 1:chunk, :])
        b_last = each(lambda n, b, h: b_c[n][chunk - 1:chunk, :])
        decay = each(lambda n, b, h: jnp.exp(b_last[n] + m_prev[n] - m_new[n]))
        kw = each(lambda n, b, h: k(b, h).astype(F32)
                  * jnp.exp(b_last[n] - b_c[n] + li_c[n] - m_new[n]))
        kv = each(lambda n, b, h: _dot_tn(kw[n].astype(BF16), v(b, h)))
        for n, (b, h) in enumerate(chains):
            c_s[b, h] = decay[n] * c_s[b, h] + kv[n]
            n_s[b, h:h + 1, :] = decay[n] * n_s[b, h:h + 1, :] + jnp.sum(kw[n], axis=0, keepdims=True)
            m_s[b, h:h + 1, :] = m_new[n]

        hn = each(lambda n, b, h: hh[n] * lax.rsqrt(jnp.mean(hh[n] * hh[n], axis=1, keepdims=True) + EPS)
                  * ng_ref[:, cols(h)])
        for n, (b, h) in enumerate(chains):
            ym_ref[b, rows, cols(h)] = (hn[n] * og_ref[b, rows, cols(h)]).astype(BF16)
        return carry

    lax.fori_loop(0, nchunks, body, 0)

    @pl.when(j == pl.num_programs(0) - 1)
    def _():
        c1_ref[...] = c_s[...]
        n1_ref[...] = n_s[...]
        m1_ref[...] = m_s[...]


def _mlstm(q, k, v, og, gcol, grow, c0, n0, m0, ng, *, ct, chunk):
    bsz, seq, _ = q.shape
    nchunks = ct // chunk
    grid = (seq // ct,)
    tok = lambda c: pl.BlockSpec((bsz, ct, c), lambda j: (0, j, 0))
    st_c = pl.BlockSpec((bsz, N_HEADS, HEAD_DIM, HEAD_DIM), lambda j: (0, 0, 0, 0))
    st_n = pl.BlockSpec((bsz, N_HEADS, HEAD_DIM), lambda j: (0, 0, 0))
    st_m = pl.BlockSpec((bsz, N_HEADS, 1), lambda j: (0, 0, 0))
    return pl.pallas_call(
        functools.partial(_mlstm_kernel, chunk=chunk, nchunks=nchunks, bsz=bsz),
        grid=grid,
        in_specs=[tok(D_MLSTM), tok(D_MLSTM), tok(D_MLSTM), tok(D_MLSTM), tok(8),
                  pl.BlockSpec((bsz, nchunks, 8, chunk), lambda j: (0, j, 0, 0)),
                  st_c, st_n, st_m,
                  pl.BlockSpec((1, D_MLSTM), lambda j: (0, 0))],
        out_specs=[tok(D_MLSTM), st_c, st_n, st_m],
        out_shape=[jax.ShapeDtypeStruct((bsz, seq, D_MLSTM), BF16),
                   jax.ShapeDtypeStruct((bsz, N_HEADS, HEAD_DIM, HEAD_DIM), F32),
                   jax.ShapeDtypeStruct((bsz, N_HEADS, HEAD_DIM), F32),
                   jax.ShapeDtypeStruct((bsz, N_HEADS, 1), F32)],
        scratch_shapes=[pltpu.VMEM((bsz, N_HEADS, HEAD_DIM, HEAD_DIM), F32),
                        pltpu.VMEM((bsz, N_HEADS, HEAD_DIM), F32),
                        pltpu.VMEM((bsz, N_HEADS, 1), F32)],
        compiler_params=_cparams(("arbitrary",)),
        name="mlstm",
    )(q, k, v, og, gcol, grow, c0, n0, m0, ng)


def _post_kernel(ym_ref, yc_ref, x_ref, wmo_ref, gx_ref, wxq_ref, mk_ref, mv_ref, wxo_ref,
                 gf_ref, wrt_ref, br_ref,
                 h2_ref, xn_ref, eid_ref, gate_ref, rank_ref, cnt_ref, cnt_s, *, tm, sub, group):
    step = pl.program_id(0) * pl.num_programs(1) + pl.program_id(1)

    @pl.when(step % group == 0)
    def _():
        cnt_s[...] = jnp.zeros_like(cnt_s)

    mix = _dot(ym_ref[...], wmo_ref[0:D_MLSTM, :]) + _dot(yc_ref[...], wmo_ref[D_MLSTM:D_MODEL, :])
    h1 = x_ref[...] + mix

    xq = _dot(_rms(h1, gx_ref[...]).astype(BF16), wxq_ref[...]).astype(BF16)
    att = jnp.zeros((tm, D_MODEL), F32)
    for hd in range(N_XHEADS):
        cols = slice(hd * XHEAD_DIM, (hd + 1) * XHEAD_DIM)
        s = _dot_nt(xq[:, cols], mk_ref[:, cols]) * (XHEAD_DIM ** -0.5)
        e = jnp.exp(s - jnp.max(s, axis=-1, keepdims=True))
        p = (e / jnp.sum(e, axis=-1, keepdims=True)).astype(BF16)
        o = _dot(p, mv_ref[:, cols]).astype(BF16)
        att = att + _dot(o, wxo_ref[cols, :])
    h2 = h1 + att
    h2_ref[...] = h2

    xn2 = _rms(h2, gf_ref[...]).astype(BF16)
    xn_ref[...] = xn2

    logits = _dot_nt(wrt_ref[...], xn2) + br_ref[...]
    eidx = lax.broadcasted_iota(I32, logits.shape, 0).astype(F32)
    work = logits
    vals, ids, hots = [], [], []
    for _ in range(TOP_K):
        mx = jnp.max(work, axis=0, keepdims=True)
        idx = jnp.min(jnp.where(work == mx, eidx, float(N_EXPERTS)), axis=0, keepdims=True)
        sel = eidx == idx
        vals.append(mx)
        ids.append(idx)
        hots.append(sel)
        work = jnp.where(sel, -jnp.inf, work)
    exps = [jnp.exp(v - vals[0]) for v in vals]
    denom = exps[0] + exps[1] + exps[2] + exps[3]

    picked = jnp.zeros(logits.shape, F32)
    for sel in hots:
        picked = picked + sel.astype(F32)
    shift = jnp.full((tm, tm), sub.bit_length() - 1, I32)
    tj = lax.broadcasted_iota(I32, (tm, tm), 0)
    tt = lax.broadcasted_iota(I32, (tm, tm), 1)
    same = lax.shift_right_logical(tj, shift) == lax.shift_right_logical(tt, shift)
    before = jnp.where(jnp.logical_and(tj < tt, same), 1.0, 0.0).astype(BF16)
    prior = _dot(picked.astype(BF16), before) + cnt_s[...]
    for kk in range(TOP_K):
        eid_ref[kk:kk + 1, :] = ids[kk].astype(I32)
        gate_ref[kk:kk + 1, :] = exps[kk] / denom
        rank_ref[kk:kk + 1, :] = jnp.sum(jnp.where(hots[kk], prior, 0.0), axis=0,
                                         keepdims=True).astype(I32)
    for s in range(tm // sub):
        total = cnt_s[...] + jnp.sum(picked[:, s * sub:(s + 1) * sub], axis=1, keepdims=True)
        cnt_ref[s] = total.astype(I32)
    cnt_s[...] = total


def _post(ym, yc, x, wmo, gx, wxq, mkb, mvb, wxo, gf, wrt, br, *, tm, sub, group):
    bsz, seq, _ = x.shape
    nj = seq // tm
    grid = (bsz, nj)
    nsub = tm // sub
    n_tiles = bsz * nj * nsub // group
    tok = lambda c: pl.BlockSpec((None, tm, c), lambda b, j: (b, j, 0))
    full = lambda shape: pl.BlockSpec(shape, lambda b, j: (0,) * len(shape))
    mem = pl.BlockSpec((None, N_MEM, D_MODEL), lambda b, j: (b, 0, 0))
    sel = pl.BlockSpec((None, TOP_K, tm), lambda b, j: (b, 0, j))
    return pl.pallas_call(
        functools.partial(_post_kernel, tm=tm, sub=sub, group=group),
        grid=grid,
        in_specs=[tok(D_MLSTM), tok(D_CONV), tok(D_MODEL), full((D_MODEL, D_MODEL)),
                  full((1, D_MODEL)), full((D_MODEL, D_MODEL)), mem, mem,
                  full((D_MODEL, D_MODEL)), full((1, D_MODEL)), full((N_EXPERTS, D_MODEL)),
                  full((N_EXPERTS, 1))],
        out_specs=[tok(D_MODEL), tok(D_MODEL), sel, sel, sel,
                   pl.BlockSpec((nsub, N_EXPERTS, 1), lambda b, j: ((b * nj + j) // group, 0, 0))],
        out_shape=[jax.ShapeDtypeStruct((bsz, seq, D_MODEL), F32),
                   jax.ShapeDtypeStruct((bsz, seq, D_MODEL), BF16),
                   jax.ShapeDtypeStruct((bsz, TOP_K, seq), I32),
                   jax.ShapeDtypeStruct((bsz, TOP_K, seq), F32),
                   jax.ShapeDtypeStruct((bsz, TOP_K, seq), I32),
                   jax.ShapeDtypeStruct((n_tiles, N_EXPERTS, 1), I32)],
        scratch_shapes=[pltpu.VMEM((N_EXPERTS, 1), F32)],
        compiler_params=_cparams(("arbitrary", "arbitrary")),
        name="post",
    )(ym, yc, x, wmo, gx, wxq, mkb, mvb, wxo, gf, wrt, br)


def _sorted_rows(n_tokens):
    return -(-(TOP_K * n_tokens + N_EXPERTS * (SUBLANES - 1)) // ROW_BLOCK) * ROW_BLOCK


def _dispatch_kernel(tab_ref, fill_ref, eid_ref, rank_ref, ls_ref, x_ref, eids_ref, ranks_ref, xs_ref,
                     xb_ref, srt, zero_s, pending, sems, *, n_ptiles, n_blocks):
    i = pl.program_id(0)
    cur = i % 2

    def granule(src, src_row, dst_row, sl):
        return pltpu.make_async_copy(
            src.at[pl.ds(pl.multiple_of(src_row, SUBLANES), SUBLANES), :],
            xb_ref.at[pl.ds(pl.multiple_of(dst_row, SUBLANES), SUBLANES), :], sems.at[sl])

    def drain(count, sl):
        def body(g, c):
            granule(zero_s, 0, 0, sl).wait()
            return c
        lax.fori_loop(0, count, body, 0)

    @pl.when(i == 0)
    def _():
        pending[0] = 0

    def sort_and_move(eid, rank, x):
        ntok = x.shape[0]
        nrows = _sorted_rows(ntok)
        e_iota = lax.broadcasted_iota(I32, (N_EXPERTS, ntok), 0)
        s_iota = lax.broadcasted_iota(I32, (nrows, ntok), 0)
        seg_start = ls_ref[...]
        hit = None
        for kk in range(TOP_K):
            start = jnp.sum(jnp.where(e_iota == eid[kk:kk + 1, :], seg_start, 0.0),
                            axis=0, keepdims=True).astype(I32)
            match = s_iota == start + rank[kk:kk + 1, :]
            hit = match if hit is None else jnp.logical_or(hit, match)
        perm = jnp.where(hit, 1.0, 0.0).astype(BF16)
        srt[cur, 0:nrows, :] = _dot(perm, x)
        drain(pending[0], 1 - cur)
        for e in range(N_EXPERTS):
            dst = tab_ref[e]
            src = tab_ref[2 * N_EXPERTS + e]

            def issue(g, c):
                granule(srt.at[cur], src + g * SUBLANES, dst + g * SUBLANES, cur).start()
                return c
            lax.fori_loop(0, tab_ref[N_EXPERTS + e], issue, 0)
        pending[0] = tab_ref[3 * N_EXPERTS]

    @pl.when(i < n_ptiles)
    def _():
        sort_and_move(eid_ref[...], rank_ref[...], x_ref[...])

    @pl.when(i == n_ptiles)
    def _():
        sort_and_move(eids_ref[...], ranks_ref[...], xs_ref[...])
        drain(pending[0], cur)
        zero_s[...] = jnp.zeros_like(zero_s)
        for e in range(N_EXPERTS):
            dst = fill_ref[e]

            def zissue(g, c):
                granule(zero_s, 0, dst + g * SUBLANES, cur).start()
                return c
            lax.fori_loop(0, fill_ref[N_EXPERTS + e], zissue, 0)
        drain(fill_ref[2 * N_EXPERTS], cur)
        first_free = fill_ref[2 * N_EXPERTS + 1]

        def blk_copy(b):
            return pltpu.make_async_copy(
                zero_s, xb_ref.at[pl.ds(pl.multiple_of(b * ROW_BLOCK, ROW_BLOCK), ROW_BLOCK), :],
                sems.at[cur])

        def bissue(b, c):
            blk_copy(b).start()
            return c
        lax.fori_loop(first_free, n_blocks, bissue, 0)

        def bdrain(b, c):
            blk_copy(0).wait()
            return c
        lax.fori_loop(first_free, n_blocks, bdrain, 0)


def _dispatch(tab, fill, eid_p, rank_p, seg_start, xn_p, eid_s, rank_s, xn_s, *, tmd, n_blocks):
    n_ptiles = eid_p.shape[0]
    n_sample = xn_s.shape[0]
    last = n_ptiles - 1
    smem = lambda shape, imap: pl.BlockSpec(shape, imap, memory_space=pltpu.SMEM)
    return pl.pallas_call(
        functools.partial(_dispatch_kernel, n_ptiles=n_ptiles, n_blocks=n_blocks),
        grid=(n_ptiles + 1,),
        in_specs=[smem((TABLE_WIDTH,), lambda i: (i,)),
                  smem((TABLE_WIDTH,), lambda i: (0,)),
                  pl.BlockSpec((None, TOP_K, tmd), lambda i: (jnp.minimum(i, last), 0, 0)),
                  pl.BlockSpec((None, TOP_K, tmd), lambda i: (jnp.minimum(i, last), 0, 0)),
                  pl.BlockSpec((None, N_EXPERTS, 1), lambda i: (i, 0, 0)),
                  pl.BlockSpec((tmd, D_MODEL), lambda i: (jnp.minimum(i, last), 0)),
                  pl.BlockSpec((TOP_K, n_sample), lambda i: (0, 0)),
                  pl.BlockSpec((TOP_K, n_sample), lambda i: (0, 0)),
                  pl.BlockSpec((n_sample, D_MODEL), lambda i: (0, 0))],
        out_specs=pl.BlockSpec(memory_space=pl.ANY),
        out_shape=jax.ShapeDtypeStruct((n_blocks * ROW_BLOCK, D_MODEL), F32),
        scratch_shapes=[pltpu.VMEM((2, _sorted_rows(tmd), D_MODEL), F32),
                        pltpu.VMEM((ROW_BLOCK, D_MODEL), F32), pltpu.SMEM((1,), I32),
                        pltpu.SemaphoreType.DMA((2,))],
        compiler_params=_cparams(("arbitrary",)),
        name="dispatch",
    )(tab, fill, eid_p, rank_p, seg_start, xn_p, eid_s, rank_s, xn_s)


def _expert_kernel(tab_ref, xb_ref, wu_hbm, bu_ref, wd_hbm, bd_ref, yb_ref,
                   wu_f, wd_f, wu_s, wd_s, xbuf, ybuf, wsem, xsem, ysem, *, n_blocks):
    e = pl.program_id(0)
    first = tab_ref[e]
    nblk = tab_ref[N_EXPERTS + e]
    n_used = tab_ref[2 * N_EXPERTS]

    def hbm_rows(b):
        return pl.ds(pl.multiple_of(b * ROW_BLOCK, ROW_BLOCK), ROW_BLOCK)

    def x_copy(b, sl):
        return pltpu.make_async_copy(xb_ref.at[hbm_rows(b), :], xbuf.at[sl], xsem.at[sl])

    def y_copy(b, sl):
        return pltpu.make_async_copy(ybuf.at[sl], yb_ref.at[hbm_rows(b), :], ysem.at[sl])

    def w_copies(ex, sl):
        return (pltpu.make_async_copy(wu_hbm.at[ex], wu_f.at[sl], wsem.at[0, sl]),
                pltpu.make_async_copy(wd_hbm.at[ex], wd_f.at[sl], wsem.at[1, sl]))

    wslot = e % 2

    @pl.when(e == 0)
    def _():
        for c in w_copies(0, 0):
            c.start()
        for b in range(ROW_LOOKAHEAD):
            @pl.when(b < n_used)
            def _():
                x_copy(b, b % ROW_RING).start()

    @pl.when(e + 1 < N_EXPERTS)
    def _():
        for c in w_copies(e + 1, 1 - wslot):
            c.start()

    for c in w_copies(e, wslot):
        c.wait()

    step = 128

    def cast(r, c):
        rows = pl.ds(pl.multiple_of(r * step, step), step)
        wu_s[rows, :] = wu_f[wslot, rows, :].astype(BF16)
        wd_s[rows, :] = wd_f[wslot, rows, :].astype(BF16)
        return c
    lax.fori_loop(0, D_MODEL // step, cast, 0)

    def block(b, c):
        sl = b % ROW_RING
        x_copy(b, sl).wait()

        @pl.when(b + ROW_LOOKAHEAD < n_used)
        def _():
            x_copy(b + ROW_LOOKAHEAD, (b + ROW_LOOKAHEAD) % ROW_RING).start()

        @pl.when(b >= ROW_RING)
        def _():
            y_copy(b, sl).wait()

        hcat = _dot(xbuf[sl].astype(BF16), wu_s[...]) + bu_ref[...]
        glu = jnp.minimum(hcat[:, 0:D_FF], SWIGLU_LIMIT)
        lin = jnp.clip(hcat[:, D_FF:2 * D_FF], -SWIGLU_LIMIT, SWIGLU_LIMIT)
        act = glu * _sigmoid(SWIGLU_ALPHA * glu) * (lin + 1.0)
        ybuf[sl] = _dot(act.astype(BF16), wd_s[...]) + bd_ref[...]
        y_copy(b, sl).start()
        return c
    lax.fori_loop(first, first + nblk, block, 0)

    @pl.when(e == N_EXPERTS - 1)
    def _():
        for k in range(ROW_RING):
            @pl.when(k < n_used)
            def _():
                y_copy(0, (n_used - 1 - k) % ROW_RING).wait()

        ybuf[0] = jnp.zeros((ROW_BLOCK, D_MODEL), F32)

        def zissue(b, c):
            y_copy(b, 0).start()
            return c
        lax.fori_loop(n_used, n_blocks, zissue, 0)

        def zdrain(b, c):
            y_copy(0, 0).wait()
            return c
        lax.fori_loop(n_used, n_blocks, zdrain, 0)


def _experts(tab, xb, w_up, b_up, w_down, b_down, *, n_blocks):
    w_map = lambda e, tab: (e, 0, 0)
    grid_spec = pltpu.PrefetchScalarGridSpec(
        num_scalar_prefetch=1,
        grid=(N_EXPERTS,),
        in_specs=[pl.BlockSpec(memory_space=pl.ANY),
                  pl.BlockSpec(memory_space=pl.ANY),
                  pl.BlockSpec((None, 1, 2 * D_FF), w_map),
                  pl.BlockSpec(memory_space=pl.ANY),
                  pl.BlockSpec((None, 1, D_MODEL), w_map)],
        out_specs=pl.BlockSpec(memory_space=pl.ANY),
        scratch_shapes=[pltpu.VMEM((2, D_MODEL, 2 * D_FF), F32), pltpu.VMEM((2, D_FF, D_MODEL), F32),
                        pltpu.VMEM((D_MODEL, 2 * D_FF), BF16), pltpu.VMEM((D_FF, D_MODEL), BF16),
                        pltpu.VMEM((ROW_RING, ROW_BLOCK, D_MODEL), F32),
                        pltpu.VMEM((ROW_RING, ROW_BLOCK, D_MODEL), F32),
                        pltpu.SemaphoreType.DMA((2, 2)),
                        pltpu.SemaphoreType.DMA((ROW_RING,)), pltpu.SemaphoreType.DMA((ROW_RING,))],
    )
    return pl.pallas_call(
        functools.partial(_expert_kernel, n_blocks=n_blocks),
        grid_spec=grid_spec,
        out_shape=jax.ShapeDtypeStruct((n_blocks * ROW_BLOCK, D_MODEL), F32),
        compiler_params=_cparams(("arbitrary",)),
        name="experts",
    )(tab, xb, w_up, b_up, w_down, b_down)


def _split_bf16(a):
    hi = a.astype(BF16)
    return hi, (a - hi.astype(F32)).astype(BF16)


def _combine_kernel(tab_ref, nxt_ref, slot_ref, gate_ref, h2_ref, gfin_ref, yb_ref, y_ref, buf, sems,
                    *, tm):
    i = pl.program_id(0)
    n = pl.num_programs(0)
    cur = i % 2
    nrows = buf.shape[1]

    def granule(src_row, sl, dst_row):
        return pltpu.make_async_copy(
            yb_ref.at[pl.ds(pl.multiple_of(src_row, SUBLANES), SUBLANES), :],
            buf.at[sl, pl.ds(pl.multiple_of(dst_row, SUBLANES), SUBLANES), :], sems.at[sl])

    def gather(t_ref, sl):
        for e in range(N_EXPERTS):
            src = t_ref[e]
            dst = t_ref[2 * N_EXPERTS + e]

            def issue(g, c):
                granule(src + g * SUBLANES, sl, dst + g * SUBLANES).start()
                return c
            lax.fori_loop(0, t_ref[N_EXPERTS + e], issue, 0)

    @pl.when(i == 0)
    def _():
        buf[...] = jnp.zeros_like(buf)
        gather(tab_ref, 0)

    @pl.when(i + 1 < n)
    def _():
        gather(nxt_ref, 1 - cur)

    def drain(g, c):
        granule(0, cur, 0).wait()
        return c
    lax.fori_loop(0, tab_ref[3 * N_EXPERTS], drain, 0)

    s_iota = lax.broadcasted_iota(I32, (tm, nrows), 1)
    wgt = jnp.zeros((tm, nrows), F32)
    for kk in range(TOP_K):
        wgt = wgt + jnp.where(s_iota == slot_ref[:, kk:kk + 1], gate_ref[:, kk:kk + 1], 0.0)
    w_hi, w_lo = _split_bf16(wgt)
    y_hi, y_lo = _split_bf16(buf[cur])
    acc = _dot(w_hi, y_hi) + (_dot(w_hi, y_lo) + _dot(w_lo, y_hi))
    y_ref[...] = _rms(h2_ref[...] + acc, gfin_ref[...])


def _combine(tab, slot_col, gate_col, h2, gfin, yb, *, tm):
    ntok = h2.shape[0]
    n = ntok // tm
    smem = lambda imap: pl.BlockSpec((TABLE_WIDTH,), imap, memory_space=pltpu.SMEM)
    return pl.pallas_call(
        functools.partial(_combine_kernel, tm=tm),
        grid=(n,),
        in_specs=[smem(lambda i: (i,)), smem(lambda i: (jnp.minimum(i + 1, n - 1),)),
                  pl.BlockSpec((tm, TOP_K), lambda i: (i, 0)),
                  pl.BlockSpec((tm, TOP_K), lambda i: (i, 0)),
                  pl.BlockSpec((tm, D_MODEL), lambda i: (i, 0)),
                  pl.BlockSpec((1, D_MODEL), lambda i: (0, 0)),
                  pl.BlockSpec(memory_space=pl.ANY)],
        out_specs=pl.BlockSpec((tm, D_MODEL), lambda i: (i, 0)),
        out_shape=jax.ShapeDtypeStruct((ntok, D_MODEL), F32),
        scratch_shapes=[pltpu.VMEM((2, _sorted_rows(tm), D_MODEL), F32),
                        pltpu.SemaphoreType.DMA((2,))],
        compiler_params=_cparams(("arbitrary",)),
        name="combine",
    )(tab, tab, slot_col, gate_col, h2, gfin, yb)


def _tiles(a, tile):
    bsz, kk, seq = a.shape
    return a.reshape(bsz, kk, seq // tile, tile).transpose(0, 2, 1, 3).reshape(-1, kk, tile)


def _path(x, cbuf, c0, n0, m0, mkb, mvb, wts, *, tm_in, ct, chunk, tm_post, sub, group):
    q, k, v, og, gcol, grow, yc, nbuf = _inproj(
        x, wts["g_mix"], wts["wq"], wts["wg"], wts["wgt"], wts["wc"], wts["bg"], wts["bgt"],
        wts["cw"], cbuf, tm=tm_in, chunk=chunk)
    ym, c1, n1, m1 = _mlstm(q, k, v, og, gcol, grow, c0, n0, m0, wts["ng"], ct=ct, chunk=chunk)
    h2, xn, eid, gate, rank, cnt = _post(
        ym, yc, x, wts["wmo"], wts["g_x"], wts["wxq"], mkb, mvb, wts["wxo"], wts["g_ffn"],
        wts["wrt"], wts["br"], tm=tm_post, sub=sub, group=group)
    return dict(h2=h2, xn=xn, eid=eid, gate=gate, rank=rank, cnt=cnt[:, :, 0],
                c1=c1, n1=n1, m1=m1[..., 0], nbuf=nbuf)


def kernel(x_prompt, x_sample, state_mlstm_c, state_mlstm_n, state_mlstm_m, state_conv, cache_mem_k, cache_mem_v, mem_prompt, norm_mix_g, w_in, b_gate, mlstm_norm_g, conv_w, w_mix_out, norm_x_g, norm_mem_g, w_xq, w_xk, w_xv, w_xo, norm_ffn_g, w_router, b_router, w_up, b_up, w_down, b_down, norm_final_g):
    bp, lp, _ = x_prompt.shape
    bs, ls, _ = x_sample.shape
    l = 0
    row = lambda a: a.reshape(1, -1)

    wi = w_in[l]
    gate_cols = wi[:, 4 * D_MLSTM:4 * D_MLSTM + 2 * N_HEADS]
    wts = dict(
        g_mix=row(norm_mix_g[l]),
        wq=wi[:, 0:4 * D_MLSTM].astype(BF16),
        wg=jnp.pad(gate_cols, ((0, 0), (0, 128 - 2 * N_HEADS))).astype(BF16),
        wgt=gate_cols.T.astype(BF16),
        wc=wi[:, 4 * D_MLSTM + 2 * N_HEADS:].astype(BF16),
        bg=row(b_gate[l]), bgt=b_gate[l].reshape(-1, 1),
        cw=conv_w[l], ng=row(mlstm_norm_g[l]),
        wmo=w_mix_out[l].astype(BF16), g_x=row(norm_x_g[l]), wxq=w_xq[l].astype(BF16),
        wxo=w_xo[l].astype(BF16), g_ffn=row(norm_ffn_g[l]),
        wrt=w_router[l].T.astype(BF16), br=b_router[l].reshape(-1, 1),
    )

    mk, mv, mkb, mvb = _memkv(mem_prompt.reshape(bp * N_MEM, D_MODEL), row(norm_mem_g[l]),
                              w_xk[l].astype(BF16), w_xv[l].astype(BF16))
    zeros = lambda *s: jnp.zeros(s, F32)
    tm_post, tmd = 512, 256
    pr = _path(x_prompt, zeros(bp, CONV_WIDTH - 1, D_CONV), zeros(bp, N_HEADS, HEAD_DIM, HEAD_DIM),
               zeros(bp, N_HEADS, HEAD_DIM), zeros(bp, N_HEADS, 1),
               mkb.reshape(bp, N_MEM, D_MODEL), mvb.reshape(bp, N_MEM, D_MODEL), wts,
               tm_in=512, ct=512, chunk=CHUNK, tm_post=tm_post, sub=tmd, group=1)
    sa = _path(x_sample, state_conv[l], state_mlstm_c[l], state_mlstm_n[l],
               state_mlstm_m[l][..., None],
               cache_mem_k[l].reshape(bs, N_MEM, D_MODEL).astype(BF16),
               cache_mem_v[l].reshape(bs, N_MEM, D_MODEL).astype(BF16), wts,
               tm_in=ls, ct=ls, chunk=min(CHUNK, ls), tm_post=ls, sub=ls, group=bs)

    n_ptok, n_stok = bp * lp, bs * ls
    cnt = jnp.concatenate([pr["cnt"], sa["cnt"]], axis=0)
    n_tiles = cnt.shape[0]
    n_blocks = -(-(TOP_K * (n_ptok + n_stok) + n_tiles * N_EXPERTS * (SUBLANES - 1)) // ROW_BLOCK) \
        + N_EXPERTS
    seg = (cnt + SUBLANES - 1) // SUBLANES * SUBLANES
    seg_src = jnp.cumsum(seg, axis=1) - seg
    tot = jnp.sum(seg, axis=0)
    blocks_e = (tot + ROW_BLOCK - 1) // ROW_BLOCK
    padded = blocks_e * ROW_BLOCK
    pstart = jnp.cumsum(padded) - padded
    bend = jnp.cumsum(blocks_e)
    n_used = bend[-1]
    exp_tab = jnp.concatenate([bend - blocks_e, blocks_e, n_used[None]]).astype(I32)
    seg_dst = pstart[None, :] + jnp.cumsum(seg, axis=0) - seg

    gran = seg // SUBLANES
    tab = jnp.concatenate([seg_dst, gran, seg_src, jnp.sum(gran, axis=1, keepdims=True)], axis=1)
    tab = jnp.pad(tab, ((0, 0), (0, TABLE_WIDTH - tab.shape[1]))).astype(I32).reshape(-1)
    fgran = (padded - tot) // SUBLANES
    fill = jnp.concatenate([pstart + tot, fgran, jnp.sum(fgran)[None], n_used[None]])
    fill = jnp.pad(fill, (0, TABLE_WIDTH - fill.shape[0])).astype(I32)

    def lookup(table, eid):
        hit = eid[..., None] == jnp.arange(N_EXPERTS, dtype=I32)
        return jnp.sum(jnp.where(hit, table[:, None, None, :], 0), axis=-1)

    eid_pt, rank_pt = _tiles(pr["eid"], tmd), _tiles(pr["rank"], tmd)
    eid_st = sa["eid"].transpose(1, 0, 2).reshape(1, TOP_K, n_stok)
    rank_st = sa["rank"].transpose(1, 0, 2).reshape(1, TOP_K, n_stok)
    slot_p = (lookup(seg_src[:-1], eid_pt) + rank_pt).transpose(0, 2, 1).reshape(n_ptok, TOP_K)
    slot_s = (lookup(seg_src[-1:], eid_st) + rank_st).transpose(0, 2, 1).reshape(n_stok, TOP_K)

    xb = _dispatch(tab, fill, eid_pt, rank_pt, seg_src.astype(F32)[..., None],
                   pr["xn"].reshape(n_ptok, D_MODEL), eid_st[0], rank_st[0],
                   sa["xn"].reshape(n_stok, D_MODEL), tmd=tmd, n_blocks=n_blocks)
    yb = _experts(exp_tab, xb, w_up[l], b_up[l][:, None, :], w_down[l], b_down[l][:, None, :],
                  n_blocks=n_blocks)

    gfin = row(norm_final_g)
    split = (n_tiles - 1) * TABLE_WIDTH
    y_p = _combine(tab[:split], slot_p.astype(I32), pr["gate"].transpose(0, 2, 1).reshape(n_ptok, TOP_K),
                   pr["h2"].reshape(n_ptok, D_MODEL), gfin, yb, tm=tmd)
    y_s = _combine(tab[split:], slot_s.astype(I32), sa["gate"].transpose(0, 2, 1).reshape(n_stok, TOP_K),
                   sa["h2"].reshape(n_stok, D_MODEL), gfin, yb, tm=n_stok)

    lead = lambda a: a[None]
    return (y_p.reshape(bp, lp, D_MODEL), y_s.reshape(bs, ls, D_MODEL),
            lead(pr["c1"]), lead(pr["n1"]), lead(pr["m1"]), lead(pr["nbuf"]),
            lead(mk.reshape(bp, N_MEM, N_XHEADS, XHEAD_DIM)),
            lead(mv.reshape(bp, N_MEM, N_XHEADS, XHEAD_DIM)),
            lead(sa["c1"]), lead(sa["n1"]), lead(sa["m1"]), lead(sa["nbuf"]))
```

```python
import functools

import jax
import jax.numpy as jnp
import numpy as np
from jax import lax
from jax.experimental import pallas as pl
from jax.experimental.pallas import tpu as pltpu

F32 = jnp.float32
BF16 = jnp.bfloat16
I32 = jnp.int32

D_MODEL = 1024
N_HEADS = 4
HEAD_DIM = 128
D_MLSTM = N_HEADS * HEAD_DIM
D_CONV = D_MODEL - D_MLSTM
CONV_WIDTH = 3
CHUNK = 64
N_MEM = 256
N_XHEADS = 4
XHEAD_DIM = D_MODEL // N_XHEADS
N_EXPERTS = 32
TOP_K = 4
D_FF = D_MODEL
SWIGLU_LIMIT = 7.0
SWIGLU_ALPHA = 1.702
EPS = 1e-5

SUBLANES = 8
TABLE_WIDTH = 256
ROW_BLOCK = 256
ROW_LOOKAHEAD = 3
ROW_RING = ROW_LOOKAHEAD + 1
VMEM_LIMIT = 56 * 1024 * 1024


def _cparams(sem):
    return pltpu.CompilerParams(dimension_semantics=sem, vmem_limit_bytes=VMEM_LIMIT)


def _rms(x, g):
    return x * lax.rsqrt(jnp.mean(x * x, axis=-1, keepdims=True) + EPS) * g


def _log_sigmoid(x):
    return -(jnp.maximum(-x, 0.0) + jnp.log1p(jnp.exp(-jnp.abs(x))))


def _sigmoid(x):
    return 1.0 / (1.0 + jnp.exp(-x))


def _dot(a, b):
    return jnp.dot(a, b, preferred_element_type=F32)


def _dot_nt(a, b):
    return lax.dot_general(a, b, (((1,), (1,)), ((), ())), preferred_element_type=F32)


def _dot_tn(a, b):
    return lax.dot_general(a, b, (((0,), (0,)), ((), ())), preferred_element_type=F32)


def _memkv_kernel(mem_ref, g_ref, wk_ref, wv_ref, mk_ref, mv_ref, mkb_ref, mvb_ref):
    mn = _rms(mem_ref[...], g_ref[...]).astype(BF16)
    mk = _dot(mn, wk_ref[...])
    mv = _dot(mn, wv_ref[...])
    mk_ref[...] = mk
    mv_ref[...] = mv
    mkb_ref[...] = mk.astype(BF16)
    mvb_ref[...] = mv.astype(BF16)


def _memkv(mem2d, g, wk, wv):
    rows = mem2d.shape[0]
    tm = N_MEM
    row_spec = pl.BlockSpec((tm, D_MODEL), lambda i: (i, 0))
    full = lambda shape: pl.BlockSpec(shape, lambda i: (0,) * len(shape))
    return pl.pallas_call(
        _memkv_kernel,
        grid=(rows // tm,),
        in_specs=[row_spec, full((1, D_MODEL)), full((D_MODEL, D_MODEL)), full((D_MODEL, D_MODEL))],
        out_specs=[row_spec, row_spec, row_spec, row_spec],
        out_shape=[jax.ShapeDtypeStruct((rows, D_MODEL), F32)] * 2
        + [jax.ShapeDtypeStruct((rows, D_MODEL), BF16)] * 2,
        compiler_params=_cparams(("arbitrary",)),
        name="memkv",
    )(mem2d, g, wk, wv)


def _inproj_kernel(x_ref, g_ref, wq_ref, wg_ref, wgt_ref, wc_ref, bg_ref, bgt_ref, cw_ref, cbuf_ref,
                   q_ref, k_ref, v_ref, og_ref, gcol_ref, grow_ref, yc_ref, nbuf_ref,
                   carry_ref, *, tm, chunk):
    j = pl.program_id(1)

    @pl.when(j == 0)
    def _():
        carry_ref[0:2, :] = cbuf_ref[...]

    xb = _rms(x_ref[...], g_ref[...]).astype(BF16)

    p = _dot(xb, wq_ref[...])
    q_ref[...] = p[:, 0:D_MLSTM].astype(BF16)
    k_ref[...] = (p[:, D_MLSTM:2 * D_MLSTM] * (HEAD_DIM ** -0.5)).astype(BF16)
    v_ref[...] = p[:, 2 * D_MLSTM:3 * D_MLSTM].astype(BF16)
    og_ref[...] = _sigmoid(p[:, 3 * D_MLSTM:4 * D_MLSTM])

    gc = _dot(xb, wg_ref[...])[:, 0:2 * N_HEADS] + bg_ref[...]
    col = lax.broadcasted_iota(I32, gc.shape, 1)
    gcol_ref[...] = jnp.where(col < N_HEADS, gc, _log_sigmoid(gc))
    gr = _dot_nt(wgt_ref[...], xb) + bgt_ref[...]
    row = lax.broadcasted_iota(I32, gr.shape, 0)
    gr = jnp.where(row < N_HEADS, gr, _log_sigmoid(gr))
    for c in range(tm // chunk):
        grow_ref[c] = gr[:, c * chunk:(c + 1) * chunk]

    pc = _dot(xb, wc_ref[...])
    cb = pc[:, 0:D_CONV]
    u = pc[:, D_CONV:2 * D_CONV] * pc[:, 2 * D_CONV:3 * D_CONV]
    c0 = carry_ref[0:1, :]
    c1 = carry_ref[1:2, :]
    rid = lax.broadcasted_iota(I32, u.shape, 0)
    um1 = jnp.where(rid == 0, c1, pltpu.roll(u, 1, 0))
    um2 = jnp.where(rid == 0, c0, jnp.where(rid == 1, c1, pltpu.roll(u, 2, 0)))
    yc = cw_ref[0:1, :] * um2 + cw_ref[1:2, :] * um1 + cw_ref[2:3, :] * u
    yc_ref[...] = (cb * yc).astype(BF16)
    tail = u[tm - 2:tm, :]
    carry_ref[0:2, :] = tail
    nbuf_ref[...] = tail


def _inproj(x, g, wq, wg, wgt, wc, bg, bgt, cw, cbuf, *, tm, chunk):
    bsz, seq, _ = x.shape
    grid = (bsz, seq // tm)
    tok = lambda c: pl.BlockSpec((None, tm, c), lambda b, j: (b, j, 0))
    full = lambda shape: pl.BlockSpec(shape, lambda b, j: (0,) * len(shape))
    nck = tm // chunk
    return pl.pallas_call(
        functools.partial(_inproj_kernel, tm=tm, chunk=chunk),
        grid=grid,
        in_specs=[tok(D_MODEL), full((1, D_MODEL)), full((D_MODEL, 4 * D_MLSTM)),
                  full((D_MODEL, 128)), full((8, D_MODEL)), full((D_MODEL, 3 * D_CONV)),
                  full((1, 8)), full((8, 1)), full((CONV_WIDTH, D_CONV)),
                  pl.BlockSpec((None, 2, D_CONV), lambda b, j: (b, 0, 0))],
        out_specs=[tok(D_MLSTM), tok(D_MLSTM), tok(D_MLSTM), tok(D_MLSTM), tok(8),
                   pl.BlockSpec((None, nck, 8, chunk), lambda b, j: (b, j, 0, 0)),
                   tok(D_CONV),
                   pl.BlockSpec((None, 2, D_CONV), lambda b, j: (b, 0, 0))],
        out_shape=[jax.ShapeDtypeStruct((bsz, seq, D_MLSTM), BF16)] * 3
        + [jax.ShapeDtypeStruct((bsz, seq, D_MLSTM), F32),
           jax.ShapeDtypeStruct((bsz, seq, 8), F32),
           jax.ShapeDtypeStruct((bsz, seq // chunk, 8, chunk), F32),
           jax.ShapeDtypeStruct((bsz, seq, D_CONV), BF16),
           jax.ShapeDtypeStruct((bsz, 2, D_CONV), F32)],
        scratch_shapes=[pltpu.VMEM((8, D_CONV), F32)],
        compiler_params=_cparams(("arbitrary", "arbitrary")),
        name="inproj",
    )(x, g, wq, wg, wgt, wc, bg, bgt, cw, cbuf)


def _mlstm_kernel(q_ref, k_ref, v_ref, og_ref, gc_ref, gr_ref, c0_ref, n0_ref, m0_ref, ng_ref,
                  ym_ref, c1_ref, n1_ref, m1_ref, c_s, n_s, m_s, *, chunk, nchunks, bsz):
    j = pl.program_id(0)

    @pl.when(j == 0)
    def _():
        c_s[...] = c0_ref[...]
        n_s[...] = n0_ref[...]
        m_s[...] = m0_ref[...]

    ti = lax.broadcasted_iota(I32, (chunk, chunk), 0)
    ji = lax.broadcasted_iota(I32, (chunk, chunk), 1)
    causal = ji <= ti

    def body(ci, carry):
        r0 = pl.multiple_of(ci * chunk, chunk)
        rows = pl.ds(r0, chunk)
        chains = [(b, h) for b in range(bsz) for h in range(N_HEADS)]
        cols = lambda h: slice(h * HEAD_DIM, (h + 1) * HEAD_DIM)
        each = lambda f: [f(n, b, h) for n, (b, h) in enumerate(chains)]
        q = lambda b, h: q_ref[b, rows, cols(h)]
        k = lambda b, h: k_ref[b, rows, cols(h)]
        v = lambda b, h: v_ref[b, rows, cols(h)]
        gcs = [gc_ref[b, rows, :] for b in range(bsz)]
        grs = [gr_ref[b, ci] for b in range(bsz)]
        li_c = each(lambda n, b, h: gcs[b][:, h:h + 1])
        lf_c = each(lambda n, b, h: gcs[b][:, N_HEADS + h:N_HEADS + h + 1])
        li_r = each(lambda n, b, h: grs[b][h:h + 1, :])
        lf_r = each(lambda n, b, h: grs[b][N_HEADS + h:N_HEADS + h + 1, :])
        m_prev = each(lambda n, b, h: m_s[b, h:h + 1, :])

        b_c = each(lambda n, b, h: jnp.sum(jnp.where(causal, lf_r[n], 0.0), axis=1, keepdims=True))
        b_r = each(lambda n, b, h: jnp.sum(jnp.where(ti <= ji, lf_c[n], 0.0), axis=0, keepdims=True))
        dmat = each(lambda n, b, h: jnp.where(causal, b_c[n] - b_r[n] + li_r[n], -jnp.inf))
        dmax = each(lambda n, b, h: jnp.max(dmat[n], axis=1, keepdims=True))
        inter = each(lambda n, b, h: b_c[n] + m_prev[n])
        m_t = each(lambda n, b, h: jnp.maximum(inter[n], dmax[n]))
        w_inter = each(lambda n, b, h: jnp.exp(inter[n] - m_t[n]))
        s = each(lambda n, b, h: _dot_nt(q(b, h), k(b, h)) * jnp.exp(dmat[n] - m_t[n]))
        qc = each(lambda n, b, h: _dot(q(b, h), c_s[b, h].astype(BF16)))
        sv = each(lambda n, b, h: _dot(s[n].astype(BF16), v(b, h)))
        qn = each(lambda n, b, h: jnp.sum(q(b, h).astype(F32) * n_s[b, h:h + 1, :], axis=1,
                                          keepdims=True))
        den = each(lambda n, b, h: w_inter[n] * qn[n] + jnp.sum(s[n], axis=1, keepdims=True))
        hh = each(lambda n, b, h: (w_inter[n] * qc[n] + sv[n])
                  / jnp.maximum(jnp.abs(den[n]), jnp.exp(-m_t[n])))

        m_new = each(lambda n, b, h: m_t[n][chunk - 1:chunk, :])
        b_last = each(lambda n, b, h: b_c[n][chunk - 1:chunk, :])
        decay = each(lambda n, b, h: jnp.exp(b_last[n] + m_prev[n] - m_new[n]))
        kw = each(lambda n, b, h: k(b, h).astype(F32)
                  * jnp.exp(b_last[n] - b_c[n] + li_c[n] - m_new[n]))
        kv = each(lambda n, b, h: _dot_tn(kw[n].astype(BF16), v(b, h)))
        for n, (b, h) in enumerate(chains):
            c_s[b, h] = decay[n] * c_s[b, h] + kv[n]
            n_s[b, h:h + 1, :] = decay[n] * n_s[b, h:h + 1, :] + jnp.sum(kw[n], axis=0, keepdims=True)
            m_s[b, h:h + 1, :] = m_new[n]

        hn = each(lambda n, b, h: hh[n] * lax.rsqrt(jnp.mean(hh[n] * hh[n], axis=1, keepdims=True) + EPS)
                  * ng_ref[:, cols(h)])
        for n, (b, h) in enumerate(chains):
            ym_ref[b, rows, cols(h)] = (hn[n] * og_ref[b, rows, cols(h)]).astype(BF16)
        return carry

    lax.fori_loop(0, nchunks, body, 0)

    @pl.when(j == pl.num_programs(0) - 1)
    def _():
        c1_ref[...] = c_s[...]
        n1_ref[...] = n_s[...]
        m1_ref[...] = m_s[...]


def _mlstm(q, k, v, og, gcol, grow, c0, n0, m0, ng, *, ct, chunk):
    bsz, seq, _ = q.shape
    nchunks = ct // chunk
    grid = (seq // ct,)
    tok = lambda c: pl.BlockSpec((bsz, ct, c), lambda j: (0, j, 0))
    st_c = pl.BlockSpec((bsz, N_HEADS, HEAD_DIM, HEAD_DIM), lambda j: (0, 0, 0, 0))
    st_n = pl.BlockSpec((bsz, N_HEADS, HEAD_DIM), lambda j: (0, 0, 0))
    st_m = pl.BlockSpec((bsz, N_HEADS, 1), lambda j: (0, 0, 0))
    return pl.pallas_call(
        functools.partial(_mlstm_kernel, chunk=chunk, nchunks=nchunks, bsz=bsz),
        grid=grid,
        in_specs=[tok(D_MLSTM), tok(D_MLSTM), tok(D_MLSTM), tok(D_MLSTM), tok(8),
                  pl.BlockSpec((bsz, nchunks, 8, chunk), lambda j: (0, j, 0, 0)),
                  st_c, st_n, st_m,
                  pl.BlockSpec((1, D_MLSTM), lambda j: (0, 0))],
        out_specs=[tok(D_MLSTM), st_c, st_n, st_m],
        out_shape=[jax.ShapeDtypeStruct((bsz, seq, D_MLSTM), BF16),
                   jax.ShapeDtypeStruct((bsz, N_HEADS, HEAD_DIM, HEAD_DIM), F32),
                   jax.ShapeDtypeStruct((bsz, N_HEADS, HEAD_DIM), F32),
                   jax.ShapeDtypeStruct((bsz, N_HEADS, 1), F32)],
        scratch_shapes=[pltpu.VMEM((bsz, N_HEADS, HEAD_DIM, HEAD_DIM), F32),
                        pltpu.VMEM((bsz, N_HEADS, HEAD_DIM), F32),
                        pltpu.VMEM((bsz, N_HEADS, 1), F32)],
        compiler_params=_cparams(("arbitrary",)),
        name="mlstm",
    )(q, k, v, og, gcol, grow, c0, n0, m0, ng)


def _post_kernel(ym_ref, yc_ref, x_ref, wmo_ref, gx_ref, wxq_ref, mk_ref, mv_ref, wxo_ref,
                 gf_ref, wrt_ref, br_ref,
                 h2_ref, xn_ref, eid_ref, gate_ref, rank_ref, cnt_ref, cnt_s, *, tm, sub, group):
    step = pl.program_id(0) * pl.num_programs(1) + pl.program_id(1)

    @pl.when(step % group == 0)
    def _():
        cnt_s[...] = jnp.zeros_like(cnt_s)

    mix = _dot(ym_ref[...], wmo_ref[0:D_MLSTM, :]) + _dot(yc_ref[...], wmo_ref[D_MLSTM:D_MODEL, :])
    h1 = x_ref[...] + mix

    xq = _dot(_rms(h1, gx_ref[...]).astype(BF16), wxq_ref[...]).astype(BF16)
    att = jnp.zeros((tm, D_MODEL), F32)
    for hd in range(N_XHEADS):
        cols = slice(hd * XHEAD_DIM, (hd + 1) * XHEAD_DIM)
        s = _dot_nt(xq[:, cols], mk_ref[:, cols]) * (XHEAD_DIM ** -0.5)
        e = jnp.exp(s - jnp.max(s, axis=-1, keepdims=True))
        p = (e / jnp.sum(e, axis=-1, keepdims=True)).astype(BF16)
        o = _dot(p, mv_ref[:, cols]).astype(BF16)
        att = att + _dot(o, wxo_ref[cols, :])
    h2 = h1 + att
    h2_ref[...] = h2

    xn2 = _rms(h2, gf_ref[...]).astype(BF16)
    xn_ref[...] = xn2

    logits = _dot_nt(wrt_ref[...], xn2) + br_ref[...]
    eidx = lax.broadcasted_iota(I32, logits.shape, 0).astype(F32)
    work = logits
    vals, ids, hots = [], [], []
    for _ in range(TOP_K):
        mx = jnp.max(work, axis=0, keepdims=True)
        idx = jnp.min(jnp.where(work == mx, eidx, float(N_EXPERTS)), axis=0, keepdims=True)
        sel = eidx == idx
        vals.append(mx)
        ids.append(idx)
        hots.append(sel)
        work = jnp.where(sel, -jnp.inf, work)
    exps = [jnp.exp(v - vals[0]) for v in vals]
    denom = exps[0] + exps[1] + exps[2] + exps[3]

    picked = jnp.zeros(logits.shape, F32)
    for sel in hots:
        picked = picked + sel.astype(F32)
    shift = jnp.full((tm, tm), sub.bit_length() - 1, I32)
    tj = lax.broadcasted_iota(I32, (tm, tm), 0)
    tt = lax.broadcasted_iota(I32, (tm, tm), 1)
    same = lax.shift_right_logical(tj, shift) == lax.shift_right_logical(tt, shift)
    before = jnp.where(jnp.logical_and(tj < tt, same), 1.0, 0.0).astype(BF16)
    prior = _dot(picked.astype(BF16), before) + cnt_s[...]
    for kk in range(TOP_K):
        eid_ref[kk:kk + 1, :] = ids[kk].astype(I32)
        gate_ref[kk:kk + 1, :] = exps[kk] / denom
        rank_ref[kk:kk + 1, :] = jnp.sum(jnp.where(hots[kk], prior, 0.0), axis=0,
                                         keepdims=True).astype(I32)
    for s in range(tm // sub):
        total = cnt_s[...] + jnp.sum(picked[:, s * sub:(s + 1) * sub], axis=1, keepdims=True)
        cnt_ref[s] = total.astype(I32)
    cnt_s[...] = total


def _post(ym, yc, x, wmo, gx, wxq, mkb, mvb, wxo, gf, wrt, br, *, tm, sub, group):
    bsz, seq, _ = x.shape
    nj = seq // tm
    grid = (bsz, nj)
    nsub = tm // sub
    n_tiles = bsz * nj * nsub // group
    tok = lambda c: pl.BlockSpec((None, tm, c), lambda b, j: (b, j, 0))
    full = lambda shape: pl.BlockSpec(shape, lambda b, j: (0,) * len(shape))
    mem = pl.BlockSpec((None, N_MEM, D_MODEL), lambda b, j: (b, 0, 0))
    sel = pl.BlockSpec((None, TOP_K, tm), lambda b, j: (b, 0, j))
    return pl.pallas_call(
        functools.partial(_post_kernel, tm=tm, sub=sub, group=group),
        grid=grid,
        in_specs=[tok(D_MLSTM), tok(D_CONV), tok(D_MODEL), full((D_MODEL, D_MODEL)),
                  full((1, D_MODEL)), full((D_MODEL, D_MODEL)), mem, mem,
                  full((D_MODEL, D_MODEL)), full((1, D_MODEL)), full((N_EXPERTS, D_MODEL)),
                  full((N_EXPERTS, 1))],
        out_specs=[tok(D_MODEL), tok(D_MODEL), sel, sel, sel,
                   pl.BlockSpec((nsub, N_EXPERTS, 1), lambda b, j: ((b * nj + j) // group, 0, 0))],
        out_shape=[jax.ShapeDtypeStruct((bsz, seq, D_MODEL), F32),
                   jax.ShapeDtypeStruct((bsz, seq, D_MODEL), BF16),
                   jax.ShapeDtypeStruct((bsz, TOP_K, seq), I32),
                   jax.ShapeDtypeStruct((bsz, TOP_K, seq), F32),
                   jax.ShapeDtypeStruct((bsz, TOP_K, seq), I32),
                   jax.ShapeDtypeStruct((n_tiles, N_EXPERTS, 1), I32)],
        scratch_shapes=[pltpu.VMEM((N_EXPERTS, 1), F32)],
        compiler_params=_cparams(("arbitrary", "arbitrary")),
        name="post",
    )(ym, yc, x, wmo, gx, wxq, mkb, mvb, wxo, gf, wrt, br)


def _sorted_rows(n_tokens):
    return -(-(TOP_K * n_tokens + N_EXPERTS * (SUBLANES - 1)) // ROW_BLOCK) * ROW_BLOCK


def _dispatch_kernel(tab_ref, fill_ref, eid_ref, rank_ref, ls_ref, x_ref, eids_ref, ranks_ref, xs_ref,
                     xb_ref, srt, zero_s, pending, sems, *, n_ptiles, n_blocks):
    i = pl.program_id(0)
    cur = i % 2

    def granule(src, src_row, dst_row, sl):
        return pltpu.make_async_copy(
            src.at[pl.ds(pl.multiple_of(src_row, SUBLANES), SUBLANES), :],
            xb_ref.at[pl.ds(pl.multiple_of(dst_row, SUBLANES), SUBLANES), :], sems.at[sl])

    def drain(count, sl):
        def body(g, c):
            granule(zero_s, 0, 0, sl).wait()
            return c
        lax.fori_loop(0, count, body, 0)

    @pl.when(i == 0)
    def _():
        pending[0] = 0

    def sort_and_move(eid, rank, x):
        ntok = x.shape[0]
        nrows = _sorted_rows(ntok)
        e_iota = lax.broadcasted_iota(I32, (N_EXPERTS, ntok), 0)
        s_iota = lax.broadcasted_iota(I32, (nrows, ntok), 0)
        seg_start = ls_ref[...]
        hit = None
        for kk in range(TOP_K):
            start = jnp.sum(jnp.where(e_iota == eid[kk:kk + 1, :], seg_start, 0.0),
                            axis=0, keepdims=True).astype(I32)
            match = s_iota == start + rank[kk:kk + 1, :]
            hit = match if hit is None else jnp.logical_or(hit, match)
        perm = jnp.where(hit, 1.0, 0.0).astype(BF16)
        srt[cur, 0:nrows, :] = _dot(perm, x)
        drain(pending[0], 1 - cur)
        total = tab_ref[TABLE_WIDTH - 1]

        def issue(g, c):
            granule(srt.at[cur], g * SUBLANES, tab_ref[g], cur).start()
            return c
        lax.fori_loop(0, total, issue, 0)
        pending[0] = total

    @pl.when(i < n_ptiles)
    def _():
        sort_and_move(eid_ref[...], rank_ref[...], x_ref[...])

    @pl.when(i == n_ptiles)
    def _():
        sort_and_move(eids_ref[...], ranks_ref[...], xs_ref[...])
        drain(pending[0], cur)
        zero_s[...] = jnp.zeros_like(zero_s)
        for e in range(N_EXPERTS):
            dst = fill_ref[e]

            def zissue(g, c):
                granule(zero_s, 0, dst + g * SUBLANES, cur).start()
                return c
            lax.fori_loop(0, fill_ref[N_EXPERTS + e], zissue, 0)
        drain(fill_ref[2 * N_EXPERTS], cur)
        first_free = fill_ref[2 * N_EXPERTS + 1]

        def blk_copy(b):
            return pltpu.make_async_copy(
                zero_s, xb_ref.at[pl.ds(pl.multiple_of(b * ROW_BLOCK, ROW_BLOCK), ROW_BLOCK), :],
                sems.at[cur])

        def bissue(b, c):
            blk_copy(b).start()
            return c
        lax.fori_loop(first_free, n_blocks, bissue, 0)

        def bdrain(b, c):
            blk_copy(0).wait()
            return c
        lax.fori_loop(first_free, n_blocks, bdrain, 0)


def _dispatch(tab, fill, eid_p, rank_p, seg_start, xn_p, eid_s, rank_s, xn_s, *, tmd, n_blocks):
    n_ptiles = eid_p.shape[0]
    n_sample = xn_s.shape[0]
    last = n_ptiles - 1
    smem = lambda shape, imap: pl.BlockSpec(shape, imap, memory_space=pltpu.SMEM)
    return pl.pallas_call(
        functools.partial(_dispatch_kernel, n_ptiles=n_ptiles, n_blocks=n_blocks),
        grid=(n_ptiles + 1,),
        in_specs=[smem((TABLE_WIDTH,), lambda i: (i,)),
                  smem((TABLE_WIDTH,), lambda i: (0,)),
                  pl.BlockSpec((None, TOP_K, tmd), lambda i: (jnp.minimum(i, last), 0, 0)),
                  pl.BlockSpec((None, TOP_K, tmd), lambda i: (jnp.minimum(i, last), 0, 0)),
                  pl.BlockSpec((None, N_EXPERTS, 1), lambda i: (i, 0, 0)),
                  pl.BlockSpec((tmd, D_MODEL), lambda i: (jnp.minimum(i, last), 0)),
                  pl.BlockSpec((TOP_K, n_sample), lambda i: (0, 0)),
                  pl.BlockSpec((TOP_K, n_sample), lambda i: (0, 0)),
                  pl.BlockSpec((n_sample, D_MODEL), lambda i: (0, 0))],
        out_specs=pl.BlockSpec(memory_space=pl.ANY),
        out_shape=jax.ShapeDtypeStruct((n_blocks * ROW_BLOCK, D_MODEL), F32),
        scratch_shapes=[pltpu.VMEM((2, _sorted_rows(tmd), D_MODEL), F32),
                        pltpu.VMEM((ROW_BLOCK, D_MODEL), F32), pltpu.SMEM((1,), I32),
                        pltpu.SemaphoreType.DMA((2,))],
        compiler_params=_cparams(("arbitrary",)),
        name="dispatch",
    )(tab, fill, eid_p, rank_p, seg_start, xn_p, eid_s, rank_s, xn_s)


def _expert_kernel(tab_ref, xb_ref, wu_hbm, bu_ref, wd_hbm, bd_ref, yb_ref,
                   wu_f, wd_f, wu_s, wd_s, xbuf, ybuf, wsem, xsem, ysem, *, n_blocks):
    e = pl.program_id(0)
    first = tab_ref[e]
    nblk = tab_ref[N_EXPERTS + e]
    n_used = tab_ref[2 * N_EXPERTS]

    def hbm_rows(b):
        return pl.ds(pl.multiple_of(b * ROW_BLOCK, ROW_BLOCK), ROW_BLOCK)

    def x_copy(b, sl):
        return pltpu.make_async_copy(xb_ref.at[hbm_rows(b), :], xbuf.at[sl], xsem.at[sl])

    def y_copy(b, sl):
        return pltpu.make_async_copy(ybuf.at[sl], yb_ref.at[hbm_rows(b), :], ysem.at[sl])

    def w_copies(ex, sl):
        return (pltpu.make_async_copy(wu_hbm.at[ex], wu_f.at[sl], wsem.at[0, sl]),
                pltpu.make_async_copy(wd_hbm.at[ex], wd_f.at[sl], wsem.at[1, sl]))

    wslot = e % 2

    @pl.when(e == 0)
    def _():
        for c in w_copies(0, 0):
            c.start()
        for b in range(ROW_LOOKAHEAD):
            @pl.when(b < n_used)
            def _():
                x_copy(b, b % ROW_RING).start()

    @pl.when(e + 1 < N_EXPERTS)
    def _():
        for c in w_copies(e + 1, 1 - wslot):
            c.start()

    for c in w_copies(e, wslot):
        c.wait()

    step = 128

    def cast(r, c):
        rows = pl.ds(pl.multiple_of(r * step, step), step)
        wu_s[rows, :] = wu_f[wslot, rows, :].astype(BF16)
        wd_s[rows, :] = wd_f[wslot, rows, :].astype(BF16)
        return c
    lax.fori_loop(0, D_MODEL // step, cast, 0)

    def block(b, c):
        sl = b % ROW_RING
        x_copy(b, sl).wait()

        @pl.when(b + ROW_LOOKAHEAD < n_used)
        def _():
            x_copy(b + ROW_LOOKAHEAD, (b + ROW_LOOKAHEAD) % ROW_RING).start()

        @pl.when(b >= ROW_RING)
        def _():
            y_copy(b, sl).wait()

        hcat = _dot(xbuf[sl].astype(BF16), wu_s[...]) + bu_ref[...]
        glu = jnp.minimum(hcat[:, 0:D_FF], SWIGLU_LIMIT)
        lin = jnp.clip(hcat[:, D_FF:2 * D_FF], -SWIGLU_LIMIT, SWIGLU_LIMIT)
        act = glu * _sigmoid(SWIGLU_ALPHA * glu) * (lin + 1.0)
        ybuf[sl] = _dot(act.astype(BF16), wd_s[...]) + bd_ref[...]
        y_copy(b, sl).start()
        return c
    lax.fori_loop(first, first + nblk, block, 0)

    @pl.when(e == N_EXPERTS - 1)
    def _():
        for k in range(ROW_RING):
            @pl.when(k < n_used)
            def _():
                y_copy(0, (n_used - 1 - k) % ROW_RING).wait()

        ybuf[0] = jnp.zeros((ROW_BLOCK, D_MODEL), F32)

        def zissue(b, c):
            y_copy(b, 0).start()
            return c
        lax.fori_loop(n_used, n_blocks, zissue, 0)

        def zdrain(b, c):
            y_copy(0, 0).wait()
            return c
        lax.fori_loop(n_used, n_blocks, zdrain, 0)


def _experts(tab, xb, w_up, b_up, w_down, b_down, *, n_blocks):
    w_map = lambda e, tab: (e, 0, 0)
    grid_spec = pltpu.PrefetchScalarGridSpec(
        num_scalar_prefetch=1,
        grid=(N_EXPERTS,),
        in_specs=[pl.BlockSpec(memory_space=pl.ANY),
                  pl.BlockSpec(memory_space=pl.ANY),
                  pl.BlockSpec((None, 1, 2 * D_FF), w_map),
                  pl.BlockSpec(memory_space=pl.ANY),
                  pl.BlockSpec((None, 1, D_MODEL), w_map)],
        out_specs=pl.BlockSpec(memory_space=pl.ANY),
        scratch_shapes=[pltpu.VMEM((2, D_MODEL, 2 * D_FF), F32), pltpu.VMEM((2, D_FF, D_MODEL), F32),
                        pltpu.VMEM((D_MODEL, 2 * D_FF), BF16), pltpu.VMEM((D_FF, D_MODEL), BF16),
                        pltpu.VMEM((ROW_RING, ROW_BLOCK, D_MODEL), F32),
                        pltpu.VMEM((ROW_RING, ROW_BLOCK, D_MODEL), F32),
                        pltpu.SemaphoreType.DMA((2, 2)),
                        pltpu.SemaphoreType.DMA((ROW_RING,)), pltpu.SemaphoreType.DMA((ROW_RING,))],
    )
    return pl.pallas_call(
        functools.partial(_expert_kernel, n_blocks=n_blocks),
        grid_spec=grid_spec,
        out_shape=jax.ShapeDtypeStruct((n_blocks * ROW_BLOCK, D_MODEL), F32),
        compiler_params=_cparams(("arbitrary",)),
        name="experts",
    )(tab, xb, w_up, b_up, w_down, b_down)


def _split_bf16(a):
    hi = a.astype(BF16)
    return hi, (a - hi.astype(F32)).astype(BF16)


def _combine_kernel(tab_ref, nxt_ref, slot_ref, gate_ref, h2_ref, gfin_ref, yb_ref, y_ref, buf, sems,
                    *, tm):
    i = pl.program_id(0)
    n = pl.num_programs(0)
    cur = i % 2
    nrows = buf.shape[1]

    def granule(src_row, sl, dst_row):
        return pltpu.make_async_copy(
            yb_ref.at[pl.ds(pl.multiple_of(src_row, SUBLANES), SUBLANES), :],
            buf.at[sl, pl.ds(pl.multiple_of(dst_row, SUBLANES), SUBLANES), :], sems.at[sl])

    def gather(t_ref, sl):
        def issue(g, c):
            granule(t_ref[g], sl, g * SUBLANES).start()
            return c
        lax.fori_loop(0, t_ref[TABLE_WIDTH - 1], issue, 0)

    @pl.when(i == 0)
    def _():
        buf[...] = jnp.zeros_like(buf)
        gather(tab_ref, 0)

    @pl.when(i + 1 < n)
    def _():
        gather(nxt_ref, 1 - cur)

    def drain(g, c):
        granule(0, cur, 0).wait()
        return c
    lax.fori_loop(0, tab_ref[TABLE_WIDTH - 1], drain, 0)

    s_iota = lax.broadcasted_iota(I32, (tm, nrows), 1)
    wgt = jnp.zeros((tm, nrows), F32)
    for kk in range(TOP_K):
        wgt = wgt + jnp.where(s_iota == slot_ref[:, kk:kk + 1], gate_ref[:, kk:kk + 1], 0.0)
    w_hi, w_lo = _split_bf16(wgt)
    y_hi, y_lo = _split_bf16(buf[cur])
    acc = _dot(w_hi, y_hi) + (_dot(w_hi, y_lo) + _dot(w_lo, y_hi))
    y_ref[...] = _rms(h2_ref[...] + acc, gfin_ref[...])


def _combine(tab, slot_col, gate_col, h2, gfin, yb, *, tm):
    ntok = h2.shape[0]
    n = ntok // tm
    smem = lambda imap: pl.BlockSpec((TABLE_WIDTH,), imap, memory_space=pltpu.SMEM)
    return pl.pallas_call(
        functools.partial(_combine_kernel, tm=tm),
        grid=(n,),
        in_specs=[smem(lambda i: (i,)), smem(lambda i: (jnp.minimum(i + 1, n - 1),)),
                  pl.BlockSpec((tm, TOP_K), lambda i: (i, 0)),
                  pl.BlockSpec((tm, TOP_K), lambda i: (i, 0)),
                  pl.BlockSpec((tm, D_MODEL), lambda i: (i, 0)),
                  pl.BlockSpec((1, D_MODEL), lambda i: (0, 0)),
                  pl.BlockSpec(memory_space=pl.ANY)],
        out_specs=pl.BlockSpec((tm, D_MODEL), lambda i: (i, 0)),
        out_shape=jax.ShapeDtypeStruct((ntok, D_MODEL), F32),
        scratch_shapes=[pltpu.VMEM((2, _sorted_rows(tm), D_MODEL), F32),
                        pltpu.SemaphoreType.DMA((2,))],
        compiler_params=_cparams(("arbitrary",)),
        name="combine",
    )(tab, tab, slot_col, gate_col, h2, gfin, yb)


def _tiles(a, tile):
    bsz, kk, seq = a.shape
    return a.reshape(bsz, kk, seq // tile, tile).transpose(0, 2, 1, 3).reshape(-1, kk, tile)


def _path(x, cbuf, c0, n0, m0, mkb, mvb, wts, *, tm_in, ct, chunk, tm_post, sub, group):
    q, k, v, og, gcol, grow, yc, nbuf = _inproj(
        x, wts["g_mix"], wts["wq"], wts["wg"], wts["wgt"], wts["wc"], wts["bg"], wts["bgt"],
        wts["cw"], cbuf, tm=tm_in, chunk=chunk)
    ym, c1, n1, m1 = _mlstm(q, k, v, og, gcol, grow, c0, n0, m0, wts["ng"], ct=ct, chunk=chunk)
    h2, xn, eid, gate, rank, cnt = _post(
        ym, yc, x, wts["wmo"], wts["g_x"], wts["wxq"], mkb, mvb, wts["wxo"], wts["g_ffn"],
        wts["wrt"], wts["br"], tm=tm_post, sub=sub, group=group)
    return dict(h2=h2, xn=xn, eid=eid, gate=gate, rank=rank, cnt=cnt[:, :, 0],
                c1=c1, n1=n1, m1=m1[..., 0], nbuf=nbuf)


def kernel(x_prompt, x_sample, state_mlstm_c, state_mlstm_n, state_mlstm_m, state_conv, cache_mem_k, cache_mem_v, mem_prompt, norm_mix_g, w_in, b_gate, mlstm_norm_g, conv_w, w_mix_out, norm_x_g, norm_mem_g, w_xq, w_xk, w_xv, w_xo, norm_ffn_g, w_router, b_router, w_up, b_up, w_down, b_down, norm_final_g):
    bp, lp, _ = x_prompt.shape
    bs, ls, _ = x_sample.shape
    l = 0
    row = lambda a: a.reshape(1, -1)

    wi = w_in[l]
    gate_cols = wi[:, 4 * D_MLSTM:4 * D_MLSTM + 2 * N_HEADS]
    wts = dict(
        g_mix=row(norm_mix_g[l]),
        wq=wi[:, 0:4 * D_MLSTM].astype(BF16),
        wg=jnp.pad(gate_cols, ((0, 0), (0, 128 - 2 * N_HEADS))).astype(BF16),
        wgt=gate_cols.T.astype(BF16),
        wc=wi[:, 4 * D_MLSTM + 2 * N_HEADS:].astype(BF16),
        bg=row(b_gate[l]), bgt=b_gate[l].reshape(-1, 1),
        cw=conv_w[l], ng=row(mlstm_norm_g[l]),
        wmo=w_mix_out[l].astype(BF16), g_x=row(norm_x_g[l]), wxq=w_xq[l].astype(BF16),
        wxo=w_xo[l].astype(BF16), g_ffn=row(norm_ffn_g[l]),
        wrt=w_router[l].T.astype(BF16), br=b_router[l].reshape(-1, 1),
    )

    mk, mv, mkb, mvb = _memkv(mem_prompt.reshape(bp * N_MEM, D_MODEL), row(norm_mem_g[l]),
                              w_xk[l].astype(BF16), w_xv[l].astype(BF16))
    zeros = lambda *s: jnp.zeros(s, F32)
    tm_post, tmd = 512, 256
    pr = _path(x_prompt, zeros(bp, CONV_WIDTH - 1, D_CONV), zeros(bp, N_HEADS, HEAD_DIM, HEAD_DIM),
               zeros(bp, N_HEADS, HEAD_DIM), zeros(bp, N_HEADS, 1),
               mkb.reshape(bp, N_MEM, D_MODEL), mvb.reshape(bp, N_MEM, D_MODEL), wts,
               tm_in=512, ct=512, chunk=CHUNK, tm_post=tm_post, sub=tmd, group=1)
    sa = _path(x_sample, state_conv[l], state_mlstm_c[l], state_mlstm_n[l],
               state_mlstm_m[l][..., None],
               cache_mem_k[l].reshape(bs, N_MEM, D_MODEL).astype(BF16),
               cache_mem_v[l].reshape(bs, N_MEM, D_MODEL).astype(BF16), wts,
               tm_in=ls, ct=ls, chunk=min(CHUNK, ls), tm_post=ls, sub=ls, group=bs)

    n_ptok, n_stok = bp * lp, bs * ls
    cnt = jnp.concatenate([pr["cnt"], sa["cnt"]], axis=0)
    n_tiles = cnt.shape[0]
    n_blocks = -(-(TOP_K * (n_ptok + n_stok) + n_tiles * N_EXPERTS * (SUBLANES - 1)) // ROW_BLOCK) \
        + N_EXPERTS
    seg = (cnt + SUBLANES - 1) // SUBLANES * SUBLANES
    seg_src = jnp.cumsum(seg, axis=1) - seg
    tot = jnp.sum(seg, axis=0)
    blocks_e = (tot + ROW_BLOCK - 1) // ROW_BLOCK
    padded = blocks_e * ROW_BLOCK
    pstart = jnp.cumsum(padded) - padded
    bend = jnp.cumsum(blocks_e)
    n_used = bend[-1]
    exp_tab = jnp.concatenate([bend - blocks_e, blocks_e, n_used[None]]).astype(I32)
    seg_dst = pstart[None, :] + jnp.cumsum(seg, axis=0) - seg

    grow0 = jnp.arange(TABLE_WIDTH - 1, dtype=I32) * SUBLANES
    owner = jnp.sum((seg_src + seg)[:, None, :] <= grow0[None, :, None], axis=-1)
    shift = jnp.sum(jnp.where(owner[..., None] == jnp.arange(N_EXPERTS, dtype=I32),
                              (seg_dst - seg_src)[:, None, :], 0), axis=-1)
    n_gran = jnp.sum(seg, axis=1, keepdims=True) // SUBLANES
    gdst = jnp.where(grow0[None, :] < n_gran * SUBLANES, grow0[None, :] + shift, 0)
    tab = jnp.concatenate([gdst, n_gran], axis=1).astype(I32).reshape(-1)
    fgran = (padded - tot) // SUBLANES
    fill = jnp.concatenate([pstart + tot, fgran, jnp.sum(fgran)[None], n_used[None]])
    fill = jnp.pad(fill, (0, TABLE_WIDTH - fill.shape[0])).astype(I32)

    def lookup(table, eid):
        hit = eid[..., None] == jnp.arange(N_EXPERTS, dtype=I32)
        return jnp.sum(jnp.where(hit, table[:, None, None, :], 0), axis=-1)

    eid_pt, rank_pt = _tiles(pr["eid"], tmd), _tiles(pr["rank"], tmd)
    eid_st = sa["eid"].transpose(1, 0, 2).reshape(1, TOP_K, n_stok)
    rank_st = sa["rank"].transpose(1, 0, 2).reshape(1, TOP_K, n_stok)
    slot_p = (lookup(seg_src[:-1], eid_pt) + rank_pt).transpose(0, 2, 1).reshape(n_ptok, TOP_K)
    slot_s = (lookup(seg_src[-1:], eid_st) + rank_st).transpose(0, 2, 1).reshape(n_stok, TOP_K)

    xb = _dispatch(tab, fill, eid_pt, rank_pt, seg_src.astype(F32)[..., None],
                   pr["xn"].reshape(n_ptok, D_MODEL), eid_st[0], rank_st[0],
                   sa["xn"].reshape(n_stok, D_MODEL), tmd=tmd, n_blocks=n_blocks)
    yb = _experts(exp_tab, xb, w_up[l], b_up[l][:, None, :], w_down[l], b_down[l][:, None, :],
                  n_blocks=n_blocks)

    gfin = row(norm_final_g)
    split = (n_tiles - 1) * TABLE_WIDTH
    y_p = _combine(tab[:split], slot_p.astype(I32), pr["gate"].transpose(0, 2, 1).reshape(n_ptok, TOP_K),
                   pr["h2"].reshape(n_ptok, D_MODEL), gfin, yb, tm=tmd)
    y_s = _combine(tab[split:], slot_s.astype(I32), sa["gate"].transpose(0, 2, 1).reshape(n_stok, TOP_K),
                   sa["h2"].reshape(n_stok, D_MODEL), gfin, yb, tm=n_stok)

    lead = lambda a: a[None]
    return (y_p.reshape(bp, lp, D_MODEL), y_s.reshape(bs, ls, D_MODEL),
            lead(pr["c1"]), lead(pr["n1"]), lead(pr["m1"]), lead(pr["nbuf"]),
            lead(mk.reshape(bp, N_MEM, N_XHEADS, XHEAD_DIM)),
            lead(mv.reshape(bp, N_MEM, N_XHEADS, XHEAD_DIM)),
            lead(sa["c1"]), lead(sa["n1"]), lead(sa["m1"]), lead(sa["nbuf"]))
```

```python
import functools

import jax
import jax.numpy as jnp
import numpy as np
from jax import lax
from jax.experimental import pallas as pl
from jax.experimental.pallas import tpu as pltpu

F32 = jnp.float32
BF16 = jnp.bfloat16
I32 = jnp.int32

D_MODEL = 1024
N_HEADS = 4
HEAD_DIM = 128
D_MLSTM = N_HEADS * HEAD_DIM
D_CONV = D_MODEL - D_MLSTM
CONV_WIDTH = 3
CHUNK = 64
N_MEM = 256
N_XHEADS = 4
XHEAD_DIM = D_MODEL // N_XHEADS
N_EXPERTS = 32
TOP_K = 4
D_FF = D_MODEL
SWIGLU_LIMIT = 7.0
SWIGLU_ALPHA = 1.702
EPS = 1e-5

SUBLANES = 8
TABLE_WIDTH = 256
SORT_PARTS = 2
UNSORT_CHUNK = 256
ROW_BLOCK = 256
ROW_LOOKAHEAD = 5
ROW_RING = ROW_LOOKAHEAD + 1
VMEM_LIMIT = 56 * 1024 * 1024


def _cparams(sem):
    return pltpu.CompilerParams(dimension_semantics=sem, vmem_limit_bytes=VMEM_LIMIT)


def _rms(x, g):
    return x * lax.rsqrt(jnp.mean(x * x, axis=-1, keepdims=True) + EPS) * g


def _log_sigmoid(x):
    return -(jnp.maximum(-x, 0.0) + jnp.log1p(jnp.exp(-jnp.abs(x))))


def _sigmoid(x):
    return 1.0 / (1.0 + jnp.exp(-x))


def _dot(a, b):
    return jnp.dot(a, b, preferred_element_type=F32)


def _dot_nt(a, b):
    return lax.dot_general(a, b, (((1,), (1,)), ((), ())), preferred_element_type=F32)


def _dot_tn(a, b):
    return lax.dot_general(a, b, (((0,), (0,)), ((), ())), preferred_element_type=F32)


def _memkv_kernel(mem_ref, g_ref, wk_ref, wv_ref, mk_ref, mv_ref, mkb_ref, mvb_ref):
    mn = _rms(mem_ref[...], g_ref[...]).astype(BF16)
    mk = _dot(mn, wk_ref[...])
    mv = _dot(mn, wv_ref[...])
    mk_ref[...] = mk
    mv_ref[...] = mv
    mkb_ref[...] = mk.astype(BF16)
    mvb_ref[...] = mv.astype(BF16)


def _memkv(mem2d, g, wk, wv):
    rows = mem2d.shape[0]
    tm = N_MEM
    row_spec = pl.BlockSpec((tm, D_MODEL), lambda i: (i, 0))
    full = lambda shape: pl.BlockSpec(shape, lambda i: (0,) * len(shape))
    return pl.pallas_call(
        _memkv_kernel,
        grid=(rows // tm,),
        in_specs=[row_spec, full((1, D_MODEL)), full((D_MODEL, D_MODEL)), full((D_MODEL, D_MODEL))],
        out_specs=[row_spec, row_spec, row_spec, row_spec],
        out_shape=[jax.ShapeDtypeStruct((rows, D_MODEL), F32)] * 2
        + [jax.ShapeDtypeStruct((rows, D_MODEL), BF16)] * 2,
        compiler_params=_cparams(("arbitrary",)),
        name="memkv",
    )(mem2d, g, wk, wv)


def _inproj_kernel(x_ref, g_ref, wq_ref, wg_ref, wgt_ref, wc_ref, bg_ref, bgt_ref, cw_ref, cbuf_ref,
                   q_ref, k_ref, v_ref, og_ref, gcol_ref, grow_ref, yc_ref, nbuf_ref,
                   carry_ref, *, tm, chunk):
    j = pl.program_id(1)

    @pl.when(j == 0)
    def _():
        carry_ref[0:2, :] = cbuf_ref[...]

    xb = _rms(x_ref[...], g_ref[...]).astype(BF16)

    p = _dot(xb, wq_ref[...])
    q_ref[...] = p[:, 0:D_MLSTM].astype(BF16)
    k_ref[...] = (p[:, D_MLSTM:2 * D_MLSTM] * (HEAD_DIM ** -0.5)).astype(BF16)
    v_ref[...] = p[:, 2 * D_MLSTM:3 * D_MLSTM].astype(BF16)
    og_ref[...] = _sigmoid(p[:, 3 * D_MLSTM:4 * D_MLSTM])

    gc = _dot(xb, wg_ref[...])[:, 0:2 * N_HEADS] + bg_ref[...]
    col = lax.broadcasted_iota(I32, gc.shape, 1)
    gcol_ref[...] = jnp.where(col < N_HEADS, gc, _log_sigmoid(gc))
    gr = _dot_nt(wgt_ref[...], xb) + bgt_ref[...]
    row = lax.broadcasted_iota(I32, gr.shape, 0)
    gr = jnp.where(row < N_HEADS, gr, _log_sigmoid(gr))
    for c in range(tm // chunk):
        grow_ref[c] = gr[:, c * chunk:(c + 1) * chunk]

    pc = _dot(xb, wc_ref[...])
    cb = pc[:, 0:D_CONV]
    u = pc[:, D_CONV:2 * D_CONV] * pc[:, 2 * D_CONV:3 * D_CONV]
    c0 = carry_ref[0:1, :]
    c1 = carry_ref[1:2, :]
    rid = lax.broadcasted_iota(I32, u.shape, 0)
    um1 = jnp.where(rid == 0, c1, pltpu.roll(u, 1, 0))
    um2 = jnp.where(rid == 0, c0, jnp.where(rid == 1, c1, pltpu.roll(u, 2, 0)))
    yc = cw_ref[0:1, :] * um2 + cw_ref[1:2, :] * um1 + cw_ref[2:3, :] * u
    yc_ref[...] = (cb * yc).astype(BF16)
    tail = u[tm - 2:tm, :]
    carry_ref[0:2, :] = tail
    nbuf_ref[...] = tail


def _inproj(x, g, wq, wg, wgt, wc, bg, bgt, cw, cbuf, *, tm, chunk):
    bsz, seq, _ = x.shape
    grid = (bsz, seq // tm)
    tok = lambda c: pl.BlockSpec((None, tm, c), lambda b, j: (b, j, 0))
    full = lambda shape: pl.BlockSpec(shape, lambda b, j: (0,) * len(shape))
    nck = tm // chunk
    return pl.pallas_call(
        functools.partial(_inproj_kernel, tm=tm, chunk=chunk),
        grid=grid,
        in_specs=[tok(D_MODEL), full((1, D_MODEL)), full((D_MODEL, 4 * D_MLSTM)),
                  full((D_MODEL, 128)), full((8, D_MODEL)), full((D_MODEL, 3 * D_CONV)),
                  full((1, 8)), full((8, 1)), full((CONV_WIDTH, D_CONV)),
                  pl.BlockSpec((None, 2, D_CONV), lambda b, j: (b, 0, 0))],
        out_specs=[tok(D_MLSTM), tok(D_MLSTM), tok(D_MLSTM), tok(D_MLSTM), tok(8),
                   pl.BlockSpec((None, nck, 8, chunk), lambda b, j: (b, j, 0, 0)),
                   tok(D_CONV),
                   pl.BlockSpec((None, 2, D_CONV), lambda b, j: (b, 0, 0))],
        out_shape=[jax.ShapeDtypeStruct((bsz, seq, D_MLSTM), BF16)] * 3
        + [jax.ShapeDtypeStruct((bsz, seq, D_MLSTM), F32),
           jax.ShapeDtypeStruct((bsz, seq, 8), F32),
           jax.ShapeDtypeStruct((bsz, seq // chunk, 8, chunk), F32),
           jax.ShapeDtypeStruct((bsz, seq, D_CONV), BF16),
           jax.ShapeDtypeStruct((bsz, 2, D_CONV), F32)],
        scratch_shapes=[pltpu.VMEM((8, D_CONV), F32)],
        compiler_params=_cparams(("arbitrary", "arbitrary")),
        name="inproj",
    )(x, g, wq, wg, wgt, wc, bg, bgt, cw, cbuf)


def _mlstm_kernel(q_ref, k_ref, v_ref, og_ref, gc_ref, gr_ref, c0_ref, n0_ref, m0_ref, ng_ref,
                  ym_ref, c1_ref, n1_ref, m1_ref, c_s, n_s, m_s, *, chunk, nchunks, bsz):
    j = pl.program_id(0)

    @pl.when(j == 0)
    def _():
        c_s[...] = c0_ref[...]
        n_s[...] = n0_ref[...]
        m_s[...] = m0_ref[...]

    ti = lax.broadcasted_iota(I32, (chunk, chunk), 0)
    ji = lax.broadcasted_iota(I32, (chunk, chunk), 1)
    causal = ji <= ti

    def body(ci, carry):
        r0 = pl.multiple_of(ci * chunk, chunk)
        rows = pl.ds(r0, chunk)
        chains = [(b, h) for b in range(bsz) for h in range(N_HEADS)]
        cols = lambda h: slice(h * HEAD_DIM, (h + 1) * HEAD_DIM)
        each = lambda f: [f(n, b, h) for n, (b, h) in enumerate(chains)]
        q = lambda b, h: q_ref[b, rows, cols(h)]
        k = lambda b, h: k_ref[b, rows, cols(h)]
        v = lambda b, h: v_ref[b, rows, cols(h)]
        gcs = [gc_ref[b, rows, :] for b in range(bsz)]
        grs = [gr_ref[b, ci] for b in range(bsz)]
        li_c = each(lambda n, b, h: gcs[b][:, h:h + 1])
        lf_c = each(lambda n, b, h: gcs[b][:, N_HEADS + h:N_HEADS + h + 1])
        li_r = each(lambda n, b, h: grs[b][h:h + 1, :])
        lf_r = each(lambda n, b, h: grs[b][N_HEADS + h:N_HEADS + h + 1, :])
        m_prev = each(lambda n, b, h: m_s[b, h:h + 1, :])

        b_c = each(lambda n, b, h: jnp.sum(jnp.where(causal, lf_r[n], 0.0), axis=1, keepdims=True))
        b_r = each(lambda n, b, h: jnp.sum(jnp.where(ti <= ji, lf_c[n], 0.0), axis=0, keepdims=True))
        dmat = each(lambda n, b, h: jnp.where(causal, b_c[n] - b_r[n] + li_r[n], -jnp.inf))
        dmax = each(lambda n, b, h: jnp.max(dmat[n], axis=1, keepdims=True))
        inter = each(lambda n, b, h: b_c[n] + m_prev[n])
        m_t = each(lambda n, b, h: jnp.maximum(inter[n], dmax[n]))
        w_inter = each(lambda n, b, h: jnp.exp(inter[n] - m_t[n]))
        s = each(lambda n, b, h: _dot_nt(q(b, h), k(b, h)) * jnp.exp(dmat[n] - m_t[n]))
        qc = each(lambda n, b, h: _dot(q(b, h), c_s[b, h].astype(BF16)))
        sv = each(lambda n, b, h: _dot(s[n].astype(BF16), v(b, h)))
        qn = each(lambda n, b, h: jnp.sum(q(b, h).astype(F32) * n_s[b, h:h + 1, :], axis=1,
                                          keepdims=True))
        den = each(lambda n, b, h: w_inter[n] * qn[n] + jnp.sum(s[n], axis=1, keepdims=True))
        hh = each(lambda n, b, h: (w_inter[n] * qc[n] + sv[n])
                  / jnp.maximum(jnp.abs(den[n]), jnp.exp(-m_t[n])))

        m_new = each(lambda n, b, h: m_t[n][chunk - 1:chunk, :])
        b_last = each(lambda n, b, h: b_c[n][chunk - 1:chunk, :])
        decay = each(lambda n, b, h: jnp.exp(b_last[n] + m_prev[n] - m_new[n]))
        kw = each(lambda n, b, h: k(b, h).astype(F32)
                  * jnp.exp(b_last[n] - b_c[n] + li_c[n] - m_new[n]))
        kv = each(lambda n, b, h: _dot_tn(kw[n].astype(BF16), v(b, h)))
        for n, (b, h) in enumerate(chains):
            c_s[b, h] = decay[n] * c_s[b, h] + kv[n]
            n_s[b, h:h + 1, :] = decay[n] * n_s[b, h:h + 1, :] + jnp.sum(kw[n], axis=0, keepdims=True)
            m_s[b, h:h + 1, :] = m_new[n]

        hn = each(lambda n, b, h: hh[n] * lax.rsqrt(jnp.mean(hh[n] * hh[n], axis=1, keepdims=True) + EPS)
                  * ng_ref[:, cols(h)])
        for n, (b, h) in enumerate(chains):
            ym_ref[b, rows, cols(h)] = (hn[n] * og_ref[b, rows, cols(h)]).astype(BF16)
        return carry

    lax.fori_loop(0, nchunks, body, 0)

    @pl.when(j == pl.num_programs(0) - 1)
    def _():
        c1_ref[...] = c_s[...]
        n1_ref[...] = n_s[...]
        m1_ref[...] = m_s[...]


def _mlstm(q, k, v, og, gcol, grow, c0, n0, m0, ng, *, ct, chunk):
    bsz, seq, _ = q.shape
    nchunks = ct // chunk
    grid = (seq // ct,)
    tok = lambda c: pl.BlockSpec((bsz, ct, c), lambda j: (0, j, 0))
    st_c = pl.BlockSpec((bsz, N_HEADS, HEAD_DIM, HEAD_DIM), lambda j: (0, 0, 0, 0))
    st_n = pl.BlockSpec((bsz, N_HEADS, HEAD_DIM), lambda j: (0, 0, 0))
    st_m = pl.BlockSpec((bsz, N_HEADS, 1), lambda j: (0, 0, 0))
    return pl.pallas_call(
        functools.partial(_mlstm_kernel, chunk=chunk, nchunks=nchunks, bsz=bsz),
        grid=grid,
        in_specs=[tok(D_MLSTM), tok(D_MLSTM), tok(D_MLSTM), tok(D_MLSTM), tok(8),
                  pl.BlockSpec((bsz, nchunks, 8, chunk), lambda j: (0, j, 0, 0)),
                  st_c, st_n, st_m,
                  pl.BlockSpec((1, D_MLSTM), lambda j: (0, 0))],
        out_specs=[tok(D_MLSTM), st_c, st_n, st_m],
        out_shape=[jax.ShapeDtypeStruct((bsz, seq, D_MLSTM), BF16),
                   jax.ShapeDtypeStruct((bsz, N_HEADS, HEAD_DIM, HEAD_DIM), F32),
                   jax.ShapeDtypeStruct((bsz, N_HEADS, HEAD_DIM), F32),
                   jax.ShapeDtypeStruct((bsz, N_HEADS, 1), F32)],
        scratch_shapes=[pltpu.VMEM((bsz, N_HEADS, HEAD_DIM, HEAD_DIM), F32),
                        pltpu.VMEM((bsz, N_HEADS, HEAD_DIM), F32),
                        pltpu.VMEM((bsz, N_HEADS, 1), F32)],
        compiler_params=_cparams(("arbitrary",)),
        name="mlstm",
    )(q, k, v, og, gcol, grow, c0, n0, m0, ng)


def _post_kernel(ym_ref, yc_ref, x_ref, wmo_ref, gx_ref, wxq_ref, mk_ref, mv_ref, wxo_ref,
                 gf_ref, wrt_ref, br_ref,
                 h2_ref, xn_ref, eid_ref, gate_ref, rank_ref, cnt_ref, cnt_s, *, tm, sub, group):
    step = pl.program_id(0) * pl.num_programs(1) + pl.program_id(1)

    @pl.when(step % group == 0)
    def _():
        cnt_s[...] = jnp.zeros_like(cnt_s)

    mix = _dot(ym_ref[...], wmo_ref[0:D_MLSTM, :]) + _dot(yc_ref[...], wmo_ref[D_MLSTM:D_MODEL, :])
    h1 = x_ref[...] + mix

    xq = _dot(_rms(h1, gx_ref[...]).astype(BF16), wxq_ref[...]).astype(BF16)
    att = jnp.zeros((tm, D_MODEL), F32)
    for hd in range(N_XHEADS):
        cols = slice(hd * XHEAD_DIM, (hd + 1) * XHEAD_DIM)
        s = _dot_nt(xq[:, cols], mk_ref[:, cols]) * (XHEAD_DIM ** -0.5)
        e = jnp.exp(s - jnp.max(s, axis=-1, keepdims=True))
        p = (e / jnp.sum(e, axis=-1, keepdims=True)).astype(BF16)
        o = _dot(p, mv_ref[:, cols]).astype(BF16)
        att = att + _dot(o, wxo_ref[cols, :])
    h2 = h1 + att
    h2_ref[...] = h2

    xn2 = _rms(h2, gf_ref[...]).astype(BF16)
    xn_ref[...] = xn2

    logits = _dot_nt(wrt_ref[...], xn2) + br_ref[...]
    eidx = lax.broadcasted_iota(I32, logits.shape, 0).astype(F32)
    work = logits
    vals, ids, hots = [], [], []
    for _ in range(TOP_K):
        mx = jnp.max(work, axis=0, keepdims=True)
        idx = jnp.min(jnp.where(work == mx, eidx, float(N_EXPERTS)), axis=0, keepdims=True)
        sel = eidx == idx
        vals.append(mx)
        ids.append(idx)
        hots.append(sel)
        work = jnp.where(sel, -jnp.inf, work)
    exps = [jnp.exp(v - vals[0]) for v in vals]
    denom = exps[0] + exps[1] + exps[2] + exps[3]

    picked = jnp.zeros(logits.shape, F32)
    for sel in hots:
        picked = picked + sel.astype(F32)
    shift = jnp.full((tm, tm), sub.bit_length() - 1, I32)
    tj = lax.broadcasted_iota(I32, (tm, tm), 0)
    tt = lax.broadcasted_iota(I32, (tm, tm), 1)
    same = lax.shift_right_logical(tj, shift) == lax.shift_right_logical(tt, shift)
    before = jnp.where(jnp.logical_and(tj < tt, same), 1.0, 0.0).astype(BF16)
    prior = _dot(picked.astype(BF16), before) + cnt_s[...]
    for kk in range(TOP_K):
        eid_ref[kk:kk + 1, :] = ids[kk].astype(I32)
        gate_ref[kk:kk + 1, :] = exps[kk] / denom
        rank_ref[kk:kk + 1, :] = jnp.sum(jnp.where(hots[kk], prior, 0.0), axis=0,
                                         keepdims=True).astype(I32)
    for s in range(tm // sub):
        total = cnt_s[...] + jnp.sum(picked[:, s * sub:(s + 1) * sub], axis=1, keepdims=True)
        cnt_ref[s] = total.astype(I32)
    cnt_s[...] = total


def _post(ym, yc, x, wmo, gx, wxq, mkb, mvb, wxo, gf, wrt, br, *, tm, sub, group):
    bsz, seq, _ = x.shape
    nj = seq // tm
    grid = (bsz, nj)
    nsub = tm // sub
    n_tiles = bsz * nj * nsub // group
    tok = lambda c: pl.BlockSpec((None, tm, c), lambda b, j: (b, j, 0))
    full = lambda shape: pl.BlockSpec(shape, lambda b, j: (0,) * len(shape))
    mem = pl.BlockSpec((None, N_MEM, D_MODEL), lambda b, j: (b, 0, 0))
    sel = pl.BlockSpec((None, TOP_K, tm), lambda b, j: (b, 0, j))
    return pl.pallas_call(
        functools.partial(_post_kernel, tm=tm, sub=sub, group=group),
        grid=grid,
        in_specs=[tok(D_MLSTM), tok(D_CONV), tok(D_MODEL), full((D_MODEL, D_MODEL)),
                  full((1, D_MODEL)), full((D_MODEL, D_MODEL)), mem, mem,
                  full((D_MODEL, D_MODEL)), full((1, D_MODEL)), full((N_EXPERTS, D_MODEL)),
                  full((N_EXPERTS, 1))],
        out_specs=[tok(D_MODEL), tok(D_MODEL), sel, sel, sel,
                   pl.BlockSpec((nsub, N_EXPERTS, 1), lambda b, j: ((b * nj + j) // group, 0, 0))],
        out_shape=[jax.ShapeDtypeStruct((bsz, seq, D_MODEL), F32),
                   jax.ShapeDtypeStruct((bsz, seq, D_MODEL), BF16),
                   jax.ShapeDtypeStruct((bsz, TOP_K, seq), I32),
                   jax.ShapeDtypeStruct((bsz, TOP_K, seq), F32),
                   jax.ShapeDtypeStruct((bsz, TOP_K, seq), I32),
                   jax.ShapeDtypeStruct((n_tiles, N_EXPERTS, 1), I32)],
        scratch_shapes=[pltpu.VMEM((N_EXPERTS, 1), F32)],
        compiler_params=_cparams(("arbitrary", "arbitrary")),
        name="post",
    )(ym, yc, x, wmo, gx, wxq, mkb, mvb, wxo, gf, wrt, br)


def _sorted_rows(n_tokens):
    return -(-(TOP_K * n_tokens + N_EXPERTS * (SUBLANES - 1)) // ROW_BLOCK) * ROW_BLOCK


def _dispatch_kernel(tab_ref, ptab_ref, fill_ref, eid_ref, rank_ref, ls_ref, x_ref, eids_ref, ranks_ref,
                     xs_ref, xb_ref, srt, zero_s, pending, sems, *, n_ptiles, n_blocks):
    i = pl.program_id(0)
    cur = i % 2

    def granule(src, src_row, dst_row, sl):
        return pltpu.make_async_copy(
            src.at[pl.ds(pl.multiple_of(src_row, SUBLANES), SUBLANES), :],
            xb_ref.at[pl.ds(pl.multiple_of(dst_row, SUBLANES), SUBLANES), :], sems.at[sl])

    def drain(count, sl):
        def body(g, c):
            granule(zero_s, 0, 0, sl).wait()
            return c
        lax.fori_loop(0, count, body, 0)

    @pl.when(i == 0)
    def _():
        pending[0] = 0
        pending[1] = 0

    def issue(t_ref, sl, lo, hi):
        def body(g, c):
            granule(srt.at[sl], g * SUBLANES, t_ref[g], sl).start()
            return c
        lax.fori_loop(lo, hi, body, 0)

    def sort_and_move(eid, rank, x):
        ntok = x.shape[0]
        part = _sorted_rows(ntok) // SORT_PARTS
        burst = -(-(TABLE_WIDTH - 1) // SORT_PARTS)
        drain(pending[1], cur)
        prev_total = pending[0]
        e_iota = lax.broadcasted_iota(I32, (N_EXPERTS, ntok), 0)
        seg_start = ls_ref[...]
        slots = [jnp.sum(jnp.where(e_iota == eid[kk:kk + 1, :], seg_start, 0.0),
                         axis=0, keepdims=True).astype(I32) + rank[kk:kk + 1, :]
                 for kk in range(TOP_K)]
        for p in range(SORT_PARTS):
            issue(ptab_ref, 1 - cur, p * burst, jnp.minimum((p + 1) * burst, prev_total))
            s_iota = lax.broadcasted_iota(I32, (part, ntok), 0) + p * part
            hit = s_iota == slots[0]
            for kk in range(1, TOP_K):
                hit = jnp.logical_or(hit, s_iota == slots[kk])
            perm = jnp.where(hit, 1.0, 0.0).astype(BF16)
            srt[cur, p * part:(p + 1) * part, :] = _dot(perm, x)
        pending[1] = prev_total
        pending[0] = tab_ref[TABLE_WIDTH - 1]

    @pl.when(i < n_ptiles)
    def _():
        sort_and_move(eid_ref[...], rank_ref[...], x_ref[...])

    @pl.when(i == n_ptiles)
    def _():
        sort_and_move(eids_ref[...], ranks_ref[...], xs_ref[...])
        issue(tab_ref, cur, 0, pending[0])
        drain(pending[1], 1 - cur)
        drain(pending[0], cur)
        zero_s[...] = jnp.zeros_like(zero_s)
        for e in range(N_EXPERTS):
            dst = fill_ref[e]

            def zissue(g, c):
                granule(zero_s, 0, dst + g * SUBLANES, cur).start()
                return c
            lax.fori_loop(0, fill_ref[N_EXPERTS + e], zissue, 0)
        drain(fill_ref[2 * N_EXPERTS], cur)
        first_free = fill_ref[2 * N_EXPERTS + 1]

        def blk_copy(b):
            return pltpu.make_async_copy(
                zero_s, xb_ref.at[pl.ds(pl.multiple_of(b * ROW_BLOCK, ROW_BLOCK), ROW_BLOCK), :],
                sems.at[cur])

        def bissue(b, c):
            blk_copy(b).start()
            return c
        lax.fori_loop(first_free, n_blocks, bissue, 0)

        def bdrain(b, c):
            blk_copy(0).wait()
            return c
        lax.fori_loop(first_free, n_blocks, bdrain, 0)


def _dispatch(tab, fill, eid_p, rank_p, seg_start, xn_p, eid_s, rank_s, xn_s, *, tmd, n_blocks):
    n_ptiles = eid_p.shape[0]
    n_sample = xn_s.shape[0]
    last = n_ptiles - 1
    smem = lambda shape, imap: pl.BlockSpec(shape, imap, memory_space=pltpu.SMEM)
    return pl.pallas_call(
        functools.partial(_dispatch_kernel, n_ptiles=n_ptiles, n_blocks=n_blocks),
        grid=(n_ptiles + 1,),
        in_specs=[smem((TABLE_WIDTH,), lambda i: (i,)),
                  smem((TABLE_WIDTH,), lambda i: (jnp.maximum(i - 1, 0),)),
                  smem((TABLE_WIDTH,), lambda i: (0,)),
                  pl.BlockSpec((None, TOP_K, tmd), lambda i: (jnp.minimum(i, last), 0, 0)),
                  pl.BlockSpec((None, TOP_K, tmd), lambda i: (jnp.minimum(i, last), 0, 0)),
                  pl.BlockSpec((None, N_EXPERTS, 1), lambda i: (i, 0, 0)),
                  pl.BlockSpec((tmd, D_MODEL), lambda i: (jnp.minimum(i, last), 0)),
                  pl.BlockSpec((TOP_K, n_sample), lambda i: (0, 0)),
                  pl.BlockSpec((TOP_K, n_sample), lambda i: (0, 0)),
                  pl.BlockSpec((n_sample, D_MODEL), lambda i: (0, 0))],
        out_specs=pl.BlockSpec(memory_space=pl.ANY),
        out_shape=jax.ShapeDtypeStruct((n_blocks * ROW_BLOCK, D_MODEL), F32),
        scratch_shapes=[pltpu.VMEM((2, _sorted_rows(tmd), D_MODEL), F32),
                        pltpu.VMEM((ROW_BLOCK, D_MODEL), F32), pltpu.SMEM((2,), I32),
                        pltpu.SemaphoreType.DMA((2,))],
        compiler_params=_cparams(("arbitrary",)),
        name="dispatch",
    )(tab, tab, fill, eid_p, rank_p, seg_start, xn_p, eid_s, rank_s, xn_s)


def _expert_kernel(tab_ref, xb_ref, wu_hbm, bu_ref, wd_hbm, bd_ref, yb_ref,
                   wu_f, wd_f, wu_s, wd_s, xbuf, ybuf, wsem, xsem, ysem, *, n_blocks):
    e = pl.program_id(0)
    first = tab_ref[e]
    nblk = tab_ref[N_EXPERTS + e]
    n_used = tab_ref[2 * N_EXPERTS]

    def hbm_rows(b):
        return pl.ds(pl.multiple_of(b * ROW_BLOCK, ROW_BLOCK), ROW_BLOCK)

    def x_copy(b, sl):
        return pltpu.make_async_copy(xb_ref.at[hbm_rows(b), :], xbuf.at[sl], xsem.at[sl])

    def y_copy(b, sl):
        return pltpu.make_async_copy(ybuf.at[sl], yb_ref.at[hbm_rows(b), :], ysem.at[sl])

    def w_copies(ex, sl):
        return (pltpu.make_async_copy(wu_hbm.at[ex], wu_f.at[sl], wsem.at[0, sl]),
                pltpu.make_async_copy(wd_hbm.at[ex], wd_f.at[sl], wsem.at[1, sl]))

    wslot = e % 2

    @pl.when(e == 0)
    def _():
        for c in w_copies(0, 0):
            c.start()
        for b in range(ROW_LOOKAHEAD):
            @pl.when(b < n_used)
            def _():
                x_copy(b, b % ROW_RING).start()

    @pl.when(e + 1 < N_EXPERTS)
    def _():
        for c in w_copies(e + 1, 1 - wslot):
            c.start()

    for c in w_copies(e, wslot):
        c.wait()

    step = 128

    def cast(r, c):
        rows = pl.ds(pl.multiple_of(r * step, step), step)
        wu_s[rows, :] = wu_f[wslot, rows, :].astype(BF16)
        wd_s[rows, :] = wd_f[wslot, rows, :].astype(BF16)
        return c
    lax.fori_loop(0, D_MODEL // step, cast, 0)

    def block(b, c):
        sl = b % ROW_RING
        x_copy(b, sl).wait()

        @pl.when(b + ROW_LOOKAHEAD < n_used)
        def _():
            x_copy(b + ROW_LOOKAHEAD, (b + ROW_LOOKAHEAD) % ROW_RING).start()

        @pl.when(b >= ROW_RING)
        def _():
            y_copy(b, sl).wait()

        hcat = _dot(xbuf[sl].astype(BF16), wu_s[...]) + bu_ref[...]
        glu = jnp.minimum(hcat[:, 0:D_FF], SWIGLU_LIMIT)
        lin = jnp.clip(hcat[:, D_FF:2 * D_FF], -SWIGLU_LIMIT, SWIGLU_LIMIT)
        act = glu * _sigmoid(SWIGLU_ALPHA * glu) * (lin + 1.0)
        ybuf[sl] = _dot(act.astype(BF16), wd_s[...]) + bd_ref[...]
        y_copy(b, sl).start()
        return c
    lax.fori_loop(first, first + nblk, block, 0)

    @pl.when(e == N_EXPERTS - 1)
    def _():
        for k in range(ROW_RING):
            @pl.when(k < n_used)
            def _():
                y_copy(0, (n_used - 1 - k) % ROW_RING).wait()

        ybuf[0] = jnp.zeros((ROW_BLOCK, D_MODEL), F32)

        def zissue(b, c):
            y_copy(b, 0).start()
            return c
        lax.fori_loop(n_used, n_blocks, zissue, 0)

        def zdrain(b, c):
            y_copy(0, 0).wait()
            return c
        lax.fori_loop(n_used, n_blocks, zdrain, 0)


def _experts(tab, xb, w_up, b_up, w_down, b_down, *, n_blocks):
    w_map = lambda e, tab: (e, 0, 0)
    grid_spec = pltpu.PrefetchScalarGridSpec(
        num_scalar_prefetch=1,
        grid=(N_EXPERTS,),
        in_specs=[pl.BlockSpec(memory_space=pl.ANY),
                  pl.BlockSpec(memory_space=pl.ANY),
                  pl.BlockSpec((None, 1, 2 * D_FF), w_map),
                  pl.BlockSpec(memory_space=pl.ANY),
                  pl.BlockSpec((None, 1, D_MODEL), w_map)],
        out_specs=pl.BlockSpec(memory_space=pl.ANY),
        scratch_shapes=[pltpu.VMEM((2, D_MODEL, 2 * D_FF), F32), pltpu.VMEM((2, D_FF, D_MODEL), F32),
                        pltpu.VMEM((D_MODEL, 2 * D_FF), BF16), pltpu.VMEM((D_FF, D_MODEL), BF16),
                        pltpu.VMEM((ROW_RING, ROW_BLOCK, D_MODEL), F32),
                        pltpu.VMEM((ROW_RING, ROW_BLOCK, D_MODEL), F32),
                        pltpu.SemaphoreType.DMA((2, 2)),
                        pltpu.SemaphoreType.DMA((ROW_RING,)), pltpu.SemaphoreType.DMA((ROW_RING,))],
    )
    return pl.pallas_call(
        functools.partial(_expert_kernel, n_blocks=n_blocks),
        grid_spec=grid_spec,
        out_shape=jax.ShapeDtypeStruct((n_blocks * ROW_BLOCK, D_MODEL), F32),
        compiler_params=_cparams(("arbitrary",)),
        name="experts",
    )(tab, xb, w_up, b_up, w_down, b_down)


def _split_bf16(a):
    hi = a.astype(BF16)
    return hi, (a - hi.astype(F32)).astype(BF16)


def _combine_kernel(tab_ref, nxt_ref, slot_ref, gate_ref, h2_ref, gfin_ref, yb_ref, y_ref, buf, sems,
                    *, tm):
    i = pl.program_id(0)
    n = pl.num_programs(0)
    cur = i % 2
    nrows = buf.shape[1]

    def granule(src_row, sl, dst_row):
        return pltpu.make_async_copy(
            yb_ref.at[pl.ds(pl.multiple_of(src_row, SUBLANES), SUBLANES), :],
            buf.at[sl, pl.ds(pl.multiple_of(dst_row, SUBLANES), SUBLANES), :], sems.at[sl])

    def gather(t_ref, sl, lo, hi):
        def issue(g, c):
            granule(t_ref[g], sl, g * SUBLANES).start()
            return c
        lax.fori_loop(lo, hi, issue, 0)

    @pl.when(i == 0)
    def _():
        buf[...] = jnp.zeros_like(buf)
        gather(tab_ref, 0, 0, tab_ref[TABLE_WIDTH - 1])

    def drain(g, c):
        granule(0, cur, 0).wait()
        return c
    lax.fori_loop(0, tab_ref[TABLE_WIDTH - 1], drain, 0)

    n_next = jnp.where(i + 1 < n, nxt_ref[TABLE_WIDTH - 1], 0)
    half = nrows // (2 * UNSORT_CHUNK) * UNSORT_CHUNK
    bounds = [0, half, nrows]
    burst = -(-(TABLE_WIDTH - 1) // 2)
    acc = h2_ref[...]
    for p in range(2):
        gather(nxt_ref, 1 - cur, p * burst, jnp.minimum((p + 1) * burst, n_next))
        lo, hi = bounds[p], bounds[p + 1]
        s_iota = lax.broadcasted_iota(I32, (tm, hi - lo), 1) + lo
        wgt = jnp.zeros((tm, hi - lo), F32)
        for kk in range(TOP_K):
            wgt = wgt + jnp.where(s_iota == slot_ref[:, kk:kk + 1], gate_ref[:, kk:kk + 1], 0.0)
        w_hi, w_lo = _split_bf16(wgt)
        y_hi, y_lo = _split_bf16(buf[cur, lo:hi, :])
        acc = acc + (_dot(w_hi, y_hi) + (_dot(w_hi, y_lo) + _dot(w_lo, y_hi)))
    y_ref[...] = _rms(acc, gfin_ref[...])


def _combine(tab, slot_col, gate_col, h2, gfin, yb, *, tm):
    ntok = h2.shape[0]
    n = ntok // tm
    smem = lambda imap: pl.BlockSpec((TABLE_WIDTH,), imap, memory_space=pltpu.SMEM)
    return pl.pallas_call(
        functools.partial(_combine_kernel, tm=tm),
        grid=(n,),
        in_specs=[smem(lambda i: (i,)), smem(lambda i: (jnp.minimum(i + 1, n - 1),)),
                  pl.BlockSpec((tm, TOP_K), lambda i: (i, 0)),
                  pl.BlockSpec((tm, TOP_K), lambda i: (i, 0)),
                  pl.BlockSpec((tm, D_MODEL), lambda i: (i, 0)),
                  pl.BlockSpec((1, D_MODEL), lambda i: (0, 0)),
                  pl.BlockSpec(memory_space=pl.ANY)],
        out_specs=pl.BlockSpec((tm, D_MODEL), lambda i: (i, 0)),
        out_shape=jax.ShapeDtypeStruct((ntok, D_MODEL), F32),
        scratch_shapes=[pltpu.VMEM((2, _sorted_rows(tm), D_MODEL), F32),
                        pltpu.SemaphoreType.DMA((2,))],
        compiler_params=_cparams(("arbitrary",)),
        name="combine",
    )(tab, tab, slot_col, gate_col, h2, gfin, yb)


def _tiles(a, tile):
    bsz, kk, seq = a.shape
    return a.reshape(bsz, kk, seq // tile, tile).transpose(0, 2, 1, 3).reshape(-1, kk, tile)


def _path(x, cbuf, c0, n0, m0, mkb, mvb, wts, *, tm_in, ct, chunk, tm_post, sub, group):
    q, k, v, og, gcol, grow, yc, nbuf = _inproj(
        x, wts["g_mix"], wts["wq"], wts["wg"], wts["wgt"], wts["wc"], wts["bg"], wts["bgt"],
        wts["cw"], cbuf, tm=tm_in, chunk=chunk)
    ym, c1, n1, m1 = _mlstm(q, k, v, og, gcol, grow, c0, n0, m0, wts["ng"], ct=ct, chunk=chunk)
    h2, xn, eid, gate, rank, cnt = _post(
        ym, yc, x, wts["wmo"], wts["g_x"], wts["wxq"], mkb, mvb, wts["wxo"], wts["g_ffn"],
        wts["wrt"], wts["br"], tm=tm_post, sub=sub, group=group)
    return dict(h2=h2, xn=xn, eid=eid, gate=gate, rank=rank, cnt=cnt[:, :, 0],
                c1=c1, n1=n1, m1=m1[..., 0], nbuf=nbuf)


def kernel(x_prompt, x_sample, state_mlstm_c, state_mlstm_n, state_mlstm_m, state_conv, cache_mem_k, cache_mem_v, mem_prompt, norm_mix_g, w_in, b_gate, mlstm_norm_g, conv_w, w_mix_out, norm_x_g, norm_mem_g, w_xq, w_xk, w_xv, w_xo, norm_ffn_g, w_router, b_router, w_up, b_up, w_down, b_down, norm_final_g):
    bp, lp, _ = x_prompt.shape
    bs, ls, _ = x_sample.shape
    l = 0
    row = lambda a: a.reshape(1, -1)

    wi = w_in[l]
    gate_cols = wi[:, 4 * D_MLSTM:4 * D_MLSTM + 2 * N_HEADS]
    wts = dict(
        g_mix=row(norm_mix_g[l]),
        wq=wi[:, 0:4 * D_MLSTM].astype(BF16),
        wg=jnp.pad(gate_cols, ((0, 0), (0, 128 - 2 * N_HEADS))).astype(BF16),
        wgt=gate_cols.T.astype(BF16),
        wc=wi[:, 4 * D_MLSTM + 2 * N_HEADS:].astype(BF16),
        bg=row(b_gate[l]), bgt=b_gate[l].reshape(-1, 1),
        cw=conv_w[l], ng=row(mlstm_norm_g[l]),
        wmo=w_mix_out[l].astype(BF16), g_x=row(norm_x_g[l]), wxq=w_xq[l].astype(BF16),
        wxo=w_xo[l].astype(BF16), g_ffn=row(norm_ffn_g[l]),
        wrt=w_router[l].T.astype(BF16), br=b_router[l].reshape(-1, 1),
    )

    mk, mv, mkb, mvb = _memkv(mem_prompt.reshape(bp * N_MEM, D_MODEL), row(norm_mem_g[l]),
                              w_xk[l].astype(BF16), w_xv[l].astype(BF16))
    zeros = lambda *s: jnp.zeros(s, F32)
    tm_post, tmd = 512, 256
    pr = _path(x_prompt, zeros(bp, CONV_WIDTH - 1, D_CONV), zeros(bp, N_HEADS, HEAD_DIM, HEAD_DIM),
               zeros(bp, N_HEADS, HEAD_DIM), zeros(bp, N_HEADS, 1),
               mkb.reshape(bp, N_MEM, D_MODEL), mvb.reshape(bp, N_MEM, D_MODEL), wts,
               tm_in=512, ct=512, chunk=CHUNK, tm_post=tm_post, sub=tmd, group=1)
    sa = _path(x_sample, state_conv[l], state_mlstm_c[l], state_mlstm_n[l],
               state_mlstm_m[l][..., None],
               cache_mem_k[l].reshape(bs, N_MEM, D_MODEL).astype(BF16),
               cache_mem_v[l].reshape(bs, N_MEM, D_MODEL).astype(BF16), wts,
               tm_in=ls, ct=ls, chunk=min(CHUNK, ls), tm_post=ls, sub=ls, group=bs)

    n_ptok, n_stok = bp * lp, bs * ls
    cnt = jnp.concatenate([pr["cnt"], sa["cnt"]], axis=0)
    n_tiles = cnt.shape[0]
    n_blocks = -(-(TOP_K * (n_ptok + n_stok) + n_tiles * N_EXPERTS * (SUBLANES - 1)) // ROW_BLOCK) \
        + N_EXPERTS
    seg = (cnt + SUBLANES - 1) // SUBLANES * SUBLANES
    seg_src = jnp.cumsum(seg, axis=1) - seg
    tot = jnp.sum(seg, axis=0)
    blocks_e = (tot + ROW_BLOCK - 1) // ROW_BLOCK
    padded = blocks_e * ROW_BLOCK
    pstart = jnp.cumsum(padded) - padded
    bend = jnp.cumsum(blocks_e)
    n_used = bend[-1]
    exp_tab = jnp.concatenate([bend - blocks_e, blocks_e, n_used[None]]).astype(I32)
    seg_dst = pstart[None, :] + jnp.cumsum(seg, axis=0) - seg

    grow0 = jnp.arange(TABLE_WIDTH - 1, dtype=I32) * SUBLANES
    owner = jnp.sum((seg_src + seg)[:, None, :] <= grow0[None, :, None], axis=-1)
    shift = jnp.sum(jnp.where(owner[..., None] == jnp.arange(N_EXPERTS, dtype=I32),
                              (seg_dst - seg_src)[:, None, :], 0), axis=-1)
    n_gran = jnp.sum(seg, axis=1, keepdims=True) // SUBLANES
    gdst = jnp.where(grow0[None, :] < n_gran * SUBLANES, grow0[None, :] + shift, 0)
    tab = jnp.concatenate([gdst, n_gran], axis=1).astype(I32).reshape(-1)
    fgran = (padded - tot) // SUBLANES
    fill = jnp.concatenate([pstart + tot, fgran, jnp.sum(fgran)[None], n_used[None]])
    fill = jnp.pad(fill, (0, TABLE_WIDTH - fill.shape[0])).astype(I32)

    def lookup(table, eid):
        hit = eid[..., None] == jnp.arange(N_EXPERTS, dtype=I32)
        return jnp.sum(jnp.where(hit, table[:, None, None, :], 0), axis=-1)

    eid_pt, rank_pt = _tiles(pr["eid"], tmd), _tiles(pr["rank"], tmd)
    eid_st = sa["eid"].transpose(1, 0, 2).reshape(1, TOP_K, n_stok)
    rank_st = sa["rank"].transpose(1, 0, 2).reshape(1, TOP_K, n_stok)
    slot_p = (lookup(seg_src[:-1], eid_pt) + rank_pt).transpose(0, 2, 1).reshape(n_ptok, TOP_K)
    slot_s = (lookup(seg_src[-1:], eid_st) + rank_st).transpose(0, 2, 1).reshape(n_stok, TOP_K)

    xb = _dispatch(tab, fill, eid_pt, rank_pt, seg_src.astype(F32)[..., None],
                   pr["xn"].reshape(n_ptok, D_MODEL), eid_st[0], rank_st[0],
                   sa["xn"].reshape(n_stok, D_MODEL), tmd=tmd, n_blocks=n_blocks)
    yb = _experts(exp_tab, xb, w_up[l], b_up[l][:, None, :], w_down[l], b_down[l][:, None, :],
                  n_blocks=n_blocks)

    gfin = row(norm_final_g)
    split = (n_tiles - 1) * TABLE_WIDTH
    y_p = _combine(tab[:split], slot_p.astype(I32), pr["gate"].transpose(0, 2, 1).reshape(n_ptok, TOP_K),
                   pr["h2"].reshape(n_ptok, D_MODEL), gfin, yb, tm=tmd)
    y_s = _combine(tab[split:], slot_s.astype(I32), sa["gate"].transpose(0, 2, 1).reshape(n_stok, TOP_K),
                   sa["h2"].reshape(n_stok, D_MODEL), gfin, yb, tm=n_stok)

    lead = lambda a: a[None]
    return (y_p.reshape(bp, lp, D_MODEL), y_s.reshape(bs, ls, D_MODEL),
            lead(pr["c1"]), lead(pr["n1"]), lead(pr["m1"]), lead(pr["nbuf"]),
            lead(mk.reshape(bp, N_MEM, N_XHEADS, XHEAD_DIM)),
            lead(mv.reshape(bp, N_MEM, N_XHEADS, XHEAD_DIM)),
            lead(sa["c1"]), lead(sa["n1"]), lead(sa["m1"]), lead(sa["nbuf"]))
```

```python
import functools

import jax
import jax.numpy as jnp
import numpy as np
from jax import lax
from jax.experimental import pallas as pl
from jax.experimental.pallas import tpu as pltpu

F32 = jnp.float32
BF16 = jnp.bfloat16
I32 = jnp.int32

D_MODEL = 1024
N_HEADS = 4
HEAD_DIM = 128
D_MLSTM = N_HEADS * HEAD_DIM
D_CONV = D_MODEL - D_MLSTM
CONV_WIDTH = 3
CHUNK = 64
N_MEM = 256
N_XHEADS = 4
XHEAD_DIM = D_MODEL // N_XHEADS
N_EXPERTS = 32
TOP_K = 4
D_FF = D_MODEL
SWIGLU_LIMIT = 7.0
SWIGLU_ALPHA = 1.702
EPS = 1e-5

SUBLANES = 8
BIG_ROWS = 4 * SUBLANES
MAX_BIG, MAX_SMALL = 64, 3 * N_EXPERTS
TAB_BIG_DST, TAB_BIG_SRC = 0, MAX_BIG
TAB_SMALL_DST, TAB_SMALL_SRC = 2 * MAX_BIG, 2 * MAX_BIG + MAX_SMALL
TAB_COUNTS = 2 * MAX_BIG + 2 * MAX_SMALL
TABLE_WIDTH = 512
ROW_BLOCK = 256
ROW_LOOKAHEAD = 3
ROW_RING = ROW_LOOKAHEAD + 1
VMEM_LIMIT = 56 * 1024 * 1024


def _cparams(sem):
    return pltpu.CompilerParams(dimension_semantics=sem, vmem_limit_bytes=VMEM_LIMIT)


def _rms(x, g):
    return x * lax.rsqrt(jnp.mean(x * x, axis=-1, keepdims=True) + EPS) * g


def _log_sigmoid(x):
    return -(jnp.maximum(-x, 0.0) + jnp.log1p(jnp.exp(-jnp.abs(x))))


def _sigmoid(x):
    return 1.0 / (1.0 + jnp.exp(-x))


def _dot(a, b):
    return jnp.dot(a, b, preferred_element_type=F32)


def _dot_nt(a, b):
    return lax.dot_general(a, b, (((1,), (1,)), ((), ())), preferred_element_type=F32)


def _dot_tn(a, b):
    return lax.dot_general(a, b, (((0,), (0,)), ((), ())), preferred_element_type=F32)


def _memkv_kernel(mem_ref, g_ref, wk_ref, wv_ref, mk_ref, mv_ref, mkb_ref, mvb_ref):
    mn = _rms(mem_ref[...], g_ref[...]).astype(BF16)
    mk = _dot(mn, wk_ref[...])
    mv = _dot(mn, wv_ref[...])
    mk_ref[...] = mk
    mv_ref[...] = mv
    mkb_ref[...] = mk.astype(BF16)
    mvb_ref[...] = mv.astype(BF16)


def _memkv(mem2d, g, wk, wv):
    rows = mem2d.shape[0]
    tm = N_MEM
    row_spec = pl.BlockSpec((tm, D_MODEL), lambda i: (i, 0))
    full = lambda shape: pl.BlockSpec(shape, lambda i: (0,) * len(shape))
    return pl.pallas_call(
        _memkv_kernel,
        grid=(rows // tm,),
        in_specs=[row_spec, full((1, D_MODEL)), full((D_MODEL, D_MODEL)), full((D_MODEL, D_MODEL))],
        out_specs=[row_spec, row_spec, row_spec, row_spec],
        out_shape=[jax.ShapeDtypeStruct((rows, D_MODEL), F32)] * 2
        + [jax.ShapeDtypeStruct((rows, D_MODEL), BF16)] * 2,
        compiler_params=_cparams(("arbitrary",)),
        name="memkv",
    )(mem2d, g, wk, wv)


def _inproj_kernel(x_ref, g_ref, wq_ref, wg_ref, wgt_ref, wc_ref, bg_ref, bgt_ref, cw_ref, cbuf_ref,
                   q_ref, k_ref, v_ref, og_ref, gcol_ref, grow_ref, yc_ref, nbuf_ref,
                   carry_ref, *, tm, chunk):
    j = pl.program_id(1)

    @pl.when(j == 0)
    def _():
        carry_ref[0:2, :] = cbuf_ref[...]

    xb = _rms(x_ref[...], g_ref[...]).astype(BF16)

    p = _dot(xb, wq_ref[...])
    q_ref[...] = p[:, 0:D_MLSTM].astype(BF16)
    k_ref[...] = (p[:, D_MLSTM:2 * D_MLSTM] * (HEAD_DIM ** -0.5)).astype(BF16)
    v_ref[...] = p[:, 2 * D_MLSTM:3 * D_MLSTM].astype(BF16)
    og_ref[...] = _sigmoid(p[:, 3 * D_MLSTM:4 * D_MLSTM])

    gc = _dot(xb, wg_ref[...])[:, 0:2 * N_HEADS] + bg_ref[...]
    col = lax.broadcasted_iota(I32, gc.shape, 1)
    gcol_ref[...] = jnp.where(col < N_HEADS, gc, _log_sigmoid(gc))
    gr = _dot_nt(wgt_ref[...], xb) + bgt_ref[...]
    row = lax.broadcasted_iota(I32, gr.shape, 0)
    gr = jnp.where(row < N_HEADS, gr, _log_sigmoid(gr))
    for c in range(tm // chunk):
        grow_ref[c] = gr[:, c * chunk:(c + 1) * chunk]

    pc = _dot(xb, wc_ref[...])
    cb = pc[:, 0:D_CONV]
    u = pc[:, D_CONV:2 * D_CONV] * pc[:, 2 * D_CONV:3 * D_CONV]
    c0 = carry_ref[0:1, :]
    c1 = carry_ref[1:2, :]
    rid = lax.broadcasted_iota(I32, u.shape, 0)
    um1 = jnp.where(rid == 0, c1, pltpu.roll(u, 1, 0))
    um2 = jnp.where(rid == 0, c0, jnp.where(rid == 1, c1, pltpu.roll(u, 2, 0)))
    yc = cw_ref[0:1, :] * um2 + cw_ref[1:2, :] * um1 + cw_ref[2:3, :] * u
    yc_ref[...] = (cb * yc).astype(BF16)
    tail = u[tm - 2:tm, :]
    carry_ref[0:2, :] = tail
    nbuf_ref[...] = tail


def _inproj(x, g, wq, wg, wgt, wc, bg, bgt, cw, cbuf, *, tm, chunk):
    bsz, seq, _ = x.shape
    grid = (bsz, seq // tm)
    tok = lambda c: pl.BlockSpec((None, tm, c), lambda b, j: (b, j, 0))
    full = lambda shape: pl.BlockSpec(shape, lambda b, j: (0,) * len(shape))
    nck = tm // chunk
    return pl.pallas_call(
        functools.partial(_inproj_kernel, tm=tm, chunk=chunk),
        grid=grid,
        in_specs=[tok(D_MODEL), full((1, D_MODEL)), full((D_MODEL, 4 * D_MLSTM)),
                  full((D_MODEL, 128)), full((8, D_MODEL)), full((D_MODEL, 3 * D_CONV)),
                  full((1, 8)), full((8, 1)), full((CONV_WIDTH, D_CONV)),
                  pl.BlockSpec((None, 2, D_CONV), lambda b, j: (b, 0, 0))],
        out_specs=[tok(D_MLSTM), tok(D_MLSTM), tok(D_MLSTM), tok(D_MLSTM), tok(8),
                   pl.BlockSpec((None, nck, 8, chunk), lambda b, j: (b, j, 0, 0)),
                   tok(D_CONV),
                   pl.BlockSpec((None, 2, D_CONV), lambda b, j: (b, 0, 0))],
        out_shape=[jax.ShapeDtypeStruct((bsz, seq, D_MLSTM), BF16)] * 3
        + [jax.ShapeDtypeStruct((bsz, seq, D_MLSTM), F32),
           jax.ShapeDtypeStruct((bsz, seq, 8), F32),
           jax.ShapeDtypeStruct((bsz, seq // chunk, 8, chunk), F32),
           jax.ShapeDtypeStruct((bsz, seq, D_CONV), BF16),
           jax.ShapeDtypeStruct((bsz, 2, D_CONV), F32)],
        scratch_shapes=[pltpu.VMEM((8, D_CONV), F32)],
        compiler_params=_cparams(("arbitrary", "arbitrary")),
        name="inproj",
    )(x, g, wq, wg, wgt, wc, bg, bgt, cw, cbuf)


def _mlstm_kernel(q_ref, k_ref, v_ref, og_ref, gc_ref, gr_ref, c0_ref, n0_ref, m0_ref, ng_ref,
                  ym_ref, c1_ref, n1_ref, m1_ref, c_s, n_s, m_s, *, chunk, nchunks, bsz):
    j = pl.program_id(0)

    @pl.when(j == 0)
    def _():
        c_s[...] = c0_ref[...]
        n_s[...] = n0_ref[...]
        m_s[...] = m0_ref[...]

    ti = lax.broadcasted_iota(I32, (chunk, chunk), 0)
    ji = lax.broadcasted_iota(I32, (chunk, chunk), 1)
    causal = ji <= ti

    def body(ci, carry):
        r0 = pl.multiple_of(ci * chunk, chunk)
        rows = pl.ds(r0, chunk)
        chains = [(b, h) for b in range(bsz) for h in range(N_HEADS)]
        cols = lambda h: slice(h * HEAD_DIM, (h + 1) * HEAD_DIM)
        each = lambda f: [f(n, b, h) for n, (b, h) in enumerate(chains)]
        q = lambda b, h: q_ref[b, rows, cols(h)]
        k = lambda b, h: k_ref[b, rows, cols(h)]
        v = lambda b, h: v_ref[b, rows, cols(h)]
        gcs = [gc_ref[b, rows, :] for b in range(bsz)]
        grs = [gr_ref[b, ci] for b in range(bsz)]
        li_c = each(lambda n, b, h: gcs[b][:, h:h + 1])
        lf_c = each(lambda n, b, h: gcs[b][:, N_HEADS + h:N_HEADS + h + 1])
        li_r = each(lambda n, b, h: grs[b][h:h + 1, :])
        lf_r = each(lambda n, b, h: grs[b][N_HEADS + h:N_HEADS + h + 1, :])
        m_prev = each(lambda n, b, h: m_s[b, h:h + 1, :])

        b_c = each(lambda n, b, h: jnp.sum(jnp.where(causal, lf_r[n], 0.0), axis=1, keepdims=True))
        b_r = each(lambda n, b, h: jnp.sum(jnp.where(ti <= ji, lf_c[n], 0.0), axis=0, keepdims=True))
        dmat = each(lambda n, b, h: jnp.where(causal, b_c[n] - b_r[n] + li_r[n], -jnp.inf))
        dmax = each(lambda n, b, h: jnp.max(dmat[n], axis=1, keepdims=True))
        inter = each(lambda n, b, h: b_c[n] + m_prev[n])
        m_t = each(lambda n, b, h: jnp.maximum(inter[n], dmax[n]))
        w_inter = each(lambda n, b, h: jnp.exp(inter[n] - m_t[n]))
        s = each(lambda n, b, h: _dot_nt(q(b, h), k(b, h)) * jnp.exp(dmat[n] - m_t[n]))
        qc = each(lambda n, b, h: _dot(q(b, h), c_s[b, h].astype(BF16)))
        sv = each(lambda n, b, h: _dot(s[n].astype(BF16), v(b, h)))
        qn = each(lambda n, b, h: jnp.sum(q(b, h).astype(F32) * n_s[b, h:h + 1, :], axis=1,
                                          keepdims=True))
        den = each(lambda n, b, h: w_inter[n] * qn[n] + jnp.sum(s[n], axis=1, keepdims=True))
        hh = each(lambda n, b, h: (w_inter[n] * qc[n] + sv[n])
                  / jnp.maximum(jnp.abs(den[n]), jnp.exp(-m_t[n])))

        m_new = each(lambda n, b, h: m_t[n][chunk - 1:chunk, :])
        b_last = each(lambda n, b, h: b_c[n][chunk - 1:chunk, :])
        decay = each(lambda n, b, h: jnp.exp(b_last[n] + m_prev[n] - m_new[n]))
        kw = each(lambda n, b, h: k(b, h).astype(F32)
                  * jnp.exp(b_last[n] - b_c[n] + li_c[n] - m_new[n]))
        kv = each(lambda n, b, h: _dot_tn(kw[n].astype(BF16), v(b, h)))
        for n, (b, h) in enumerate(chains):
            c_s[b, h] = decay[n] * c_s[b, h] + kv[n]
            n_s[b, h:h + 1, :] = decay[n] * n_s[b, h:h + 1, :] + jnp.sum(kw[n], axis=0, keepdims=True)
            m_s[b, h:h + 1, :] = m_new[n]

        hn = each(lambda n, b, h: hh[n] * lax.rsqrt(jnp.mean(hh[n] * hh[n], axis=1, keepdims=True) + EPS)
                  * ng_ref[:, cols(h)])
        for n, (b, h) in enumerate(chains):
            ym_ref[b, rows, cols(h)] = (hn[n] * og_ref[b, rows, cols(h)]).astype(BF16)
        return carry

    lax.fori_loop(0, nchunks, body, 0)

    @pl.when(j == pl.num_programs(0) - 1)
    def _():
        c1_ref[...] = c_s[...]
        n1_ref[...] = n_s[...]
        m1_ref[...] = m_s[...]


def _mlstm(q, k, v, og, gcol, grow, c0, n0, m0, ng, *, ct, chunk):
    bsz, seq, _ = q.shape
    nchunks = ct // chunk
    grid = (seq // ct,)
    tok = lambda c: pl.BlockSpec((bsz, ct, c), lambda j: (0, j, 0))
    st_c = pl.BlockSpec((bsz, N_HEADS, HEAD_DIM, HEAD_DIM), lambda j: (0, 0, 0, 0))
    st_n = pl.BlockSpec((bsz, N_HEADS, HEAD_DIM), lambda j: (0, 0, 0))
    st_m = pl.BlockSpec((bsz, N_HEADS, 1), lambda j: (0, 0, 0))
    return pl.pallas_call(
        functools.partial(_mlstm_kernel, chunk=chunk, nchunks=nchunks, bsz=bsz),
        grid=grid,
        in_specs=[tok(D_MLSTM), tok(D_MLSTM), tok(D_MLSTM), tok(D_MLSTM), tok(8),
                  pl.BlockSpec((bsz, nchunks, 8, chunk), lambda j: (0, j, 0, 0)),
                  st_c, st_n, st_m,
                  pl.BlockSpec((1, D_MLSTM), lambda j: (0, 0))],
        out_specs=[tok(D_MLSTM), st_c, st_n, st_m],
        out_shape=[jax.ShapeDtypeStruct((bsz, seq, D_MLSTM), BF16),
                   jax.ShapeDtypeStruct((bsz, N_HEADS, HEAD_DIM, HEAD_DIM), F32),
                   jax.ShapeDtypeStruct((bsz, N_HEADS, HEAD_DIM), F32),
                   jax.ShapeDtypeStruct((bsz, N_HEADS, 1), F32)],
        scratch_shapes=[pltpu.VMEM((bsz, N_HEADS, HEAD_DIM, HEAD_DIM), F32),
                        pltpu.VMEM((bsz, N_HEADS, HEAD_DIM), F32),
                        pltpu.VMEM((bsz, N_HEADS, 1), F32)],
        compiler_params=_cparams(("arbitrary",)),
        name="mlstm",
    )(q, k, v, og, gcol, grow, c0, n0, m0, ng)


def _post_kernel(ym_ref, yc_ref, x_ref, wmo_ref, gx_ref, wxq_ref, mk_ref, mv_ref, wxo_ref,
                 gf_ref, wrt_ref, br_ref,
                 h2_ref, xn_ref, eid_ref, gate_ref, rank_ref, cnt_ref, cnt_s, *, tm, sub, group):
    step = pl.program_id(0) * pl.num_programs(1) + pl.program_id(1)

    @pl.when(step % group == 0)
    def _():
        cnt_s[...] = jnp.zeros_like(cnt_s)

    mix = _dot(ym_ref[...], wmo_ref[0:D_MLSTM, :]) + _dot(yc_ref[...], wmo_ref[D_MLSTM:D_MODEL, :])
    h1 = x_ref[...] + mix

    xq = _dot(_rms(h1, gx_ref[...]).astype(BF16), wxq_ref[...]).astype(BF16)
    att = jnp.zeros((tm, D_MODEL), F32)
    for hd in range(N_XHEADS):
        cols = slice(hd * XHEAD_DIM, (hd + 1) * XHEAD_DIM)
        s = _dot_nt(xq[:, cols], mk_ref[:, cols]) * (XHEAD_DIM ** -0.5)
        e = jnp.exp(s - jnp.max(s, axis=-1, keepdims=True))
        p = (e / jnp.sum(e, axis=-1, keepdims=True)).astype(BF16)
        o = _dot(p, mv_ref[:, cols]).astype(BF16)
        att = att + _dot(o, wxo_ref[cols, :])
    h2 = h1 + att
    h2_ref[...] = h2

    xn2 = _rms(h2, gf_ref[...]).astype(BF16)
    xn_ref[...] = xn2

    logits = _dot_nt(wrt_ref[...], xn2) + br_ref[...]
    eidx = lax.broadcasted_iota(I32, logits.shape, 0).astype(F32)
    work = logits
    vals, ids, hots = [], [], []
    for _ in range(TOP_K):
        mx = jnp.max(work, axis=0, keepdims=True)
        idx = jnp.min(jnp.where(work == mx, eidx, float(N_EXPERTS)), axis=0, keepdims=True)
        sel = eidx == idx
        vals.append(mx)
        ids.append(idx)
        hots.append(sel)
        work = jnp.where(sel, -jnp.inf, work)
    exps = [jnp.exp(v - vals[0]) for v in vals]
    denom = exps[0] + exps[1] + exps[2] + exps[3]

    picked = jnp.zeros(logits.shape, F32)
    for sel in hots:
        picked = picked + sel.astype(F32)
    shift = jnp.full((tm, tm), sub.bit_length() - 1, I32)
    tj = lax.broadcasted_iota(I32, (tm, tm), 0)
    tt = lax.broadcasted_iota(I32, (tm, tm), 1)
    same = lax.shift_right_logical(tj, shift) == lax.shift_right_logical(tt, shift)
    before = jnp.where(jnp.logical_and(tj < tt, same), 1.0, 0.0).astype(BF16)
    prior = _dot(picked.astype(BF16), before) + cnt_s[...]
    for kk in range(TOP_K):
        eid_ref[kk:kk + 1, :] = ids[kk].astype(I32)
        gate_ref[kk:kk + 1, :] = exps[kk] / denom
        rank_ref[kk:kk + 1, :] = jnp.sum(jnp.where(hots[kk], prior, 0.0), axis=0,
                                         keepdims=True).astype(I32)
    for s in range(tm // sub):
        total = cnt_s[...] + jnp.sum(picked[:, s * sub:(s + 1) * sub], axis=1, keepdims=True)
        cnt_ref[s] = total.astype(I32)
    cnt_s[...] = total


def _post(ym, yc, x, wmo, gx, wxq, mkb, mvb, wxo, gf, wrt, br, *, tm, sub, group):
    bsz, seq, _ = x.shape
    nj = seq // tm
    grid = (bsz, nj)
    nsub = tm // sub
    n_tiles = bsz * nj * nsub // group
    tok = lambda c: pl.BlockSpec((None, tm, c), lambda b, j: (b, j, 0))
    full = lambda shape: pl.BlockSpec(shape, lambda b, j: (0,) * len(shape))
    mem = pl.BlockSpec((None, N_MEM, D_MODEL), lambda b, j: (b, 0, 0))
    sel = pl.BlockSpec((None, TOP_K, tm), lambda b, j: (b, 0, j))
    return pl.pallas_call(
        functools.partial(_post_kernel, tm=tm, sub=sub, group=group),
        grid=grid,
        in_specs=[tok(D_MLSTM), tok(D_CONV), tok(D_MODEL), full((D_MODEL, D_MODEL)),
                  full((1, D_MODEL)), full((D_MODEL, D_MODEL)), mem, mem,
                  full((D_MODEL, D_MODEL)), full((1, D_MODEL)), full((N_EXPERTS, D_MODEL)),
                  full((N_EXPERTS, 1))],
        out_specs=[tok(D_MODEL), tok(D_MODEL), sel, sel, sel,
                   pl.BlockSpec((nsub, N_EXPERTS, 1), lambda b, j: ((b * nj + j) // group, 0, 0))],
        out_shape=[jax.ShapeDtypeStruct((bsz, seq, D_MODEL), F32),
                   jax.ShapeDtypeStruct((bsz, seq, D_MODEL), BF16),
                   jax.ShapeDtypeStruct((bsz, TOP_K, seq), I32),
                   jax.ShapeDtypeStruct((bsz, TOP_K, seq), F32),
                   jax.ShapeDtypeStruct((bsz, TOP_K, seq), I32),
                   jax.ShapeDtypeStruct((n_tiles, N_EXPERTS, 1), I32)],
        scratch_shapes=[pltpu.VMEM((N_EXPERTS, 1), F32)],
        compiler_params=_cparams(("arbitrary", "arbitrary")),
        name="post",
    )(ym, yc, x, wmo, gx, wxq, mkb, mvb, wxo, gf, wrt, br)


def _sorted_rows(n_tokens):
    return -(-(TOP_K * n_tokens + N_EXPERTS * (SUBLANES - 1)) // ROW_BLOCK) * ROW_BLOCK


def _dispatch_kernel(tab_ref, fill_ref, eid_ref, rank_ref, ls_ref, x_ref, eids_ref, ranks_ref, xs_ref,
                     xb_ref, srt, zero_s, pending, sems, *, n_ptiles, n_blocks):
    i = pl.program_id(0)
    cur = i % 2

    def rows_copy(src, src_row, dst_row, nrows, sl):
        return pltpu.make_async_copy(
            src.at[pl.ds(pl.multiple_of(src_row, SUBLANES), nrows), :],
            xb_ref.at[pl.ds(pl.multiple_of(dst_row, SUBLANES), nrows), :], sems.at[sl])

    def granule(src, src_row, dst_row, sl):
        return rows_copy(src, src_row, dst_row, SUBLANES, sl)

    def drain(count, sl, nrows=SUBLANES):
        def body(g, c):
            rows_copy(zero_s, 0, 0, nrows, sl).wait()
            return c
        lax.fori_loop(0, count, body, 0)

    @pl.when(i == 0)
    def _():
        pending[0] = 0
        pending[1] = 0

    def issue_list(count, dst_at, src_at, nrows):
        def body(h, c):
            g = 2 * h
            rows_copy(srt.at[cur], tab_ref[src_at + g], tab_ref[dst_at + g], nrows, cur).start(priority=0)

            @pl.when(g + 1 < count)
            def _():
                rows_copy(srt.at[cur], tab_ref[src_at + g + 1], tab_ref[dst_at + g + 1], nrows,
                          cur).start(priority=1)
            return c
        lax.fori_loop(0, (count + 1) // 2, body, 0)

    def sort_and_move(eid, rank, x):
        ntok = x.shape[0]
        nrows = _sorted_rows(ntok)
        e_iota = lax.broadcasted_iota(I32, (N_EXPERTS, ntok), 0)
        s_iota = lax.broadcasted_iota(I32, (nrows, ntok), 0)
        seg_start = ls_ref[...]
        hit = None
        for kk in range(TOP_K):
            start = jnp.sum(jnp.where(e_iota == eid[kk:kk + 1, :], seg_start, 0.0),
                            axis=0, keepdims=True).astype(I32)
            match = s_iota == start + rank[kk:kk + 1, :]
            hit = match if hit is None else jnp.logical_or(hit, match)
        perm = jnp.where(hit, 1.0, 0.0).astype(BF16)
        srt[cur, 0:nrows, :] = _dot(perm, x)
        drain(pending[0], 1 - cur, BIG_ROWS)
        drain(pending[1], 1 - cur)
        n_big, n_small = tab_ref[TAB_COUNTS], tab_ref[TAB_COUNTS + 1]
        issue_list(n_big, TAB_BIG_DST, TAB_BIG_SRC, BIG_ROWS)
        issue_list(n_small, TAB_SMALL_DST, TAB_SMALL_SRC, SUBLANES)
        pending[0] = n_big
        pending[1] = n_small

    @pl.when(i < n_ptiles)
    def _():
        sort_and_move(eid_ref[...], rank_ref[...], x_ref[...])

    @pl.when(i == n_ptiles)
    def _():
        sort_and_move(eids_ref[...], ranks_ref[...], xs_ref[...])
        drain(pending[0], cur, BIG_ROWS)
        drain(pending[1], cur)
        zero_s[...] = jnp.zeros_like(zero_s)
        for e in range(N_EXPERTS):
            dst = fill_ref[e]

            def zissue(g, c):
                granule(zero_s, 0, dst + g * SUBLANES, cur).start()
                return c
            lax.fori_loop(0, fill_ref[N_EXPERTS + e], zissue, 0)
        drain(fill_ref[2 * N_EXPERTS], cur)
        first_free = fill_ref[2 * N_EXPERTS + 1]

        def blk_copy(b):
            return pltpu.make_async_copy(
                zero_s, xb_ref.at[pl.ds(pl.multiple_of(b * ROW_BLOCK, ROW_BLOCK), ROW_BLOCK), :],
                sems.at[cur])

        def bissue(b, c):
            blk_copy(b).start()
            return c
        lax.fori_loop(first_free, n_blocks, bissue, 0)

        def bdrain(b, c):
            blk_copy(0).wait()
            return c
        lax.fori_loop(first_free, n_blocks, bdrain, 0)


def _dispatch(tab, fill, eid_p, rank_p, seg_start, xn_p, eid_s, rank_s, xn_s, *, tmd, n_blocks):
    n_ptiles = eid_p.shape[0]
    n_sample = xn_s.shape[0]
    last = n_ptiles - 1
    smem = lambda shape, imap: pl.BlockSpec(shape, imap, memory_space=pltpu.SMEM)
    return pl.pallas_call(
        functools.partial(_dispatch_kernel, n_ptiles=n_ptiles, n_blocks=n_blocks),
        grid=(n_ptiles + 1,),
        in_specs=[smem((TABLE_WIDTH,), lambda i: (i,)),
                  smem((TABLE_WIDTH,), lambda i: (0,)),
                  pl.BlockSpec((None, TOP_K, tmd), lambda i: (jnp.minimum(i, last), 0, 0)),
                  pl.BlockSpec((None, TOP_K, tmd), lambda i: (jnp.minimum(i, last), 0, 0)),
                  pl.BlockSpec((None, N_EXPERTS, 1), lambda i: (i, 0, 0)),
                  pl.BlockSpec((tmd, D_MODEL), lambda i: (jnp.minimum(i, last), 0)),
                  pl.BlockSpec((TOP_K, n_sample), lambda i: (0, 0)),
                  pl.BlockSpec((TOP_K, n_sample), lambda i: (0, 0)),
                  pl.BlockSpec((n_sample, D_MODEL), lambda i: (0, 0))],
        out_specs=pl.BlockSpec(memory_space=pl.ANY),
        out_shape=jax.ShapeDtypeStruct((n_blocks * ROW_BLOCK, D_MODEL), F32),
        scratch_shapes=[pltpu.VMEM((2, _sorted_rows(tmd), D_MODEL), F32),
                        pltpu.VMEM((ROW_BLOCK, D_MODEL), F32), pltpu.SMEM((2,), I32),
                        pltpu.SemaphoreType.DMA((2,))],
        compiler_params=_cparams(("arbitrary",)),
        name="dispatch",
    )(tab, fill, eid_p, rank_p, seg_start, xn_p, eid_s, rank_s, xn_s)


def _expert_kernel(tab_ref, xb_ref, wu_hbm, bu_ref, wd_hbm, bd_ref, yb_ref,
                   wu_f, wd_f, wu_s, wd_s, xbuf, ybuf, wsem, xsem, ysem, *, n_blocks):
    e = pl.program_id(0)
    first = tab_ref[e]
    nblk = tab_ref[N_EXPERTS + e]
    n_used = tab_ref[2 * N_EXPERTS]

    def hbm_rows(b):
        return pl.ds(pl.multiple_of(b * ROW_BLOCK, ROW_BLOCK), ROW_BLOCK)

    def x_copy(b, sl):
        return pltpu.make_async_copy(xb_ref.at[hbm_rows(b), :], xbuf.at[sl], xsem.at[sl])

    def y_copy(b, sl):
        return pltpu.make_async_copy(ybuf.at[sl], yb_ref.at[hbm_rows(b), :], ysem.at[sl])

    def w_copies(ex, sl):
        return (pltpu.make_async_copy(wu_hbm.at[ex], wu_f.at[sl], wsem.at[0, sl]),
                pltpu.make_async_copy(wd_hbm.at[ex], wd_f.at[sl], wsem.at[1, sl]))

    wslot = e % 2

    @pl.when(e == 0)
    def _():
        for c in w_copies(0, 0):
            c.start()
        for b in range(ROW_LOOKAHEAD):
            @pl.when(b < n_used)
            def _():
                x_copy(b, b % ROW_RING).start()

    @pl.when(e + 1 < N_EXPERTS)
    def _():
        for c in w_copies(e + 1, 1 - wslot):
            c.start()

    for c in w_copies(e, wslot):
        c.wait()

    step = 128

    def cast(r, c):
        rows = pl.ds(pl.multiple_of(r * step, step), step)
        wu_s[rows, :] = wu_f[wslot, rows, :].astype(BF16)
        wd_s[rows, :] = wd_f[wslot, rows, :].astype(BF16)
        return c
    lax.fori_loop(0, D_MODEL // step, cast, 0)

    def block(b, c):
        sl = b % ROW_RING
        x_copy(b, sl).wait()

        @pl.when(b + ROW_LOOKAHEAD < n_used)
        def _():
            x_copy(b + ROW_LOOKAHEAD, (b + ROW_LOOKAHEAD) % ROW_RING).start()

        @pl.when(b >= ROW_RING)
        def _():
            y_copy(b, sl).wait()

        hcat = _dot(xbuf[sl].astype(BF16), wu_s[...]) + bu_ref[...]
        glu = jnp.minimum(hcat[:, 0:D_FF], SWIGLU_LIMIT)
        lin = jnp.clip(hcat[:, D_FF:2 * D_FF], -SWIGLU_LIMIT, SWIGLU_LIMIT)
        act = glu * _sigmoid(SWIGLU_ALPHA * glu) * (lin + 1.0)
        ybuf[sl] = _dot(act.astype(BF16), wd_s[...]) + bd_ref[...]
        y_copy(b, sl).start()
        return c
    lax.fori_loop(first, first + nblk, block, 0)

    @pl.when(e == N_EXPERTS - 1)
    def _():
        for k in range(ROW_RING):
            @pl.when(k < n_used)
            def _():
                y_copy(0, (n_used - 1 - k) % ROW_RING).wait()

        ybuf[0] = jnp.zeros((ROW_BLOCK, D_MODEL), F32)

        def zissue(b, c):
            y_copy(b, 0).start()
            return c
        lax.fori_loop(n_used, n_blocks, zissue, 0)

        def zdrain(b, c):
            y_copy(0, 0).wait()
            return c
        lax.fori_loop(n_used, n_blocks, zdrain, 0)


def _experts(tab, xb, w_up, b_up, w_down, b_down, *, n_blocks):
    w_map = lambda e, tab: (e, 0, 0)
    grid_spec = pltpu.PrefetchScalarGridSpec(
        num_scalar_prefetch=1,
        grid=(N_EXPERTS,),
        in_specs=[pl.BlockSpec(memory_space=pl.ANY),
                  pl.BlockSpec(memory_space=pl.ANY),
                  pl.BlockSpec((None, 1, 2 * D_FF), w_map),
                  pl.BlockSpec(memory_space=pl.ANY),
                  pl.BlockSpec((None, 1, D_MODEL), w_map)],
        out_specs=pl.BlockSpec(memory_space=pl.ANY),
        scratch_shapes=[pltpu.VMEM((2, D_MODEL, 2 * D_FF), F32), pltpu.VMEM((2, D_FF, D_MODEL), F32),
                        pltpu.VMEM((D_MODEL, 2 * D_FF), BF16), pltpu.VMEM((D_FF, D_MODEL), BF16),
                        pltpu.VMEM((ROW_RING, ROW_BLOCK, D_MODEL), F32),
                        pltpu.VMEM((ROW_RING, ROW_BLOCK, D_MODEL), F32),
                        pltpu.SemaphoreType.DMA((2, 2)),
                        pltpu.SemaphoreType.DMA((ROW_RING,)), pltpu.SemaphoreType.DMA((ROW_RING,))],
    )
    return pl.pallas_call(
        functools.partial(_expert_kernel, n_blocks=n_blocks),
        grid_spec=grid_spec,
        out_shape=jax.ShapeDtypeStruct((n_blocks * ROW_BLOCK, D_MODEL), F32),
        compiler_params=_cparams(("arbitrary",)),
        name="experts",
    )(tab, xb, w_up, b_up, w_down, b_down)


def _split_bf16(a):
    hi = a.astype(BF16)
    return hi, (a - hi.astype(F32)).astype(BF16)


def _combine_kernel(tab_ref, nxt_ref, slot_ref, gate_ref, h2_ref, gfin_ref, yb_ref, y_ref, buf, sems,
                    *, tm):
    i = pl.program_id(0)
    n = pl.num_programs(0)
    cur = i % 2
    nrows = buf.shape[1]

    def rows_copy(src_row, sl, dst_row, nrows):
        return pltpu.make_async_copy(
            yb_ref.at[pl.ds(pl.multiple_of(src_row, SUBLANES), nrows), :],
            buf.at[sl, pl.ds(pl.multiple_of(dst_row, SUBLANES), nrows), :], sems.at[sl])

    def gather(t_ref, sl):
        def fetch(count, yb_at, tile_at, nrows):
            def body(h, c):
                g = 2 * h
                rows_copy(t_ref[yb_at + g], sl, t_ref[tile_at + g], nrows).start(priority=0)

                @pl.when(g + 1 < count)
                def _():
                    rows_copy(t_ref[yb_at + g + 1], sl, t_ref[tile_at + g + 1], nrows).start(priority=1)
                return c
            lax.fori_loop(0, (count + 1) // 2, body, 0)
        fetch(t_ref[TAB_COUNTS], TAB_BIG_DST, TAB_BIG_SRC, BIG_ROWS)
        fetch(t_ref[TAB_COUNTS + 1], TAB_SMALL_DST, TAB_SMALL_SRC, SUBLANES)

    @pl.when(i == 0)
    def _():
        buf[...] = jnp.zeros_like(buf)
        gather(tab_ref, 0)

    @pl.when(i + 1 < n)
    def _():
        gather(nxt_ref, 1 - cur)

    def drain(count, nrows):
        def body(g, c):
            rows_copy(0, cur, 0, nrows).wait()
            return c
        lax.fori_loop(0, count, body, 0)
    drain(tab_ref[TAB_COUNTS], BIG_ROWS)
    drain(tab_ref[TAB_COUNTS + 1], SUBLANES)

    s_iota = lax.broadcasted_iota(I32, (tm, nrows), 1)
    wgt = jnp.zeros((tm, nrows), F32)
    for kk in range(TOP_K):
        wgt = wgt + jnp.where(s_iota == slot_ref[:, kk:kk + 1], gate_ref[:, kk:kk + 1], 0.0)
    w_hi, w_lo = _split_bf16(wgt)
    y_hi, y_lo = _split_bf16(buf[cur])
    acc = _dot(w_hi, y_hi) + (_dot(w_hi, y_lo) + _dot(w_lo, y_hi))
    y_ref[...] = _rms(h2_ref[...] + acc, gfin_ref[...])


def _combine(tab, slot_col, gate_col, h2, gfin, yb, *, tm):
    ntok = h2.shape[0]
    n = ntok // tm
    smem = lambda imap: pl.BlockSpec((TABLE_WIDTH,), imap, memory_space=pltpu.SMEM)
    return pl.pallas_call(
        functools.partial(_combine_kernel, tm=tm),
        grid=(n,),
        in_specs=[smem(lambda i: (i,)), smem(lambda i: (jnp.minimum(i + 1, n - 1),)),
                  pl.BlockSpec((tm, TOP_K), lambda i: (i, 0)),
                  pl.BlockSpec((tm, TOP_K), lambda i: (i, 0)),
                  pl.BlockSpec((tm, D_MODEL), lambda i: (i, 0)),
                  pl.BlockSpec((1, D_MODEL), lambda i: (0, 0)),
                  pl.BlockSpec(memory_space=pl.ANY)],
        out_specs=pl.BlockSpec((tm, D_MODEL), lambda i: (i, 0)),
        out_shape=jax.ShapeDtypeStruct((ntok, D_MODEL), F32),
        scratch_shapes=[pltpu.VMEM((2, _sorted_rows(tm), D_MODEL), F32),
                        pltpu.SemaphoreType.DMA((2,))],
        compiler_params=_cparams(("arbitrary",)),
        name="combine",
    )(tab, tab, slot_col, gate_col, h2, gfin, yb)


def _copy_tables(seg, seg_src, seg_dst):
    experts = jnp.arange(N_EXPERTS, dtype=I32)

    def flatten(counts, bound, src0, dst0, step):
        ends = jnp.cumsum(counts, axis=1)
        idx = jnp.arange(bound, dtype=I32)
        owner = jnp.sum(ends[:, None, :] <= idx[None, :, None], axis=-1)
        pick = lambda a: jnp.sum(jnp.where(owner[..., None] == experts, a[:, None, :], 0), axis=-1)
        off = (idx[None, :] - pick(ends - counts)) * step
        valid = idx[None, :] < ends[:, -1:]
        return (jnp.where(valid, pick(dst0) + off, 0), jnp.where(valid, pick(src0) + off, 0),
                ends[:, -1:])

    n_big = seg // BIG_ROWS
    n_small = (seg - n_big * BIG_ROWS) // SUBLANES
    b_dst, b_src, b_n = flatten(n_big, MAX_BIG, seg_src, seg_dst, BIG_ROWS)
    s_dst, s_src, s_n = flatten(n_small, MAX_SMALL, seg_src + n_big * BIG_ROWS,
                                seg_dst + n_big * BIG_ROWS, SUBLANES)
    tab = jnp.concatenate([b_dst, b_src, s_dst, s_src, b_n, s_n], axis=1)
    return jnp.pad(tab, ((0, 0), (0, TABLE_WIDTH - tab.shape[1]))).astype(I32).reshape(-1)


def _tiles(a, tile):
    bsz, kk, seq = a.shape
    return a.reshape(bsz, kk, seq // tile, tile).transpose(0, 2, 1, 3).reshape(-1, kk, tile)


def _path(x, cbuf, c0, n0, m0, mkb, mvb, wts, *, tm_in, ct, chunk, tm_post, sub, group):
    q, k, v, og, gcol, grow, yc, nbuf = _inproj(
        x, wts["g_mix"], wts["wq"], wts["wg"], wts["wgt"], wts["wc"], wts["bg"], wts["bgt"],
        wts["cw"], cbuf, tm=tm_in, chunk=chunk)
    ym, c1, n1, m1 = _mlstm(q, k, v, og, gcol, grow, c0, n0, m0, wts["ng"], ct=ct, chunk=chunk)
    h2, xn, eid, gate, rank, cnt = _post(
        ym, yc, x, wts["wmo"], wts["g_x"], wts["wxq"], mkb, mvb, wts["wxo"], wts["g_ffn"],
        wts["wrt"], wts["br"], tm=tm_post, sub=sub, group=group)
    return dict(h2=h2, xn=xn, eid=eid, gate=gate, rank=rank, cnt=cnt[:, :, 0],
                c1=c1, n1=n1, m1=m1[..., 0], nbuf=nbuf)


def kernel(x_prompt, x_sample, state_mlstm_c, state_mlstm_n, state_mlstm_m, state_conv, cache_mem_k, cache_mem_v, mem_prompt, norm_mix_g, w_in, b_gate, mlstm_norm_g, conv_w, w_mix_out, norm_x_g, norm_mem_g, w_xq, w_xk, w_xv, w_xo, norm_ffn_g, w_router, b_router, w_up, b_up, w_down, b_down, norm_final_g):
    bp, lp, _ = x_prompt.shape
    bs, ls, _ = x_sample.shape
    l = 0
    row = lambda a: a.reshape(1, -1)

    wi = w_in[l]
    gate_cols = wi[:, 4 * D_MLSTM:4 * D_MLSTM + 2 * N_HEADS]
    wts = dict(
        g_mix=row(norm_mix_g[l]),
        wq=wi[:, 0:4 * D_MLSTM].astype(BF16),
        wg=jnp.pad(gate_cols, ((0, 0), (0, 128 - 2 * N_HEADS))).astype(BF16),
        wgt=gate_cols.T.astype(BF16),
        wc=wi[:, 4 * D_MLSTM + 2 * N_HEADS:].astype(BF16),
        bg=row(b_gate[l]), bgt=b_gate[l].reshape(-1, 1),
        cw=conv_w[l], ng=row(mlstm_norm_g[l]),
        wmo=w_mix_out[l].astype(BF16), g_x=row(norm_x_g[l]), wxq=w_xq[l].astype(BF16),
        wxo=w_xo[l].astype(BF16), g_ffn=row(norm_ffn_g[l]),
        wrt=w_router[l].T.astype(BF16), br=b_router[l].reshape(-1, 1),
    )

    mk, mv, mkb, mvb = _memkv(mem_prompt.reshape(bp * N_MEM, D_MODEL), row(norm_mem_g[l]),
                              w_xk[l].astype(BF16), w_xv[l].astype(BF16))
    zeros = lambda *s: jnp.zeros(s, F32)
    tm_post, tmd = 512, 256
    assert _sorted_rows(tmd) // BIG_ROWS <= MAX_BIG
    pr = _path(x_prompt, zeros(bp, CONV_WIDTH - 1, D_CONV), zeros(bp, N_HEADS, HEAD_DIM, HEAD_DIM),
               zeros(bp, N_HEADS, HEAD_DIM), zeros(bp, N_HEADS, 1),
               mkb.reshape(bp, N_MEM, D_MODEL), mvb.reshape(bp, N_MEM, D_MODEL), wts,
               tm_in=512, ct=512, chunk=CHUNK, tm_post=tm_post, sub=tmd, group=1)
    sa = _path(x_sample, state_conv[l], state_mlstm_c[l], state_mlstm_n[l],
               state_mlstm_m[l][..., None],
               cache_mem_k[l].reshape(bs, N_MEM, D_MODEL).astype(BF16),
               cache_mem_v[l].reshape(bs, N_MEM, D_MODEL).astype(BF16), wts,
               tm_in=ls, ct=ls, chunk=min(CHUNK, ls), tm_post=ls, sub=ls, group=bs)

    n_ptok, n_stok = bp * lp, bs * ls
    cnt = jnp.concatenate([pr["cnt"], sa["cnt"]], axis=0)
    n_tiles = cnt.shape[0]
    n_blocks = -(-(TOP_K * (n_ptok + n_stok) + n_tiles * N_EXPERTS * (SUBLANES - 1)) // ROW_BLOCK) \
        + N_EXPERTS
    seg = (cnt + SUBLANES - 1) // SUBLANES * SUBLANES
    seg_src = jnp.cumsum(seg, axis=1) - seg
    tot = jnp.sum(seg, axis=0)
    blocks_e = (tot + ROW_BLOCK - 1) // ROW_BLOCK
    padded = blocks_e * ROW_BLOCK
    pstart = jnp.cumsum(padded) - padded
    bend = jnp.cumsum(blocks_e)
    n_used = bend[-1]
    exp_tab = jnp.concatenate([bend - blocks_e, blocks_e, n_used[None]]).astype(I32)
    seg_dst = pstart[None, :] + jnp.cumsum(seg, axis=0) - seg

    tab = _copy_tables(seg, seg_src, seg_dst)
    fgran = (padded - tot) // SUBLANES
    fill = jnp.concatenate([pstart + tot, fgran, jnp.sum(fgran)[None], n_used[None]])
    fill = jnp.pad(fill, (0, TABLE_WIDTH - fill.shape[0])).astype(I32)

    def lookup(table, eid):
        hit = eid[..., None] == jnp.arange(N_EXPERTS, dtype=I32)
        return jnp.sum(jnp.where(hit, table[:, None, None, :], 0), axis=-1)

    eid_pt, rank_pt = _tiles(pr["eid"], tmd), _tiles(pr["rank"], tmd)
    eid_st = sa["eid"].transpose(1, 0, 2).reshape(1, TOP_K, n_stok)
    rank_st = sa["rank"].transpose(1, 0, 2).reshape(1, TOP_K, n_stok)
    slot_p = (lookup(seg_src[:-1], eid_pt) + rank_pt).transpose(0, 2, 1).reshape(n_ptok, TOP_K)
    slot_s = (lookup(seg_src[-1:], eid_st) + rank_st).transpose(0, 2, 1).reshape(n_stok, TOP_K)

    xb = _dispatch(tab, fill, eid_pt, rank_pt, seg_src.astype(F32)[..., None],
                   pr["xn"].reshape(n_ptok, D_MODEL), eid_st[0], rank_st[0],
                   sa["xn"].reshape(n_stok, D_MODEL), tmd=tmd, n_blocks=n_blocks)
    yb = _experts(exp_tab, xb, w_up[l], b_up[l][:, None, :], w_down[l], b_down[l][:, None, :],
                  n_blocks=n_blocks)

    gfin = row(norm_final_g)
    split = (n_tiles - 1) * TABLE_WIDTH
    y_p = _combine(tab[:split], slot_p.astype(I32), pr["gate"].transpose(0, 2, 1).reshape(n_ptok, TOP_K),
                   pr["h2"].reshape(n_ptok, D_MODEL), gfin, yb, tm=tmd)
    y_s = _combine(tab[split:], slot_s.astype(I32), sa["gate"].transpose(0, 2, 1).reshape(n_stok, TOP_K),
                   sa["h2"].reshape(n_stok, D_MODEL), gfin, yb, tm=n_stok)

    lead = lambda a: a[None]
    return (y_p.reshape(bp, lp, D_MODEL), y_s.reshape(bs, ls, D_MODEL),
            lead(pr["c1"]), lead(pr["n1"]), lead(pr["m1"]), lead(pr["nbuf"]),
            lead(mk.reshape(bp, N_MEM, N_XHEADS, XHEAD_DIM)),
            lead(mv.reshape(bp, N_MEM, N_XHEADS, XHEAD_DIM)),
            lead(sa["c1"]), lead(sa["n1"]), lead(sa["m1"]), lead(sa["nbuf"]))
```

```python
import functools

import jax
import jax.numpy as jnp
import numpy as np
from jax import lax
from jax.experimental import pallas as pl
from jax.experimental.pallas import tpu as pltpu

F32 = jnp.float32
BF16 = jnp.bfloat16
I32 = jnp.int32

D_MODEL = 1024
N_HEADS = 4
HEAD_DIM = 128
D_MLSTM = N_HEADS * HEAD_DIM
D_CONV = D_MODEL - D_MLSTM
CONV_WIDTH = 3
CHUNK = 64
N_MEM = 256
N_XHEADS = 4
XHEAD_DIM = D_MODEL // N_XHEADS
N_EXPERTS = 32
TOP_K = 4
D_FF = D_MODEL
SWIGLU_LIMIT = 7.0
SWIGLU_ALPHA = 1.702
EPS = 1e-5

SUBLANES = 8
BIG_ROWS = 4 * SUBLANES
MAX_BIG, MAX_SMALL = 64, 3 * N_EXPERTS
TAB_BIG_DST, TAB_BIG_SRC = 0, MAX_BIG
TAB_SMALL_DST, TAB_SMALL_SRC = 2 * MAX_BIG, 2 * MAX_BIG + MAX_SMALL
TAB_COUNTS = 2 * MAX_BIG + 2 * MAX_SMALL
TABLE_WIDTH = 512
ROW_BLOCK = 256
ROW_LOOKAHEAD = 3
ROW_RING = ROW_LOOKAHEAD + 1
VMEM_LIMIT = 56 * 1024 * 1024


def _cparams(sem):
    return pltpu.CompilerParams(dimension_semantics=sem, vmem_limit_bytes=VMEM_LIMIT)


def _rms(x, g):
    return x * lax.rsqrt(jnp.mean(x * x, axis=-1, keepdims=True) + EPS) * g


def _log_sigmoid(x):
    return -(jnp.maximum(-x, 0.0) + jnp.log1p(jnp.exp(-jnp.abs(x))))


def _sigmoid(x):
    return 1.0 / (1.0 + jnp.exp(-x))


def _dot(a, b):
    return jnp.dot(a, b, preferred_element_type=F32)


def _dot_nt(a, b):
    return lax.dot_general(a, b, (((1,), (1,)), ((), ())), preferred_element_type=F32)


def _dot_tn(a, b):
    return lax.dot_general(a, b, (((0,), (0,)), ((), ())), preferred_element_type=F32)


def _memkv_kernel(mem_ref, g_ref, wk_ref, wv_ref, mk_ref, mv_ref, mkb_ref, mvb_ref):
    mn = _rms(mem_ref[...], g_ref[...]).astype(BF16)
    mk = _dot(mn, wk_ref[...])
    mv = _dot(mn, wv_ref[...])
    mk_ref[...] = mk
    mv_ref[...] = mv
    mkb_ref[...] = mk.astype(BF16)
    mvb_ref[...] = mv.astype(BF16)


def _memkv(mem2d, g, wk, wv):
    rows = mem2d.shape[0]
    tm = N_MEM
    row_spec = pl.BlockSpec((tm, D_MODEL), lambda i: (i, 0))
    full = lambda shape: pl.BlockSpec(shape, lambda i: (0,) * len(shape))
    return pl.pallas_call(
        _memkv_kernel,
        grid=(rows // tm,),
        in_specs=[row_spec, full((1, D_MODEL)), full((D_MODEL, D_MODEL)), full((D_MODEL, D_MODEL))],
        out_specs=[row_spec, row_spec, row_spec, row_spec],
        out_shape=[jax.ShapeDtypeStruct((rows, D_MODEL), F32)] * 2
        + [jax.ShapeDtypeStruct((rows, D_MODEL), BF16)] * 2,
        compiler_params=_cparams(("arbitrary",)),
        name="memkv",
    )(mem2d, g, wk, wv)


def _inproj_kernel(x_ref, g_ref, wq_ref, wg_ref, wgt_ref, wc_ref, bg_ref, bgt_ref, cw_ref, cbuf_ref,
                   q_ref, k_ref, v_ref, og_ref, gcol_ref, grow_ref, yc_ref, nbuf_ref,
                   carry_ref, *, tm, chunk):
    j = pl.program_id(1)

    @pl.when(j == 0)
    def _():
        carry_ref[0:2, :] = cbuf_ref[...]

    xb = _rms(x_ref[...], g_ref[...]).astype(BF16)

    p = _dot(xb, wq_ref[...])
    q_ref[...] = p[:, 0:D_MLSTM].astype(BF16)
    k_ref[...] = (p[:, D_MLSTM:2 * D_MLSTM] * (HEAD_DIM ** -0.5)).astype(BF16)
    v_ref[...] = p[:, 2 * D_MLSTM:3 * D_MLSTM].astype(BF16)
    og_ref[...] = _sigmoid(p[:, 3 * D_MLSTM:4 * D_MLSTM])

    gc = _dot(xb, wg_ref[...])[:, 0:2 * N_HEADS] + bg_ref[...]
    col = lax.broadcasted_iota(I32, gc.shape, 1)
    gcol_ref[...] = jnp.where(col < N_HEADS, gc, _log_sigmoid(gc))
    gr = _dot_nt(wgt_ref[...], xb) + bgt_ref[...]
    row = lax.broadcasted_iota(I32, gr.shape, 0)
    gr = jnp.where(row < N_HEADS, gr, _log_sigmoid(gr))
    for c in range(tm // chunk):
        grow_ref[c] = gr[:, c * chunk:(c + 1) * chunk]

    pc = _dot(xb, wc_ref[...])
    cb = pc[:, 0:D_CONV]
    u = pc[:, D_CONV:2 * D_CONV] * pc[:, 2 * D_CONV:3 * D_CONV]
    c0 = carry_ref[0:1, :]
    c1 = carry_ref[1:2, :]
    rid = lax.broadcasted_iota(I32, u.shape, 0)
    um1 = jnp.where(rid == 0, c1, pltpu.roll(u, 1, 0))
    um2 = jnp.where(rid == 0, c0, jnp.where(rid == 1, c1, pltpu.roll(u, 2, 0)))
    yc = cw_ref[0:1, :] * um2 + cw_ref[1:2, :] * um1 + cw_ref[2:3, :] * u
    yc_ref[...] = (cb * yc).astype(BF16)
    tail = u[tm - 2:tm, :]
    carry_ref[0:2, :] = tail
    nbuf_ref[...] = tail


def _inproj(x, g, wq, wg, wgt, wc, bg, bgt, cw, cbuf, *, tm, chunk):
    bsz, seq, _ = x.shape
    grid = (bsz, seq // tm)
    tok = lambda c: pl.BlockSpec((None, tm, c), lambda b, j: (b, j, 0))
    full = lambda shape: pl.BlockSpec(shape, lambda b, j: (0,) * len(shape))
    nck = tm // chunk
    return pl.pallas_call(
        functools.partial(_inproj_kernel, tm=tm, chunk=chunk),
        grid=grid,
        in_specs=[tok(D_MODEL), full((1, D_MODEL)), full((D_MODEL, 4 * D_MLSTM)),
                  full((D_MODEL, 128)), full((8, D_MODEL)), full((D_MODEL, 3 * D_CONV)),
                  full((1, 8)), full((8, 1)), full((CONV_WIDTH, D_CONV)),
                  pl.BlockSpec((None, 2, D_CONV), lambda b, j: (b, 0, 0))],
        out_specs=[tok(D_MLSTM), tok(D_MLSTM), tok(D_MLSTM), tok(D_MLSTM), tok(8),
                   pl.BlockSpec((None, nck, 8, chunk), lambda b, j: (b, j, 0, 0)),
                   tok(D_CONV),
                   pl.BlockSpec((None, 2, D_CONV), lambda b, j: (b, 0, 0))],
        out_shape=[jax.ShapeDtypeStruct((bsz, seq, D_MLSTM), BF16)] * 3
        + [jax.ShapeDtypeStruct((bsz, seq, D_MLSTM), F32),
           jax.ShapeDtypeStruct((bsz, seq, 8), F32),
           jax.ShapeDtypeStruct((bsz, seq // chunk, 8, chunk), F32),
           jax.ShapeDtypeStruct((bsz, seq, D_CONV), BF16),
           jax.ShapeDtypeStruct((bsz, 2, D_CONV), F32)],
        scratch_shapes=[pltpu.VMEM((8, D_CONV), F32)],
        compiler_params=_cparams(("arbitrary", "arbitrary")),
        name="inproj",
    )(x, g, wq, wg, wgt, wc, bg, bgt, cw, cbuf)


def _mlstm_kernel(q_ref, k_ref, v_ref, og_ref, gc_ref, gr_ref, c0_ref, n0_ref, m0_ref, ng_ref,
                  ym_ref, c1_ref, n1_ref, m1_ref, c_s, n_s, m_s, *, chunk, nchunks, bsz):
    j = pl.program_id(0)

    @pl.when(j == 0)
    def _():
        c_s[...] = c0_ref[...]
        n_s[...] = n0_ref[...]
        m_s[...] = m0_ref[...]

    ti = lax.broadcasted_iota(I32, (chunk, chunk), 0)
    ji = lax.broadcasted_iota(I32, (chunk, chunk), 1)
    causal = ji <= ti

    def body(ci, carry):
        r0 = pl.multiple_of(ci * chunk, chunk)
        rows = pl.ds(r0, chunk)
        chains = [(b, h) for b in range(bsz) for h in range(N_HEADS)]
        cols = lambda h: slice(h * HEAD_DIM, (h + 1) * HEAD_DIM)
        each = lambda f: [f(n, b, h) for n, (b, h) in enumerate(chains)]
        q = lambda b, h: q_ref[b, rows, cols(h)]
        k = lambda b, h: k_ref[b, rows, cols(h)]
        v = lambda b, h: v_ref[b, rows, cols(h)]
        gcs = [gc_ref[b, rows, :] for b in range(bsz)]
        grs = [gr_ref[b, ci] for b in range(bsz)]
        li_c = each(lambda n, b, h: gcs[b][:, h:h + 1])
        lf_c = each(lambda n, b, h: gcs[b][:, N_HEADS + h:N_HEADS + h + 1])
        li_r = each(lambda n, b, h: grs[b][h:h + 1, :])
        lf_r = each(lambda n, b, h: grs[b][N_HEADS + h:N_HEADS + h + 1, :])
        m_prev = each(lambda n, b, h: m_s[b, h:h + 1, :])

        b_c = each(lambda n, b, h: jnp.sum(jnp.where(causal, lf_r[n], 0.0), axis=1, keepdims=True))
        b_r = each(lambda n, b, h: jnp.sum(jnp.where(ti <= ji, lf_c[n], 0.0), axis=0, keepdims=True))
        dmat = each(lambda n, b, h: jnp.where(causal, b_c[n] - b_r[n] + li_r[n], -jnp.inf))
        dmax = each(lambda n, b, h: jnp.max(dmat[n], axis=1, keepdims=True))
        inter = each(lambda n, b, h: b_c[n] + m_prev[n])
        m_t = each(lambda n, b, h: jnp.maximum(inter[n], dmax[n]))
        w_inter = each(lambda n, b, h: jnp.exp(inter[n] - m_t[n]))
        s = each(lambda n, b, h: _dot_nt(q(b, h), k(b, h)) * jnp.exp(dmat[n] - m_t[n]))
        qc = each(lambda n, b, h: _dot(q(b, h), c_s[b, h].astype(BF16)))
        sv = each(lambda n, b, h: _dot(s[n].astype(BF16), v(b, h)))
        qn = each(lambda n, b, h: jnp.sum(q(b, h).astype(F32) * n_s[b, h:h + 1, :], axis=1,
                                          keepdims=True))
        den = each(lambda n, b, h: w_inter[n] * qn[n] + jnp.sum(s[n], axis=1, keepdims=True))
        hh = each(lambda n, b, h: (w_inter[n] * qc[n] + sv[n])
                  / jnp.maximum(jnp.abs(den[n]), jnp.exp(-m_t[n])))

        m_new = each(lambda n, b, h: m_t[n][chunk - 1:chunk, :])
        b_last = each(lambda n, b, h: b_c[n][chunk - 1:chunk, :])
        decay = each(lambda n, b, h: jnp.exp(b_last[n] + m_prev[n] - m_new[n]))
        kw = each(lambda n, b, h: k(b, h).astype(F32)
                  * jnp.exp(b_last[n] - b_c[n] + li_c[n] - m_new[n]))
        kv = each(lambda n, b, h: _dot_tn(kw[n].astype(BF16), v(b, h)))
        for n, (b, h) in enumerate(chains):
            c_s[b, h] = decay[n] * c_s[b, h] + kv[n]
            n_s[b, h:h + 1, :] = decay[n] * n_s[b, h:h + 1, :] + jnp.sum(kw[n], axis=0, keepdims=True)
            m_s[b, h:h + 1, :] = m_new[n]

        hn = each(lambda n, b, h: hh[n] * lax.rsqrt(jnp.mean(hh[n] * hh[n], axis=1, keepdims=True) + EPS)
                  * ng_ref[:, cols(h)])
        for n, (b, h) in enumerate(chains):
            ym_ref[b, rows, cols(h)] = (hn[n] * og_ref[b, rows, cols(h)]).astype(BF16)
        return carry

    lax.fori_loop(0, nchunks, body, 0)

    @pl.when(j == pl.num_programs(0) - 1)
    def _():
        c1_ref[...] = c_s[...]
        n1_ref[...] = n_s[...]
        m1_ref[...] = m_s[...]


def _mlstm(q, k, v, og, gcol, grow, c0, n0, m0, ng, *, ct, chunk):
    bsz, seq, _ = q.shape
    nchunks = ct // chunk
    grid = (seq // ct,)
    tok = lambda c: pl.BlockSpec((bsz, ct, c), lambda j: (0, j, 0))
    st_c = pl.BlockSpec((bsz, N_HEADS, HEAD_DIM, HEAD_DIM), lambda j: (0, 0, 0, 0))
    st_n = pl.BlockSpec((bsz, N_HEADS, HEAD_DIM), lambda j: (0, 0, 0))
    st_m = pl.BlockSpec((bsz, N_HEADS, 1), lambda j: (0, 0, 0))
    return pl.pallas_call(
        functools.partial(_mlstm_kernel, chunk=chunk, nchunks=nchunks, bsz=bsz),
        grid=grid,
        in_specs=[tok(D_MLSTM), tok(D_MLSTM), tok(D_MLSTM), tok(D_MLSTM), tok(8),
                  pl.BlockSpec((bsz, nchunks, 8, chunk), lambda j: (0, j, 0, 0)),
                  st_c, st_n, st_m,
                  pl.BlockSpec((1, D_MLSTM), lambda j: (0, 0))],
        out_specs=[tok(D_MLSTM), st_c, st_n, st_m],
        out_shape=[jax.ShapeDtypeStruct((bsz, seq, D_MLSTM), BF16),
                   jax.ShapeDtypeStruct((bsz, N_HEADS, HEAD_DIM, HEAD_DIM), F32),
                   jax.ShapeDtypeStruct((bsz, N_HEADS, HEAD_DIM), F32),
                   jax.ShapeDtypeStruct((bsz, N_HEADS, 1), F32)],
        scratch_shapes=[pltpu.VMEM((bsz, N_HEADS, HEAD_DIM, HEAD_DIM), F32),
                        pltpu.VMEM((bsz, N_HEADS, HEAD_DIM), F32),
                        pltpu.VMEM((bsz, N_HEADS, 1), F32)],
        compiler_params=_cparams(("arbitrary",)),
        name="mlstm",
    )(q, k, v, og, gcol, grow, c0, n0, m0, ng)


def _post_kernel(ym_ref, yc_ref, x_ref, wmo_ref, gx_ref, wxq_ref, mk_ref, mv_ref, wxo_ref,
                 gf_ref, wrt_ref, br_ref,
                 h2_ref, xn_ref, eid_ref, gate_ref, rank_ref, cnt_ref, cnt_s, *, tm, sub, group):
    step = pl.program_id(0) * pl.num_programs(1) + pl.program_id(1)

    @pl.when(step % group == 0)
    def _():
        cnt_s[...] = jnp.zeros_like(cnt_s)

    mix = _dot(ym_ref[...], wmo_ref[0:D_MLSTM, :]) + _dot(yc_ref[...], wmo_ref[D_MLSTM:D_MODEL, :])
    h1 = x_ref[...] + mix

    xq = _dot(_rms(h1, gx_ref[...]).astype(BF16), wxq_ref[...]).astype(BF16)

    def attend(q_rows, mem):
        att = jnp.zeros((q_rows.shape[0], D_MODEL), F32)
        for hd in range(N_XHEADS):
            cols = slice(hd * XHEAD_DIM, (hd + 1) * XHEAD_DIM)
            s = _dot_nt(q_rows[:, cols], mk_ref[mem, :, cols]) * (XHEAD_DIM ** -0.5)
            e = jnp.exp(s - jnp.max(s, axis=-1, keepdims=True))
            p = (e / jnp.sum(e, axis=-1, keepdims=True)).astype(BF16)
            o = _dot(p, mv_ref[mem, :, cols]).astype(BF16)
            att = att + _dot(o, wxo_ref[cols, :])
        return att

    n_mem = mk_ref.shape[0]
    per = tm // n_mem
    att = jnp.concatenate([attend(xq[m * per:(m + 1) * per, :], m) for m in range(n_mem)], axis=0)
    h2 = h1 + att
    h2_ref[...] = h2

    xn2 = _rms(h2, gf_ref[...]).astype(BF16)
    xn_ref[...] = xn2

    logits = _dot_nt(wrt_ref[...], xn2) + br_ref[...]
    eidx = lax.broadcasted_iota(I32, logits.shape, 0).astype(F32)
    work = logits
    vals, ids, hots = [], [], []
    for _ in range(TOP_K):
        mx = jnp.max(work, axis=0, keepdims=True)
        idx = jnp.min(jnp.where(work == mx, eidx, float(N_EXPERTS)), axis=0, keepdims=True)
        sel = eidx == idx
        vals.append(mx)
        ids.append(idx)
        hots.append(sel)
        work = jnp.where(sel, -jnp.inf, work)
    exps = [jnp.exp(v - vals[0]) for v in vals]
    denom = exps[0] + exps[1] + exps[2] + exps[3]

    picked = jnp.zeros(logits.shape, F32)
    for sel in hots:
        picked = picked + sel.astype(F32)
    shift = jnp.full((tm, tm), sub.bit_length() - 1, I32)
    tj = lax.broadcasted_iota(I32, (tm, tm), 0)
    tt = lax.broadcasted_iota(I32, (tm, tm), 1)
    same = lax.shift_right_logical(tj, shift) == lax.shift_right_logical(tt, shift)
    before = jnp.where(jnp.logical_and(tj < tt, same), 1.0, 0.0).astype(BF16)
    prior = _dot(picked.astype(BF16), before) + cnt_s[...]
    for kk in range(TOP_K):
        eid_ref[kk:kk + 1, :] = ids[kk].astype(I32)
        gate_ref[kk:kk + 1, :] = exps[kk] / denom
        rank_ref[kk:kk + 1, :] = jnp.sum(jnp.where(hots[kk], prior, 0.0), axis=0,
                                         keepdims=True).astype(I32)
    for s in range(tm // sub):
        total = cnt_s[...] + jnp.sum(picked[:, s * sub:(s + 1) * sub], axis=1, keepdims=True)
        cnt_ref[s] = total.astype(I32)
    cnt_s[...] = total


def _post(ym, yc, x, wmo, gx, wxq, mkb, mvb, wxo, gf, wrt, br, *, tm, sub, group):
    bsz, seq, _ = x.shape
    nj = seq // tm
    grid = (bsz, nj)
    nsub = tm // sub
    n_tiles = bsz * nj * nsub // group
    tok = lambda c: pl.BlockSpec((None, tm, c), lambda b, j: (b, j, 0))
    full = lambda shape: pl.BlockSpec(shape, lambda b, j: (0,) * len(shape))
    n_mem = mkb.shape[0] // bsz
    mem = pl.BlockSpec((n_mem, N_MEM, D_MODEL), lambda b, j: (b, 0, 0))
    sel = pl.BlockSpec((None, TOP_K, tm), lambda b, j: (b, 0, j))
    return pl.pallas_call(
        functools.partial(_post_kernel, tm=tm, sub=sub, group=group),
        grid=grid,
        in_specs=[tok(D_MLSTM), tok(D_CONV), tok(D_MODEL), full((D_MODEL, D_MODEL)),
                  full((1, D_MODEL)), full((D_MODEL, D_MODEL)), mem, mem,
                  full((D_MODEL, D_MODEL)), full((1, D_MODEL)), full((N_EXPERTS, D_MODEL)),
                  full((N_EXPERTS, 1))],
        out_specs=[tok(D_MODEL), tok(D_MODEL), sel, sel, sel,
                   pl.BlockSpec((nsub, N_EXPERTS, 1), lambda b, j: ((b * nj + j) // group, 0, 0))],
        out_shape=[jax.ShapeDtypeStruct((bsz, seq, D_MODEL), F32),
                   jax.ShapeDtypeStruct((bsz, seq, D_MODEL), BF16),
                   jax.ShapeDtypeStruct((bsz, TOP_K, seq), I32),
                   jax.ShapeDtypeStruct((bsz, TOP_K, seq), F32),
                   jax.ShapeDtypeStruct((bsz, TOP_K, seq), I32),
                   jax.ShapeDtypeStruct((n_tiles, N_EXPERTS, 1), I32)],
        scratch_shapes=[pltpu.VMEM((N_EXPERTS, 1), F32)],
        compiler_params=_cparams(("arbitrary", "arbitrary")),
        name="post",
    )(ym, yc, x, wmo, gx, wxq, mkb, mvb, wxo, gf, wrt, br)


def _sorted_rows(n_tokens):
    return -(-(TOP_K * n_tokens + N_EXPERTS * (SUBLANES - 1)) // ROW_BLOCK) * ROW_BLOCK


def _dispatch_kernel(tab_ref, fill_ref, eid_ref, rank_ref, ls_ref, x_ref, eids_ref, ranks_ref, xs_ref,
                     xb_ref, srt, zero_s, pending, sems, *, n_ptiles, n_blocks):
    i = pl.program_id(0)
    cur = i % 2

    def rows_copy(src, src_row, dst_row, nrows, sl):
        return pltpu.make_async_copy(
            src.at[pl.ds(pl.multiple_of(src_row, SUBLANES), nrows), :],
            xb_ref.at[pl.ds(pl.multiple_of(dst_row, SUBLANES), nrows), :], sems.at[sl])

    def granule(src, src_row, dst_row, sl):
        return rows_copy(src, src_row, dst_row, SUBLANES, sl)

    def drain(count, sl, nrows=SUBLANES):
        def body(g, c):
            rows_copy(zero_s, 0, 0, nrows, sl).wait()
            return c
        lax.fori_loop(0, count, body, 0)

    @pl.when(i == 0)
    def _():
        pending[0] = 0
        pending[1] = 0

    def issue_list(count, dst_at, src_at, nrows):
        def body(h, c):
            g = 2 * h
            rows_copy(srt.at[cur], tab_ref[src_at + g], tab_ref[dst_at + g], nrows, cur).start(priority=0)

            @pl.when(g + 1 < count)
            def _():
                rows_copy(srt.at[cur], tab_ref[src_at + g + 1], tab_ref[dst_at + g + 1], nrows,
                          cur).start(priority=1)
            return c
        lax.fori_loop(0, (count + 1) // 2, body, 0)

    def sort_and_move(eid, rank, x):
        ntok = x.shape[0]
        nrows = _sorted_rows(ntok)
        e_iota = lax.broadcasted_iota(I32, (N_EXPERTS, ntok), 0)
        s_iota = lax.broadcasted_iota(I32, (nrows, ntok), 0)
        seg_start = ls_ref[...]
        hit = None
        for kk in range(TOP_K):
            start = jnp.sum(jnp.where(e_iota == eid[kk:kk + 1, :], seg_start, 0.0),
                            axis=0, keepdims=True).astype(I32)
            match = s_iota == start + rank[kk:kk + 1, :]
            hit = match if hit is None else jnp.logical_or(hit, match)
        perm = jnp.where(hit, 1.0, 0.0).astype(BF16)
        srt[cur, 0:nrows, :] = _dot(perm, x)
        drain(pending[0], 1 - cur, BIG_ROWS)
        drain(pending[1], 1 - cur)
        n_big, n_small = tab_ref[TAB_COUNTS], tab_ref[TAB_COUNTS + 1]
        issue_list(n_big, TAB_BIG_DST, TAB_BIG_SRC, BIG_ROWS)
        issue_list(n_small, TAB_SMALL_DST, TAB_SMALL_SRC, SUBLANES)
        pending[0] = n_big
        pending[1] = n_small

    @pl.when(i < n_ptiles)
    def _():
        sort_and_move(eid_ref[...], rank_ref[...], x_ref[...])

    @pl.when(i == n_ptiles)
    def _():
        sort_and_move(eids_ref[...], ranks_ref[...], xs_ref[...])
        drain(pending[0], cur, BIG_ROWS)
        drain(pending[1], cur)
        zero_s[...] = jnp.zeros_like(zero_s)
        for e in range(N_EXPERTS):
            dst = fill_ref[e]

            def zissue(g, c):
                granule(zero_s, 0, dst + g * SUBLANES, cur).start()
                return c
            lax.fori_loop(0, fill_ref[N_EXPERTS + e], zissue, 0)
        drain(fill_ref[2 * N_EXPERTS], cur)
        first_free = fill_ref[2 * N_EXPERTS + 1]

        def blk_copy(b):
            return pltpu.make_async_copy(
                zero_s, xb_ref.at[pl.ds(pl.multiple_of(b * ROW_BLOCK, ROW_BLOCK), ROW_BLOCK), :],
                sems.at[cur])

        def bissue(b, c):
            blk_copy(b).start()
            return c
        lax.fori_loop(first_free, n_blocks, bissue, 0)

        def bdrain(b, c):
            blk_copy(0).wait()
            return c
        lax.fori_loop(first_free, n_blocks, bdrain, 0)


def _dispatch(tab, fill, eid_p, rank_p, seg_start, xn_p, eid_s, rank_s, xn_s, *, tmd, n_blocks):
    n_ptiles = eid_p.shape[0]
    n_sample = xn_s.shape[0]
    last = n_ptiles - 1
    smem = lambda shape, imap: pl.BlockSpec(shape, imap, memory_space=pltpu.SMEM)
    return pl.pallas_call(
        functools.partial(_dispatch_kernel, n_ptiles=n_ptiles, n_blocks=n_blocks),
        grid=(n_ptiles + 1,),
        in_specs=[smem((TABLE_WIDTH,), lambda i: (i,)),
                  smem((TABLE_WIDTH,), lambda i: (0,)),
                  pl.BlockSpec((None, TOP_K, tmd), lambda i: (jnp.minimum(i, last), 0, 0)),
                  pl.BlockSpec((None, TOP_K, tmd), lambda i: (jnp.minimum(i, last), 0, 0)),
                  pl.BlockSpec((None, N_EXPERTS, 1), lambda i: (i, 0, 0)),
                  pl.BlockSpec((tmd, D_MODEL), lambda i: (jnp.minimum(i, last), 0)),
                  pl.BlockSpec((TOP_K, n_sample), lambda i: (0, 0)),
                  pl.BlockSpec((TOP_K, n_sample), lambda i: (0, 0)),
                  pl.BlockSpec((n_sample, D_MODEL), lambda i: (0, 0))],
        out_specs=pl.BlockSpec(memory_space=pl.ANY),
        out_shape=jax.ShapeDtypeStruct((n_blocks * ROW_BLOCK, D_MODEL), F32),
        scratch_shapes=[pltpu.VMEM((2, _sorted_rows(tmd), D_MODEL), F32),
                        pltpu.VMEM((ROW_BLOCK, D_MODEL), F32), pltpu.SMEM((2,), I32),
                        pltpu.SemaphoreType.DMA((2,))],
        compiler_params=_cparams(("arbitrary",)),
        name="dispatch",
    )(tab, fill, eid_p, rank_p, seg_start, xn_p, eid_s, rank_s, xn_s)


def _expert_kernel(tab_ref, xb_ref, wu_hbm, bu_ref, wd_hbm, bd_ref, yb_ref,
                   wu_f, wd_f, wu_s, wd_s, xbuf, ybuf, wsem, xsem, ysem, *, n_blocks):
    e = pl.program_id(0)
    first = tab_ref[e]
    nblk = tab_ref[N_EXPERTS + e]
    n_used = tab_ref[2 * N_EXPERTS]

    def hbm_rows(b):
        return pl.ds(pl.multiple_of(b * ROW_BLOCK, ROW_BLOCK), ROW_BLOCK)

    def x_copy(b, sl):
        return pltpu.make_async_copy(xb_ref.at[hbm_rows(b), :], xbuf.at[sl], xsem.at[sl])

    def y_copy(b, sl):
        return pltpu.make_async_copy(ybuf.at[sl], yb_ref.at[hbm_rows(b), :], ysem.at[sl])

    def w_copies(ex, sl):
        return (pltpu.make_async_copy(wu_hbm.at[ex], wu_f.at[sl], wsem.at[0, sl]),
                pltpu.make_async_copy(wd_hbm.at[ex], wd_f.at[sl], wsem.at[1, sl]))

    wslot = e % 2

    @pl.when(e == 0)
    def _():
        for c in w_copies(0, 0):
            c.start()
        for b in range(ROW_LOOKAHEAD):
            @pl.when(b < n_used)
            def _():
                x_copy(b, b % ROW_RING).start()

    @pl.when(e + 1 < N_EXPERTS)
    def _():
        for c in w_copies(e + 1, 1 - wslot):
            c.start()

    for c in w_copies(e, wslot):
        c.wait()

    step = 128

    def cast(r, c):
        rows = pl.ds(pl.multiple_of(r * step, step), step)
        wu_s[rows, :] = wu_f[wslot, rows, :].astype(BF16)
        wd_s[rows, :] = wd_f[wslot, rows, :].astype(BF16)
        return c
    lax.fori_loop(0, D_MODEL // step, cast, 0)

    def block(b, c):
        sl = b % ROW_RING
        x_copy(b, sl).wait()

        @pl.when(b + ROW_LOOKAHEAD < n_used)
        def _():
            x_copy(b + ROW_LOOKAHEAD, (b + ROW_LOOKAHEAD) % ROW_RING).start()

        @pl.when(b >= ROW_RING)
        def _():
            y_copy(b, sl).wait()

        hcat = _dot(xbuf[sl].astype(BF16), wu_s[...]) + bu_ref[...]
        glu = jnp.minimum(hcat[:, 0:D_FF], SWIGLU_LIMIT)
        lin = jnp.clip(hcat[:, D_FF:2 * D_FF], -SWIGLU_LIMIT, SWIGLU_LIMIT)
        act = glu * _sigmoid(SWIGLU_ALPHA * glu) * (lin + 1.0)
        ybuf[sl] = _dot(act.astype(BF16), wd_s[...]) + bd_ref[...]
        y_copy(b, sl).start()
        return c
    lax.fori_loop(first, first + nblk, block, 0)

    @pl.when(e == N_EXPERTS - 1)
    def _():
        for k in range(ROW_RING):
            @pl.when(k < n_used)
            def _():
                y_copy(0, (n_used - 1 - k) % ROW_RING).wait()

        ybuf[0] = jnp.zeros((ROW_BLOCK, D_MODEL), F32)

        def zissue(b, c):
            y_copy(b, 0).start()
            return c
        lax.fori_loop(n_used, n_blocks, zissue, 0)

        def zdrain(b, c):
            y_copy(0, 0).wait()
            return c
        lax.fori_loop(n_used, n_blocks, zdrain, 0)


def _experts(tab, xb, w_up, b_up, w_down, b_down, *, n_blocks):
    w_map = lambda e, tab: (e, 0, 0)
    grid_spec = pltpu.PrefetchScalarGridSpec(
        num_scalar_prefetch=1,
        grid=(N_EXPERTS,),
        in_specs=[pl.BlockSpec(memory_space=pl.ANY),
                  pl.BlockSpec(memory_space=pl.ANY),
                  pl.BlockSpec((None, 1, 2 * D_FF), w_map),
                  pl.BlockSpec(memory_space=pl.ANY),
                  pl.BlockSpec((None, 1, D_MODEL), w_map)],
        out_specs=pl.BlockSpec(memory_space=pl.ANY),
        scratch_shapes=[pltpu.VMEM((2, D_MODEL, 2 * D_FF), F32), pltpu.VMEM((2, D_FF, D_MODEL), F32),
                        pltpu.VMEM((D_MODEL, 2 * D_FF), BF16), pltpu.VMEM((D_FF, D_MODEL), BF16),
                        pltpu.VMEM((ROW_RING, ROW_BLOCK, D_MODEL), F32),
                        pltpu.VMEM((ROW_RING, ROW_BLOCK, D_MODEL), F32),
                        pltpu.SemaphoreType.DMA((2, 2)),
                        pltpu.SemaphoreType.DMA((ROW_RING,)), pltpu.SemaphoreType.DMA((ROW_RING,))],
    )
    return pl.pallas_call(
        functools.partial(_expert_kernel, n_blocks=n_blocks),
        grid_spec=grid_spec,
        out_shape=jax.ShapeDtypeStruct((n_blocks * ROW_BLOCK, D_MODEL), F32),
        compiler_params=_cparams(("arbitrary",)),
        name="experts",
    )(tab, xb, w_up, b_up, w_down, b_down)


def _split_bf16(a):
    hi = a.astype(BF16)
    return hi, (a - hi.astype(F32)).astype(BF16)


def _combine_kernel(tab_ref, nxt_ref, slot_ref, gate_ref, h2_ref, gfin_ref, yb_ref, y_ref, buf, sems,
                    *, tm):
    i = pl.program_id(0)
    n = pl.num_programs(0)
    cur = i % 2
    nrows = buf.shape[1]

    def rows_copy(src_row, sl, dst_row, nrows):
        return pltpu.make_async_copy(
            yb_ref.at[pl.ds(pl.multiple_of(src_row, SUBLANES), nrows), :],
            buf.at[sl, pl.ds(pl.multiple_of(dst_row, SUBLANES), nrows), :], sems.at[sl])

    def gather(t_ref, sl):
        def fetch(count, yb_at, tile_at, nrows):
            def body(h, c):
                g = 2 * h
                rows_copy(t_ref[yb_at + g], sl, t_ref[tile_at + g], nrows).start(priority=0)

                @pl.when(g + 1 < count)
                def _():
                    rows_copy(t_ref[yb_at + g + 1], sl, t_ref[tile_at + g + 1], nrows).start(priority=1)
                return c
            lax.fori_loop(0, (count + 1) // 2, body, 0)
        fetch(t_ref[TAB_COUNTS], TAB_BIG_DST, TAB_BIG_SRC, BIG_ROWS)
        fetch(t_ref[TAB_COUNTS + 1], TAB_SMALL_DST, TAB_SMALL_SRC, SUBLANES)

    @pl.when(i == 0)
    def _():
        buf[...] = jnp.zeros_like(buf)
        gather(tab_ref, 0)

    @pl.when(i + 1 < n)
    def _():
        gather(nxt_ref, 1 - cur)

    def drain(count, nrows):
        def body(g, c):
            rows_copy(0, cur, 0, nrows).wait()
            return c
        lax.fori_loop(0, count, body, 0)
    drain(tab_ref[TAB_COUNTS], BIG_ROWS)
    drain(tab_ref[TAB_COUNTS + 1], SUBLANES)

    s_iota = lax.broadcasted_iota(I32, (tm, nrows), 1)
    wgt = jnp.zeros((tm, nrows), F32)
    for kk in range(TOP_K):
        wgt = wgt + jnp.where(s_iota == slot_ref[:, kk:kk + 1], gate_ref[:, kk:kk + 1], 0.0)
    w_hi, w_lo = _split_bf16(wgt)
    rows = buf[cur].astype(BF16)
    acc = _dot(w_hi, rows) + _dot(w_lo, rows)
    y_ref[...] = _rms(h2_ref[...] + acc, gfin_ref[...])


def _combine(tab, slot_col, gate_col, h2, gfin, yb, *, tm):
    ntok = h2.shape[0]
    n = ntok // tm
    smem = lambda imap: pl.BlockSpec((TABLE_WIDTH,), imap, memory_space=pltpu.SMEM)
    return pl.pallas_call(
        functools.partial(_combine_kernel, tm=tm),
        grid=(n,),
        in_specs=[smem(lambda i: (i,)), smem(lambda i: (jnp.minimum(i + 1, n - 1),)),
                  pl.BlockSpec((tm, TOP_K), lambda i: (i, 0)),
                  pl.BlockSpec((tm, TOP_K), lambda i: (i, 0)),
                  pl.BlockSpec((tm, D_MODEL), lambda i: (i, 0)),
                  pl.BlockSpec((1, D_MODEL), lambda i: (0, 0)),
                  pl.BlockSpec(memory_space=pl.ANY)],
        out_specs=pl.BlockSpec((tm, D_MODEL), lambda i: (i, 0)),
        out_shape=jax.ShapeDtypeStruct((ntok, D_MODEL), F32),
        scratch_shapes=[pltpu.VMEM((2, _sorted_rows(tm), D_MODEL), F32),
                        pltpu.SemaphoreType.DMA((2,))],
        compiler_params=_cparams(("arbitrary",)),
        name="combine",
    )(tab, tab, slot_col, gate_col, h2, gfin, yb)


def _copy_tables(seg, seg_src, seg_dst):
    experts = jnp.arange(N_EXPERTS, dtype=I32)

    def flatten(counts, bound, src0, dst0, step):
        ends = jnp.cumsum(counts, axis=1)
        idx = jnp.arange(bound, dtype=I32)
        owner = jnp.sum(ends[:, None, :] <= idx[None, :, None], axis=-1)
        pick = lambda a: jnp.sum(jnp.where(owner[..., None] == experts, a[:, None, :], 0), axis=-1)
        off = (idx[None, :] - pick(ends - counts)) * step
        valid = idx[None, :] < ends[:, -1:]
        return (jnp.where(valid, pick(dst0) + off, 0), jnp.where(valid, pick(src0) + off, 0),
                ends[:, -1:])

    n_big = seg // BIG_ROWS
    n_small = (seg - n_big * BIG_ROWS) // SUBLANES
    b_dst, b_src, b_n = flatten(n_big, MAX_BIG, seg_src, seg_dst, BIG_ROWS)
    s_dst, s_src, s_n = flatten(n_small, MAX_SMALL, seg_src + n_big * BIG_ROWS,
                                seg_dst + n_big * BIG_ROWS, SUBLANES)
    tab = jnp.concatenate([b_dst, b_src, s_dst, s_src, b_n, s_n], axis=1)
    return jnp.pad(tab, ((0, 0), (0, TABLE_WIDTH - tab.shape[1]))).astype(I32).reshape(-1)


def _tiles(a, tile):
    bsz, kk, seq = a.shape
    return a.reshape(bsz, kk, seq // tile, tile).transpose(0, 2, 1, 3).reshape(-1, kk, tile)


def _path(x, cbuf, c0, n0, m0, mkb, mvb, wts, *, tm_in, ct, chunk, tm_post, sub, fold):
    q, k, v, og, gcol, grow, yc, nbuf = _inproj(
        x, wts["g_mix"], wts["wq"], wts["wg"], wts["wgt"], wts["wc"], wts["bg"], wts["bgt"],
        wts["cw"], cbuf, tm=tm_in, chunk=chunk)
    ym, c1, n1, m1 = _mlstm(q, k, v, og, gcol, grow, c0, n0, m0, wts["ng"], ct=ct, chunk=chunk)
    if fold:
        ym, yc, x = (a.reshape(1, -1, a.shape[-1]) for a in (ym, yc, x))
    h2, xn, eid, gate, rank, cnt = _post(
        ym, yc, x, wts["wmo"], wts["g_x"], wts["wxq"], mkb, mvb, wts["wxo"], wts["g_ffn"],
        wts["wrt"], wts["br"], tm=tm_post, sub=sub, group=1)
    return dict(h2=h2, xn=xn, eid=eid, gate=gate, rank=rank, cnt=cnt[:, :, 0],
                c1=c1, n1=n1, m1=m1[..., 0], nbuf=nbuf)


def kernel(x_prompt, x_sample, state_mlstm_c, state_mlstm_n, state_mlstm_m, state_conv, cache_mem_k, cache_mem_v, mem_prompt, norm_mix_g, w_in, b_gate, mlstm_norm_g, conv_w, w_mix_out, norm_x_g, norm_mem_g, w_xq, w_xk, w_xv, w_xo, norm_ffn_g, w_router, b_router, w_up, b_up, w_down, b_down, norm_final_g):
    bp, lp, _ = x_prompt.shape
    bs, ls, _ = x_sample.shape
    l = 0
    row = lambda a: a.reshape(1, -1)

    wi = w_in[l]
    gate_cols = wi[:, 4 * D_MLSTM:4 * D_MLSTM + 2 * N_HEADS]
    wts = dict(
        g_mix=row(norm_mix_g[l]),
        wq=wi[:, 0:4 * D_MLSTM].astype(BF16),
        wg=jnp.pad(gate_cols, ((0, 0), (0, 128 - 2 * N_HEADS))).astype(BF16),
        wgt=gate_cols.T.astype(BF16),
        wc=wi[:, 4 * D_MLSTM + 2 * N_HEADS:].astype(BF16),
        bg=row(b_gate[l]), bgt=b_gate[l].reshape(-1, 1),
        cw=conv_w[l], ng=row(mlstm_norm_g[l]),
        wmo=w_mix_out[l].astype(BF16), g_x=row(norm_x_g[l]), wxq=w_xq[l].astype(BF16),
        wxo=w_xo[l].astype(BF16), g_ffn=row(norm_ffn_g[l]),
        wrt=w_router[l].T.astype(BF16), br=b_router[l].reshape(-1, 1),
    )

    mk, mv, mkb, mvb = _memkv(mem_prompt.reshape(bp * N_MEM, D_MODEL), row(norm_mem_g[l]),
                              w_xk[l].astype(BF16), w_xv[l].astype(BF16))
    zeros = lambda *s: jnp.zeros(s, F32)
    tm_post, tmd = 512, 256
    assert _sorted_rows(tmd) // BIG_ROWS <= MAX_BIG
    pr = _path(x_prompt, zeros(bp, CONV_WIDTH - 1, D_CONV), zeros(bp, N_HEADS, HEAD_DIM, HEAD_DIM),
               zeros(bp, N_HEADS, HEAD_DIM), zeros(bp, N_HEADS, 1),
               mkb.reshape(bp, N_MEM, D_MODEL), mvb.reshape(bp, N_MEM, D_MODEL), wts,
               tm_in=512, ct=512, chunk=CHUNK, tm_post=tm_post, sub=tmd, fold=False)
    sa = _path(x_sample, state_conv[l], state_mlstm_c[l], state_mlstm_n[l],
               state_mlstm_m[l][..., None],
               cache_mem_k[l].reshape(bs, N_MEM, D_MODEL).astype(BF16),
               cache_mem_v[l].reshape(bs, N_MEM, D_MODEL).astype(BF16), wts,
               tm_in=ls, ct=ls, chunk=min(CHUNK, ls), tm_post=bs * ls, sub=bs * ls, fold=True)

    n_ptok, n_stok = bp * lp, bs * ls
    cnt = jnp.concatenate([pr["cnt"], sa["cnt"]], axis=0)
    n_tiles = cnt.shape[0]
    n_blocks = -(-(TOP_K * (n_ptok + n_stok) + n_tiles * N_EXPERTS * (SUBLANES - 1)) // ROW_BLOCK) \
        + N_EXPERTS
    seg = (cnt + SUBLANES - 1) // SUBLANES * SUBLANES
    seg_src = jnp.cumsum(seg, axis=1) - seg
    tot = jnp.sum(seg, axis=0)
    blocks_e = (tot + ROW_BLOCK - 1) // ROW_BLOCK
    padded = blocks_e * ROW_BLOCK
    pstart = jnp.cumsum(padded) - padded
    bend = jnp.cumsum(blocks_e)
    n_used = bend[-1]
    exp_tab = jnp.concatenate([bend - blocks_e, blocks_e, n_used[None]]).astype(I32)
    seg_dst = pstart[None, :] + jnp.cumsum(seg, axis=0) - seg

    tab = _copy_tables(seg, seg_src, seg_dst)
    fgran = (padded - tot) // SUBLANES
    fill = jnp.concatenate([pstart + tot, fgran, jnp.sum(fgran)[None], n_used[None]])
    fill = jnp.pad(fill, (0, TABLE_WIDTH - fill.shape[0])).astype(I32)

    def lookup(table, eid):
        hit = eid[..., None] == jnp.arange(N_EXPERTS, dtype=I32)
        return jnp.sum(jnp.where(hit, table[:, None, None, :], 0), axis=-1)

    eid_pt, rank_pt = _tiles(pr["eid"], tmd), _tiles(pr["rank"], tmd)
    eid_st = sa["eid"].transpose(1, 0, 2).reshape(1, TOP_K, n_stok)
    rank_st = sa["rank"].transpose(1, 0, 2).reshape(1, TOP_K, n_stok)
    slot_p = (lookup(seg_src[:-1], eid_pt) + rank_pt).transpose(0, 2, 1).reshape(n_ptok, TOP_K)
    slot_s = (lookup(seg_src[-1:], eid_st) + rank_st).transpose(0, 2, 1).reshape(n_stok, TOP_K)

    xb = _dispatch(tab, fill, eid_pt, rank_pt, seg_src.astype(F32)[..., None],
                   pr["xn"].reshape(n_ptok, D_MODEL), eid_st[0], rank_st[0],
                   sa["xn"].reshape(n_stok, D_MODEL), tmd=tmd, n_blocks=n_blocks)
    yb = _experts(exp_tab, xb, w_up[l], b_up[l][:, None, :], w_down[l], b_down[l][:, None, :],
                  n_blocks=n_blocks)

    gfin = row(norm_final_g)
    split = (n_tiles - 1) * TABLE_WIDTH
    y_p = _combine(tab[:split], slot_p.astype(I32), pr["gate"].transpose(0, 2, 1).reshape(n_ptok, TOP_K),
                   pr["h2"].reshape(n_ptok, D_MODEL), gfin, yb, tm=tmd)
    y_s = _combine(tab[split:], slot_s.astype(I32), sa["gate"].transpose(0, 2, 1).reshape(n_stok, TOP_K),
                   sa["h2"].reshape(n_stok, D_MODEL), gfin, yb, tm=n_stok)

    lead = lambda a: a[None]
    return (y_p.reshape(bp, lp, D_MODEL), y_s.reshape(bs, ls, D_MODEL),
            lead(pr["c1"]), lead(pr["n1"]), lead(pr["m1"]), lead(pr["nbuf"]),
            lead(mk.reshape(bp, N_MEM, N_XHEADS, XHEAD_DIM)),
            lead(mv.reshape(bp, N_MEM, N_XHEADS, XHEAD_DIM)),
            lead(sa["c1"]), lead(sa["n1"]), lead(sa["m1"]), lead(sa["nbuf"]))
```

```python
import functools

import jax
import jax.numpy as jnp
import numpy as np
from jax import lax
from jax.experimental import pallas as pl
from jax.experimental.pallas import tpu as pltpu

F32 = jnp.float32
BF16 = jnp.bfloat16
I32 = jnp.int32

D_MODEL = 1024
N_HEADS = 4
HEAD_DIM = 128
D_MLSTM = N_HEADS * HEAD_DIM
D_CONV = D_MODEL - D_MLSTM
CONV_WIDTH = 3
CHUNK = 64
N_MEM = 256
N_XHEADS = 4
XHEAD_DIM = D_MODEL // N_XHEADS
N_EXPERTS = 32
TOP_K = 4
D_FF = D_MODEL
SWIGLU_LIMIT = 7.0
SWIGLU_ALPHA = 1.702
EPS = 1e-5

SUBLANES = 8
BIG_ROWS = 4 * SUBLANES
MAX_BIG, MAX_SMALL = 64, 3 * N_EXPERTS
TAB_BIG_DST, TAB_BIG_SRC = 0, MAX_BIG
TAB_SMALL_DST, TAB_SMALL_SRC = 2 * MAX_BIG, 2 * MAX_BIG + MAX_SMALL
TAB_COUNTS = 2 * MAX_BIG + 2 * MAX_SMALL
TABLE_WIDTH = 512
POST_GROUP_ROWS = 256
ROW_BLOCK = 256
ROW_LOOKAHEAD = 3
ROW_RING = ROW_LOOKAHEAD + 1
VMEM_LIMIT = 56 * 1024 * 1024


def _cparams(sem):
    return pltpu.CompilerParams(dimension_semantics=sem, vmem_limit_bytes=VMEM_LIMIT)


def _rms(x, g):
    return x * lax.rsqrt(jnp.mean(x * x, axis=-1, keepdims=True) + EPS) * g


def _log_sigmoid(x):
    return -(jnp.maximum(-x, 0.0) + jnp.log1p(jnp.exp(-jnp.abs(x))))


def _sigmoid(x):
    return 1.0 / (1.0 + jnp.exp(-x))


def _dot(a, b):
    return jnp.dot(a, b, preferred_element_type=F32)


def _dot_nt(a, b):
    return lax.dot_general(a, b, (((1,), (1,)), ((), ())), preferred_element_type=F32)


def _dot_tn(a, b):
    return lax.dot_general(a, b, (((0,), (0,)), ((), ())), preferred_element_type=F32)


def _memkv_kernel(mem_ref, g_ref, wk_ref, wv_ref, mk_ref, mv_ref, mkb_ref, mvb_ref):
    mn = _rms(mem_ref[...], g_ref[...]).astype(BF16)
    mk = _dot(mn, wk_ref[...])
    mv = _dot(mn, wv_ref[...])
    mk_ref[...] = mk
    mv_ref[...] = mv
    mkb_ref[...] = mk.astype(BF16)
    mvb_ref[...] = mv.astype(BF16)


def _memkv(mem2d, g, wk, wv):
    rows = mem2d.shape[0]
    tm = N_MEM
    row_spec = pl.BlockSpec((tm, D_MODEL), lambda i: (i, 0))
    full = lambda shape: pl.BlockSpec(shape, lambda i: (0,) * len(shape))
    return pl.pallas_call(
        _memkv_kernel,
        grid=(rows // tm,),
        in_specs=[row_spec, full((1, D_MODEL)), full((D_MODEL, D_MODEL)), full((D_MODEL, D_MODEL))],
        out_specs=[row_spec, row_spec, row_spec, row_spec],
        out_shape=[jax.ShapeDtypeStruct((rows, D_MODEL), F32)] * 2
        + [jax.ShapeDtypeStruct((rows, D_MODEL), BF16)] * 2,
        compiler_params=_cparams(("arbitrary",)),
        name="memkv",
    )(mem2d, g, wk, wv)


def _inproj_kernel(x_ref, g_ref, wq_ref, wg_ref, wgt_ref, wc_ref, bg_ref, bgt_ref, cw_ref, cbuf_ref,
                   q_ref, k_ref, v_ref, og_ref, gcol_ref, grow_ref, yc_ref, nbuf_ref,
                   carry_ref, *, tm, chunk):
    j = pl.program_id(1)

    @pl.when(j == 0)
    def _():
        carry_ref[0:2, :] = cbuf_ref[...]

    xb = _rms(x_ref[...], g_ref[...]).astype(BF16)

    p = _dot(xb, wq_ref[...])
    q_ref[...] = p[:, 0:D_MLSTM].astype(BF16)
    k_ref[...] = (p[:, D_MLSTM:2 * D_MLSTM] * (HEAD_DIM ** -0.5)).astype(BF16)
    v_ref[...] = p[:, 2 * D_MLSTM:3 * D_MLSTM].astype(BF16)
    og_ref[...] = _sigmoid(p[:, 3 * D_MLSTM:4 * D_MLSTM])

    gc = _dot(xb, wg_ref[...])[:, 0:2 * N_HEADS] + bg_ref[...]
    col = lax.broadcasted_iota(I32, gc.shape, 1)
    gcol_ref[...] = jnp.where(col < N_HEADS, gc, _log_sigmoid(gc))
    gr = _dot_nt(wgt_ref[...], xb) + bgt_ref[...]
    row = lax.broadcasted_iota(I32, gr.shape, 0)
    gr = jnp.where(row < N_HEADS, gr, _log_sigmoid(gr))
    for c in range(tm // chunk):
        grow_ref[c] = gr[:, c * chunk:(c + 1) * chunk]

    pc = _dot(xb, wc_ref[...])
    cb = pc[:, 0:D_CONV]
    u = pc[:, D_CONV:2 * D_CONV] * pc[:, 2 * D_CONV:3 * D_CONV]
    c0 = carry_ref[0:1, :]
    c1 = carry_ref[1:2, :]
    rid = lax.broadcasted_iota(I32, u.shape, 0)
    um1 = jnp.where(rid == 0, c1, pltpu.roll(u, 1, 0))
    um2 = jnp.where(rid == 0, c0, jnp.where(rid == 1, c1, pltpu.roll(u, 2, 0)))
    yc = cw_ref[0:1, :] * um2 + cw_ref[1:2, :] * um1 + cw_ref[2:3, :] * u
    yc_ref[...] = (cb * yc).astype(BF16)
    tail = u[tm - 2:tm, :]
    carry_ref[0:2, :] = tail
    nbuf_ref[...] = tail


def _inproj(x, g, wq, wg, wgt, wc, bg, bgt, cw, cbuf, *, tm, chunk):
    bsz, seq, _ = x.shape
    grid = (bsz, seq // tm)
    tok = lambda c: pl.BlockSpec((None, tm, c), lambda b, j: (b, j, 0))
    full = lambda shape: pl.BlockSpec(shape, lambda b, j: (0,) * len(shape))
    nck = tm // chunk
    return pl.pallas_call(
        functools.partial(_inproj_kernel, tm=tm, chunk=chunk),
        grid=grid,
        in_specs=[tok(D_MODEL), full((1, D_MODEL)), full((D_MODEL, 4 * D_MLSTM)),
                  full((D_MODEL, 128)), full((8, D_MODEL)), full((D_MODEL, 3 * D_CONV)),
                  full((1, 8)), full((8, 1)), full((CONV_WIDTH, D_CONV)),
                  pl.BlockSpec((None, 2, D_CONV), lambda b, j: (b, 0, 0))],
        out_specs=[tok(D_MLSTM), tok(D_MLSTM), tok(D_MLSTM), tok(D_MLSTM), tok(8),
                   pl.BlockSpec((None, nck, 8, chunk), lambda b, j: (b, j, 0, 0)),
                   tok(D_CONV),
                   pl.BlockSpec((None, 2, D_CONV), lambda b, j: (b, 0, 0))],
        out_shape=[jax.ShapeDtypeStruct((bsz, seq, D_MLSTM), BF16)] * 3
        + [jax.ShapeDtypeStruct((bsz, seq, D_MLSTM), F32),
           jax.ShapeDtypeStruct((bsz, seq, 8), F32),
           jax.ShapeDtypeStruct((bsz, seq // chunk, 8, chunk), F32),
           jax.ShapeDtypeStruct((bsz, seq, D_CONV), BF16),
           jax.ShapeDtypeStruct((bsz, 2, D_CONV), F32)],
        scratch_shapes=[pltpu.VMEM((8, D_CONV), F32)],
        compiler_params=_cparams(("arbitrary", "arbitrary")),
        name="inproj",
    )(x, g, wq, wg, wgt, wc, bg, bgt, cw, cbuf)


def _mlstm_kernel(q_ref, k_ref, v_ref, og_ref, gc_ref, gr_ref, c0_ref, n0_ref, m0_ref, ng_ref,
                  ym_ref, c1_ref, n1_ref, m1_ref, c_s, n_s, m_s, *, chunk, nchunks, bsz):
    j = pl.program_id(0)

    @pl.when(j == 0)
    def _():
        c_s[...] = c0_ref[...]
        n_s[...] = n0_ref[...]
        m_s[...] = m0_ref[...]

    ti = lax.broadcasted_iota(I32, (chunk, chunk), 0)
    ji = lax.broadcasted_iota(I32, (chunk, chunk), 1)
    causal = ji <= ti

    def body(ci, carry):
        r0 = pl.multiple_of(ci * chunk, chunk)
        rows = pl.ds(r0, chunk)
        chains = [(b, h) for b in range(bsz) for h in range(N_HEADS)]
        cols = lambda h: slice(h * HEAD_DIM, (h + 1) * HEAD_DIM)
        each = lambda f: [f(n, b, h) for n, (b, h) in enumerate(chains)]
        q = lambda b, h: q_ref[b, rows, cols(h)]
        k = lambda b, h: k_ref[b, rows, cols(h)]
        v = lambda b, h: v_ref[b, rows, cols(h)]
        gcs = [gc_ref[b, rows, :] for b in range(bsz)]
        grs = [gr_ref[b, ci] for b in range(bsz)]
        li_c = each(lambda n, b, h: gcs[b][:, h:h + 1])
        lf_c = each(lambda n, b, h: gcs[b][:, N_HEADS + h:N_HEADS + h + 1])
        li_r = each(lambda n, b, h: grs[b][h:h + 1, :])
        lf_r = each(lambda n, b, h: grs[b][N_HEADS + h:N_HEADS + h + 1, :])
        m_prev = each(lambda n, b, h: m_s[b, h:h + 1, :])

        b_c = each(lambda n, b, h: jnp.sum(jnp.where(causal, lf_r[n], 0.0), axis=1, keepdims=True))
        b_r = each(lambda n, b, h: jnp.sum(jnp.where(ti <= ji, lf_c[n], 0.0), axis=0, keepdims=True))
        dmat = each(lambda n, b, h: jnp.where(causal, b_c[n] - b_r[n] + li_r[n], -jnp.inf))
        dmax = each(lambda n, b, h: jnp.max(dmat[n], axis=1, keepdims=True))
        inter = each(lambda n, b, h: b_c[n] + m_prev[n])
        m_t = each(lambda n, b, h: jnp.maximum(inter[n], dmax[n]))
        w_inter = each(lambda n, b, h: jnp.exp(inter[n] - m_t[n]))
        s = each(lambda n, b, h: _dot_nt(q(b, h), k(b, h)) * jnp.exp(dmat[n] - m_t[n]))
        qc = each(lambda n, b, h: _dot(q(b, h), c_s[b, h].astype(BF16)))
        sv = each(lambda n, b, h: _dot(s[n].astype(BF16), v(b, h)))
        qn = each(lambda n, b, h: jnp.sum(q(b, h).astype(F32) * n_s[b, h:h + 1, :], axis=1,
                                          keepdims=True))
        den = each(lambda n, b, h: w_inter[n] * qn[n] + jnp.sum(s[n], axis=1, keepdims=True))
        hh = each(lambda n, b, h: (w_inter[n] * qc[n] + sv[n])
                  / jnp.maximum(jnp.abs(den[n]), jnp.exp(-m_t[n])))

        m_new = each(lambda n, b, h: m_t[n][chunk - 1:chunk, :])
        b_last = each(lambda n, b, h: b_c[n][chunk - 1:chunk, :])
        decay = each(lambda n, b, h: jnp.exp(b_last[n] + m_prev[n] - m_new[n]))
        kw = each(lambda n, b, h: k(b, h).astype(F32)
                  * jnp.exp(b_last[n] - b_c[n] + li_c[n] - m_new[n]))
        kv = each(lambda n, b, h: _dot_tn(kw[n].astype(BF16), v(b, h)))
        for n, (b, h) in enumerate(chains):
            c_s[b, h] = decay[n] * c_s[b, h] + kv[n]
            n_s[b, h:h + 1, :] = decay[n] * n_s[b, h:h + 1, :] + jnp.sum(kw[n], axis=0, keepdims=True)
            m_s[b, h:h + 1, :] = m_new[n]

        hn = each(lambda n, b, h: hh[n] * lax.rsqrt(jnp.mean(hh[n] * hh[n], axis=1, keepdims=True) + EPS)
                  * ng_ref[:, cols(h)])
        for n, (b, h) in enumerate(chains):
            ym_ref[b, rows, cols(h)] = (hn[n] * og_ref[b, rows, cols(h)]).astype(BF16)
        return carry

    lax.fori_loop(0, nchunks, body, 0)

    @pl.when(j == pl.num_programs(0) - 1)
    def _():
        c1_ref[...] = c_s[...]
        n1_ref[...] = n_s[...]
        m1_ref[...] = m_s[...]


def _mlstm(q, k, v, og, gcol, grow, c0, n0, m0, ng, *, ct, chunk):
    bsz, seq, _ = q.shape
    nchunks = ct // chunk
    grid = (seq // ct,)
    tok = lambda c: pl.BlockSpec((bsz, ct, c), lambda j: (0, j, 0))
    st_c = pl.BlockSpec((bsz, N_HEADS, HEAD_DIM, HEAD_DIM), lambda j: (0, 0, 0, 0))
    st_n = pl.BlockSpec((bsz, N_HEADS, HEAD_DIM), lambda j: (0, 0, 0))
    st_m = pl.BlockSpec((bsz, N_HEADS, 1), lambda j: (0, 0, 0))
    return pl.pallas_call(
        functools.partial(_mlstm_kernel, chunk=chunk, nchunks=nchunks, bsz=bsz),
        grid=grid,
        in_specs=[tok(D_MLSTM), tok(D_MLSTM), tok(D_MLSTM), tok(D_MLSTM), tok(8),
                  pl.BlockSpec((bsz, nchunks, 8, chunk), lambda j: (0, j, 0, 0)),
                  st_c, st_n, st_m,
                  pl.BlockSpec((1, D_MLSTM), lambda j: (0, 0))],
        out_specs=[tok(D_MLSTM), st_c, st_n, st_m],
        out_shape=[jax.ShapeDtypeStruct((bsz, seq, D_MLSTM), BF16),
                   jax.ShapeDtypeStruct((bsz, N_HEADS, HEAD_DIM, HEAD_DIM), F32),
                   jax.ShapeDtypeStruct((bsz, N_HEADS, HEAD_DIM), F32),
                   jax.ShapeDtypeStruct((bsz, N_HEADS, 1), F32)],
        scratch_shapes=[pltpu.VMEM((bsz, N_HEADS, HEAD_DIM, HEAD_DIM), F32),
                        pltpu.VMEM((bsz, N_HEADS, HEAD_DIM), F32),
                        pltpu.VMEM((bsz, N_HEADS, 1), F32)],
        compiler_params=_cparams(("arbitrary",)),
        name="mlstm",
    )(q, k, v, og, gcol, grow, c0, n0, m0, ng)


def _post_kernel(ym_ref, yc_ref, x_ref, wmo_ref, gx_ref, wxq_ref, mk_ref, mv_ref, wxo_ref,
                 gf_ref, wrt_ref, br_ref,
                 h2_ref, xn_ref, eid_ref, gate_ref, rank_ref, cnt_ref, cnt_s, *, tm, sub, group):
    step = pl.program_id(0) * pl.num_programs(1) + pl.program_id(1)

    @pl.when(step % group == 0)
    def _():
        cnt_s[...] = jnp.zeros_like(cnt_s)

    n_mem = mk_ref.shape[0]
    per = tm // n_mem
    n_groups = max(1, tm // POST_GROUP_ROWS) if n_mem == 1 else 1
    rows = [slice(g * (tm // n_groups), (g + 1) * (tm // n_groups)) for g in range(n_groups)]
    each = lambda f: [f(g, r) for g, r in enumerate(rows)]

    mix = each(lambda g, r: _dot(ym_ref[r, :], wmo_ref[0:D_MLSTM, :])
               + _dot(yc_ref[r, :], wmo_ref[D_MLSTM:D_MODEL, :]))
    h1 = each(lambda g, r: x_ref[r, :] + mix[g])
    xq = each(lambda g, r: _dot(_rms(h1[g], gx_ref[...]).astype(BF16), wxq_ref[...]).astype(BF16))

    def streams(g):
        if n_mem == 1:
            return [(slice(None), 0)]
        return [(slice(m * per, (m + 1) * per), m) for m in range(n_mem)]

    att = each(lambda g, r: [jnp.zeros((xq[g][q, :].shape[0], D_MODEL), F32) for q, _ in streams(g)])
    for hd in range(N_XHEADS):
        cols = slice(hd * XHEAD_DIM, (hd + 1) * XHEAD_DIM)
        s = each(lambda g, r: [_dot_nt(xq[g][q, cols], mk_ref[m, :, cols]) * (XHEAD_DIM ** -0.5)
                               for q, m in streams(g)])
        e = each(lambda g, r: [jnp.exp(v - jnp.max(v, axis=-1, keepdims=True)) for v in s[g]])
        p = each(lambda g, r: [(v / jnp.sum(v, axis=-1, keepdims=True)).astype(BF16) for v in e[g]])
        o = each(lambda g, r: [_dot(v, mv_ref[m, :, cols]).astype(BF16)
                               for v, (_, m) in zip(p[g], streams(g))])
        att = each(lambda g, r: [a + _dot(v, wxo_ref[cols, :]) for a, v in zip(att[g], o[g])])
    h2 = each(lambda g, r: h1[g] + jnp.concatenate(att[g], axis=0))
    xn2_parts = each(lambda g, r: _rms(h2[g], gf_ref[...]).astype(BF16))
    for g, r in enumerate(rows):
        h2_ref[r, :] = h2[g]
        xn_ref[r, :] = xn2_parts[g]
    xn2 = jnp.concatenate(xn2_parts, axis=0)

    logits = _dot_nt(wrt_ref[...], xn2) + br_ref[...]
    eidx = lax.broadcasted_iota(I32, logits.shape, 0).astype(F32)
    work = logits
    vals, ids, hots = [], [], []
    for _ in range(TOP_K):
        mx = jnp.max(work, axis=0, keepdims=True)
        idx = jnp.min(jnp.where(work == mx, eidx, float(N_EXPERTS)), axis=0, keepdims=True)
        sel = eidx == idx
        vals.append(mx)
        ids.append(idx)
        hots.append(sel)
        work = jnp.where(sel, -jnp.inf, work)
    exps = [jnp.exp(v - vals[0]) for v in vals]
    denom = exps[0] + exps[1] + exps[2] + exps[3]

    picked = jnp.zeros(logits.shape, F32)
    for sel in hots:
        picked = picked + sel.astype(F32)
    shift = jnp.full((tm, tm), sub.bit_length() - 1, I32)
    tj = lax.broadcasted_iota(I32, (tm, tm), 0)
    tt = lax.broadcasted_iota(I32, (tm, tm), 1)
    same = lax.shift_right_logical(tj, shift) == lax.shift_right_logical(tt, shift)
    before = jnp.where(jnp.logical_and(tj < tt, same), 1.0, 0.0).astype(BF16)
    prior = _dot(picked.astype(BF16), before) + cnt_s[...]
    for kk in range(TOP_K):
        eid_ref[kk:kk + 1, :] = ids[kk].astype(I32)
        gate_ref[kk:kk + 1, :] = exps[kk] / denom
        rank_ref[kk:kk + 1, :] = jnp.sum(jnp.where(hots[kk], prior, 0.0), axis=0,
                                         keepdims=True).astype(I32)
    for s in range(tm // sub):
        total = cnt_s[...] + jnp.sum(picked[:, s * sub:(s + 1) * sub], axis=1, keepdims=True)
        cnt_ref[s] = total.astype(I32)
    cnt_s[...] = total


def _post(ym, yc, x, wmo, gx, wxq, mkb, mvb, wxo, gf, wrt, br, *, tm, sub, group):
    bsz, seq, _ = x.shape
    nj = seq // tm
    grid = (bsz, nj)
    nsub = tm // sub
    n_tiles = bsz * nj * nsub // group
    tok = lambda c: pl.BlockSpec((None, tm, c), lambda b, j: (b, j, 0))
    full = lambda shape: pl.BlockSpec(shape, lambda b, j: (0,) * len(shape))
    n_mem = mkb.shape[0] // bsz
    mem = pl.BlockSpec((n_mem, N_MEM, D_MODEL), lambda b, j: (b, 0, 0))
    sel = pl.BlockSpec((None, TOP_K, tm), lambda b, j: (b, 0, j))
    return pl.pallas_call(
        functools.partial(_post_kernel, tm=tm, sub=sub, group=group),
        grid=grid,
        in_specs=[tok(D_MLSTM), tok(D_CONV), tok(D_MODEL), full((D_MODEL, D_MODEL)),
                  full((1, D_MODEL)), full((D_MODEL, D_MODEL)), mem, mem,
                  full((D_MODEL, D_MODEL)), full((1, D_MODEL)), full((N_EXPERTS, D_MODEL)),
                  full((N_EXPERTS, 1))],
        out_specs=[tok(D_MODEL), tok(D_MODEL), sel, sel, sel,
                   pl.BlockSpec((nsub, N_EXPERTS, 1), lambda b, j: ((b * nj + j) // group, 0, 0))],
        out_shape=[jax.ShapeDtypeStruct((bsz, seq, D_MODEL), F32),
                   jax.ShapeDtypeStruct((bsz, seq, D_MODEL), BF16),
                   jax.ShapeDtypeStruct((bsz, TOP_K, seq), I32),
                   jax.ShapeDtypeStruct((bsz, TOP_K, seq), F32),
                   jax.ShapeDtypeStruct((bsz, TOP_K, seq), I32),
                   jax.ShapeDtypeStruct((n_tiles, N_EXPERTS, 1), I32)],
        scratch_shapes=[pltpu.VMEM((N_EXPERTS, 1), F32)],
        compiler_params=_cparams(("arbitrary", "arbitrary")),
        name="post",
    )(ym, yc, x, wmo, gx, wxq, mkb, mvb, wxo, gf, wrt, br)


def _sorted_rows(n_tokens):
    return -(-(TOP_K * n_tokens + N_EXPERTS * (SUBLANES - 1)) // ROW_BLOCK) * ROW_BLOCK


def _dispatch_kernel(tab_ref, fill_ref, eid_ref, rank_ref, ls_ref, x_ref, eids_ref, ranks_ref, xs_ref,
                     xb_ref, srt, zero_s, pending, sems, *, n_ptiles, n_blocks):
    i = pl.program_id(0)
    cur = i % 2

    def rows_copy(src, src_row, dst_row, nrows, sl):
        return pltpu.make_async_copy(
            src.at[pl.ds(pl.multiple_of(src_row, SUBLANES), nrows), :],
            xb_ref.at[pl.ds(pl.multiple_of(dst_row, SUBLANES), nrows), :], sems.at[sl])

    def granule(src, src_row, dst_row, sl):
        return rows_copy(src, src_row, dst_row, SUBLANES, sl)

    def drain(count, sl, nrows=SUBLANES):
        def body(g, c):
            rows_copy(zero_s, 0, 0, nrows, sl).wait()
            return c
        lax.fori_loop(0, count, body, 0)

    @pl.when(i == 0)
    def _():
        pending[0] = 0
        pending[1] = 0

    def issue_list(count, dst_at, src_at, nrows):
        def body(h, c):
            g = 2 * h
            rows_copy(srt.at[cur], tab_ref[src_at + g], tab_ref[dst_at + g], nrows, cur).start(priority=0)

            @pl.when(g + 1 < count)
            def _():
                rows_copy(srt.at[cur], tab_ref[src_at + g + 1], tab_ref[dst_at + g + 1], nrows,
                          cur).start(priority=1)
            return c
        lax.fori_loop(0, (count + 1) // 2, body, 0)

    def sort_and_move(eid, rank, x):
        ntok = x.shape[0]
        nrows = _sorted_rows(ntok)
        e_iota = lax.broadcasted_iota(I32, (N_EXPERTS, ntok), 0)
        s_iota = lax.broadcasted_iota(I32, (nrows, ntok), 0)
        seg_start = ls_ref[...]
        hit = None
        for kk in range(TOP_K):
            start = jnp.sum(jnp.where(e_iota == eid[kk:kk + 1, :], seg_start, 0.0),
                            axis=0, keepdims=True).astype(I32)
            match = s_iota == start + rank[kk:kk + 1, :]
            hit = match if hit is None else jnp.logical_or(hit, match)
        perm = jnp.where(hit, 1.0, 0.0).astype(BF16)
        srt[cur, 0:nrows, :] = _dot(perm, x)
        drain(pending[0], 1 - cur, BIG_ROWS)
        drain(pending[1], 1 - cur)
        n_big, n_small = tab_ref[TAB_COUNTS], tab_ref[TAB_COUNTS + 1]
        issue_list(n_big, TAB_BIG_DST, TAB_BIG_SRC, BIG_ROWS)
        issue_list(n_small, TAB_SMALL_DST, TAB_SMALL_SRC, SUBLANES)
        pending[0] = n_big
        pending[1] = n_small

    @pl.when(i < n_ptiles)
    def _():
        sort_and_move(eid_ref[...], rank_ref[...], x_ref[...])

    @pl.when(i == n_ptiles)
    def _():
        sort_and_move(eids_ref[...], ranks_ref[...], xs_ref[...])
        drain(pending[0], cur, BIG_ROWS)
        drain(pending[1], cur)
        zero_s[...] = jnp.zeros_like(zero_s)
        for e in range(N_EXPERTS):
            dst = fill_ref[e]

            def zissue(g, c):
                granule(zero_s, 0, dst + g * SUBLANES, cur).start()
                return c
            lax.fori_loop(0, fill_ref[N_EXPERTS + e], zissue, 0)
        drain(fill_ref[2 * N_EXPERTS], cur)
        first_free = fill_ref[2 * N_EXPERTS + 1]

        def blk_copy(b):
            return pltpu.make_async_copy(
                zero_s, xb_ref.at[pl.ds(pl.multiple_of(b * ROW_BLOCK, ROW_BLOCK), ROW_BLOCK), :],
                sems.at[cur])

        def bissue(b, c):
            blk_copy(b).start()
            return c
        lax.fori_loop(first_free, n_blocks, bissue, 0)

        def bdrain(b, c):
            blk_copy(0).wait()
            return c
        lax.fori_loop(first_free, n_blocks, bdrain, 0)


def _dispatch(tab, fill, eid_p, rank_p, seg_start, xn_p, eid_s, rank_s, xn_s, *, tmd, n_blocks):
    n_ptiles = eid_p.shape[0]
    n_sample = xn_s.shape[0]
    last = n_ptiles - 1
    smem = lambda shape, imap: pl.BlockSpec(shape, imap, memory_space=pltpu.SMEM)
    return pl.pallas_call(
        functools.partial(_dispatch_kernel, n_ptiles=n_ptiles, n_blocks=n_blocks),
        grid=(n_ptiles + 1,),
        in_specs=[smem((TABLE_WIDTH,), lambda i: (i,)),
                  smem((TABLE_WIDTH,), lambda i: (0,)),
                  pl.BlockSpec((None, TOP_K, tmd), lambda i: (jnp.minimum(i, last), 0, 0)),
                  pl.BlockSpec((None, TOP_K, tmd), lambda i: (jnp.minimum(i, last), 0, 0)),
                  pl.BlockSpec((None, N_EXPERTS, 1), lambda i: (i, 0, 0)),
                  pl.BlockSpec((tmd, D_MODEL), lambda i: (jnp.minimum(i, last), 0)),
                  pl.BlockSpec((TOP_K, n_sample), lambda i: (0, 0)),
                  pl.BlockSpec((TOP_K, n_sample), lambda i: (0, 0)),
                  pl.BlockSpec((n_sample, D_MODEL), lambda i: (0, 0))],
        out_specs=pl.BlockSpec(memory_space=pl.ANY),
        out_shape=jax.ShapeDtypeStruct((n_blocks * ROW_BLOCK, D_MODEL), F32),
        scratch_shapes=[pltpu.VMEM((2, _sorted_rows(tmd), D_MODEL), F32),
                        pltpu.VMEM((ROW_BLOCK, D_MODEL), F32), pltpu.SMEM((2,), I32),
                        pltpu.SemaphoreType.DMA((2,))],
        compiler_params=_cparams(("arbitrary",)),
        name="dispatch",
    )(tab, fill, eid_p, rank_p, seg_start, xn_p, eid_s, rank_s, xn_s)


def _expert_kernel(tab_ref, xb_ref, wu_hbm, bu_ref, wd_hbm, bd_ref, yb_ref,
                   wu_f, wd_f, wu_s, wd_s, xbuf, ybuf, wsem, xsem, ysem, *, n_blocks):
    e = pl.program_id(0)
    first = tab_ref[e]
    nblk = tab_ref[N_EXPERTS + e]
    n_used = tab_ref[2 * N_EXPERTS]

    def hbm_rows(b):
        return pl.ds(pl.multiple_of(b * ROW_BLOCK, ROW_BLOCK), ROW_BLOCK)

    def x_copy(b, sl):
        return pltpu.make_async_copy(xb_ref.at[hbm_rows(b), :], xbuf.at[sl], xsem.at[sl])

    def y_copy(b, sl):
        return pltpu.make_async_copy(ybuf.at[sl], yb_ref.at[hbm_rows(b), :], ysem.at[sl])

    def w_copies(ex, sl):
        return (pltpu.make_async_copy(wu_hbm.at[ex], wu_f.at[sl], wsem.at[0, sl]),
                pltpu.make_async_copy(wd_hbm.at[ex], wd_f.at[sl], wsem.at[1, sl]))

    wslot = e % 2

    @pl.when(e == 0)
    def _():
        for c in w_copies(0, 0):
            c.start()
        for b in range(ROW_LOOKAHEAD):
            @pl.when(b < n_used)
            def _():
                x_copy(b, b % ROW_RING).start()

    @pl.when(e + 1 < N_EXPERTS)
    def _():
        for c in w_copies(e + 1, 1 - wslot):
            c.start()

    for c in w_copies(e, wslot):
        c.wait()

    step = 128

    def cast(r, c):
        rows = pl.ds(pl.multiple_of(r * step, step), step)
        wu_s[rows, :] = wu_f[wslot, rows, :].astype(BF16)
        wd_s[rows, :] = wd_f[wslot, rows, :].astype(BF16)
        return c
    lax.fori_loop(0, D_MODEL // step, cast, 0)

    def block(b, c):
        sl = b % ROW_RING
        x_copy(b, sl).wait()

        @pl.when(b + ROW_LOOKAHEAD < n_used)
        def _():
            x_copy(b + ROW_LOOKAHEAD, (b + ROW_LOOKAHEAD) % ROW_RING).start()

        @pl.when(b >= ROW_RING)
        def _():
            y_copy(b, sl).wait()

        hcat = _dot(xbuf[sl].astype(BF16), wu_s[...]) + bu_ref[...]
        glu = jnp.minimum(hcat[:, 0:D_FF], SWIGLU_LIMIT)
        lin = jnp.clip(hcat[:, D_FF:2 * D_FF], -SWIGLU_LIMIT, SWIGLU_LIMIT)
        act = glu * _sigmoid(SWIGLU_ALPHA * glu) * (lin + 1.0)
        ybuf[sl] = _dot(act.astype(BF16), wd_s[...]) + bd_ref[...]
        y_copy(b, sl).start()
        return c
    lax.fori_loop(first, first + nblk, block, 0)

    @pl.when(e == N_EXPERTS - 1)
    def _():
        for k in range(ROW_RING):
            @pl.when(k < n_used)
            def _():
                y_copy(0, (n_used - 1 - k) % ROW_RING).wait()

        ybuf[0] = jnp.zeros((ROW_BLOCK, D_MODEL), F32)

        def zissue(b, c):
            y_copy(b, 0).start()
            return c
        lax.fori_loop(n_used, n_blocks, zissue, 0)

        def zdrain(b, c):
            y_copy(0, 0).wait()
            return c
        lax.fori_loop(n_used, n_blocks, zdrain, 0)


def _experts(tab, xb, w_up, b_up, w_down, b_down, *, n_blocks):
    w_map = lambda e, tab: (e, 0, 0)
    grid_spec = pltpu.PrefetchScalarGridSpec(
        num_scalar_prefetch=1,
        grid=(N_EXPERTS,),
        in_specs=[pl.BlockSpec(memory_space=pl.ANY),
                  pl.BlockSpec(memory_space=pl.ANY),
                  pl.BlockSpec((None, 1, 2 * D_FF), w_map),
                  pl.BlockSpec(memory_space=pl.ANY),
                  pl.BlockSpec((None, 1, D_MODEL), w_map)],
        out_specs=pl.BlockSpec(memory_space=pl.ANY),
        scratch_shapes=[pltpu.VMEM((2, D_MODEL, 2 * D_FF), F32), pltpu.VMEM((2, D_FF, D_MODEL), F32),
                        pltpu.VMEM((D_MODEL, 2 * D_FF), BF16), pltpu.VMEM((D_FF, D_MODEL), BF16),
                        pltpu.VMEM((ROW_RING, ROW_BLOCK, D_MODEL), F32),
                        pltpu.VMEM((ROW_RING, ROW_BLOCK, D_MODEL), F32),
                        pltpu.SemaphoreType.DMA((2, 2)),
                        pltpu.SemaphoreType.DMA((ROW_RING,)), pltpu.SemaphoreType.DMA((ROW_RING,))],
    )
    return pl.pallas_call(
        functools.partial(_expert_kernel, n_blocks=n_blocks),
        grid_spec=grid_spec,
        out_shape=jax.ShapeDtypeStruct((n_blocks * ROW_BLOCK, D_MODEL), F32),
        compiler_params=_cparams(("arbitrary",)),
        name="experts",
    )(tab, xb, w_up, b_up, w_down, b_down)


def _split_bf16(a):
    hi = a.astype(BF16)
    return hi, (a - hi.astype(F32)).astype(BF16)


def _combine_kernel(tab_ref, nxt_ref, slot_ref, gate_ref, h2_ref, gfin_ref, yb_ref, y_ref, buf, sems,
                    *, tm):
    i = pl.program_id(0)
    n = pl.num_programs(0)
    cur = i % 2
    nrows = buf.shape[1]

    def rows_copy(src_row, sl, dst_row, nrows):
        return pltpu.make_async_copy(
            yb_ref.at[pl.ds(pl.multiple_of(src_row, SUBLANES), nrows), :],
            buf.at[sl, pl.ds(pl.multiple_of(dst_row, SUBLANES), nrows), :], sems.at[sl])

    def gather(t_ref, sl):
        def fetch(count, yb_at, tile_at, nrows):
            def body(h, c):
                g = 2 * h
                rows_copy(t_ref[yb_at + g], sl, t_ref[tile_at + g], nrows).start(priority=0)

                @pl.when(g + 1 < count)
                def _():
                    rows_copy(t_ref[yb_at + g + 1], sl, t_ref[tile_at + g + 1], nrows).start(priority=1)
                return c
            lax.fori_loop(0, (count + 1) // 2, body, 0)
        fetch(t_ref[TAB_COUNTS], TAB_BIG_DST, TAB_BIG_SRC, BIG_ROWS)
        fetch(t_ref[TAB_COUNTS + 1], TAB_SMALL_DST, TAB_SMALL_SRC, SUBLANES)

    @pl.when(i == 0)
    def _():
        buf[...] = jnp.zeros_like(buf)
        gather(tab_ref, 0)

    @pl.when(i + 1 < n)
    def _():
        gather(nxt_ref, 1 - cur)

    def drain(count, nrows):
        def body(g, c):
            rows_copy(0, cur, 0, nrows).wait()
            return c
        lax.fori_loop(0, count, body, 0)
    drain(tab_ref[TAB_COUNTS], BIG_ROWS)
    drain(tab_ref[TAB_COUNTS + 1], SUBLANES)

    s_iota = lax.broadcasted_iota(I32, (tm, nrows), 1)
    wgt = jnp.zeros((tm, nrows), F32)
    for kk in range(TOP_K):
        wgt = wgt + jnp.where(s_iota == slot_ref[:, kk:kk + 1], gate_ref[:, kk:kk + 1], 0.0)
    w_hi, w_lo = _split_bf16(wgt)
    rows = buf[cur].astype(BF16)
    acc = _dot(w_hi, rows) + _dot(w_lo, rows)
    y_ref[...] = _rms(h2_ref[...] + acc, gfin_ref[...])


def _combine(tab, slot_col, gate_col, h2, gfin, yb, *, tm):
    ntok = h2.shape[0]
    n = ntok // tm
    smem = lambda imap: pl.BlockSpec((TABLE_WIDTH,), imap, memory_space=pltpu.SMEM)
    return pl.pallas_call(
        functools.partial(_combine_kernel, tm=tm),
        grid=(n,),
        in_specs=[smem(lambda i: (i,)), smem(lambda i: (jnp.minimum(i + 1, n - 1),)),
                  pl.BlockSpec((tm, TOP_K), lambda i: (i, 0)),
                  pl.BlockSpec((tm, TOP_K), lambda i: (i, 0)),
                  pl.BlockSpec((tm, D_MODEL), lambda i: (i, 0)),
                  pl.BlockSpec((1, D_MODEL), lambda i: (0, 0)),
                  pl.BlockSpec(memory_space=pl.ANY)],
        out_specs=pl.BlockSpec((tm, D_MODEL), lambda i: (i, 0)),
        out_shape=jax.ShapeDtypeStruct((ntok, D_MODEL), F32),
        scratch_shapes=[pltpu.VMEM((2, _sorted_rows(tm), D_MODEL), F32),
                        pltpu.SemaphoreType.DMA((2,))],
        compiler_params=_cparams(("arbitrary",)),
        name="combine",
    )(tab, tab, slot_col, gate_col, h2, gfin, yb)


def _copy_tables(seg, seg_src, seg_dst):
    experts = jnp.arange(N_EXPERTS, dtype=I32)

    def flatten(counts, bound, src0, dst0, step):
        ends = jnp.cumsum(counts, axis=1)
        idx = jnp.arange(bound, dtype=I32)
        owner = jnp.sum(ends[:, None, :] <= idx[None, :, None], axis=-1)
        pick = lambda a: jnp.sum(jnp.where(owner[..., None] == experts, a[:, None, :], 0), axis=-1)
        off = (idx[None, :] - pick(ends - counts)) * step
        valid = idx[None, :] < ends[:, -1:]
        return (jnp.where(valid, pick(dst0) + off, 0), jnp.where(valid, pick(src0) + off, 0),
                ends[:, -1:])

    n_big = seg // BIG_ROWS
    n_small = (seg - n_big * BIG_ROWS) // SUBLANES
    b_dst, b_src, b_n = flatten(n_big, MAX_BIG, seg_src, seg_dst, BIG_ROWS)
    s_dst, s_src, s_n = flatten(n_small, MAX_SMALL, seg_src + n_big * BIG_ROWS,
                                seg_dst + n_big * BIG_ROWS, SUBLANES)
    tab = jnp.concatenate([b_dst, b_src, s_dst, s_src, b_n, s_n], axis=1)
    return jnp.pad(tab, ((0, 0), (0, TABLE_WIDTH - tab.shape[1]))).astype(I32).reshape(-1)


def _tiles(a, tile):
    bsz, kk, seq = a.shape
    return a.reshape(bsz, kk, seq // tile, tile).transpose(0, 2, 1, 3).reshape(-1, kk, tile)


def _path(x, cbuf, c0, n0, m0, mkb, mvb, wts, *, tm_in, ct, chunk, tm_post, sub, fold):
    q, k, v, og, gcol, grow, yc, nbuf = _inproj(
        x, wts["g_mix"], wts["wq"], wts["wg"], wts["wgt"], wts["wc"], wts["bg"], wts["bgt"],
        wts["cw"], cbuf, tm=tm_in, chunk=chunk)
    ym, c1, n1, m1 = _mlstm(q, k, v, og, gcol, grow, c0, n0, m0, wts["ng"], ct=ct, chunk=chunk)
    if fold:
        ym, yc, x = (a.reshape(1, -1, a.shape[-1]) for a in (ym, yc, x))
    h2, xn, eid, gate, rank, cnt = _post(
        ym, yc, x, wts["wmo"], wts["g_x"], wts["wxq"], mkb, mvb, wts["wxo"], wts["g_ffn"],
        wts["wrt"], wts["br"], tm=tm_post, sub=sub, group=1)
    return dict(h2=h2, xn=xn, eid=eid, gate=gate, rank=rank, cnt=cnt[:, :, 0],
                c1=c1, n1=n1, m1=m1[..., 0], nbuf=nbuf)


def kernel(x_prompt, x_sample, state_mlstm_c, state_mlstm_n, state_mlstm_m, state_conv, cache_mem_k, cache_mem_v, mem_prompt, norm_mix_g, w_in, b_gate, mlstm_norm_g, conv_w, w_mix_out, norm_x_g, norm_mem_g, w_xq, w_xk, w_xv, w_xo, norm_ffn_g, w_router, b_router, w_up, b_up, w_down, b_down, norm_final_g):
    bp, lp, _ = x_prompt.shape
    bs, ls, _ = x_sample.shape
    l = 0
    row = lambda a: a.reshape(1, -1)

    wi = w_in[l]
    gate_cols = wi[:, 4 * D_MLSTM:4 * D_MLSTM + 2 * N_HEADS]
    wts = dict(
        g_mix=row(norm_mix_g[l]),
        wq=wi[:, 0:4 * D_MLSTM].astype(BF16),
        wg=jnp.pad(gate_cols, ((0, 0), (0, 128 - 2 * N_HEADS))).astype(BF16),
        wgt=gate_cols.T.astype(BF16),
        wc=wi[:, 4 * D_MLSTM + 2 * N_HEADS:].astype(BF16),
        bg=row(b_gate[l]), bgt=b_gate[l].reshape(-1, 1),
        cw=conv_w[l], ng=row(mlstm_norm_g[l]),
        wmo=w_mix_out[l].astype(BF16), g_x=row(norm_x_g[l]), wxq=w_xq[l].astype(BF16),
        wxo=w_xo[l].astype(BF16), g_ffn=row(norm_ffn_g[l]),
        wrt=w_router[l].T.astype(BF16), br=b_router[l].reshape(-1, 1),
    )

    mk, mv, mkb, mvb = _memkv(mem_prompt.reshape(bp * N_MEM, D_MODEL), row(norm_mem_g[l]),
                              w_xk[l].astype(BF16), w_xv[l].astype(BF16))
    zeros = lambda *s: jnp.zeros(s, F32)
    tm_post, tmd = 512, 256
    assert _sorted_rows(tmd) // BIG_ROWS <= MAX_BIG
    pr = _path(x_prompt, zeros(bp, CONV_WIDTH - 1, D_CONV), zeros(bp, N_HEADS, HEAD_DIM, HEAD_DIM),
               zeros(bp, N_HEADS, HEAD_DIM), zeros(bp, N_HEADS, 1),
               mkb.reshape(bp, N_MEM, D_MODEL), mvb.reshape(bp, N_MEM, D_MODEL), wts,
               tm_in=512, ct=512, chunk=CHUNK, tm_post=tm_post, sub=tmd, fold=False)
    sa = _path(x_sample, state_conv[l], state_mlstm_c[l], state_mlstm_n[l],
               state_mlstm_m[l][..., None],
               cache_mem_k[l].reshape(bs, N_MEM, D_MODEL).astype(BF16),
               cache_mem_v[l].reshape(bs, N_MEM, D_MODEL).astype(BF16), wts,
               tm_in=ls, ct=ls, chunk=min(CHUNK, ls), tm_post=bs * ls, sub=bs * ls, fold=True)

    n_ptok, n_stok = bp * lp, bs * ls
    cnt = jnp.concatenate([pr["cnt"], sa["cnt"]], axis=0)
    n_tiles = cnt.shape[0]
    n_blocks = -(-(TOP_K * (n_ptok + n_stok) + n_tiles * N_EXPERTS * (SUBLANES - 1)) // ROW_BLOCK) \
        + N_EXPERTS
    seg = (cnt + SUBLANES - 1) // SUBLANES * SUBLANES
    seg_src = jnp.cumsum(seg, axis=1) - seg
    tot = jnp.sum(seg, axis=0)
    blocks_e = (tot + ROW_BLOCK - 1) // ROW_BLOCK
    padded = blocks_e * ROW_BLOCK
    pstart = jnp.cumsum(padded) - padded
    bend = jnp.cumsum(blocks_e)
    n_used = bend[-1]
    exp_tab = jnp.concatenate([bend - blocks_e, blocks_e, n_used[None]]).astype(I32)
    seg_dst = pstart[None, :] + jnp.cumsum(seg, axis=0) - seg

    tab = _copy_tables(seg, seg_src, seg_dst)
    fgran = (padded - tot) // SUBLANES
    fill = jnp.concatenate([pstart + tot, fgran, jnp.sum(fgran)[None], n_used[None]])
    fill = jnp.pad(fill, (0, TABLE_WIDTH - fill.shape[0])).astype(I32)

    def lookup(table, eid):
        hit = eid[..., None] == jnp.arange(N_EXPERTS, dtype=I32)
        return jnp.sum(jnp.where(hit, table[:, None, None, :], 0), axis=-1)

    eid_pt, rank_pt = _tiles(pr["eid"], tmd), _tiles(pr["rank"], tmd)
    eid_st = sa["eid"].transpose(1, 0, 2).reshape(1, TOP_K, n_stok)
    rank_st = sa["rank"].transpose(1, 0, 2).reshape(1, TOP_K, n_stok)
    slot_p = (lookup(seg_src[:-1], eid_pt) + rank_pt).transpose(0, 2, 1).reshape(n_ptok, TOP_K)
    slot_s = (lookup(seg_src[-1:], eid_st) + rank_st).transpose(0, 2, 1).reshape(n_stok, TOP_K)

    xb = _dispatch(tab, fill, eid_pt, rank_pt, seg_src.astype(F32)[..., None],
                   pr["xn"].reshape(n_ptok, D_MODEL), eid_st[0], rank_st[0],
                   sa["xn"].reshape(n_stok, D_MODEL), tmd=tmd, n_blocks=n_blocks)
    yb = _experts(exp_tab, xb, w_up[l], b_up[l][:, None, :], w_down[l], b_down[l][:, None, :],
                  n_blocks=n_blocks)

    gfin = row(norm_final_g)
    split = (n_tiles - 1) * TABLE_WIDTH
    y_p = _combine(tab[:split], slot_p.astype(I32), pr["gate"].transpose(0, 2, 1).reshape(n_ptok, TOP_K),
                   pr["h2"].reshape(n_ptok, D_MODEL), gfin, yb, tm=tmd)
    y_s = _combine(tab[split:], slot_s.astype(I32), sa["gate"].transpose(0, 2, 1).reshape(n_stok, TOP_K),
                   sa["h2"].reshape(n_stok, D_MODEL), gfin, yb, tm=n_stok)

    lead = lambda a: a[None]
    return (y_p.reshape(bp, lp, D_MODEL), y_s.reshape(bs, ls, D_MODEL),
            lead(pr["c1"]), lead(pr["n1"]), lead(pr["m1"]), lead(pr["nbuf"]),
            lead(mk.reshape(bp, N_MEM, N_XHEADS, XHEAD_DIM)),
            lead(mv.reshape(bp, N_MEM, N_XHEADS, XHEAD_DIM)),
            lead(sa["c1"]), lead(sa["n1"]), lead(sa["m1"]), lead(sa["nbuf"]))
```

```python
import functools

import jax
import jax.numpy as jnp
import numpy as np
from jax import lax
from jax.experimental import pallas as pl
from jax.experimental.pallas import tpu as pltpu

F32 = jnp.float32
BF16 = jnp.bfloat16
I32 = jnp.int32

D_MODEL = 1024
N_HEADS = 4
HEAD_DIM = 128
D_MLSTM = N_HEADS * HEAD_DIM
D_CONV = D_MODEL - D_MLSTM
CONV_WIDTH = 3
CHUNK = 64
N_MEM = 256
N_XHEADS = 4
XHEAD_DIM = D_MODEL // N_XHEADS
N_EXPERTS = 32
TOP_K = 4
D_FF = D_MODEL
SWIGLU_LIMIT = 7.0
SWIGLU_ALPHA = 1.702
EPS = 1e-5

SUBLANES = 8
BIG_ROWS = 4 * SUBLANES
MAX_BIG, MAX_SMALL = 64, 3 * N_EXPERTS
TAB_BIG_DST, TAB_BIG_SRC = 0, MAX_BIG
TAB_SMALL_DST, TAB_SMALL_SRC = 2 * MAX_BIG, 2 * MAX_BIG + MAX_SMALL
TAB_COUNTS = 2 * MAX_BIG + 2 * MAX_SMALL
TABLE_WIDTH = 512
POST_GROUP_ROWS = 512
ROW_BLOCK = 256
ROW_LOOKAHEAD = 3
ROW_RING = ROW_LOOKAHEAD + 1
VMEM_LIMIT = 56 * 1024 * 1024


def _cparams(sem):
    return pltpu.CompilerParams(dimension_semantics=sem, vmem_limit_bytes=VMEM_LIMIT)


def _rms(x, g):
    return x * lax.rsqrt(jnp.mean(x * x, axis=-1, keepdims=True) + EPS) * g


def _log_sigmoid(x):
    return -(jnp.maximum(-x, 0.0) + jnp.log1p(jnp.exp(-jnp.abs(x))))


def _sigmoid(x):
    return 1.0 / (1.0 + jnp.exp(-x))


def _dot(a, b):
    return jnp.dot(a, b, preferred_element_type=F32)


def _dot_nt(a, b):
    return lax.dot_general(a, b, (((1,), (1,)), ((), ())), preferred_element_type=F32)


def _dot_tn(a, b):
    return lax.dot_general(a, b, (((0,), (0,)), ((), ())), preferred_element_type=F32)


def _memkv_kernel(mem_ref, g_ref, wk_ref, wv_ref, mk_ref, mv_ref, mkb_ref, mvb_ref):
    mn = _rms(mem_ref[...], g_ref[...]).astype(BF16)
    mk = _dot(mn, wk_ref[...])
    mv = _dot(mn, wv_ref[...])
    mk_ref[...] = mk
    mv_ref[...] = mv
    mkb_ref[...] = mk.astype(BF16)
    mvb_ref[...] = mv.astype(BF16)


def _memkv(mem2d, g, wk, wv):
    rows = mem2d.shape[0]
    tm = N_MEM
    row_spec = pl.BlockSpec((tm, D_MODEL), lambda i: (i, 0))
    full = lambda shape: pl.BlockSpec(shape, lambda i: (0,) * len(shape))
    return pl.pallas_call(
        _memkv_kernel,
        grid=(rows // tm,),
        in_specs=[row_spec, full((1, D_MODEL)), full((D_MODEL, D_MODEL)), full((D_MODEL, D_MODEL))],
        out_specs=[row_spec, row_spec, row_spec, row_spec],
        out_shape=[jax.ShapeDtypeStruct((rows, D_MODEL), F32)] * 2
        + [jax.ShapeDtypeStruct((rows, D_MODEL), BF16)] * 2,
        compiler_params=_cparams(("arbitrary",)),
        name="memkv",
    )(mem2d, g, wk, wv)


def _inproj_kernel(x_ref, g_ref, wq_ref, wg_ref, wgt_ref, wc_ref, bg_ref, bgt_ref, cw_ref, cbuf_ref,
                   q_ref, k_ref, v_ref, og_ref, gcol_ref, grow_ref, yc_ref, nbuf_ref,
                   carry_ref, *, tm, chunk):
    j = pl.program_id(1)
    streams = cbuf_ref.shape[0]
    per = tm // streams

    @pl.when(j == 0)
    def _():
        carry_ref[0:2, :] = cbuf_ref[0]

    xb = _rms(x_ref[...], g_ref[...]).astype(BF16)

    p = _dot(xb, wq_ref[...])
    q_ref[...] = p[:, 0:D_MLSTM].astype(BF16)
    k_ref[...] = (p[:, D_MLSTM:2 * D_MLSTM] * (HEAD_DIM ** -0.5)).astype(BF16)
    v_ref[...] = p[:, 2 * D_MLSTM:3 * D_MLSTM].astype(BF16)
    og_ref[...] = _sigmoid(p[:, 3 * D_MLSTM:4 * D_MLSTM])

    gc = _dot(xb, wg_ref[...])[:, 0:2 * N_HEADS] + bg_ref[...]
    col = lax.broadcasted_iota(I32, gc.shape, 1)
    gcol_ref[...] = jnp.where(col < N_HEADS, gc, _log_sigmoid(gc))
    gr = _dot_nt(wgt_ref[...], xb) + bgt_ref[...]
    row = lax.broadcasted_iota(I32, gr.shape, 0)
    gr = jnp.where(row < N_HEADS, gr, _log_sigmoid(gr))
    for c in range(tm // chunk):
        grow_ref[c] = gr[:, c * chunk:(c + 1) * chunk]

    pc = _dot(xb, wc_ref[...])
    cb = pc[:, 0:D_CONV]
    u = pc[:, D_CONV:2 * D_CONV] * pc[:, 2 * D_CONV:3 * D_CONV]
    rid = lax.broadcasted_iota(I32, u.shape, 0)
    um1 = pltpu.roll(u, 1, 0)
    um2 = pltpu.roll(u, 2, 0)
    for m in range(streams):
        prev = carry_ref if streams == 1 else cbuf_ref.at[m]
        c0, c1 = prev[0:1, :], prev[1:2, :]
        um1 = jnp.where(rid == m * per, c1, um1)
        um2 = jnp.where(rid == m * per, c0, jnp.where(rid == m * per + 1, c1, um2))
        nbuf_ref[m] = u[(m + 1) * per - 2:(m + 1) * per, :]
    yc = cw_ref[0:1, :] * um2 + cw_ref[1:2, :] * um1 + cw_ref[2:3, :] * u
    yc_ref[...] = (cb * yc).astype(BF16)
    carry_ref[0:2, :] = u[tm - 2:tm, :]


def _inproj(x, g, wq, wg, wgt, wc, bg, bgt, cw, cbuf, *, tm, chunk):
    bsz, seq, _ = x.shape
    grid = (bsz, seq // tm)
    tok = lambda c: pl.BlockSpec((None, tm, c), lambda b, j: (b, j, 0))
    full = lambda shape: pl.BlockSpec(shape, lambda b, j: (0,) * len(shape))
    nck = tm // chunk
    streams = cbuf.shape[0] // bsz
    assert streams == 1 or tm == seq
    conv_state = pl.BlockSpec((streams, 2, D_CONV), lambda b, j: (b, 0, 0))
    return pl.pallas_call(
        functools.partial(_inproj_kernel, tm=tm, chunk=chunk),
        grid=grid,
        in_specs=[tok(D_MODEL), full((1, D_MODEL)), full((D_MODEL, 4 * D_MLSTM)),
                  full((D_MODEL, 128)), full((8, D_MODEL)), full((D_MODEL, 3 * D_CONV)),
                  full((1, 8)), full((8, 1)), full((CONV_WIDTH, D_CONV)), conv_state],
        out_specs=[tok(D_MLSTM), tok(D_MLSTM), tok(D_MLSTM), tok(D_MLSTM), tok(8),
                   pl.BlockSpec((None, nck, 8, chunk), lambda b, j: (b, j, 0, 0)),
                   tok(D_CONV), conv_state],
        out_shape=[jax.ShapeDtypeStruct((bsz, seq, D_MLSTM), BF16)] * 3
        + [jax.ShapeDtypeStruct((bsz, seq, D_MLSTM), F32),
           jax.ShapeDtypeStruct((bsz, seq, 8), F32),
           jax.ShapeDtypeStruct((bsz, seq // chunk, 8, chunk), F32),
           jax.ShapeDtypeStruct((bsz, seq, D_CONV), BF16),
           jax.ShapeDtypeStruct(cbuf.shape, F32)],
        scratch_shapes=[pltpu.VMEM((8, D_CONV), F32)],
        compiler_params=_cparams(("arbitrary", "arbitrary")),
        name="inproj",
    )(x, g, wq, wg, wgt, wc, bg, bgt, cw, cbuf)


def _mlstm_kernel(q_ref, k_ref, v_ref, og_ref, gc_ref, gr_ref, c0_ref, n0_ref, m0_ref, ng_ref,
                  ym_ref, c1_ref, n1_ref, m1_ref, c_s, n_s, m_s, *, chunk, nchunks, bsz):
    j = pl.program_id(0)

    @pl.when(j == 0)
    def _():
        c_s[...] = c0_ref[...]
        n_s[...] = n0_ref[...]
        m_s[...] = m0_ref[...]

    ti = lax.broadcasted_iota(I32, (chunk, chunk), 0)
    ji = lax.broadcasted_iota(I32, (chunk, chunk), 1)
    causal = ji <= ti

    def body(ci, carry):
        r0 = pl.multiple_of(ci * chunk, chunk)
        rows = pl.ds(r0, chunk)
        chains = [(b, h) for b in range(bsz) for h in range(N_HEADS)]
        cols = lambda h: slice(h * HEAD_DIM, (h + 1) * HEAD_DIM)
        each = lambda f: [f(n, b, h) for n, (b, h) in enumerate(chains)]
        q = lambda b, h: q_ref[b, rows, cols(h)]
        k = lambda b, h: k_ref[b, rows, cols(h)]
        v = lambda b, h: v_ref[b, rows, cols(h)]
        gcs = [gc_ref[b, rows, :] for b in range(bsz)]
        grs = [gr_ref[b, ci] for b in range(bsz)]
        li_c = each(lambda n, b, h: gcs[b][:, h:h + 1])
        lf_c = each(lambda n, b, h: gcs[b][:, N_HEADS + h:N_HEADS + h + 1])
        li_r = each(lambda n, b, h: grs[b][h:h + 1, :])
        lf_r = each(lambda n, b, h: grs[b][N_HEADS + h:N_HEADS + h + 1, :])
        m_prev = each(lambda n, b, h: m_s[b, h:h + 1, :])

        b_c = each(lambda n, b, h: jnp.sum(jnp.where(causal, lf_r[n], 0.0), axis=1, keepdims=True))
        b_r = each(lambda n, b, h: jnp.sum(jnp.where(ti <= ji, lf_c[n], 0.0), axis=0, keepdims=True))
        dmat = each(lambda n, b, h: jnp.where(causal, b_c[n] - b_r[n] + li_r[n], -jnp.inf))
        dmax = each(lambda n, b, h: jnp.max(dmat[n], axis=1, keepdims=True))
        inter = each(lambda n, b, h: b_c[n] + m_prev[n])
        m_t = each(lambda n, b, h: jnp.maximum(inter[n], dmax[n]))
        w_inter = each(lambda n, b, h: jnp.exp(inter[n] - m_t[n]))
        s = each(lambda n, b, h: _dot_nt(q(b, h), k(b, h)) * jnp.exp(dmat[n] - m_t[n]))
        qc = each(lambda n, b, h: _dot(q(b, h), c_s[b, h].astype(BF16)))
        sv = each(lambda n, b, h: _dot(s[n].astype(BF16), v(b, h)))
        qn = each(lambda n, b, h: jnp.sum(q(b, h).astype(F32) * n_s[b, h:h + 1, :], axis=1,
                                          keepdims=True))
        den = each(lambda n, b, h: w_inter[n] * qn[n] + jnp.sum(s[n], axis=1, keepdims=True))
        hh = each(lambda n, b, h: (w_inter[n] * qc[n] + sv[n])
                  / jnp.maximum(jnp.abs(den[n]), jnp.exp(-m_t[n])))

        m_new = each(lambda n, b, h: m_t[n][chunk - 1:chunk, :])
        b_last = each(lambda n, b, h: b_c[n][chunk - 1:chunk, :])
        decay = each(lambda n, b, h: jnp.exp(b_last[n] + m_prev[n] - m_new[n]))
        kw = each(lambda n, b, h: k(b, h).astype(F32)
                  * jnp.exp(b_last[n] - b_c[n] + li_c[n] - m_new[n]))
        kv = each(lambda n, b, h: _dot_tn(kw[n].astype(BF16), v(b, h)))
        for n, (b, h) in enumerate(chains):
            c_s[b, h] = decay[n] * c_s[b, h] + kv[n]
            n_s[b, h:h + 1, :] = decay[n] * n_s[b, h:h + 1, :] + jnp.sum(kw[n], axis=0, keepdims=True)
            m_s[b, h:h + 1, :] = m_new[n]

        hn = each(lambda n, b, h: hh[n] * lax.rsqrt(jnp.mean(hh[n] * hh[n], axis=1, keepdims=True) + EPS)
                  * ng_ref[:, cols(h)])
        for n, (b, h) in enumerate(chains):
            ym_ref[b, rows, cols(h)] = (hn[n] * og_ref[b, rows, cols(h)]).astype(BF16)
        return carry

    lax.fori_loop(0, nchunks, body, 0)

    @pl.when(j == pl.num_programs(0) - 1)
    def _():
        c1_ref[...] = c_s[...]
        n1_ref[...] = n_s[...]
        m1_ref[...] = m_s[...]


def _mlstm(q, k, v, og, gcol, grow, c0, n0, m0, ng, *, ct, chunk):
    bsz, seq, _ = q.shape
    nchunks = ct // chunk
    grid = (seq // ct,)
    tok = lambda c: pl.BlockSpec((bsz, ct, c), lambda j: (0, j, 0))
    st_c = pl.BlockSpec((bsz, N_HEADS, HEAD_DIM, HEAD_DIM), lambda j: (0, 0, 0, 0))
    st_n = pl.BlockSpec((bsz, N_HEADS, HEAD_DIM), lambda j: (0, 0, 0))
    st_m = pl.BlockSpec((bsz, N_HEADS, 1), lambda j: (0, 0, 0))
    return pl.pallas_call(
        functools.partial(_mlstm_kernel, chunk=chunk, nchunks=nchunks, bsz=bsz),
        grid=grid,
        in_specs=[tok(D_MLSTM), tok(D_MLSTM), tok(D_MLSTM), tok(D_MLSTM), tok(8),
                  pl.BlockSpec((bsz, nchunks, 8, chunk), lambda j: (0, j, 0, 0)),
                  st_c, st_n, st_m,
                  pl.BlockSpec((1, D_MLSTM), lambda j: (0, 0))],
        out_specs=[tok(D_MLSTM), st_c, st_n, st_m],
        out_shape=[jax.ShapeDtypeStruct((bsz, seq, D_MLSTM), BF16),
                   jax.ShapeDtypeStruct((bsz, N_HEADS, HEAD_DIM, HEAD_DIM), F32),
                   jax.ShapeDtypeStruct((bsz, N_HEADS, HEAD_DIM), F32),
                   jax.ShapeDtypeStruct((bsz, N_HEADS, 1), F32)],
        scratch_shapes=[pltpu.VMEM((bsz, N_HEADS, HEAD_DIM, HEAD_DIM), F32),
                        pltpu.VMEM((bsz, N_HEADS, HEAD_DIM), F32),
                        pltpu.VMEM((bsz, N_HEADS, 1), F32)],
        compiler_params=_cparams(("arbitrary",)),
        name="mlstm",
    )(q, k, v, og, gcol, grow, c0, n0, m0, ng)


def _post_kernel(ym_ref, yc_ref, x_ref, wmo_ref, gx_ref, wxq_ref, mk_ref, mv_ref, wxo_ref,
                 gf_ref, wrt_ref, br_ref,
                 h2_ref, xn_ref, eid_ref, gate_ref, rank_ref, cnt_ref, cnt_s, *, tm, sub, group):
    step = pl.program_id(0) * pl.num_programs(1) + pl.program_id(1)

    @pl.when(step % group == 0)
    def _():
        cnt_s[...] = jnp.zeros_like(cnt_s)

    n_mem = mk_ref.shape[0]
    per = tm // n_mem
    n_groups = max(1, tm // POST_GROUP_ROWS) if n_mem == 1 else 1
    rows = [slice(g * (tm // n_groups), (g + 1) * (tm // n_groups)) for g in range(n_groups)]
    each = lambda f: [f(g, r) for g, r in enumerate(rows)]

    mix = each(lambda g, r: _dot(ym_ref[r, :], wmo_ref[0:D_MLSTM, :])
               + _dot(yc_ref[r, :], wmo_ref[D_MLSTM:D_MODEL, :]))
    h1 = each(lambda g, r: x_ref[r, :] + mix[g])
    xq = each(lambda g, r: _dot(_rms(h1[g], gx_ref[...]).astype(BF16), wxq_ref[...]).astype(BF16))

    def streams(g):
        if n_mem == 1:
            return [(slice(None), 0)]
        return [(slice(m * per, (m + 1) * per), m) for m in range(n_mem)]

    att = each(lambda g, r: [jnp.zeros((xq[g][q, :].shape[0], D_MODEL), F32) for q, _ in streams(g)])
    for hd in range(N_XHEADS):
        cols = slice(hd * XHEAD_DIM, (hd + 1) * XHEAD_DIM)
        s = each(lambda g, r: [_dot_nt(xq[g][q, cols], mk_ref[m, :, cols]) * (XHEAD_DIM ** -0.5)
                               for q, m in streams(g)])
        e = each(lambda g, r: [jnp.exp(v - jnp.max(v, axis=-1, keepdims=True)) for v in s[g]])
        p = each(lambda g, r: [(v / jnp.sum(v, axis=-1, keepdims=True)).astype(BF16) for v in e[g]])
        o = each(lambda g, r: [_dot(v, mv_ref[m, :, cols]).astype(BF16)
                               for v, (_, m) in zip(p[g], streams(g))])
        att = each(lambda g, r: [a + _dot(v, wxo_ref[cols, :]) for a, v in zip(att[g], o[g])])
    h2 = each(lambda g, r: h1[g] + jnp.concatenate(att[g], axis=0))
    xn2_parts = each(lambda g, r: _rms(h2[g], gf_ref[...]).astype(BF16))
    for g, r in enumerate(rows):
        h2_ref[r, :] = h2[g]
        xn_ref[r, :] = xn2_parts[g]
    xn2 = jnp.concatenate(xn2_parts, axis=0)

    logits = _dot_nt(wrt_ref[...], xn2) + br_ref[...]
    eidx = lax.broadcasted_iota(I32, logits.shape, 0).astype(F32)
    work = logits
    vals, ids, hots = [], [], []
    for _ in range(TOP_K):
        mx = jnp.max(work, axis=0, keepdims=True)
        idx = jnp.min(jnp.where(work == mx, eidx, float(N_EXPERTS)), axis=0, keepdims=True)
        sel = eidx == idx
        vals.append(mx)
        ids.append(idx)
        hots.append(sel)
        work = jnp.where(sel, -jnp.inf, work)
    exps = [jnp.exp(v - vals[0]) for v in vals]
    denom = exps[0] + exps[1] + exps[2] + exps[3]

    picked = jnp.zeros(logits.shape, F32)
    for sel in hots:
        picked = picked + sel.astype(F32)
    shift = jnp.full((tm, tm), sub.bit_length() - 1, I32)
    tj = lax.broadcasted_iota(I32, (tm, tm), 0)
    tt = lax.broadcasted_iota(I32, (tm, tm), 1)
    same = lax.shift_right_logical(tj, shift) == lax.shift_right_logical(tt, shift)
    before = jnp.where(jnp.logical_and(tj < tt, same), 1.0, 0.0).astype(BF16)
    prior = _dot(picked.astype(BF16), before) + cnt_s[...]
    for kk in range(TOP_K):
        eid_ref[kk:kk + 1, :] = ids[kk].astype(I32)
        gate_ref[kk:kk + 1, :] = exps[kk] / denom
        rank_ref[kk:kk + 1, :] = jnp.sum(jnp.where(hots[kk], prior, 0.0), axis=0,
                                         keepdims=True).astype(I32)
    for s in range(tm // sub):
        total = cnt_s[...] + jnp.sum(picked[:, s * sub:(s + 1) * sub], axis=1, keepdims=True)
        cnt_ref[s] = total.astype(I32)
    cnt_s[...] = total


def _post(ym, yc, x, wmo, gx, wxq, mkb, mvb, wxo, gf, wrt, br, *, tm, sub, group):
    bsz, seq, _ = x.shape
    nj = seq // tm
    grid = (bsz, nj)
    nsub = tm // sub
    n_tiles = bsz * nj * nsub // group
    tok = lambda c: pl.BlockSpec((None, tm, c), lambda b, j: (b, j, 0))
    full = lambda shape: pl.BlockSpec(shape, lambda b, j: (0,) * len(shape))
    n_mem = mkb.shape[0] // bsz
    mem = pl.BlockSpec((n_mem, N_MEM, D_MODEL), lambda b, j: (b, 0, 0))
    sel = pl.BlockSpec((None, TOP_K, tm), lambda b, j: (b, 0, j))
    return pl.pallas_call(
        functools.partial(_post_kernel, tm=tm, sub=sub, group=group),
        grid=grid,
        in_specs=[tok(D_MLSTM), tok(D_CONV), tok(D_MODEL), full((D_MODEL, D_MODEL)),
                  full((1, D_MODEL)), full((D_MODEL, D_MODEL)), mem, mem,
                  full((D_MODEL, D_MODEL)), full((1, D_MODEL)), full((N_EXPERTS, D_MODEL)),
                  full((N_EXPERTS, 1))],
        out_specs=[tok(D_MODEL), tok(D_MODEL), sel, sel, sel,
                   pl.BlockSpec((nsub, N_EXPERTS, 1), lambda b, j: ((b * nj + j) // group, 0, 0))],
        out_shape=[jax.ShapeDtypeStruct((bsz, seq, D_MODEL), F32),
                   jax.ShapeDtypeStruct((bsz, seq, D_MODEL), BF16),
                   jax.ShapeDtypeStruct((bsz, TOP_K, seq), I32),
                   jax.ShapeDtypeStruct((bsz, TOP_K, seq), F32),
                   jax.ShapeDtypeStruct((bsz, TOP_K, seq), I32),
                   jax.ShapeDtypeStruct((n_tiles, N_EXPERTS, 1), I32)],
        scratch_shapes=[pltpu.VMEM((N_EXPERTS, 1), F32)],
        compiler_params=_cparams(("arbitrary", "arbitrary")),
        name="post",
    )(ym, yc, x, wmo, gx, wxq, mkb, mvb, wxo, gf, wrt, br)


def _sorted_rows(n_tokens):
    return -(-(TOP_K * n_tokens + N_EXPERTS * (SUBLANES - 1)) // ROW_BLOCK) * ROW_BLOCK


def _dispatch_kernel(tab_ref, fill_ref, eid_ref, rank_ref, ls_ref, x_ref, eids_ref, ranks_ref, xs_ref,
                     xb_ref, srt, zero_s, pending, sems, *, n_ptiles, n_blocks):
    i = pl.program_id(0)
    cur = i % 2

    def rows_copy(src, src_row, dst_row, nrows, sl):
        return pltpu.make_async_copy(
            src.at[pl.ds(pl.multiple_of(src_row, SUBLANES), nrows), :],
            xb_ref.at[pl.ds(pl.multiple_of(dst_row, SUBLANES), nrows), :], sems.at[sl])

    def granule(src, src_row, dst_row, sl):
        return rows_copy(src, src_row, dst_row, SUBLANES, sl)

    def drain(count, sl, nrows=SUBLANES):
        def body(g, c):
            rows_copy(zero_s, 0, 0, nrows, sl).wait()
            return c
        lax.fori_loop(0, count, body, 0)

    @pl.when(i == 0)
    def _():
        pending[0] = 0
        pending[1] = 0

    def issue_list(count, dst_at, src_at, nrows):
        def body(h, c):
            g = 2 * h
            rows_copy(srt.at[cur], tab_ref[src_at + g], tab_ref[dst_at + g], nrows, cur).start(priority=0)

            @pl.when(g + 1 < count)
            def _():
                rows_copy(srt.at[cur], tab_ref[src_at + g + 1], tab_ref[dst_at + g + 1], nrows,
                          cur).start(priority=1)
            return c
        lax.fori_loop(0, (count + 1) // 2, body, 0)

    def sort_and_move(eid, rank, x):
        ntok = x.shape[0]
        nrows = _sorted_rows(ntok)
        e_iota = lax.broadcasted_iota(I32, (N_EXPERTS, ntok), 0)
        s_iota = lax.broadcasted_iota(I32, (nrows, ntok), 0)
        seg_start = ls_ref[...]
        hit = None
        for kk in range(TOP_K):
            start = jnp.sum(jnp.where(e_iota == eid[kk:kk + 1, :], seg_start, 0.0),
                            axis=0, keepdims=True).astype(I32)
            match = s_iota == start + rank[kk:kk + 1, :]
            hit = match if hit is None else jnp.logical_or(hit, match)
        perm = jnp.where(hit, 1.0, 0.0).astype(BF16)
        srt[cur, 0:nrows, :] = _dot(perm, x)
        drain(pending[0], 1 - cur, BIG_ROWS)
        drain(pending[1], 1 - cur)
        n_big, n_small = tab_ref[TAB_COUNTS], tab_ref[TAB_COUNTS + 1]
        issue_list(n_big, TAB_BIG_DST, TAB_BIG_SRC, BIG_ROWS)
        issue_list(n_small, TAB_SMALL_DST, TAB_SMALL_SRC, SUBLANES)
        pending[0] = n_big
        pending[1] = n_small

    @pl.when(i < n_ptiles)
    def _():
        sort_and_move(eid_ref[...], rank_ref[...], x_ref[...])

    @pl.when(i == n_ptiles)
    def _():
        sort_and_move(eids_ref[...], ranks_ref[...], xs_ref[...])
        drain(pending[0], cur, BIG_ROWS)
        drain(pending[1], cur)
        zero_s[...] = jnp.zeros_like(zero_s)
        for e in range(N_EXPERTS):
            dst = fill_ref[e]

            def zissue(g, c):
                granule(zero_s, 0, dst + g * SUBLANES, cur).start()
                return c
            lax.fori_loop(0, fill_ref[N_EXPERTS + e], zissue, 0)
        drain(fill_ref[2 * N_EXPERTS], cur)
        first_free = fill_ref[2 * N_EXPERTS + 1]

        def blk_copy(b):
            return pltpu.make_async_copy(
                zero_s, xb_ref.at[pl.ds(pl.multiple_of(b * ROW_BLOCK, ROW_BLOCK), ROW_BLOCK), :],
                sems.at[cur])

        def bissue(b, c):
            blk_copy(b).start()
            return c
        lax.fori_loop(first_free, n_blocks, bissue, 0)

        def bdrain(b, c):
            blk_copy(0).wait()
            return c
        lax.fori_loop(first_free, n_blocks, bdrain, 0)


def _dispatch(tab, fill, eid_p, rank_p, seg_start, xn_p, eid_s, rank_s, xn_s, *, tmd, n_blocks):
    n_ptiles = eid_p.shape[0]
    n_sample = xn_s.shape[0]
    last = n_ptiles - 1
    smem = lambda shape, imap: pl.BlockSpec(shape, imap, memory_space=pltpu.SMEM)
    return pl.pallas_call(
        functools.partial(_dispatch_kernel, n_ptiles=n_ptiles, n_blocks=n_blocks),
        grid=(n_ptiles + 1,),
        in_specs=[smem((TABLE_WIDTH,), lambda i: (i,)),
                  smem((TABLE_WIDTH,), lambda i: (0,)),
                  pl.BlockSpec((None, TOP_K, tmd), lambda i: (jnp.minimum(i, last), 0, 0)),
                  pl.BlockSpec((None, TOP_K, tmd), lambda i: (jnp.minimum(i, last), 0, 0)),
                  pl.BlockSpec((None, N_EXPERTS, 1), lambda i: (i, 0, 0)),
                  pl.BlockSpec((tmd, D_MODEL), lambda i: (jnp.minimum(i, last), 0)),
                  pl.BlockSpec((TOP_K, n_sample), lambda i: (0, 0)),
                  pl.BlockSpec((TOP_K, n_sample), lambda i: (0, 0)),
                  pl.BlockSpec((n_sample, D_MODEL), lambda i: (0, 0))],
        out_specs=pl.BlockSpec(memory_space=pl.ANY),
        out_shape=jax.ShapeDtypeStruct((n_blocks * ROW_BLOCK, D_MODEL), F32),
        scratch_shapes=[pltpu.VMEM((2, _sorted_rows(tmd), D_MODEL), F32),
                        pltpu.VMEM((ROW_BLOCK, D_MODEL), F32), pltpu.SMEM((2,), I32),
                        pltpu.SemaphoreType.DMA((2,))],
        compiler_params=_cparams(("arbitrary",)),
        name="dispatch",
    )(tab, fill, eid_p, rank_p, seg_start, xn_p, eid_s, rank_s, xn_s)


def _expert_kernel(tab_ref, xb_ref, wu_hbm, bu_ref, wd_hbm, bd_ref, yb_ref,
                   wu_f, wd_f, wu_s, wd_s, xbuf, ybuf, wsem, xsem, ysem, *, n_blocks):
    e = pl.program_id(0)
    first = tab_ref[e]
    nblk = tab_ref[N_EXPERTS + e]
    n_used = tab_ref[2 * N_EXPERTS]

    def hbm_rows(b):
        return pl.ds(pl.multiple_of(b * ROW_BLOCK, ROW_BLOCK), ROW_BLOCK)

    def x_copy(b, sl):
        return pltpu.make_async_copy(xb_ref.at[hbm_rows(b), :], xbuf.at[sl], xsem.at[sl])

    def y_copy(b, sl):
        return pltpu.make_async_copy(ybuf.at[sl], yb_ref.at[hbm_rows(b), :], ysem.at[sl])

    def w_copies(ex, sl):
        return (pltpu.make_async_copy(wu_hbm.at[ex], wu_f.at[sl], wsem.at[0, sl]),
                pltpu.make_async_copy(wd_hbm.at[ex], wd_f.at[sl], wsem.at[1, sl]))

    wslot = e % 2

    @pl.when(e == 0)
    def _():
        for c in w_copies(0, 0):
            c.start()
        for b in range(ROW_LOOKAHEAD):
            @pl.when(b < n_used)
            def _():
                x_copy(b, b % ROW_RING).start()

    @pl.when(e + 1 < N_EXPERTS)
    def _():
        for c in w_copies(e + 1, 1 - wslot):
            c.start()

    for c in w_copies(e, wslot):
        c.wait()

    step = 128

    def cast(r, c):
        rows = pl.ds(pl.multiple_of(r * step, step), step)
        wu_s[rows, :] = wu_f[wslot, rows, :].astype(BF16)
        wd_s[rows, :] = wd_f[wslot, rows, :].astype(BF16)
        return c
    lax.fori_loop(0, D_MODEL // step, cast, 0)

    def block(b, c):
        sl = b % ROW_RING
        x_copy(b, sl).wait()

        @pl.when(b + ROW_LOOKAHEAD < n_used)
        def _():
            x_copy(b + ROW_LOOKAHEAD, (b + ROW_LOOKAHEAD) % ROW_RING).start()

        @pl.when(b >= ROW_RING)
        def _():
            y_copy(b, sl).wait()

        hcat = _dot(xbuf[sl].astype(BF16), wu_s[...]) + bu_ref[...]
        glu = jnp.minimum(hcat[:, 0:D_FF], SWIGLU_LIMIT)
        lin = jnp.clip(hcat[:, D_FF:2 * D_FF], -SWIGLU_LIMIT, SWIGLU_LIMIT)
        act = glu * _sigmoid(SWIGLU_ALPHA * glu) * (lin + 1.0)
        ybuf[sl] = _dot(act.astype(BF16), wd_s[...]) + bd_ref[...]
        y_copy(b, sl).start()
        return c
    lax.fori_loop(first, first + nblk, block, 0)

    @pl.when(e == N_EXPERTS - 1)
    def _():
        for k in range(ROW_RING):
            @pl.when(k < n_used)
            def _():
                y_copy(0, (n_used - 1 - k) % ROW_RING).wait()

        ybuf[0] = jnp.zeros((ROW_BLOCK, D_MODEL), F32)

        def zissue(b, c):
            y_copy(b, 0).start()
            return c
        lax.fori_loop(n_used, n_blocks, zissue, 0)

        def zdrain(b, c):
            y_copy(0, 0).wait()
            return c
        lax.fori_loop(n_used, n_blocks, zdrain, 0)


def _experts(tab, xb, w_up, b_up, w_down, b_down, *, n_blocks):
    w_map = lambda e, tab: (e, 0, 0)
    grid_spec = pltpu.PrefetchScalarGridSpec(
        num_scalar_prefetch=1,
        grid=(N_EXPERTS,),
        in_specs=[pl.BlockSpec(memory_space=pl.ANY),
                  pl.BlockSpec(memory_space=pl.ANY),
                  pl.BlockSpec((None, 1, 2 * D_FF), w_map),
                  pl.BlockSpec(memory_space=pl.ANY),
                  pl.BlockSpec((None, 1, D_MODEL), w_map)],
        out_specs=pl.BlockSpec(memory_space=pl.ANY),
        scratch_shapes=[pltpu.VMEM((2, D_MODEL, 2 * D_FF), F32), pltpu.VMEM((2, D_FF, D_MODEL), F32),
                        pltpu.VMEM((D_MODEL, 2 * D_FF), BF16), pltpu.VMEM((D_FF, D_MODEL), BF16),
                        pltpu.VMEM((ROW_RING, ROW_BLOCK, D_MODEL), F32),
                        pltpu.VMEM((ROW_RING, ROW_BLOCK, D_MODEL), F32),
                        pltpu.SemaphoreType.DMA((2, 2)),
                        pltpu.SemaphoreType.DMA((ROW_RING,)), pltpu.SemaphoreType.DMA((ROW_RING,))],
    )
    return pl.pallas_call(
        functools.partial(_expert_kernel, n_blocks=n_blocks),
        grid_spec=grid_spec,
        out_shape=jax.ShapeDtypeStruct((n_blocks * ROW_BLOCK, D_MODEL), F32),
        compiler_params=_cparams(("arbitrary",)),
        name="experts",
    )(tab, xb, w_up, b_up, w_down, b_down)


def _split_bf16(a):
    hi = a.astype(BF16)
    return hi, (a - hi.astype(F32)).astype(BF16)


def _combine_kernel(tab_ref, nxt_ref, slot_ref, gate_ref, h2_ref, gfin_ref, yb_ref, y_ref, buf, sems,
                    *, tm):
    i = pl.program_id(0)
    n = pl.num_programs(0)
    cur = i % 2
    nrows = buf.shape[1]

    def rows_copy(src_row, sl, dst_row, nrows):
        return pltpu.make_async_copy(
            yb_ref.at[pl.ds(pl.multiple_of(src_row, SUBLANES), nrows), :],
            buf.at[sl, pl.ds(pl.multiple_of(dst_row, SUBLANES), nrows), :], sems.at[sl])

    def gather(t_ref, sl):
        def fetch(count, yb_at, tile_at, nrows):
            def body(h, c):
                g = 2 * h
                rows_copy(t_ref[yb_at + g], sl, t_ref[tile_at + g], nrows).start(priority=0)

                @pl.when(g + 1 < count)
                def _():
                    rows_copy(t_ref[yb_at + g + 1], sl, t_ref[tile_at + g + 1], nrows).start(priority=1)
                return c
            lax.fori_loop(0, (count + 1) // 2, body, 0)
        fetch(t_ref[TAB_COUNTS], TAB_BIG_DST, TAB_BIG_SRC, BIG_ROWS)
        fetch(t_ref[TAB_COUNTS + 1], TAB_SMALL_DST, TAB_SMALL_SRC, SUBLANES)

    @pl.when(i == 0)
    def _():
        buf[...] = jnp.zeros_like(buf)
        gather(tab_ref, 0)

    @pl.when(i + 1 < n)
    def _():
        gather(nxt_ref, 1 - cur)

    def drain(count, nrows):
        def body(g, c):
            rows_copy(0, cur, 0, nrows).wait()
            return c
        lax.fori_loop(0, count, body, 0)
    drain(tab_ref[TAB_COUNTS], BIG_ROWS)
    drain(tab_ref[TAB_COUNTS + 1], SUBLANES)

    s_iota = lax.broadcasted_iota(I32, (tm, nrows), 1)
    wgt = jnp.zeros((tm, nrows), F32)
    for kk in range(TOP_K):
        wgt = wgt + jnp.where(s_iota == slot_ref[:, kk:kk + 1], gate_ref[:, kk:kk + 1], 0.0)
    w_hi, w_lo = _split_bf16(wgt)
    rows = buf[cur].astype(BF16)
    acc = _dot(w_hi, rows) + _dot(w_lo, rows)
    y_ref[...] = _rms(h2_ref[...] + acc, gfin_ref[...])


def _combine(tab, slot_col, gate_col, h2, gfin, yb, *, tm):
    ntok = h2.shape[0]
    n = ntok // tm
    smem = lambda imap: pl.BlockSpec((TABLE_WIDTH,), imap, memory_space=pltpu.SMEM)
    return pl.pallas_call(
        functools.partial(_combine_kernel, tm=tm),
        grid=(n,),
        in_specs=[smem(lambda i: (i,)), smem(lambda i: (jnp.minimum(i + 1, n - 1),)),
                  pl.BlockSpec((tm, TOP_K), lambda i: (i, 0)),
                  pl.BlockSpec((tm, TOP_K), lambda i: (i, 0)),
                  pl.BlockSpec((tm, D_MODEL), lambda i: (i, 0)),
                  pl.BlockSpec((1, D_MODEL), lambda i: (0, 0)),
                  pl.BlockSpec(memory_space=pl.ANY)],
        out_specs=pl.BlockSpec((tm, D_MODEL), lambda i: (i, 0)),
        out_shape=jax.ShapeDtypeStruct((ntok, D_MODEL), F32),
        scratch_shapes=[pltpu.VMEM((2, _sorted_rows(tm), D_MODEL), F32),
                        pltpu.SemaphoreType.DMA((2,))],
        compiler_params=_cparams(("arbitrary",)),
        name="combine",
    )(tab, tab, slot_col, gate_col, h2, gfin, yb)


def _copy_tables(seg, seg_src, seg_dst):
    experts = jnp.arange(N_EXPERTS, dtype=I32)

    def flatten(counts, bound, src0, dst0, step):
        ends = jnp.cumsum(counts, axis=1)
        idx = jnp.arange(bound, dtype=I32)
        owner = jnp.sum(ends[:, None, :] <= idx[None, :, None], axis=-1)
        pick = lambda a: jnp.sum(jnp.where(owner[..., None] == experts, a[:, None, :], 0), axis=-1)
        off = (idx[None, :] - pick(ends - counts)) * step
        valid = idx[None, :] < ends[:, -1:]
        return (jnp.where(valid, pick(dst0) + off, 0), jnp.where(valid, pick(src0) + off, 0),
                ends[:, -1:])

    n_big = seg // BIG_ROWS
    n_small = (seg - n_big * BIG_ROWS) // SUBLANES
    b_dst, b_src, b_n = flatten(n_big, MAX_BIG, seg_src, seg_dst, BIG_ROWS)
    s_dst, s_src, s_n = flatten(n_small, MAX_SMALL, seg_src + n_big * BIG_ROWS,
                                seg_dst + n_big * BIG_ROWS, SUBLANES)
    tab = jnp.concatenate([b_dst, b_src, s_dst, s_src, b_n, s_n], axis=1)
    return jnp.pad(tab, ((0, 0), (0, TABLE_WIDTH - tab.shape[1]))).astype(I32).reshape(-1)


def _tiles(a, tile):
    bsz, kk, seq = a.shape
    return a.reshape(bsz, kk, seq // tile, tile).transpose(0, 2, 1, 3).reshape(-1, kk, tile)


def _path(x, cbuf, c0, n0, m0, mkb, mvb, wts, *, tm_in, ct, chunk, tm_post, sub, fold):
    bsz, seq, _ = x.shape
    if fold:
        x = x.reshape(1, bsz * seq, D_MODEL)
        q, k, v, og, gcol, grow, yc, nbuf = _inproj(
            x, wts["g_mix"], wts["wq"], wts["wg"], wts["wgt"], wts["wc"], wts["bg"], wts["bgt"],
            wts["cw"], cbuf, tm=bsz * seq, chunk=bsz * seq)
        q, k, v, og, gcol = (a.reshape(bsz, seq, a.shape[-1]) for a in (q, k, v, og, gcol))
        grow = grow.reshape(2 * N_HEADS, bsz, seq).transpose(1, 0, 2)[:, None]
    else:
        q, k, v, og, gcol, grow, yc, nbuf = _inproj(
            x, wts["g_mix"], wts["wq"], wts["wg"], wts["wgt"], wts["wc"], wts["bg"], wts["bgt"],
            wts["cw"], cbuf, tm=tm_in, chunk=chunk)
    ym, c1, n1, m1 = _mlstm(q, k, v, og, gcol, grow, c0, n0, m0, wts["ng"], ct=ct, chunk=chunk)
    if fold:
        ym = ym.reshape(1, bsz * seq, D_MLSTM)
    h2, xn, eid, gate, rank, cnt = _post(
        ym, yc, x, wts["wmo"], wts["g_x"], wts["wxq"], mkb, mvb, wts["wxo"], wts["g_ffn"],
        wts["wrt"], wts["br"], tm=tm_post, sub=sub, group=1)
    return dict(h2=h2, xn=xn, eid=eid, gate=gate, rank=rank, cnt=cnt[:, :, 0],
                c1=c1, n1=n1, m1=m1[..., 0], nbuf=nbuf)


def kernel(x_prompt, x_sample, state_mlstm_c, state_mlstm_n, state_mlstm_m, state_conv, cache_mem_k, cache_mem_v, mem_prompt, norm_mix_g, w_in, b_gate, mlstm_norm_g, conv_w, w_mix_out, norm_x_g, norm_mem_g, w_xq, w_xk, w_xv, w_xo, norm_ffn_g, w_router, b_router, w_up, b_up, w_down, b_down, norm_final_g):
    bp, lp, _ = x_prompt.shape
    bs, ls, _ = x_sample.shape
    l = 0
    row = lambda a: a.reshape(1, -1)

    wi = w_in[l]
    gate_cols = wi[:, 4 * D_MLSTM:4 * D_MLSTM + 2 * N_HEADS]
    wts = dict(
        g_mix=row(norm_mix_g[l]),
        wq=wi[:, 0:4 * D_MLSTM].astype(BF16),
        wg=jnp.pad(gate_cols, ((0, 0), (0, 128 - 2 * N_HEADS))).astype(BF16),
        wgt=gate_cols.T.astype(BF16),
        wc=wi[:, 4 * D_MLSTM + 2 * N_HEADS:].astype(BF16),
        bg=row(b_gate[l]), bgt=b_gate[l].reshape(-1, 1),
        cw=conv_w[l], ng=row(mlstm_norm_g[l]),
        wmo=w_mix_out[l].astype(BF16), g_x=row(norm_x_g[l]), wxq=w_xq[l].astype(BF16),
        wxo=w_xo[l].astype(BF16), g_ffn=row(norm_ffn_g[l]),
        wrt=w_router[l].T.astype(BF16), br=b_router[l].reshape(-1, 1),
    )

    mk, mv, mkb, mvb = _memkv(mem_prompt.reshape(bp * N_MEM, D_MODEL), row(norm_mem_g[l]),
                              w_xk[l].astype(BF16), w_xv[l].astype(BF16))
    zeros = lambda *s: jnp.zeros(s, F32)
    tm_post, tmd = 512, 256
    assert _sorted_rows(tmd) // BIG_ROWS <= MAX_BIG
    pr = _path(x_prompt, zeros(bp, CONV_WIDTH - 1, D_CONV), zeros(bp, N_HEADS, HEAD_DIM, HEAD_DIM),
               zeros(bp, N_HEADS, HEAD_DIM), zeros(bp, N_HEADS, 1),
               mkb.reshape(bp, N_MEM, D_MODEL), mvb.reshape(bp, N_MEM, D_MODEL), wts,
               tm_in=512, ct=512, chunk=CHUNK, tm_post=tm_post, sub=tmd, fold=False)
    sa = _path(x_sample, state_conv[l], state_mlstm_c[l], state_mlstm_n[l],
               state_mlstm_m[l][..., None],
               cache_mem_k[l].reshape(bs, N_MEM, D_MODEL).astype(BF16),
               cache_mem_v[l].reshape(bs, N_MEM, D_MODEL).astype(BF16), wts,
               tm_in=ls, ct=ls, chunk=min(CHUNK, ls), tm_post=bs * ls, sub=bs * ls, fold=True)

    n_ptok, n_stok = bp * lp, bs * ls
    cnt = jnp.concatenate([pr["cnt"], sa["cnt"]], axis=0)
    n_tiles = cnt.shape[0]
    n_blocks = -(-(TOP_K * (n_ptok + n_stok) + n_tiles * N_EXPERTS * (SUBLANES - 1)) // ROW_BLOCK) \
        + N_EXPERTS
    seg = (cnt + SUBLANES - 1) // SUBLANES * SUBLANES
    seg_src = jnp.cumsum(seg, axis=1) - seg
    tot = jnp.sum(seg, axis=0)
    blocks_e = (tot + ROW_BLOCK - 1) // ROW_BLOCK
    padded = blocks_e * ROW_BLOCK
    pstart = jnp.cumsum(padded) - padded
    bend = jnp.cumsum(blocks_e)
    n_used = bend[-1]
    exp_tab = jnp.concatenate([bend - blocks_e, blocks_e, n_used[None]]).astype(I32)
    seg_dst = pstart[None, :] + jnp.cumsum(seg, axis=0) - seg

    tab = _copy_tables(seg, seg_src, seg_dst)
    fgran = (padded - tot) // SUBLANES
    fill = jnp.concatenate([pstart + tot, fgran, jnp.sum(fgran)[None], n_used[None]])
    fill = jnp.pad(fill, (0, TABLE_WIDTH - fill.shape[0])).astype(I32)

    def lookup(table, eid):
        hit = eid[..., None] == jnp.arange(N_EXPERTS, dtype=I32)
        return jnp.sum(jnp.where(hit, table[:, None, None, :], 0), axis=-1)

    eid_pt, rank_pt = _tiles(pr["eid"], tmd), _tiles(pr["rank"], tmd)
    eid_st = sa["eid"].transpose(1, 0, 2).reshape(1, TOP_K, n_stok)
    rank_st = sa["rank"].transpose(1, 0, 2).reshape(1, TOP_K, n_stok)
    slot_p = (lookup(seg_src[:-1], eid_pt) + rank_pt).transpose(0, 2, 1).reshape(n_ptok, TOP_K)
    slot_s = (lookup(seg_src[-1:], eid_st) + rank_st).transpose(0, 2, 1).reshape(n_stok, TOP_K)

    xb = _dispatch(tab, fill, eid_pt, rank_pt, seg_src.astype(F32)[..., None],
                   pr["xn"].reshape(n_ptok, D_MODEL), eid_st[0], rank_st[0],
                   sa["xn"].reshape(n_stok, D_MODEL), tmd=tmd, n_blocks=n_blocks)
    yb = _experts(exp_tab, xb, w_up[l], b_up[l][:, None, :], w_down[l], b_down[l][:, None, :],
                  n_blocks=n_blocks)

    gfin = row(norm_final_g)
    split = (n_tiles - 1) * TABLE_WIDTH
    y_p = _combine(tab[:split], slot_p.astype(I32), pr["gate"].transpose(0, 2, 1).reshape(n_ptok, TOP_K),
                   pr["h2"].reshape(n_ptok, D_MODEL), gfin, yb, tm=tmd)
    y_s = _combine(tab[split:], slot_s.astype(I32), sa["gate"].transpose(0, 2, 1).reshape(n_stok, TOP_K),
                   sa["h2"].reshape(n_stok, D_MODEL), gfin, yb, tm=n_stok)

    lead = lambda a: a[None]
    return (y_p.reshape(bp, lp, D_MODEL), y_s.reshape(bs, ls, D_MODEL),
            lead(pr["c1"]), lead(pr["n1"]), lead(pr["m1"]), lead(pr["nbuf"]),
            lead(mk.reshape(bp, N_MEM, N_XHEADS, XHEAD_DIM)),
            lead(mv.reshape(bp, N_MEM, N_XHEADS, XHEAD_DIM)),
            lead(sa["c1"]), lead(sa["n1"]), lead(sa["m1"]), lead(sa["nbuf"]))
```

```python
import functools

import jax
import jax.numpy as jnp
from jax import lax
from jax.experimental import pallas as pl
from jax.experimental.pallas import tpu as pltpu

F32 = jnp.float32
BF16 = jnp.bfloat16
I32 = jnp.int32

D_MODEL = 1024
N_HEADS = 4
HEAD_DIM = 128
D_MLSTM = N_HEADS * HEAD_DIM
D_CONV = D_MODEL - D_MLSTM
CONV_WIDTH = 3
CHUNK = 64
N_MEM = 256
N_XHEADS = 4
XHEAD_DIM = D_MODEL // N_XHEADS
N_EXPERTS = 32
TOP_K = 4
D_FF = D_MODEL
SWIGLU_LIMIT = 7.0
SWIGLU_ALPHA = 1.702
EPS = 1e-5

SUBLANES = 8
BIG_ROWS = 4 * SUBLANES
COPIES_PER_TRIP = 4
MAX_BIG, MAX_SMALL = 64, 3 * N_EXPERTS
TAB_BIG_DST, TAB_BIG_SRC = 0, MAX_BIG
TAB_SMALL_DST, TAB_SMALL_SRC = 2 * MAX_BIG, 2 * MAX_BIG + MAX_SMALL
TAB_COUNTS = 2 * MAX_BIG + 2 * MAX_SMALL
TABLE_WIDTH = 512
POST_GROUP_ROWS = 512
ROW_BLOCK = 256
ROW_LOOKAHEAD = 3
ROW_RING = ROW_LOOKAHEAD + 1
VMEM_LIMIT = 56 * 1024 * 1024


def _cparams(sem):
    return pltpu.CompilerParams(dimension_semantics=sem, vmem_limit_bytes=VMEM_LIMIT)


def _rms(x, g):
    return x * lax.rsqrt(jnp.mean(x * x, axis=-1, keepdims=True) + EPS) * g


def _log_sigmoid(x):
    return -(jnp.maximum(-x, 0.0) + jnp.log1p(jnp.exp(-jnp.abs(x))))


def _sigmoid(x):
    return 1.0 / (1.0 + jnp.exp(-x))


def _dot(a, b):
    return jnp.dot(a, b, preferred_element_type=F32)


def _dot_nt(a, b):
    return lax.dot_general(a, b, (((1,), (1,)), ((), ())), preferred_element_type=F32)


def _dot_tn(a, b):
    return lax.dot_general(a, b, (((0,), (0,)), ((), ())), preferred_element_type=F32)


def _memkv_kernel(mem_ref, g_ref, wk_ref, wv_ref, mk_ref, mv_ref, mkb_ref, mvb_ref):
    mn = _rms(mem_ref[...], g_ref[...]).astype(BF16)
    mk = _dot(mn, wk_ref[...])
    mv = _dot(mn, wv_ref[...])
    mk_ref[...] = mk
    mv_ref[...] = mv
    mkb_ref[...] = mk.astype(BF16)
    mvb_ref[...] = mv.astype(BF16)


def _memkv(mem2d, g, wk, wv):
    rows = mem2d.shape[0]
    tm = N_MEM
    row_spec = pl.BlockSpec((tm, D_MODEL), lambda i: (i, 0))
    full = lambda shape: pl.BlockSpec(shape, lambda i: (0,) * len(shape))
    return pl.pallas_call(
        _memkv_kernel,
        grid=(rows // tm,),
        in_specs=[row_spec, full((1, D_MODEL)), full((D_MODEL, D_MODEL)), full((D_MODEL, D_MODEL))],
        out_specs=[row_spec, row_spec, row_spec, row_spec],
        out_shape=[jax.ShapeDtypeStruct((rows, D_MODEL), F32)] * 2
        + [jax.ShapeDtypeStruct((rows, D_MODEL), BF16)] * 2,
        compiler_params=_cparams(("arbitrary",)),
        name="memkv",
    )(mem2d, g, wk, wv)


def _inproj_kernel(x_ref, g_ref, wq_ref, wg_ref, wgt_ref, wc_ref, bg_ref, bgt_ref, cw_ref, cbuf_ref,
                   q_ref, k_ref, v_ref, og_ref, gcol_ref, grow_ref, yc_ref, nbuf_ref,
                   carry_ref, *, tm, chunk):
    j = pl.program_id(1)
    streams = cbuf_ref.shape[0]
    per = tm // streams

    @pl.when(j == 0)
    def _():
        carry_ref[0:2, :] = cbuf_ref[0]

    xb = _rms(x_ref[...], g_ref[...]).astype(BF16)

    p = _dot(xb, wq_ref[...])
    q_ref[...] = p[:, 0:D_MLSTM].astype(BF16)
    k_ref[...] = (p[:, D_MLSTM:2 * D_MLSTM] * (HEAD_DIM ** -0.5)).astype(BF16)
    v_ref[...] = p[:, 2 * D_MLSTM:3 * D_MLSTM].astype(BF16)
    og_ref[...] = _sigmoid(p[:, 3 * D_MLSTM:4 * D_MLSTM])

    gc = _dot(xb, wg_ref[...])[:, 0:2 * N_HEADS] + bg_ref[...]
    col = lax.broadcasted_iota(I32, gc.shape, 1)
    gcol_ref[...] = jnp.where(col < N_HEADS, gc, _log_sigmoid(gc))
    gr = _dot_nt(wgt_ref[...], xb) + bgt_ref[...]
    row = lax.broadcasted_iota(I32, gr.shape, 0)
    gr = jnp.where(row < N_HEADS, gr, _log_sigmoid(gr))
    for c in range(tm // chunk):
        grow_ref[c] = gr[:, c * chunk:(c + 1) * chunk]

    pc = _dot(xb, wc_ref[...])
    cb = pc[:, 0:D_CONV]
    u = pc[:, D_CONV:2 * D_CONV] * pc[:, 2 * D_CONV:3 * D_CONV]
    rid = lax.broadcasted_iota(I32, u.shape, 0)
    um1 = pltpu.roll(u, 1, 0)
    um2 = pltpu.roll(u, 2, 0)
    for m in range(streams):
        prev = carry_ref if streams == 1 else cbuf_ref.at[m]
        c0, c1 = prev[0:1, :], prev[1:2, :]
        um1 = jnp.where(rid == m * per, c1, um1)
        um2 = jnp.where(rid == m * per, c0, jnp.where(rid == m * per + 1, c1, um2))
        nbuf_ref[m] = u[(m + 1) * per - 2:(m + 1) * per, :]
    yc = cw_ref[0:1, :] * um2 + cw_ref[1:2, :] * um1 + cw_ref[2:3, :] * u
    yc_ref[...] = (cb * yc).astype(BF16)
    carry_ref[0:2, :] = u[tm - 2:tm, :]


def _inproj(x, g, wq, wg, wgt, wc, bg, bgt, cw, cbuf, *, tm, chunk):
    bsz, seq, _ = x.shape
    grid = (bsz, seq // tm)
    tok = lambda c: pl.BlockSpec((None, tm, c), lambda b, j: (b, j, 0))
    full = lambda shape: pl.BlockSpec(shape, lambda b, j: (0,) * len(shape))
    nck = tm // chunk
    streams = cbuf.shape[0] // bsz
    assert streams == 1 or tm == seq
    conv_state = pl.BlockSpec((streams, 2, D_CONV), lambda b, j: (b, 0, 0))
    return pl.pallas_call(
        functools.partial(_inproj_kernel, tm=tm, chunk=chunk),
        grid=grid,
        in_specs=[tok(D_MODEL), full((1, D_MODEL)), full((D_MODEL, 4 * D_MLSTM)),
                  full((D_MODEL, 128)), full((8, D_MODEL)), full((D_MODEL, 3 * D_CONV)),
                  full((1, 8)), full((8, 1)), full((CONV_WIDTH, D_CONV)), conv_state],
        out_specs=[tok(D_MLSTM), tok(D_MLSTM), tok(D_MLSTM), tok(D_MLSTM), tok(8),
                   pl.BlockSpec((None, nck, 8, chunk), lambda b, j: (b, j, 0, 0)),
                   tok(D_CONV), conv_state],
        out_shape=[jax.ShapeDtypeStruct((bsz, seq, D_MLSTM), BF16)] * 3
        + [jax.ShapeDtypeStruct((bsz, seq, D_MLSTM), F32),
           jax.ShapeDtypeStruct((bsz, seq, 8), F32),
           jax.ShapeDtypeStruct((bsz, seq // chunk, 8, chunk), F32),
           jax.ShapeDtypeStruct((bsz, seq, D_CONV), BF16),
           jax.ShapeDtypeStruct(cbuf.shape, F32)],
        scratch_shapes=[pltpu.VMEM((8, D_CONV), F32)],
        compiler_params=_cparams(("arbitrary", "arbitrary")),
        name="inproj",
    )(x, g, wq, wg, wgt, wc, bg, bgt, cw, cbuf)


def _mlstm_kernel(q_ref, k_ref, v_ref, og_ref, gc_ref, gr_ref, c0_ref, n0_ref, m0_ref, ng_ref,
                  ym_ref, c1_ref, n1_ref, m1_ref, c_s, n_s, m_s, *, chunk, nchunks, bsz):
    j = pl.program_id(0)

    @pl.when(j == 0)
    def _():
        c_s[...] = c0_ref[...]
        n_s[...] = n0_ref[...]
        m_s[...] = m0_ref[...]

    ti = lax.broadcasted_iota(I32, (chunk, chunk), 0)
    ji = lax.broadcasted_iota(I32, (chunk, chunk), 1)
    causal = ji <= ti

    def body(ci, carry):
        r0 = pl.multiple_of(ci * chunk, chunk)
        rows = pl.ds(r0, chunk)
        chains = [(b, h) for b in range(bsz) for h in range(N_HEADS)]
        cols = lambda h: slice(h * HEAD_DIM, (h + 1) * HEAD_DIM)
        each = lambda f: [f(n, b, h) for n, (b, h) in enumerate(chains)]
        q = lambda b, h: q_ref[b, rows, cols(h)]
        k = lambda b, h: k_ref[b, rows, cols(h)]
        v = lambda b, h: v_ref[b, rows, cols(h)]
        gcs = [gc_ref[b, rows, :] for b in range(bsz)]
        grs = [gr_ref[b, ci] for b in range(bsz)]
        li_c = each(lambda n, b, h: gcs[b][:, h:h + 1])
        lf_c = each(lambda n, b, h: gcs[b][:, N_HEADS + h:N_HEADS + h + 1])
        li_r = each(lambda n, b, h: grs[b][h:h + 1, :])
        lf_r = each(lambda n, b, h: grs[b][N_HEADS + h:N_HEADS + h + 1, :])
        m_prev = each(lambda n, b, h: m_s[b, h:h + 1, :])

        b_c = each(lambda n, b, h: jnp.sum(jnp.where(causal, lf_r[n], 0.0), axis=1, keepdims=True))
        b_r = each(lambda n, b, h: jnp.sum(jnp.where(ti <= ji, lf_c[n], 0.0), axis=0, keepdims=True))
        dmat = each(lambda n, b, h: jnp.where(causal, b_c[n] - b_r[n] + li_r[n], -jnp.inf))
        dmax = each(lambda n, b, h: jnp.max(dmat[n], axis=1, keepdims=True))
        inter = each(lambda n, b, h: b_c[n] + m_prev[n])
        m_t = each(lambda n, b, h: jnp.maximum(inter[n], dmax[n]))
        w_inter = each(lambda n, b, h: jnp.exp(inter[n] - m_t[n]))
        s = each(lambda n, b, h: _dot_nt(q(b, h), k(b, h)) * jnp.exp(dmat[n] - m_t[n]))
        qc = each(lambda n, b, h: _dot(q(b, h), c_s[b, h].astype(BF16)))
        sv = each(lambda n, b, h: _dot(s[n].astype(BF16), v(b, h)))
        qn = each(lambda n, b, h: jnp.sum(q(b, h).astype(F32) * n_s[b, h:h + 1, :], axis=1,
                                          keepdims=True))
        den = each(lambda n, b, h: w_inter[n] * qn[n] + jnp.sum(s[n], axis=1, keepdims=True))
        hh = each(lambda n, b, h: (w_inter[n] * qc[n] + sv[n])
                  / jnp.maximum(jnp.abs(den[n]), jnp.exp(-m_t[n])))

        m_new = each(lambda n, b, h: m_t[n][chunk - 1:chunk, :])
        b_last = each(lambda n, b, h: b_c[n][chunk - 1:chunk, :])
        decay = each(lambda n, b, h: jnp.exp(b_last[n] + m_prev[n] - m_new[n]))
        kw = each(lambda n, b, h: k(b, h).astype(F32)
                  * jnp.exp(b_last[n] - b_c[n] + li_c[n] - m_new[n]))
        kv = each(lambda n, b, h: _dot_tn(kw[n].astype(BF16), v(b, h)))
        for n, (b, h) in enumerate(chains):
            c_s[b, h] = decay[n] * c_s[b, h] + kv[n]
            n_s[b, h:h + 1, :] = decay[n] * n_s[b, h:h + 1, :] + jnp.sum(kw[n], axis=0, keepdims=True)
            m_s[b, h:h + 1, :] = m_new[n]

        hn = each(lambda n, b, h: hh[n] * lax.rsqrt(jnp.mean(hh[n] * hh[n], axis=1, keepdims=True) + EPS)
                  * ng_ref[:, cols(h)])
        for n, (b, h) in enumerate(chains):
            ym_ref[b, rows, cols(h)] = (hn[n] * og_ref[b, rows, cols(h)]).astype(BF16)
        return carry

    lax.fori_loop(0, nchunks, body, 0)

    @pl.when(j == pl.num_programs(0) - 1)
    def _():
        c1_ref[...] = c_s[...]
        n1_ref[...] = n_s[...]
        m1_ref[...] = m_s[...]


def _mlstm(q, k, v, og, gcol, grow, c0, n0, m0, ng, *, ct, chunk):
    bsz, seq, _ = q.shape
    nchunks = ct // chunk
    grid = (seq // ct,)
    tok = lambda c: pl.BlockSpec((bsz, ct, c), lambda j: (0, j, 0))
    st_c = pl.BlockSpec((bsz, N_HEADS, HEAD_DIM, HEAD_DIM), lambda j: (0, 0, 0, 0))
    st_n = pl.BlockSpec((bsz, N_HEADS, HEAD_DIM), lambda j: (0, 0, 0))
    st_m = pl.BlockSpec((bsz, N_HEADS, 1), lambda j: (0, 0, 0))
    return pl.pallas_call(
        functools.partial(_mlstm_kernel, chunk=chunk, nchunks=nchunks, bsz=bsz),
        grid=grid,
        in_specs=[tok(D_MLSTM), tok(D_MLSTM), tok(D_MLSTM), tok(D_MLSTM), tok(8),
                  pl.BlockSpec((bsz, nchunks, 8, chunk), lambda j: (0, j, 0, 0)),
                  st_c, st_n, st_m,
                  pl.BlockSpec((1, D_MLSTM), lambda j: (0, 0))],
        out_specs=[tok(D_MLSTM), st_c, st_n, st_m],
        out_shape=[jax.ShapeDtypeStruct((bsz, seq, D_MLSTM), BF16),
                   jax.ShapeDtypeStruct((bsz, N_HEADS, HEAD_DIM, HEAD_DIM), F32),
                   jax.ShapeDtypeStruct((bsz, N_HEADS, HEAD_DIM), F32),
                   jax.ShapeDtypeStruct((bsz, N_HEADS, 1), F32)],
        scratch_shapes=[pltpu.VMEM((bsz, N_HEADS, HEAD_DIM, HEAD_DIM), F32),
                        pltpu.VMEM((bsz, N_HEADS, HEAD_DIM), F32),
                        pltpu.VMEM((bsz, N_HEADS, 1), F32)],
        compiler_params=_cparams(("arbitrary",)),
        name="mlstm",
    )(q, k, v, og, gcol, grow, c0, n0, m0, ng)


def _post_kernel(ym_ref, yc_ref, x_ref, wmo_ref, gx_ref, wxq_ref, mk_ref, mv_ref, wxo_ref,
                 gf_ref, wrt_ref, br_ref,
                 h2_ref, xn_ref, eid_ref, gate_ref, rank_ref, cnt_ref, *, tm, sub):
    n_mem = mk_ref.shape[0]
    per = tm // n_mem
    n_groups = max(1, tm // POST_GROUP_ROWS) if n_mem == 1 else 1
    rows = [slice(g * (tm // n_groups), (g + 1) * (tm // n_groups)) for g in range(n_groups)]
    each = lambda f: [f(g, r) for g, r in enumerate(rows)]

    mix = each(lambda g, r: _dot(ym_ref[r, :], wmo_ref[0:D_MLSTM, :])
               + _dot(yc_ref[r, :], wmo_ref[D_MLSTM:D_MODEL, :]))
    h1 = each(lambda g, r: x_ref[r, :] + mix[g])
    xq = each(lambda g, r: _dot(_rms(h1[g], gx_ref[...]).astype(BF16), wxq_ref[...]).astype(BF16))

    def streams(g):
        if n_mem == 1:
            return [(slice(None), 0)]
        return [(slice(m * per, (m + 1) * per), m) for m in range(n_mem)]

    att = each(lambda g, r: [jnp.zeros((xq[g][q, :].shape[0], D_MODEL), F32) for q, _ in streams(g)])
    for hd in range(N_XHEADS):
        cols = slice(hd * XHEAD_DIM, (hd + 1) * XHEAD_DIM)
        s = each(lambda g, r: [_dot_nt(xq[g][q, cols], mk_ref[m, :, cols]) * (XHEAD_DIM ** -0.5)
                               for q, m in streams(g)])
        e = each(lambda g, r: [jnp.exp(v - jnp.max(v, axis=-1, keepdims=True)) for v in s[g]])
        p = each(lambda g, r: [(v / jnp.sum(v, axis=-1, keepdims=True)).astype(BF16) for v in e[g]])
        o = each(lambda g, r: [_dot(v, mv_ref[m, :, cols]).astype(BF16)
                               for v, (_, m) in zip(p[g], streams(g))])
        att = each(lambda g, r: [a + _dot(v, wxo_ref[cols, :]) for a, v in zip(att[g], o[g])])
    h2 = each(lambda g, r: h1[g] + jnp.concatenate(att[g], axis=0))
    xn2_parts = each(lambda g, r: _rms(h2[g], gf_ref[...]).astype(BF16))
    for g, r in enumerate(rows):
        h2_ref[r, :] = h2[g]
        xn_ref[r, :] = xn2_parts[g]
    xn2 = jnp.concatenate(xn2_parts, axis=0)

    logits = _dot_nt(wrt_ref[...], xn2) + br_ref[...]
    eidx = lax.broadcasted_iota(I32, logits.shape, 0).astype(F32)
    work = logits
    vals, ids, hots = [], [], []
    for _ in range(TOP_K):
        mx = jnp.max(work, axis=0, keepdims=True)
        idx = jnp.min(jnp.where(work == mx, eidx, float(N_EXPERTS)), axis=0, keepdims=True)
        sel = eidx == idx
        vals.append(mx)
        ids.append(idx)
        hots.append(sel)
        work = jnp.where(sel, -jnp.inf, work)
    exps = [jnp.exp(v - vals[0]) for v in vals]
    denom = exps[0] + exps[1] + exps[2] + exps[3]

    picked = jnp.zeros(logits.shape, F32)
    for sel in hots:
        picked = picked + sel.astype(F32)
    shift = jnp.full((tm, tm), sub.bit_length() - 1, I32)
    tj = lax.broadcasted_iota(I32, (tm, tm), 0)
    tt = lax.broadcasted_iota(I32, (tm, tm), 1)
    same = lax.shift_right_logical(tj, shift) == lax.shift_right_logical(tt, shift)
    before = jnp.where(jnp.logical_and(tj < tt, same), 1.0, 0.0).astype(BF16)
    prior = _dot(picked.astype(BF16), before)
    for kk in range(TOP_K):
        eid_ref[kk:kk + 1, :] = ids[kk].astype(I32)
        gate_ref[kk:kk + 1, :] = exps[kk] / denom
        rank_ref[kk:kk + 1, :] = jnp.sum(jnp.where(hots[kk], prior, 0.0), axis=0,
                                         keepdims=True).astype(I32)
    for s in range(tm // sub):
        cnt_ref[s] = jnp.sum(picked[:, s * sub:(s + 1) * sub], axis=1, keepdims=True).astype(I32)


def _post(ym, yc, x, wmo, gx, wxq, mkb, mvb, wxo, gf, wrt, br, *, tm, sub):
    bsz, seq, _ = x.shape
    nj = seq // tm
    grid = (bsz, nj)
    nsub = tm // sub
    n_tiles = bsz * nj * nsub
    tok = lambda c: pl.BlockSpec((None, tm, c), lambda b, j: (b, j, 0))
    full = lambda shape: pl.BlockSpec(shape, lambda b, j: (0,) * len(shape))
    n_mem = mkb.shape[0] // bsz
    mem = pl.BlockSpec((n_mem, N_MEM, D_MODEL), lambda b, j: (b, 0, 0))
    sel = pl.BlockSpec((None, TOP_K, tm), lambda b, j: (b, 0, j))
    return pl.pallas_call(
        functools.partial(_post_kernel, tm=tm, sub=sub),
        grid=grid,
        in_specs=[tok(D_MLSTM), tok(D_CONV), tok(D_MODEL), full((D_MODEL, D_MODEL)),
                  full((1, D_MODEL)), full((D_MODEL, D_MODEL)), mem, mem,
                  full((D_MODEL, D_MODEL)), full((1, D_MODEL)), full((N_EXPERTS, D_MODEL)),
                  full((N_EXPERTS, 1))],
        out_specs=[tok(D_MODEL), tok(D_MODEL), sel, sel, sel,
                   pl.BlockSpec((nsub, N_EXPERTS, 1), lambda b, j: (b * nj + j, 0, 0))],
        out_shape=[jax.ShapeDtypeStruct((bsz, seq, D_MODEL), F32),
                   jax.ShapeDtypeStruct((bsz, seq, D_MODEL), BF16),
                   jax.ShapeDtypeStruct((bsz, TOP_K, seq), I32),
                   jax.ShapeDtypeStruct((bsz, TOP_K, seq), F32),
                   jax.ShapeDtypeStruct((bsz, TOP_K, seq), I32),
                   jax.ShapeDtypeStruct((n_tiles, N_EXPERTS, 1), I32)],
        compiler_params=_cparams(("arbitrary", "arbitrary")),
        name="post",
    )(ym, yc, x, wmo, gx, wxq, mkb, mvb, wxo, gf, wrt, br)


def _sorted_rows(n_tokens):
    return -(-(TOP_K * n_tokens + N_EXPERTS * (SUBLANES - 1)) // ROW_BLOCK) * ROW_BLOCK


def _dispatch_kernel(tab_ref, fill_ref, eid_ref, rank_ref, ls_ref, x_ref, eids_ref, ranks_ref, xs_ref,
                     xb_ref, srt, zero_s, pending, sems, *, n_ptiles, n_blocks):
    i = pl.program_id(0)
    cur = i % 2

    def rows_copy(src, src_row, dst_row, nrows, sl):
        return pltpu.make_async_copy(
            src.at[pl.ds(pl.multiple_of(src_row, SUBLANES), nrows), :],
            xb_ref.at[pl.ds(pl.multiple_of(dst_row, SUBLANES), nrows), :], sems.at[sl])

    def granule(src, src_row, dst_row, sl):
        return rows_copy(src, src_row, dst_row, SUBLANES, sl)

    def drain(count, sl, nrows=SUBLANES):
        def body(g, c):
            rows_copy(zero_s, 0, 0, nrows, sl).wait()
            return c
        lax.fori_loop(0, count, body, 0)

    @pl.when(i == 0)
    def _():
        pending[0] = 0
        pending[1] = 0

    def issue_list(count, dst_at, src_at, nrows):
        def body(h, c):
            for u in range(COPIES_PER_TRIP):
                g = COPIES_PER_TRIP * h + u

                @pl.when(g < count)
                def _():
                    rows_copy(srt.at[cur], tab_ref[src_at + g], tab_ref[dst_at + g], nrows,
                              cur).start(priority=u % 2)
            return c
        lax.fori_loop(0, (count + COPIES_PER_TRIP - 1) // COPIES_PER_TRIP, body, 0)

    def sort_and_move(eid, rank, x):
        ntok = x.shape[0]
        nrows = _sorted_rows(ntok)
        e_iota = lax.broadcasted_iota(I32, (N_EXPERTS, ntok), 0)
        s_iota = lax.broadcasted_iota(I32, (nrows, ntok), 0)
        seg_start = ls_ref[...]
        hit = None
        for kk in range(TOP_K):
            start = jnp.sum(jnp.where(e_iota == eid[kk:kk + 1, :], seg_start, 0.0),
                            axis=0, keepdims=True).astype(I32)
            match = s_iota == start + rank[kk:kk + 1, :]
            hit = match if hit is None else jnp.logical_or(hit, match)
        perm = jnp.where(hit, 1.0, 0.0).astype(BF16)
        srt[cur, 0:nrows, :] = _dot(perm, x)
        drain(pending[0], 1 - cur, BIG_ROWS)
        drain(pending[1], 1 - cur)
        n_big, n_small = tab_ref[TAB_COUNTS], tab_ref[TAB_COUNTS + 1]
        issue_list(n_big, TAB_BIG_DST, TAB_BIG_SRC, BIG_ROWS)
        issue_list(n_small, TAB_SMALL_DST, TAB_SMALL_SRC, SUBLANES)
        pending[0] = n_big
        pending[1] = n_small

    @pl.when(i < n_ptiles)
    def _():
        sort_and_move(eid_ref[...], rank_ref[...], x_ref[...])

    @pl.when(i == n_ptiles)
    def _():
        sort_and_move(eids_ref[...], ranks_ref[...], xs_ref[...])
        drain(pending[0], cur, BIG_ROWS)
        drain(pending[1], cur)
        zero_s[...] = jnp.zeros_like(zero_s)
        for e in range(N_EXPERTS):
            dst = fill_ref[e]

            def zissue(g, c):
                granule(zero_s, 0, dst + g * SUBLANES, cur).start()
                return c
            lax.fori_loop(0, fill_ref[N_EXPERTS + e], zissue, 0)
        drain(fill_ref[2 * N_EXPERTS], cur)
        first_free = fill_ref[2 * N_EXPERTS + 1]

        def blk_copy(b):
            return pltpu.make_async_copy(
                zero_s, xb_ref.at[pl.ds(pl.multiple_of(b * ROW_BLOCK, ROW_BLOCK), ROW_BLOCK), :],
                sems.at[cur])

        def bissue(b, c):
            blk_copy(b).start()
            return c
        lax.fori_loop(first_free, n_blocks, bissue, 0)

        def bdrain(b, c):
            blk_copy(0).wait()
            return c
        lax.fori_loop(first_free, n_blocks, bdrain, 0)


def _dispatch(tab, fill, eid_p, rank_p, seg_start, xn_p, eid_s, rank_s, xn_s, *, tmd, n_blocks):
    n_ptiles = eid_p.shape[0]
    n_sample = xn_s.shape[0]
    last = n_ptiles - 1
    smem = lambda shape, imap: pl.BlockSpec(shape, imap, memory_space=pltpu.SMEM)
    return pl.pallas_call(
        functools.partial(_dispatch_kernel, n_ptiles=n_ptiles, n_blocks=n_blocks),
        grid=(n_ptiles + 1,),
        in_specs=[smem((TABLE_WIDTH,), lambda i: (i,)),
                  smem((TABLE_WIDTH,), lambda i: (0,)),
                  pl.BlockSpec((None, TOP_K, tmd), lambda i: (jnp.minimum(i, last), 0, 0)),
                  pl.BlockSpec((None, TOP_K, tmd), lambda i: (jnp.minimum(i, last), 0, 0)),
                  pl.BlockSpec((None, N_EXPERTS, 1), lambda i: (i, 0, 0)),
                  pl.BlockSpec((tmd, D_MODEL), lambda i: (jnp.minimum(i, last), 0)),
                  pl.BlockSpec((TOP_K, n_sample), lambda i: (0, 0)),
                  pl.BlockSpec((TOP_K, n_sample), lambda i: (0, 0)),
                  pl.BlockSpec((n_sample, D_MODEL), lambda i: (0, 0))],
        out_specs=pl.BlockSpec(memory_space=pl.ANY),
        out_shape=jax.ShapeDtypeStruct((n_blocks * ROW_BLOCK, D_MODEL), F32),
        scratch_shapes=[pltpu.VMEM((2, _sorted_rows(tmd), D_MODEL), F32),
                        pltpu.VMEM((ROW_BLOCK, D_MODEL), F32), pltpu.SMEM((2,), I32),
                        pltpu.SemaphoreType.DMA((2,))],
        compiler_params=_cparams(("arbitrary",)),
        name="dispatch",
    )(tab, fill, eid_p, rank_p, seg_start, xn_p, eid_s, rank_s, xn_s)


def _expert_kernel(tab_ref, xb_ref, wu_hbm, bu_ref, wd_hbm, bd_ref, yb_ref,
                   wu_f, wd_f, wu_s, wd_s, xbuf, ybuf, wsem, xsem, ysem, *, n_blocks):
    e = pl.program_id(0)
    first = tab_ref[e]
    nblk = tab_ref[N_EXPERTS + e]
    n_used = tab_ref[2 * N_EXPERTS]

    def hbm_rows(b):
        return pl.ds(pl.multiple_of(b * ROW_BLOCK, ROW_BLOCK), ROW_BLOCK)

    def x_copy(b, sl):
        return pltpu.make_async_copy(xb_ref.at[hbm_rows(b), :], xbuf.at[sl], xsem.at[sl])

    def y_copy(b, sl):
        return pltpu.make_async_copy(ybuf.at[sl], yb_ref.at[hbm_rows(b), :], ysem.at[sl])

    def w_copies(ex, sl):
        return (pltpu.make_async_copy(wu_hbm.at[ex], wu_f.at[sl], wsem.at[0, sl]),
                pltpu.make_async_copy(wd_hbm.at[ex], wd_f.at[sl], wsem.at[1, sl]))

    wslot = e % 2

    @pl.when(e == 0)
    def _():
        for c in w_copies(0, 0):
            c.start()
        for b in range(ROW_LOOKAHEAD):
            @pl.when(b < n_used)
            def _():
                x_copy(b, b % ROW_RING).start()

    @pl.when(e + 1 < N_EXPERTS)
    def _():
        for c in w_copies(e + 1, 1 - wslot):
            c.start()

    for c in w_copies(e, wslot):
        c.wait()

    step = 128

    def cast(r, c):
        rows = pl.ds(pl.multiple_of(r * step, step), step)
        wu_s[rows, :] = wu_f[wslot, rows, :].astype(BF16)
        wd_s[rows, :] = wd_f[wslot, rows, :].astype(BF16)
        return c
    lax.fori_loop(0, D_MODEL // step, cast, 0)

    def block(b, c):
        sl = b % ROW_RING
        x_copy(b, sl).wait()

        @pl.when(b + ROW_LOOKAHEAD < n_used)
        def _():
            x_copy(b + ROW_LOOKAHEAD, (b + ROW_LOOKAHEAD) % ROW_RING).start()

        @pl.when(b >= ROW_RING)
        def _():
            y_copy(b, sl).wait()

        hcat = _dot(xbuf[sl].astype(BF16), wu_s[...]) + bu_ref[...]
        glu = jnp.minimum(hcat[:, 0:D_FF], SWIGLU_LIMIT)
        lin = jnp.clip(hcat[:, D_FF:2 * D_FF], -SWIGLU_LIMIT, SWIGLU_LIMIT)
        act = glu * _sigmoid(SWIGLU_ALPHA * glu) * (lin + 1.0)
        ybuf[sl] = _dot(act.astype(BF16), wd_s[...]) + bd_ref[...]
        y_copy(b, sl).start()
        return c
    lax.fori_loop(first, first + nblk, block, 0)

    @pl.when(e == N_EXPERTS - 1)
    def _():
        for k in range(ROW_RING):
            @pl.when(k < n_used)
            def _():
                y_copy(0, (n_used - 1 - k) % ROW_RING).wait()

        ybuf[0] = jnp.zeros((ROW_BLOCK, D_MODEL), F32)

        def zissue(b, c):
            y_copy(b, 0).start()
            return c
        lax.fori_loop(n_used, n_blocks, zissue, 0)

        def zdrain(b, c):
            y_copy(0, 0).wait()
            return c
        lax.fori_loop(n_used, n_blocks, zdrain, 0)


def _experts(tab, xb, w_up, b_up, w_down, b_down, *, n_blocks):
    w_map = lambda e, tab: (e, 0, 0)
    grid_spec = pltpu.PrefetchScalarGridSpec(
        num_scalar_prefetch=1,
        grid=(N_EXPERTS,),
        in_specs=[pl.BlockSpec(memory_space=pl.ANY),
                  pl.BlockSpec(memory_space=pl.ANY),
                  pl.BlockSpec((None, 1, 2 * D_FF), w_map),
                  pl.BlockSpec(memory_space=pl.ANY),
                  pl.BlockSpec((None, 1, D_MODEL), w_map)],
        out_specs=pl.BlockSpec(memory_space=pl.ANY),
        scratch_shapes=[pltpu.VMEM((2, D_MODEL, 2 * D_FF), F32), pltpu.VMEM((2, D_FF, D_MODEL), F32),
                        pltpu.VMEM((D_MODEL, 2 * D_FF), BF16), pltpu.VMEM((D_FF, D_MODEL), BF16),
                        pltpu.VMEM((ROW_RING, ROW_BLOCK, D_MODEL), F32),
                        pltpu.VMEM((ROW_RING, ROW_BLOCK, D_MODEL), F32),
                        pltpu.SemaphoreType.DMA((2, 2)),
                        pltpu.SemaphoreType.DMA((ROW_RING,)), pltpu.SemaphoreType.DMA((ROW_RING,))],
    )
    return pl.pallas_call(
        functools.partial(_expert_kernel, n_blocks=n_blocks),
        grid_spec=grid_spec,
        out_shape=jax.ShapeDtypeStruct((n_blocks * ROW_BLOCK, D_MODEL), F32),
        compiler_params=_cparams(("arbitrary",)),
        name="experts",
    )(tab, xb, w_up, b_up, w_down, b_down)


def _split_bf16(a):
    hi = a.astype(BF16)
    return hi, (a - hi.astype(F32)).astype(BF16)


def _combine_kernel(tab_ref, nxt_ref, slot_ref, gate_ref, h2_ref, gfin_ref, yb_ref, y_ref, buf, sems,
                    *, tm):
    i = pl.program_id(0)
    n = pl.num_programs(0)
    cur = i % 2
    nrows = buf.shape[1]

    def rows_copy(src_row, sl, dst_row, nrows):
        return pltpu.make_async_copy(
            yb_ref.at[pl.ds(pl.multiple_of(src_row, SUBLANES), nrows), :],
            buf.at[sl, pl.ds(pl.multiple_of(dst_row, SUBLANES), nrows), :], sems.at[sl])

    def gather(t_ref, sl):
        def fetch(count, yb_at, tile_at, nrows):
            def body(h, c):
                for u in range(COPIES_PER_TRIP):
                    g = COPIES_PER_TRIP * h + u

                    @pl.when(g < count)
                    def _():
                        rows_copy(t_ref[yb_at + g], sl, t_ref[tile_at + g], nrows).start(priority=u % 2)
                return c
            lax.fori_loop(0, (count + COPIES_PER_TRIP - 1) // COPIES_PER_TRIP, body, 0)
        fetch(t_ref[TAB_COUNTS], TAB_BIG_DST, TAB_BIG_SRC, BIG_ROWS)
        fetch(t_ref[TAB_COUNTS + 1], TAB_SMALL_DST, TAB_SMALL_SRC, SUBLANES)

    @pl.when(i == 0)
    def _():
        buf[...] = jnp.zeros_like(buf)
        gather(tab_ref, 0)

    @pl.when(i + 1 < n)
    def _():
        gather(nxt_ref, 1 - cur)

    def drain(count, nrows):
        def body(g, c):
            rows_copy(0, cur, 0, nrows).wait()
            return c
        lax.fori_loop(0, count, body, 0)
    drain(tab_ref[TAB_COUNTS], BIG_ROWS)
    drain(tab_ref[TAB_COUNTS + 1], SUBLANES)

    s_iota = lax.broadcasted_iota(I32, (tm, nrows), 1)
    wgt = jnp.zeros((tm, nrows), F32)
    for kk in range(TOP_K):
        wgt = wgt + jnp.where(s_iota == slot_ref[:, kk:kk + 1], gate_ref[:, kk:kk + 1], 0.0)
    w_hi, w_lo = _split_bf16(wgt)
    rows = buf[cur].astype(BF16)
    acc = _dot(w_hi, rows) + _dot(w_lo, rows)
    y_ref[...] = _rms(h2_ref[...] + acc, gfin_ref[...])


def _combine(tab, slot_col, gate_col, h2, gfin, yb, *, tm):
    ntok = h2.shape[0]
    n = ntok // tm
    smem = lambda imap: pl.BlockSpec((TABLE_WIDTH,), imap, memory_space=pltpu.SMEM)
    return pl.pallas_call(
        functools.partial(_combine_kernel, tm=tm),
        grid=(n,),
        in_specs=[smem(lambda i: (i,)), smem(lambda i: (jnp.minimum(i + 1, n - 1),)),
                  pl.BlockSpec((tm, TOP_K), lambda i: (i, 0)),
                  pl.BlockSpec((tm, TOP_K), lambda i: (i, 0)),
                  pl.BlockSpec((tm, D_MODEL), lambda i: (i, 0)),
                  pl.BlockSpec((1, D_MODEL), lambda i: (0, 0)),
                  pl.BlockSpec(memory_space=pl.ANY)],
        out_specs=pl.BlockSpec((tm, D_MODEL), lambda i: (i, 0)),
        out_shape=jax.ShapeDtypeStruct((ntok, D_MODEL), F32),
        scratch_shapes=[pltpu.VMEM((2, _sorted_rows(tm), D_MODEL), F32),
                        pltpu.SemaphoreType.DMA((2,))],
        compiler_params=_cparams(("arbitrary",)),
        name="combine",
    )(tab, tab, slot_col, gate_col, h2, gfin, yb)


def _copy_tables(seg, seg_src, seg_dst):
    experts = jnp.arange(N_EXPERTS, dtype=I32)

    def flatten(counts, bound, src0, dst0, step):
        ends = jnp.cumsum(counts, axis=1)
        idx = jnp.arange(bound, dtype=I32)
        owner = jnp.sum(ends[:, None, :] <= idx[None, :, None], axis=-1)
        pick = lambda a: jnp.sum(jnp.where(owner[..., None] == experts, a[:, None, :], 0), axis=-1)
        off = (idx[None, :] - pick(ends - counts)) * step
        valid = idx[None, :] < ends[:, -1:]
        return (jnp.where(valid, pick(dst0) + off, 0), jnp.where(valid, pick(src0) + off, 0),
                ends[:, -1:])

    n_big = seg // BIG_ROWS
    n_small = (seg - n_big * BIG_ROWS) // SUBLANES
    b_dst, b_src, b_n = flatten(n_big, MAX_BIG, seg_src, seg_dst, BIG_ROWS)
    s_dst, s_src, s_n = flatten(n_small, MAX_SMALL, seg_src + n_big * BIG_ROWS,
                                seg_dst + n_big * BIG_ROWS, SUBLANES)
    tab = jnp.concatenate([b_dst, b_src, s_dst, s_src, b_n, s_n], axis=1)
    return jnp.pad(tab, ((0, 0), (0, TABLE_WIDTH - tab.shape[1]))).astype(I32).reshape(-1)


def _tiles(a, tile):
    bsz, kk, seq = a.shape
    return a.reshape(bsz, kk, seq // tile, tile).transpose(0, 2, 1, 3).reshape(-1, kk, tile)


def _path(x, cbuf, c0, n0, m0, mkb, mvb, wts, *, tm_in, ct, chunk, tm_post, sub, fold):
    bsz, seq, _ = x.shape
    if fold:
        x = x.reshape(1, bsz * seq, D_MODEL)
        q, k, v, og, gcol, grow, yc, nbuf = _inproj(
            x, wts["g_mix"], wts["wq"], wts["wg"], wts["wgt"], wts["wc"], wts["bg"], wts["bgt"],
            wts["cw"], cbuf, tm=bsz * seq, chunk=bsz * seq)
        q, k, v, og, gcol = (a.reshape(bsz, seq, a.shape[-1]) for a in (q, k, v, og, gcol))
        grow = grow.reshape(2 * N_HEADS, bsz, seq).transpose(1, 0, 2)[:, None]
    else:
        q, k, v, og, gcol, grow, yc, nbuf = _inproj(
            x, wts["g_mix"], wts["wq"], wts["wg"], wts["wgt"], wts["wc"], wts["bg"], wts["bgt"],
            wts["cw"], cbuf, tm=tm_in, chunk=chunk)
    ym, c1, n1, m1 = _mlstm(q, k, v, og, gcol, grow, c0, n0, m0, wts["ng"], ct=ct, chunk=chunk)
    if fold:
        ym = ym.reshape(1, bsz * seq, D_MLSTM)
    h2, xn, eid, gate, rank, cnt = _post(
        ym, yc, x, wts["wmo"], wts["g_x"], wts["wxq"], mkb, mvb, wts["wxo"], wts["g_ffn"],
        wts["wrt"], wts["br"], tm=tm_post, sub=sub)
    return dict(h2=h2, xn=xn, eid=eid, gate=gate, rank=rank, cnt=cnt[:, :, 0],
                c1=c1, n1=n1, m1=m1[..., 0], nbuf=nbuf)


def kernel(x_prompt, x_sample, state_mlstm_c, state_mlstm_n, state_mlstm_m, state_conv, cache_mem_k, cache_mem_v, mem_prompt, norm_mix_g, w_in, b_gate, mlstm_norm_g, conv_w, w_mix_out, norm_x_g, norm_mem_g, w_xq, w_xk, w_xv, w_xo, norm_ffn_g, w_router, b_router, w_up, b_up, w_down, b_down, norm_final_g):
    bp, lp, _ = x_prompt.shape
    bs, ls, _ = x_sample.shape
    l = 0
    row = lambda a: a.reshape(1, -1)

    wi = w_in[l]
    gate_cols = wi[:, 4 * D_MLSTM:4 * D_MLSTM + 2 * N_HEADS]
    wts = dict(
        g_mix=row(norm_mix_g[l]),
        wq=wi[:, 0:4 * D_MLSTM].astype(BF16),
        wg=jnp.pad(gate_cols, ((0, 0), (0, 128 - 2 * N_HEADS))).astype(BF16),
        wgt=gate_cols.T.astype(BF16),
        wc=wi[:, 4 * D_MLSTM + 2 * N_HEADS:].astype(BF16),
        bg=row(b_gate[l]), bgt=b_gate[l].reshape(-1, 1),
        cw=conv_w[l], ng=row(mlstm_norm_g[l]),
        wmo=w_mix_out[l].astype(BF16), g_x=row(norm_x_g[l]), wxq=w_xq[l].astype(BF16),
        wxo=w_xo[l].astype(BF16), g_ffn=row(norm_ffn_g[l]),
        wrt=w_router[l].T.astype(BF16), br=b_router[l].reshape(-1, 1),
    )

    mk, mv, mkb, mvb = _memkv(mem_prompt.reshape(bp * N_MEM, D_MODEL), row(norm_mem_g[l]),
                              w_xk[l].astype(BF16), w_xv[l].astype(BF16))
    zeros = lambda *s: jnp.zeros(s, F32)
    tm_post, tmd = 512, 256
    assert _sorted_rows(tmd) // BIG_ROWS <= MAX_BIG
    pr = _path(x_prompt, zeros(bp, CONV_WIDTH - 1, D_CONV), zeros(bp, N_HEADS, HEAD_DIM, HEAD_DIM),
               zeros(bp, N_HEADS, HEAD_DIM), zeros(bp, N_HEADS, 1),
               mkb.reshape(bp, N_MEM, D_MODEL), mvb.reshape(bp, N_MEM, D_MODEL), wts,
               tm_in=512, ct=512, chunk=CHUNK, tm_post=tm_post, sub=tmd, fold=False)
    sa = _path(x_sample, state_conv[l], state_mlstm_c[l], state_mlstm_n[l],
               state_mlstm_m[l][..., None],
               cache_mem_k[l].reshape(bs, N_MEM, D_MODEL).astype(BF16),
               cache_mem_v[l].reshape(bs, N_MEM, D_MODEL).astype(BF16), wts,
               tm_in=ls, ct=ls, chunk=min(CHUNK, ls), tm_post=bs * ls, sub=bs * ls, fold=True)

    n_ptok, n_stok = bp * lp, bs * ls
    cnt = jnp.concatenate([pr["cnt"], sa["cnt"]], axis=0)
    n_tiles = cnt.shape[0]
    n_blocks = -(-(TOP_K * (n_ptok + n_stok) + n_tiles * N_EXPERTS * (SUBLANES - 1)) // ROW_BLOCK) \
        + N_EXPERTS
    seg = (cnt + SUBLANES - 1) // SUBLANES * SUBLANES
    seg_src = jnp.cumsum(seg, axis=1) - seg
    tot = jnp.sum(seg, axis=0)
    blocks_e = (tot + ROW_BLOCK - 1) // ROW_BLOCK
    padded = blocks_e * ROW_BLOCK
    pstart = jnp.cumsum(padded) - padded
    bend = jnp.cumsum(blocks_e)
    n_used = bend[-1]
    exp_tab = jnp.concatenate([bend - blocks_e, blocks_e, n_used[None]]).astype(I32)
    seg_dst = pstart[None, :] + jnp.cumsum(seg, axis=0) - seg

    tab = _copy_tables(seg, seg_src, seg_dst)
    fgran = (padded - tot) // SUBLANES
    fill = jnp.concatenate([pstart + tot, fgran, jnp.sum(fgran)[None], n_used[None]])
    fill = jnp.pad(fill, (0, TABLE_WIDTH - fill.shape[0])).astype(I32)

    def lookup(table, eid):
        hit = eid[..., None] == jnp.arange(N_EXPERTS, dtype=I32)
        return jnp.sum(jnp.where(hit, table[:, None, None, :], 0), axis=-1)

    eid_pt, rank_pt = _tiles(pr["eid"], tmd), _tiles(pr["rank"], tmd)
    eid_st = sa["eid"].transpose(1, 0, 2).reshape(1, TOP_K, n_stok)
    rank_st = sa["rank"].transpose(1, 0, 2).reshape(1, TOP_K, n_stok)
    slot_p = (lookup(seg_src[:-1], eid_pt) + rank_pt).transpose(0, 2, 1).reshape(n_ptok, TOP_K)
    slot_s = (lookup(seg_src[-1:], eid_st) + rank_st).transpose(0, 2, 1).reshape(n_stok, TOP_K)

    xb = _dispatch(tab, fill, eid_pt, rank_pt, seg_src.astype(F32)[..., None],
                   pr["xn"].reshape(n_ptok, D_MODEL), eid_st[0], rank_st[0],
                   sa["xn"].reshape(n_stok, D_MODEL), tmd=tmd, n_blocks=n_blocks)
    yb = _experts(exp_tab, xb, w_up[l], b_up[l][:, None, :], w_down[l], b_down[l][:, None, :],
                  n_blocks=n_blocks)

    gfin = row(norm_final_g)
    split = (n_tiles - 1) * TABLE_WIDTH
    y_p = _combine(tab[:split], slot_p.astype(I32), pr["gate"].transpose(0, 2, 1).reshape(n_ptok, TOP_K),
                   pr["h2"].reshape(n_ptok, D_MODEL), gfin, yb, tm=tmd)
    y_s = _combine(tab[split:], slot_s.astype(I32), sa["gate"].transpose(0, 2, 1).reshape(n_stok, TOP_K),
                   sa["h2"].reshape(n_stok, D_MODEL), gfin, yb, tm=n_stok)

    lead = lambda a: a[None]
    return (y_p.reshape(bp, lp, D_MODEL), y_s.reshape(bs, ls, D_MODEL),
            lead(pr["c1"]), lead(pr["n1"]), lead(pr["m1"]), lead(pr["nbuf"]),
            lead(mk.reshape(bp, N_MEM, N_XHEADS, XHEAD_DIM)),
            lead(mv.reshape(bp, N_MEM, N_XHEADS, XHEAD_DIM)),
            lead(sa["c1"]), lead(sa["n1"]), lead(sa["m1"]), lead(sa["nbuf"]))
```

```python
import functools

import jax
import jax.numpy as jnp
from jax import lax
from jax.experimental import pallas as pl
from jax.experimental.pallas import tpu as pltpu

F32 = jnp.float32
BF16 = jnp.bfloat16
I32 = jnp.int32

D_MODEL = 1024
N_HEADS = 4
HEAD_DIM = 128
D_MLSTM = N_HEADS * HEAD_DIM
D_CONV = D_MODEL - D_MLSTM
CONV_WIDTH = 3
CHUNK = 64
N_MEM = 256
N_XHEADS = 4
XHEAD_DIM = D_MODEL // N_XHEADS
N_EXPERTS = 32
TOP_K = 4
D_FF = D_MODEL
SWIGLU_LIMIT = 7.0
SWIGLU_ALPHA = 1.702
EPS = 1e-5

SUBLANES = 8
BIG_ROWS = 4 * SUBLANES
COPIES_PER_TRIP = 4
MAX_BIG, MAX_SMALL = 64, 3 * N_EXPERTS
TAB_BIG_DST, TAB_BIG_SRC = 0, MAX_BIG
TAB_SMALL_DST, TAB_SMALL_SRC = 2 * MAX_BIG, 2 * MAX_BIG + MAX_SMALL
TAB_COUNTS = 2 * MAX_BIG + 2 * MAX_SMALL
TABLE_WIDTH = 512
POST_GROUP_ROWS = 512
ROW_BLOCK = 256
ROW_LOOKAHEAD = 3
ROW_RING = ROW_LOOKAHEAD + 2
VMEM_LIMIT = 56 * 1024 * 1024


def _cparams(sem):
    return pltpu.CompilerParams(dimension_semantics=sem, vmem_limit_bytes=VMEM_LIMIT)


def _rms(x, g):
    return x * lax.rsqrt(jnp.mean(x * x, axis=-1, keepdims=True) + EPS) * g


def _log_sigmoid(x):
    return -(jnp.maximum(-x, 0.0) + jnp.log1p(jnp.exp(-jnp.abs(x))))


def _sigmoid(x):
    return 1.0 / (1.0 + jnp.exp(-x))


def _dot(a, b):
    return jnp.dot(a, b, preferred_element_type=F32)


def _dot_nt(a, b):
    return lax.dot_general(a, b, (((1,), (1,)), ((), ())), preferred_element_type=F32)


def _dot_tn(a, b):
    return lax.dot_general(a, b, (((0,), (0,)), ((), ())), preferred_element_type=F32)


def _memkv_kernel(mem_ref, g_ref, wk_ref, wv_ref, mk_ref, mv_ref, mkb_ref, mvb_ref):
    mn = _rms(mem_ref[...], g_ref[...]).astype(BF16)
    mk = _dot(mn, wk_ref[...])
    mv = _dot(mn, wv_ref[...])
    mk_ref[...] = mk
    mv_ref[...] = mv
    mkb_ref[...] = mk.astype(BF16)
    mvb_ref[...] = mv.astype(BF16)


def _memkv(mem2d, g, wk, wv):
    rows = mem2d.shape[0]
    tm = N_MEM
    row_spec = pl.BlockSpec((tm, D_MODEL), lambda i: (i, 0))
    full = lambda shape: pl.BlockSpec(shape, lambda i: (0,) * len(shape))
    return pl.pallas_call(
        _memkv_kernel,
        grid=(rows // tm,),
        in_specs=[row_spec, full((1, D_MODEL)), full((D_MODEL, D_MODEL)), full((D_MODEL, D_MODEL))],
        out_specs=[row_spec, row_spec, row_spec, row_spec],
        out_shape=[jax.ShapeDtypeStruct((rows, D_MODEL), F32)] * 2
        + [jax.ShapeDtypeStruct((rows, D_MODEL), BF16)] * 2,
        compiler_params=_cparams(("arbitrary",)),
        name="memkv",
    )(mem2d, g, wk, wv)


def _inproj_kernel(x_ref, g_ref, wq_ref, wg_ref, wgt_ref, wc_ref, bg_ref, bgt_ref, cw_ref, cbuf_ref,
                   q_ref, k_ref, v_ref, og_ref, gcol_ref, grow_ref, yc_ref, nbuf_ref,
                   carry_ref, *, tm, chunk):
    j = pl.program_id(1)
    streams = cbuf_ref.shape[0]
    per = tm // streams

    @pl.when(j == 0)
    def _():
        carry_ref[0:2, :] = cbuf_ref[0]

    xb = _rms(x_ref[...], g_ref[...]).astype(BF16)

    p = _dot(xb, wq_ref[...])
    q_ref[...] = p[:, 0:D_MLSTM].astype(BF16)
    k_ref[...] = (p[:, D_MLSTM:2 * D_MLSTM] * (HEAD_DIM ** -0.5)).astype(BF16)
    v_ref[...] = p[:, 2 * D_MLSTM:3 * D_MLSTM].astype(BF16)
    og_ref[...] = _sigmoid(p[:, 3 * D_MLSTM:4 * D_MLSTM])

    gc = _dot(xb, wg_ref[...])[:, 0:2 * N_HEADS] + bg_ref[...]
    col = lax.broadcasted_iota(I32, gc.shape, 1)
    gcol_ref[...] = jnp.where(col < N_HEADS, gc, _log_sigmoid(gc))
    gr = _dot_nt(wgt_ref[...], xb) + bgt_ref[...]
    row = lax.broadcasted_iota(I32, gr.shape, 0)
    gr = jnp.where(row < N_HEADS, gr, _log_sigmoid(gr))
    for c in range(tm // chunk):
        grow_ref[c] = gr[:, c * chunk:(c + 1) * chunk]

    pc = _dot(xb, wc_ref[...])
    cb = pc[:, 0:D_CONV]
    u = pc[:, D_CONV:2 * D_CONV] * pc[:, 2 * D_CONV:3 * D_CONV]
    rid = lax.broadcasted_iota(I32, u.shape, 0)
    um1 = pltpu.roll(u, 1, 0)
    um2 = pltpu.roll(u, 2, 0)
    for m in range(streams):
        prev = carry_ref if streams == 1 else cbuf_ref.at[m]
        c0, c1 = prev[0:1, :], prev[1:2, :]
        um1 = jnp.where(rid == m * per, c1, um1)
        um2 = jnp.where(rid == m * per, c0, jnp.where(rid == m * per + 1, c1, um2))
        nbuf_ref[m] = u[(m + 1) * per - 2:(m + 1) * per, :]
    yc = cw_ref[0:1, :] * um2 + cw_ref[1:2, :] * um1 + cw_ref[2:3, :] * u
    yc_ref[...] = (cb * yc).astype(BF16)
    carry_ref[0:2, :] = u[tm - 2:tm, :]


def _inproj(x, g, wq, wg, wgt, wc, bg, bgt, cw, cbuf, *, tm, chunk):
    bsz, seq, _ = x.shape
    grid = (bsz, seq // tm)
    tok = lambda c: pl.BlockSpec((None, tm, c), lambda b, j: (b, j, 0))
    full = lambda shape: pl.BlockSpec(shape, lambda b, j: (0,) * len(shape))
    nck = tm // chunk
    streams = cbuf.shape[0] // bsz
    assert streams == 1 or tm == seq
    conv_state = pl.BlockSpec((streams, 2, D_CONV), lambda b, j: (b, 0, 0))
    return pl.pallas_call(
        functools.partial(_inproj_kernel, tm=tm, chunk=chunk),
        grid=grid,
        in_specs=[tok(D_MODEL), full((1, D_MODEL)), full((D_MODEL, 4 * D_MLSTM)),
                  full((D_MODEL, 128)), full((8, D_MODEL)), full((D_MODEL, 3 * D_CONV)),
                  full((1, 8)), full((8, 1)), full((CONV_WIDTH, D_CONV)), conv_state],
        out_specs=[tok(D_MLSTM), tok(D_MLSTM), tok(D_MLSTM), tok(D_MLSTM), tok(8),
                   pl.BlockSpec((None, nck, 8, chunk), lambda b, j: (b, j, 0, 0)),
                   tok(D_CONV), conv_state],
        out_shape=[jax.ShapeDtypeStruct((bsz, seq, D_MLSTM), BF16)] * 3
        + [jax.ShapeDtypeStruct((bsz, seq, D_MLSTM), F32),
           jax.ShapeDtypeStruct((bsz, seq, 8), F32),
           jax.ShapeDtypeStruct((bsz, seq // chunk, 8, chunk), F32),
           jax.ShapeDtypeStruct((bsz, seq, D_CONV), BF16),
           jax.ShapeDtypeStruct(cbuf.shape, F32)],
        scratch_shapes=[pltpu.VMEM((8, D_CONV), F32)],
        compiler_params=_cparams(("arbitrary", "arbitrary")),
        name="inproj",
    )(x, g, wq, wg, wgt, wc, bg, bgt, cw, cbuf)


def _mlstm_kernel(q_ref, k_ref, v_ref, og_ref, gc_ref, gr_ref, c0_ref, n0_ref, m0_ref, ng_ref,
                  ym_ref, c1_ref, n1_ref, m1_ref, c_s, n_s, m_s, *, chunk, nchunks, bsz):
    j = pl.program_id(0)

    @pl.when(j == 0)
    def _():
        c_s[...] = c0_ref[...]
        n_s[...] = n0_ref[...]
        m_s[...] = m0_ref[...]

    ti = lax.broadcasted_iota(I32, (chunk, chunk), 0)
    ji = lax.broadcasted_iota(I32, (chunk, chunk), 1)
    causal = ji <= ti

    def body(ci, carry):
        r0 = pl.multiple_of(ci * chunk, chunk)
        rows = pl.ds(r0, chunk)
        chains = [(b, h) for b in range(bsz) for h in range(N_HEADS)]
        cols = lambda h: slice(h * HEAD_DIM, (h + 1) * HEAD_DIM)
        each = lambda f: [f(n, b, h) for n, (b, h) in enumerate(chains)]
        q = lambda b, h: q_ref[b, rows, cols(h)]
        k = lambda b, h: k_ref[b, rows, cols(h)]
        v = lambda b, h: v_ref[b, rows, cols(h)]
        gcs = [gc_ref[b, rows, :] for b in range(bsz)]
        grs = [gr_ref[b, ci] for b in range(bsz)]
        li_c = each(lambda n, b, h: gcs[b][:, h:h + 1])
        lf_c = each(lambda n, b, h: gcs[b][:, N_HEADS + h:N_HEADS + h + 1])
        li_r = each(lambda n, b, h: grs[b][h:h + 1, :])
        lf_r = each(lambda n, b, h: grs[b][N_HEADS + h:N_HEADS + h + 1, :])
        m_prev = each(lambda n, b, h: m_s[b, h:h + 1, :])

        b_c = each(lambda n, b, h: jnp.sum(jnp.where(causal, lf_r[n], 0.0), axis=1, keepdims=True))
        b_r = each(lambda n, b, h: jnp.sum(jnp.where(ti <= ji, lf_c[n], 0.0), axis=0, keepdims=True))
        dmat = each(lambda n, b, h: jnp.where(causal, b_c[n] - b_r[n] + li_r[n], -jnp.inf))
        dmax = each(lambda n, b, h: jnp.max(dmat[n], axis=1, keepdims=True))
        inter = each(lambda n, b, h: b_c[n] + m_prev[n])
        m_t = each(lambda n, b, h: jnp.maximum(inter[n], dmax[n]))
        w_inter = each(lambda n, b, h: jnp.exp(inter[n] - m_t[n]))
        s = each(lambda n, b, h: _dot_nt(q(b, h), k(b, h)) * jnp.exp(dmat[n] - m_t[n]))
        qc = each(lambda n, b, h: _dot(q(b, h), c_s[b, h].astype(BF16)))
        sv = each(lambda n, b, h: _dot(s[n].astype(BF16), v(b, h)))
        qn = each(lambda n, b, h: jnp.sum(q(b, h).astype(F32) * n_s[b, h:h + 1, :], axis=1,
                                          keepdims=True))
        den = each(lambda n, b, h: w_inter[n] * qn[n] + jnp.sum(s[n], axis=1, keepdims=True))
        hh = each(lambda n, b, h: (w_inter[n] * qc[n] + sv[n])
                  / jnp.maximum(jnp.abs(den[n]), jnp.exp(-m_t[n])))

        m_new = each(lambda n, b, h: m_t[n][chunk - 1:chunk, :])
        b_last = each(lambda n, b, h: b_c[n][chunk - 1:chunk, :])
        decay = each(lambda n, b, h: jnp.exp(b_last[n] + m_prev[n] - m_new[n]))
        kw = each(lambda n, b, h: k(b, h).astype(F32)
                  * jnp.exp(b_last[n] - b_c[n] + li_c[n] - m_new[n]))
        kv = each(lambda n, b, h: _dot_tn(kw[n].astype(BF16), v(b, h)))
        for n, (b, h) in enumerate(chains):
            c_s[b, h] = decay[n] * c_s[b, h] + kv[n]
            n_s[b, h:h + 1, :] = decay[n] * n_s[b, h:h + 1, :] + jnp.sum(kw[n], axis=0, keepdims=True)
            m_s[b, h:h + 1, :] = m_new[n]

        hn = each(lambda n, b, h: hh[n] * lax.rsqrt(jnp.mean(hh[n] * hh[n], axis=1, keepdims=True) + EPS)
                  * ng_ref[:, cols(h)])
        for n, (b, h) in enumerate(chains):
            ym_ref[b, rows, cols(h)] = (hn[n] * og_ref[b, rows, cols(h)]).astype(BF16)
        return carry

    lax.fori_loop(0, nchunks, body, 0)

    @pl.when(j == pl.num_programs(0) - 1)
    def _():
        c1_ref[...] = c_s[...]
        n1_ref[...] = n_s[...]
        m1_ref[...] = m_s[...]


def _mlstm(q, k, v, og, gcol, grow, c0, n0, m0, ng, *, ct, chunk):
    bsz, seq, _ = q.shape
    nchunks = ct // chunk
    grid = (seq // ct,)
    tok = lambda c: pl.BlockSpec((bsz, ct, c), lambda j: (0, j, 0))
    st_c = pl.BlockSpec((bsz, N_HEADS, HEAD_DIM, HEAD_DIM), lambda j: (0, 0, 0, 0))
    st_n = pl.BlockSpec((bsz, N_HEADS, HEAD_DIM), lambda j: (0, 0, 0))
    st_m = pl.BlockSpec((bsz, N_HEADS, 1), lambda j: (0, 0, 0))
    return pl.pallas_call(
        functools.partial(_mlstm_kernel, chunk=chunk, nchunks=nchunks, bsz=bsz),
        grid=grid,
        in_specs=[tok(D_MLSTM), tok(D_MLSTM), tok(D_MLSTM), tok(D_MLSTM), tok(8),
                  pl.BlockSpec((bsz, nchunks, 8, chunk), lambda j: (0, j, 0, 0)),
                  st_c, st_n, st_m,
                  pl.BlockSpec((1, D_MLSTM), lambda j: (0, 0))],
        out_specs=[tok(D_MLSTM), st_c, st_n, st_m],
        out_shape=[jax.ShapeDtypeStruct((bsz, seq, D_MLSTM), BF16),
                   jax.ShapeDtypeStruct((bsz, N_HEADS, HEAD_DIM, HEAD_DIM), F32),
                   jax.ShapeDtypeStruct((bsz, N_HEADS, HEAD_DIM), F32),
                   jax.ShapeDtypeStruct((bsz, N_HEADS, 1), F32)],
        scratch_shapes=[pltpu.VMEM((bsz, N_HEADS, HEAD_DIM, HEAD_DIM), F32),
                        pltpu.VMEM((bsz, N_HEADS, HEAD_DIM), F32),
                        pltpu.VMEM((bsz, N_HEADS, 1), F32)],
        compiler_params=_cparams(("arbitrary",)),
        name="mlstm",
    )(q, k, v, og, gcol, grow, c0, n0, m0, ng)


def _post_kernel(ym_ref, yc_ref, x_ref, wmo_ref, gx_ref, wxq_ref, mk_ref, mv_ref, wxo_ref,
                 gf_ref, wrt_ref, br_ref,
                 h2_ref, xn_ref, eid_ref, gate_ref, rank_ref, cnt_ref, *, tm, sub):
    n_mem = mk_ref.shape[0]
    per = tm // n_mem
    n_groups = max(1, tm // POST_GROUP_ROWS) if n_mem == 1 else 1
    rows = [slice(g * (tm // n_groups), (g + 1) * (tm // n_groups)) for g in range(n_groups)]
    each = lambda f: [f(g, r) for g, r in enumerate(rows)]

    mix = each(lambda g, r: _dot(ym_ref[r, :], wmo_ref[0:D_MLSTM, :])
               + _dot(yc_ref[r, :], wmo_ref[D_MLSTM:D_MODEL, :]))
    h1 = each(lambda g, r: x_ref[r, :] + mix[g])
    xq = each(lambda g, r: _dot(_rms(h1[g], gx_ref[...]).astype(BF16), wxq_ref[...]).astype(BF16))

    def streams(g):
        if n_mem == 1:
            return [(slice(None), 0)]
        return [(slice(m * per, (m + 1) * per), m) for m in range(n_mem)]

    cols = lambda hd: slice(hd * XHEAD_DIM, (hd + 1) * XHEAD_DIM)
    units = [(g, q, m, hd) for g in range(n_groups) for q, m in streams(g) for hd in range(N_XHEADS)]
    s = [_dot_nt(xq[g][q, cols(hd)], mk_ref[m, :, cols(hd)]) * (XHEAD_DIM ** -0.5)
         for g, q, m, hd in units]
    e = [jnp.exp(v - jnp.max(v, axis=-1, keepdims=True)) for v in s]
    p = [(v / jnp.sum(v, axis=-1, keepdims=True)).astype(BF16) for v in e]
    o = [_dot(v, mv_ref[m, :, cols(hd)]).astype(BF16) for v, (g, q, m, hd) in zip(p, units)]
    per_group = len(units) // n_groups
    o_rows = [jnp.concatenate(
        [jnp.concatenate(o[g * per_group + st * N_XHEADS:g * per_group + (st + 1) * N_XHEADS], axis=1)
         for st in range(per_group // N_XHEADS)], axis=0) for g in range(n_groups)]
    h2 = each(lambda g, r: h1[g] + _dot(o_rows[g], wxo_ref[...]))
    xn2_parts = each(lambda g, r: _rms(h2[g], gf_ref[...]).astype(BF16))
    for g, r in enumerate(rows):
        h2_ref[r, :] = h2[g]
        xn_ref[r, :] = xn2_parts[g]
    xn2 = jnp.concatenate(xn2_parts, axis=0)

    logits = _dot_nt(wrt_ref[...], xn2) + br_ref[...]
    eidx = lax.broadcasted_iota(I32, logits.shape, 0).astype(F32)
    work = logits
    vals, ids, hots = [], [], []
    for _ in range(TOP_K):
        mx = jnp.max(work, axis=0, keepdims=True)
        idx = jnp.min(jnp.where(work == mx, eidx, float(N_EXPERTS)), axis=0, keepdims=True)
        sel = eidx == idx
        vals.append(mx)
        ids.append(idx)
        hots.append(sel)
        work = jnp.where(sel, -jnp.inf, work)
    exps = [jnp.exp(v - vals[0]) for v in vals]
    denom = exps[0] + exps[1] + exps[2] + exps[3]

    picked = jnp.zeros(logits.shape, F32)
    for sel in hots:
        picked = picked + sel.astype(F32)
    shift = jnp.full((tm, tm), sub.bit_length() - 1, I32)
    tj = lax.broadcasted_iota(I32, (tm, tm), 0)
    tt = lax.broadcasted_iota(I32, (tm, tm), 1)
    same = lax.shift_right_logical(tj, shift) == lax.shift_right_logical(tt, shift)
    before = jnp.where(jnp.logical_and(tj < tt, same), 1.0, 0.0).astype(BF16)
    prior = _dot(picked.astype(BF16), before)
    for kk in range(TOP_K):
        eid_ref[kk:kk + 1, :] = ids[kk].astype(I32)
        gate_ref[kk:kk + 1, :] = exps[kk] / denom
        rank_ref[kk:kk + 1, :] = jnp.sum(jnp.where(hots[kk], prior, 0.0), axis=0,
                                         keepdims=True).astype(I32)
    for s in range(tm // sub):
        cnt_ref[s] = jnp.sum(picked[:, s * sub:(s + 1) * sub], axis=1, keepdims=True).astype(I32)


def _post(ym, yc, x, wmo, gx, wxq, mkb, mvb, wxo, gf, wrt, br, *, tm, sub):
    bsz, seq, _ = x.shape
    nj = seq // tm
    grid = (bsz, nj)
    nsub = tm // sub
    n_tiles = bsz * nj * nsub
    tok = lambda c: pl.BlockSpec((None, tm, c), lambda b, j: (b, j, 0))
    full = lambda shape: pl.BlockSpec(shape, lambda b, j: (0,) * len(shape))
    n_mem = mkb.shape[0] // bsz
    mem = pl.BlockSpec((n_mem, N_MEM, D_MODEL), lambda b, j: (b, 0, 0))
    sel = pl.BlockSpec((None, TOP_K, tm), lambda b, j: (b, 0, j))
    return pl.pallas_call(
        functools.partial(_post_kernel, tm=tm, sub=sub),
        grid=grid,
        in_specs=[tok(D_MLSTM), tok(D_CONV), tok(D_MODEL), full((D_MODEL, D_MODEL)),
                  full((1, D_MODEL)), full((D_MODEL, D_MODEL)), mem, mem,
                  full((D_MODEL, D_MODEL)), full((1, D_MODEL)), full((N_EXPERTS, D_MODEL)),
                  full((N_EXPERTS, 1))],
        out_specs=[tok(D_MODEL), tok(D_MODEL), sel, sel, sel,
                   pl.BlockSpec((nsub, N_EXPERTS, 1), lambda b, j: (b * nj + j, 0, 0))],
        out_shape=[jax.ShapeDtypeStruct((bsz, seq, D_MODEL), F32),
                   jax.ShapeDtypeStruct((bsz, seq, D_MODEL), BF16),
                   jax.ShapeDtypeStruct((bsz, TOP_K, seq), I32),
                   jax.ShapeDtypeStruct((bsz, TOP_K, seq), F32),
                   jax.ShapeDtypeStruct((bsz, TOP_K, seq), I32),
                   jax.ShapeDtypeStruct((n_tiles, N_EXPERTS, 1), I32)],
        compiler_params=_cparams(("arbitrary", "arbitrary")),
        name="post",
    )(ym, yc, x, wmo, gx, wxq, mkb, mvb, wxo, gf, wrt, br)


def _sorted_rows(n_tokens):
    return -(-(TOP_K * n_tokens + N_EXPERTS * (SUBLANES - 1)) // ROW_BLOCK) * ROW_BLOCK


def _dispatch_kernel(tab_ref, fill_ref, eid_ref, rank_ref, ls_ref, x_ref, eids_ref, ranks_ref, xs_ref,
                     xb_ref, srt, zero_s, pending, sems, *, n_ptiles, n_blocks):
    i = pl.program_id(0)
    cur = i % 2

    def rows_copy(src, src_row, dst_row, nrows, sl):
        return pltpu.make_async_copy(
            src.at[pl.ds(pl.multiple_of(src_row, SUBLANES), nrows), :],
            xb_ref.at[pl.ds(pl.multiple_of(dst_row, SUBLANES), nrows), :], sems.at[sl])

    def granule(src, src_row, dst_row, sl):
        return rows_copy(src, src_row, dst_row, SUBLANES, sl)

    def drain(count, sl, nrows=SUBLANES):
        def body(g, c):
            rows_copy(zero_s, 0, 0, nrows, sl).wait()
            return c
        lax.fori_loop(0, count, body, 0)

    @pl.when(i == 0)
    def _():
        pending[0] = 0
        pending[1] = 0

    def issue_list(count, dst_at, src_at, nrows):
        def body(h, c):
            for u in range(COPIES_PER_TRIP):
                g = COPIES_PER_TRIP * h + u

                @pl.when(g < count)
                def _():
                    rows_copy(srt.at[cur], tab_ref[src_at + g], tab_ref[dst_at + g], nrows,
                              cur).start(priority=u % 2)
            return c
        lax.fori_loop(0, (count + COPIES_PER_TRIP - 1) // COPIES_PER_TRIP, body, 0)

    def sort_and_move(eid, rank, x):
        ntok = x.shape[0]
        nrows = _sorted_rows(ntok)
        e_iota = lax.broadcasted_iota(I32, (N_EXPERTS, ntok), 0)
        s_iota = lax.broadcasted_iota(I32, (nrows, ntok), 0)
        seg_start = ls_ref[...]
        hit = None
        for kk in range(TOP_K):
            start = jnp.sum(jnp.where(e_iota == eid[kk:kk + 1, :], seg_start, 0.0),
                            axis=0, keepdims=True).astype(I32)
            match = s_iota == start + rank[kk:kk + 1, :]
            hit = match if hit is None else jnp.logical_or(hit, match)
        perm = jnp.where(hit, 1.0, 0.0).astype(BF16)
        srt[cur, 0:nrows, :] = _dot(perm, x)
        drain(pending[0], 1 - cur, BIG_ROWS)
        drain(pending[1], 1 - cur)
        n_big, n_small = tab_ref[TAB_COUNTS], tab_ref[TAB_COUNTS + 1]
        issue_list(n_big, TAB_BIG_DST, TAB_BIG_SRC, BIG_ROWS)
        issue_list(n_small, TAB_SMALL_DST, TAB_SMALL_SRC, SUBLANES)
        pending[0] = n_big
        pending[1] = n_small

    @pl.when(i < n_ptiles)
    def _():
        sort_and_move(eid_ref[...], rank_ref[...], x_ref[...])

    @pl.when(i == n_ptiles)
    def _():
        sort_and_move(eids_ref[...], ranks_ref[...], xs_ref[...])
        drain(pending[0], cur, BIG_ROWS)
        drain(pending[1], cur)
        zero_s[...] = jnp.zeros_like(zero_s)
        for e in range(N_EXPERTS):
            dst = fill_ref[e]

            def zissue(g, c):
                granule(zero_s, 0, dst + g * SUBLANES, cur).start()
                return c
            lax.fori_loop(0, fill_ref[N_EXPERTS + e], zissue, 0)
        drain(fill_ref[2 * N_EXPERTS], cur)
        first_free = fill_ref[2 * N_EXPERTS + 1]

        def blk_copy(b):
            return pltpu.make_async_copy(
                zero_s, xb_ref.at[pl.ds(pl.multiple_of(b * ROW_BLOCK, ROW_BLOCK), ROW_BLOCK), :],
                sems.at[cur])

        def bissue(b, c):
            blk_copy(b).start()
            return c
        lax.fori_loop(first_free, n_blocks, bissue, 0)

        def bdrain(b, c):
            blk_copy(0).wait()
            return c
        lax.fori_loop(first_free, n_blocks, bdrain, 0)


def _dispatch(tab, fill, eid_p, rank_p, seg_start, xn_p, eid_s, rank_s, xn_s, *, tmd, n_blocks):
    n_ptiles = eid_p.shape[0]
    n_sample = xn_s.shape[0]
    last = n_ptiles - 1
    smem = lambda shape, imap: pl.BlockSpec(shape, imap, memory_space=pltpu.SMEM)
    return pl.pallas_call(
        functools.partial(_dispatch_kernel, n_ptiles=n_ptiles, n_blocks=n_blocks),
        grid=(n_ptiles + 1,),
        in_specs=[smem((TABLE_WIDTH,), lambda i: (i,)),
                  smem((TABLE_WIDTH,), lambda i: (0,)),
                  pl.BlockSpec((None, TOP_K, tmd), lambda i: (jnp.minimum(i, last), 0, 0)),
                  pl.BlockSpec((None, TOP_K, tmd), lambda i: (jnp.minimum(i, last), 0, 0)),
                  pl.BlockSpec((None, N_EXPERTS, 1), lambda i: (i, 0, 0)),
                  pl.BlockSpec((tmd, D_MODEL), lambda i: (jnp.minimum(i, last), 0)),
                  pl.BlockSpec((TOP_K, n_sample), lambda i: (0, 0)),
                  pl.BlockSpec((TOP_K, n_sample), lambda i: (0, 0)),
                  pl.BlockSpec((n_sample, D_MODEL), lambda i: (0, 0))],
        out_specs=pl.BlockSpec(memory_space=pl.ANY),
        out_shape=jax.ShapeDtypeStruct((n_blocks * ROW_BLOCK, D_MODEL), F32),
        scratch_shapes=[pltpu.VMEM((2, _sorted_rows(tmd), D_MODEL), F32),
                        pltpu.VMEM((ROW_BLOCK, D_MODEL), F32), pltpu.SMEM((2,), I32),
                        pltpu.SemaphoreType.DMA((2,))],
        compiler_params=_cparams(("arbitrary",)),
        name="dispatch",
    )(tab, fill, eid_p, rank_p, seg_start, xn_p, eid_s, rank_s, xn_s)


def _expert_kernel(tab_ref, xb_ref, wu_hbm, bu_ref, wd_hbm, bd_ref, yb_ref,
                   wu_f, wd_f, wu_s, wd_s, xbuf, ybuf, wsem, xsem, ysem, *, n_blocks):
    e = pl.program_id(0)
    first = tab_ref[e]
    nblk = tab_ref[N_EXPERTS + e]
    n_used = tab_ref[2 * N_EXPERTS]

    def hbm_rows(b):
        return pl.ds(pl.multiple_of(b * ROW_BLOCK, ROW_BLOCK), ROW_BLOCK)

    def x_copy(b, sl):
        return pltpu.make_async_copy(xb_ref.at[hbm_rows(b), :], xbuf.at[sl], xsem.at[sl])

    def y_copy(b, sl):
        return pltpu.make_async_copy(ybuf.at[sl], yb_ref.at[hbm_rows(b), :], ysem.at[sl])

    def w_copies(ex, sl):
        return (pltpu.make_async_copy(wu_hbm.at[ex], wu_f.at[sl], wsem.at[0, sl]),
                pltpu.make_async_copy(wd_hbm.at[ex], wd_f.at[sl], wsem.at[1, sl]))

    wslot = e % 2

    @pl.when(e == 0)
    def _():
        for c in w_copies(0, 0):
            c.start()
        for b in range(ROW_LOOKAHEAD):
            @pl.when(b < n_used)
            def _():
                x_copy(b, b % ROW_RING).start()

    @pl.when(e + 1 < N_EXPERTS)
    def _():
        for c in w_copies(e + 1, 1 - wslot):
            c.start()

    for c in w_copies(e, wslot):
        c.wait()

    step = 128

    def cast(r, c):
        rows = pl.ds(pl.multiple_of(r * step, step), step)
        wu_s[rows, :] = wu_f[wslot, rows, :].astype(BF16)
        wd_s[rows, :] = wd_f[wslot, rows, :].astype(BF16)
        return c
    lax.fori_loop(0, D_MODEL // step, cast, 0)

    def arrive(b):
        sl = b % ROW_RING
        x_copy(b, sl).wait()

        @pl.when(b + ROW_LOOKAHEAD < n_used)
        def _():
            x_copy(b + ROW_LOOKAHEAD, (b + ROW_LOOKAHEAD) % ROW_RING).start()

        @pl.when(b >= ROW_RING)
        def _():
            y_copy(b, sl).wait()
        return sl

    def up(sl):
        return _dot(xbuf[sl].astype(BF16), wu_s[...]) + bu_ref[...]

    def gated(hcat):
        glu = jnp.minimum(hcat[:, 0:D_FF], SWIGLU_LIMIT)
        lin = jnp.clip(hcat[:, D_FF:2 * D_FF], -SWIGLU_LIMIT, SWIGLU_LIMIT)
        return (glu * _sigmoid(SWIGLU_ALPHA * glu) * (lin + 1.0)).astype(BF16)

    def down(act, sl):
        ybuf[sl] = _dot(act, wd_s[...]) + bd_ref[...]

    def two_blocks(j, c):
        b0 = first + 2 * j
        slots = [arrive(b0), arrive(b0 + 1)]
        hidden = [up(sl) for sl in slots]
        acts = [gated(h) for h in hidden]
        for a, sl in zip(acts, slots):
            down(a, sl)
        y_copy(b0, slots[0]).start()
        y_copy(b0 + 1, slots[1]).start()
        return c
    lax.fori_loop(0, nblk // 2, two_blocks, 0)

    @pl.when(nblk % 2 == 1)
    def _():
        b = first + nblk - 1
        sl = arrive(b)
        down(gated(up(sl)), sl)
        y_copy(b, sl).start()

    @pl.when(e == N_EXPERTS - 1)
    def _():
        for k in range(ROW_RING):
            @pl.when(k < n_used)
            def _():
                y_copy(0, (n_used - 1 - k) % ROW_RING).wait()

        ybuf[0] = jnp.zeros((ROW_BLOCK, D_MODEL), F32)

        def zissue(b, c):
            y_copy(b, 0).start()
            return c
        lax.fori_loop(n_used, n_blocks, zissue, 0)

        def zdrain(b, c):
            y_copy(0, 0).wait()
            return c
        lax.fori_loop(n_used, n_blocks, zdrain, 0)


def _experts(tab, xb, w_up, b_up, w_down, b_down, *, n_blocks):
    w_map = lambda e, tab: (e, 0, 0)
    grid_spec = pltpu.PrefetchScalarGridSpec(
        num_scalar_prefetch=1,
        grid=(N_EXPERTS,),
        in_specs=[pl.BlockSpec(memory_space=pl.ANY),
                  pl.BlockSpec(memory_space=pl.ANY),
                  pl.BlockSpec((None, 1, 2 * D_FF), w_map),
                  pl.BlockSpec(memory_space=pl.ANY),
                  pl.BlockSpec((None, 1, D_MODEL), w_map)],
        out_specs=pl.BlockSpec(memory_space=pl.ANY),
        scratch_shapes=[pltpu.VMEM((2, D_MODEL, 2 * D_FF), F32), pltpu.VMEM((2, D_FF, D_MODEL), F32),
                        pltpu.VMEM((D_MODEL, 2 * D_FF), BF16), pltpu.VMEM((D_FF, D_MODEL), BF16),
                        pltpu.VMEM((ROW_RING, ROW_BLOCK, D_MODEL), F32),
                        pltpu.VMEM((ROW_RING, ROW_BLOCK, D_MODEL), F32),
                        pltpu.SemaphoreType.DMA((2, 2)),
                        pltpu.SemaphoreType.DMA((ROW_RING,)), pltpu.SemaphoreType.DMA((ROW_RING,))],
    )
    return pl.pallas_call(
        functools.partial(_expert_kernel, n_blocks=n_blocks),
        grid_spec=grid_spec,
        out_shape=jax.ShapeDtypeStruct((n_blocks * ROW_BLOCK, D_MODEL), F32),
        compiler_params=_cparams(("arbitrary",)),
        name="experts",
    )(tab, xb, w_up, b_up, w_down, b_down)


def _split_bf16(a):
    hi = a.astype(BF16)
    return hi, (a - hi.astype(F32)).astype(BF16)


def _combine_kernel(tab_ref, nxt_ref, slot_ref, gate_ref, h2_ref, gfin_ref, yb_ref, y_ref, buf, sems,
                    *, tm):
    i = pl.program_id(0)
    n = pl.num_programs(0)
    cur = i % 2
    nrows = buf.shape[1]

    def rows_copy(src_row, sl, dst_row, nrows):
        return pltpu.make_async_copy(
            yb_ref.at[pl.ds(pl.multiple_of(src_row, SUBLANES), nrows), :],
            buf.at[sl, pl.ds(pl.multiple_of(dst_row, SUBLANES), nrows), :], sems.at[sl])

    def gather(t_ref, sl):
        def fetch(count, yb_at, tile_at, nrows):
            def body(h, c):
                for u in range(COPIES_PER_TRIP):
                    g = COPIES_PER_TRIP * h + u

                    @pl.when(g < count)
                    def _():
                        rows_copy(t_ref[yb_at + g], sl, t_ref[tile_at + g], nrows).start(priority=u % 2)
                return c
            lax.fori_loop(0, (count + COPIES_PER_TRIP - 1) // COPIES_PER_TRIP, body, 0)
        fetch(t_ref[TAB_COUNTS], TAB_BIG_DST, TAB_BIG_SRC, BIG_ROWS)
        fetch(t_ref[TAB_COUNTS + 1], TAB_SMALL_DST, TAB_SMALL_SRC, SUBLANES)

    @pl.when(i == 0)
    def _():
        buf[...] = jnp.zeros_like(buf)
        gather(tab_ref, 0)

    @pl.when(i + 1 < n)
    def _():
        gather(nxt_ref, 1 - cur)

    def drain(count, nrows):
        def body(g, c):
            rows_copy(0, cur, 0, nrows).wait()
            return c
        lax.fori_loop(0, count, body, 0)
    drain(tab_ref[TAB_COUNTS], BIG_ROWS)
    drain(tab_ref[TAB_COUNTS + 1], SUBLANES)

    s_iota = lax.broadcasted_iota(I32, (tm, nrows), 1)
    wgt = jnp.zeros((tm, nrows), F32)
    for kk in range(TOP_K):
        wgt = wgt + jnp.where(s_iota == slot_ref[:, kk:kk + 1], gate_ref[:, kk:kk + 1], 0.0)
    w_hi, w_lo = _split_bf16(wgt)
    rows = buf[cur].astype(BF16)
    acc = _dot(w_hi, rows) + _dot(w_lo, rows)
    y_ref[...] = _rms(h2_ref[...] + acc, gfin_ref[...])


def _combine(tab, slot_col, gate_col, h2, gfin, yb, *, tm):
    ntok = h2.shape[0]
    n = ntok // tm
    smem = lambda imap: pl.BlockSpec((TABLE_WIDTH,), imap, memory_space=pltpu.SMEM)
    return pl.pallas_call(
        functools.partial(_combine_kernel, tm=tm),
        grid=(n,),
        in_specs=[smem(lambda i: (i,)), smem(lambda i: (jnp.minimum(i + 1, n - 1),)),
                  pl.BlockSpec((tm, TOP_K), lambda i: (i, 0)),
                  pl.BlockSpec((tm, TOP_K), lambda i: (i, 0)),
                  pl.BlockSpec((tm, D_MODEL), lambda i: (i, 0)),
                  pl.BlockSpec((1, D_MODEL), lambda i: (0, 0)),
                  pl.BlockSpec(memory_space=pl.ANY)],
        out_specs=pl.BlockSpec((tm, D_MODEL), lambda i: (i, 0)),
        out_shape=jax.ShapeDtypeStruct((ntok, D_MODEL), F32),
        scratch_shapes=[pltpu.VMEM((2, _sorted_rows(tm), D_MODEL), F32),
                        pltpu.SemaphoreType.DMA((2,))],
        compiler_params=_cparams(("arbitrary",)),
        name="combine",
    )(tab, tab, slot_col, gate_col, h2, gfin, yb)


def _copy_tables(seg, seg_src, seg_dst):
    experts = jnp.arange(N_EXPERTS, dtype=I32)

    def flatten(counts, bound, src0, dst0, step):
        ends = jnp.cumsum(counts, axis=1)
        idx = jnp.arange(bound, dtype=I32)
        owner = jnp.sum(ends[:, None, :] <= idx[None, :, None], axis=-1)
        pick = lambda a: jnp.sum(jnp.where(owner[..., None] == experts, a[:, None, :], 0), axis=-1)
        off = (idx[None, :] - pick(ends - counts)) * step
        valid = idx[None, :] < ends[:, -1:]
        return (jnp.where(valid, pick(dst0) + off, 0), jnp.where(valid, pick(src0) + off, 0),
                ends[:, -1:])

    n_big = seg // BIG_ROWS
    n_small = (seg - n_big * BIG_ROWS) // SUBLANES
    b_dst, b_src, b_n = flatten(n_big, MAX_BIG, seg_src, seg_dst, BIG_ROWS)
    s_dst, s_src, s_n = flatten(n_small, MAX_SMALL, seg_src + n_big * BIG_ROWS,
                                seg_dst + n_big * BIG_ROWS, SUBLANES)
    tab = jnp.concatenate([b_dst, b_src, s_dst, s_src, b_n, s_n], axis=1)
    return jnp.pad(tab, ((0, 0), (0, TABLE_WIDTH - tab.shape[1]))).astype(I32).reshape(-1)


def _tiles(a, tile):
    bsz, kk, seq = a.shape
    return a.reshape(bsz, kk, seq // tile, tile).transpose(0, 2, 1, 3).reshape(-1, kk, tile)


def _path(x, cbuf, c0, n0, m0, mkb, mvb, wts, *, tm_in, ct, chunk, tm_post, sub, fold):
    bsz, seq, _ = x.shape
    if fold:
        x = x.reshape(1, bsz * seq, D_MODEL)
        q, k, v, og, gcol, grow, yc, nbuf = _inproj(
            x, wts["g_mix"], wts["wq"], wts["wg"], wts["wgt"], wts["wc"], wts["bg"], wts["bgt"],
            wts["cw"], cbuf, tm=bsz * seq, chunk=bsz * seq)
        q, k, v, og, gcol = (a.reshape(bsz, seq, a.shape[-1]) for a in (q, k, v, og, gcol))
        grow = grow.reshape(2 * N_HEADS, bsz, seq).transpose(1, 0, 2)[:, None]
    else:
        q, k, v, og, gcol, grow, yc, nbuf = _inproj(
            x, wts["g_mix"], wts["wq"], wts["wg"], wts["wgt"], wts["wc"], wts["bg"], wts["bgt"],
            wts["cw"], cbuf, tm=tm_in, chunk=chunk)
    ym, c1, n1, m1 = _mlstm(q, k, v, og, gcol, grow, c0, n0, m0, wts["ng"], ct=ct, chunk=chunk)
    if fold:
        ym = ym.reshape(1, bsz * seq, D_MLSTM)
    h2, xn, eid, gate, rank, cnt = _post(
        ym, yc, x, wts["wmo"], wts["g_x"], wts["wxq"], mkb, mvb, wts["wxo"], wts["g_ffn"],
        wts["wrt"], wts["br"], tm=tm_post, sub=sub)
    return dict(h2=h2, xn=xn, eid=eid, gate=gate, rank=rank, cnt=cnt[:, :, 0],
                c1=c1, n1=n1, m1=m1[..., 0], nbuf=nbuf)


def kernel(x_prompt, x_sample, state_mlstm_c, state_mlstm_n, state_mlstm_m, state_conv, cache_mem_k, cache_mem_v, mem_prompt, norm_mix_g, w_in, b_gate, mlstm_norm_g, conv_w, w_mix_out, norm_x_g, norm_mem_g, w_xq, w_xk, w_xv, w_xo, norm_ffn_g, w_router, b_router, w_up, b_up, w_down, b_down, norm_final_g):
    bp, lp, _ = x_prompt.shape
    bs, ls, _ = x_sample.shape
    l = 0
    row = lambda a: a.reshape(1, -1)

    wi = w_in[l]
    gate_cols = wi[:, 4 * D_MLSTM:4 * D_MLSTM + 2 * N_HEADS]
    wts = dict(
        g_mix=row(norm_mix_g[l]),
        wq=wi[:, 0:4 * D_MLSTM].astype(BF16),
        wg=jnp.pad(gate_cols, ((0, 0), (0, 128 - 2 * N_HEADS))).astype(BF16),
        wgt=gate_cols.T.astype(BF16),
        wc=wi[:, 4 * D_MLSTM + 2 * N_HEADS:].astype(BF16),
        bg=row(b_gate[l]), bgt=b_gate[l].reshape(-1, 1),
        cw=conv_w[l], ng=row(mlstm_norm_g[l]),
        wmo=w_mix_out[l].astype(BF16), g_x=row(norm_x_g[l]), wxq=w_xq[l].astype(BF16),
        wxo=w_xo[l].astype(BF16), g_ffn=row(norm_ffn_g[l]),
        wrt=w_router[l].T.astype(BF16), br=b_router[l].reshape(-1, 1),
    )

    mk, mv, mkb, mvb = _memkv(mem_prompt.reshape(bp * N_MEM, D_MODEL), row(norm_mem_g[l]),
                              w_xk[l].astype(BF16), w_xv[l].astype(BF16))
    zeros = lambda *s: jnp.zeros(s, F32)
    tm_post, tmd = 512, 256
    assert _sorted_rows(tmd) // BIG_ROWS <= MAX_BIG
    pr = _path(x_prompt, zeros(bp, CONV_WIDTH - 1, D_CONV), zeros(bp, N_HEADS, HEAD_DIM, HEAD_DIM),
               zeros(bp, N_HEADS, HEAD_DIM), zeros(bp, N_HEADS, 1),
               mkb.reshape(bp, N_MEM, D_MODEL), mvb.reshape(bp, N_MEM, D_MODEL), wts,
               tm_in=512, ct=512, chunk=CHUNK, tm_post=tm_post, sub=tmd, fold=False)
    sa = _path(x_sample, state_conv[l], state_mlstm_c[l], state_mlstm_n[l],
               state_mlstm_m[l][..., None],
               cache_mem_k[l].reshape(bs, N_MEM, D_MODEL).astype(BF16),
               cache_mem_v[l].reshape(bs, N_MEM, D_MODEL).astype(BF16), wts,
               tm_in=ls, ct=ls, chunk=min(CHUNK, ls), tm_post=bs * ls, sub=bs * ls, fold=True)

    n_ptok, n_stok = bp * lp, bs * ls
    cnt = jnp.concatenate([pr["cnt"], sa["cnt"]], axis=0)
    n_tiles = cnt.shape[0]
    n_blocks = -(-(TOP_K * (n_ptok + n_stok) + n_tiles * N_EXPERTS * (SUBLANES - 1)) // ROW_BLOCK) \
        + N_EXPERTS
    seg = (cnt + SUBLANES - 1) // SUBLANES * SUBLANES
    seg_src = jnp.cumsum(seg, axis=1) - seg
    tot = jnp.sum(seg, axis=0)
    blocks_e = (tot + ROW_BLOCK - 1) // ROW_BLOCK
    padded = blocks_e * ROW_BLOCK
    pstart = jnp.cumsum(padded) - padded
    bend = jnp.cumsum(blocks_e)
    n_used = bend[-1]
    exp_tab = jnp.concatenate([bend - blocks_e, blocks_e, n_used[None]]).astype(I32)
    seg_dst = pstart[None, :] + jnp.cumsum(seg, axis=0) - seg

    tab = _copy_tables(seg, seg_src, seg_dst)
    fgran = (padded - tot) // SUBLANES
    fill = jnp.concatenate([pstart + tot, fgran, jnp.sum(fgran)[None], n_used[None]])
    fill = jnp.pad(fill, (0, TABLE_WIDTH - fill.shape[0])).astype(I32)

    def lookup(table, eid):
        hit = eid[..., None] == jnp.arange(N_EXPERTS, dtype=I32)
        return jnp.sum(jnp.where(hit, table[:, None, None, :], 0), axis=-1)

    eid_pt, rank_pt = _tiles(pr["eid"], tmd), _tiles(pr["rank"], tmd)
    eid_st = sa["eid"].transpose(1, 0, 2).reshape(1, TOP_K, n_stok)
    rank_st = sa["rank"].transpose(1, 0, 2).reshape(1, TOP_K, n_stok)
    slot_p = (lookup(seg_src[:-1], eid_pt) + rank_pt).transpose(0, 2, 1).reshape(n_ptok, TOP_K)
    slot_s = (lookup(seg_src[-1:], eid_st) + rank_st).transpose(0, 2, 1).reshape(n_stok, TOP_K)

    xb = _dispatch(tab, fill, eid_pt, rank_pt, seg_src.astype(F32)[..., None],
                   pr["xn"].reshape(n_ptok, D_MODEL), eid_st[0], rank_st[0],
                   sa["xn"].reshape(n_stok, D_MODEL), tmd=tmd, n_blocks=n_blocks)
    yb = _experts(exp_tab, xb, w_up[l], b_up[l][:, None, :], w_down[l], b_down[l][:, None, :],
                  n_blocks=n_blocks)

    gfin = row(norm_final_g)
    split = (n_tiles - 1) * TABLE_WIDTH
    y_p = _combine(tab[:split], slot_p.astype(I32), pr["gate"].transpose(0, 2, 1).reshape(n_ptok, TOP_K),
                   pr["h2"].reshape(n_ptok, D_MODEL), gfin, yb, tm=tmd)
    y_s = _combine(tab[split:], slot_s.astype(I32), sa["gate"].transpose(0, 2, 1).reshape(n_stok, TOP_K),
                   sa["h2"].reshape(n_stok, D_MODEL), gfin, yb, tm=n_stok)

    lead = lambda a: a[None]
    return (y_p.reshape(bp, lp, D_MODEL), y_s.reshape(bs, ls, D_MODEL),
            lead(pr["c1"]), lead(pr["n1"]), lead(pr["m1"]), lead(pr["nbuf"]),
            lead(mk.reshape(bp, N_MEM, N_XHEADS, XHEAD_DIM)),
            lead(mv.reshape(bp, N_MEM, N_XHEADS, XHEAD_DIM)),
            lead(sa["c1"]), lead(sa["n1"]), lead(sa["m1"]), lead(sa["nbuf"]))
```

```python
import functools

import jax
import jax.numpy as jnp
from jax import lax
from jax.experimental import pallas as pl
from jax.experimental.pallas import tpu as pltpu

F32 = jnp.float32
BF16 = jnp.bfloat16
I32 = jnp.int32

D_MODEL = 1024
N_HEADS = 4
HEAD_DIM = 128
D_MLSTM = N_HEADS * HEAD_DIM
D_CONV = D_MODEL - D_MLSTM
CONV_WIDTH = 3
CHUNK = 64
N_MEM = 256
N_XHEADS = 4
XHEAD_DIM = D_MODEL // N_XHEADS
N_EXPERTS = 32
TOP_K = 4
D_FF = D_MODEL
SWIGLU_LIMIT = 7.0
SWIGLU_ALPHA = 1.702
EPS = 1e-5

SUBLANES = 16
BIG_ROWS = 2 * SUBLANES
COPIES_PER_TRIP = 4
MAX_BIG, MAX_SMALL = 64, 3 * N_EXPERTS
TAB_BIG_DST, TAB_BIG_SRC = 0, MAX_BIG
TAB_SMALL_DST, TAB_SMALL_SRC = 2 * MAX_BIG, 2 * MAX_BIG + MAX_SMALL
TAB_COUNTS = 2 * MAX_BIG + 2 * MAX_SMALL
TABLE_WIDTH = 512
POST_GROUP_ROWS = 512
ROW_BLOCK = 256
ROW_LOOKAHEAD = 3
ROW_RING = ROW_LOOKAHEAD + 2
VMEM_LIMIT = 56 * 1024 * 1024


def _cparams(sem):
    return pltpu.CompilerParams(dimension_semantics=sem, vmem_limit_bytes=VMEM_LIMIT)


def _rms(x, g):
    return x * lax.rsqrt(jnp.mean(x * x, axis=-1, keepdims=True) + EPS) * g


def _log_sigmoid(x):
    return -(jnp.maximum(-x, 0.0) + jnp.log1p(jnp.exp(-jnp.abs(x))))


def _sigmoid(x):
    return 1.0 / (1.0 + jnp.exp(-x))


def _dot(a, b):
    return jnp.dot(a, b, preferred_element_type=F32)


def _dot_nt(a, b):
    return lax.dot_general(a, b, (((1,), (1,)), ((), ())), preferred_element_type=F32)


def _dot_tn(a, b):
    return lax.dot_general(a, b, (((0,), (0,)), ((), ())), preferred_element_type=F32)


def _memkv_kernel(mem_ref, g_ref, wk_ref, wv_ref, mk_ref, mv_ref, mkb_ref, mvb_ref):
    mn = _rms(mem_ref[...], g_ref[...]).astype(BF16)
    mk = _dot(mn, wk_ref[...])
    mv = _dot(mn, wv_ref[...])
    mk_ref[...] = mk
    mv_ref[...] = mv
    mkb_ref[...] = mk.astype(BF16)
    mvb_ref[...] = mv.astype(BF16)


def _memkv(mem2d, g, wk, wv):
    rows = mem2d.shape[0]
    tm = N_MEM
    row_spec = pl.BlockSpec((tm, D_MODEL), lambda i: (i, 0))
    full = lambda shape: pl.BlockSpec(shape, lambda i: (0,) * len(shape))
    return pl.pallas_call(
        _memkv_kernel,
        grid=(rows // tm,),
        in_specs=[row_spec, full((1, D_MODEL)), full((D_MODEL, D_MODEL)), full((D_MODEL, D_MODEL))],
        out_specs=[row_spec, row_spec, row_spec, row_spec],
        out_shape=[jax.ShapeDtypeStruct((rows, D_MODEL), F32)] * 2
        + [jax.ShapeDtypeStruct((rows, D_MODEL), BF16)] * 2,
        compiler_params=_cparams(("arbitrary",)),
        name="memkv",
    )(mem2d, g, wk, wv)


def _inproj_kernel(x_ref, g_ref, wq_ref, wg_ref, wgt_ref, wc_ref, bg_ref, bgt_ref, cw_ref, cbuf_ref,
                   q_ref, k_ref, v_ref, og_ref, gcol_ref, grow_ref, yc_ref, nbuf_ref,
                   carry_ref, *, tm, chunk):
    j = pl.program_id(1)
    streams = cbuf_ref.shape[0]
    per = tm // streams

    @pl.when(j == 0)
    def _():
        carry_ref[0:2, :] = cbuf_ref[0]

    xb = _rms(x_ref[...], g_ref[...]).astype(BF16)

    p = _dot(xb, wq_ref[...])
    q_ref[...] = p[:, 0:D_MLSTM].astype(BF16)
    k_ref[...] = (p[:, D_MLSTM:2 * D_MLSTM] * (HEAD_DIM ** -0.5)).astype(BF16)
    v_ref[...] = p[:, 2 * D_MLSTM:3 * D_MLSTM].astype(BF16)
    og_ref[...] = _sigmoid(p[:, 3 * D_MLSTM:4 * D_MLSTM])

    gc = _dot(xb, wg_ref[...])[:, 0:2 * N_HEADS] + bg_ref[...]
    col = lax.broadcasted_iota(I32, gc.shape, 1)
    gcol_ref[...] = jnp.where(col < N_HEADS, gc, _log_sigmoid(gc))
    gr = _dot_nt(wgt_ref[...], xb) + bgt_ref[...]
    row = lax.broadcasted_iota(I32, gr.shape, 0)
    gr = jnp.where(row < N_HEADS, gr, _log_sigmoid(gr))
    for c in range(tm // chunk):
        grow_ref[c] = gr[:, c * chunk:(c + 1) * chunk]

    pc = _dot(xb, wc_ref[...])
    cb = pc[:, 0:D_CONV]
    u = pc[:, D_CONV:2 * D_CONV] * pc[:, 2 * D_CONV:3 * D_CONV]
    rid = lax.broadcasted_iota(I32, u.shape, 0)
    um1 = pltpu.roll(u, 1, 0)
    um2 = pltpu.roll(u, 2, 0)
    for m in range(streams):
        prev = carry_ref if streams == 1 else cbuf_ref.at[m]
        c0, c1 = prev[0:1, :], prev[1:2, :]
        um1 = jnp.where(rid == m * per, c1, um1)
        um2 = jnp.where(rid == m * per, c0, jnp.where(rid == m * per + 1, c1, um2))
        nbuf_ref[m] = u[(m + 1) * per - 2:(m + 1) * per, :]
    yc = cw_ref[0:1, :] * um2 + cw_ref[1:2, :] * um1 + cw_ref[2:3, :] * u
    yc_ref[...] = (cb * yc).astype(BF16)
    carry_ref[0:2, :] = u[tm - 2:tm, :]


def _inproj(x, g, wq, wg, wgt, wc, bg, bgt, cw, cbuf, *, tm, chunk):
    bsz, seq, _ = x.shape
    grid = (bsz, seq // tm)
    tok = lambda c: pl.BlockSpec((None, tm, c), lambda b, j: (b, j, 0))
    full = lambda shape: pl.BlockSpec(shape, lambda b, j: (0,) * len(shape))
    nck = tm // chunk
    streams = cbuf.shape[0] // bsz
    assert streams == 1 or tm == seq
    conv_state = pl.BlockSpec((streams, 2, D_CONV), lambda b, j: (b, 0, 0))
    return pl.pallas_call(
        functools.partial(_inproj_kernel, tm=tm, chunk=chunk),
        grid=grid,
        in_specs=[tok(D_MODEL), full((1, D_MODEL)), full((D_MODEL, 4 * D_MLSTM)),
                  full((D_MODEL, 128)), full((8, D_MODEL)), full((D_MODEL, 3 * D_CONV)),
                  full((1, 8)), full((8, 1)), full((CONV_WIDTH, D_CONV)), conv_state],
        out_specs=[tok(D_MLSTM), tok(D_MLSTM), tok(D_MLSTM), tok(D_MLSTM), tok(8),
                   pl.BlockSpec((None, nck, 8, chunk), lambda b, j: (b, j, 0, 0)),
                   tok(D_CONV), conv_state],
        out_shape=[jax.ShapeDtypeStruct((bsz, seq, D_MLSTM), BF16)] * 3
        + [jax.ShapeDtypeStruct((bsz, seq, D_MLSTM), F32),
           jax.ShapeDtypeStruct((bsz, seq, 8), F32),
           jax.ShapeDtypeStruct((bsz, seq // chunk, 8, chunk), F32),
           jax.ShapeDtypeStruct((bsz, seq, D_CONV), BF16),
           jax.ShapeDtypeStruct(cbuf.shape, F32)],
        scratch_shapes=[pltpu.VMEM((8, D_CONV), F32)],
        compiler_params=_cparams(("arbitrary", "arbitrary")),
        name="inproj",
    )(x, g, wq, wg, wgt, wc, bg, bgt, cw, cbuf)


def _mlstm_kernel(q_ref, k_ref, v_ref, og_ref, gc_ref, gr_ref, c0_ref, n0_ref, m0_ref, ng_ref,
                  ym_ref, c1_ref, n1_ref, m1_ref, c_s, n_s, m_s, *, chunk, nchunks, bsz):
    j = pl.program_id(0)

    @pl.when(j == 0)
    def _():
        c_s[...] = c0_ref[...]
        n_s[...] = n0_ref[...]
        m_s[...] = m0_ref[...]

    ti = lax.broadcasted_iota(I32, (chunk, chunk), 0)
    ji = lax.broadcasted_iota(I32, (chunk, chunk), 1)
    causal = ji <= ti

    def body(ci, carry):
        r0 = pl.multiple_of(ci * chunk, chunk)
        rows = pl.ds(r0, chunk)
        chains = [(b, h) for b in range(bsz) for h in range(N_HEADS)]
        cols = lambda h: slice(h * HEAD_DIM, (h + 1) * HEAD_DIM)
        each = lambda f: [f(n, b, h) for n, (b, h) in enumerate(chains)]
        q = lambda b, h: q_ref[b, rows, cols(h)]
        k = lambda b, h: k_ref[b, rows, cols(h)]
        v = lambda b, h: v_ref[b, rows, cols(h)]
        gcs = [gc_ref[b, rows, :] for b in range(bsz)]
        grs = [gr_ref[b, ci] for b in range(bsz)]
        li_c = each(lambda n, b, h: gcs[b][:, h:h + 1])
        lf_c = each(lambda n, b, h: gcs[b][:, N_HEADS + h:N_HEADS + h + 1])
        li_r = each(lambda n, b, h: grs[b][h:h + 1, :])
        lf_r = each(lambda n, b, h: grs[b][N_HEADS + h:N_HEADS + h + 1, :])
        m_prev = each(lambda n, b, h: m_s[b, h:h + 1, :])

        b_c = each(lambda n, b, h: jnp.sum(jnp.where(causal, lf_r[n], 0.0), axis=1, keepdims=True))
        b_r = each(lambda n, b, h: jnp.sum(jnp.where(ti <= ji, lf_c[n], 0.0), axis=0, keepdims=True))
        dmat = each(lambda n, b, h: jnp.where(causal, b_c[n] - b_r[n] + li_r[n], -jnp.inf))
        dmax = each(lambda n, b, h: jnp.max(dmat[n], axis=1, keepdims=True))
        inter = each(lambda n, b, h: b_c[n] + m_prev[n])
        m_t = each(lambda n, b, h: jnp.maximum(inter[n], dmax[n]))
        w_inter = each(lambda n, b, h: jnp.exp(inter[n] - m_t[n]))
        s = each(lambda n, b, h: _dot_nt(q(b, h), k(b, h)) * jnp.exp(dmat[n] - m_t[n]))
        qc = each(lambda n, b, h: _dot(q(b, h), c_s[b, h].astype(BF16)))
        sv = each(lambda n, b, h: _dot(s[n].astype(BF16), v(b, h)))
        qn = each(lambda n, b, h: jnp.sum(q(b, h).astype(F32) * n_s[b, h:h + 1, :], axis=1,
                                          keepdims=True))
        den = each(lambda n, b, h: w_inter[n] * qn[n] + jnp.sum(s[n], axis=1, keepdims=True))
        hh = each(lambda n, b, h: (w_inter[n] * qc[n] + sv[n])
                  / jnp.maximum(jnp.abs(den[n]), jnp.exp(-m_t[n])))

        m_new = each(lambda n, b, h: m_t[n][chunk - 1:chunk, :])
        b_last = each(lambda n, b, h: b_c[n][chunk - 1:chunk, :])
        decay = each(lambda n, b, h: jnp.exp(b_last[n] + m_prev[n] - m_new[n]))
        kw = each(lambda n, b, h: k(b, h).astype(F32)
                  * jnp.exp(b_last[n] - b_c[n] + li_c[n] - m_new[n]))
        kv = each(lambda n, b, h: _dot_tn(kw[n].astype(BF16), v(b, h)))
        for n, (b, h) in enumerate(chains):
            c_s[b, h] = decay[n] * c_s[b, h] + kv[n]
            n_s[b, h:h + 1, :] = decay[n] * n_s[b, h:h + 1, :] + jnp.sum(kw[n], axis=0, keepdims=True)
            m_s[b, h:h + 1, :] = m_new[n]

        hn = each(lambda n, b, h: hh[n] * lax.rsqrt(jnp.mean(hh[n] * hh[n], axis=1, keepdims=True) + EPS)
                  * ng_ref[:, cols(h)])
        for n, (b, h) in enumerate(chains):
            ym_ref[b, rows, cols(h)] = (hn[n] * og_ref[b, rows, cols(h)]).astype(BF16)
        return carry

    lax.fori_loop(0, nchunks, body, 0)

    @pl.when(j == pl.num_programs(0) - 1)
    def _():
        c1_ref[...] = c_s[...]
        n1_ref[...] = n_s[...]
        m1_ref[...] = m_s[...]


def _mlstm(q, k, v, og, gcol, grow, c0, n0, m0, ng, *, ct, chunk):
    bsz, seq, _ = q.shape
    nchunks = ct // chunk
    grid = (seq // ct,)
    tok = lambda c: pl.BlockSpec((bsz, ct, c), lambda j: (0, j, 0))
    st_c = pl.BlockSpec((bsz, N_HEADS, HEAD_DIM, HEAD_DIM), lambda j: (0, 0, 0, 0))
    st_n = pl.BlockSpec((bsz, N_HEADS, HEAD_DIM), lambda j: (0, 0, 0))
    st_m = pl.BlockSpec((bsz, N_HEADS, 1), lambda j: (0, 0, 0))
    return pl.pallas_call(
        functools.partial(_mlstm_kernel, chunk=chunk, nchunks=nchunks, bsz=bsz),
        grid=grid,
        in_specs=[tok(D_MLSTM), tok(D_MLSTM), tok(D_MLSTM), tok(D_MLSTM), tok(8),
                  pl.BlockSpec((bsz, nchunks, 8, chunk), lambda j: (0, j, 0, 0)),
                  st_c, st_n, st_m,
                  pl.BlockSpec((1, D_MLSTM), lambda j: (0, 0))],
        out_specs=[tok(D_MLSTM), st_c, st_n, st_m],
        out_shape=[jax.ShapeDtypeStruct((bsz, seq, D_MLSTM), BF16),
                   jax.ShapeDtypeStruct((bsz, N_HEADS, HEAD_DIM, HEAD_DIM), F32),
                   jax.ShapeDtypeStruct((bsz, N_HEADS, HEAD_DIM), F32),
                   jax.ShapeDtypeStruct((bsz, N_HEADS, 1), F32)],
        scratch_shapes=[pltpu.VMEM((bsz, N_HEADS, HEAD_DIM, HEAD_DIM), F32),
                        pltpu.VMEM((bsz, N_HEADS, HEAD_DIM), F32),
                        pltpu.VMEM((bsz, N_HEADS, 1), F32)],
        compiler_params=_cparams(("arbitrary",)),
        name="mlstm",
    )(q, k, v, og, gcol, grow, c0, n0, m0, ng)


def _post_kernel(ym_ref, yc_ref, x_ref, wmo_ref, gx_ref, wxq_ref, mk_ref, mv_ref, wxo_ref,
                 gf_ref, wrt_ref, br_ref,
                 h2_ref, xn_ref, eid_ref, gate_ref, rank_ref, cnt_ref, *, tm, sub):
    n_mem = mk_ref.shape[0]
    per = tm // n_mem
    n_groups = max(1, tm // POST_GROUP_ROWS) if n_mem == 1 else 1
    rows = [slice(g * (tm // n_groups), (g + 1) * (tm // n_groups)) for g in range(n_groups)]
    each = lambda f: [f(g, r) for g, r in enumerate(rows)]

    mix = each(lambda g, r: _dot(ym_ref[r, :], wmo_ref[0:D_MLSTM, :])
               + _dot(yc_ref[r, :], wmo_ref[D_MLSTM:D_MODEL, :]))
    h1 = each(lambda g, r: x_ref[r, :] + mix[g])
    xq = each(lambda g, r: _dot(_rms(h1[g], gx_ref[...]).astype(BF16), wxq_ref[...]).astype(BF16))

    def streams(g):
        if n_mem == 1:
            return [(slice(None), 0)]
        return [(slice(m * per, (m + 1) * per), m) for m in range(n_mem)]

    cols = lambda hd: slice(hd * XHEAD_DIM, (hd + 1) * XHEAD_DIM)
    units = [(g, q, m, hd) for g in range(n_groups) for q, m in streams(g) for hd in range(N_XHEADS)]
    s = [_dot_nt(xq[g][q, cols(hd)], mk_ref[m, :, cols(hd)]) * (XHEAD_DIM ** -0.5)
         for g, q, m, hd in units]
    e = [jnp.exp(v - jnp.max(v, axis=-1, keepdims=True)) for v in s]
    p = [(v / jnp.sum(v, axis=-1, keepdims=True)).astype(BF16) for v in e]
    o = [_dot(v, mv_ref[m, :, cols(hd)]).astype(BF16) for v, (g, q, m, hd) in zip(p, units)]
    per_group = len(units) // n_groups
    o_rows = [jnp.concatenate(
        [jnp.concatenate(o[g * per_group + st * N_XHEADS:g * per_group + (st + 1) * N_XHEADS], axis=1)
         for st in range(per_group // N_XHEADS)], axis=0) for g in range(n_groups)]
    h2 = each(lambda g, r: h1[g] + _dot(o_rows[g], wxo_ref[...]))
    xn2_parts = each(lambda g, r: _rms(h2[g], gf_ref[...]).astype(BF16))
    for g, r in enumerate(rows):
        h2_ref[r, :] = h2[g]
        xn_ref[r, :] = xn2_parts[g]
    xn2 = jnp.concatenate(xn2_parts, axis=0)

    logits = _dot_nt(wrt_ref[...], xn2) + br_ref[...]
    eidx = lax.broadcasted_iota(I32, logits.shape, 0).astype(F32)
    work = logits
    vals, ids, hots = [], [], []
    for _ in range(TOP_K):
        mx = jnp.max(work, axis=0, keepdims=True)
        idx = jnp.min(jnp.where(work == mx, eidx, float(N_EXPERTS)), axis=0, keepdims=True)
        sel = eidx == idx
        vals.append(mx)
        ids.append(idx)
        hots.append(sel)
        work = jnp.where(sel, -jnp.inf, work)
    exps = [jnp.exp(v - vals[0]) for v in vals]
    denom = exps[0] + exps[1] + exps[2] + exps[3]

    picked = jnp.zeros(logits.shape, F32)
    for sel in hots:
        picked = picked + sel.astype(F32)
    shift = jnp.full((tm, tm), sub.bit_length() - 1, I32)
    tj = lax.broadcasted_iota(I32, (tm, tm), 0)
    tt = lax.broadcasted_iota(I32, (tm, tm), 1)
    same = lax.shift_right_logical(tj, shift) == lax.shift_right_logical(tt, shift)
    before = jnp.where(jnp.logical_and(tj < tt, same), 1.0, 0.0).astype(BF16)
    prior = _dot(picked.astype(BF16), before)
    for kk in range(TOP_K):
        eid_ref[kk:kk + 1, :] = ids[kk].astype(I32)
        gate_ref[kk:kk + 1, :] = exps[kk] / denom
        rank_ref[kk:kk + 1, :] = jnp.sum(jnp.where(hots[kk], prior, 0.0), axis=0,
                                         keepdims=True).astype(I32)
    for s in range(tm // sub):
        cnt_ref[s] = jnp.sum(picked[:, s * sub:(s + 1) * sub], axis=1, keepdims=True).astype(I32)


def _post(ym, yc, x, wmo, gx, wxq, mkb, mvb, wxo, gf, wrt, br, *, tm, sub):
    bsz, seq, _ = x.shape
    nj = seq // tm
    grid = (bsz, nj)
    nsub = tm // sub
    n_tiles = bsz * nj * nsub
    tok = lambda c: pl.BlockSpec((None, tm, c), lambda b, j: (b, j, 0))
    full = lambda shape: pl.BlockSpec(shape, lambda b, j: (0,) * len(shape))
    n_mem = mkb.shape[0] // bsz
    mem = pl.BlockSpec((n_mem, N_MEM, D_MODEL), lambda b, j: (b, 0, 0))
    sel = pl.BlockSpec((None, TOP_K, tm), lambda b, j: (b, 0, j))
    return pl.pallas_call(
        functools.partial(_post_kernel, tm=tm, sub=sub),
        grid=grid,
        in_specs=[tok(D_MLSTM), tok(D_CONV), tok(D_MODEL), full((D_MODEL, D_MODEL)),
                  full((1, D_MODEL)), full((D_MODEL, D_MODEL)), mem, mem,
                  full((D_MODEL, D_MODEL)), full((1, D_MODEL)), full((N_EXPERTS, D_MODEL)),
                  full((N_EXPERTS, 1))],
        out_specs=[tok(D_MODEL), tok(D_MODEL), sel, sel, sel,
                   pl.BlockSpec((nsub, N_EXPERTS, 1), lambda b, j: (b * nj + j, 0, 0))],
        out_shape=[jax.ShapeDtypeStruct((bsz, seq, D_MODEL), F32),
                   jax.ShapeDtypeStruct((bsz, seq, D_MODEL), BF16),
                   jax.ShapeDtypeStruct((bsz, TOP_K, seq), I32),
                   jax.ShapeDtypeStruct((bsz, TOP_K, seq), F32),
                   jax.ShapeDtypeStruct((bsz, TOP_K, seq), I32),
                   jax.ShapeDtypeStruct((n_tiles, N_EXPERTS, 1), I32)],
        compiler_params=_cparams(("arbitrary", "arbitrary")),
        name="post",
    )(ym, yc, x, wmo, gx, wxq, mkb, mvb, wxo, gf, wrt, br)


def _sorted_rows(n_tokens):
    return -(-(TOP_K * n_tokens + N_EXPERTS * (SUBLANES - 1)) // ROW_BLOCK) * ROW_BLOCK


def _dispatch_kernel(tab_ref, fill_ref, eid_ref, rank_ref, ls_ref, x_ref, eids_ref, ranks_ref, xs_ref,
                     xb_ref, srt, zero_s, pending, sems, *, n_ptiles, n_blocks):
    i = pl.program_id(0)
    cur = i % 2

    def rows_copy(src, src_row, dst_row, nrows, sl):
        return pltpu.make_async_copy(
            src.at[pl.ds(pl.multiple_of(src_row, SUBLANES), nrows), :],
            xb_ref.at[pl.ds(pl.multiple_of(dst_row, SUBLANES), nrows), :], sems.at[sl])

    def granule(src, src_row, dst_row, sl):
        return rows_copy(src, src_row, dst_row, SUBLANES, sl)

    def drain(count, sl, nrows=SUBLANES):
        def body(g, c):
            rows_copy(zero_s, 0, 0, nrows, sl).wait()
            return c
        lax.fori_loop(0, count, body, 0)

    @pl.when(i == 0)
    def _():
        pending[0] = 0
        pending[1] = 0

    def issue_list(count, dst_at, src_at, nrows):
        def body(h, c):
            for u in range(COPIES_PER_TRIP):
                g = COPIES_PER_TRIP * h + u

                @pl.when(g < count)
                def _():
                    rows_copy(srt.at[cur], tab_ref[src_at + g], tab_ref[dst_at + g], nrows,
                              cur).start(priority=u % 2)
            return c
        lax.fori_loop(0, (count + COPIES_PER_TRIP - 1) // COPIES_PER_TRIP, body, 0)

    def sort_and_move(eid, rank, x):
        ntok = x.shape[0]
        nrows = _sorted_rows(ntok)
        e_iota = lax.broadcasted_iota(I32, (N_EXPERTS, ntok), 0)
        s_iota = lax.broadcasted_iota(I32, (nrows, ntok), 0)
        seg_start = ls_ref[...]
        hit = None
        for kk in range(TOP_K):
            start = jnp.sum(jnp.where(e_iota == eid[kk:kk + 1, :], seg_start, 0.0),
                            axis=0, keepdims=True).astype(I32)
            match = s_iota == start + rank[kk:kk + 1, :]
            hit = match if hit is None else jnp.logical_or(hit, match)
        perm = jnp.where(hit, 1.0, 0.0).astype(BF16)
        srt[cur, 0:nrows, :] = _dot(perm, x).astype(BF16)
        drain(pending[0], 1 - cur, BIG_ROWS)
        drain(pending[1], 1 - cur)
        n_big, n_small = tab_ref[TAB_COUNTS], tab_ref[TAB_COUNTS + 1]
        issue_list(n_big, TAB_BIG_DST, TAB_BIG_SRC, BIG_ROWS)
        issue_list(n_small, TAB_SMALL_DST, TAB_SMALL_SRC, SUBLANES)
        pending[0] = n_big
        pending[1] = n_small

    @pl.when(i < n_ptiles)
    def _():
        sort_and_move(eid_ref[...], rank_ref[...], x_ref[...])

    @pl.when(i == n_ptiles)
    def _():
        sort_and_move(eids_ref[...], ranks_ref[...], xs_ref[...])
        drain(pending[0], cur, BIG_ROWS)
        drain(pending[1], cur)
        zero_s[...] = jnp.zeros_like(zero_s)
        for e in range(N_EXPERTS):
            dst = fill_ref[e]

            def zissue(g, c):
                granule(zero_s, 0, dst + g * SUBLANES, cur).start()
                return c
            lax.fori_loop(0, fill_ref[N_EXPERTS + e], zissue, 0)
        drain(fill_ref[2 * N_EXPERTS], cur)
        first_free = fill_ref[2 * N_EXPERTS + 1]

        def blk_copy(b):
            return pltpu.make_async_copy(
                zero_s, xb_ref.at[pl.ds(pl.multiple_of(b * ROW_BLOCK, ROW_BLOCK), ROW_BLOCK), :],
                sems.at[cur])

        def bissue(b, c):
            blk_copy(b).start()
            return c
        lax.fori_loop(first_free, n_blocks, bissue, 0)

        def bdrain(b, c):
            blk_copy(0).wait()
            return c
        lax.fori_loop(first_free, n_blocks, bdrain, 0)


def _dispatch(tab, fill, eid_p, rank_p, seg_start, xn_p, eid_s, rank_s, xn_s, *, tmd, n_blocks):
    n_ptiles = eid_p.shape[0]
    n_sample = xn_s.shape[0]
    last = n_ptiles - 1
    smem = lambda shape, imap: pl.BlockSpec(shape, imap, memory_space=pltpu.SMEM)
    return pl.pallas_call(
        functools.partial(_dispatch_kernel, n_ptiles=n_ptiles, n_blocks=n_blocks),
        grid=(n_ptiles + 1,),
        in_specs=[smem((TABLE_WIDTH,), lambda i: (i,)),
                  smem((TABLE_WIDTH,), lambda i: (0,)),
                  pl.BlockSpec((None, TOP_K, tmd), lambda i: (jnp.minimum(i, last), 0, 0)),
                  pl.BlockSpec((None, TOP_K, tmd), lambda i: (jnp.minimum(i, last), 0, 0)),
                  pl.BlockSpec((None, N_EXPERTS, 1), lambda i: (i, 0, 0)),
                  pl.BlockSpec((tmd, D_MODEL), lambda i: (jnp.minimum(i, last), 0)),
                  pl.BlockSpec((TOP_K, n_sample), lambda i: (0, 0)),
                  pl.BlockSpec((TOP_K, n_sample), lambda i: (0, 0)),
                  pl.BlockSpec((n_sample, D_MODEL), lambda i: (0, 0))],
        out_specs=pl.BlockSpec(memory_space=pl.ANY),
        out_shape=jax.ShapeDtypeStruct((n_blocks * ROW_BLOCK, D_MODEL), BF16),
        scratch_shapes=[pltpu.VMEM((2, _sorted_rows(tmd), D_MODEL), BF16),
                        pltpu.VMEM((ROW_BLOCK, D_MODEL), BF16), pltpu.SMEM((2,), I32),
                        pltpu.SemaphoreType.DMA((2,))],
        compiler_params=_cparams(("arbitrary",)),
        name="dispatch",
    )(tab, fill, eid_p, rank_p, seg_start, xn_p, eid_s, rank_s, xn_s)


def _expert_kernel(tab_ref, xb_ref, wu_hbm, bu_ref, wd_hbm, bd_ref, yb_ref,
                   wu_f, wd_f, wu_s, wd_s, xbuf, ybuf, wsem, xsem, ysem, *, n_blocks):
    e = pl.program_id(0)
    first = tab_ref[e]
    nblk = tab_ref[N_EXPERTS + e]
    n_used = tab_ref[2 * N_EXPERTS]

    def hbm_rows(b):
        return pl.ds(pl.multiple_of(b * ROW_BLOCK, ROW_BLOCK), ROW_BLOCK)

    def x_copy(b, sl):
        return pltpu.make_async_copy(xb_ref.at[hbm_rows(b), :], xbuf.at[sl], xsem.at[sl])

    def y_copy(b, sl):
        return pltpu.make_async_copy(ybuf.at[sl], yb_ref.at[hbm_rows(b), :], ysem.at[sl])

    def w_copies(ex, sl):
        return (pltpu.make_async_copy(wu_hbm.at[ex], wu_f.at[sl], wsem.at[0, sl]),
                pltpu.make_async_copy(wd_hbm.at[ex], wd_f.at[sl], wsem.at[1, sl]))

    wslot = e % 2

    @pl.when(e == 0)
    def _():
        for c in w_copies(0, 0):
            c.start()
        for b in range(ROW_LOOKAHEAD):
            @pl.when(b < n_used)
            def _():
                x_copy(b, b % ROW_RING).start()

    @pl.when(e + 1 < N_EXPERTS)
    def _():
        for c in w_copies(e + 1, 1 - wslot):
            c.start()

    for c in w_copies(e, wslot):
        c.wait()

    step = 128

    def cast(r, c):
        rows = pl.ds(pl.multiple_of(r * step, step), step)
        wu_s[rows, :] = wu_f[wslot, rows, :].astype(BF16)
        wd_s[rows, :] = wd_f[wslot, rows, :].astype(BF16)
        return c
    lax.fori_loop(0, D_MODEL // step, cast, 0)

    def arrive(b):
        sl = b % ROW_RING
        x_copy(b, sl).wait()

        @pl.when(b + ROW_LOOKAHEAD < n_used)
        def _():
            x_copy(b + ROW_LOOKAHEAD, (b + ROW_LOOKAHEAD) % ROW_RING).start()

        @pl.when(b >= ROW_RING)
        def _():
            y_copy(b, sl).wait()
        return sl

    def up(sl):
        return _dot(xbuf[sl], wu_s[...]) + bu_ref[...]

    def gated(hcat):
        glu = jnp.minimum(hcat[:, 0:D_FF], SWIGLU_LIMIT)
        lin = jnp.clip(hcat[:, D_FF:2 * D_FF], -SWIGLU_LIMIT, SWIGLU_LIMIT)
        return (glu * _sigmoid(SWIGLU_ALPHA * glu) * (lin + 1.0)).astype(BF16)

    def down(act, sl):
        ybuf[sl] = _dot(act, wd_s[...]) + bd_ref[...]

    def two_blocks(j, c):
        b0 = first + 2 * j
        slots = [arrive(b0), arrive(b0 + 1)]
        hidden = [up(sl) for sl in slots]
        acts = [gated(h) for h in hidden]
        for a, sl in zip(acts, slots):
            down(a, sl)
        y_copy(b0, slots[0]).start()
        y_copy(b0 + 1, slots[1]).start()
        return c
    lax.fori_loop(0, nblk // 2, two_blocks, 0)

    @pl.when(nblk % 2 == 1)
    def _():
        b = first + nblk - 1
        sl = arrive(b)
        down(gated(up(sl)), sl)
        y_copy(b, sl).start()

    @pl.when(e == N_EXPERTS - 1)
    def _():
        for k in range(ROW_RING):
            @pl.when(k < n_used)
            def _():
                y_copy(0, (n_used - 1 - k) % ROW_RING).wait()

        ybuf[0] = jnp.zeros((ROW_BLOCK, D_MODEL), F32)

        def zissue(b, c):
            y_copy(b, 0).start()
            return c
        lax.fori_loop(n_used, n_blocks, zissue, 0)

        def zdrain(b, c):
            y_copy(0, 0).wait()
            return c
        lax.fori_loop(n_used, n_blocks, zdrain, 0)


def _experts(tab, xb, w_up, b_up, w_down, b_down, *, n_blocks):
    w_map = lambda e, tab: (e, 0, 0)
    grid_spec = pltpu.PrefetchScalarGridSpec(
        num_scalar_prefetch=1,
        grid=(N_EXPERTS,),
        in_specs=[pl.BlockSpec(memory_space=pl.ANY),
                  pl.BlockSpec(memory_space=pl.ANY),
                  pl.BlockSpec((None, 1, 2 * D_FF), w_map),
                  pl.BlockSpec(memory_space=pl.ANY),
                  pl.BlockSpec((None, 1, D_MODEL), w_map)],
        out_specs=pl.BlockSpec(memory_space=pl.ANY),
        scratch_shapes=[pltpu.VMEM((2, D_MODEL, 2 * D_FF), F32), pltpu.VMEM((2, D_FF, D_MODEL), F32),
                        pltpu.VMEM((D_MODEL, 2 * D_FF), BF16), pltpu.VMEM((D_FF, D_MODEL), BF16),
                        pltpu.VMEM((ROW_RING, ROW_BLOCK, D_MODEL), BF16),
                        pltpu.VMEM((ROW_RING, ROW_BLOCK, D_MODEL), F32),
                        pltpu.SemaphoreType.DMA((2, 2)),
                        pltpu.SemaphoreType.DMA((ROW_RING,)), pltpu.SemaphoreType.DMA((ROW_RING,))],
    )
    return pl.pallas_call(
        functools.partial(_expert_kernel, n_blocks=n_blocks),
        grid_spec=grid_spec,
        out_shape=jax.ShapeDtypeStruct((n_blocks * ROW_BLOCK, D_MODEL), F32),
        compiler_params=_cparams(("arbitrary",)),
        name="experts",
    )(tab, xb, w_up, b_up, w_down, b_down)


def _split_bf16(a):
    hi = a.astype(BF16)
    return hi, (a - hi.astype(F32)).astype(BF16)


def _combine_kernel(tab_ref, nxt_ref, slot_ref, gate_ref, h2_ref, gfin_ref, yb_ref, y_ref, buf, sems,
                    *, tm):
    i = pl.program_id(0)
    n = pl.num_programs(0)
    cur = i % 2
    nrows = buf.shape[1]

    def rows_copy(src_row, sl, dst_row, nrows):
        return pltpu.make_async_copy(
            yb_ref.at[pl.ds(pl.multiple_of(src_row, SUBLANES), nrows), :],
            buf.at[sl, pl.ds(pl.multiple_of(dst_row, SUBLANES), nrows), :], sems.at[sl])

    def gather(t_ref, sl):
        def fetch(count, yb_at, tile_at, nrows):
            def body(h, c):
                for u in range(COPIES_PER_TRIP):
                    g = COPIES_PER_TRIP * h + u

                    @pl.when(g < count)
                    def _():
                        rows_copy(t_ref[yb_at + g], sl, t_ref[tile_at + g], nrows).start(priority=u % 2)
                return c
            lax.fori_loop(0, (count + COPIES_PER_TRIP - 1) // COPIES_PER_TRIP, body, 0)
        fetch(t_ref[TAB_COUNTS], TAB_BIG_DST, TAB_BIG_SRC, BIG_ROWS)
        fetch(t_ref[TAB_COUNTS + 1], TAB_SMALL_DST, TAB_SMALL_SRC, SUBLANES)

    @pl.when(i == 0)
    def _():
        buf[...] = jnp.zeros_like(buf)
        gather(tab_ref, 0)

    @pl.when(i + 1 < n)
    def _():
        gather(nxt_ref, 1 - cur)

    def drain(count, nrows):
        def body(g, c):
            rows_copy(0, cur, 0, nrows).wait()
            return c
        lax.fori_loop(0, count, body, 0)
    drain(tab_ref[TAB_COUNTS], BIG_ROWS)
    drain(tab_ref[TAB_COUNTS + 1], SUBLANES)

    s_iota = lax.broadcasted_iota(I32, (tm, nrows), 1)
    wgt = jnp.zeros((tm, nrows), F32)
    for kk in range(TOP_K):
        wgt = wgt + jnp.where(s_iota == slot_ref[:, kk:kk + 1], gate_ref[:, kk:kk + 1], 0.0)
    w_hi, w_lo = _split_bf16(wgt)
    rows = buf[cur].astype(BF16)
    acc = _dot(w_hi, rows) + _dot(w_lo, rows)
    y_ref[...] = _rms(h2_ref[...] + acc, gfin_ref[...])


def _combine(tab, slot_col, gate_col, h2, gfin, yb, *, tm):
    ntok = h2.shape[0]
    n = ntok // tm
    smem = lambda imap: pl.BlockSpec((TABLE_WIDTH,), imap, memory_space=pltpu.SMEM)
    return pl.pallas_call(
        functools.partial(_combine_kernel, tm=tm),
        grid=(n,),
        in_specs=[smem(lambda i: (i,)), smem(lambda i: (jnp.minimum(i + 1, n - 1),)),
                  pl.BlockSpec((tm, TOP_K), lambda i: (i, 0)),
                  pl.BlockSpec((tm, TOP_K), lambda i: (i, 0)),
                  pl.BlockSpec((tm, D_MODEL), lambda i: (i, 0)),
                  pl.BlockSpec((1, D_MODEL), lambda i: (0, 0)),
                  pl.BlockSpec(memory_space=pl.ANY)],
        out_specs=pl.BlockSpec((tm, D_MODEL), lambda i: (i, 0)),
        out_shape=jax.ShapeDtypeStruct((ntok, D_MODEL), F32),
        scratch_shapes=[pltpu.VMEM((2, _sorted_rows(tm), D_MODEL), F32),
                        pltpu.SemaphoreType.DMA((2,))],
        compiler_params=_cparams(("arbitrary",)),
        name="combine",
    )(tab, tab, slot_col, gate_col, h2, gfin, yb)


def _copy_tables(seg, seg_src, seg_dst):
    experts = jnp.arange(N_EXPERTS, dtype=I32)

    def flatten(counts, bound, src0, dst0, step):
        ends = jnp.cumsum(counts, axis=1)
        idx = jnp.arange(bound, dtype=I32)
        owner = jnp.sum(ends[:, None, :] <= idx[None, :, None], axis=-1)
        pick = lambda a: jnp.sum(jnp.where(owner[..., None] == experts, a[:, None, :], 0), axis=-1)
        off = (idx[None, :] - pick(ends - counts)) * step
        valid = idx[None, :] < ends[:, -1:]
        return (jnp.where(valid, pick(dst0) + off, 0), jnp.where(valid, pick(src0) + off, 0),
                ends[:, -1:])

    n_big = seg // BIG_ROWS
    n_small = (seg - n_big * BIG_ROWS) // SUBLANES
    b_dst, b_src, b_n = flatten(n_big, MAX_BIG, seg_src, seg_dst, BIG_ROWS)
    s_dst, s_src, s_n = flatten(n_small, MAX_SMALL, seg_src + n_big * BIG_ROWS,
                                seg_dst + n_big * BIG_ROWS, SUBLANES)
    tab = jnp.concatenate([b_dst, b_src, s_dst, s_src, b_n, s_n], axis=1)
    return jnp.pad(tab, ((0, 0), (0, TABLE_WIDTH - tab.shape[1]))).astype(I32).reshape(-1)


def _tiles(a, tile):
    bsz, kk, seq = a.shape
    return a.reshape(bsz, kk, seq // tile, tile).transpose(0, 2, 1, 3).reshape(-1, kk, tile)


def _path(x, cbuf, c0, n0, m0, mkb, mvb, wts, *, tm_in, ct, chunk, tm_post, sub, fold):
    bsz, seq, _ = x.shape
    if fold:
        x = x.reshape(1, bsz * seq, D_MODEL)
        q, k, v, og, gcol, grow, yc, nbuf = _inproj(
            x, wts["g_mix"], wts["wq"], wts["wg"], wts["wgt"], wts["wc"], wts["bg"], wts["bgt"],
            wts["cw"], cbuf, tm=bsz * seq, chunk=bsz * seq)
        q, k, v, og, gcol = (a.reshape(bsz, seq, a.shape[-1]) for a in (q, k, v, og, gcol))
        grow = grow.reshape(2 * N_HEADS, bsz, seq).transpose(1, 0, 2)[:, None]
    else:
        q, k, v, og, gcol, grow, yc, nbuf = _inproj(
            x, wts["g_mix"], wts["wq"], wts["wg"], wts["wgt"], wts["wc"], wts["bg"], wts["bgt"],
            wts["cw"], cbuf, tm=tm_in, chunk=chunk)
    ym, c1, n1, m1 = _mlstm(q, k, v, og, gcol, grow, c0, n0, m0, wts["ng"], ct=ct, chunk=chunk)
    if fold:
        ym = ym.reshape(1, bsz * seq, D_MLSTM)
    h2, xn, eid, gate, rank, cnt = _post(
        ym, yc, x, wts["wmo"], wts["g_x"], wts["wxq"], mkb, mvb, wts["wxo"], wts["g_ffn"],
        wts["wrt"], wts["br"], tm=tm_post, sub=sub)
    return dict(h2=h2, xn=xn, eid=eid, gate=gate, rank=rank, cnt=cnt[:, :, 0],
                c1=c1, n1=n1, m1=m1[..., 0], nbuf=nbuf)


def kernel(x_prompt, x_sample, state_mlstm_c, state_mlstm_n, state_mlstm_m, state_conv, cache_mem_k, cache_mem_v, mem_prompt, norm_mix_g, w_in, b_gate, mlstm_norm_g, conv_w, w_mix_out, norm_x_g, norm_mem_g, w_xq, w_xk, w_xv, w_xo, norm_ffn_g, w_router, b_router, w_up, b_up, w_down, b_down, norm_final_g):
    bp, lp, _ = x_prompt.shape
    bs, ls, _ = x_sample.shape
    l = 0
    row = lambda a: a.reshape(1, -1)

    wi = w_in[l]
    gate_cols = wi[:, 4 * D_MLSTM:4 * D_MLSTM + 2 * N_HEADS]
    wts = dict(
        g_mix=row(norm_mix_g[l]),
        wq=wi[:, 0:4 * D_MLSTM].astype(BF16),
        wg=jnp.pad(gate_cols, ((0, 0), (0, 128 - 2 * N_HEADS))).astype(BF16),
        wgt=gate_cols.T.astype(BF16),
        wc=wi[:, 4 * D_MLSTM + 2 * N_HEADS:].astype(BF16),
        bg=row(b_gate[l]), bgt=b_gate[l].reshape(-1, 1),
        cw=conv_w[l], ng=row(mlstm_norm_g[l]),
        wmo=w_mix_out[l].astype(BF16), g_x=row(norm_x_g[l]), wxq=w_xq[l].astype(BF16),
        wxo=w_xo[l].astype(BF16), g_ffn=row(norm_ffn_g[l]),
        wrt=w_router[l].T.astype(BF16), br=b_router[l].reshape(-1, 1),
    )

    mk, mv, mkb, mvb = _memkv(mem_prompt.reshape(bp * N_MEM, D_MODEL), row(norm_mem_g[l]),
                              w_xk[l].astype(BF16), w_xv[l].astype(BF16))
    zeros = lambda *s: jnp.zeros(s, F32)
    tm_post, tmd = 512, 256
    assert _sorted_rows(tmd) // BIG_ROWS <= MAX_BIG
    pr = _path(x_prompt, zeros(bp, CONV_WIDTH - 1, D_CONV), zeros(bp, N_HEADS, HEAD_DIM, HEAD_DIM),
               zeros(bp, N_HEADS, HEAD_DIM), zeros(bp, N_HEADS, 1),
               mkb.reshape(bp, N_MEM, D_MODEL), mvb.reshape(bp, N_MEM, D_MODEL), wts,
               tm_in=512, ct=512, chunk=CHUNK, tm_post=tm_post, sub=tmd, fold=False)
    sa = _path(x_sample, state_conv[l], state_mlstm_c[l], state_mlstm_n[l],
               state_mlstm_m[l][..., None],
               cache_mem_k[l].reshape(bs, N_MEM, D_MODEL).astype(BF16),
               cache_mem_v[l].reshape(bs, N_MEM, D_MODEL).astype(BF16), wts,
               tm_in=ls, ct=ls, chunk=min(CHUNK, ls), tm_post=bs * ls, sub=bs * ls, fold=True)

    n_ptok, n_stok = bp * lp, bs * ls
    cnt = jnp.concatenate([pr["cnt"], sa["cnt"]], axis=0)
    n_tiles = cnt.shape[0]
    n_blocks = -(-(TOP_K * (n_ptok + n_stok) + n_tiles * N_EXPERTS * (SUBLANES - 1)) // ROW_BLOCK) \
        + N_EXPERTS
    seg = (cnt + SUBLANES - 1) // SUBLANES * SUBLANES
    seg_src = jnp.cumsum(seg, axis=1) - seg
    tot = jnp.sum(seg, axis=0)
    blocks_e = (tot + ROW_BLOCK - 1) // ROW_BLOCK
    padded = blocks_e * ROW_BLOCK
    pstart = jnp.cumsum(padded) - padded
    bend = jnp.cumsum(blocks_e)
    n_used = bend[-1]
    exp_tab = jnp.concatenate([bend - blocks_e, blocks_e, n_used[None]]).astype(I32)
    seg_dst = pstart[None, :] + jnp.cumsum(seg, axis=0) - seg

    tab = _copy_tables(seg, seg_src, seg_dst)
    fgran = (padded - tot) // SUBLANES
    fill = jnp.concatenate([pstart + tot, fgran, jnp.sum(fgran)[None], n_used[None]])
    fill = jnp.pad(fill, (0, TABLE_WIDTH - fill.shape[0])).astype(I32)

    def lookup(table, eid):
        hit = eid[..., None] == jnp.arange(N_EXPERTS, dtype=I32)
        return jnp.sum(jnp.where(hit, table[:, None, None, :], 0), axis=-1)

    eid_pt, rank_pt = _tiles(pr["eid"], tmd), _tiles(pr["rank"], tmd)
    eid_st = sa["eid"].transpose(1, 0, 2).reshape(1, TOP_K, n_stok)
    rank_st = sa["rank"].transpose(1, 0, 2).reshape(1, TOP_K, n_stok)
    slot_p = (lookup(seg_src[:-1], eid_pt) + rank_pt).transpose(0, 2, 1).reshape(n_ptok, TOP_K)
    slot_s = (lookup(seg_src[-1:], eid_st) + rank_st).transpose(0, 2, 1).reshape(n_stok, TOP_K)

    xb = _dispatch(tab, fill, eid_pt, rank_pt, seg_src.astype(F32)[..., None],
                   pr["xn"].reshape(n_ptok, D_MODEL), eid_st[0], rank_st[0],
                   sa["xn"].reshape(n_stok, D_MODEL), tmd=tmd, n_blocks=n_blocks)
    yb = _experts(exp_tab, xb, w_up[l], b_up[l][:, None, :], w_down[l], b_down[l][:, None, :],
                  n_blocks=n_blocks)

    gfin = row(norm_final_g)
    split = (n_tiles - 1) * TABLE_WIDTH
    y_p = _combine(tab[:split], slot_p.astype(I32), pr["gate"].transpose(0, 2, 1).reshape(n_ptok, TOP_K),
                   pr["h2"].reshape(n_ptok, D_MODEL), gfin, yb, tm=tmd)
    y_s = _combine(tab[split:], slot_s.astype(I32), sa["gate"].transpose(0, 2, 1).reshape(n_stok, TOP_K),
                   sa["h2"].reshape(n_stok, D_MODEL), gfin, yb, tm=n_stok)

    lead = lambda a: a[None]
    return (y_p.reshape(bp, lp, D_MODEL), y_s.reshape(bs, ls, D_MODEL),
            lead(pr["c1"]), lead(pr["n1"]), lead(pr["m1"]), lead(pr["nbuf"]),
            lead(mk.reshape(bp, N_MEM, N_XHEADS, XHEAD_DIM)),
            lead(mv.reshape(bp, N_MEM, N_XHEADS, XHEAD_DIM)),
            lead(sa["c1"]), lead(sa["n1"]), lead(sa["m1"]), lead(sa["nbuf"]))
```

```python
import functools

import jax
import jax.numpy as jnp
from jax import lax
from jax.experimental import pallas as pl
from jax.experimental.pallas import tpu as pltpu

F32 = jnp.float32
BF16 = jnp.bfloat16
I32 = jnp.int32

D_MODEL = 1024
N_HEADS = 4
HEAD_DIM = 128
D_MLSTM = N_HEADS * HEAD_DIM
D_CONV = D_MODEL - D_MLSTM
CONV_WIDTH = 3
CHUNK = 64
N_MEM = 256
N_XHEADS = 4
XHEAD_DIM = D_MODEL // N_XHEADS
N_EXPERTS = 32
TOP_K = 4
D_FF = D_MODEL
SWIGLU_LIMIT = 7.0
SWIGLU_ALPHA = 1.702
EPS = 1e-5

SUBLANES = 8
BIG_ROWS = 4 * SUBLANES
COPIES_PER_TRIP = 4
MAX_BIG, MAX_SMALL = 64, 3 * N_EXPERTS
TAB_BIG_DST, TAB_BIG_SRC = 0, MAX_BIG
TAB_SMALL_DST, TAB_SMALL_SRC = 2 * MAX_BIG, 2 * MAX_BIG + MAX_SMALL
TAB_COUNTS = 2 * MAX_BIG + 2 * MAX_SMALL
TABLE_WIDTH = 512
POST_GROUP_ROWS = 512
ROW_BLOCK = 256
ROW_LOOKAHEAD = 3
ROW_RING = ROW_LOOKAHEAD + 2
VMEM_LIMIT = 56 * 1024 * 1024


def _cparams(sem):
    return pltpu.CompilerParams(dimension_semantics=sem, vmem_limit_bytes=VMEM_LIMIT)


def _rms(x, g):
    return x * lax.rsqrt(jnp.mean(x * x, axis=-1, keepdims=True) + EPS) * g


def _log_sigmoid(x):
    return -(jnp.maximum(-x, 0.0) + jnp.log1p(jnp.exp(-jnp.abs(x))))


def _sigmoid(x):
    return 1.0 / (1.0 + jnp.exp(-x))


def _dot(a, b):
    return jnp.dot(a, b, preferred_element_type=F32)


def _dot_nt(a, b):
    return lax.dot_general(a, b, (((1,), (1,)), ((), ())), preferred_element_type=F32)


def _dot_tn(a, b):
    return lax.dot_general(a, b, (((0,), (0,)), ((), ())), preferred_element_type=F32)


def _memkv_kernel(mem_ref, g_ref, wk_ref, wv_ref, mk_ref, mv_ref, mkb_ref, mvb_ref):
    mn = _rms(mem_ref[...], g_ref[...]).astype(BF16)
    mk = _dot(mn, wk_ref[...])
    mv = _dot(mn, wv_ref[...])
    mk_ref[...] = mk
    mv_ref[...] = mv
    mkb_ref[...] = mk.astype(BF16)
    mvb_ref[...] = mv.astype(BF16)


def _memkv(mem2d, g, wk, wv):
    rows = mem2d.shape[0]
    tm = N_MEM
    row_spec = pl.BlockSpec((tm, D_MODEL), lambda i: (i, 0))
    full = lambda shape: pl.BlockSpec(shape, lambda i: (0,) * len(shape))
    return pl.pallas_call(
        _memkv_kernel,
        grid=(rows // tm,),
        in_specs=[row_spec, full((1, D_MODEL)), full((D_MODEL, D_MODEL)), full((D_MODEL, D_MODEL))],
        out_specs=[row_spec, row_spec, row_spec, row_spec],
        out_shape=[jax.ShapeDtypeStruct((rows, D_MODEL), F32)] * 2
        + [jax.ShapeDtypeStruct((rows, D_MODEL), BF16)] * 2,
        compiler_params=_cparams(("arbitrary",)),
        name="memkv",
    )(mem2d, g, wk, wv)


def _inproj_kernel(x_ref, g_ref, wq_ref, wg_ref, wgt_ref, wc_ref, bg_ref, bgt_ref, cw_ref, cbuf_ref,
                   q_ref, k_ref, v_ref, og_ref, gcol_ref, grow_ref, yc_ref, nbuf_ref,
                   carry_ref, *, tm, chunk):
    j = pl.program_id(1)
    streams = cbuf_ref.shape[0]
    per = tm // streams

    @pl.when(j == 0)
    def _():
        carry_ref[0:2, :] = cbuf_ref[0]

    xb = _rms(x_ref[...], g_ref[...]).astype(BF16)

    p = _dot(xb, wq_ref[...])
    q_ref[...] = p[:, 0:D_MLSTM].astype(BF16)
    k_ref[...] = (p[:, D_MLSTM:2 * D_MLSTM] * (HEAD_DIM ** -0.5)).astype(BF16)
    v_ref[...] = p[:, 2 * D_MLSTM:3 * D_MLSTM].astype(BF16)
    og_ref[...] = _sigmoid(p[:, 3 * D_MLSTM:4 * D_MLSTM])

    gc = _dot(xb, wg_ref[...])[:, 0:2 * N_HEADS] + bg_ref[...]
    col = lax.broadcasted_iota(I32, gc.shape, 1)
    gcol_ref[...] = jnp.where(col < N_HEADS, gc, _log_sigmoid(gc))
    gr = _dot_nt(wgt_ref[...], xb) + bgt_ref[...]
    row = lax.broadcasted_iota(I32, gr.shape, 0)
    gr = jnp.where(row < N_HEADS, gr, _log_sigmoid(gr))
    for c in range(tm // chunk):
        grow_ref[c] = gr[:, c * chunk:(c + 1) * chunk]

    pc = _dot(xb, wc_ref[...])
    cb = pc[:, 0:D_CONV]
    u = pc[:, D_CONV:2 * D_CONV] * pc[:, 2 * D_CONV:3 * D_CONV]
    rid = lax.broadcasted_iota(I32, u.shape, 0)
    um1 = pltpu.roll(u, 1, 0)
    um2 = pltpu.roll(u, 2, 0)
    for m in range(streams):
        prev = carry_ref if streams == 1 else cbuf_ref.at[m]
        c0, c1 = prev[0:1, :], prev[1:2, :]
        um1 = jnp.where(rid == m * per, c1, um1)
        um2 = jnp.where(rid == m * per, c0, jnp.where(rid == m * per + 1, c1, um2))
        nbuf_ref[m] = u[(m + 1) * per - 2:(m + 1) * per, :]
    yc = cw_ref[0:1, :] * um2 + cw_ref[1:2, :] * um1 + cw_ref[2:3, :] * u
    yc_ref[...] = (cb * yc).astype(BF16)
    carry_ref[0:2, :] = u[tm - 2:tm, :]


def _inproj(x, g, wq, wg, wgt, wc, bg, bgt, cw, cbuf, *, tm, chunk):
    bsz, seq, _ = x.shape
    grid = (bsz, seq // tm)
    tok = lambda c: pl.BlockSpec((None, tm, c), lambda b, j: (b, j, 0))
    full = lambda shape: pl.BlockSpec(shape, lambda b, j: (0,) * len(shape))
    nck = tm // chunk
    streams = cbuf.shape[0] // bsz
    assert streams == 1 or tm == seq
    conv_state = pl.BlockSpec((streams, 2, D_CONV), lambda b, j: (b, 0, 0))
    return pl.pallas_call(
        functools.partial(_inproj_kernel, tm=tm, chunk=chunk),
        grid=grid,
        in_specs=[tok(D_MODEL), full((1, D_MODEL)), full((D_MODEL, 4 * D_MLSTM)),
                  full((D_MODEL, 128)), full((8, D_MODEL)), full((D_MODEL, 3 * D_CONV)),
                  full((1, 8)), full((8, 1)), full((CONV_WIDTH, D_CONV)), conv_state],
        out_specs=[tok(D_MLSTM), tok(D_MLSTM), tok(D_MLSTM), tok(D_MLSTM), tok(8),
                   pl.BlockSpec((None, nck, 8, chunk), lambda b, j: (b, j, 0, 0)),
                   tok(D_CONV), conv_state],
        out_shape=[jax.ShapeDtypeStruct((bsz, seq, D_MLSTM), BF16)] * 3
        + [jax.ShapeDtypeStruct((bsz, seq, D_MLSTM), F32),
           jax.ShapeDtypeStruct((bsz, seq, 8), F32),
           jax.ShapeDtypeStruct((bsz, seq // chunk, 8, chunk), F32),
           jax.ShapeDtypeStruct((bsz, seq, D_CONV), BF16),
           jax.ShapeDtypeStruct(cbuf.shape, F32)],
        scratch_shapes=[pltpu.VMEM((8, D_CONV), F32)],
        compiler_params=_cparams(("arbitrary", "arbitrary")),
        name="inproj",
    )(x, g, wq, wg, wgt, wc, bg, bgt, cw, cbuf)


def _mlstm_kernel(q_ref, k_ref, v_ref, og_ref, gc_ref, gr_ref, c0_ref, n0_ref, m0_ref, ng_ref,
                  ym_ref, c1_ref, n1_ref, m1_ref, c_s, n_s, m_s, *, chunk, nchunks, bsz):
    j = pl.program_id(0)

    @pl.when(j == 0)
    def _():
        c_s[...] = c0_ref[...]
        n_s[...] = n0_ref[...]
        m_s[...] = m0_ref[...]

    ti = lax.broadcasted_iota(I32, (chunk, chunk), 0)
    ji = lax.broadcasted_iota(I32, (chunk, chunk), 1)
    causal = ji <= ti

    def body(ci, carry):
        r0 = pl.multiple_of(ci * chunk, chunk)
        rows = pl.ds(r0, chunk)
        chains = [(b, h) for b in range(bsz) for h in range(N_HEADS)]
        cols = lambda h: slice(h * HEAD_DIM, (h + 1) * HEAD_DIM)
        each = lambda f: [f(n, b, h) for n, (b, h) in enumerate(chains)]
        q = lambda b, h: q_ref[b, rows, cols(h)]
        k = lambda b, h: k_ref[b, rows, cols(h)]
        v = lambda b, h: v_ref[b, rows, cols(h)]
        gcs = [gc_ref[b, rows, :] for b in range(bsz)]
        grs = [gr_ref[b, ci] for b in range(bsz)]
        li_c = each(lambda n, b, h: gcs[b][:, h:h + 1])
        lf_c = each(lambda n, b, h: gcs[b][:, N_HEADS + h:N_HEADS + h + 1])
        li_r = each(lambda n, b, h: grs[b][h:h + 1, :])
        lf_r = each(lambda n, b, h: grs[b][N_HEADS + h:N_HEADS + h + 1, :])
        m_prev = each(lambda n, b, h: m_s[b, h:h + 1, :])

        b_c = each(lambda n, b, h: jnp.sum(jnp.where(causal, lf_r[n], 0.0), axis=1, keepdims=True))
        b_r = each(lambda n, b, h: jnp.sum(jnp.where(ti <= ji, lf_c[n], 0.0), axis=0, keepdims=True))
        dmat = each(lambda n, b, h: jnp.where(causal, b_c[n] - b_r[n] + li_r[n], -jnp.inf))
        dmax = each(lambda n, b, h: jnp.max(dmat[n], axis=1, keepdims=True))
        inter = each(lambda n, b, h: b_c[n] + m_prev[n])
        m_t = each(lambda n, b, h: jnp.maximum(inter[n], dmax[n]))
        w_inter = each(lambda n, b, h: jnp.exp(inter[n] - m_t[n]))
        s = each(lambda n, b, h: _dot_nt(q(b, h), k(b, h)) * jnp.exp(dmat[n] - m_t[n]))
        qc = each(lambda n, b, h: _dot(q(b, h), c_s[b, h].astype(BF16)))
        sv = each(lambda n, b, h: _dot(s[n].astype(BF16), v(b, h)))
        qn = each(lambda n, b, h: jnp.sum(q(b, h).astype(F32) * n_s[b, h:h + 1, :], axis=1,
                                          keepdims=True))
        den = each(lambda n, b, h: w_inter[n] * qn[n] + jnp.sum(s[n], axis=1, keepdims=True))
        hh = each(lambda n, b, h: (w_inter[n] * qc[n] + sv[n])
                  / jnp.maximum(jnp.abs(den[n]), jnp.exp(-m_t[n])))

        m_new = each(lambda n, b, h: m_t[n][chunk - 1:chunk, :])
        b_last = each(lambda n, b, h: b_c[n][chunk - 1:chunk, :])
        decay = each(lambda n, b, h: jnp.exp(b_last[n] + m_prev[n] - m_new[n]))
        kw = each(lambda n, b, h: k(b, h).astype(F32)
                  * jnp.exp(b_last[n] - b_c[n] + li_c[n] - m_new[n]))
        kv = each(lambda n, b, h: _dot_tn(kw[n].astype(BF16), v(b, h)))
        for n, (b, h) in enumerate(chains):
            c_s[b, h] = decay[n] * c_s[b, h] + kv[n]
            n_s[b, h:h + 1, :] = decay[n] * n_s[b, h:h + 1, :] + jnp.sum(kw[n], axis=0, keepdims=True)
            m_s[b, h:h + 1, :] = m_new[n]

        hn = each(lambda n, b, h: hh[n] * lax.rsqrt(jnp.mean(hh[n] * hh[n], axis=1, keepdims=True) + EPS)
                  * ng_ref[:, cols(h)])
        for n, (b, h) in enumerate(chains):
            ym_ref[b, rows, cols(h)] = (hn[n] * og_ref[b, rows, cols(h)]).astype(BF16)
        return carry

    lax.fori_loop(0, nchunks, body, 0)

    @pl.when(j == pl.num_programs(0) - 1)
    def _():
        c1_ref[...] = c_s[...]
        n1_ref[...] = n_s[...]
        m1_ref[...] = m_s[...]


def _mlstm(q, k, v, og, gcol, grow, c0, n0, m0, ng, *, ct, chunk):
    bsz, seq, _ = q.shape
    nchunks = ct // chunk
    grid = (seq // ct,)
    tok = lambda c: pl.BlockSpec((bsz, ct, c), lambda j: (0, j, 0))
    st_c = pl.BlockSpec((bsz, N_HEADS, HEAD_DIM, HEAD_DIM), lambda j: (0, 0, 0, 0))
    st_n = pl.BlockSpec((bsz, N_HEADS, HEAD_DIM), lambda j: (0, 0, 0))
    st_m = pl.BlockSpec((bsz, N_HEADS, 1), lambda j: (0, 0, 0))
    return pl.pallas_call(
        functools.partial(_mlstm_kernel, chunk=chunk, nchunks=nchunks, bsz=bsz),
        grid=grid,
        in_specs=[tok(D_MLSTM), tok(D_MLSTM), tok(D_MLSTM), tok(D_MLSTM), tok(8),
                  pl.BlockSpec((bsz, nchunks, 8, chunk), lambda j: (0, j, 0, 0)),
                  st_c, st_n, st_m,
                  pl.BlockSpec((1, D_MLSTM), lambda j: (0, 0))],
        out_specs=[tok(D_MLSTM), st_c, st_n, st_m],
        out_shape=[jax.ShapeDtypeStruct((bsz, seq, D_MLSTM), BF16),
                   jax.ShapeDtypeStruct((bsz, N_HEADS, HEAD_DIM, HEAD_DIM), F32),
                   jax.ShapeDtypeStruct((bsz, N_HEADS, HEAD_DIM), F32),
                   jax.ShapeDtypeStruct((bsz, N_HEADS, 1), F32)],
        scratch_shapes=[pltpu.VMEM((bsz, N_HEADS, HEAD_DIM, HEAD_DIM), F32),
                        pltpu.VMEM((bsz, N_HEADS, HEAD_DIM), F32),
                        pltpu.VMEM((bsz, N_HEADS, 1), F32)],
        compiler_params=_cparams(("arbitrary",)),
        name="mlstm",
    )(q, k, v, og, gcol, grow, c0, n0, m0, ng)


def _post_kernel(ym_ref, yc_ref, x_ref, wmo_ref, gx_ref, wxq_ref, mk_ref, mv_ref, wxo_ref,
                 gf_ref, wrt_ref, br_ref,
                 h2_ref, xn_ref, eid_ref, gate_ref, rank_ref, cnt_ref, *, tm, sub):
    n_mem = mk_ref.shape[0]
    per = tm // n_mem
    n_groups = max(1, tm // POST_GROUP_ROWS) if n_mem == 1 else 1
    rows = [slice(g * (tm // n_groups), (g + 1) * (tm // n_groups)) for g in range(n_groups)]
    each = lambda f: [f(g, r) for g, r in enumerate(rows)]

    mix = each(lambda g, r: _dot(ym_ref[r, :], wmo_ref[0:D_MLSTM, :])
               + _dot(yc_ref[r, :], wmo_ref[D_MLSTM:D_MODEL, :]))
    h1 = each(lambda g, r: x_ref[r, :] + mix[g])
    xq = each(lambda g, r: _dot(_rms(h1[g], gx_ref[...]).astype(BF16), wxq_ref[...]).astype(BF16))

    def streams(g):
        if n_mem == 1:
            return [(slice(None), 0)]
        return [(slice(m * per, (m + 1) * per), m) for m in range(n_mem)]

    cols = lambda hd: slice(hd * XHEAD_DIM, (hd + 1) * XHEAD_DIM)
    units = [(g, q, m, hd) for g in range(n_groups) for q, m in streams(g) for hd in range(N_XHEADS)]
    s = [_dot_nt(xq[g][q, cols(hd)], mk_ref[m, :, cols(hd)]) * (XHEAD_DIM ** -0.5)
         for g, q, m, hd in units]
    e = [jnp.exp(v - jnp.max(v, axis=-1, keepdims=True)) for v in s]
    p = [(v / jnp.sum(v, axis=-1, keepdims=True)).astype(BF16) for v in e]
    o = [_dot(v, mv_ref[m, :, cols(hd)]).astype(BF16) for v, (g, q, m, hd) in zip(p, units)]
    per_group = len(units) // n_groups
    o_rows = [jnp.concatenate(
        [jnp.concatenate(o[g * per_group + st * N_XHEADS:g * per_group + (st + 1) * N_XHEADS], axis=1)
         for st in range(per_group // N_XHEADS)], axis=0) for g in range(n_groups)]
    h2 = each(lambda g, r: h1[g] + _dot(o_rows[g], wxo_ref[...]))
    xn2_parts = each(lambda g, r: _rms(h2[g], gf_ref[...]).astype(BF16))
    for g, r in enumerate(rows):
        h2_ref[r, :] = h2[g]
        xn_ref[r, :] = xn2_parts[g]
    xn2 = jnp.concatenate(xn2_parts, axis=0)

    logits = _dot_nt(wrt_ref[...], xn2) + br_ref[...]
    eidx = lax.broadcasted_iota(I32, logits.shape, 0).astype(F32)
    work = logits
    vals, ids, hots = [], [], []
    for _ in range(TOP_K):
        mx = jnp.max(work, axis=0, keepdims=True)
        idx = jnp.min(jnp.where(work == mx, eidx, float(N_EXPERTS)), axis=0, keepdims=True)
        sel = eidx == idx
        vals.append(mx)
        ids.append(idx)
        hots.append(sel)
        work = jnp.where(sel, -jnp.inf, work)
    exps = [jnp.exp(v - vals[0]) for v in vals]
    denom = exps[0] + exps[1] + exps[2] + exps[3]

    picked = jnp.zeros(logits.shape, F32)
    for sel in hots:
        picked = picked + sel.astype(F32)
    shift = jnp.full((tm, tm), sub.bit_length() - 1, I32)
    tj = lax.broadcasted_iota(I32, (tm, tm), 0)
    tt = lax.broadcasted_iota(I32, (tm, tm), 1)
    same = lax.shift_right_logical(tj, shift) == lax.shift_right_logical(tt, shift)
    before = jnp.where(jnp.logical_and(tj < tt, same), 1.0, 0.0).astype(BF16)
    prior = _dot(picked.astype(BF16), before)
    for kk in range(TOP_K):
        eid_ref[kk:kk + 1, :] = ids[kk].astype(I32)
        gate_ref[kk:kk + 1, :] = exps[kk] / denom
        rank_ref[kk:kk + 1, :] = jnp.sum(jnp.where(hots[kk], prior, 0.0), axis=0,
                                         keepdims=True).astype(I32)
    for s in range(tm // sub):
        cnt_ref[s] = jnp.sum(picked[:, s * sub:(s + 1) * sub], axis=1, keepdims=True).astype(I32)


def _post(ym, yc, x, wmo, gx, wxq, mkb, mvb, wxo, gf, wrt, br, *, tm, sub):
    bsz, seq, _ = x.shape
    nj = seq // tm
    grid = (bsz, nj)
    nsub = tm // sub
    n_tiles = bsz * nj * nsub
    tok = lambda c: pl.BlockSpec((None, tm, c), lambda b, j: (b, j, 0))
    full = lambda shape: pl.BlockSpec(shape, lambda b, j: (0,) * len(shape))
    n_mem = mkb.shape[0] // bsz
    mem = pl.BlockSpec((n_mem, N_MEM, D_MODEL), lambda b, j: (b, 0, 0))
    sel = pl.BlockSpec((None, TOP_K, tm), lambda b, j: (b, 0, j))
    return pl.pallas_call(
        functools.partial(_post_kernel, tm=tm, sub=sub),
        grid=grid,
        in_specs=[tok(D_MLSTM), tok(D_CONV), tok(D_MODEL), full((D_MODEL, D_MODEL)),
                  full((1, D_MODEL)), full((D_MODEL, D_MODEL)), mem, mem,
                  full((D_MODEL, D_MODEL)), full((1, D_MODEL)), full((N_EXPERTS, D_MODEL)),
                  full((N_EXPERTS, 1))],
        out_specs=[tok(D_MODEL), tok(D_MODEL), sel, sel, sel,
                   pl.BlockSpec((nsub, N_EXPERTS, 1), lambda b, j: (b * nj + j, 0, 0))],
        out_shape=[jax.ShapeDtypeStruct((bsz, seq, D_MODEL), F32),
                   jax.ShapeDtypeStruct((bsz, seq, D_MODEL), BF16),
                   jax.ShapeDtypeStruct((bsz, TOP_K, seq), I32),
                   jax.ShapeDtypeStruct((bsz, TOP_K, seq), F32),
                   jax.ShapeDtypeStruct((bsz, TOP_K, seq), I32),
                   jax.ShapeDtypeStruct((n_tiles, N_EXPERTS, 1), I32)],
        compiler_params=_cparams(("arbitrary", "arbitrary")),
        name="post",
    )(ym, yc, x, wmo, gx, wxq, mkb, mvb, wxo, gf, wrt, br)


def _sorted_rows(n_tokens):
    return -(-(TOP_K * n_tokens + N_EXPERTS * (SUBLANES - 1)) // ROW_BLOCK) * ROW_BLOCK


def _dispatch_kernel(tab_ref, fill_ref, eid_ref, rank_ref, ls_ref, x_ref, eids_ref, ranks_ref, xs_ref,
                     xb_ref, srt, zero_s, pending, sems, *, n_ptiles, n_blocks):
    i = pl.program_id(0)
    cur = i % 2

    def rows_copy(src, src_row, dst_row, nrows, sl):
        return pltpu.make_async_copy(
            src.at[pl.ds(pl.multiple_of(src_row, SUBLANES), nrows), :],
            xb_ref.at[pl.ds(pl.multiple_of(dst_row, SUBLANES), nrows), :], sems.at[sl])

    def granule(src, src_row, dst_row, sl):
        return rows_copy(src, src_row, dst_row, SUBLANES, sl)

    def drain(count, sl, nrows=SUBLANES):
        def body(g, c):
            rows_copy(zero_s, 0, 0, nrows, sl).wait()
            return c
        lax.fori_loop(0, count, body, 0)

    @pl.when(i == 0)
    def _():
        pending[0] = 0
        pending[1] = 0

    def issue_list(count, dst_at, src_at, nrows):
        def body(h, c):
            for u in range(COPIES_PER_TRIP):
                g = COPIES_PER_TRIP * h + u

                @pl.when(g < count)
                def _():
                    rows_copy(srt.at[cur], tab_ref[src_at + g], tab_ref[dst_at + g], nrows,
                              cur).start(priority=u % 2)
            return c
        lax.fori_loop(0, (count + COPIES_PER_TRIP - 1) // COPIES_PER_TRIP, body, 0)

    def sort_and_move(eid, rank, x):
        ntok = x.shape[0]
        nrows = _sorted_rows(ntok)
        e_iota = lax.broadcasted_iota(I32, (N_EXPERTS, ntok), 0)
        s_iota = lax.broadcasted_iota(I32, (nrows, ntok), 0)
        seg_start = ls_ref[...]
        hit = None
        for kk in range(TOP_K):
            start = jnp.sum(jnp.where(e_iota == eid[kk:kk + 1, :], seg_start, 0.0),
                            axis=0, keepdims=True).astype(I32)
            match = s_iota == start + rank[kk:kk + 1, :]
            hit = match if hit is None else jnp.logical_or(hit, match)
        perm = jnp.where(hit, 1.0, 0.0).astype(BF16)
        srt[cur, 0:nrows, :] = _dot(perm, x)
        drain(pending[0], 1 - cur, BIG_ROWS)
        drain(pending[1], 1 - cur)
        n_big, n_small = tab_ref[TAB_COUNTS], tab_ref[TAB_COUNTS + 1]
        issue_list(n_big, TAB_BIG_DST, TAB_BIG_SRC, BIG_ROWS)
        issue_list(n_small, TAB_SMALL_DST, TAB_SMALL_SRC, SUBLANES)
        pending[0] = n_big
        pending[1] = n_small

    @pl.when(i < n_ptiles)
    def _():
        sort_and_move(eid_ref[...], rank_ref[...], x_ref[...])

    @pl.when(i == n_ptiles)
    def _():
        sort_and_move(eids_ref[...], ranks_ref[...], xs_ref[...])
        drain(pending[0], cur, BIG_ROWS)
        drain(pending[1], cur)
        zero_s[...] = jnp.zeros_like(zero_s)
        for e in range(N_EXPERTS):
            dst = fill_ref[e]

            def zissue(g, c):
                granule(zero_s, 0, dst + g * SUBLANES, cur).start()
                return c
            lax.fori_loop(0, fill_ref[N_EXPERTS + e], zissue, 0)
        drain(fill_ref[2 * N_EXPERTS], cur)
        first_free = fill_ref[2 * N_EXPERTS + 1]

        def blk_copy(b):
            return pltpu.make_async_copy(
                zero_s, xb_ref.at[pl.ds(pl.multiple_of(b * ROW_BLOCK, ROW_BLOCK), ROW_BLOCK), :],
                sems.at[cur])

        def bissue(b, c):
            blk_copy(b).start()
            return c
        lax.fori_loop(first_free, n_blocks, bissue, 0)

        def bdrain(b, c):
            blk_copy(0).wait()
            return c
        lax.fori_loop(first_free, n_blocks, bdrain, 0)


def _dispatch(tab, fill, eid_p, rank_p, seg_start, xn_p, eid_s, rank_s, xn_s, *, tmd, n_blocks):
    n_ptiles = eid_p.shape[0]
    n_sample = xn_s.shape[0]
    last = n_ptiles - 1
    smem = lambda shape, imap: pl.BlockSpec(shape, imap, memory_space=pltpu.SMEM)
    return pl.pallas_call(
        functools.partial(_dispatch_kernel, n_ptiles=n_ptiles, n_blocks=n_blocks),
        grid=(n_ptiles + 1,),
        in_specs=[smem((TABLE_WIDTH,), lambda i: (i,)),
                  smem((TABLE_WIDTH,), lambda i: (0,)),
                  pl.BlockSpec((None, TOP_K, tmd), lambda i: (jnp.minimum(i, last), 0, 0)),
                  pl.BlockSpec((None, TOP_K, tmd), lambda i: (jnp.minimum(i, last), 0, 0)),
                  pl.BlockSpec((None, N_EXPERTS, 1), lambda i: (i, 0, 0)),
                  pl.BlockSpec((tmd, D_MODEL), lambda i: (jnp.minimum(i, last), 0)),
                  pl.BlockSpec((TOP_K, n_sample), lambda i: (0, 0)),
                  pl.BlockSpec((TOP_K, n_sample), lambda i: (0, 0)),
                  pl.BlockSpec((n_sample, D_MODEL), lambda i: (0, 0))],
        out_specs=pl.BlockSpec(memory_space=pl.ANY),
        out_shape=jax.ShapeDtypeStruct((n_blocks * ROW_BLOCK, D_MODEL), F32),
        scratch_shapes=[pltpu.VMEM((2, _sorted_rows(tmd), D_MODEL), F32),
                        pltpu.VMEM((ROW_BLOCK, D_MODEL), F32), pltpu.SMEM((2,), I32),
                        pltpu.SemaphoreType.DMA((2,))],
        compiler_params=_cparams(("arbitrary",)),
        name="dispatch",
    )(tab, fill, eid_p, rank_p, seg_start, xn_p, eid_s, rank_s, xn_s)


def _expert_kernel(tab_ref, xb_ref, wu_hbm, bu_ref, wd_hbm, bd_ref, yb_ref,
                   wu_f, wd_f, wu_s, wd_s, xbuf, ybuf, wsem, xsem, ysem):
    e = pl.program_id(0)
    first = tab_ref[e]
    nblk = tab_ref[N_EXPERTS + e]
    n_used = tab_ref[2 * N_EXPERTS]

    def hbm_rows(b):
        return pl.ds(pl.multiple_of(b * ROW_BLOCK, ROW_BLOCK), ROW_BLOCK)

    def x_copy(b, sl):
        return pltpu.make_async_copy(xb_ref.at[hbm_rows(b), :], xbuf.at[sl], xsem.at[sl])

    def y_copy(b, sl):
        return pltpu.make_async_copy(ybuf.at[sl], yb_ref.at[hbm_rows(b), :], ysem.at[sl])

    def w_copies(ex, sl):
        return (pltpu.make_async_copy(wu_hbm.at[ex], wu_f.at[sl], wsem.at[0, sl]),
                pltpu.make_async_copy(wd_hbm.at[ex], wd_f.at[sl], wsem.at[1, sl]))

    wslot = e % 2

    @pl.when(e == 0)
    def _():
        for c in w_copies(0, 0):
            c.start()
        for b in range(ROW_LOOKAHEAD):
            @pl.when(b < n_used)
            def _():
                x_copy(b, b % ROW_RING).start()

    @pl.when(e + 1 < N_EXPERTS)
    def _():
        for c in w_copies(e + 1, 1 - wslot):
            c.start()

    for c in w_copies(e, wslot):
        c.wait()

    step = 128

    def cast(r, c):
        rows = pl.ds(pl.multiple_of(r * step, step), step)
        wu_s[rows, :] = wu_f[wslot, rows, :].astype(BF16)
        wd_s[rows, :] = wd_f[wslot, rows, :].astype(BF16)
        return c
    lax.fori_loop(0, D_MODEL // step, cast, 0)

    def arrive(b):
        sl = b % ROW_RING
        x_copy(b, sl).wait()

        @pl.when(b + ROW_LOOKAHEAD < n_used)
        def _():
            x_copy(b + ROW_LOOKAHEAD, (b + ROW_LOOKAHEAD) % ROW_RING).start()

        @pl.when(b >= ROW_RING)
        def _():
            y_copy(b, sl).wait()
        return sl

    def up(sl):
        return _dot(xbuf[sl].astype(BF16), wu_s[...]) + bu_ref[...]

    def gated(hcat):
        glu = jnp.minimum(hcat[:, 0:D_FF], SWIGLU_LIMIT)
        lin = jnp.clip(hcat[:, D_FF:2 * D_FF], -SWIGLU_LIMIT, SWIGLU_LIMIT)
        return (glu * _sigmoid(SWIGLU_ALPHA * glu) * (lin + 1.0)).astype(BF16)

    def down(act, sl):
        ybuf[sl] = _dot(act, wd_s[...]) + bd_ref[...]

    def two_blocks(j, c):
        b0 = first + 2 * j
        slots = [arrive(b0), arrive(b0 + 1)]
        hidden = [up(sl) for sl in slots]
        acts = [gated(h) for h in hidden]
        for a, sl in zip(acts, slots):
            down(a, sl)
        y_copy(b0, slots[0]).start()
        y_copy(b0 + 1, slots[1]).start()
        return c
    lax.fori_loop(0, nblk // 2, two_blocks, 0)

    @pl.when(nblk % 2 == 1)
    def _():
        b = first + nblk - 1
        sl = arrive(b)
        down(gated(up(sl)), sl)
        y_copy(b, sl).start()

    @pl.when(e == N_EXPERTS - 1)
    def _():
        for k in range(ROW_RING):
            @pl.when(k < n_used)
            def _():
                y_copy(0, (n_used - 1 - k) % ROW_RING).wait()


def _experts(tab, xb, w_up, b_up, w_down, b_down):
    w_map = lambda e, tab: (e, 0, 0)
    grid_spec = pltpu.PrefetchScalarGridSpec(
        num_scalar_prefetch=1,
        grid=(N_EXPERTS,),
        in_specs=[pl.BlockSpec(memory_space=pl.ANY),
                  pl.BlockSpec(memory_space=pl.ANY),
                  pl.BlockSpec((None, 1, 2 * D_FF), w_map),
                  pl.BlockSpec(memory_space=pl.ANY),
                  pl.BlockSpec((None, 1, D_MODEL), w_map)],
        out_specs=pl.BlockSpec(memory_space=pl.ANY),
        scratch_shapes=[pltpu.VMEM((2, D_MODEL, 2 * D_FF), F32), pltpu.VMEM((2, D_FF, D_MODEL), F32),
                        pltpu.VMEM((D_MODEL, 2 * D_FF), BF16), pltpu.VMEM((D_FF, D_MODEL), BF16),
                        pltpu.VMEM((ROW_RING, ROW_BLOCK, D_MODEL), F32),
                        pltpu.VMEM((ROW_RING, ROW_BLOCK, D_MODEL), F32),
                        pltpu.SemaphoreType.DMA((2, 2)),
                        pltpu.SemaphoreType.DMA((ROW_RING,)), pltpu.SemaphoreType.DMA((ROW_RING,))],
    )
    return pl.pallas_call(
        _expert_kernel,
        grid_spec=grid_spec,
        out_shape=jax.ShapeDtypeStruct(xb.shape, F32),
        input_output_aliases={1: 0},
        compiler_params=_cparams(("arbitrary",)),
        name="experts",
    )(tab, xb, w_up, b_up, w_down, b_down)


def _split_bf16(a):
    hi = a.astype(BF16)
    return hi, (a - hi.astype(F32)).astype(BF16)


def _combine_kernel(tab_ref, nxt_ref, slot_ref, gate_ref, h2_ref, gfin_ref, yb_ref, y_ref, buf, sems,
                    *, tm):
    i = pl.program_id(0)
    n = pl.num_programs(0)
    cur = i % 2
    nrows = buf.shape[1]

    def rows_copy(src_row, sl, dst_row, nrows):
        return pltpu.make_async_copy(
            yb_ref.at[pl.ds(pl.multiple_of(src_row, SUBLANES), nrows), :],
            buf.at[sl, pl.ds(pl.multiple_of(dst_row, SUBLANES), nrows), :], sems.at[sl])

    def gather(t_ref, sl):
        def fetch(count, yb_at, tile_at, nrows):
            def body(h, c):
                for u in range(COPIES_PER_TRIP):
                    g = COPIES_PER_TRIP * h + u

                    @pl.when(g < count)
                    def _():
                        rows_copy(t_ref[yb_at + g], sl, t_ref[tile_at + g], nrows).start(priority=u % 2)
                return c
            lax.fori_loop(0, (count + COPIES_PER_TRIP - 1) // COPIES_PER_TRIP, body, 0)
        fetch(t_ref[TAB_COUNTS], TAB_BIG_DST, TAB_BIG_SRC, BIG_ROWS)
        fetch(t_ref[TAB_COUNTS + 1], TAB_SMALL_DST, TAB_SMALL_SRC, SUBLANES)

    @pl.when(i == 0)
    def _():
        buf[...] = jnp.zeros_like(buf)
        gather(tab_ref, 0)

    @pl.when(i + 1 < n)
    def _():
        gather(nxt_ref, 1 - cur)

    def drain(count, nrows):
        def body(g, c):
            rows_copy(0, cur, 0, nrows).wait()
            return c
        lax.fori_loop(0, count, body, 0)
    drain(tab_ref[TAB_COUNTS], BIG_ROWS)
    drain(tab_ref[TAB_COUNTS + 1], SUBLANES)

    s_iota = lax.broadcasted_iota(I32, (tm, nrows), 1)
    wgt = jnp.zeros((tm, nrows), F32)
    for kk in range(TOP_K):
        wgt = wgt + jnp.where(s_iota == slot_ref[:, kk:kk + 1], gate_ref[:, kk:kk + 1], 0.0)
    w_hi, w_lo = _split_bf16(wgt)
    rows = buf[cur].astype(BF16)
    acc = _dot(w_hi, rows) + _dot(w_lo, rows)
    y_ref[...] = _rms(h2_ref[...] + acc, gfin_ref[...])


def _combine(tab, slot_col, gate_col, h2, gfin, yb, *, tm):
    ntok = h2.shape[0]
    n = ntok // tm
    smem = lambda imap: pl.BlockSpec((TABLE_WIDTH,), imap, memory_space=pltpu.SMEM)
    return pl.pallas_call(
        functools.partial(_combine_kernel, tm=tm),
        grid=(n,),
        in_specs=[smem(lambda i: (i,)), smem(lambda i: (jnp.minimum(i + 1, n - 1),)),
                  pl.BlockSpec((tm, TOP_K), lambda i: (i, 0)),
                  pl.BlockSpec((tm, TOP_K), lambda i: (i, 0)),
                  pl.BlockSpec((tm, D_MODEL), lambda i: (i, 0)),
                  pl.BlockSpec((1, D_MODEL), lambda i: (0, 0)),
                  pl.BlockSpec(memory_space=pl.ANY)],
        out_specs=pl.BlockSpec((tm, D_MODEL), lambda i: (i, 0)),
        out_shape=jax.ShapeDtypeStruct((ntok, D_MODEL), F32),
        scratch_shapes=[pltpu.VMEM((2, _sorted_rows(tm), D_MODEL), F32),
                        pltpu.SemaphoreType.DMA((2,))],
        compiler_params=_cparams(("arbitrary",)),
        name="combine",
    )(tab, tab, slot_col, gate_col, h2, gfin, yb)


def _copy_tables(seg, seg_src, seg_dst):
    experts = jnp.arange(N_EXPERTS, dtype=I32)

    def flatten(counts, bound, src0, dst0, step):
        ends = jnp.cumsum(counts, axis=1)
        idx = jnp.arange(bound, dtype=I32)
        owner = jnp.sum(ends[:, None, :] <= idx[None, :, None], axis=-1)
        pick = lambda a: jnp.sum(jnp.where(owner[..., None] == experts, a[:, None, :], 0), axis=-1)
        off = (idx[None, :] - pick(ends - counts)) * step
        valid = idx[None, :] < ends[:, -1:]
        return (jnp.where(valid, pick(dst0) + off, 0), jnp.where(valid, pick(src0) + off, 0),
                ends[:, -1:])

    n_big = seg // BIG_ROWS
    n_small = (seg - n_big * BIG_ROWS) // SUBLANES
    b_dst, b_src, b_n = flatten(n_big, MAX_BIG, seg_src, seg_dst, BIG_ROWS)
    s_dst, s_src, s_n = flatten(n_small, MAX_SMALL, seg_src + n_big * BIG_ROWS,
                                seg_dst + n_big * BIG_ROWS, SUBLANES)
    tab = jnp.concatenate([b_dst, b_src, s_dst, s_src, b_n, s_n], axis=1)
    return jnp.pad(tab, ((0, 0), (0, TABLE_WIDTH - tab.shape[1]))).astype(I32).reshape(-1)


def _tiles(a, tile):
    bsz, kk, seq = a.shape
    return a.reshape(bsz, kk, seq // tile, tile).transpose(0, 2, 1, 3).reshape(-1, kk, tile)


def _path(x, cbuf, c0, n0, m0, mkb, mvb, wts, *, tm_in, ct, chunk, tm_post, sub, fold):
    bsz, seq, _ = x.shape
    if fold:
        x = x.reshape(1, bsz * seq, D_MODEL)
        q, k, v, og, gcol, grow, yc, nbuf = _inproj(
            x, wts["g_mix"], wts["wq"], wts["wg"], wts["wgt"], wts["wc"], wts["bg"], wts["bgt"],
            wts["cw"], cbuf, tm=bsz * seq, chunk=bsz * seq)
        q, k, v, og, gcol = (a.reshape(bsz, seq, a.shape[-1]) for a in (q, k, v, og, gcol))
        grow = grow.reshape(2 * N_HEADS, bsz, seq).transpose(1, 0, 2)[:, None]
    else:
        q, k, v, og, gcol, grow, yc, nbuf = _inproj(
            x, wts["g_mix"], wts["wq"], wts["wg"], wts["wgt"], wts["wc"], wts["bg"], wts["bgt"],
            wts["cw"], cbuf, tm=tm_in, chunk=chunk)
    ym, c1, n1, m1 = _mlstm(q, k, v, og, gcol, grow, c0, n0, m0, wts["ng"], ct=ct, chunk=chunk)
    if fold:
        ym = ym.reshape(1, bsz * seq, D_MLSTM)
    h2, xn, eid, gate, rank, cnt = _post(
        ym, yc, x, wts["wmo"], wts["g_x"], wts["wxq"], mkb, mvb, wts["wxo"], wts["g_ffn"],
        wts["wrt"], wts["br"], tm=tm_post, sub=sub)
    return dict(h2=h2, xn=xn, eid=eid, gate=gate, rank=rank, cnt=cnt[:, :, 0],
                c1=c1, n1=n1, m1=m1[..., 0], nbuf=nbuf)


def kernel(x_prompt, x_sample, state_mlstm_c, state_mlstm_n, state_mlstm_m, state_conv, cache_mem_k, cache_mem_v, mem_prompt, norm_mix_g, w_in, b_gate, mlstm_norm_g, conv_w, w_mix_out, norm_x_g, norm_mem_g, w_xq, w_xk, w_xv, w_xo, norm_ffn_g, w_router, b_router, w_up, b_up, w_down, b_down, norm_final_g):
    bp, lp, _ = x_prompt.shape
    bs, ls, _ = x_sample.shape
    l = 0
    row = lambda a: a.reshape(1, -1)

    wi = w_in[l]
    gate_cols = wi[:, 4 * D_MLSTM:4 * D_MLSTM + 2 * N_HEADS]
    wts = dict(
        g_mix=row(norm_mix_g[l]),
        wq=wi[:, 0:4 * D_MLSTM].astype(BF16),
        wg=jnp.pad(gate_cols, ((0, 0), (0, 128 - 2 * N_HEADS))).astype(BF16),
        wgt=gate_cols.T.astype(BF16),
        wc=wi[:, 4 * D_MLSTM + 2 * N_HEADS:].astype(BF16),
        bg=row(b_gate[l]), bgt=b_gate[l].reshape(-1, 1),
        cw=conv_w[l], ng=row(mlstm_norm_g[l]),
        wmo=w_mix_out[l].astype(BF16), g_x=row(norm_x_g[l]), wxq=w_xq[l].astype(BF16),
        wxo=w_xo[l].astype(BF16), g_ffn=row(norm_ffn_g[l]),
        wrt=w_router[l].T.astype(BF16), br=b_router[l].reshape(-1, 1),
    )

    mk, mv, mkb, mvb = _memkv(mem_prompt.reshape(bp * N_MEM, D_MODEL), row(norm_mem_g[l]),
                              w_xk[l].astype(BF16), w_xv[l].astype(BF16))
    zeros = lambda *s: jnp.zeros(s, F32)
    tm_post, tmd = 512, 256
    assert _sorted_rows(tmd) // BIG_ROWS <= MAX_BIG
    pr = _path(x_prompt, zeros(bp, CONV_WIDTH - 1, D_CONV), zeros(bp, N_HEADS, HEAD_DIM, HEAD_DIM),
               zeros(bp, N_HEADS, HEAD_DIM), zeros(bp, N_HEADS, 1),
               mkb.reshape(bp, N_MEM, D_MODEL), mvb.reshape(bp, N_MEM, D_MODEL), wts,
               tm_in=512, ct=512, chunk=CHUNK, tm_post=tm_post, sub=tmd, fold=False)
    sa = _path(x_sample, state_conv[l], state_mlstm_c[l], state_mlstm_n[l],
               state_mlstm_m[l][..., None],
               cache_mem_k[l].reshape(bs, N_MEM, D_MODEL).astype(BF16),
               cache_mem_v[l].reshape(bs, N_MEM, D_MODEL).astype(BF16), wts,
               tm_in=ls, ct=ls, chunk=min(CHUNK, ls), tm_post=bs * ls, sub=bs * ls, fold=True)

    n_ptok, n_stok = bp * lp, bs * ls
    cnt = jnp.concatenate([pr["cnt"], sa["cnt"]], axis=0)
    n_tiles = cnt.shape[0]
    n_blocks = -(-(TOP_K * (n_ptok + n_stok) + n_tiles * N_EXPERTS * (SUBLANES - 1)) // ROW_BLOCK) \
        + N_EXPERTS
    seg = (cnt + SUBLANES - 1) // SUBLANES * SUBLANES
    seg_src = jnp.cumsum(seg, axis=1) - seg
    tot = jnp.sum(seg, axis=0)
    blocks_e = (tot + ROW_BLOCK - 1) // ROW_BLOCK
    padded = blocks_e * ROW_BLOCK
    pstart = jnp.cumsum(padded) - padded
    bend = jnp.cumsum(blocks_e)
    n_used = bend[-1]
    exp_tab = jnp.concatenate([bend - blocks_e, blocks_e, n_used[None]]).astype(I32)
    seg_dst = pstart[None, :] + jnp.cumsum(seg, axis=0) - seg

    tab = _copy_tables(seg, seg_src, seg_dst)
    fgran = (padded - tot) // SUBLANES
    fill = jnp.concatenate([pstart + tot, fgran, jnp.sum(fgran)[None], n_used[None]])
    fill = jnp.pad(fill, (0, TABLE_WIDTH - fill.shape[0])).astype(I32)

    def lookup(table, eid):
        hit = eid[..., None] == jnp.arange(N_EXPERTS, dtype=I32)
        return jnp.sum(jnp.where(hit, table[:, None, None, :], 0), axis=-1)

    eid_pt, rank_pt = _tiles(pr["eid"], tmd), _tiles(pr["rank"], tmd)
    eid_st = sa["eid"].transpose(1, 0, 2).reshape(1, TOP_K, n_stok)
    rank_st = sa["rank"].transpose(1, 0, 2).reshape(1, TOP_K, n_stok)
    slot_p = (lookup(seg_src[:-1], eid_pt) + rank_pt).transpose(0, 2, 1).reshape(n_ptok, TOP_K)
    slot_s = (lookup(seg_src[-1:], eid_st) + rank_st).transpose(0, 2, 1).reshape(n_stok, TOP_K)

    xb = _dispatch(tab, fill, eid_pt, rank_pt, seg_src.astype(F32)[..., None],
                   pr["xn"].reshape(n_ptok, D_MODEL), eid_st[0], rank_st[0],
                   sa["xn"].reshape(n_stok, D_MODEL), tmd=tmd, n_blocks=n_blocks)
    yb = _experts(exp_tab, xb, w_up[l], b_up[l][:, None, :], w_down[l], b_down[l][:, None, :])

    gfin = row(norm_final_g)
    split = (n_tiles - 1) * TABLE_WIDTH
    y_p = _combine(tab[:split], slot_p.astype(I32), pr["gate"].transpose(0, 2, 1).reshape(n_ptok, TOP_K),
                   pr["h2"].reshape(n_ptok, D_MODEL), gfin, yb, tm=tmd)
    y_s = _combine(tab[split:], slot_s.astype(I32), sa["gate"].transpose(0, 2, 1).reshape(n_stok, TOP_K),
                   sa["h2"].reshape(n_stok, D_MODEL), gfin, yb, tm=n_stok)

    lead = lambda a: a[None]
    return (y_p.reshape(bp, lp, D_MODEL), y_s.reshape(bs, ls, D_MODEL),
            lead(pr["c1"]), lead(pr["n1"]), lead(pr["m1"]), lead(pr["nbuf"]),
            lead(mk.reshape(bp, N_MEM, N_XHEADS, XHEAD_DIM)),
            lead(mv.reshape(bp, N_MEM, N_XHEADS, XHEAD_DIM)),
            lead(sa["c1"]), lead(sa["n1"]), lead(sa["m1"]), lead(sa["nbuf"]))
```

```python
import functools

import jax
import jax.numpy as jnp
from jax import lax
from jax.experimental import pallas as pl
from jax.experimental.pallas import tpu as pltpu

F32 = jnp.float32
BF16 = jnp.bfloat16
I32 = jnp.int32

D_MODEL = 1024
N_HEADS = 4
HEAD_DIM = 128
D_MLSTM = N_HEADS * HEAD_DIM
D_CONV = D_MODEL - D_MLSTM
CONV_WIDTH = 3
CHUNK = 64
N_MEM = 256
N_XHEADS = 4
XHEAD_DIM = D_MODEL // N_XHEADS
N_EXPERTS = 32
TOP_K = 4
D_FF = D_MODEL
SWIGLU_LIMIT = 7.0
SWIGLU_ALPHA = 1.702
EPS = 1e-5

SUBLANES = 8
BIG_ROWS = 4 * SUBLANES
COPIES_PER_TRIP = 4
MAX_BIG, MAX_SMALL = 64, 3 * N_EXPERTS
TAB_BIG_DST, TAB_BIG_SRC = 0, MAX_BIG
TAB_SMALL_DST, TAB_SMALL_SRC = 2 * MAX_BIG, 2 * MAX_BIG + MAX_SMALL
TAB_COUNTS = 2 * MAX_BIG + 2 * MAX_SMALL
TABLE_WIDTH = 512
POST_GROUP_ROWS = 512
ROW_BLOCK = 256
ROW_LOOKAHEAD = 3
ROW_RING = ROW_LOOKAHEAD + 2
VMEM_LIMIT = 56 * 1024 * 1024


def _cparams(sem):
    return pltpu.CompilerParams(dimension_semantics=sem, vmem_limit_bytes=VMEM_LIMIT)


def _rms(x, g):
    return x * lax.rsqrt(jnp.mean(x * x, axis=-1, keepdims=True) + EPS) * g


def _log_sigmoid(x):
    return -(jnp.maximum(-x, 0.0) + jnp.log1p(jnp.exp(-jnp.abs(x))))


def _sigmoid(x):
    return 1.0 / (1.0 + jnp.exp(-x))


def _dot(a, b):
    return jnp.dot(a, b, preferred_element_type=F32)


def _dot_nt(a, b):
    return lax.dot_general(a, b, (((1,), (1,)), ((), ())), preferred_element_type=F32)


def _dot_tn(a, b):
    return lax.dot_general(a, b, (((0,), (0,)), ((), ())), preferred_element_type=F32)


def _memkv_kernel(mem_ref, g_ref, wk_ref, wv_ref, mk_ref, mv_ref, mkb_ref, mvb_ref):
    mn = _rms(mem_ref[...], g_ref[...]).astype(BF16)
    mk = _dot(mn, wk_ref[...])
    mv = _dot(mn, wv_ref[...])
    mk_ref[...] = mk
    mv_ref[...] = mv
    mkb_ref[...] = mk.astype(BF16)
    mvb_ref[...] = mv.astype(BF16)


def _memkv(mem2d, g, wk, wv):
    rows = mem2d.shape[0]
    tm = N_MEM
    row_spec = pl.BlockSpec((tm, D_MODEL), lambda i: (i, 0))
    full = lambda shape: pl.BlockSpec(shape, lambda i: (0,) * len(shape))
    return pl.pallas_call(
        _memkv_kernel,
        grid=(rows // tm,),
        in_specs=[row_spec, full((1, D_MODEL)), full((D_MODEL, D_MODEL)), full((D_MODEL, D_MODEL))],
        out_specs=[row_spec, row_spec, row_spec, row_spec],
        out_shape=[jax.ShapeDtypeStruct((rows, D_MODEL), F32)] * 2
        + [jax.ShapeDtypeStruct((rows, D_MODEL), BF16)] * 2,
        compiler_params=_cparams(("arbitrary",)),
        name="memkv",
    )(mem2d, g, wk, wv)


def _inproj_kernel(x_ref, g_ref, wq_ref, wg_ref, wgt_ref, wc_ref, bg_ref, bgt_ref, cw_ref, cbuf_ref,
                   q_ref, k_ref, v_ref, og_ref, gcol_ref, grow_ref, yc_ref, nbuf_ref,
                   carry_ref, *, tm, chunk):
    j = pl.program_id(1)
    streams = cbuf_ref.shape[0]
    per = tm // streams

    @pl.when(j == 0)
    def _():
        carry_ref[0:2, :] = cbuf_ref[0]

    xb = _rms(x_ref[...], g_ref[...]).astype(BF16)

    p = _dot(xb, wq_ref[...])
    q_ref[...] = p[:, 0:D_MLSTM].astype(BF16)
    k_ref[...] = (p[:, D_MLSTM:2 * D_MLSTM] * (HEAD_DIM ** -0.5)).astype(BF16)
    v_ref[...] = p[:, 2 * D_MLSTM:3 * D_MLSTM].astype(BF16)
    og_ref[...] = _sigmoid(p[:, 3 * D_MLSTM:4 * D_MLSTM])

    gc = _dot(xb, wg_ref[...])[:, 0:2 * N_HEADS] + bg_ref[...]
    col = lax.broadcasted_iota(I32, gc.shape, 1)
    gcol_ref[...] = jnp.where(col < N_HEADS, gc, _log_sigmoid(gc))
    gr = _dot_nt(wgt_ref[...], xb) + bgt_ref[...]
    row = lax.broadcasted_iota(I32, gr.shape, 0)
    gr = jnp.where(row < N_HEADS, gr, _log_sigmoid(gr))
    for c in range(tm // chunk):
        grow_ref[c] = gr[:, c * chunk:(c + 1) * chunk]

    pc = _dot(xb, wc_ref[...])
    cb = pc[:, 0:D_CONV]
    u = pc[:, D_CONV:2 * D_CONV] * pc[:, 2 * D_CONV:3 * D_CONV]
    rid = lax.broadcasted_iota(I32, u.shape, 0)
    um1 = pltpu.roll(u, 1, 0)
    um2 = pltpu.roll(u, 2, 0)
    for m in range(streams):
        prev = carry_ref if streams == 1 else cbuf_ref.at[m]
        c0, c1 = prev[0:1, :], prev[1:2, :]
        um1 = jnp.where(rid == m * per, c1, um1)
        um2 = jnp.where(rid == m * per, c0, jnp.where(rid == m * per + 1, c1, um2))
        nbuf_ref[m] = u[(m + 1) * per - 2:(m + 1) * per, :]
    yc = cw_ref[0:1, :] * um2 + cw_ref[1:2, :] * um1 + cw_ref[2:3, :] * u
    yc_ref[...] = (cb * yc).astype(BF16)
    carry_ref[0:2, :] = u[tm - 2:tm, :]


def _inproj(x, g, wq, wg, wgt, wc, bg, bgt, cw, cbuf, *, tm, chunk):
    bsz, seq, _ = x.shape
    grid = (bsz, seq // tm)
    tok = lambda c: pl.BlockSpec((None, tm, c), lambda b, j: (b, j, 0))
    full = lambda shape: pl.BlockSpec(shape, lambda b, j: (0,) * len(shape))
    nck = tm // chunk
    streams = cbuf.shape[0] // bsz
    assert streams == 1 or tm == seq
    conv_state = pl.BlockSpec((streams, 2, D_CONV), lambda b, j: (b, 0, 0))
    return pl.pallas_call(
        functools.partial(_inproj_kernel, tm=tm, chunk=chunk),
        grid=grid,
        in_specs=[tok(D_MODEL), full((1, D_MODEL)), full((D_MODEL, 4 * D_MLSTM)),
                  full((D_MODEL, 128)), full((8, D_MODEL)), full((D_MODEL, 3 * D_CONV)),
                  full((1, 8)), full((8, 1)), full((CONV_WIDTH, D_CONV)), conv_state],
        out_specs=[tok(D_MLSTM), tok(D_MLSTM), tok(D_MLSTM), tok(D_MLSTM), tok(8),
                   pl.BlockSpec((None, nck, 8, chunk), lambda b, j: (b, j, 0, 0)),
                   tok(D_CONV), conv_state],
        out_shape=[jax.ShapeDtypeStruct((bsz, seq, D_MLSTM), BF16)] * 3
        + [jax.ShapeDtypeStruct((bsz, seq, D_MLSTM), F32),
           jax.ShapeDtypeStruct((bsz, seq, 8), F32),
           jax.ShapeDtypeStruct((bsz, seq // chunk, 8, chunk), F32),
           jax.ShapeDtypeStruct((bsz, seq, D_CONV), BF16),
           jax.ShapeDtypeStruct(cbuf.shape, F32)],
        scratch_shapes=[pltpu.VMEM((8, D_CONV), F32)],
        compiler_params=_cparams(("arbitrary", "arbitrary")),
        name="inproj",
    )(x, g, wq, wg, wgt, wc, bg, bgt, cw, cbuf)


def _mlstm_kernel(q_ref, k_ref, v_ref, og_ref, gc_ref, gr_ref, c0_ref, n0_ref, m0_ref, ng_ref,
                  ym_ref, c1_ref, n1_ref, m1_ref, c_s, n_s, m_s, *, chunk, nchunks, bsz):
    j = pl.program_id(0)

    @pl.when(j == 0)
    def _():
        c_s[...] = c0_ref[...]
        n_s[...] = n0_ref[...]
        m_s[...] = m0_ref[...]

    ti = lax.broadcasted_iota(I32, (chunk, chunk), 0)
    ji = lax.broadcasted_iota(I32, (chunk, chunk), 1)
    causal = ji <= ti

    def body(ci, carry):
        r0 = pl.multiple_of(ci * chunk, chunk)
        rows = pl.ds(r0, chunk)
        chains = [(b, h) for b in range(bsz) for h in range(N_HEADS)]
        cols = lambda h: slice(h * HEAD_DIM, (h + 1) * HEAD_DIM)
        each = lambda f: [f(n, b, h) for n, (b, h) in enumerate(chains)]
        q = lambda b, h: q_ref[b, rows, cols(h)]
        k = lambda b, h: k_ref[b, rows, cols(h)]
        v = lambda b, h: v_ref[b, rows, cols(h)]
        gcs = [gc_ref[b, rows, :] for b in range(bsz)]
        grs = [gr_ref[b, ci] for b in range(bsz)]
        li_c = each(lambda n, b, h: gcs[b][:, h:h + 1])
        lf_c = each(lambda n, b, h: gcs[b][:, N_HEADS + h:N_HEADS + h + 1])
        li_r = each(lambda n, b, h: grs[b][h:h + 1, :])
        lf_r = each(lambda n, b, h: grs[b][N_HEADS + h:N_HEADS + h + 1, :])
        m_prev = each(lambda n, b, h: m_s[b, h:h + 1, :])

        b_c = each(lambda n, b, h: jnp.sum(jnp.where(causal, lf_r[n], 0.0), axis=1, keepdims=True))
        b_r = each(lambda n, b, h: jnp.sum(jnp.where(ti <= ji, lf_c[n], 0.0), axis=0, keepdims=True))
        dmat = each(lambda n, b, h: jnp.where(causal, b_c[n] - b_r[n] + li_r[n], -jnp.inf))
        dmax = each(lambda n, b, h: jnp.max(dmat[n], axis=1, keepdims=True))
        inter = each(lambda n, b, h: b_c[n] + m_prev[n])
        m_t = each(lambda n, b, h: jnp.maximum(inter[n], dmax[n]))
        w_inter = each(lambda n, b, h: jnp.exp(inter[n] - m_t[n]))
        s = each(lambda n, b, h: _dot_nt(q(b, h), k(b, h)) * jnp.exp(dmat[n] - m_t[n]))
        qc = each(lambda n, b, h: _dot(q(b, h), c_s[b, h].astype(BF16)))
        sv = each(lambda n, b, h: _dot(s[n].astype(BF16), v(b, h)))
        qn = each(lambda n, b, h: jnp.sum(q(b, h).astype(F32) * n_s[b, h:h + 1, :], axis=1,
                                          keepdims=True))
        den = each(lambda n, b, h: w_inter[n] * qn[n] + jnp.sum(s[n], axis=1, keepdims=True))
        hh = each(lambda n, b, h: (w_inter[n] * qc[n] + sv[n])
                  / jnp.maximum(jnp.abs(den[n]), jnp.exp(-m_t[n])))

        m_new = each(lambda n, b, h: m_t[n][chunk - 1:chunk, :])
        b_last = each(lambda n, b, h: b_c[n][chunk - 1:chunk, :])
        decay = each(lambda n, b, h: jnp.exp(b_last[n] + m_prev[n] - m_new[n]))
        kw = each(lambda n, b, h: k(b, h).astype(F32)
                  * jnp.exp(b_last[n] - b_c[n] + li_c[n] - m_new[n]))
        kv = each(lambda n, b, h: _dot_tn(kw[n].astype(BF16), v(b, h)))
        for n, (b, h) in enumerate(chains):
            c_s[b, h] = decay[n] * c_s[b, h] + kv[n]
            n_s[b, h:h + 1, :] = decay[n] * n_s[b, h:h + 1, :] + jnp.sum(kw[n], axis=0, keepdims=True)
            m_s[b, h:h + 1, :] = m_new[n]

        hn = each(lambda n, b, h: hh[n] * lax.rsqrt(jnp.mean(hh[n] * hh[n], axis=1, keepdims=True) + EPS)
                  * ng_ref[:, cols(h)])
        for n, (b, h) in enumerate(chains):
            ym_ref[b, rows, cols(h)] = (hn[n] * og_ref[b, rows, cols(h)]).astype(BF16)
        return carry

    lax.fori_loop(0, nchunks, body, 0)

    @pl.when(j == pl.num_programs(0) - 1)
    def _():
        c1_ref[...] = c_s[...]
        n1_ref[...] = n_s[...]
        m1_ref[...] = m_s[...]


def _mlstm(q, k, v, og, gcol, grow, c0, n0, m0, ng, *, ct, chunk):
    bsz, seq, _ = q.shape
    nchunks = ct // chunk
    grid = (seq // ct,)
    tok = lambda c: pl.BlockSpec((bsz, ct, c), lambda j: (0, j, 0))
    st_c = pl.BlockSpec((bsz, N_HEADS, HEAD_DIM, HEAD_DIM), lambda j: (0, 0, 0, 0))
    st_n = pl.BlockSpec((bsz, N_HEADS, HEAD_DIM), lambda j: (0, 0, 0))
    st_m = pl.BlockSpec((bsz, N_HEADS, 1), lambda j: (0, 0, 0))
    return pl.pallas_call(
        functools.partial(_mlstm_kernel, chunk=chunk, nchunks=nchunks, bsz=bsz),
        grid=grid,
        in_specs=[tok(D_MLSTM), tok(D_MLSTM), tok(D_MLSTM), tok(D_MLSTM), tok(8),
                  pl.BlockSpec((bsz, nchunks, 8, chunk), lambda j: (0, j, 0, 0)),
                  st_c, st_n, st_m,
                  pl.BlockSpec((1, D_MLSTM), lambda j: (0, 0))],
        out_specs=[tok(D_MLSTM), st_c, st_n, st_m],
        out_shape=[jax.ShapeDtypeStruct((bsz, seq, D_MLSTM), BF16),
                   jax.ShapeDtypeStruct((bsz, N_HEADS, HEAD_DIM, HEAD_DIM), F32),
                   jax.ShapeDtypeStruct((bsz, N_HEADS, HEAD_DIM), F32),
                   jax.ShapeDtypeStruct((bsz, N_HEADS, 1), F32)],
        scratch_shapes=[pltpu.VMEM((bsz, N_HEADS, HEAD_DIM, HEAD_DIM), F32),
                        pltpu.VMEM((bsz, N_HEADS, HEAD_DIM), F32),
                        pltpu.VMEM((bsz, N_HEADS, 1), F32)],
        compiler_params=_cparams(("arbitrary",)),
        name="mlstm",
    )(q, k, v, og, gcol, grow, c0, n0, m0, ng)


def _post_kernel(ym_ref, yc_ref, x_ref, wmo_ref, gx_ref, wxq_ref, mk_ref, mv_ref, wxo_ref,
                 gf_ref, wrt_ref, br_ref,
                 h2_ref, xn_ref, eid_ref, gate_ref, rank_ref, cnt_ref, *, tm, sub):
    n_mem = mk_ref.shape[0]
    per = tm // n_mem
    n_groups = max(1, tm // POST_GROUP_ROWS) if n_mem == 1 else 1
    rows = [slice(g * (tm // n_groups), (g + 1) * (tm // n_groups)) for g in range(n_groups)]
    each = lambda f: [f(g, r) for g, r in enumerate(rows)]

    mix = each(lambda g, r: _dot(ym_ref[r, :], wmo_ref[0:D_MLSTM, :])
               + _dot(yc_ref[r, :], wmo_ref[D_MLSTM:D_MODEL, :]))
    h1 = each(lambda g, r: x_ref[r, :] + mix[g])
    xq = each(lambda g, r: _dot(_rms(h1[g], gx_ref[...]).astype(BF16), wxq_ref[...]).astype(BF16))

    def streams(g):
        if n_mem == 1:
            return [(slice(None), 0)]
        return [(slice(m * per, (m + 1) * per), m) for m in range(n_mem)]

    cols = lambda hd: slice(hd * XHEAD_DIM, (hd + 1) * XHEAD_DIM)
    units = [(g, q, m, hd) for g in range(n_groups) for q, m in streams(g) for hd in range(N_XHEADS)]
    s = [_dot_nt(xq[g][q, cols(hd)], mk_ref[m, :, cols(hd)]) * (XHEAD_DIM ** -0.5)
         for g, q, m, hd in units]
    e = [jnp.exp(v - jnp.max(v, axis=-1, keepdims=True)) for v in s]
    p = [(v / jnp.sum(v, axis=-1, keepdims=True)).astype(BF16) for v in e]
    o = [_dot(v, mv_ref[m, :, cols(hd)]).astype(BF16) for v, (g, q, m, hd) in zip(p, units)]
    per_group = len(units) // n_groups
    o_rows = [jnp.concatenate(
        [jnp.concatenate(o[g * per_group + st * N_XHEADS:g * per_group + (st + 1) * N_XHEADS], axis=1)
         for st in range(per_group // N_XHEADS)], axis=0) for g in range(n_groups)]
    h2 = each(lambda g, r: h1[g] + _dot(o_rows[g], wxo_ref[...]))
    xn2_parts = each(lambda g, r: _rms(h2[g], gf_ref[...]).astype(BF16))
    for g, r in enumerate(rows):
        h2_ref[r, :] = h2[g]
        xn_ref[r, :] = xn2_parts[g]
    xn2 = jnp.concatenate(xn2_parts, axis=0)

    logits = _dot_nt(wrt_ref[...], xn2) + br_ref[...]
    eidx = lax.broadcasted_iota(I32, logits.shape, 0).astype(F32)
    work = logits
    vals, ids, hots = [], [], []
    for _ in range(TOP_K):
        mx = jnp.max(work, axis=0, keepdims=True)
        idx = jnp.min(jnp.where(work == mx, eidx, float(N_EXPERTS)), axis=0, keepdims=True)
        sel = eidx == idx
        vals.append(mx)
        ids.append(idx)
        hots.append(sel)
        work = jnp.where(sel, -jnp.inf, work)
    exps = [jnp.exp(v - vals[0]) for v in vals]
    denom = exps[0] + exps[1] + exps[2] + exps[3]

    picked = jnp.zeros(logits.shape, F32)
    for sel in hots:
        picked = picked + sel.astype(F32)
    shift = jnp.full((tm, tm), sub.bit_length() - 1, I32)
    tj = lax.broadcasted_iota(I32, (tm, tm), 0)
    tt = lax.broadcasted_iota(I32, (tm, tm), 1)
    same = lax.shift_right_logical(tj, shift) == lax.shift_right_logical(tt, shift)
    before = jnp.where(jnp.logical_and(tj < tt, same), 1.0, 0.0).astype(BF16)
    prior = _dot(picked.astype(BF16), before)
    for kk in range(TOP_K):
        eid_ref[kk:kk + 1, :] = ids[kk].astype(I32)
        gate_ref[kk:kk + 1, :] = exps[kk] / denom
        rank_ref[kk:kk + 1, :] = jnp.sum(jnp.where(hots[kk], prior, 0.0), axis=0,
                                         keepdims=True).astype(I32)
    for s in range(tm // sub):
        cnt_ref[s] = jnp.sum(picked[:, s * sub:(s + 1) * sub], axis=1, keepdims=True).astype(I32)


def _post(ym, yc, x, wmo, gx, wxq, mkb, mvb, wxo, gf, wrt, br, *, tm, sub):
    bsz, seq, _ = x.shape
    nj = seq // tm
    grid = (bsz, nj)
    nsub = tm // sub
    n_tiles = bsz * nj * nsub
    tok = lambda c: pl.BlockSpec((None, tm, c), lambda b, j: (b, j, 0))
    full = lambda shape: pl.BlockSpec(shape, lambda b, j: (0,) * len(shape))
    n_mem = mkb.shape[0] // bsz
    mem = pl.BlockSpec((n_mem, N_MEM, D_MODEL), lambda b, j: (b, 0, 0))
    sel = pl.BlockSpec((None, TOP_K, tm), lambda b, j: (b, 0, j))
    return pl.pallas_call(
        functools.partial(_post_kernel, tm=tm, sub=sub),
        grid=grid,
        in_specs=[tok(D_MLSTM), tok(D_CONV), tok(D_MODEL), full((D_MODEL, D_MODEL)),
                  full((1, D_MODEL)), full((D_MODEL, D_MODEL)), mem, mem,
                  full((D_MODEL, D_MODEL)), full((1, D_MODEL)), full((N_EXPERTS, D_MODEL)),
                  full((N_EXPERTS, 1))],
        out_specs=[tok(D_MODEL), tok(D_MODEL), sel, sel, sel,
                   pl.BlockSpec((nsub, N_EXPERTS, 1), lambda b, j: (b * nj + j, 0, 0))],
        out_shape=[jax.ShapeDtypeStruct((bsz, seq, D_MODEL), F32),
                   jax.ShapeDtypeStruct((bsz, seq, D_MODEL), BF16),
                   jax.ShapeDtypeStruct((bsz, TOP_K, seq), I32),
                   jax.ShapeDtypeStruct((bsz, TOP_K, seq), F32),
                   jax.ShapeDtypeStruct((bsz, TOP_K, seq), I32),
                   jax.ShapeDtypeStruct((n_tiles, N_EXPERTS, 1), I32)],
        compiler_params=_cparams(("arbitrary", "arbitrary")),
        name="post",
    )(ym, yc, x, wmo, gx, wxq, mkb, mvb, wxo, gf, wrt, br)


def _sorted_rows(n_tokens):
    return -(-(TOP_K * n_tokens + N_EXPERTS * (SUBLANES - 1)) // ROW_BLOCK) * ROW_BLOCK


def _dispatch_kernel(tab_ref, fill_ref, eid_ref, rank_ref, ls_ref, x_ref, eids_ref, ranks_ref, xs_ref,
                     xb_ref, srt, zero_s, pending, sems, *, n_ptiles, n_blocks):
    i = pl.program_id(0)
    cur = i % 2

    def rows_copy(src, src_row, dst_row, nrows, sl):
        return pltpu.make_async_copy(
            src.at[pl.ds(pl.multiple_of(src_row, SUBLANES), nrows), :],
            xb_ref.at[pl.ds(pl.multiple_of(dst_row, SUBLANES), nrows), :], sems.at[sl])

    def granule(src, src_row, dst_row, sl):
        return rows_copy(src, src_row, dst_row, SUBLANES, sl)

    def drain(count, sl, nrows=SUBLANES):
        def body(g, c):
            rows_copy(zero_s, 0, 0, nrows, sl).wait()
            return c
        lax.fori_loop(0, count, body, 0)

    @pl.when(i == 0)
    def _():
        pending[0] = 0
        pending[1] = 0

    def issue_list(count, dst_at, src_at, nrows):
        def body(h, c):
            for u in range(COPIES_PER_TRIP):
                g = COPIES_PER_TRIP * h + u

                @pl.when(g < count)
                def _():
                    rows_copy(srt.at[cur], tab_ref[src_at + g], tab_ref[dst_at + g], nrows,
                              cur).start(priority=u % 2)
            return c
        lax.fori_loop(0, (count + COPIES_PER_TRIP - 1) // COPIES_PER_TRIP, body, 0)

    def sort_and_move(eid, rank, x):
        ntok = x.shape[0]
        nrows = _sorted_rows(ntok)
        e_iota = lax.broadcasted_iota(I32, (N_EXPERTS, ntok), 0)
        s_iota = lax.broadcasted_iota(I32, (nrows, ntok), 0)
        seg_start = ls_ref[...]
        hit = None
        for kk in range(TOP_K):
            start = jnp.sum(jnp.where(e_iota == eid[kk:kk + 1, :], seg_start, 0.0),
                            axis=0, keepdims=True).astype(I32)
            match = s_iota == start + rank[kk:kk + 1, :]
            hit = match if hit is None else jnp.logical_or(hit, match)
        perm = jnp.where(hit, 1.0, 0.0).astype(BF16)
        srt[cur, 0:nrows, :] = _dot(perm, x)
        drain(pending[0], 1 - cur, BIG_ROWS)
        drain(pending[1], 1 - cur)
        n_big, n_small = tab_ref[TAB_COUNTS], tab_ref[TAB_COUNTS + 1]
        issue_list(n_big, TAB_BIG_DST, TAB_BIG_SRC, BIG_ROWS)
        issue_list(n_small, TAB_SMALL_DST, TAB_SMALL_SRC, SUBLANES)
        pending[0] = n_big
        pending[1] = n_small

    @pl.when(i < n_ptiles)
    def _():
        sort_and_move(eid_ref[...], rank_ref[...], x_ref[...])

    fill = 2
    first_free = fill_ref[2 * N_EXPERTS + 1]

    def blk_copy(b):
        return pltpu.make_async_copy(
            zero_s, xb_ref.at[pl.ds(pl.multiple_of(b * ROW_BLOCK, ROW_BLOCK), ROW_BLOCK), :],
            sems.at[fill])

    @pl.when(i == 0)
    def _():
        zero_s[...] = jnp.zeros_like(zero_s)
        for e in range(N_EXPERTS):
            dst = fill_ref[e]

            def zissue(g, c):
                granule(zero_s, 0, dst + g * SUBLANES, fill).start()
                return c
            lax.fori_loop(0, fill_ref[N_EXPERTS + e], zissue, 0)

        def bissue(b, c):
            blk_copy(b).start()
            return c
        lax.fori_loop(first_free, n_blocks, bissue, 0)

    @pl.when(i == n_ptiles)
    def _():
        sort_and_move(eids_ref[...], ranks_ref[...], xs_ref[...])
        drain(pending[0], cur, BIG_ROWS)
        drain(pending[1], cur)
        drain(fill_ref[2 * N_EXPERTS], fill)

        def bdrain(b, c):
            blk_copy(0).wait()
            return c
        lax.fori_loop(first_free, n_blocks, bdrain, 0)


def _dispatch(tab, fill, eid_p, rank_p, seg_start, xn_p, eid_s, rank_s, xn_s, *, tmd, n_blocks):
    n_ptiles = eid_p.shape[0]
    n_sample = xn_s.shape[0]
    last = n_ptiles - 1
    smem = lambda shape, imap: pl.BlockSpec(shape, imap, memory_space=pltpu.SMEM)
    return pl.pallas_call(
        functools.partial(_dispatch_kernel, n_ptiles=n_ptiles, n_blocks=n_blocks),
        grid=(n_ptiles + 1,),
        in_specs=[smem((TABLE_WIDTH,), lambda i: (i,)),
                  smem((TABLE_WIDTH,), lambda i: (0,)),
                  pl.BlockSpec((None, TOP_K, tmd), lambda i: (jnp.minimum(i, last), 0, 0)),
                  pl.BlockSpec((None, TOP_K, tmd), lambda i: (jnp.minimum(i, last), 0, 0)),
                  pl.BlockSpec((None, N_EXPERTS, 1), lambda i: (i, 0, 0)),
                  pl.BlockSpec((tmd, D_MODEL), lambda i: (jnp.minimum(i, last), 0)),
                  pl.BlockSpec((TOP_K, n_sample), lambda i: (0, 0)),
                  pl.BlockSpec((TOP_K, n_sample), lambda i: (0, 0)),
                  pl.BlockSpec((n_sample, D_MODEL), lambda i: (0, 0))],
        out_specs=pl.BlockSpec(memory_space=pl.ANY),
        out_shape=jax.ShapeDtypeStruct((n_blocks * ROW_BLOCK, D_MODEL), F32),
        scratch_shapes=[pltpu.VMEM((2, _sorted_rows(tmd), D_MODEL), F32),
                        pltpu.VMEM((ROW_BLOCK, D_MODEL), F32), pltpu.SMEM((2,), I32),
                        pltpu.SemaphoreType.DMA((3,))],
        compiler_params=_cparams(("arbitrary",)),
        name="dispatch",
    )(tab, fill, eid_p, rank_p, seg_start, xn_p, eid_s, rank_s, xn_s)


def _expert_kernel(tab_ref, xb_ref, wu_hbm, bu_ref, wd_hbm, bd_ref, yb_ref,
                   wu_f, wd_f, wu_s, wd_s, xbuf, ybuf, wsem, xsem, ysem):
    e = pl.program_id(0)
    first = tab_ref[e]
    nblk = tab_ref[N_EXPERTS + e]
    n_used = tab_ref[2 * N_EXPERTS]

    def hbm_rows(b):
        return pl.ds(pl.multiple_of(b * ROW_BLOCK, ROW_BLOCK), ROW_BLOCK)

    def x_copy(b, sl):
        return pltpu.make_async_copy(xb_ref.at[hbm_rows(b), :], xbuf.at[sl], xsem.at[sl])

    def y_copy(b, sl):
        return pltpu.make_async_copy(ybuf.at[sl], yb_ref.at[hbm_rows(b), :], ysem.at[sl])

    def w_copies(ex, sl):
        return (pltpu.make_async_copy(wu_hbm.at[ex], wu_f.at[sl], wsem.at[0, sl]),
                pltpu.make_async_copy(wd_hbm.at[ex], wd_f.at[sl], wsem.at[1, sl]))

    wslot = e % 2

    @pl.when(e == 0)
    def _():
        for c in w_copies(0, 0):
            c.start()
        for b in range(ROW_LOOKAHEAD):
            @pl.when(b < n_used)
            def _():
                x_copy(b, b % ROW_RING).start()

    @pl.when(e + 1 < N_EXPERTS)
    def _():
        for c in w_copies(e + 1, 1 - wslot):
            c.start()

    for c in w_copies(e, wslot):
        c.wait()

    step = 128

    def cast(r, c):
        rows = pl.ds(pl.multiple_of(r * step, step), step)
        wu_s[rows, :] = wu_f[wslot, rows, :].astype(BF16)
        wd_s[rows, :] = wd_f[wslot, rows, :].astype(BF16)
        return c
    lax.fori_loop(0, D_MODEL // step, cast, 0)

    def arrive(b):
        sl = b % ROW_RING
        x_copy(b, sl).wait()

        @pl.when(b + ROW_LOOKAHEAD < n_used)
        def _():
            x_copy(b + ROW_LOOKAHEAD, (b + ROW_LOOKAHEAD) % ROW_RING).start()

        @pl.when(b >= ROW_RING)
        def _():
            y_copy(b, sl).wait()
        return sl

    def up(sl):
        return _dot(xbuf[sl].astype(BF16), wu_s[...]) + bu_ref[...]

    def gated(hcat):
        glu = jnp.minimum(hcat[:, 0:D_FF], SWIGLU_LIMIT)
        lin = jnp.clip(hcat[:, D_FF:2 * D_FF], -SWIGLU_LIMIT, SWIGLU_LIMIT)
        return (glu * _sigmoid(SWIGLU_ALPHA * glu) * (lin + 1.0)).astype(BF16)

    def down(act, sl):
        ybuf[sl] = _dot(act, wd_s[...]) + bd_ref[...]

    def two_blocks(j, c):
        b0 = first + 2 * j
        slots = [arrive(b0), arrive(b0 + 1)]
        hidden = [up(sl) for sl in slots]
        acts = [gated(h) for h in hidden]
        for a, sl in zip(acts, slots):
            down(a, sl)
        y_copy(b0, slots[0]).start()
        y_copy(b0 + 1, slots[1]).start()
        return c
    lax.fori_loop(0, nblk // 2, two_blocks, 0)

    @pl.when(nblk % 2 == 1)
    def _():
        b = first + nblk - 1
        sl = arrive(b)
        down(gated(up(sl)), sl)
        y_copy(b, sl).start()

    @pl.when(e == N_EXPERTS - 1)
    def _():
        for k in range(ROW_RING):
            @pl.when(k < n_used)
            def _():
                y_copy(0, (n_used - 1 - k) % ROW_RING).wait()


def _experts(tab, xb, w_up, b_up, w_down, b_down):
    w_map = lambda e, tab: (e, 0, 0)
    grid_spec = pltpu.PrefetchScalarGridSpec(
        num_scalar_prefetch=1,
        grid=(N_EXPERTS,),
        in_specs=[pl.BlockSpec(memory_space=pl.ANY),
                  pl.BlockSpec(memory_space=pl.ANY),
                  pl.BlockSpec((None, 1, 2 * D_FF), w_map),
                  pl.BlockSpec(memory_space=pl.ANY),
                  pl.BlockSpec((None, 1, D_MODEL), w_map)],
        out_specs=pl.BlockSpec(memory_space=pl.ANY),
        scratch_shapes=[pltpu.VMEM((2, D_MODEL, 2 * D_FF), F32), pltpu.VMEM((2, D_FF, D_MODEL), F32),
                        pltpu.VMEM((D_MODEL, 2 * D_FF), BF16), pltpu.VMEM((D_FF, D_MODEL), BF16),
                        pltpu.VMEM((ROW_RING, ROW_BLOCK, D_MODEL), F32),
                        pltpu.VMEM((ROW_RING, ROW_BLOCK, D_MODEL), F32),
                        pltpu.SemaphoreType.DMA((2, 2)),
                        pltpu.SemaphoreType.DMA((ROW_RING,)), pltpu.SemaphoreType.DMA((ROW_RING,))],
    )
    return pl.pallas_call(
        _expert_kernel,
        grid_spec=grid_spec,
        out_shape=jax.ShapeDtypeStruct(xb.shape, F32),
        input_output_aliases={1: 0},
        compiler_params=_cparams(("arbitrary",)),
        name="experts",
    )(tab, xb, w_up, b_up, w_down, b_down)


def _split_bf16(a):
    hi = a.astype(BF16)
    return hi, (a - hi.astype(F32)).astype(BF16)


def _combine_kernel(tab_ref, nxt_ref, slot_ref, gate_ref, h2_ref, gfin_ref, yb_ref, y_ref, buf, sems,
                    *, tm):
    i = pl.program_id(0)
    n = pl.num_programs(0)
    cur = i % 2
    nrows = buf.shape[1]

    def rows_copy(src_row, sl, dst_row, nrows):
        return pltpu.make_async_copy(
            yb_ref.at[pl.ds(pl.multiple_of(src_row, SUBLANES), nrows), :],
            buf.at[sl, pl.ds(pl.multiple_of(dst_row, SUBLANES), nrows), :], sems.at[sl])

    def gather(t_ref, sl):
        def fetch(count, yb_at, tile_at, nrows):
            def body(h, c):
                for u in range(COPIES_PER_TRIP):
                    g = COPIES_PER_TRIP * h + u

                    @pl.when(g < count)
                    def _():
                        rows_copy(t_ref[yb_at + g], sl, t_ref[tile_at + g], nrows).start(priority=u % 2)
                return c
            lax.fori_loop(0, (count + COPIES_PER_TRIP - 1) // COPIES_PER_TRIP, body, 0)
        fetch(t_ref[TAB_COUNTS], TAB_BIG_DST, TAB_BIG_SRC, BIG_ROWS)
        fetch(t_ref[TAB_COUNTS + 1], TAB_SMALL_DST, TAB_SMALL_SRC, SUBLANES)

    @pl.when(i == 0)
    def _():
        buf[...] = jnp.zeros_like(buf)
        gather(tab_ref, 0)

    @pl.when(i + 1 < n)
    def _():
        gather(nxt_ref, 1 - cur)

    def drain(count, nrows):
        def body(g, c):
            rows_copy(0, cur, 0, nrows).wait()
            return c
        lax.fori_loop(0, count, body, 0)
    drain(tab_ref[TAB_COUNTS], BIG_ROWS)
    drain(tab_ref[TAB_COUNTS + 1], SUBLANES)

    s_iota = lax.broadcasted_iota(I32, (tm, nrows), 1)
    wgt = jnp.zeros((tm, nrows), F32)
    for kk in range(TOP_K):
        wgt = wgt + jnp.where(s_iota == slot_ref[:, kk:kk + 1], gate_ref[:, kk:kk + 1], 0.0)
    w_hi, w_lo = _split_bf16(wgt)
    rows = buf[cur].astype(BF16)
    acc = _dot(w_hi, rows) + _dot(w_lo, rows)
    y_ref[...] = _rms(h2_ref[...] + acc, gfin_ref[...])


def _combine(tab, slot_col, gate_col, h2, gfin, yb, *, tm):
    ntok = h2.shape[0]
    n = ntok // tm
    smem = lambda imap: pl.BlockSpec((TABLE_WIDTH,), imap, memory_space=pltpu.SMEM)
    return pl.pallas_call(
        functools.partial(_combine_kernel, tm=tm),
        grid=(n,),
        in_specs=[smem(lambda i: (i,)), smem(lambda i: (jnp.minimum(i + 1, n - 1),)),
                  pl.BlockSpec((tm, TOP_K), lambda i: (i, 0)),
                  pl.BlockSpec((tm, TOP_K), lambda i: (i, 0)),
                  pl.BlockSpec((tm, D_MODEL), lambda i: (i, 0)),
                  pl.BlockSpec((1, D_MODEL), lambda i: (0, 0)),
                  pl.BlockSpec(memory_space=pl.ANY)],
        out_specs=pl.BlockSpec((tm, D_MODEL), lambda i: (i, 0)),
        out_shape=jax.ShapeDtypeStruct((ntok, D_MODEL), F32),
        scratch_shapes=[pltpu.VMEM((2, _sorted_rows(tm), D_MODEL), F32),
                        pltpu.SemaphoreType.DMA((2,))],
        compiler_params=_cparams(("arbitrary",)),
        name="combine",
    )(tab, tab, slot_col, gate_col, h2, gfin, yb)


def _copy_tables(seg, seg_src, seg_dst):
    experts = jnp.arange(N_EXPERTS, dtype=I32)

    def flatten(counts, bound, src0, dst0, step):
        ends = jnp.cumsum(counts, axis=1)
        idx = jnp.arange(bound, dtype=I32)
        owner = jnp.sum(ends[:, None, :] <= idx[None, :, None], axis=-1)
        pick = lambda a: jnp.sum(jnp.where(owner[..., None] == experts, a[:, None, :], 0), axis=-1)
        off = (idx[None, :] - pick(ends - counts)) * step
        valid = idx[None, :] < ends[:, -1:]
        return (jnp.where(valid, pick(dst0) + off, 0), jnp.where(valid, pick(src0) + off, 0),
                ends[:, -1:])

    n_big = seg // BIG_ROWS
    n_small = (seg - n_big * BIG_ROWS) // SUBLANES
    b_dst, b_src, b_n = flatten(n_big, MAX_BIG, seg_src, seg_dst, BIG_ROWS)
    s_dst, s_src, s_n = flatten(n_small, MAX_SMALL, seg_src + n_big * BIG_ROWS,
                                seg_dst + n_big * BIG_ROWS, SUBLANES)
    tab = jnp.concatenate([b_dst, b_src, s_dst, s_src, b_n, s_n], axis=1)
    return jnp.pad(tab, ((0, 0), (0, TABLE_WIDTH - tab.shape[1]))).astype(I32).reshape(-1)


def _tiles(a, tile):
    bsz, kk, seq = a.shape
    return a.reshape(bsz, kk, seq // tile, tile).transpose(0, 2, 1, 3).reshape(-1, kk, tile)


def _path(x, cbuf, c0, n0, m0, mkb, mvb, wts, *, tm_in, ct, chunk, tm_post, sub, fold):
    bsz, seq, _ = x.shape
    if fold:
        x = x.reshape(1, bsz * seq, D_MODEL)
        q, k, v, og, gcol, grow, yc, nbuf = _inproj(
            x, wts["g_mix"], wts["wq"], wts["wg"], wts["wgt"], wts["wc"], wts["bg"], wts["bgt"],
            wts["cw"], cbuf, tm=bsz * seq, chunk=bsz * seq)
        q, k, v, og, gcol = (a.reshape(bsz, seq, a.shape[-1]) for a in (q, k, v, og, gcol))
        grow = grow.reshape(2 * N_HEADS, bsz, seq).transpose(1, 0, 2)[:, None]
    else:
        q, k, v, og, gcol, grow, yc, nbuf = _inproj(
            x, wts["g_mix"], wts["wq"], wts["wg"], wts["wgt"], wts["wc"], wts["bg"], wts["bgt"],
            wts["cw"], cbuf, tm=tm_in, chunk=chunk)
    ym, c1, n1, m1 = _mlstm(q, k, v, og, gcol, grow, c0, n0, m0, wts["ng"], ct=ct, chunk=chunk)
    if fold:
        ym = ym.reshape(1, bsz * seq, D_MLSTM)
    h2, xn, eid, gate, rank, cnt = _post(
        ym, yc, x, wts["wmo"], wts["g_x"], wts["wxq"], mkb, mvb, wts["wxo"], wts["g_ffn"],
        wts["wrt"], wts["br"], tm=tm_post, sub=sub)
    return dict(h2=h2, xn=xn, eid=eid, gate=gate, rank=rank, cnt=cnt[:, :, 0],
                c1=c1, n1=n1, m1=m1[..., 0], nbuf=nbuf)


def kernel(x_prompt, x_sample, state_mlstm_c, state_mlstm_n, state_mlstm_m, state_conv, cache_mem_k, cache_mem_v, mem_prompt, norm_mix_g, w_in, b_gate, mlstm_norm_g, conv_w, w_mix_out, norm_x_g, norm_mem_g, w_xq, w_xk, w_xv, w_xo, norm_ffn_g, w_router, b_router, w_up, b_up, w_down, b_down, norm_final_g):
    bp, lp, _ = x_prompt.shape
    bs, ls, _ = x_sample.shape
    l = 0
    row = lambda a: a.reshape(1, -1)

    wi = w_in[l]
    gate_cols = wi[:, 4 * D_MLSTM:4 * D_MLSTM + 2 * N_HEADS]
    wts = dict(
        g_mix=row(norm_mix_g[l]),
        wq=wi[:, 0:4 * D_MLSTM].astype(BF16),
        wg=jnp.pad(gate_cols, ((0, 0), (0, 128 - 2 * N_HEADS))).astype(BF16),
        wgt=gate_cols.T.astype(BF16),
        wc=wi[:, 4 * D_MLSTM + 2 * N_HEADS:].astype(BF16),
        bg=row(b_gate[l]), bgt=b_gate[l].reshape(-1, 1),
        cw=conv_w[l], ng=row(mlstm_norm_g[l]),
        wmo=w_mix_out[l].astype(BF16), g_x=row(norm_x_g[l]), wxq=w_xq[l].astype(BF16),
        wxo=w_xo[l].astype(BF16), g_ffn=row(norm_ffn_g[l]),
        wrt=w_router[l].T.astype(BF16), br=b_router[l].reshape(-1, 1),
    )

    mk, mv, mkb, mvb = _memkv(mem_prompt.reshape(bp * N_MEM, D_MODEL), row(norm_mem_g[l]),
                              w_xk[l].astype(BF16), w_xv[l].astype(BF16))
    zeros = lambda *s: jnp.zeros(s, F32)
    tm_post, tmd = 512, 256
    assert _sorted_rows(tmd) // BIG_ROWS <= MAX_BIG
    pr = _path(x_prompt, zeros(bp, CONV_WIDTH - 1, D_CONV), zeros(bp, N_HEADS, HEAD_DIM, HEAD_DIM),
               zeros(bp, N_HEADS, HEAD_DIM), zeros(bp, N_HEADS, 1),
               mkb.reshape(bp, N_MEM, D_MODEL), mvb.reshape(bp, N_MEM, D_MODEL), wts,
               tm_in=512, ct=512, chunk=CHUNK, tm_post=tm_post, sub=tmd, fold=False)
    sa = _path(x_sample, state_conv[l], state_mlstm_c[l], state_mlstm_n[l],
               state_mlstm_m[l][..., None],
               cache_mem_k[l].reshape(bs, N_MEM, D_MODEL).astype(BF16),
               cache_mem_v[l].reshape(bs, N_MEM, D_MODEL).astype(BF16), wts,
               tm_in=ls, ct=ls, chunk=min(CHUNK, ls), tm_post=bs * ls, sub=bs * ls, fold=True)

    n_ptok, n_stok = bp * lp, bs * ls
    cnt = jnp.concatenate([pr["cnt"], sa["cnt"]], axis=0)
    n_tiles = cnt.shape[0]
    n_blocks = -(-(TOP_K * (n_ptok + n_stok) + n_tiles * N_EXPERTS * (SUBLANES - 1)) // ROW_BLOCK) \
        + N_EXPERTS
    seg = (cnt + SUBLANES - 1) // SUBLANES * SUBLANES
    seg_src = jnp.cumsum(seg, axis=1) - seg
    tot = jnp.sum(seg, axis=0)
    blocks_e = (tot + ROW_BLOCK - 1) // ROW_BLOCK
    padded = blocks_e * ROW_BLOCK
    pstart = jnp.cumsum(padded) - padded
    bend = jnp.cumsum(blocks_e)
    n_used = bend[-1]
    exp_tab = jnp.concatenate([bend - blocks_e, blocks_e, n_used[None]]).astype(I32)
    seg_dst = pstart[None, :] + jnp.cumsum(seg, axis=0) - seg

    tab = _copy_tables(seg, seg_src, seg_dst)
    fgran = (padded - tot) // SUBLANES
    fill = jnp.concatenate([pstart + tot, fgran, jnp.sum(fgran)[None], n_used[None]])
    fill = jnp.pad(fill, (0, TABLE_WIDTH - fill.shape[0])).astype(I32)

    def lookup(table, eid):
        hit = eid[..., None] == jnp.arange(N_EXPERTS, dtype=I32)
        return jnp.sum(jnp.where(hit, table[:, None, None, :], 0), axis=-1)

    eid_pt, rank_pt = _tiles(pr["eid"], tmd), _tiles(pr["rank"], tmd)
    eid_st = sa["eid"].transpose(1, 0, 2).reshape(1, TOP_K, n_stok)
    rank_st = sa["rank"].transpose(1, 0, 2).reshape(1, TOP_K, n_stok)
    slot_p = (lookup(seg_src[:-1], eid_pt) + rank_pt).transpose(0, 2, 1).reshape(n_ptok, TOP_K)
    slot_s = (lookup(seg_src[-1:], eid_st) + rank_st).transpose(0, 2, 1).reshape(n_stok, TOP_K)

    xb = _dispatch(tab, fill, eid_pt, rank_pt, seg_src.astype(F32)[..., None],
                   pr["xn"].reshape(n_ptok, D_MODEL), eid_st[0], rank_st[0],
                   sa["xn"].reshape(n_stok, D_MODEL), tmd=tmd, n_blocks=n_blocks)
    yb = _experts(exp_tab, xb, w_up[l], b_up[l][:, None, :], w_down[l], b_down[l][:, None, :])

    gfin = row(norm_final_g)
    split = (n_tiles - 1) * TABLE_WIDTH
    y_p = _combine(tab[:split], slot_p.astype(I32), pr["gate"].transpose(0, 2, 1).reshape(n_ptok, TOP_K),
                   pr["h2"].reshape(n_ptok, D_MODEL), gfin, yb, tm=tmd)
    y_s = _combine(tab[split:], slot_s.astype(I32), sa["gate"].transpose(0, 2, 1).reshape(n_stok, TOP_K),
                   sa["h2"].reshape(n_stok, D_MODEL), gfin, yb, tm=n_stok)

    lead = lambda a: a[None]
    return (y_p.reshape(bp, lp, D_MODEL), y_s.reshape(bs, ls, D_MODEL),
            lead(pr["c1"]), lead(pr["n1"]), lead(pr["m1"]), lead(pr["nbuf"]),
            lead(mk.reshape(bp, N_MEM, N_XHEADS, XHEAD_DIM)),
            lead(mv.reshape(bp, N_MEM, N_XHEADS, XHEAD_DIM)),
            lead(sa["c1"]), lead(sa["n1"]), lead(sa["m1"]), lead(sa["nbuf"]))
```

```python
import functools

import jax
import jax.numpy as jnp
from jax import lax
from jax.experimental import pallas as pl
from jax.experimental.pallas import tpu as pltpu

F32 = jnp.float32
BF16 = jnp.bfloat16
I32 = jnp.int32

D_MODEL = 1024
N_HEADS = 4
HEAD_DIM = 128
D_MLSTM = N_HEADS * HEAD_DIM
D_CONV = D_MODEL - D_MLSTM
CONV_WIDTH = 3
CHUNK = 64
N_MEM = 256
N_XHEADS = 4
XHEAD_DIM = D_MODEL // N_XHEADS
N_EXPERTS = 32
TOP_K = 4
D_FF = D_MODEL
SWIGLU_LIMIT = 7.0
SWIGLU_ALPHA = 1.702
EPS = 1e-5

SUBLANES = 8
BIG_ROWS = 4 * SUBLANES
COPIES_PER_TRIP = 4
MAX_BIG, MAX_SMALL = 64, 3 * N_EXPERTS
TAB_BIG_DST, TAB_BIG_SRC = 0, MAX_BIG
TAB_SMALL_DST, TAB_SMALL_SRC = 2 * MAX_BIG, 2 * MAX_BIG + MAX_SMALL
TAB_COUNTS = 2 * MAX_BIG + 2 * MAX_SMALL
TABLE_WIDTH = 512
POST_GROUP_ROWS = 512
ROW_BLOCK = 256
ROW_LOOKAHEAD = 3
ROW_RING = ROW_LOOKAHEAD + 2
VMEM_LIMIT = 56 * 1024 * 1024


def _cparams(sem):
    return pltpu.CompilerParams(dimension_semantics=sem, vmem_limit_bytes=VMEM_LIMIT)


def _rms(x, g):
    return x * lax.rsqrt(jnp.mean(x * x, axis=-1, keepdims=True) + EPS) * g


def _log_sigmoid(x):
    return -(jnp.maximum(-x, 0.0) + jnp.log1p(jnp.exp(-jnp.abs(x))))


def _sigmoid(x):
    return 1.0 / (1.0 + jnp.exp(-x))


def _dot(a, b):
    return jnp.dot(a, b, preferred_element_type=F32)


def _dot_nt(a, b):
    return lax.dot_general(a, b, (((1,), (1,)), ((), ())), preferred_element_type=F32)


def _dot_tn(a, b):
    return lax.dot_general(a, b, (((0,), (0,)), ((), ())), preferred_element_type=F32)


def _memkv_kernel(mem_ref, g_ref, wk_ref, wv_ref, mk_ref, mv_ref, mkb_ref, mvb_ref):
    mn = _rms(mem_ref[...], g_ref[...]).astype(BF16)
    mk = _dot(mn, wk_ref[...])
    mv = _dot(mn, wv_ref[...])
    mk_ref[...] = mk
    mv_ref[...] = mv
    mkb_ref[...] = mk.astype(BF16)
    mvb_ref[...] = mv.astype(BF16)


def _memkv(mem2d, g, wk, wv):
    rows = mem2d.shape[0]
    tm = N_MEM
    row_spec = pl.BlockSpec((tm, D_MODEL), lambda i: (i, 0))
    full = lambda shape: pl.BlockSpec(shape, lambda i: (0,) * len(shape))
    return pl.pallas_call(
        _memkv_kernel,
        grid=(rows // tm,),
        in_specs=[row_spec, full((1, D_MODEL)), full((D_MODEL, D_MODEL)), full((D_MODEL, D_MODEL))],
        out_specs=[row_spec, row_spec, row_spec, row_spec],
        out_shape=[jax.ShapeDtypeStruct((rows, D_MODEL), F32)] * 2
        + [jax.ShapeDtypeStruct((rows, D_MODEL), BF16)] * 2,
        compiler_params=_cparams(("arbitrary",)),
        name="memkv",
    )(mem2d, g, wk, wv)


def _inproj_kernel(x_ref, g_ref, wq_ref, wg_ref, wgt_ref, wc_ref, bg_ref, bgt_ref, cw_ref, cbuf_ref,
                   q_ref, k_ref, v_ref, og_ref, gcol_ref, grow_ref, yc_ref, nbuf_ref,
                   carry_ref, *, tm, chunk):
    j = pl.program_id(1)
    streams = cbuf_ref.shape[0]
    per = tm // streams

    @pl.when(j == 0)
    def _():
        carry_ref[0:2, :] = cbuf_ref[0]

    xb = _rms(x_ref[...], g_ref[...]).astype(BF16)

    p = _dot(xb, wq_ref[...])
    q_ref[...] = p[:, 0:D_MLSTM].astype(BF16)
    k_ref[...] = (p[:, D_MLSTM:2 * D_MLSTM] * (HEAD_DIM ** -0.5)).astype(BF16)
    v_ref[...] = p[:, 2 * D_MLSTM:3 * D_MLSTM].astype(BF16)
    og_ref[...] = _sigmoid(p[:, 3 * D_MLSTM:4 * D_MLSTM])

    gc = _dot(xb, wg_ref[...])[:, 0:2 * N_HEADS] + bg_ref[...]
    col = lax.broadcasted_iota(I32, gc.shape, 1)
    gcol_ref[...] = jnp.where(col < N_HEADS, gc, _log_sigmoid(gc))
    gr = _dot_nt(wgt_ref[...], xb) + bgt_ref[...]
    row = lax.broadcasted_iota(I32, gr.shape, 0)
    gr = jnp.where(row < N_HEADS, gr, _log_sigmoid(gr))
    for c in range(tm // chunk):
        grow_ref[c] = gr[:, c * chunk:(c + 1) * chunk]

    pc = _dot(xb, wc_ref[...])
    cb = pc[:, 0:D_CONV]
    u = pc[:, D_CONV:2 * D_CONV] * pc[:, 2 * D_CONV:3 * D_CONV]
    rid = lax.broadcasted_iota(I32, u.shape, 0)
    um1 = pltpu.roll(u, 1, 0)
    um2 = pltpu.roll(u, 2, 0)
    for m in range(streams):
        prev = carry_ref if streams == 1 else cbuf_ref.at[m]
        c0, c1 = prev[0:1, :], prev[1:2, :]
        um1 = jnp.where(rid == m * per, c1, um1)
        um2 = jnp.where(rid == m * per, c0, jnp.where(rid == m * per + 1, c1, um2))
        nbuf_ref[m] = u[(m + 1) * per - 2:(m + 1) * per, :]
    yc = cw_ref[0:1, :] * um2 + cw_ref[1:2, :] * um1 + cw_ref[2:3, :] * u
    yc_ref[...] = (cb * yc).astype(BF16)
    carry_ref[0:2, :] = u[tm - 2:tm, :]


def _inproj(x, g, wq, wg, wgt, wc, bg, bgt, cw, cbuf, *, tm, chunk):
    bsz, seq, _ = x.shape
    grid = (bsz, seq // tm)
    tok = lambda c: pl.BlockSpec((None, tm, c), lambda b, j: (b, j, 0))
    full = lambda shape: pl.BlockSpec(shape, lambda b, j: (0,) * len(shape))
    nck = tm // chunk
    streams = cbuf.shape[0] // bsz
    assert streams == 1 or tm == seq
    conv_state = pl.BlockSpec((streams, 2, D_CONV), lambda b, j: (b, 0, 0))
    return pl.pallas_call(
        functools.partial(_inproj_kernel, tm=tm, chunk=chunk),
        grid=grid,
        in_specs=[tok(D_MODEL), full((1, D_MODEL)), full((D_MODEL, 4 * D_MLSTM)),
                  full((D_MODEL, 128)), full((8, D_MODEL)), full((D_MODEL, 3 * D_CONV)),
                  full((1, 8)), full((8, 1)), full((CONV_WIDTH, D_CONV)), conv_state],
        out_specs=[tok(D_MLSTM), tok(D_MLSTM), tok(D_MLSTM), tok(D_MLSTM), tok(8),
                   pl.BlockSpec((None, nck, 8, chunk), lambda b, j: (b, j, 0, 0)),
                   tok(D_CONV), conv_state],
        out_shape=[jax.ShapeDtypeStruct((bsz, seq, D_MLSTM), BF16)] * 3
        + [jax.ShapeDtypeStruct((bsz, seq, D_MLSTM), F32),
           jax.ShapeDtypeStruct((bsz, seq, 8), F32),
           jax.ShapeDtypeStruct((bsz, seq // chunk, 8, chunk), F32),
           jax.ShapeDtypeStruct((bsz, seq, D_CONV), BF16),
           jax.ShapeDtypeStruct(cbuf.shape, F32)],
        scratch_shapes=[pltpu.VMEM((8, D_CONV), F32)],
        compiler_params=_cparams(("arbitrary", "arbitrary")),
        name="inproj",
    )(x, g, wq, wg, wgt, wc, bg, bgt, cw, cbuf)


def _mlstm_kernel(q_ref, k_ref, v_ref, og_ref, gc_ref, gr_ref, c0_ref, n0_ref, m0_ref, ng_ref,
                  ym_ref, c1_ref, n1_ref, m1_ref, c_s, n_s, m_s, *, chunk, nchunks, bsz):
    j = pl.program_id(0)

    @pl.when(j == 0)
    def _():
        c_s[...] = c0_ref[...]
        n_s[...] = n0_ref[...]
        m_s[...] = m0_ref[...]

    ti = lax.broadcasted_iota(I32, (chunk, chunk), 0)
    ji = lax.broadcasted_iota(I32, (chunk, chunk), 1)
    causal = ji <= ti

    def body(ci, carry):
        r0 = pl.multiple_of(ci * chunk, chunk)
        rows = pl.ds(r0, chunk)
        chains = [(b, h) for b in range(bsz) for h in range(N_HEADS)]
        cols = lambda h: slice(h * HEAD_DIM, (h + 1) * HEAD_DIM)
        each = lambda f: [f(n, b, h) for n, (b, h) in enumerate(chains)]
        q = lambda b, h: q_ref[b, rows, cols(h)]
        k = lambda b, h: k_ref[b, rows, cols(h)]
        v = lambda b, h: v_ref[b, rows, cols(h)]
        gcs = [gc_ref[b, rows, :] for b in range(bsz)]
        grs = [gr_ref[b, ci] for b in range(bsz)]
        li_c = each(lambda n, b, h: gcs[b][:, h:h + 1])
        lf_c = each(lambda n, b, h: gcs[b][:, N_HEADS + h:N_HEADS + h + 1])
        li_r = each(lambda n, b, h: grs[b][h:h + 1, :])
        lf_r = each(lambda n, b, h: grs[b][N_HEADS + h:N_HEADS + h + 1, :])
        m_prev = each(lambda n, b, h: m_s[b, h:h + 1, :])

        b_c = each(lambda n, b, h: jnp.sum(jnp.where(causal, lf_r[n], 0.0), axis=1, keepdims=True))
        b_r = each(lambda n, b, h: jnp.sum(jnp.where(ti <= ji, lf_c[n], 0.0), axis=0, keepdims=True))
        dmat = each(lambda n, b, h: jnp.where(causal, b_c[n] - b_r[n] + li_r[n], -jnp.inf))
        dmax = each(lambda n, b, h: jnp.max(dmat[n], axis=1, keepdims=True))
        inter = each(lambda n, b, h: b_c[n] + m_prev[n])
        m_t = each(lambda n, b, h: jnp.maximum(inter[n], dmax[n]))
        w_inter = each(lambda n, b, h: jnp.exp(inter[n] - m_t[n]))
        s = each(lambda n, b, h: _dot_nt(q(b, h), k(b, h)) * jnp.exp(dmat[n] - m_t[n]))
        qc = each(lambda n, b, h: _dot(q(b, h), c_s[b, h].astype(BF16)))
        sv = each(lambda n, b, h: _dot(s[n].astype(BF16), v(b, h)))
        qn = each(lambda n, b, h: jnp.sum(q(b, h).astype(F32) * n_s[b, h:h + 1, :], axis=1,
                                          keepdims=True))
        den = each(lambda n, b, h: w_inter[n] * qn[n] + jnp.sum(s[n], axis=1, keepdims=True))
        hh = each(lambda n, b, h: (w_inter[n] * qc[n] + sv[n])
                  / jnp.maximum(jnp.abs(den[n]), jnp.exp(-m_t[n])))

        m_new = each(lambda n, b, h: m_t[n][chunk - 1:chunk, :])
        b_last = each(lambda n, b, h: b_c[n][chunk - 1:chunk, :])
        decay = each(lambda n, b, h: jnp.exp(b_last[n] + m_prev[n] - m_new[n]))
        kw = each(lambda n, b, h: k(b, h).astype(F32)
                  * jnp.exp(b_last[n] - b_c[n] + li_c[n] - m_new[n]))
        kv = each(lambda n, b, h: _dot_tn(kw[n].astype(BF16), v(b, h)))
        for n, (b, h) in enumerate(chains):
            c_s[b, h] = decay[n] * c_s[b, h] + kv[n]
            n_s[b, h:h + 1, :] = decay[n] * n_s[b, h:h + 1, :] + jnp.sum(kw[n], axis=0, keepdims=True)
            m_s[b, h:h + 1, :] = m_new[n]

        hn = each(lambda n, b, h: hh[n] * lax.rsqrt(jnp.mean(hh[n] * hh[n], axis=1, keepdims=True) + EPS)
                  * ng_ref[:, cols(h)])
        for n, (b, h) in enumerate(chains):
            ym_ref[b, rows, cols(h)] = (hn[n] * og_ref[b, rows, cols(h)]).astype(BF16)
        return carry

    lax.fori_loop(0, nchunks, body, 0)

    @pl.when(j == pl.num_programs(0) - 1)
    def _():
        c1_ref[...] = c_s[...]
        n1_ref[...] = n_s[...]
        m1_ref[...] = m_s[...]


def _mlstm(q, k, v, og, gcol, grow, c0, n0, m0, ng, *, ct, chunk):
    bsz, seq, _ = q.shape
    nchunks = ct // chunk
    grid = (seq // ct,)
    tok = lambda c: pl.BlockSpec((bsz, ct, c), lambda j: (0, j, 0))
    st_c = pl.BlockSpec((bsz, N_HEADS, HEAD_DIM, HEAD_DIM), lambda j: (0, 0, 0, 0))
    st_n = pl.BlockSpec((bsz, N_HEADS, HEAD_DIM), lambda j: (0, 0, 0))
    st_m = pl.BlockSpec((bsz, N_HEADS, 1), lambda j: (0, 0, 0))
    return pl.pallas_call(
        functools.partial(_mlstm_kernel, chunk=chunk, nchunks=nchunks, bsz=bsz),
        grid=grid,
        in_specs=[tok(D_MLSTM), tok(D_MLSTM), tok(D_MLSTM), tok(D_MLSTM), tok(8),
                  pl.BlockSpec((bsz, nchunks, 8, chunk), lambda j: (0, j, 0, 0)),
                  st_c, st_n, st_m,
                  pl.BlockSpec((1, D_MLSTM), lambda j: (0, 0))],
        out_specs=[tok(D_MLSTM), st_c, st_n, st_m],
        out_shape=[jax.ShapeDtypeStruct((bsz, seq, D_MLSTM), BF16),
                   jax.ShapeDtypeStruct((bsz, N_HEADS, HEAD_DIM, HEAD_DIM), F32),
                   jax.ShapeDtypeStruct((bsz, N_HEADS, HEAD_DIM), F32),
                   jax.ShapeDtypeStruct((bsz, N_HEADS, 1), F32)],
        scratch_shapes=[pltpu.VMEM((bsz, N_HEADS, HEAD_DIM, HEAD_DIM), F32),
                        pltpu.VMEM((bsz, N_HEADS, HEAD_DIM), F32),
                        pltpu.VMEM((bsz, N_HEADS, 1), F32)],
        compiler_params=_cparams(("arbitrary",)),
        name="mlstm",
    )(q, k, v, og, gcol, grow, c0, n0, m0, ng)


def _post_kernel(ym_ref, yc_ref, x_ref, wmo_ref, gx_ref, wxq_ref, mk_ref, mv_ref, wxo_ref,
                 gf_ref, wrt_ref, br_ref,
                 h2_ref, xn_ref, eid_ref, gate_ref, rank_ref, cnt_ref, *, tm, sub):
    n_mem = mk_ref.shape[0]
    per = tm // n_mem
    n_groups = max(1, tm // POST_GROUP_ROWS) if n_mem == 1 else 1
    rows = [slice(g * (tm // n_groups), (g + 1) * (tm // n_groups)) for g in range(n_groups)]
    each = lambda f: [f(g, r) for g, r in enumerate(rows)]

    mix = each(lambda g, r: _dot(ym_ref[r, :], wmo_ref[0:D_MLSTM, :])
               + _dot(yc_ref[r, :], wmo_ref[D_MLSTM:D_MODEL, :]))
    h1 = each(lambda g, r: x_ref[r, :] + mix[g])
    xq = each(lambda g, r: _dot(_rms(h1[g], gx_ref[...]).astype(BF16), wxq_ref[...]).astype(BF16))

    def streams(g):
        if n_mem == 1:
            return [(slice(None), 0)]
        return [(slice(m * per, (m + 1) * per), m) for m in range(n_mem)]

    cols = lambda hd: slice(hd * XHEAD_DIM, (hd + 1) * XHEAD_DIM)
    units = [(g, q, m, hd) for g in range(n_groups) for q, m in streams(g) for hd in range(N_XHEADS)]
    s = [_dot_nt(xq[g][q, cols(hd)], mk_ref[m, :, cols(hd)]) * (XHEAD_DIM ** -0.5)
         for g, q, m, hd in units]
    e = [jnp.exp(v - jnp.max(v, axis=-1, keepdims=True)) for v in s]
    p = [(v / jnp.sum(v, axis=-1, keepdims=True)).astype(BF16) for v in e]
    o = [_dot(v, mv_ref[m, :, cols(hd)]).astype(BF16) for v, (g, q, m, hd) in zip(p, units)]
    per_group = len(units) // n_groups
    o_rows = [jnp.concatenate(
        [jnp.concatenate(o[g * per_group + st * N_XHEADS:g * per_group + (st + 1) * N_XHEADS], axis=1)
         for st in range(per_group // N_XHEADS)], axis=0) for g in range(n_groups)]
    h2 = each(lambda g, r: h1[g] + _dot(o_rows[g], wxo_ref[...]))
    xn2_parts = each(lambda g, r: _rms(h2[g], gf_ref[...]).astype(BF16))
    for g, r in enumerate(rows):
        h2_ref[r, :] = h2[g]
        xn_ref[r, :] = xn2_parts[g]
    xn2 = jnp.concatenate(xn2_parts, axis=0)

    logits = _dot_nt(wrt_ref[...], xn2) + br_ref[...]
    eidx = lax.broadcasted_iota(I32, logits.shape, 0).astype(F32)
    work = logits
    vals, ids, hots = [], [], []
    for _ in range(TOP_K):
        mx = jnp.max(work, axis=0, keepdims=True)
        idx = jnp.min(jnp.where(work == mx, eidx, float(N_EXPERTS)), axis=0, keepdims=True)
        sel = eidx == idx
        vals.append(mx)
        ids.append(idx)
        hots.append(sel)
        work = jnp.where(sel, -jnp.inf, work)
    exps = [jnp.exp(v - vals[0]) for v in vals]
    denom = exps[0] + exps[1] + exps[2] + exps[3]

    picked = jnp.zeros(logits.shape, F32)
    for sel in hots:
        picked = picked + sel.astype(F32)
    shift = jnp.full((tm, tm), sub.bit_length() - 1, I32)
    tj = lax.broadcasted_iota(I32, (tm, tm), 0)
    tt = lax.broadcasted_iota(I32, (tm, tm), 1)
    same = lax.shift_right_logical(tj, shift) == lax.shift_right_logical(tt, shift)
    before = jnp.where(jnp.logical_and(tj < tt, same), 1.0, 0.0).astype(BF16)
    prior = _dot(picked.astype(BF16), before)
    for kk in range(TOP_K):
        eid_ref[kk:kk + 1, :] = ids[kk].astype(I32)
        gate_ref[kk:kk + 1, :] = exps[kk] / denom
        rank_ref[kk:kk + 1, :] = jnp.sum(jnp.where(hots[kk], prior, 0.0), axis=0,
                                         keepdims=True).astype(I32)
    for s in range(tm // sub):
        cnt_ref[s] = jnp.sum(picked[:, s * sub:(s + 1) * sub], axis=1, keepdims=True).astype(I32)


def _post(ym, yc, x, wmo, gx, wxq, mkb, mvb, wxo, gf, wrt, br, *, tm, sub):
    bsz, seq, _ = x.shape
    nj = seq // tm
    grid = (bsz, nj)
    nsub = tm // sub
    n_tiles = bsz * nj * nsub
    tok = lambda c: pl.BlockSpec((None, tm, c), lambda b, j: (b, j, 0))
    full = lambda shape: pl.BlockSpec(shape, lambda b, j: (0,) * len(shape))
    n_mem = mkb.shape[0] // bsz
    mem = pl.BlockSpec((n_mem, N_MEM, D_MODEL), lambda b, j: (b, 0, 0))
    sel = pl.BlockSpec((None, TOP_K, tm), lambda b, j: (b, 0, j))
    return pl.pallas_call(
        functools.partial(_post_kernel, tm=tm, sub=sub),
        grid=grid,
        in_specs=[tok(D_MLSTM), tok(D_CONV), tok(D_MODEL), full((D_MODEL, D_MODEL)),
                  full((1, D_MODEL)), full((D_MODEL, D_MODEL)), mem, mem,
                  full((D_MODEL, D_MODEL)), full((1, D_MODEL)), full((N_EXPERTS, D_MODEL)),
                  full((N_EXPERTS, 1))],
        out_specs=[tok(D_MODEL), tok(D_MODEL), sel, sel, sel,
                   pl.BlockSpec((nsub, N_EXPERTS, 1), lambda b, j: (b * nj + j, 0, 0))],
        out_shape=[jax.ShapeDtypeStruct((bsz, seq, D_MODEL), F32),
                   jax.ShapeDtypeStruct((bsz, seq, D_MODEL), BF16),
                   jax.ShapeDtypeStruct((bsz, TOP_K, seq), I32),
                   jax.ShapeDtypeStruct((bsz, TOP_K, seq), F32),
                   jax.ShapeDtypeStruct((bsz, TOP_K, seq), I32),
                   jax.ShapeDtypeStruct((n_tiles, N_EXPERTS, 1), I32)],
        compiler_params=_cparams(("arbitrary", "arbitrary")),
        name="post",
    )(ym, yc, x, wmo, gx, wxq, mkb, mvb, wxo, gf, wrt, br)


def _sorted_rows(n_tokens):
    return -(-(TOP_K * n_tokens + N_EXPERTS * (SUBLANES - 1)) // ROW_BLOCK) * ROW_BLOCK


def _dispatch_kernel(tab_ref, fill_ref, eid_ref, rank_ref, ls_ref, x_ref, eids_ref, ranks_ref, xs_ref,
                     xb_ref, srt, zero_s, pending, sems, *, n_ptiles, n_blocks):
    i = pl.program_id(0)
    cur = i % 2

    def rows_copy(src, src_row, dst_row, nrows, sl):
        return pltpu.make_async_copy(
            src.at[pl.ds(pl.multiple_of(src_row, SUBLANES), nrows), :],
            xb_ref.at[pl.ds(pl.multiple_of(dst_row, SUBLANES), nrows), :], sems.at[sl])

    def granule(src, src_row, dst_row, sl):
        return rows_copy(src, src_row, dst_row, SUBLANES, sl)

    def drain(count, sl, nrows=SUBLANES):
        def body(g, c):
            rows_copy(zero_s, 0, 0, nrows, sl).wait()
            return c
        lax.fori_loop(0, count, body, 0)

    @pl.when(i == 0)
    def _():
        pending[0] = 0
        pending[1] = 0

    def issue_list(count, dst_at, src_at, nrows):
        def body(h, c):
            for u in range(COPIES_PER_TRIP):
                g = COPIES_PER_TRIP * h + u

                @pl.when(g < count)
                def _():
                    rows_copy(srt.at[cur], tab_ref[src_at + g], tab_ref[dst_at + g], nrows,
                              cur).start(priority=u % 2)
            return c
        lax.fori_loop(0, (count + COPIES_PER_TRIP - 1) // COPIES_PER_TRIP, body, 0)

    def sort_and_move(eid, rank, x):
        ntok = x.shape[0]
        nrows = _sorted_rows(ntok)
        e_iota = lax.broadcasted_iota(I32, (N_EXPERTS, ntok), 0)
        s_iota = lax.broadcasted_iota(I32, (nrows, ntok), 0)
        seg_start = ls_ref[...]
        hit = None
        for kk in range(TOP_K):
            start = jnp.sum(jnp.where(e_iota == eid[kk:kk + 1, :], seg_start, 0.0),
                            axis=0, keepdims=True).astype(I32)
            match = s_iota == start + rank[kk:kk + 1, :]
            hit = match if hit is None else jnp.logical_or(hit, match)
        perm = jnp.where(hit, 1.0, 0.0).astype(BF16)
        srt[cur, 0:nrows, :] = _dot(perm, x)
        drain(pending[0], 1 - cur, BIG_ROWS)
        drain(pending[1], 1 - cur)
        n_big, n_small = tab_ref[TAB_COUNTS], tab_ref[TAB_COUNTS + 1]
        issue_list(n_big, TAB_BIG_DST, TAB_BIG_SRC, BIG_ROWS)
        issue_list(n_small, TAB_SMALL_DST, TAB_SMALL_SRC, SUBLANES)
        pending[0] = n_big
        pending[1] = n_small

    @pl.when(i < n_ptiles)
    def _():
        sort_and_move(eid_ref[...], rank_ref[...], x_ref[...])

    @pl.when(i == n_ptiles)
    def _():
        sort_and_move(eids_ref[...], ranks_ref[...], xs_ref[...])
        drain(pending[0], cur, BIG_ROWS)
        drain(pending[1], cur)
        zero_s[...] = jnp.zeros_like(zero_s)
        for e in range(N_EXPERTS):
            dst = fill_ref[e]

            def zissue(g, c):
                granule(zero_s, 0, dst + g * SUBLANES, cur).start()
                return c
            lax.fori_loop(0, fill_ref[N_EXPERTS + e], zissue, 0)
        drain(fill_ref[2 * N_EXPERTS], cur)
        first_free = fill_ref[2 * N_EXPERTS + 1]

        def blk_copy(b):
            return pltpu.make_async_copy(
                zero_s, xb_ref.at[pl.ds(pl.multiple_of(b * ROW_BLOCK, ROW_BLOCK), ROW_BLOCK), :],
                sems.at[cur])

        def bissue(b, c):
            blk_copy(b).start()
            return c
        lax.fori_loop(first_free, n_blocks, bissue, 0)

        def bdrain(b, c):
            blk_copy(0).wait()
            return c
        lax.fori_loop(first_free, n_blocks, bdrain, 0)


def _dispatch(tab, fill, eid_p, rank_p, seg_start, xn_p, eid_s, rank_s, xn_s, *, tmd, n_blocks):
    n_ptiles = eid_p.shape[0]
    n_sample = xn_s.shape[0]
    last = n_ptiles - 1
    smem = lambda shape, imap: pl.BlockSpec(shape, imap, memory_space=pltpu.SMEM)
    return pl.pallas_call(
        functools.partial(_dispatch_kernel, n_ptiles=n_ptiles, n_blocks=n_blocks),
        grid=(n_ptiles + 1,),
        in_specs=[smem((TABLE_WIDTH,), lambda i: (i,)),
                  smem((TABLE_WIDTH,), lambda i: (0,)),
                  pl.BlockSpec((None, TOP_K, tmd), lambda i: (jnp.minimum(i, last), 0, 0)),
                  pl.BlockSpec((None, TOP_K, tmd), lambda i: (jnp.minimum(i, last), 0, 0)),
                  pl.BlockSpec((None, N_EXPERTS, 1), lambda i: (i, 0, 0)),
                  pl.BlockSpec((tmd, D_MODEL), lambda i: (jnp.minimum(i, last), 0)),
                  pl.BlockSpec((TOP_K, n_sample), lambda i: (0, 0)),
                  pl.BlockSpec((TOP_K, n_sample), lambda i: (0, 0)),
                  pl.BlockSpec((n_sample, D_MODEL), lambda i: (0, 0))],
        out_specs=pl.BlockSpec(memory_space=pl.ANY),
        out_shape=jax.ShapeDtypeStruct((n_blocks * ROW_BLOCK, D_MODEL), F32),
        scratch_shapes=[pltpu.VMEM((2, _sorted_rows(tmd), D_MODEL), F32),
                        pltpu.VMEM((ROW_BLOCK, D_MODEL), F32), pltpu.SMEM((2,), I32),
                        pltpu.SemaphoreType.DMA((2,))],
        compiler_params=_cparams(("arbitrary",)),
        name="dispatch",
    )(tab, fill, eid_p, rank_p, seg_start, xn_p, eid_s, rank_s, xn_s)


def _expert_kernel(tab_ref, xb_ref, wu_hbm, bu_ref, wd_hbm, bd_ref, yb_ref,
                   wu_f, wd_f, wu_s, wd_s, xbuf, ybuf, wsem, xsem, ysem):
    e = pl.program_id(0)
    first = tab_ref[e]
    nblk = tab_ref[N_EXPERTS + e]
    n_used = tab_ref[2 * N_EXPERTS]

    def hbm_rows(b):
        return pl.ds(pl.multiple_of(b * ROW_BLOCK, ROW_BLOCK), ROW_BLOCK)

    def x_copy(b, sl):
        return pltpu.make_async_copy(xb_ref.at[hbm_rows(b), :], xbuf.at[sl], xsem.at[sl])

    def y_copy(b, sl):
        return pltpu.make_async_copy(ybuf.at[sl], yb_ref.at[hbm_rows(b), :], ysem.at[sl])

    def w_copies(ex, sl):
        return (pltpu.make_async_copy(wu_hbm.at[ex], wu_f.at[sl], wsem.at[0, sl]),
                pltpu.make_async_copy(wd_hbm.at[ex], wd_f.at[sl], wsem.at[1, sl]))

    wslot = e % 2

    @pl.when(e == 0)
    def _():
        for c in w_copies(0, 0):
            c.start()
        for b in range(ROW_LOOKAHEAD):
            @pl.when(b < n_used)
            def _():
                x_copy(b, b % ROW_RING).start()

    @pl.when(e + 1 < N_EXPERTS)
    def _():
        for c in w_copies(e + 1, 1 - wslot):
            c.start()

    for c in w_copies(e, wslot):
        c.wait()

    step = 128

    def cast(r, c):
        rows = pl.ds(pl.multiple_of(r * step, step), step)
        wu_s[rows, :] = wu_f[wslot, rows, :].astype(BF16)
        wd_s[rows, :] = wd_f[wslot, rows, :].astype(BF16)
        return c
    lax.fori_loop(0, D_MODEL // step, cast, 0)

    def arrive(b):
        sl = b % ROW_RING
        x_copy(b, sl).wait()

        @pl.when(b + ROW_LOOKAHEAD < n_used)
        def _():
            x_copy(b + ROW_LOOKAHEAD, (b + ROW_LOOKAHEAD) % ROW_RING).start()

        @pl.when(b >= ROW_RING)
        def _():
            y_copy(b, sl).wait()
        return sl

    def up(sl):
        return _dot(xbuf[sl].astype(BF16), wu_s[...]) + bu_ref[...]

    def gated(hcat):
        glu = jnp.minimum(hcat[:, 0:D_FF], SWIGLU_LIMIT)
        lin = jnp.clip(hcat[:, D_FF:2 * D_FF], -SWIGLU_LIMIT, SWIGLU_LIMIT)
        return (glu * _sigmoid(SWIGLU_ALPHA * glu) * (lin + 1.0)).astype(BF16)

    def down(act, sl):
        ybuf[sl] = _dot(act, wd_s[...]) + bd_ref[...]

    def two_blocks(j, c):
        b0 = first + 2 * j
        slots = [arrive(b0), arrive(b0 + 1)]
        hidden = [up(sl) for sl in slots]
        acts = [gated(h) for h in hidden]
        for a, sl in zip(acts, slots):
            down(a, sl)
        y_copy(b0, slots[0]).start()
        y_copy(b0 + 1, slots[1]).start()
        return c
    lax.fori_loop(0, nblk // 2, two_blocks, 0)

    @pl.when(nblk % 2 == 1)
    def _():
        b = first + nblk - 1
        sl = arrive(b)
        down(gated(up(sl)), sl)
        y_copy(b, sl).start()

    @pl.when(e == N_EXPERTS - 1)
    def _():
        for k in range(ROW_RING):
            @pl.when(k < n_used)
            def _():
                y_copy(0, (n_used - 1 - k) % ROW_RING).wait()


def _experts(tab, xb, w_up, b_up, w_down, b_down):
    w_map = lambda e, tab: (e, 0, 0)
    grid_spec = pltpu.PrefetchScalarGridSpec(
        num_scalar_prefetch=1,
        grid=(N_EXPERTS,),
        in_specs=[pl.BlockSpec(memory_space=pl.ANY),
                  pl.BlockSpec(memory_space=pl.ANY),
                  pl.BlockSpec((None, 1, 2 * D_FF), w_map),
                  pl.BlockSpec(memory_space=pl.ANY),
                  pl.BlockSpec((None, 1, D_MODEL), w_map)],
        out_specs=pl.BlockSpec(memory_space=pl.ANY),
        scratch_shapes=[pltpu.VMEM((2, D_MODEL, 2 * D_FF), F32), pltpu.VMEM((2, D_FF, D_MODEL), F32),
                        pltpu.VMEM((D_MODEL, 2 * D_FF), BF16), pltpu.VMEM((D_FF, D_MODEL), BF16),
                        pltpu.VMEM((ROW_RING, ROW_BLOCK, D_MODEL), F32),
                        pltpu.VMEM((ROW_RING, ROW_BLOCK, D_MODEL), F32),
                        pltpu.SemaphoreType.DMA((2, 2)),
                        pltpu.SemaphoreType.DMA((ROW_RING,)), pltpu.SemaphoreType.DMA((ROW_RING,))],
    )
    return pl.pallas_call(
        _expert_kernel,
        grid_spec=grid_spec,
        out_shape=jax.ShapeDtypeStruct(xb.shape, F32),
        input_output_aliases={1: 0},
        compiler_params=_cparams(("arbitrary",)),
        name="experts",
    )(tab, xb, w_up, b_up, w_down, b_down)


def _split_bf16(a):
    hi = a.astype(BF16)
    return hi, (a - hi.astype(F32)).astype(BF16)


def _combine_kernel(tab_ref, nxt_ref, slot_ref, gate_ref, h2_ref, gfin_ref, yb_ref, y_ref, buf, sems,
                    *, tm):
    i = pl.program_id(0)
    n = pl.num_programs(0)
    cur = i % 2
    nrows = buf.shape[1]

    def rows_copy(src_row, sl, dst_row, nrows):
        return pltpu.make_async_copy(
            yb_ref.at[pl.ds(pl.multiple_of(src_row, SUBLANES), nrows), :],
            buf.at[sl, pl.ds(pl.multiple_of(dst_row, SUBLANES), nrows), :], sems.at[sl])

    def gather(t_ref, sl):
        def fetch(count, yb_at, tile_at, nrows):
            def body(h, c):
                for u in range(COPIES_PER_TRIP):
                    g = COPIES_PER_TRIP * h + u

                    @pl.when(g < count)
                    def _():
                        rows_copy(t_ref[yb_at + g], sl, t_ref[tile_at + g], nrows).start(priority=u % 2)
                return c
            lax.fori_loop(0, (count + COPIES_PER_TRIP - 1) // COPIES_PER_TRIP, body, 0)
        fetch(t_ref[TAB_COUNTS], TAB_BIG_DST, TAB_BIG_SRC, BIG_ROWS)
        fetch(t_ref[TAB_COUNTS + 1], TAB_SMALL_DST, TAB_SMALL_SRC, SUBLANES)

    @pl.when(i == 0)
    def _():
        buf[...] = jnp.zeros_like(buf)
        gather(tab_ref, 0)

    @pl.when(i + 1 < n)
    def _():
        gather(nxt_ref, 1 - cur)

    def drain(count, nrows):
        def body(g, c):
            rows_copy(0, cur, 0, nrows).wait()
            return c
        lax.fori_loop(0, count, body, 0)
    drain(tab_ref[TAB_COUNTS], BIG_ROWS)
    drain(tab_ref[TAB_COUNTS + 1], SUBLANES)

    s_iota = lax.broadcasted_iota(I32, (tm, nrows), 1)
    wgt = jnp.zeros((tm, nrows), F32)
    for kk in range(TOP_K):
        wgt = wgt + jnp.where(s_iota == slot_ref[:, kk:kk + 1], gate_ref[:, kk:kk + 1], 0.0)
    w_hi, w_lo = _split_bf16(wgt)
    rows = buf[cur].astype(BF16)
    acc = _dot(w_hi, rows) + _dot(w_lo, rows)
    y_ref[...] = _rms(h2_ref[...] + acc, gfin_ref[...])


def _combine(tab, slot_col, gate_col, h2, gfin, yb, *, tm):
    ntok = h2.shape[0]
    n = ntok // tm
    smem = lambda imap: pl.BlockSpec((TABLE_WIDTH,), imap, memory_space=pltpu.SMEM)
    return pl.pallas_call(
        functools.partial(_combine_kernel, tm=tm),
        grid=(n,),
        in_specs=[smem(lambda i: (i,)), smem(lambda i: (jnp.minimum(i + 1, n - 1),)),
                  pl.BlockSpec((tm, TOP_K), lambda i: (i, 0)),
                  pl.BlockSpec((tm, TOP_K), lambda i: (i, 0)),
                  pl.BlockSpec((tm, D_MODEL), lambda i: (i, 0)),
                  pl.BlockSpec((1, D_MODEL), lambda i: (0, 0)),
                  pl.BlockSpec(memory_space=pl.ANY)],
        out_specs=pl.BlockSpec((tm, D_MODEL), lambda i: (i, 0)),
        out_shape=jax.ShapeDtypeStruct((ntok, D_MODEL), F32),
        scratch_shapes=[pltpu.VMEM((2, _sorted_rows(tm), D_MODEL), F32),
                        pltpu.SemaphoreType.DMA((2,))],
        compiler_params=_cparams(("arbitrary",)),
        name="combine",
    )(tab, tab, slot_col, gate_col, h2, gfin, yb)


def _copy_tables(seg, seg_src, seg_dst):
    experts = jnp.arange(N_EXPERTS, dtype=I32)

    def flatten(counts, bound, src0, dst0, step):
        ends = jnp.cumsum(counts, axis=1)
        idx = jnp.arange(bound, dtype=I32)
        owner = jnp.sum(ends[:, None, :] <= idx[None, :, None], axis=-1)
        pick = lambda a: jnp.sum(jnp.where(owner[..., None] == experts, a[:, None, :], 0), axis=-1)
        off = (idx[None, :] - pick(ends - counts)) * step
        valid = idx[None, :] < ends[:, -1:]
        return (jnp.where(valid, pick(dst0) + off, 0), jnp.where(valid, pick(src0) + off, 0),
                ends[:, -1:])

    n_big = seg // BIG_ROWS
    n_small = (seg - n_big * BIG_ROWS) // SUBLANES
    b_dst, b_src, b_n = flatten(n_big, MAX_BIG, seg_src, seg_dst, BIG_ROWS)
    s_dst, s_src, s_n = flatten(n_small, MAX_SMALL, seg_src + n_big * BIG_ROWS,
                                seg_dst + n_big * BIG_ROWS, SUBLANES)
    tab = jnp.concatenate([b_dst, b_src, s_dst, s_src, b_n, s_n], axis=1)
    return jnp.pad(tab, ((0, 0), (0, TABLE_WIDTH - tab.shape[1]))).astype(I32).reshape(-1)


def _tiles(a, tile):
    bsz, kk, seq = a.shape
    return a.reshape(bsz, kk, seq // tile, tile).transpose(0, 2, 1, 3).reshape(-1, kk, tile)


def _path(x, cbuf, c0, n0, m0, mkb, mvb, wts, *, tm_in, ct, chunk, tm_post, sub, fold):
    bsz, seq, _ = x.shape
    if fold:
        x = x.reshape(1, bsz * seq, D_MODEL)
        q, k, v, og, gcol, grow, yc, nbuf = _inproj(
            x, wts["g_mix"], wts["wq"], wts["wg"], wts["wgt"], wts["wc"], wts["bg"], wts["bgt"],
            wts["cw"], cbuf, tm=bsz * seq, chunk=bsz * seq)
        q, k, v, og, gcol = (a.reshape(bsz, seq, a.shape[-1]) for a in (q, k, v, og, gcol))
        grow = grow.reshape(2 * N_HEADS, bsz, seq).transpose(1, 0, 2)[:, None]
    else:
        q, k, v, og, gcol, grow, yc, nbuf = _inproj(
            x, wts["g_mix"], wts["wq"], wts["wg"], wts["wgt"], wts["wc"], wts["bg"], wts["bgt"],
            wts["cw"], cbuf, tm=tm_in, chunk=chunk)
    ym, c1, n1, m1 = _mlstm(q, k, v, og, gcol, grow, c0, n0, m0, wts["ng"], ct=ct, chunk=chunk)
    if fold:
        ym = ym.reshape(1, bsz * seq, D_MLSTM)
    h2, xn, eid, gate, rank, cnt = _post(
        ym, yc, x, wts["wmo"], wts["g_x"], wts["wxq"], mkb, mvb, wts["wxo"], wts["g_ffn"],
        wts["wrt"], wts["br"], tm=tm_post, sub=sub)
    return dict(h2=h2, xn=xn, eid=eid, gate=gate, rank=rank, cnt=cnt[:, :, 0],
                c1=c1, n1=n1, m1=m1[..., 0], nbuf=nbuf)


def kernel(x_prompt, x_sample, state_mlstm_c, state_mlstm_n, state_mlstm_m, state_conv, cache_mem_k, cache_mem_v, mem_prompt, norm_mix_g, w_in, b_gate, mlstm_norm_g, conv_w, w_mix_out, norm_x_g, norm_mem_g, w_xq, w_xk, w_xv, w_xo, norm_ffn_g, w_router, b_router, w_up, b_up, w_down, b_down, norm_final_g):
    bp, lp, _ = x_prompt.shape
    bs, ls, _ = x_sample.shape
    assert w_in.shape[0] == 1, "single-layer stack only"
    l = 0
    row = lambda a: a.reshape(1, -1)

    wi = w_in[l]
    gate_cols = wi[:, 4 * D_MLSTM:4 * D_MLSTM + 2 * N_HEADS]
    wts = dict(
        g_mix=row(norm_mix_g[l]),
        wq=wi[:, 0:4 * D_MLSTM].astype(BF16),
        wg=jnp.pad(gate_cols, ((0, 0), (0, 128 - 2 * N_HEADS))).astype(BF16),
        wgt=gate_cols.T.astype(BF16),
        wc=wi[:, 4 * D_MLSTM + 2 * N_HEADS:].astype(BF16),
        bg=row(b_gate[l]), bgt=b_gate[l].reshape(-1, 1),
        cw=conv_w[l], ng=row(mlstm_norm_g[l]),
        wmo=w_mix_out[l].astype(BF16), g_x=row(norm_x_g[l]), wxq=w_xq[l].astype(BF16),
        wxo=w_xo[l].astype(BF16), g_ffn=row(norm_ffn_g[l]),
        wrt=w_router[l].T.astype(BF16), br=b_router[l].reshape(-1, 1),
    )

    mk, mv, mkb, mvb = _memkv(mem_prompt.reshape(bp * N_MEM, D_MODEL), row(norm_mem_g[l]),
                              w_xk[l].astype(BF16), w_xv[l].astype(BF16))
    zeros = lambda *s: jnp.zeros(s, F32)
    tm_post, tmd = 512, 256
    assert _sorted_rows(tmd) // BIG_ROWS <= MAX_BIG
    pr = _path(x_prompt, zeros(bp, CONV_WIDTH - 1, D_CONV), zeros(bp, N_HEADS, HEAD_DIM, HEAD_DIM),
               zeros(bp, N_HEADS, HEAD_DIM), zeros(bp, N_HEADS, 1),
               mkb.reshape(bp, N_MEM, D_MODEL), mvb.reshape(bp, N_MEM, D_MODEL), wts,
               tm_in=512, ct=512, chunk=CHUNK, tm_post=tm_post, sub=tmd, fold=False)
    sa = _path(x_sample, state_conv[l], state_mlstm_c[l], state_mlstm_n[l],
               state_mlstm_m[l][..., None],
               cache_mem_k[l].reshape(bs, N_MEM, D_MODEL).astype(BF16),
               cache_mem_v[l].reshape(bs, N_MEM, D_MODEL).astype(BF16), wts,
               tm_in=ls, ct=ls, chunk=min(CHUNK, ls), tm_post=bs * ls, sub=bs * ls, fold=True)

    n_ptok, n_stok = bp * lp, bs * ls
    cnt = jnp.concatenate([pr["cnt"], sa["cnt"]], axis=0)
    n_tiles = cnt.shape[0]
    n_blocks = -(-(TOP_K * (n_ptok + n_stok) + n_tiles * N_EXPERTS * (SUBLANES - 1)) // ROW_BLOCK) \
        + N_EXPERTS
    seg = (cnt + SUBLANES - 1) // SUBLANES * SUBLANES
    seg_src = jnp.cumsum(seg, axis=1) - seg
    tot = jnp.sum(seg, axis=0)
    blocks_e = (tot + ROW_BLOCK - 1) // ROW_BLOCK
    padded = blocks_e * ROW_BLOCK
    pstart = jnp.cumsum(padded) - padded
    bend = jnp.cumsum(blocks_e)
    n_used = bend[-1]
    exp_tab = jnp.concatenate([bend - blocks_e, blocks_e, n_used[None]]).astype(I32)
    seg_dst = pstart[None, :] + jnp.cumsum(seg, axis=0) - seg

    tab = _copy_tables(seg, seg_src, seg_dst)
    fgran = (padded - tot) // SUBLANES
    fill = jnp.concatenate([pstart + tot, fgran, jnp.sum(fgran)[None], n_used[None]])
    fill = jnp.pad(fill, (0, TABLE_WIDTH - fill.shape[0])).astype(I32)

    def lookup(table, eid):
        hit = eid[..., None] == jnp.arange(N_EXPERTS, dtype=I32)
        return jnp.sum(jnp.where(hit, table[:, None, None, :], 0), axis=-1)

    eid_pt, rank_pt = _tiles(pr["eid"], tmd), _tiles(pr["rank"], tmd)
    eid_st = sa["eid"].transpose(1, 0, 2).reshape(1, TOP_K, n_stok)
    rank_st = sa["rank"].transpose(1, 0, 2).reshape(1, TOP_K, n_stok)
    slot_p = (lookup(seg_src[:-1], eid_pt) + rank_pt).transpose(0, 2, 1).reshape(n_ptok, TOP_K)
    slot_s = (lookup(seg_src[-1:], eid_st) + rank_st).transpose(0, 2, 1).reshape(n_stok, TOP_K)

    xb = _dispatch(tab, fill, eid_pt, rank_pt, seg_src.astype(F32)[..., None],
                   pr["xn"].reshape(n_ptok, D_MODEL), eid_st[0], rank_st[0],
                   sa["xn"].reshape(n_stok, D_MODEL), tmd=tmd, n_blocks=n_blocks)
    yb = _experts(exp_tab, xb, w_up[l], b_up[l][:, None, :], w_down[l], b_down[l][:, None, :])

    gfin = row(norm_final_g)
    split = (n_tiles - 1) * TABLE_WIDTH
    y_p = _combine(tab[:split], slot_p.astype(I32), pr["gate"].transpose(0, 2, 1).reshape(n_ptok, TOP_K),
                   pr["h2"].reshape(n_ptok, D_MODEL), gfin, yb, tm=tmd)
    y_s = _combine(tab[split:], slot_s.astype(I32), sa["gate"].transpose(0, 2, 1).reshape(n_stok, TOP_K),
                   sa["h2"].reshape(n_stok, D_MODEL), gfin, yb, tm=n_stok)

    lead = lambda a: a[None]
    return (y_p.reshape(bp, lp, D_MODEL), y_s.reshape(bs, ls, D_MODEL),
            lead(pr["c1"]), lead(pr["n1"]), lead(pr["m1"]), lead(pr["nbuf"]),
            lead(mk.reshape(bp, N_MEM, N_XHEADS, XHEAD_DIM)),
            lead(mv.reshape(bp, N_MEM, N_XHEADS, XHEAD_DIM)),
            lead(sa["c1"]), lead(sa["n1"]), lead(sa["m1"]), lead(sa["nbuf"]))
```

```python
import functools

import jax
import jax.numpy as jnp
from jax import lax
from jax.experimental import pallas as pl
from jax.experimental.pallas import tpu as pltpu

F32 = jnp.float32
BF16 = jnp.bfloat16
I32 = jnp.int32

D_MODEL = 1024
N_HEADS = 4
HEAD_DIM = 128
D_MLSTM = N_HEADS * HEAD_DIM
D_CONV = D_MODEL - D_MLSTM
CONV_WIDTH = 3
CHUNK = 64
N_MEM = 256
N_XHEADS = 4
XHEAD_DIM = D_MODEL // N_XHEADS
N_EXPERTS = 32
TOP_K = 4
D_FF = D_MODEL
SWIGLU_LIMIT = 7.0
SWIGLU_ALPHA = 1.702
EPS = 1e-5

SUBLANES = 8
BIG_ROWS = 4 * SUBLANES
COPIES_PER_TRIP = 4
MAX_BIG, MAX_SMALL = 64, 3 * N_EXPERTS
TAB_BIG_DST, TAB_BIG_SRC = 0, MAX_BIG
TAB_SMALL_DST, TAB_SMALL_SRC = 2 * MAX_BIG, 2 * MAX_BIG + MAX_SMALL
TAB_COUNTS = 2 * MAX_BIG + 2 * MAX_SMALL
TABLE_WIDTH = 512
POST_GROUP_ROWS = 512
ROW_BLOCK = 256
ROW_LOOKAHEAD = 3
ROW_RING = ROW_LOOKAHEAD + 2
VMEM_LIMIT = 56 * 1024 * 1024


def _cparams(sem):
    return pltpu.CompilerParams(dimension_semantics=sem, vmem_limit_bytes=VMEM_LIMIT)


def _rms(x, g):
    return x * lax.rsqrt(jnp.mean(x * x, axis=-1, keepdims=True) + EPS) * g


def _log_sigmoid(x):
    return -(jnp.maximum(-x, 0.0) + jnp.log1p(jnp.exp(-jnp.abs(x))))


def _sigmoid(x):
    return 1.0 / (1.0 + jnp.exp(-x))


def _dot(a, b):
    return jnp.dot(a, b, preferred_element_type=F32)


def _dot_nt(a, b):
    return lax.dot_general(a, b, (((1,), (1,)), ((), ())), preferred_element_type=F32)


def _dot_tn(a, b):
    return lax.dot_general(a, b, (((0,), (0,)), ((), ())), preferred_element_type=F32)


def _memkv_kernel(mem_ref, g_ref, wk_ref, wv_ref, mk_ref, mv_ref, mkb_ref, mvb_ref):
    mn = _rms(mem_ref[...], g_ref[...]).astype(BF16)
    mk = _dot(mn, wk_ref[...])
    mv = _dot(mn, wv_ref[...])
    mk_ref[...] = mk
    mv_ref[...] = mv
    mkb_ref[...] = mk.astype(BF16)
    mvb_ref[...] = mv.astype(BF16)


def _memkv(mem2d, g, wk, wv):
    rows = mem2d.shape[0]
    tm = N_MEM
    row_spec = pl.BlockSpec((tm, D_MODEL), lambda i: (i, 0))
    full = lambda shape: pl.BlockSpec(shape, lambda i: (0,) * len(shape))
    return pl.pallas_call(
        _memkv_kernel,
        grid=(rows // tm,),
        in_specs=[row_spec, full((1, D_MODEL)), full((D_MODEL, D_MODEL)), full((D_MODEL, D_MODEL))],
        out_specs=[row_spec, row_spec, row_spec, row_spec],
        out_shape=[jax.ShapeDtypeStruct((rows, D_MODEL), F32)] * 2
        + [jax.ShapeDtypeStruct((rows, D_MODEL), BF16)] * 2,
        compiler_params=_cparams(("arbitrary",)),
        name="memkv",
    )(mem2d, g, wk, wv)


def _inproj_kernel(x_ref, g_ref, wq_ref, wg_ref, wgt_ref, wc_ref, bg_ref, bgt_ref, cw_ref, cbuf_ref,
                   q_ref, k_ref, v_ref, og_ref, gcol_ref, grow_ref, yc_ref, nbuf_ref,
                   carry_ref, *, tm, chunk):
    j = pl.program_id(1)
    streams = cbuf_ref.shape[0]
    per = tm // streams

    @pl.when(j == 0)
    def _():
        carry_ref[0:2, :] = cbuf_ref[0]

    xb = _rms(x_ref[...], g_ref[...]).astype(BF16)

    p = _dot(xb, wq_ref[...])
    q_ref[...] = p[:, 0:D_MLSTM].astype(BF16)
    k_ref[...] = (p[:, D_MLSTM:2 * D_MLSTM] * (HEAD_DIM ** -0.5)).astype(BF16)
    v_ref[...] = p[:, 2 * D_MLSTM:3 * D_MLSTM].astype(BF16)
    og_ref[...] = _sigmoid(p[:, 3 * D_MLSTM:4 * D_MLSTM])

    gc = _dot(xb, wg_ref[...])[:, 0:2 * N_HEADS] + bg_ref[...]
    col = lax.broadcasted_iota(I32, gc.shape, 1)
    gcol_ref[...] = jnp.where(col < N_HEADS, gc, _log_sigmoid(gc))
    gr = _dot_nt(wgt_ref[...], xb) + bgt_ref[...]
    row = lax.broadcasted_iota(I32, gr.shape, 0)
    gr = jnp.where(row < N_HEADS, gr, _log_sigmoid(gr))
    for c in range(tm // chunk):
        grow_ref[c] = gr[:, c * chunk:(c + 1) * chunk]

    pc = _dot(xb, wc_ref[...])
    cb = pc[:, 0:D_CONV]
    u = pc[:, D_CONV:2 * D_CONV] * pc[:, 2 * D_CONV:3 * D_CONV]
    rid = lax.broadcasted_iota(I32, u.shape, 0)
    um1 = pltpu.roll(u, 1, 0)
    um2 = pltpu.roll(u, 2, 0)
    for m in range(streams):
        prev = carry_ref if streams == 1 else cbuf_ref.at[m]
        c0, c1 = prev[0:1, :], prev[1:2, :]
        um1 = jnp.where(rid == m * per, c1, um1)
        um2 = jnp.where(rid == m * per, c0, jnp.where(rid == m * per + 1, c1, um2))
        nbuf_ref[m] = u[(m + 1) * per - 2:(m + 1) * per, :]
    yc = cw_ref[0:1, :] * um2 + cw_ref[1:2, :] * um1 + cw_ref[2:3, :] * u
    yc_ref[...] = (cb * yc).astype(BF16)
    carry_ref[0:2, :] = u[tm - 2:tm, :]


def _inproj(x, g, wq, wg, wgt, wc, bg, bgt, cw, cbuf, *, tm, chunk):
    bsz, seq, _ = x.shape
    grid = (bsz, seq // tm)
    tok = lambda c: pl.BlockSpec((None, tm, c), lambda b, j: (b, j, 0))
    full = lambda shape: pl.BlockSpec(shape, lambda b, j: (0,) * len(shape))
    nck = tm // chunk
    streams = cbuf.shape[0] // bsz
    assert streams == 1 or tm == seq
    conv_state = pl.BlockSpec((streams, 2, D_CONV), lambda b, j: (b, 0, 0))
    return pl.pallas_call(
        functools.partial(_inproj_kernel, tm=tm, chunk=chunk),
        grid=grid,
        in_specs=[tok(D_MODEL), full((1, D_MODEL)), full((D_MODEL, 4 * D_MLSTM)),
                  full((D_MODEL, 128)), full((8, D_MODEL)), full((D_MODEL, 3 * D_CONV)),
                  full((1, 8)), full((8, 1)), full((CONV_WIDTH, D_CONV)), conv_state],
        out_specs=[tok(D_MLSTM), tok(D_MLSTM), tok(D_MLSTM), tok(D_MLSTM), tok(8),
                   pl.BlockSpec((None, nck, 8, chunk), lambda b, j: (b, j, 0, 0)),
                   tok(D_CONV), conv_state],
        out_shape=[jax.ShapeDtypeStruct((bsz, seq, D_MLSTM), BF16)] * 3
        + [jax.ShapeDtypeStruct((bsz, seq, D_MLSTM), F32),
           jax.ShapeDtypeStruct((bsz, seq, 8), F32),
           jax.ShapeDtypeStruct((bsz, seq // chunk, 8, chunk), F32),
           jax.ShapeDtypeStruct((bsz, seq, D_CONV), BF16),
           jax.ShapeDtypeStruct(cbuf.shape, F32)],
        scratch_shapes=[pltpu.VMEM((8, D_CONV), F32)],
        compiler_params=_cparams(("arbitrary", "arbitrary")),
        name="inproj",
    )(x, g, wq, wg, wgt, wc, bg, bgt, cw, cbuf)


def _mlstm_kernel(q_ref, k_ref, v_ref, og_ref, gc_ref, gr_ref, c0_ref, n0_ref, m0_ref, ng_ref,
                  ym_ref, c1_ref, n1_ref, m1_ref, c_s, n_s, m_s, *, chunk, nchunks, bsz):
    j = pl.program_id(0)

    @pl.when(j == 0)
    def _():
        c_s[...] = c0_ref[...]
        n_s[...] = n0_ref[...]
        m_s[...] = m0_ref[...]

    ti = lax.broadcasted_iota(I32, (chunk, chunk), 0)
    ji = lax.broadcasted_iota(I32, (chunk, chunk), 1)
    causal = ji <= ti

    def body(ci, carry):
        r0 = pl.multiple_of(ci * chunk, chunk)
        rows = pl.ds(r0, chunk)
        chains = [(b, h) for b in range(bsz) for h in range(N_HEADS)]
        cols = lambda h: slice(h * HEAD_DIM, (h + 1) * HEAD_DIM)
        each = lambda f: [f(n, b, h) for n, (b, h) in enumerate(chains)]
        q = lambda b, h: q_ref[b, rows, cols(h)]
        k = lambda b, h: k_ref[b, rows, cols(h)]
        v = lambda b, h: v_ref[b, rows, cols(h)]
        gcs = [gc_ref[b, rows, :] for b in range(bsz)]
        grs = [gr_ref[b, ci] for b in range(bsz)]
        li_c = each(lambda n, b, h: gcs[b][:, h:h + 1])
        lf_c = each(lambda n, b, h: gcs[b][:, N_HEADS + h:N_HEADS + h + 1])
        li_r = each(lambda n, b, h: grs[b][h:h + 1, :])
        lf_r = each(lambda n, b, h: grs[b][N_HEADS + h:N_HEADS + h + 1, :])
        m_prev = each(lambda n, b, h: m_s[b, h:h + 1, :])

        b_c = each(lambda n, b, h: jnp.sum(jnp.where(causal, lf_r[n], 0.0), axis=1, keepdims=True))
        b_r = each(lambda n, b, h: jnp.sum(jnp.where(ti <= ji, lf_c[n], 0.0), axis=0, keepdims=True))
        dmat = each(lambda n, b, h: jnp.where(causal, b_c[n] - b_r[n] + li_r[n], -jnp.inf))
        dmax = each(lambda n, b, h: jnp.max(dmat[n], axis=1, keepdims=True))
        inter = each(lambda n, b, h: b_c[n] + m_prev[n])
        m_t = each(lambda n, b, h: jnp.maximum(inter[n], dmax[n]))
        w_inter = each(lambda n, b, h: jnp.exp(inter[n] - m_t[n]))
        s = each(lambda n, b, h: _dot_nt(q(b, h), k(b, h)) * jnp.exp(dmat[n] - m_t[n]))
        qc = each(lambda n, b, h: _dot(q(b, h), c_s[b, h].astype(BF16)))
        sv = each(lambda n, b, h: _dot(s[n].astype(BF16), v(b, h)))
        qn = each(lambda n, b, h: jnp.sum(q(b, h).astype(F32) * n_s[b, h:h + 1, :], axis=1,
                                          keepdims=True))
        den = each(lambda n, b, h: w_inter[n] * qn[n] + jnp.sum(s[n], axis=1, keepdims=True))
        hh = each(lambda n, b, h: (w_inter[n] * qc[n] + sv[n])
                  / jnp.maximum(jnp.abs(den[n]), jnp.exp(-m_t[n])))

        m_new = each(lambda n, b, h: m_t[n][chunk - 1:chunk, :])
        b_last = each(lambda n, b, h: b_c[n][chunk - 1:chunk, :])
        decay = each(lambda n, b, h: jnp.exp(b_last[n] + m_prev[n] - m_new[n]))
        kw = each(lambda n, b, h: k(b, h).astype(F32)
                  * jnp.exp(b_last[n] - b_c[n] + li_c[n] - m_new[n]))
        kv = each(lambda n, b, h: _dot_tn(kw[n].astype(BF16), v(b, h)))
        for n, (b, h) in enumerate(chains):
            c_s[b, h] = decay[n] * c_s[b, h] + kv[n]
            n_s[b, h:h + 1, :] = decay[n] * n_s[b, h:h + 1, :] + jnp.sum(kw[n], axis=0, keepdims=True)
            m_s[b, h:h + 1, :] = m_new[n]

        hn = each(lambda n, b, h: hh[n] * lax.rsqrt(jnp.mean(hh[n] * hh[n], axis=1, keepdims=True) + EPS)
                  * ng_ref[:, cols(h)])
        for n, (b, h) in enumerate(chains):
            ym_ref[b, rows, cols(h)] = (hn[n] * og_ref[b, rows, cols(h)]).astype(BF16)
        return carry

    lax.fori_loop(0, nchunks, body, 0)

    @pl.when(j == pl.num_programs(0) - 1)
    def _():
        c1_ref[...] = c_s[...]
        n1_ref[...] = n_s[...]
        m1_ref[...] = m_s[...]


def _mlstm(q, k, v, og, gcol, grow, c0, n0, m0, ng, *, ct, chunk):
    bsz, seq, _ = q.shape
    nchunks = ct // chunk
    grid = (seq // ct,)
    tok = lambda c: pl.BlockSpec((bsz, ct, c), lambda j: (0, j, 0))
    st_c = pl.BlockSpec((bsz, N_HEADS, HEAD_DIM, HEAD_DIM), lambda j: (0, 0, 0, 0))
    st_n = pl.BlockSpec((bsz, N_HEADS, HEAD_DIM), lambda j: (0, 0, 0))
    st_m = pl.BlockSpec((bsz, N_HEADS, 1), lambda j: (0, 0, 0))
    return pl.pallas_call(
        functools.partial(_mlstm_kernel, chunk=chunk, nchunks=nchunks, bsz=bsz),
        grid=grid,
        in_specs=[tok(D_MLSTM), tok(D_MLSTM), tok(D_MLSTM), tok(D_MLSTM), tok(8),
                  pl.BlockSpec((bsz, nchunks, 8, chunk), lambda j: (0, j, 0, 0)),
                  st_c, st_n, st_m,
                  pl.BlockSpec((1, D_MLSTM), lambda j: (0, 0))],
        out_specs=[tok(D_MLSTM), st_c, st_n, st_m],
        out_shape=[jax.ShapeDtypeStruct((bsz, seq, D_MLSTM), BF16),
                   jax.ShapeDtypeStruct((bsz, N_HEADS, HEAD_DIM, HEAD_DIM), F32),
                   jax.ShapeDtypeStruct((bsz, N_HEADS, HEAD_DIM), F32),
                   jax.ShapeDtypeStruct((bsz, N_HEADS, 1), F32)],
        scratch_shapes=[pltpu.VMEM((bsz, N_HEADS, HEAD_DIM, HEAD_DIM), F32),
                        pltpu.VMEM((bsz, N_HEADS, HEAD_DIM), F32),
                        pltpu.VMEM((bsz, N_HEADS, 1), F32)],
        compiler_params=_cparams(("arbitrary",)),
        name="mlstm",
    )(q, k, v, og, gcol, grow, c0, n0, m0, ng)


def _post_kernel(ym_ref, yc_ref, x_ref, wmo_ref, gx_ref, wxq_ref, mk_ref, mv_ref, wxo_ref,
                 gf_ref, wrt_ref, br_ref,
                 h2_ref, xn_ref, eid_ref, gate_ref, rank_ref, cnt_ref, *, tm, sub):
    n_mem = mk_ref.shape[0]
    per = tm // n_mem
    n_groups = max(1, tm // POST_GROUP_ROWS) if n_mem == 1 else 1
    rows = [slice(g * (tm // n_groups), (g + 1) * (tm // n_groups)) for g in range(n_groups)]
    each = lambda f: [f(g, r) for g, r in enumerate(rows)]

    mix = each(lambda g, r: _dot(ym_ref[r, :], wmo_ref[0:D_MLSTM, :])
               + _dot(yc_ref[r, :], wmo_ref[D_MLSTM:D_MODEL, :]))
    h1 = each(lambda g, r: x_ref[r, :] + mix[g])
    xq = each(lambda g, r: _dot(_rms(h1[g], gx_ref[...]).astype(BF16), wxq_ref[...]).astype(BF16))

    def streams(g):
        if n_mem == 1:
            return [(slice(None), 0)]
        return [(slice(m * per, (m + 1) * per), m) for m in range(n_mem)]

    cols = lambda hd: slice(hd * XHEAD_DIM, (hd + 1) * XHEAD_DIM)
    units = [(g, q, m, hd) for g in range(n_groups) for q, m in streams(g) for hd in range(N_XHEADS)]
    s = [_dot_nt(xq[g][q, cols(hd)], mk_ref[m, :, cols(hd)]) * (XHEAD_DIM ** -0.5)
         for g, q, m, hd in units]
    e = [jnp.exp(v - jnp.max(v, axis=-1, keepdims=True)) for v in s]
    p = [(v / jnp.sum(v, axis=-1, keepdims=True)).astype(BF16) for v in e]
    o = [_dot(v, mv_ref[m, :, cols(hd)]).astype(BF16) for v, (g, q, m, hd) in zip(p, units)]
    per_group = len(units) // n_groups
    o_rows = [jnp.concatenate(
        [jnp.concatenate(o[g * per_group + st * N_XHEADS:g * per_group + (st + 1) * N_XHEADS], axis=1)
         for st in range(per_group // N_XHEADS)], axis=0) for g in range(n_groups)]
    h2 = each(lambda g, r: h1[g] + _dot(o_rows[g], wxo_ref[...]))
    xn2_parts = each(lambda g, r: _rms(h2[g], gf_ref[...]).astype(BF16))
    for g, r in enumerate(rows):
        h2_ref[r, :] = h2[g]
        xn_ref[r, :] = xn2_parts[g]
    xn2 = jnp.concatenate(xn2_parts, axis=0)

    logits = _dot_nt(wrt_ref[...], xn2) + br_ref[...]
    eidx = lax.broadcasted_iota(I32, logits.shape, 0).astype(F32)
    work = logits
    vals, ids, hots = [], [], []
    for _ in range(TOP_K):
        mx = jnp.max(work, axis=0, keepdims=True)
        idx = jnp.min(jnp.where(work == mx, eidx, float(N_EXPERTS)), axis=0, keepdims=True)
        sel = eidx == idx
        vals.append(mx)
        ids.append(idx)
        hots.append(sel)
        work = jnp.where(sel, -jnp.inf, work)
    exps = [jnp.exp(v - vals[0]) for v in vals]
    denom = exps[0] + exps[1] + exps[2] + exps[3]

    picked = jnp.zeros(logits.shape, F32)
    for sel in hots:
        picked = picked + sel.astype(F32)
    shift = jnp.full((tm, tm), sub.bit_length() - 1, I32)
    tj = lax.broadcasted_iota(I32, (tm, tm), 0)
    tt = lax.broadcasted_iota(I32, (tm, tm), 1)
    same = lax.shift_right_logical(tj, shift) == lax.shift_right_logical(tt, shift)
    before = jnp.where(jnp.logical_and(tj < tt, same), 1.0, 0.0).astype(BF16)
    prior = _dot(picked.astype(BF16), before)
    for kk in range(TOP_K):
        eid_ref[kk:kk + 1, :] = ids[kk].astype(I32)
        gate_ref[kk:kk + 1, :] = exps[kk] / denom
        rank_ref[kk:kk + 1, :] = jnp.sum(jnp.where(hots[kk], prior, 0.0), axis=0,
                                         keepdims=True).astype(I32)
    for s in range(tm // sub):
        cnt_ref[s] = jnp.sum(picked[:, s * sub:(s + 1) * sub], axis=1, keepdims=True).astype(I32)


def _post(ym, yc, x, wmo, gx, wxq, mkb, mvb, wxo, gf, wrt, br, *, tm, sub):
    bsz, seq, _ = x.shape
    nj = seq // tm
    grid = (bsz, nj)
    nsub = tm // sub
    n_tiles = bsz * nj * nsub
    tok = lambda c: pl.BlockSpec((None, tm, c), lambda b, j: (b, j, 0))
    full = lambda shape: pl.BlockSpec(shape, lambda b, j: (0,) * len(shape))
    n_mem = mkb.shape[0] // bsz
    mem = pl.BlockSpec((n_mem, N_MEM, D_MODEL), lambda b, j: (b, 0, 0))
    sel = pl.BlockSpec((None, TOP_K, tm), lambda b, j: (b, 0, j))
    return pl.pallas_call(
        functools.partial(_post_kernel, tm=tm, sub=sub),
        grid=grid,
        in_specs=[tok(D_MLSTM), tok(D_CONV), tok(D_MODEL), full((D_MODEL, D_MODEL)),
                  full((1, D_MODEL)), full((D_MODEL, D_MODEL)), mem, mem,
                  full((D_MODEL, D_MODEL)), full((1, D_MODEL)), full((N_EXPERTS, D_MODEL)),
                  full((N_EXPERTS, 1))],
        out_specs=[tok(D_MODEL), tok(D_MODEL), sel, sel, sel,
                   pl.BlockSpec((nsub, N_EXPERTS, 1), lambda b, j: (b * nj + j, 0, 0))],
        out_shape=[jax.ShapeDtypeStruct((bsz, seq, D_MODEL), F32),
                   jax.ShapeDtypeStruct((bsz, seq, D_MODEL), BF16),
                   jax.ShapeDtypeStruct((bsz, TOP_K, seq), I32),
                   jax.ShapeDtypeStruct((bsz, TOP_K, seq), F32),
                   jax.ShapeDtypeStruct((bsz, TOP_K, seq), I32),
                   jax.ShapeDtypeStruct((n_tiles, N_EXPERTS, 1), I32)],
        compiler_params=_cparams(("arbitrary", "arbitrary")),
        name="post",
    )(ym, yc, x, wmo, gx, wxq, mkb, mvb, wxo, gf, wrt, br)


def _sorted_rows(n_tokens):
    return -(-(TOP_K * n_tokens + N_EXPERTS * (SUBLANES - 1)) // ROW_BLOCK) * ROW_BLOCK


def _dispatch_kernel(tab_ref, fill_ref, eid_ref, rank_ref, ls_ref, x_ref, eids_ref, ranks_ref, xs_ref,
                     xb_ref, srt, zero_s, pending, sems, *, n_ptiles, n_blocks):
    i = pl.program_id(0)
    cur = i % 2

    def rows_copy(src, src_row, dst_row, nrows, sl):
        return pltpu.make_async_copy(
            src.at[pl.ds(pl.multiple_of(src_row, SUBLANES), nrows), :],
            xb_ref.at[pl.ds(pl.multiple_of(dst_row, SUBLANES), nrows), :], sems.at[sl])

    def granule(src, src_row, dst_row, sl):
        return rows_copy(src, src_row, dst_row, SUBLANES, sl)

    def drain(count, sl, nrows=SUBLANES):
        def body(g, c):
            rows_copy(zero_s, 0, 0, nrows, sl).wait()
            return c
        lax.fori_loop(0, count, body, 0)

    @pl.when(i == 0)
    def _():
        pending[0] = 0
        pending[1] = 0

    def issue_list(count, dst_at, src_at, nrows):
        def body(h, c):
            for u in range(COPIES_PER_TRIP):
                g = COPIES_PER_TRIP * h + u

                @pl.when(g < count)
                def _():
                    rows_copy(srt.at[cur], tab_ref[src_at + g], tab_ref[dst_at + g], nrows,
                              cur).start(priority=u % 2)
            return c
        lax.fori_loop(0, (count + COPIES_PER_TRIP - 1) // COPIES_PER_TRIP, body, 0)

    def sort_and_move(eid, rank, x):
        ntok = x.shape[0]
        nrows = _sorted_rows(ntok)
        e_iota = lax.broadcasted_iota(I32, (N_EXPERTS, ntok), 0)
        s_iota = lax.broadcasted_iota(I32, (nrows, ntok), 0)
        seg_start = ls_ref[...]
        hit = None
        for kk in range(TOP_K):
            start = jnp.sum(jnp.where(e_iota == eid[kk:kk + 1, :], seg_start, 0.0),
                            axis=0, keepdims=True).astype(I32)
            match = s_iota == start + rank[kk:kk + 1, :]
            hit = match if hit is None else jnp.logical_or(hit, match)
        perm = jnp.where(hit, 1.0, 0.0).astype(BF16)
        srt[cur, 0:nrows, :] = _dot(perm, x)
        drain(pending[0], 1 - cur, BIG_ROWS)
        drain(pending[1], 1 - cur)
        n_big, n_small = tab_ref[TAB_COUNTS], tab_ref[TAB_COUNTS + 1]
        issue_list(n_big, TAB_BIG_DST, TAB_BIG_SRC, BIG_ROWS)
        issue_list(n_small, TAB_SMALL_DST, TAB_SMALL_SRC, SUBLANES)
        pending[0] = n_big
        pending[1] = n_small

    @pl.when(i < n_ptiles)
    def _():
        sort_and_move(eid_ref[...], rank_ref[...], x_ref[...])

    @pl.when(i == n_ptiles)
    def _():
        sort_and_move(eids_ref[...], ranks_ref[...], xs_ref[...])
        drain(pending[0], cur, BIG_ROWS)
        drain(pending[1], cur)
        zero_s[...] = jnp.zeros_like(zero_s)
        for e in range(N_EXPERTS):
            dst = fill_ref[e]

            def zissue(g, c):
                granule(zero_s, 0, dst + g * SUBLANES, cur).start()
                return c
            lax.fori_loop(0, fill_ref[N_EXPERTS + e], zissue, 0)
        drain(fill_ref[2 * N_EXPERTS], cur)
        first_free = fill_ref[2 * N_EXPERTS + 1]

        def blk_copy(b):
            return pltpu.make_async_copy(
                zero_s, xb_ref.at[pl.ds(pl.multiple_of(b * ROW_BLOCK, ROW_BLOCK), ROW_BLOCK), :],
                sems.at[cur])

        def bissue(b, c):
            blk_copy(b).start()
            return c
        lax.fori_loop(first_free, n_blocks, bissue, 0)

        def bdrain(b, c):
            blk_copy(0).wait()
            return c
        lax.fori_loop(first_free, n_blocks, bdrain, 0)


def _dispatch(tab, fill, eid_p, rank_p, seg_start, xn_p, eid_s, rank_s, xn_s, *, tmd, n_blocks):
    n_ptiles = eid_p.shape[0]
    n_sample = xn_s.shape[0]
    last = n_ptiles - 1
    smem = lambda shape, imap: pl.BlockSpec(shape, imap, memory_space=pltpu.SMEM)
    return pl.pallas_call(
        functools.partial(_dispatch_kernel, n_ptiles=n_ptiles, n_blocks=n_blocks),
        grid=(n_ptiles + 1,),
        in_specs=[smem((TABLE_WIDTH,), lambda i: (i,)),
                  smem((TABLE_WIDTH,), lambda i: (0,)),
                  pl.BlockSpec((None, TOP_K, tmd), lambda i: (jnp.minimum(i, last), 0, 0)),
                  pl.BlockSpec((None, TOP_K, tmd), lambda i: (jnp.minimum(i, last), 0, 0)),
                  pl.BlockSpec((None, N_EXPERTS, 1), lambda i: (i, 0, 0)),
                  pl.BlockSpec((tmd, D_MODEL), lambda i: (jnp.minimum(i, last), 0)),
                  pl.BlockSpec((TOP_K, n_sample), lambda i: (0, 0)),
                  pl.BlockSpec((TOP_K, n_sample), lambda i: (0, 0)),
                  pl.BlockSpec((n_sample, D_MODEL), lambda i: (0, 0))],
        out_specs=pl.BlockSpec(memory_space=pl.ANY),
        out_shape=jax.ShapeDtypeStruct((n_blocks * ROW_BLOCK, D_MODEL), F32),
        scratch_shapes=[pltpu.VMEM((2, _sorted_rows(tmd), D_MODEL), F32),
                        pltpu.VMEM((ROW_BLOCK, D_MODEL), F32), pltpu.SMEM((2,), I32),
                        pltpu.SemaphoreType.DMA((2,))],
        compiler_params=_cparams(("arbitrary",)),
        name="dispatch",
    )(tab, fill, eid_p, rank_p, seg_start, xn_p, eid_s, rank_s, xn_s)


def _expert_kernel(tab_ref, xb_ref, wu_hbm, bu_ref, wd_hbm, bd_ref, yb_ref,
                   wu_f, wd_f, wu_s, wd_s, xbuf, ybuf, wsem, xsem, ysem):
    e = pl.program_id(0)
    first = tab_ref[e]
    nblk = tab_ref[N_EXPERTS + e]
    n_used = tab_ref[2 * N_EXPERTS]

    def hbm_rows(b):
        return pl.ds(pl.multiple_of(b * ROW_BLOCK, ROW_BLOCK), ROW_BLOCK)

    def x_copy(b, sl):
        return pltpu.make_async_copy(xb_ref.at[hbm_rows(b), :], xbuf.at[sl], xsem.at[sl])

    def y_copy(b, sl):
        return pltpu.make_async_copy(ybuf.at[sl], yb_ref.at[hbm_rows(b), :], ysem.at[sl])

    def w_copies(ex, sl):
        return (pltpu.make_async_copy(wu_hbm.at[ex], wu_f.at[sl], wsem.at[0, sl]),
                pltpu.make_async_copy(wd_hbm.at[ex], wd_f.at[sl], wsem.at[1, sl]))

    wslot = e % 2

    @pl.when(e == 0)
    def _():
        for c in w_copies(0, 0):
            c.start()
        for b in range(ROW_LOOKAHEAD):
            @pl.when(b < n_used)
            def _():
                x_copy(b, b % ROW_RING).start()

    @pl.when(e + 1 < N_EXPERTS)
    def _():
        for c in w_copies(e + 1, 1 - wslot):
            c.start(priority=1)

    for c in w_copies(e, wslot):
        c.wait()

    step = 128

    def cast(r, c):
        rows = pl.ds(pl.multiple_of(r * step, step), step)
        wu_s[rows, :] = wu_f[wslot, rows, :].astype(BF16)
        wd_s[rows, :] = wd_f[wslot, rows, :].astype(BF16)
        return c
    lax.fori_loop(0, D_MODEL // step, cast, 0)

    def arrive(b):
        sl = b % ROW_RING
        x_copy(b, sl).wait()

        @pl.when(b + ROW_LOOKAHEAD < n_used)
        def _():
            x_copy(b + ROW_LOOKAHEAD, (b + ROW_LOOKAHEAD) % ROW_RING).start()

        @pl.when(b >= ROW_RING)
        def _():
            y_copy(b, sl).wait()
        return sl

    def up(sl):
        return _dot(xbuf[sl].astype(BF16), wu_s[...]) + bu_ref[...]

    def gated(hcat):
        glu = jnp.minimum(hcat[:, 0:D_FF], SWIGLU_LIMIT)
        lin = jnp.clip(hcat[:, D_FF:2 * D_FF], -SWIGLU_LIMIT, SWIGLU_LIMIT)
        return (glu * _sigmoid(SWIGLU_ALPHA * glu) * (lin + 1.0)).astype(BF16)

    def down(act, sl):
        ybuf[sl] = _dot(act, wd_s[...]) + bd_ref[...]

    def two_blocks(j, c):
        b0 = first + 2 * j
        slots = [arrive(b0), arrive(b0 + 1)]
        hidden = [up(sl) for sl in slots]
        acts = [gated(h) for h in hidden]
        for a, sl in zip(acts, slots):
            down(a, sl)
        y_copy(b0, slots[0]).start()
        y_copy(b0 + 1, slots[1]).start()
        return c
    lax.fori_loop(0, nblk // 2, two_blocks, 0)

    @pl.when(nblk % 2 == 1)
    def _():
        b = first + nblk - 1
        sl = arrive(b)
        down(gated(up(sl)), sl)
        y_copy(b, sl).start()

    @pl.when(e == N_EXPERTS - 1)
    def _():
        for k in range(ROW_RING):
            @pl.when(k < n_used)
            def _():
                y_copy(0, (n_used - 1 - k) % ROW_RING).wait()


def _experts(tab, xb, w_up, b_up, w_down, b_down):
    w_map = lambda e, tab: (e, 0, 0)
    grid_spec = pltpu.PrefetchScalarGridSpec(
        num_scalar_prefetch=1,
        grid=(N_EXPERTS,),
        in_specs=[pl.BlockSpec(memory_space=pl.ANY),
                  pl.BlockSpec(memory_space=pl.ANY),
                  pl.BlockSpec((None, 1, 2 * D_FF), w_map),
                  pl.BlockSpec(memory_space=pl.ANY),
                  pl.BlockSpec((None, 1, D_MODEL), w_map)],
        out_specs=pl.BlockSpec(memory_space=pl.ANY),
        scratch_shapes=[pltpu.VMEM((2, D_MODEL, 2 * D_FF), F32), pltpu.VMEM((2, D_FF, D_MODEL), F32),
                        pltpu.VMEM((D_MODEL, 2 * D_FF), BF16), pltpu.VMEM((D_FF, D_MODEL), BF16),
                        pltpu.VMEM((ROW_RING, ROW_BLOCK, D_MODEL), F32),
                        pltpu.VMEM((ROW_RING, ROW_BLOCK, D_MODEL), F32),
                        pltpu.SemaphoreType.DMA((2, 2)),
                        pltpu.SemaphoreType.DMA((ROW_RING,)), pltpu.SemaphoreType.DMA((ROW_RING,))],
    )
    return pl.pallas_call(
        _expert_kernel,
        grid_spec=grid_spec,
        out_shape=jax.ShapeDtypeStruct(xb.shape, F32),
        input_output_aliases={1: 0},
        compiler_params=_cparams(("arbitrary",)),
        name="experts",
    )(tab, xb, w_up, b_up, w_down, b_down)


def _split_bf16(a):
    hi = a.astype(BF16)
    return hi, (a - hi.astype(F32)).astype(BF16)


def _combine_kernel(tab_ref, nxt_ref, slot_ref, gate_ref, h2_ref, gfin_ref, yb_ref, y_ref, buf, sems,
                    *, tm):
    i = pl.program_id(0)
    n = pl.num_programs(0)
    cur = i % 2
    nrows = buf.shape[1]

    def rows_copy(src_row, sl, dst_row, nrows):
        return pltpu.make_async_copy(
            yb_ref.at[pl.ds(pl.multiple_of(src_row, SUBLANES), nrows), :],
            buf.at[sl, pl.ds(pl.multiple_of(dst_row, SUBLANES), nrows), :], sems.at[sl])

    def gather(t_ref, sl):
        def fetch(count, yb_at, tile_at, nrows):
            def body(h, c):
                for u in range(COPIES_PER_TRIP):
                    g = COPIES_PER_TRIP * h + u

                    @pl.when(g < count)
                    def _():
                        rows_copy(t_ref[yb_at + g], sl, t_ref[tile_at + g], nrows).start(priority=u % 2)
                return c
            lax.fori_loop(0, (count + COPIES_PER_TRIP - 1) // COPIES_PER_TRIP, body, 0)
        fetch(t_ref[TAB_COUNTS], TAB_BIG_DST, TAB_BIG_SRC, BIG_ROWS)
        fetch(t_ref[TAB_COUNTS + 1], TAB_SMALL_DST, TAB_SMALL_SRC, SUBLANES)

    @pl.when(i == 0)
    def _():
        buf[...] = jnp.zeros_like(buf)
        gather(tab_ref, 0)

    @pl.when(i + 1 < n)
    def _():
        gather(nxt_ref, 1 - cur)

    def drain(count, nrows):
        def body(g, c):
            rows_copy(0, cur, 0, nrows).wait()
            return c
        lax.fori_loop(0, count, body, 0)
    drain(tab_ref[TAB_COUNTS], BIG_ROWS)
    drain(tab_ref[TAB_COUNTS + 1], SUBLANES)

    s_iota = lax.broadcasted_iota(I32, (tm, nrows), 1)
    wgt = jnp.zeros((tm, nrows), F32)
    for kk in range(TOP_K):
        wgt = wgt + jnp.where(s_iota == slot_ref[:, kk:kk + 1], gate_ref[:, kk:kk + 1], 0.0)
    w_hi, w_lo = _split_bf16(wgt)
    rows = buf[cur].astype(BF16)
    acc = _dot(w_hi, rows) + _dot(w_lo, rows)
    y_ref[...] = _rms(h2_ref[...] + acc, gfin_ref[...])


def _combine(tab, slot_col, gate_col, h2, gfin, yb, *, tm):
    ntok = h2.shape[0]
    n = ntok // tm
    smem = lambda imap: pl.BlockSpec((TABLE_WIDTH,), imap, memory_space=pltpu.SMEM)
    return pl.pallas_call(
        functools.partial(_combine_kernel, tm=tm),
        grid=(n,),
        in_specs=[smem(lambda i: (i,)), smem(lambda i: (jnp.minimum(i + 1, n - 1),)),
                  pl.BlockSpec((tm, TOP_K), lambda i: (i, 0)),
                  pl.BlockSpec((tm, TOP_K), lambda i: (i, 0)),
                  pl.BlockSpec((tm, D_MODEL), lambda i: (i, 0)),
                  pl.BlockSpec((1, D_MODEL), lambda i: (0, 0)),
                  pl.BlockSpec(memory_space=pl.ANY)],
        out_specs=pl.BlockSpec((tm, D_MODEL), lambda i: (i, 0)),
        out_shape=jax.ShapeDtypeStruct((ntok, D_MODEL), F32),
        scratch_shapes=[pltpu.VMEM((2, _sorted_rows(tm), D_MODEL), F32),
                        pltpu.SemaphoreType.DMA((2,))],
        compiler_params=_cparams(("arbitrary",)),
        name="combine",
    )(tab, tab, slot_col, gate_col, h2, gfin, yb)


def _copy_tables(seg, seg_src, seg_dst):
    experts = jnp.arange(N_EXPERTS, dtype=I32)

    def flatten(counts, bound, src0, dst0, step):
        ends = jnp.cumsum(counts, axis=1)
        idx = jnp.arange(bound, dtype=I32)
        owner = jnp.sum(ends[:, None, :] <= idx[None, :, None], axis=-1)
        pick = lambda a: jnp.sum(jnp.where(owner[..., None] == experts, a[:, None, :], 0), axis=-1)
        off = (idx[None, :] - pick(ends - counts)) * step
        valid = idx[None, :] < ends[:, -1:]
        return (jnp.where(valid, pick(dst0) + off, 0), jnp.where(valid, pick(src0) + off, 0),
                ends[:, -1:])

    n_big = seg // BIG_ROWS
    n_small = (seg - n_big * BIG_ROWS) // SUBLANES
    b_dst, b_src, b_n = flatten(n_big, MAX_BIG, seg_src, seg_dst, BIG_ROWS)
    s_dst, s_src, s_n = flatten(n_small, MAX_SMALL, seg_src + n_big * BIG_ROWS,
                                seg_dst + n_big * BIG_ROWS, SUBLANES)
    tab = jnp.concatenate([b_dst, b_src, s_dst, s_src, b_n, s_n], axis=1)
    return jnp.pad(tab, ((0, 0), (0, TABLE_WIDTH - tab.shape[1]))).astype(I32).reshape(-1)


def _tiles(a, tile):
    bsz, kk, seq = a.shape
    return a.reshape(bsz, kk, seq // tile, tile).transpose(0, 2, 1, 3).reshape(-1, kk, tile)


def _path(x, cbuf, c0, n0, m0, mkb, mvb, wts, *, tm_in, ct, chunk, tm_post, sub, fold):
    bsz, seq, _ = x.shape
    if fold:
        x = x.reshape(1, bsz * seq, D_MODEL)
        q, k, v, og, gcol, grow, yc, nbuf = _inproj(
            x, wts["g_mix"], wts["wq"], wts["wg"], wts["wgt"], wts["wc"], wts["bg"], wts["bgt"],
            wts["cw"], cbuf, tm=bsz * seq, chunk=bsz * seq)
        q, k, v, og, gcol = (a.reshape(bsz, seq, a.shape[-1]) for a in (q, k, v, og, gcol))
        grow = grow.reshape(2 * N_HEADS, bsz, seq).transpose(1, 0, 2)[:, None]
    else:
        q, k, v, og, gcol, grow, yc, nbuf = _inproj(
            x, wts["g_mix"], wts["wq"], wts["wg"], wts["wgt"], wts["wc"], wts["bg"], wts["bgt"],
            wts["cw"], cbuf, tm=tm_in, chunk=chunk)
    ym, c1, n1, m1 = _mlstm(q, k, v, og, gcol, grow, c0, n0, m0, wts["ng"], ct=ct, chunk=chunk)
    if fold:
        ym = ym.reshape(1, bsz * seq, D_MLSTM)
    h2, xn, eid, gate, rank, cnt = _post(
        ym, yc, x, wts["wmo"], wts["g_x"], wts["wxq"], mkb, mvb, wts["wxo"], wts["g_ffn"],
        wts["wrt"], wts["br"], tm=tm_post, sub=sub)
    return dict(h2=h2, xn=xn, eid=eid, gate=gate, rank=rank, cnt=cnt[:, :, 0],
                c1=c1, n1=n1, m1=m1[..., 0], nbuf=nbuf)


def kernel(x_prompt, x_sample, state_mlstm_c, state_mlstm_n, state_mlstm_m, state_conv, cache_mem_k, cache_mem_v, mem_prompt, norm_mix_g, w_in, b_gate, mlstm_norm_g, conv_w, w_mix_out, norm_x_g, norm_mem_g, w_xq, w_xk, w_xv, w_xo, norm_ffn_g, w_router, b_router, w_up, b_up, w_down, b_down, norm_final_g):
    bp, lp, _ = x_prompt.shape
    bs, ls, _ = x_sample.shape
    assert w_in.shape[0] == 1, "single-layer stack only"
    l = 0
    row = lambda a: a.reshape(1, -1)

    wi = w_in[l]
    gate_cols = wi[:, 4 * D_MLSTM:4 * D_MLSTM + 2 * N_HEADS]
    wts = dict(
        g_mix=row(norm_mix_g[l]),
        wq=wi[:, 0:4 * D_MLSTM].astype(BF16),
        wg=jnp.pad(gate_cols, ((0, 0), (0, 128 - 2 * N_HEADS))).astype(BF16),
        wgt=gate_cols.T.astype(BF16),
        wc=wi[:, 4 * D_MLSTM + 2 * N_HEADS:].astype(BF16),
        bg=row(b_gate[l]), bgt=b_gate[l].reshape(-1, 1),
        cw=conv_w[l], ng=row(mlstm_norm_g[l]),
        wmo=w_mix_out[l].astype(BF16), g_x=row(norm_x_g[l]), wxq=w_xq[l].astype(BF16),
        wxo=w_xo[l].astype(BF16), g_ffn=row(norm_ffn_g[l]),
        wrt=w_router[l].T.astype(BF16), br=b_router[l].reshape(-1, 1),
    )

    mk, mv, mkb, mvb = _memkv(mem_prompt.reshape(bp * N_MEM, D_MODEL), row(norm_mem_g[l]),
                              w_xk[l].astype(BF16), w_xv[l].astype(BF16))
    zeros = lambda *s: jnp.zeros(s, F32)
    tm_post, tmd = 512, 256
    assert _sorted_rows(tmd) // BIG_ROWS <= MAX_BIG
    pr = _path(x_prompt, zeros(bp, CONV_WIDTH - 1, D_CONV), zeros(bp, N_HEADS, HEAD_DIM, HEAD_DIM),
               zeros(bp, N_HEADS, HEAD_DIM), zeros(bp, N_HEADS, 1),
               mkb.reshape(bp, N_MEM, D_MODEL), mvb.reshape(bp, N_MEM, D_MODEL), wts,
               tm_in=512, ct=512, chunk=CHUNK, tm_post=tm_post, sub=tmd, fold=False)
    sa = _path(x_sample, state_conv[l], state_mlstm_c[l], state_mlstm_n[l],
               state_mlstm_m[l][..., None],
               cache_mem_k[l].reshape(bs, N_MEM, D_MODEL).astype(BF16),
               cache_mem_v[l].reshape(bs, N_MEM, D_MODEL).astype(BF16), wts,
               tm_in=ls, ct=ls, chunk=min(CHUNK, ls), tm_post=bs * ls, sub=bs * ls, fold=True)

    n_ptok, n_stok = bp * lp, bs * ls
    cnt = jnp.concatenate([pr["cnt"], sa["cnt"]], axis=0)
    n_tiles = cnt.shape[0]
    n_blocks = -(-(TOP_K * (n_ptok + n_stok) + n_tiles * N_EXPERTS * (SUBLANES - 1)) // ROW_BLOCK) \
        + N_EXPERTS
    seg = (cnt + SUBLANES - 1) // SUBLANES * SUBLANES
    seg_src = jnp.cumsum(seg, axis=1) - seg
    tot = jnp.sum(seg, axis=0)
    blocks_e = (tot + ROW_BLOCK - 1) // ROW_BLOCK
    padded = blocks_e * ROW_BLOCK
    pstart = jnp.cumsum(padded) - padded
    bend = jnp.cumsum(blocks_e)
    n_used = bend[-1]
    exp_tab = jnp.concatenate([bend - blocks_e, blocks_e, n_used[None]]).astype(I32)
    seg_dst = pstart[None, :] + jnp.cumsum(seg, axis=0) - seg

    tab = _copy_tables(seg, seg_src, seg_dst)
    fgran = (padded - tot) // SUBLANES
    fill = jnp.concatenate([pstart + tot, fgran, jnp.sum(fgran)[None], n_used[None]])
    fill = jnp.pad(fill, (0, TABLE_WIDTH - fill.shape[0])).astype(I32)

    def lookup(table, eid):
        hit = eid[..., None] == jnp.arange(N_EXPERTS, dtype=I32)
        return jnp.sum(jnp.where(hit, table[:, None, None, :], 0), axis=-1)

    eid_pt, rank_pt = _tiles(pr["eid"], tmd), _tiles(pr["rank"], tmd)
    eid_st = sa["eid"].transpose(1, 0, 2).reshape(1, TOP_K, n_stok)
    rank_st = sa["rank"].transpose(1, 0, 2).reshape(1, TOP_K, n_stok)
    slot_p = (lookup(seg_src[:-1], eid_pt) + rank_pt).transpose(0, 2, 1).reshape(n_ptok, TOP_K)
    slot_s = (lookup(seg_src[-1:], eid_st) + rank_st).transpose(0, 2, 1).reshape(n_stok, TOP_K)

    xb = _dispatch(tab, fill, eid_pt, rank_pt, seg_src.astype(F32)[..., None],
                   pr["xn"].reshape(n_ptok, D_MODEL), eid_st[0], rank_st[0],
                   sa["xn"].reshape(n_stok, D_MODEL), tmd=tmd, n_blocks=n_blocks)
    yb = _experts(exp_tab, xb, w_up[l], b_up[l][:, None, :], w_down[l], b_down[l][:, None, :])

    gfin = row(norm_final_g)
    split = (n_tiles - 1) * TABLE_WIDTH
    y_p = _combine(tab[:split], slot_p.astype(I32), pr["gate"].transpose(0, 2, 1).reshape(n_ptok, TOP_K),
                   pr["h2"].reshape(n_ptok, D_MODEL), gfin, yb, tm=tmd)
    y_s = _combine(tab[split:], slot_s.astype(I32), sa["gate"].transpose(0, 2, 1).reshape(n_stok, TOP_K),
                   sa["h2"].reshape(n_stok, D_MODEL), gfin, yb, tm=n_stok)

    lead = lambda a: a[None]
    return (y_p.reshape(bp, lp, D_MODEL), y_s.reshape(bs, ls, D_MODEL),
            lead(pr["c1"]), lead(pr["n1"]), lead(pr["m1"]), lead(pr["nbuf"]),
            lead(mk.reshape(bp, N_MEM, N_XHEADS, XHEAD_DIM)),
            lead(mv.reshape(bp, N_MEM, N_XHEADS, XHEAD_DIM)),
            lead(sa["c1"]), lead(sa["n1"]), lead(sa["m1"]), lead(sa["nbuf"]))
```
